```python
import jax, jax.numpy as jnp
from jax import lax
import numpy as np

D_MODEL = 1024
BATCH = 8
SEQ = 16384
DEPTH = 2

HEAD_DIM = 64
HEADS_PER_GROUP = 4
ATTN_GROUPS = ((128, 1), (512, 4), (2048, 16))
N_GROUPS = len(ATTN_GROUPS)
ATTN_HEADS = HEADS_PER_GROUP * N_GROUPS
QKV_W = ATTN_HEADS * HEAD_DIM
ATTN_OUT_W = HEADS_PER_GROUP * HEAD_DIM
CONV_W = D_MODEL
CONV_K = 31
D_FF = 2816
ROPE_THETA = 10000.0
EPS = 1e-6
N_BRANCH = 2
N_IN = 3 * QKV_W + 2 * CONV_W + N_BRANCH * D_MODEL
SPLITS = (QKV_W, 2 * QKV_W, 3 * QKV_W, 3 * QKV_W + 2 * CONV_W)
N_MOD = 9
NEG_INF = -1e30

kernel_name = "hybrid_dilated_attn_conformer_conv_macaron"


def rms_norm(x, g):
    xf = x.astype(jnp.float32)
    y = xf * lax.rsqrt(jnp.mean(xf * xf, axis=-1, keepdims=True) + EPS)
    return (y * g.astype(jnp.float32)).astype(x.dtype)


def layer_norm(x, g, b):
    xf = x.astype(jnp.float32)
    mu = jnp.mean(xf, axis=-1, keepdims=True)
    xc = xf - mu
    var = jnp.mean(xc * xc, axis=-1, keepdims=True)
    return (xc * lax.rsqrt(var + EPS) * g.astype(jnp.float32) + b.astype(jnp.float32)).astype(x.dtype)


def rope_tables(s):
    half = HEAD_DIM // 2
    inv_freq = ROPE_THETA ** (-(jnp.arange(half, dtype=jnp.float32) * 2.0 / HEAD_DIM))
    ang = jnp.arange(s, dtype=jnp.float32)[:, None] * inv_freq[None, :]
    return jnp.cos(ang), jnp.sin(ang)


def apply_rope(t, cos, sin):
    half = HEAD_DIM // 2
    tf = t.astype(jnp.float32)
    t1, t2 = tf[..., :half], tf[..., half:]
    c = cos[None, :, None, :]
    s = sin[None, :, None, :]
    return jnp.concatenate([t1 * c - t2 * s, t2 * c + t1 * s], axis=-1).astype(t.dtype)


def dilated_window_attention(q, k, v, window, dilation):
    b, s, h, e = q.shape
    w = window // dilation
    sp = -(-s // window) * window
    pad = sp - s
    nb = sp // window
    seg = sp // dilation

    def to_blocks(t):
        t = jnp.pad(t, ((0, 0), (0, pad), (0, 0), (0, 0)))
        t = t.reshape(b, seg, dilation, h, e).transpose(0, 2, 1, 3, 4)
        return t.reshape(b, dilation, nb, w, h, e)

    def with_prev(t):
        prev = jnp.pad(t, ((0, 0), (0, 0), (1, 0), (0, 0), (0, 0), (0, 0)))[:, :, :-1]
        return jnp.concatenate([prev, t], axis=3)

    qb = to_blocks(q)
    kk = with_prev(to_blocks(k))
    vv = with_prev(to_blocks(v))

    scores = jnp.einsum('bdnqhe,bdnkhe->bdnhqk', qb, kk,
                        preferred_element_type=jnp.float32) * (e ** -0.5)
    qi = jnp.arange(w)[:, None]
    kj = jnp.arange(2 * w)[None, :]
    dist = qi + w - kj
    band = (dist >= 0) & (dist <= w)
    has_prev = (jnp.arange(nb)[:, None, None] > 0) | (kj[None] >= w)
    mask = (band[None] & has_prev)[:, None]
    scores = jnp.where(mask, scores, NEG_INF)
    m = jnp.max(scores, axis=-1, keepdims=True)
    p = jnp.exp(scores - m)
    den = jnp.sum(p, axis=-1)
    o = jnp.einsum('bdnhqk,bdnkhe->bdnqhe', p, vv.astype(jnp.float32))
    o = o / jnp.transpose(den, (0, 1, 2, 4, 3))[..., None]
    lse = m[..., 0] + jnp.log(den)

    o = o.reshape(b, dilation, seg, h, e).transpose(0, 2, 1, 3, 4).reshape(b, sp, h, e)[:, :s]
    lse = jnp.transpose(lse, (0, 1, 2, 4, 3)).reshape(b, dilation, seg, h)
    lse = lse.transpose(0, 2, 1, 3).reshape(b, sp, h)[:, :s]
    return o, lse


def conv_module(u, conv_w, conv_b, ln_g, ln_b, w_o):
    a, g = jnp.split(u, 2, axis=-1)
    h = a * jax.nn.sigmoid(g)
    h = lax.conv_general_dilated(h, conv_w[:, None, :], window_strides=(1,),
                                 padding=[(CONV_K - 1, 0)],
                                 dimension_numbers=('NWC', 'WIO', 'NWC'),
                                 feature_group_count=CONV_W) + conv_b
    h = jax.nn.silu(layer_norm(h, ln_g, ln_b))
    return h @ w_o


def swiglu(h, wg, wu, wd):
    return (jax.nn.silu(h @ wg) * (h @ wu)) @ wd


def _fwd_setup_inputs(seed: int = 0) -> dict:
    key = jax.random.key(seed)
    ks = jax.random.split(key, 20)
    f32 = jnp.float32
    nrm = lambda k, shape, scale: (jax.random.normal(k, shape, f32) * scale)
    return {
        "x": nrm(ks[0], (BATCH, SEQ, D_MODEL), 1.0),
        "c": nrm(ks[1], (BATCH, D_MODEL), 1.0),
        "ada_w": nrm(ks[2], (DEPTH, D_MODEL, N_MOD * D_MODEL), 0.5 * D_MODEL ** -0.5),
        "ada_b": nrm(ks[3], (DEPTH, N_MOD * D_MODEL), 0.02),
        "norm_g": 1.0 + nrm(ks[4], (DEPTH, 3, D_MODEL), 0.02),
        "ffn_wg": nrm(ks[5], (DEPTH, 2, D_MODEL, D_FF), D_MODEL ** -0.5),
        "ffn_wu": nrm(ks[6], (DEPTH, 2, D_MODEL, D_FF), D_MODEL ** -0.5),
        "ffn_wd": nrm(ks[7], (DEPTH, 2, D_FF, D_MODEL), D_FF ** -0.5),
        "w_in": nrm(ks[8], (DEPTH, D_MODEL, N_IN), D_MODEL ** -0.5),
        "attn_wo": nrm(ks[9], (DEPTH, ATTN_OUT_W, D_MODEL), ATTN_OUT_W ** -0.5),
        "conv_w": nrm(ks[10], (DEPTH, CONV_K, CONV_W), CONV_K ** -0.5),
        "conv_b": nrm(ks[11], (DEPTH, CONV_W), 0.02),
        "conv_ln_g": 1.0 + nrm(ks[12], (DEPTH, CONV_W), 0.02),
        "conv_ln_b": nrm(ks[13], (DEPTH, CONV_W), 0.02),
        "conv_wo": nrm(ks[14], (DEPTH, CONV_W, D_MODEL), CONV_W ** -0.5),
        "w_out": nrm(ks[15], (DEPTH, D_MODEL, D_MODEL), D_MODEL ** -0.5),
        "final_g": 1.0 + nrm(ks[16], (D_MODEL,), 0.02),
    }


def _fwd_reference(x, c, ada_w, ada_b, norm_g, ffn_wg, ffn_wu, ffn_wd, w_in, attn_wo,
              conv_w, conv_b, conv_ln_g, conv_ln_b, conv_wo, w_out, final_g):
    b, s, d = x.shape
    cos, sin = rope_tables(s)
    c_act = jax.nn.silu(c)
    for l in range(DEPTH):
        mod = (c_act @ ada_w[l] + ada_b[l]).reshape(b, N_MOD, d)[:, :, None, :]
        shift = lambda i: mod[:, 3 * i]
        scale = lambda i: mod[:, 3 * i + 1]
        gate = lambda i: mod[:, 3 * i + 2]
        modulate = lambda t, i: rms_norm(t, norm_g[l, i]) * (1.0 + scale(i)) + shift(i)

        h = modulate(x, 0)
        x = x + 0.5 * gate(0) * swiglu(h, ffn_wg[l, 0], ffn_wu[l, 0], ffn_wd[l, 0])

        h = modulate(x, 1)
        z = h @ w_in[l]
        q, k, v, u, gates = jnp.split(z, SPLITS, axis=-1)
        q = apply_rope(q.reshape(b, s, ATTN_HEADS, HEAD_DIM), cos, sin)
        k = apply_rope(k.reshape(b, s, ATTN_HEADS, HEAD_DIM), cos, sin)
        v = v.reshape(b, s, ATTN_HEADS, HEAD_DIM)
        outs, lses = [], []
        for g_idx, (win, dil) in enumerate(ATTN_GROUPS):
            sl = slice(g_idx * HEADS_PER_GROUP, (g_idx + 1) * HEADS_PER_GROUP)
            o_g, lse_g = dilated_window_attention(q[:, :, sl], k[:, :, sl], v[:, :, sl], win, dil)
            outs.append(o_g)
            lses.append(lse_g)
        wts = jax.nn.softmax(jnp.stack(lses, axis=0), axis=0)
        o = jnp.sum(wts[..., None] * jnp.stack(outs, axis=0), axis=0)
        y_attn = o.astype(x.dtype).reshape(b, s, ATTN_OUT_W) @ attn_wo[l]
        y_conv = conv_module(u, conv_w[l], conv_b[l], conv_ln_g[l], conv_ln_b[l], conv_wo[l])
        g_attn, g_conv = jnp.split(gates, N_BRANCH, axis=-1)
        y = jax.nn.sigmoid(g_attn) * y_attn + jax.nn.sigmoid(g_conv) * y_conv
        x = x + gate(1) * (y @ w_out[l])

        h = modulate(x, 2)
        x = x + 0.5 * gate(2) * swiglu(h, ffn_wg[l, 1], ffn_wu[l, 1], ffn_wd[l, 1])
    return rms_norm(x, final_g)


import jax as _jax
import jax.numpy as _jnp

TWIN_FORMAT = 'train_step'
FWD_PARAMS = ['x', 'c', 'ada_w', 'ada_b', 'norm_g', 'ffn_wg', 'ffn_wu', 'ffn_wd', 'w_in', 'attn_wo', 'conv_w', 'conv_b', 'conv_ln_g', 'conv_ln_b', 'conv_wo', 'w_out', 'final_g']
TWIN_WEIGHTS = ['ada_w', 'ada_b', 'norm_g', 'ffn_wg', 'ffn_wu', 'ffn_wd', 'w_in', 'attn_wo', 'conv_w', 'conv_b', 'conv_ln_g', 'conv_ln_b', 'conv_wo', 'w_out', 'final_g']
TWIN_DIFF_INPUT = 'x'
TWIN_INPUTS = ['x', 'c', 'ada_w', 'ada_b', 'norm_g', 'ffn_wg', 'ffn_wu', 'ffn_wd', 'w_in', 'attn_wo', 'conv_w', 'conv_b', 'conv_ln_g', 'conv_ln_b', 'conv_wo', 'w_out', 'final_g', 'loss_target', 'm_ada_w', 'm_ada_b', 'm_norm_g', 'm_ffn_wg', 'm_ffn_wu', 'm_ffn_wd', 'm_w_in', 'm_attn_wo', 'm_conv_w', 'm_conv_b', 'm_conv_ln_g', 'm_conv_ln_b', 'm_conv_wo', 'm_w_out', 'm_final_g', 'v_ada_w', 'v_ada_b', 'v_norm_g', 'v_ffn_wg', 'v_ffn_wu', 'v_ffn_wd', 'v_w_in', 'v_attn_wo', 'v_conv_w', 'v_conv_b', 'v_conv_ln_g', 'v_conv_ln_b', 'v_conv_wo', 'v_w_out', 'v_final_g']
TWIN_OUTPUTS = ['loss', 'grad_x', 'grad_ada_w', 'grad_ada_b', 'grad_norm_g', 'grad_ffn_wg', 'grad_ffn_wu', 'grad_ffn_wd', 'grad_w_in', 'grad_attn_wo', 'grad_conv_w', 'grad_conv_b', 'grad_conv_ln_g', 'grad_conv_ln_b', 'grad_conv_wo', 'grad_w_out', 'grad_final_g', 'delta_ada_w', 'delta_ada_b', 'delta_norm_g', 'delta_ffn_wg', 'delta_ffn_wu', 'delta_ffn_wd', 'delta_w_in', 'delta_attn_wo', 'delta_conv_w', 'delta_conv_b', 'delta_conv_ln_g', 'delta_conv_ln_b', 'delta_conv_wo', 'delta_w_out', 'delta_final_g', 'new_m_ada_w', 'new_m_ada_b', 'new_m_norm_g', 'new_m_ffn_wg', 'new_m_ffn_wu', 'new_m_ffn_wd', 'new_m_w_in', 'new_m_attn_wo', 'new_m_conv_w', 'new_m_conv_b', 'new_m_conv_ln_g', 'new_m_conv_ln_b', 'new_m_conv_wo', 'new_m_w_out', 'new_m_final_g', 'new_v_ada_w', 'new_v_ada_b', 'new_v_norm_g', 'new_v_ffn_wg', 'new_v_ffn_wu', 'new_v_ffn_wd', 'new_v_w_in', 'new_v_attn_wo', 'new_v_conv_w', 'new_v_conv_b', 'new_v_conv_ln_g', 'new_v_conv_ln_b', 'new_v_conv_wo', 'new_v_w_out', 'new_v_final_g']
TWIN_LEAF_KINDS = {'loss': 'loss', 'grad_x': 'grad_x', 'grad_ada_w': 'grad_w', 'grad_ada_b': 'grad_w', 'grad_norm_g': 'grad_w', 'grad_ffn_wg': 'grad_w', 'grad_ffn_wu': 'grad_w', 'grad_ffn_wd': 'grad_w', 'grad_w_in': 'grad_w', 'grad_attn_wo': 'grad_w', 'grad_conv_w': 'grad_w', 'grad_conv_b': 'grad_w', 'grad_conv_ln_g': 'grad_w', 'grad_conv_ln_b': 'grad_w', 'grad_conv_wo': 'grad_w', 'grad_w_out': 'grad_w', 'grad_final_g': 'grad_w', 'delta_ada_w': 'delta_w', 'delta_ada_b': 'delta_w', 'delta_norm_g': 'delta_w', 'delta_ffn_wg': 'delta_w', 'delta_ffn_wu': 'delta_w', 'delta_ffn_wd': 'delta_w', 'delta_w_in': 'delta_w', 'delta_attn_wo': 'delta_w', 'delta_conv_w': 'delta_w', 'delta_conv_b': 'delta_w', 'delta_conv_ln_g': 'delta_w', 'delta_conv_ln_b': 'delta_w', 'delta_conv_wo': 'delta_w', 'delta_w_out': 'delta_w', 'delta_final_g': 'delta_w', 'new_m_ada_w': 'new_m', 'new_m_ada_b': 'new_m', 'new_m_norm_g': 'new_m', 'new_m_ffn_wg': 'new_m', 'new_m_ffn_wu': 'new_m', 'new_m_ffn_wd': 'new_m', 'new_m_w_in': 'new_m', 'new_m_attn_wo': 'new_m', 'new_m_conv_w': 'new_m', 'new_m_conv_b': 'new_m', 'new_m_conv_ln_g': 'new_m', 'new_m_conv_ln_b': 'new_m', 'new_m_conv_wo': 'new_m', 'new_m_w_out': 'new_m', 'new_m_final_g': 'new_m', 'new_v_ada_w': 'new_v', 'new_v_ada_b': 'new_v', 'new_v_norm_g': 'new_v', 'new_v_ffn_wg': 'new_v', 'new_v_ffn_wu': 'new_v', 'new_v_ffn_wd': 'new_v', 'new_v_w_in': 'new_v', 'new_v_attn_wo': 'new_v', 'new_v_conv_w': 'new_v', 'new_v_conv_b': 'new_v', 'new_v_conv_ln_g': 'new_v', 'new_v_conv_ln_b': 'new_v', 'new_v_conv_wo': 'new_v', 'new_v_w_out': 'new_v', 'new_v_final_g': 'new_v'}


def _forward(args):
    return _fwd_reference(*[args[k] for k in FWD_PARAMS])


def _output_shape():
    def fwd():
        inp = _fwd_setup_inputs(0)
        return _fwd_reference(*[inp[k] for k in FWD_PARAMS])
    out = _jax.eval_shape(fwd)
    return out.shape, out.dtype

N_MICROBATCH = 1
ADAM_LR = 0.001
ADAM_B1 = 0.9
ADAM_B2 = 0.999
ADAM_EPS = 1e-08
ADAM_WD = 0.01
ADAM_STEP = 10
PER_EXAMPLE_BATCH_AXIS = {'x': 0, 'c': 0, 'loss_target': 0}
SHARED_INPUTS = []
_WEIGHT_DTYPES = {'ada_w': _jnp.float32, 'ada_b': _jnp.float32, 'norm_g': _jnp.float32, 'ffn_wg': _jnp.float32, 'ffn_wu': _jnp.float32, 'ffn_wd': _jnp.float32, 'w_in': _jnp.float32, 'attn_wo': _jnp.float32, 'conv_w': _jnp.float32, 'conv_b': _jnp.float32, 'conv_ln_g': _jnp.float32, 'conv_ln_b': _jnp.float32, 'conv_wo': _jnp.float32, 'w_out': _jnp.float32, 'final_g': _jnp.float32}
MOMENT_SCALE = {'ada_w': 5.130382e-02, 'ada_b': 8.494119e-02, 'norm_g': 4.982216e-02, 'ffn_wg': 2.355680e-02, 'ffn_wu': 2.280900e-02, 'ffn_wd': 3.781114e-02, 'w_in': 2.000565e-02, 'attn_wo': 1.846507e-02, 'conv_w': 3.673650e-02, 'conv_b': 7.113488e-02, 'conv_ln_g': 4.485575e-02, 'conv_ln_b': 3.782342e-02, 'conv_wo': 3.508521e-02, 'w_out': 3.959945e-02, 'final_g': 1.279872e+02}


def _to_microbatches(a, axis):
    t = _jnp.moveaxis(a, axis, 0)
    t = t.reshape((N_MICROBATCH, t.shape[0] // N_MICROBATCH) + t.shape[1:])
    return _jnp.moveaxis(t, 1, axis + 1)


def setup_inputs(seed: int = 0) -> dict:
    inp = _fwd_setup_inputs(seed)
    key = _jax.random.fold_in(_jax.random.key(seed), 7919)
    shape, _ = _output_shape()
    out = dict(inp)
    out["loss_target"] = _jax.random.normal(_jax.random.fold_in(key, 0), shape, _jnp.float32)
    for i, name in enumerate(TWIN_WEIGHTS):
        w = inp[name].astype(_jnp.float32)
        if MOMENT_SCALE is None:
            s = _jnp.sqrt(_jnp.mean(_jnp.square(w)) + 1e-30)
        else:
            s = MOMENT_SCALE[name]
        km, kv = _jax.random.split(_jax.random.fold_in(key, i + 1))
        out[name] = w
        out["m_" + name] = s * _jax.random.normal(km, w.shape, _jnp.float32)
        out["v_" + name] = (s * s) * _jax.random.uniform(kv, w.shape, _jnp.float32, 0.5, 1.5)
    if N_MICROBATCH > 1:
        for name, axis in PER_EXAMPLE_BATCH_AXIS.items():
            out[name] = _to_microbatches(out[name], axis)
    return {'x': out['x'], 'c': out['c'], 'ada_w': out['ada_w'], 'ada_b': out['ada_b'], 'norm_g': out['norm_g'], 'ffn_wg': out['ffn_wg'], 'ffn_wu': out['ffn_wu'], 'ffn_wd': out['ffn_wd'], 'w_in': out['w_in'], 'attn_wo': out['attn_wo'], 'conv_w': out['conv_w'], 'conv_b': out['conv_b'], 'conv_ln_g': out['conv_ln_g'], 'conv_ln_b': out['conv_ln_b'], 'conv_wo': out['conv_wo'], 'w_out': out['w_out'], 'final_g': out['final_g'], 'loss_target': out['loss_target'], 'm_ada_w': out['m_ada_w'], 'm_ada_b': out['m_ada_b'], 'm_norm_g': out['m_norm_g'], 'm_ffn_wg': out['m_ffn_wg'], 'm_ffn_wu': out['m_ffn_wu'], 'm_ffn_wd': out['m_ffn_wd'], 'm_w_in': out['m_w_in'], 'm_attn_wo': out['m_attn_wo'], 'm_conv_w': out['m_conv_w'], 'm_conv_b': out['m_conv_b'], 'm_conv_ln_g': out['m_conv_ln_g'], 'm_conv_ln_b': out['m_conv_ln_b'], 'm_conv_wo': out['m_conv_wo'], 'm_w_out': out['m_w_out'], 'm_final_g': out['m_final_g'], 'v_ada_w': out['v_ada_w'], 'v_ada_b': out['v_ada_b'], 'v_norm_g': out['v_norm_g'], 'v_ffn_wg': out['v_ffn_wg'], 'v_ffn_wu': out['v_ffn_wu'], 'v_ffn_wd': out['v_ffn_wd'], 'v_w_in': out['v_w_in'], 'v_attn_wo': out['v_attn_wo'], 'v_conv_w': out['v_conv_w'], 'v_conv_b': out['v_conv_b'], 'v_conv_ln_g': out['v_conv_ln_g'], 'v_conv_ln_b': out['v_conv_ln_b'], 'v_conv_wo': out['v_conv_wo'], 'v_w_out': out['v_w_out'], 'v_final_g': out['v_final_g']}


def _loss(weights, diff, rest, loss_target):
    with _jax.named_scope("forward"):
        args = {**rest, TWIN_DIFF_INPUT: diff, **{k: w.astype(_WEIGHT_DTYPES[k]) for k, w in weights.items()}}
        y = _forward(args)
    with _jax.named_scope("loss_head"):
        err = _jnp.square(y.astype(_jnp.float32) - loss_target)
        return 0.5 * _jnp.sum(_jnp.mean(err, axis=-1)) if err.ndim else 0.5 * err


def _adamw(w, g, m, v):
    m = ADAM_B1 * m + (1.0 - ADAM_B1) * g
    v = ADAM_B2 * v + (1.0 - ADAM_B2) * _jnp.square(g)
    m_hat = m / (1.0 - ADAM_B1 ** ADAM_STEP)
    v_hat = v / (1.0 - ADAM_B2 ** ADAM_STEP)
    delta = -ADAM_LR * (m_hat / (_jnp.sqrt(v_hat) + ADAM_EPS) + ADAM_WD * w)
    return delta, m, v


def reference(x, c, ada_w, ada_b, norm_g, ffn_wg, ffn_wu, ffn_wd, w_in, attn_wo, conv_w, conv_b, conv_ln_g, conv_ln_b, conv_wo, w_out, final_g, loss_target, m_ada_w, m_ada_b, m_norm_g, m_ffn_wg, m_ffn_wu, m_ffn_wd, m_w_in, m_attn_wo, m_conv_w, m_conv_b, m_conv_ln_g, m_conv_ln_b, m_conv_wo, m_w_out, m_final_g, v_ada_w, v_ada_b, v_norm_g, v_ffn_wg, v_ffn_wu, v_ffn_wd, v_w_in, v_attn_wo, v_conv_w, v_conv_b, v_conv_ln_g, v_conv_ln_b, v_conv_wo, v_w_out, v_final_g):
    given = dict(x=x, c=c, ada_w=ada_w, ada_b=ada_b, norm_g=norm_g, ffn_wg=ffn_wg, ffn_wu=ffn_wu, ffn_wd=ffn_wd, w_in=w_in, attn_wo=attn_wo, conv_w=conv_w, conv_b=conv_b, conv_ln_g=conv_ln_g, conv_ln_b=conv_ln_b, conv_wo=conv_wo, w_out=w_out, final_g=final_g, loss_target=loss_target, m_ada_w=m_ada_w, m_ada_b=m_ada_b, m_norm_g=m_norm_g, m_ffn_wg=m_ffn_wg, m_ffn_wu=m_ffn_wu, m_ffn_wd=m_ffn_wd, m_w_in=m_w_in, m_attn_wo=m_attn_wo, m_conv_w=m_conv_w, m_conv_b=m_conv_b, m_conv_ln_g=m_conv_ln_g, m_conv_ln_b=m_conv_ln_b, m_conv_wo=m_conv_wo, m_w_out=m_w_out, m_final_g=m_final_g, v_ada_w=v_ada_w, v_ada_b=v_ada_b, v_norm_g=v_norm_g, v_ffn_wg=v_ffn_wg, v_ffn_wu=v_ffn_wu, v_ffn_wd=v_ffn_wd, v_w_in=v_w_in, v_attn_wo=v_attn_wo, v_conv_w=v_conv_w, v_conv_b=v_conv_b, v_conv_ln_g=v_conv_ln_g, v_conv_ln_b=v_conv_ln_b, v_conv_wo=v_conv_wo, v_w_out=v_w_out, v_final_g=v_final_g)
    weights = {n: given[n] for n in TWIN_WEIGHTS}
    shared = {n: given[n] for n in SHARED_INPUTS}
    per_example = {n: given[n] for n in ['x', 'c']}
    grad_fn = _jax.value_and_grad(_loss, argnums=(0, 1))

    def one_microbatch(ex, loss_target):
        ex = dict(ex)
        diff = ex.pop(TWIN_DIFF_INPUT)
        return grad_fn(weights, diff, {**shared, **ex}, loss_target)

    if N_MICROBATCH == 1:
        loss, (grad_w, grad_x) = one_microbatch(per_example, given["loss_target"])
    else:
        def body(carry, xs):
            loss_sum, grad_sum = carry
            l_k, (gw_k, gx_k) = one_microbatch(xs[0], xs[1])
            with _jax.named_scope("update"):
                return (loss_sum + l_k, _jax.tree.map(_jnp.add, grad_sum, gw_k)), gx_k

        init = (_jnp.zeros((), _jnp.float32), _jax.tree.map(_jnp.zeros_like, weights))
        (loss, grad_w), grad_x = _jax.lax.scan(body, init, (per_example, given["loss_target"]))
    with _jax.named_scope("update"):
        delta_w, new_m, new_v = {}, {}, {}
        for n in TWIN_WEIGHTS:
            delta_w[n], new_m[n], new_v[n] = _adamw(weights[n], grad_w[n], given["m_" + n], given["v_" + n])
    return (loss, grad_x, *[grad_w[n] for n in TWIN_WEIGHTS], *[delta_w[n] for n in TWIN_WEIGHTS],
            *[new_m[n] for n in TWIN_WEIGHTS], *[new_v[n] for n in TWIN_WEIGHTS])
```

```python
import functools

import jax
import jax.numpy as jnp
from jax import lax
from jax.experimental import pallas as pl
from jax.experimental.pallas import tpu as pltpu

F32 = jnp.float32
BF16 = jnp.bfloat16

D = 1024
DFF = 2816
HEAD = 64
GW = 256
DILATIONS = (1, 4, 16)
BAND = 128
QKV = 2304
CONV_K = 31
HALO = 32
N_MOD = 9
EPS = 1e-6
NEG_INF = -1e30
DEPTH = 2

LANES = 1024
HALF_ROWS = 8736
PACK_ROWS = 2 * HALF_ROWS
SUM_ROWS = 416

ADAM_LR = 0.001
ADAM_B1 = 0.9
ADAM_B2 = 0.999
ADAM_EPS = 1e-08
ADAM_WD = 0.01
ADAM_STEP = 10

VMEM_LIMIT = 56 * 1024 * 1024

SHARDED = ("ada_w", "norm_g", "ffn_wg", "ffn_wu", "ffn_wd", "w_in", "attn_wo", "conv_w", "conv_wo", "w_out")
SHARD_AXIS = {"ada_w": 2, "norm_g": 2, "ffn_wg": 3, "ffn_wu": 3, "ffn_wd": 2, "w_in": 2, "attn_wo": 2,
              "conv_w": 2, "conv_wo": 1, "w_out": 1}
EXACT = ("norm_g", "conv_w")
REPLICATED = ("ada_b", "conv_b", "conv_ln_g", "conv_ln_b", "final_g")
WEIGHTS = ("ada_w", "ada_b", "norm_g", "ffn_wg", "ffn_wu", "ffn_wd", "w_in", "attn_wo", "conv_w", "conv_b",
           "conv_ln_g", "conv_ln_b", "conv_wo", "w_out", "final_g")

MESH = pl.DeviceIdType.MESH


def _params(sem=None):
    return pltpu.CompilerParams(dimension_semantics=sem, vmem_limit_bytes=VMEM_LIMIT)


def _sigmoid(v):
    return jax.nn.sigmoid(v)


def _mm(name, a_list, b_list, pairs, epilogue, out_dtypes, *, tm, tn, n_out, trans_b=False, extras=(), vecs=(),
        out_widths=None):
    m = a_list[0].shape[0]
    na, nb, ne, nv = len(a_list), len(b_list), len(extras), len(vecs)
    dn = (((1,), (1,)), ((), ())) if trans_b else (((1,), (0,)), ((), ()))

    def body(*refs):
        a_refs = refs[:na]
        b_refs = refs[na:na + nb]
        e_refs = refs[na + nb:na + nb + ne]
        v_refs = refs[na + nb + ne:na + nb + ne + nv]
        o_refs = refs[na + nb + ne + nv:]
        accs = [lax.dot_general(a_refs[ai][...], b_refs[bi][...], dn, preferred_element_type=F32)
                for ai, bi in pairs]
        res = epilogue(accs, [e[...] for e in e_refs], [v[...] for v in v_refs])
        for o_ref, r in zip(o_refs, res):
            o_ref[...] = r.astype(o_ref.dtype)

    in_specs = [pl.BlockSpec((tm, a.shape[1]), lambda j, i: (i, 0)) for a in a_list]
    if trans_b:
        in_specs += [pl.BlockSpec((tn, b.shape[1]), lambda j, i: (j, 0)) for b in b_list]
    else:
        in_specs += [pl.BlockSpec((b.shape[0], tn), lambda j, i: (0, j)) for b in b_list]
    in_specs += [pl.BlockSpec((tm, tn), functools.partial(lambda j, i, off: (i, j + off), off=off))
                 for _, off in extras]
    in_specs += [pl.BlockSpec((1, tn), lambda j, i: (0, j)) for _ in vecs]
    widths = out_widths or [tn] * len(out_dtypes)
    out_specs = [pl.BlockSpec((tm, wd), lambda j, i: (i, j)) for wd in widths]
    out_shape = [jax.ShapeDtypeStruct((m, n_out // tn * wd), dt) for dt, wd in zip(out_dtypes, widths)]
    return pl.pallas_call(
        body, out_shape=out_shape, grid=(n_out // tn, m // tm), in_specs=in_specs, out_specs=out_specs,
        name=name, compiler_params=_params(("parallel", "parallel")),
    )(*a_list, *b_list, *[e for e, _ in extras], *vecs)


def _mm_tn(name, a, b, *, tk, tn, tt, a_fn=None):
    a_list = list(a) if a_fn is not None else [a]
    na = len(a_list)
    t, k = a_list[0].shape
    n = b.shape[1]
    steps = t // tt

    def body(*refs):
        a_refs, b_ref, o_ref, acc_ref = refs[:na], refs[na], refs[na + 1], refs[na + 2]
        s = pl.program_id(2)

        @pl.when(s == 0)
        def _():
            acc_ref[...] = jnp.zeros_like(acc_ref)

        av = a_refs[0][...] if a_fn is None else a_fn([r[...] for r in a_refs])
        acc_ref[...] += lax.dot_general(av, b_ref[...], (((0,), (0,)), ((), ())), preferred_element_type=F32)

        @pl.when(s == steps - 1)
        def _():
            o_ref[...] = acc_ref[...]

    return pl.pallas_call(
        body, out_shape=jax.ShapeDtypeStruct((k, n), F32), grid=(k // tk, n // tn, steps),
        in_specs=[pl.BlockSpec((tt, tk), lambda i, j, s: (s, i))] * na
        + [pl.BlockSpec((tt, tn), lambda i, j, s: (s, j))],
        out_specs=pl.BlockSpec((tk, tn), lambda i, j, s: (i, j)),
        scratch_shapes=[pltpu.VMEM((tk, tn), F32)], name=name,
        compiler_params=_params(("parallel", "parallel", "arbitrary")),
    )(*a_list, b)


def _first(accs, extras, vecs):
    return [accs[0]]


def _total(accs, extras, vecs):
    out = accs[0]
    for r in accs[1:]:
        out = out + r
    return [out]


def _row_spec(tm, width, col=0):
    return pl.BlockSpec((tm, width), functools.partial(lambda i, col: (i, col), col=col))


def _vec_spec(width):
    return pl.BlockSpec((1, width), lambda i: (0, 0))


def _normmod_fwd(name, x, g, scale, shift, tm=512):
    t = x.shape[0]

    def body(x_ref, g_ref, sc_ref, sh_ref, h_ref):
        xv = x_ref[...]
        r = lax.rsqrt(jnp.mean(xv * xv, axis=-1, keepdims=True) + EPS)
        h_ref[...] = ((xv * r) * g_ref[...] * (1.0 + sc_ref[...]) + sh_ref[...]).astype(BF16)

    return pl.pallas_call(
        body, out_shape=jax.ShapeDtypeStruct((t, D), BF16), grid=(t // tm,),
        in_specs=[_row_spec(tm, D), _vec_spec(D), _vec_spec(D), _vec_spec(D)], out_specs=_row_spec(tm, D),
        name=name, compiler_params=_params(("parallel",)),
    )(x, g, scale, shift)


def _normmod_bwd(name, x, dh, dres, g, scale, tm=256):
    t = x.shape[0]
    steps = t // tm

    def body(x_ref, dh_ref, dres_ref, g_ref, sc_ref, dx_ref, dg_ref, dsc_ref, dsh_ref):
        i = pl.program_id(0)

        @pl.when(i == 0)
        def _():
            dg_ref[...] = jnp.zeros_like(dg_ref)
            dsh_ref[...] = jnp.zeros_like(dsh_ref)

        xv = x_ref[...]
        dh = dh_ref[...]
        r = lax.rsqrt(jnp.mean(xv * xv, axis=-1, keepdims=True) + EPS)
        xh = xv * r
        dxh = dh * (g_ref[...] * (1.0 + sc_ref[...]))
        dx_ref[...] = dres_ref[...] + r * (dxh - xh * jnp.mean(dxh * xh, axis=-1, keepdims=True))
        dg_ref[...] += jnp.sum(dh * xh, axis=0, keepdims=True)
        dsh_ref[...] += jnp.sum(dh, axis=0, keepdims=True)

        @pl.when(i == steps - 1)
        def _():
            acc = dg_ref[...]
            dg_ref[...] = acc * (1.0 + sc_ref[...])
            dsc_ref[...] = acc * g_ref[...]

    vec = jax.ShapeDtypeStruct((1, D), F32)
    return pl.pallas_call(
        body, out_shape=[jax.ShapeDtypeStruct((t, D), F32), vec, vec, vec], grid=(steps,),
        in_specs=[_row_spec(tm, D), _row_spec(tm, D), _row_spec(tm, D), _vec_spec(D), _vec_spec(D)],
        out_specs=[_row_spec(tm, D), _vec_spec(D), _vec_spec(D), _vec_spec(D)],
        name=name, compiler_params=_params(("arbitrary",)),
    )(x, dh, dres, g, scale)


def _resgate_bwd(name, dx, f, gate, coef, tm=512):
    t = dx.shape[0]

    def body(dx_ref, f_ref, gate_ref, df_ref, dgate_ref):
        @pl.when(pl.program_id(0) == 0)
        def _():
            dgate_ref[...] = jnp.zeros_like(dgate_ref)

        dxv = dx_ref[...]
        df_ref[...] = ((coef * gate_ref[...]) * dxv).astype(BF16)
        dgate_ref[...] += jnp.sum((coef * f_ref[...].astype(F32)) * dxv, axis=0, keepdims=True)

    return pl.pallas_call(
        body, out_shape=[jax.ShapeDtypeStruct((t, D), BF16), jax.ShapeDtypeStruct((1, D), F32)], grid=(t // tm,),
        in_specs=[_row_spec(tm, D), _row_spec(tm, D), _vec_spec(D)], out_specs=[_row_spec(tm, D), _vec_spec(D)],
        name=name, compiler_params=_params(("arbitrary",)),
    )(dx, f, gate)


def _loss_bwd(name, x, target, g, tm=256):
    t = x.shape[0]

    def body(x_ref, t_ref, g_ref, dx_ref, dg_ref, loss_ref):
        @pl.when(pl.program_id(0) == 0)
        def _():
            dg_ref[...] = jnp.zeros_like(dg_ref)
            loss_ref[...] = jnp.zeros_like(loss_ref)

        xv = x_ref[...]
        r = lax.rsqrt(jnp.mean(xv * xv, axis=-1, keepdims=True) + EPS)
        xh = xv * r
        err = xh * g_ref[...] - t_ref[...]
        dy = err * (1.0 / D)
        dxh = dy * g_ref[...]
        dx_ref[...] = r * (dxh - xh * jnp.mean(dxh * xh, axis=-1, keepdims=True))
        dg_ref[...] += jnp.sum(dy * xh, axis=0, keepdims=True)
        loss_ref[...] += jnp.sum(err * err, axis=0, keepdims=True) * (0.5 / D)

    vec = jax.ShapeDtypeStruct((1, D), F32)
    return pl.pallas_call(
        body, out_shape=[jax.ShapeDtypeStruct((t, D), F32), vec, vec], grid=(t // tm,),
        in_specs=[_row_spec(tm, D), _row_spec(tm, D), _vec_spec(D)],
        out_specs=[_row_spec(tm, D), _vec_spec(D), _vec_spec(D)],
        name=name, compiler_params=_params(("arbitrary",)),
    )(x, target, g)


def _mod_fwd(name, c8, ada_w, ada_b, tn=2304):
    n = ada_w.shape[1]

    def body(c_ref, w_ref, b_ref, o_ref):
        cv = c_ref[...]
        ca = (cv * _sigmoid(cv)).astype(BF16)
        o_ref[...] = jnp.dot(ca, w_ref[...], preferred_element_type=F32) + b_ref[...]

    return pl.pallas_call(
        body, out_shape=jax.ShapeDtypeStruct((8, n), F32), grid=(n // tn,),
        in_specs=[pl.BlockSpec((8, D), lambda j: (0, 0)), pl.BlockSpec((D, tn), lambda j: (0, j)),
                  pl.BlockSpec((1, tn), lambda j: (0, j))],
        out_specs=pl.BlockSpec((8, tn), lambda j: (0, j)), name=name, compiler_params=_params(("parallel",)),
    )(c8, ada_w, ada_b)


def _mod_bwd(name, c_col, dmod, tk=256):
    n = dmod.shape[1]

    def body(c_ref, d_ref, o_ref):
        cv = c_ref[...]
        o_ref[...] = (cv * _sigmoid(cv)) * d_ref[...]

    return pl.pallas_call(
        body, out_shape=jax.ShapeDtypeStruct((D, n), F32), grid=(D // tk,),
        in_specs=[pl.BlockSpec((tk, 1), lambda i: (i, 0)), pl.BlockSpec((1, n), lambda i: (0, 0))],
        out_specs=pl.BlockSpec((tk, n), lambda i: (i, 0)), name=name, compiler_params=_params(("parallel",)),
    )(c_col, dmod)


def _rope_tables(t):
    half = HEAD // 2
    inv_freq = 10000.0 ** (-(jnp.arange(half, dtype=F32) * 2.0 / HEAD))
    ang = jnp.arange(t, dtype=F32)[:, None] * inv_freq[None, :]
    cos, sin = jnp.cos(ang), jnp.sin(ang)
    cos_t = jnp.tile(jnp.concatenate([cos, cos], axis=1), (1, GW // HEAD))
    sin_t = jnp.tile(jnp.concatenate([-sin, sin], axis=1), (1, GW // HEAD))
    return cos_t, sin_t


def _rotate(tv, cos, sin_signed):
    lane = lax.broadcasted_iota(jnp.int32, tv.shape, 1)
    first = (lane % HEAD) < (HEAD // 2)
    partner = jnp.where(first, pltpu.roll(tv, GW - HEAD // 2, 1), pltpu.roll(tv, HEAD // 2, 1))
    return tv * cos + partner * sin_signed


def _rope_fwd(name, zqkv, cos_t, sin_t, tm=512):
    t = zqkv.shape[0]
    ng = len(DILATIONS)
    n = 3 * ng

    def body(*refs):
        z_refs, cos_ref, sin_ref, o_refs = refs[:n], refs[n], refs[n + 1], refs[n + 2:]
        cos, sin = cos_ref[...], sin_ref[...]
        for idx in range(n):
            r = z_refs[idx][...]
            if idx < 2 * ng:
                r = _rotate(r, cos, sin)
            if idx < ng:
                r = r * (HEAD ** -0.5)
            o_refs[idx][...] = r.astype(BF16)

    return pl.pallas_call(
        body, out_shape=[jax.ShapeDtypeStruct((t, GW), BF16)] * n, grid=(t // tm,),
        in_specs=[_row_spec(tm, GW, col) for col in range(n)] + [_row_spec(tm, GW), _row_spec(tm, GW)],
        out_specs=[_row_spec(tm, GW)] * n, name=name, compiler_params=_params(("parallel",)),
    )(*([zqkv] * n), cos_t, sin_t)


def _rope_bwd(name, grads, cos_t, sin_t, tm=512):
    t = grads[0].shape[0]
    ng = len(DILATIONS)
    n = len(grads)

    def body(*refs):
        g_refs, cos_ref, sin_ref, o_ref = refs[:n], refs[n], refs[n + 1], refs[n + 2]
        cos, sin = cos_ref[...], -sin_ref[...]
        for idx in range(n):
            r = g_refs[idx][...]
            if idx < 2 * ng:
                r = _rotate(r, cos, sin)
            if idx < ng:
                r = r * (HEAD ** -0.5)
            o_ref[:, idx * GW:(idx + 1) * GW] = r.astype(BF16)

    return pl.pallas_call(
        body, out_shape=jax.ShapeDtypeStruct((t, n * GW), BF16), grid=(t // tm,),
        in_specs=[_row_spec(tm, GW)] * (n + 2), out_specs=_row_spec(tm, n * GW),
        name=name, compiler_params=_params(("parallel",)),
    )(*grads, cos_t, sin_t)


def _head_cols(h):
    return slice(h * HEAD, (h + 1) * HEAD)


def _band_mask_q(has_prev):
    qi = lax.broadcasted_iota(jnp.int32, (BAND, 2 * BAND), 0)
    kj = lax.broadcasted_iota(jnp.int32, (BAND, 2 * BAND), 1)
    dist = qi + BAND - kj
    return (dist >= 0) & (dist <= BAND) & ((kj >= BAND) | has_prev)


def _attn_fwd(name, q, k, v, group):
    t = q.shape[0]
    d = DILATIONS[group]
    length = t // d
    qb = min(512, length)
    sub = qb // BAND
    nblk = length // qb

    def body(q_ref, kc_ref, kp_ref, vc_ref, vp_ref, o_ref, lse_ref):
        blk = pl.program_id(1)
        k_ext = jnp.concatenate([kp_ref[...], kc_ref[...]], axis=0)
        v_ext = jnp.concatenate([vp_ref[...], vc_ref[...]], axis=0)
        for j in range(sub):
            mask = _band_mask_q((blk * sub + j) > 0)
            qj = q_ref[j * BAND:(j + 1) * BAND, :]
            kj = k_ext[j * BAND:(j + 2) * BAND, :]
            vj = v_ext[j * BAND:(j + 2) * BAND, :]
            outs, lses = [], []
            for h in range(GW // HEAD):
                s = lax.dot_general(qj[:, _head_cols(h)], kj[:, _head_cols(h)], (((1,), (1,)), ((), ())),
                                    preferred_element_type=F32)
                s = jnp.where(mask, s, NEG_INF)
                m = jnp.max(s, axis=-1, keepdims=True)
                p = jnp.exp(s - m)
                den = jnp.sum(p, axis=-1, keepdims=True)
                o = jnp.dot(p.astype(BF16), vj[:, _head_cols(h)], preferred_element_type=F32)
                outs.append(o / den)
                lses.append(jnp.broadcast_to(m + jnp.log(den), (BAND, HEAD)))
            o_ref[j * BAND:(j + 1) * BAND, :] = jnp.concatenate(outs, axis=1)
            lse_ref[j * BAND:(j + 1) * BAND, :] = jnp.concatenate(lses, axis=1)

    prev = qb // BAND
    cur = lambda r, b: (b, r)
    before = lambda r, b: (jnp.maximum(b * prev - 1, 0), r)
    big, halo = pl.BlockSpec((qb, GW), cur), pl.BlockSpec((BAND, GW), before)
    view = lambda a: a.reshape(length, d * GW)
    out = pl.pallas_call(
        body, out_shape=[jax.ShapeDtypeStruct((length, d * GW), F32)] * 2, grid=(d, nblk),
        in_specs=[big, big, halo, big, halo], out_specs=[big] * 2, name=name,
        compiler_params=_params(("parallel", "parallel")),
    )(view(q), view(k), view(k), view(v), view(v))
    return out[0].reshape(t, GW), out[1].reshape(t, GW)


def _attn_merge(name, outs, lses, tm=512):
    t = outs[0].shape[0]
    n = len(outs)

    def body(*refs):
        o_refs, l_refs = refs[:n], refs[n:2 * n]
        ob_ref, of_ref, lj_ref = refs[2 * n:]
        ls = [r[...] for r in l_refs]
        m = ls[0]
        for v in ls[1:]:
            m = jnp.maximum(m, v)
        es = [jnp.exp(v - m) for v in ls]
        tot = es[0]
        for v in es[1:]:
            tot = tot + v
        acc = (es[0] / tot) * o_refs[0][...]
        for e, o_ref in zip(es[1:], o_refs[1:]):
            acc = acc + (e / tot) * o_ref[...]
        ob_ref[...] = acc.astype(BF16)
        of_ref[...] = acc
        lj_ref[...] = m + jnp.log(tot)

    return pl.pallas_call(
        body, out_shape=[jax.ShapeDtypeStruct((t, GW), BF16), jax.ShapeDtypeStruct((t, GW), F32),
                         jax.ShapeDtypeStruct((t, GW), F32)], grid=(t // tm,),
        in_specs=[_row_spec(tm, GW)] * (2 * n), out_specs=[_row_spec(tm, GW)] * 3,
        name=name, compiler_params=_params(("parallel",)),
    )(*outs, *lses)


def _attn_bwd_prep(name, do, o, tm=512):
    t = do.shape[0]

    def body(do_ref, o_ref, dsum_ref, dob_ref):
        dov = do_ref[...]
        prod = dov * o_ref[...]
        parts = [jnp.broadcast_to(jnp.sum(prod[:, _head_cols(h)], axis=-1, keepdims=True), (tm, HEAD))
                 for h in range(GW // HEAD)]
        dsum_ref[...] = jnp.concatenate(parts, axis=1)
        dob_ref[...] = dov.astype(BF16)

    return pl.pallas_call(
        body, out_shape=[jax.ShapeDtypeStruct((t, GW), F32), jax.ShapeDtypeStruct((t, GW), BF16)], grid=(t // tm,),
        in_specs=[_row_spec(tm, GW)] * 2, out_specs=[_row_spec(tm, GW)] * 2,
        name=name, compiler_params=_params(("parallel",)),
    )(do, o)


def _attn_bwd(name, q, k, v, do, lj, dsum, group):
    t = q.shape[0]
    d = DILATIONS[group]
    length = t // d
    qb = min(512, length)
    sub = qb // BAND
    nblk = length // qb
    total = length // BAND

    def body(qc_ref, qn_ref, kc_ref, kp_ref, vc_ref, vp_ref, doc_ref, don_ref, ljc_ref, ljn_ref, dsc_ref, dsn_ref,
             dq_ref, dk_ref, dv_ref):
        blk = pl.program_id(1)
        q_ext = jnp.concatenate([qc_ref[...], qn_ref[...]], axis=0)
        do_ext = jnp.concatenate([doc_ref[...], don_ref[...]], axis=0)
        lj_ext = jnp.concatenate([ljc_ref[...], ljn_ref[...]], axis=0)
        ds_ext = jnp.concatenate([dsc_ref[...], dsn_ref[...]], axis=0)
        k_ext = jnp.concatenate([kp_ref[...], kc_ref[...]], axis=0)
        v_ext = jnp.concatenate([vp_ref[...], vc_ref[...]], axis=0)
        qi2 = lax.broadcasted_iota(jnp.int32, (2 * BAND, BAND), 0)
        kj2 = lax.broadcasted_iota(jnp.int32, (2 * BAND, BAND), 1)
        for j in range(sub):
            gblk = blk * sub + j
            rows = slice(j * BAND, (j + 1) * BAND)
            rows2 = slice(j * BAND, (j + 2) * BAND)
            mask_q = _band_mask_q(gblk > 0)
            mask_k = ((qi2 < BAND) & (kj2 <= qi2)) | ((qi2 >= BAND) & (kj2 >= qi2 - BAND) & (gblk + 1 < total))
            dqs, dks, dvs = [], [], []
            for h in range(GW // HEAD):
                hc = _head_cols(h)
                col = slice(h * HEAD, h * HEAD + 1)
                qh = q_ext[rows, hc]
                kh2 = k_ext[rows2, hc]
                s = lax.dot_general(qh, kh2, (((1,), (1,)), ((), ())), preferred_element_type=F32)
                p = jnp.where(mask_q, jnp.exp(s - lj_ext[rows, col]), 0.0)
                dp = lax.dot_general(do_ext[rows, hc], v_ext[rows2, hc], (((1,), (1,)), ((), ())),
                                     preferred_element_type=F32)
                dsc = p * (dp - ds_ext[rows, col])
                dqs.append(jnp.dot(dsc.astype(BF16), kh2, preferred_element_type=F32))
                qh2 = q_ext[rows2, hc]
                kh = k_ext[BAND + j * BAND:BAND + (j + 1) * BAND, hc]
                vh = v_ext[BAND + j * BAND:BAND + (j + 1) * BAND, hc]
                doh2 = do_ext[rows2, hc]
                s2 = lax.dot_general(qh2, kh, (((1,), (1,)), ((), ())), preferred_element_type=F32)
                p2 = jnp.where(mask_k, jnp.exp(s2 - lj_ext[rows2, col]), 0.0)
                dp2 = lax.dot_general(doh2, vh, (((1,), (1,)), ((), ())), preferred_element_type=F32)
                ds2 = p2 * (dp2 - ds_ext[rows2, col])
                dvs.append(lax.dot_general(p2.astype(BF16), doh2, (((0,), (0,)), ((), ())),
                                           preferred_element_type=F32))
                dks.append(lax.dot_general(ds2.astype(BF16), qh2, (((0,), (0,)), ((), ())),
                                           preferred_element_type=F32))
            dq_ref[rows, :] = jnp.concatenate(dqs, axis=1)
            dk_ref[rows, :] = jnp.concatenate(dks, axis=1)
            dv_ref[rows, :] = jnp.concatenate(dvs, axis=1).astype(BF16)

    prev = qb // BAND
    cur = lambda r, b: (b, r)
    before = lambda r, b: (jnp.maximum(b * prev - 1, 0), r)
    after = lambda r, b: (jnp.minimum((b + 1) * prev, total - 1), r)
    big = pl.BlockSpec((qb, GW), cur)
    nxt = pl.BlockSpec((BAND, GW), after)
    prv = pl.BlockSpec((BAND, GW), before)
    view = lambda a: a.reshape(length, d * GW)
    out = pl.pallas_call(
        body, out_shape=[jax.ShapeDtypeStruct((length, d * GW), F32), jax.ShapeDtypeStruct((length, d * GW), F32),
                         jax.ShapeDtypeStruct((length, d * GW), BF16)], grid=(d, nblk),
        in_specs=[big, nxt, big, prv, big, prv, big, nxt, big, nxt, big, nxt],
        out_specs=[big] * 3, name=name, compiler_params=_params(("parallel", "parallel")),
    )(view(q), view(q), view(k), view(k), view(v), view(v), view(do), view(do), view(lj), view(lj), view(dsum),
      view(dsum))
    return out[0].reshape(t, GW), out[1].reshape(t, GW), out[2].reshape(t, GW)


def _conv_fwd(name, zu, conv_w, conv_b, ln_g, ln_b, tm=256):
    t = zu.shape[0]
    per = tm // HALO

    def body(a_ref, gl_ref, ah_ref, glh_ref, w_ref, b_ref, g_ref, beta_ref, hc_ref, s_ref, ext_ref):
        i = pl.program_id(0)
        halo = ah_ref[...] * _sigmoid(glh_ref[...])
        ext_ref[0:HALO, :] = jnp.where(i > 0, halo, 0.0)
        ext_ref[HALO:, :] = a_ref[...] * _sigmoid(gl_ref[...])
        acc = jnp.zeros((tm, D), F32) + b_ref[...]
        for kk in range(CONV_K):
            acc = acc + w_ref[kk:kk + 1, :] * ext_ref[HALO - CONV_K + 1 + kk:HALO - CONV_K + 1 + kk + tm, :]
        hc_ref[...] = acc
        mu = jnp.mean(acc, axis=-1, keepdims=True)
        xc = acc - mu
        var = jnp.mean(xc * xc, axis=-1, keepdims=True)
        ln = xc * lax.rsqrt(var + EPS) * g_ref[...] + beta_ref[...]
        s_ref[...] = (ln * _sigmoid(ln)).astype(BF16)

    halo_map = lambda col: (lambda i: (jnp.maximum(i * per - 1, 0), col))
    return pl.pallas_call(
        body, out_shape=[jax.ShapeDtypeStruct((t, D), F32), jax.ShapeDtypeStruct((t, D), BF16)], grid=(t // tm,),
        in_specs=[_row_spec(tm, D, 0), _row_spec(tm, D, 1), pl.BlockSpec((HALO, D), halo_map(0)),
                  pl.BlockSpec((HALO, D), halo_map(1)), pl.BlockSpec((HALO, D), lambda i: (0, 0)),
                  _vec_spec(D), _vec_spec(D), _vec_spec(D)],
        out_specs=[_row_spec(tm, D), _row_spec(tm, D)], scratch_shapes=[pltpu.VMEM((tm + HALO, D), F32)],
        name=name, compiler_params=_params(("parallel",)),
    )(zu, zu, zu, zu, conv_w, conv_b, ln_g, ln_b)


def _conv_ln_bwd(name, hc, ds, ln_g, ln_b, tm=256):
    t = hc.shape[0]

    def body(hc_ref, ds_ref, g_ref, beta_ref, dhc_ref, dg_ref, dbeta_ref, dbias_ref):
        @pl.when(pl.program_id(0) == 0)
        def _():
            dg_ref[...] = jnp.zeros_like(dg_ref)
            dbeta_ref[...] = jnp.zeros_like(dbeta_ref)
            dbias_ref[...] = jnp.zeros_like(dbias_ref)

        hv = hc_ref[...]
        mu = jnp.mean(hv, axis=-1, keepdims=True)
        xc = hv - mu
        rstd = lax.rsqrt(jnp.mean(xc * xc, axis=-1, keepdims=True) + EPS)
        xh = xc * rstd
        ln = xh * g_ref[...] + beta_ref[...]
        sg = _sigmoid(ln)
        dln = ds_ref[...] * (sg * (1.0 + ln * (1.0 - sg)))
        dxh = dln * g_ref[...]
        dh = rstd * (dxh - jnp.mean(dxh, axis=-1, keepdims=True) - xh * jnp.mean(dxh * xh, axis=-1, keepdims=True))
        dhc_ref[...] = dh
        dg_ref[...] += jnp.sum(dln * xh, axis=0, keepdims=True)
        dbeta_ref[...] += jnp.sum(dln, axis=0, keepdims=True)
        dbias_ref[...] += jnp.sum(dh, axis=0, keepdims=True)

    vec = jax.ShapeDtypeStruct((1, D), F32)
    return pl.pallas_call(
        body, out_shape=[jax.ShapeDtypeStruct((t, D), F32), vec, vec, vec], grid=(t // tm,),
        in_specs=[_row_spec(tm, D), _row_spec(tm, D), _vec_spec(D), _vec_spec(D)],
        out_specs=[_row_spec(tm, D), _vec_spec(D), _vec_spec(D), _vec_spec(D)],
        name=name, compiler_params=_params(("arbitrary",)),
    )(hc, ds, ln_g, ln_b)


def _conv_bwd(name, zu, dhc, conv_w, tm=256):
    t = zu.shape[0]
    per = tm // HALO
    steps = t // tm

    def body(a_ref, gl_ref, ah_ref, glh_ref, d_ref, dn_ref, w_ref, dz_ref, dw_ref, ext_ref, dext_ref):
        i = pl.program_id(0)

        @pl.when(i == 0)
        def _():
            dw_ref[...] = jnp.zeros_like(dw_ref)

        av, sg = a_ref[...], _sigmoid(gl_ref[...])
        ext_ref[0:HALO, :] = jnp.where(i > 0, ah_ref[...] * _sigmoid(glh_ref[...]), 0.0)
        ext_ref[HALO:, :] = av * sg
        dv = d_ref[...]
        dext_ref[0:tm, :] = dv
        dext_ref[tm:, :] = jnp.where(i < steps - 1, dn_ref[...], 0.0)
        acc = jnp.zeros((tm, D), F32)
        for kk in range(CONV_K):
            acc = acc + w_ref[kk:kk + 1, :] * dext_ref[CONV_K - 1 - kk:CONV_K - 1 - kk + tm, :]
            prod = dv * ext_ref[HALO - CONV_K + 1 + kk:HALO - CONV_K + 1 + kk + tm, :]
            dw_ref[kk:kk + 1, :] += jnp.sum(prod, axis=0, keepdims=True)
        dz_ref[:, 0:D] = (acc * sg).astype(BF16)
        dz_ref[:, D:] = (acc * av * sg * (1.0 - sg)).astype(BF16)

    halo_map = lambda col: (lambda i: (jnp.maximum(i * per - 1, 0), col))
    return pl.pallas_call(
        body, out_shape=[jax.ShapeDtypeStruct((t, 2 * D), BF16), jax.ShapeDtypeStruct((HALO, D), F32)],
        grid=(steps,),
        in_specs=[_row_spec(tm, D, 0), _row_spec(tm, D, 1), pl.BlockSpec((HALO, D), halo_map(0)),
                  pl.BlockSpec((HALO, D), halo_map(1)), _row_spec(tm, D),
                  pl.BlockSpec((HALO, D), lambda i: (jnp.minimum((i + 1) * per, t // HALO - 1), 0)),
                  pl.BlockSpec((HALO, D), lambda i: (0, 0))],
        out_specs=[_row_spec(tm, 2 * D), pl.BlockSpec((HALO, D), lambda i: (0, 0))],
        scratch_shapes=[pltpu.VMEM((tm + HALO, D), F32), pltpu.VMEM((tm + HALO, D), F32)],
        name=name, compiler_params=_params(("arbitrary",)),
    )(zu, zu, zu, zu, dhc, dhc, conv_w)


def _ffn_fwd(tag, x, h, wg, wu, wd, gate):
    def act(accs, extras, vecs):
        g, u = accs
        return [g, u, (g * _sigmoid(g)) * u]

    gv, uv, av = _mm(f"ffn_up_{tag}", [h], [wg, wu], [(0, 0), (0, 1)], act, [BF16, BF16, BF16],
                     tm=512, tn=1408, n_out=DFF)

    def residual(accs, extras, vecs):
        return [extras[0] + (0.5 * vecs[0]) * accs[0], accs[0]]

    x_new, f = _mm(f"ffn_down_{tag}", [av], [wd], [(0, 0)], residual, [F32, BF16], tm=512, tn=512, n_out=D,
                   extras=[(x, 0)], vecs=[gate])
    return x_new, (gv, uv, f)


def _ffn_bwd(tag, dx, x, h, saved, wg, wu, wd, g, scale, gate):
    gv, uv, f = saved
    df, dgate = _resgate_bwd(f"ffn_gate_bwd_{tag}", dx, f, gate, 0.5)

    def act(blocks):
        gf, uf = blocks[0].astype(F32), blocks[1].astype(F32)
        return ((gf * _sigmoid(gf)) * uf).astype(BF16)

    dwd = _mm_tn(f"ffn_dwd_{tag}", [gv, uv], df, tk=1408, tn=1024, tt=512, a_fn=act)

    def act_bwd(accs, extras, vecs):
        da = accs[0]
        gf, uf = extras[0].astype(F32), extras[1].astype(F32)
        sg = _sigmoid(gf)
        return [da * uf * (sg * (1.0 + gf * (1.0 - sg))), da * (gf * sg)]

    dg, du = _mm(f"ffn_da_{tag}", [df], [wd], [(0, 0)], act_bwd, [BF16, BF16], tm=512, tn=1408, n_out=DFF,
                 trans_b=True, extras=[(gv, 0), (uv, 0)])
    dwg = _mm_tn(f"ffn_dwg_{tag}", h, dg, tk=1024, tn=1408, tt=512)
    dwu = _mm_tn(f"ffn_dwu_{tag}", h, du, tk=1024, tn=1408, tt=512)
    dh, = _mm(f"ffn_dh_{tag}", [dg, du], [wg, wu], [(0, 0), (1, 1)], _total, [F32], tm=512, tn=512, n_out=D,
              trans_b=True)
    dx_in, dgn, dscale, dshift = _normmod_bwd(f"ffn_norm_bwd_{tag}", x, dh, dx, g, scale)
    return dx_in, dwg, dwu, dwd, dgn, (dshift, dscale, dgate)


def _mix_fwd(tag, x, h, w_in, attn_wo, conv_w, conv_b, ln_g, ln_b, conv_wo, w_out, gate, cos_t, sin_t):
    w_qkv, w_u, w_g = w_in[:, :QKV], w_in[:, QKV:QKV + 2 * D], w_in[:, QKV + 2 * D:]
    zqkv, = _mm(f"mix_qkv_{tag}", [h], [w_qkv], [(0, 0)], _first, [F32], tm=512, tn=768, n_out=QKV)
    zu, = _mm(f"mix_u_{tag}", [h], [w_u], [(0, 0)], _first, [F32], tm=512, tn=1024, n_out=2 * D)
    zg, = _mm(f"mix_g_{tag}", [h], [w_g], [(0, 0)], _first, [BF16], tm=512, tn=1024, n_out=2 * D)
    qkv = _rope_fwd(f"rope_{tag}", zqkv, cos_t, sin_t)
    n = len(DILATIONS)
    outs, lses = [], []
    for grp in range(n):
        o, lse = _attn_fwd(f"attn_fwd_{tag}_{grp}", qkv[grp], qkv[n + grp], qkv[2 * n + grp], grp)
        outs.append(o)
        lses.append(lse)
    ob, of, lj = _attn_merge(f"attn_merge_{tag}", outs, lses)
    hc, s = _conv_fwd(f"conv_fwd_{tag}", zu, conv_w, conv_b, ln_g, ln_b)

    def gated(accs, extras, vecs):
        ya, yc = accs
        return [_sigmoid(extras[0].astype(F32)) * ya + _sigmoid(extras[1].astype(F32)) * yc, ya, yc]

    y, ya, yc = _mm(f"mix_y_{tag}", [ob, s], [attn_wo, conv_wo], [(0, 0), (1, 1)], gated, [BF16, BF16, BF16],
                    tm=512, tn=1024, n_out=D, extras=[(zg, 0), (zg, 1)])

    def residual(accs, extras, vecs):
        return [extras[0] + vecs[0] * accs[0], accs[0]]

    x_new, f = _mm(f"mix_out_{tag}", [y], [w_out], [(0, 0)], residual, [F32, BF16], tm=512, tn=512, n_out=D,
                   extras=[(x, 0)], vecs=[gate])
    return x_new, (zu, zg, qkv, ob, of, lj, hc, s, y, ya, yc, f, (w_qkv, w_u, w_g))


def _mix_bwd(tag, dx, x, h, saved, attn_wo, conv_w, ln_g, ln_b, conv_wo, w_out, g, scale, gate, cos_t, sin_t):
    zu, zg, qkv, ob, of, lj, hc, s, y, ya, yc, f, w_parts = saved
    n = len(DILATIONS)
    df, dgate = _resgate_bwd(f"mix_gate_bwd_{tag}", dx, f, gate, 1.0)
    dw_out = _mm_tn(f"mix_dwout_{tag}", y, df, tk=1024, tn=1024, tt=512)

    def gated_bwd(accs, extras, vecs):
        dy = accs[0]
        sa, sc = _sigmoid(extras[0].astype(F32)), _sigmoid(extras[1].astype(F32))
        dga = dy * extras[2].astype(F32) * (sa * (1.0 - sa))
        dgc = dy * extras[3].astype(F32) * (sc * (1.0 - sc))
        return [dy * sa, dy * sc, jnp.concatenate([dga, dgc], axis=1)]

    dya, dyc, dzg = _mm(f"mix_dy_{tag}", [df], [w_out], [(0, 0)], gated_bwd, [BF16] * 3, tm=512, tn=1024,
                        n_out=D, trans_b=True, extras=[(zg, 0), (zg, 1), (ya, 0), (yc, 0)],
                        out_widths=[D, D, 2 * D])
    dw_attn = _mm_tn(f"mix_dwattn_{tag}", ob, dya, tk=GW, tn=1024, tt=512)
    dw_conv_o = _mm_tn(f"mix_dwconvo_{tag}", s, dyc, tk=1024, tn=1024, tt=512)
    do, = _mm(f"mix_do_{tag}", [dya], [attn_wo], [(0, 0)], _first, [F32], tm=512, tn=GW, n_out=GW, trans_b=True)
    ds, = _mm(f"mix_ds_{tag}", [dyc], [conv_wo], [(0, 0)], _first, [F32], tm=512, tn=1024, n_out=D, trans_b=True)

    dsum, dob = _attn_bwd_prep(f"attn_prep_{tag}", do, of)
    dqs, dks, dvs = [], [], []
    for grp in range(n):
        dq, dk, dv = _attn_bwd(f"attn_bwd_{tag}_{grp}", qkv[grp], qkv[n + grp], qkv[2 * n + grp], dob, lj, dsum,
                               grp)
        dqs.append(dq)
        dks.append(dk)
        dvs.append(dv)
    dzqkv = _rope_bwd(f"rope_bwd_{tag}", dqs + dks + dvs, cos_t, sin_t)

    dhc, dln_g, dln_b, dconv_b = _conv_ln_bwd(f"conv_ln_bwd_{tag}", hc, ds, ln_g, ln_b)
    dzu, dconv_w = _conv_bwd(f"conv_bwd_{tag}", zu, dhc, conv_w)

    dz_parts = [dzqkv, dzu, dzg]
    dw_in = jnp.concatenate(
        [_mm_tn(f"mix_dwin_{tag}_{i}", h, dzp, tk=1024, tn=dzp.shape[1] // 2, tt=512)
         for i, dzp in enumerate(dz_parts)], axis=1)
    dh, = _mm(f"mix_dh_{tag}", dz_parts, list(w_parts), [(0, 0), (1, 1), (2, 2)], _total, [F32], tm=512, tn=512,
              n_out=D, trans_b=True)
    dx_in, dgn, dscale, dshift = _normmod_bwd(f"mix_norm_bwd_{tag}", x, dh, dx, g, scale)
    grads = dict(w_in=dw_in, attn_wo=dw_attn, conv_w=dconv_w[:CONV_K], conv_b=dconv_b, conv_ln_g=dln_g,
                 conv_ln_b=dln_b, conv_wo=dw_conv_o, w_out=dw_out)
    return dx_in, grads, dgn, (dshift, dscale, dgate)


def _local_step(x, c, target, w, wf):
    t = x.shape[0]
    cos_t, sin_t = _rope_tables(t)
    c8 = jnp.concatenate([c, jnp.zeros((7, D), F32)], axis=0)
    row = lambda v: v.reshape(1, -1)
    conv_w_pad = jnp.concatenate([wf["conv_w"], jnp.zeros((DEPTH, HALO - CONV_K, D), F32)], axis=1)

    saved = []
    for l in range(DEPTH):
        mod = _mod_fwd(f"mod_{l}", c8, w["ada_w"][l], row(wf["ada_b"][l]))[0:1]
        mods = [mod[:, i * D:(i + 1) * D] for i in range(N_MOD)]
        gains = [row(wf["norm_g"][l, i]) for i in range(3)]
        lay = dict(mods=mods, gains=gains)

        lay["x0"] = x
        lay["h0"] = _normmod_fwd(f"norm_a_{l}", x, gains[0], mods[1], mods[0])
        x, lay["ffn0"] = _ffn_fwd(f"a_{l}", x, lay["h0"], w["ffn_wg"][l, 0], w["ffn_wu"][l, 0], w["ffn_wd"][l, 0],
                                  mods[2])
        lay["x1"] = x
        lay["h1"] = _normmod_fwd(f"norm_m_{l}", x, gains[1], mods[4], mods[3])
        x, lay["mix"] = _mix_fwd(f"{l}", x, lay["h1"], w["w_in"][l], w["attn_wo"][l], conv_w_pad[l],
                                 row(wf["conv_b"][l]), row(wf["conv_ln_g"][l]), row(wf["conv_ln_b"][l]),
                                 w["conv_wo"][l], w["w_out"][l], mods[5], cos_t, sin_t)
        lay["x2"] = x
        lay["h2"] = _normmod_fwd(f"norm_b_{l}", x, gains[2], mods[7], mods[6])
        x, lay["ffn1"] = _ffn_fwd(f"b_{l}", x, lay["h2"], w["ffn_wg"][l, 1], w["ffn_wu"][l, 1], w["ffn_wd"][l, 1],
                                  mods[8])
        saved.append(lay)

    dx, dfinal_g, loss_cols = _loss_bwd("loss_head", x, target, row(wf["final_g"]))

    per_layer = []
    for l in reversed(range(DEPTH)):
        lay = saved[l]
        mods, gains = lay["mods"], lay["gains"]
        dx, dwg1, dwu1, dwd1, dgn2, dmod2 = _ffn_bwd(f"b_{l}", dx, lay["x2"], lay["h2"], lay["ffn1"],
                                                     w["ffn_wg"][l, 1], w["ffn_wu"][l, 1], w["ffn_wd"][l, 1],
                                                     gains[2], mods[7], mods[8])
        dx, gm, dgn1, dmod1 = _mix_bwd(f"{l}", dx, lay["x1"], lay["h1"], lay["mix"], w["attn_wo"][l],
                                       conv_w_pad[l], row(wf["conv_ln_g"][l]), row(wf["conv_ln_b"][l]),
                                       w["conv_wo"][l], w["w_out"][l], gains[1], mods[4], mods[5], cos_t, sin_t)
        dx, dwg0, dwu0, dwd0, dgn0, dmod0 = _ffn_bwd(f"a_{l}", dx, lay["x0"], lay["h0"], lay["ffn0"],
                                                     w["ffn_wg"][l, 0], w["ffn_wu"][l, 0], w["ffn_wd"][l, 0],
                                                     gains[0], mods[1], mods[2])
        dmod = jnp.concatenate(list(dmod0) + list(dmod1) + list(dmod2), axis=1)
        g = dict(gm)
        g["ada_w"] = _mod_bwd(f"mod_bwd_{l}", c.reshape(D, 1), dmod)
        g["ada_b"] = dmod[0]
        g["norm_g"] = [dgn0[0], dgn1[0], dgn2[0]]
        g["ffn_wg"] = [dwg0, dwg1]
        g["ffn_wu"] = [dwu0, dwu1]
        g["ffn_wd"] = [dwd0, dwd1]
        for name in ("conv_b", "conv_ln_g", "conv_ln_b"):
            g[name] = g[name][0]
        per_layer.append(g)
    per_layer.reverse()
    grads = {name: [per_layer[l][name] for l in range(DEPTH)] for name in per_layer[0]}
    grads["final_g"] = dfinal_g[0]
    return loss_cols, dx, grads


def _leaves(value):
    if isinstance(value, (list, tuple)):
        return [leaf for v in value for leaf in _leaves(v)]
    return [value]


def _split_bits(w):
    bits = lax.bitcast_convert_type(w, jnp.uint32)
    hi = lax.bitcast_convert_type((bits >> 16).astype(jnp.uint16), BF16)
    lo = lax.bitcast_convert_type((bits & 0xFFFF).astype(jnp.uint16), BF16)
    return hi, lo


def _join_bits(hi, lo):
    h = lax.bitcast_convert_type(hi, jnp.uint16).astype(jnp.uint32)
    l = lax.bitcast_convert_type(lo, jnp.uint16).astype(jnp.uint32)
    return lax.bitcast_convert_type((h << 16) | l, F32)


def _pack(parts):
    flat = jnp.concatenate([p.reshape(-1) for p in parts])
    pad = PACK_ROWS * LANES - flat.shape[0]
    return jnp.concatenate([flat, jnp.zeros((pad,), flat.dtype)]).reshape(PACK_ROWS, LANES)


def _unpack(buf, shapes):
    flat = buf.reshape(-1)
    out, off = [], 0
    for shape in shapes:
        size = 1
        for s in shape:
            size *= s
        out.append(flat[off:off + size].reshape(shape))
        off += size
    return out


def _place():
    x, y, c = lax.axis_index("x"), lax.axis_index("y"), lax.axis_index("c")
    chips = [(1 - x, y), (x, 1 - y), (1 - x, 1 - y)]
    return x, y, c, chips


def _gather_weights(pack):
    def body(w_ref, out_ref, send_sems, recv_sems, local_sem):
        x, y, c, chips = _place()
        me = 2 * x + y
        sibling = (x, y, 1 - c)

        def half(chip, hf):
            return out_ref.at[chip, pl.ds(hf * HALF_ROWS, HALF_ROWS), :]

        def copy(k, chip, hf, to, src=None):
            return pltpu.make_async_remote_copy(
                src_ref=half(chip, hf) if src is None else src, dst_ref=half(chip, hf),
                send_sem=send_sems.at[k], recv_sem=recv_sems.at[k], device_id=to, device_id_type=MESH)

        mine = pltpu.make_async_copy(w_ref, out_ref.at[me], local_sem)
        mine.start()
        own = w_ref.at[pl.ds(c * HALF_ROWS, HALF_ROWS), :]
        first = [copy(j, me, c, (*chip, c), src=own) for j, chip in enumerate(chips)]
        for cp in first:
            cp.start()
        passed = [copy(3 + j, 2 * chip[0] + chip[1], c, sibling) for j, chip in enumerate(chips)]
        for j, chip in enumerate(chips):
            copy(j, 2 * chip[0] + chip[1], c, sibling).wait_recv()
            passed[j].start()
        for j, chip in enumerate(chips):
            copy(3 + j, 2 * chip[0] + chip[1], 1 - c, sibling).wait_recv()
        for cp in first + passed:
            cp.wait_send()
        mine.wait()

    return pl.pallas_call(
        body, out_shape=jax.ShapeDtypeStruct((4, PACK_ROWS, LANES), pack.dtype),
        in_specs=[pl.BlockSpec(memory_space=pltpu.HBM)], out_specs=pl.BlockSpec(memory_space=pltpu.HBM),
        scratch_shapes=[pltpu.SemaphoreType.DMA((6,)), pltpu.SemaphoreType.DMA((6,)), pltpu.SemaphoreType.DMA],
        name="gather_weights",
    )(pack)


def _swap_halves(grads):
    def body(g_ref, out_ref, send_sem, recv_sem):
        x, y, c, _ = _place()
        cp = pltpu.make_async_remote_copy(
            src_ref=g_ref.at[:, pl.ds((1 - c) * HALF_ROWS, HALF_ROWS), :], dst_ref=out_ref,
            send_sem=send_sem, recv_sem=recv_sem, device_id=(x, y, 1 - c), device_id_type=MESH)
        cp.start()
        cp.wait()

    return pl.pallas_call(
        body, out_shape=jax.ShapeDtypeStruct((4, HALF_ROWS, LANES), grads.dtype),
        in_specs=[pl.BlockSpec(memory_space=pltpu.HBM)], out_specs=pl.BlockSpec(memory_space=pltpu.HBM),
        scratch_shapes=[pltpu.SemaphoreType.DMA, pltpu.SemaphoreType.DMA], name="swap_halves",
    )(grads)


def _add_halves(grads, other):
    blocks = HALF_ROWS // SUM_ROWS

    def body(c_ref, g_ref, o_ref, out_ref):
        out_ref[...] = g_ref[...] + o_ref[...]

    c = lax.axis_index("c").astype(jnp.int32).reshape(1)
    grid_spec = pltpu.PrefetchScalarGridSpec(
        num_scalar_prefetch=1, grid=(4, blocks),
        in_specs=[pl.BlockSpec((1, SUM_ROWS, LANES), lambda k, i, c_ref: (k, c_ref[0] * blocks + i, 0)),
                  pl.BlockSpec((1, SUM_ROWS, LANES), lambda k, i, c_ref: (k, i, 0))],
        out_specs=pl.BlockSpec((1, SUM_ROWS, LANES), lambda k, i, c_ref: (k, i, 0)))
    return pl.pallas_call(
        body, out_shape=jax.ShapeDtypeStruct((4, HALF_ROWS, LANES), grads.dtype), grid_spec=grid_spec,
        name="add_halves", compiler_params=_params(("parallel", "parallel")),
    )(c, grads, other)


def _scatter_chips(part):
    def body(p_ref, out_ref, send_sems, recv_sems, local_sem):
        x, y, c, chips = _place()
        me = 2 * x + y
        mine = pltpu.make_async_copy(p_ref.at[me], out_ref.at[me], local_sem)
        mine.start()
        sends = [pltpu.make_async_remote_copy(
            src_ref=p_ref.at[2 * chip[0] + chip[1]], dst_ref=out_ref.at[me], send_sem=send_sems.at[j],
            recv_sem=recv_sems.at[j], device_id=(*chip, c), device_id_type=MESH) for j, chip in enumerate(chips)]
        for cp in sends:
            cp.start()
        for j, chip in enumerate(chips):
            there = 2 * chip[0] + chip[1]
            pltpu.make_async_remote_copy(
                src_ref=p_ref.at[there], dst_ref=out_ref.at[there], send_sem=send_sems.at[j],
                recv_sem=recv_sems.at[j], device_id=(*chip, c), device_id_type=MESH).wait_recv()
        for cp in sends:
            cp.wait_send()
        mine.wait()

    return pl.pallas_call(
        body, out_shape=jax.ShapeDtypeStruct((4, HALF_ROWS, LANES), part.dtype),
        in_specs=[pl.BlockSpec(memory_space=pltpu.HBM)], out_specs=pl.BlockSpec(memory_space=pltpu.HBM),
        scratch_shapes=[pltpu.SemaphoreType.DMA((3,)), pltpu.SemaphoreType.DMA((3,)), pltpu.SemaphoreType.DMA],
        name="scatter_chips",
    )(part)


def _add_chips(parts):
    def body(p_ref, out_ref):
        out_ref[...] = ((p_ref[0] + p_ref[1]) + p_ref[2]) + p_ref[3]

    return pl.pallas_call(
        body, out_shape=jax.ShapeDtypeStruct((HALF_ROWS, LANES), parts.dtype), grid=(HALF_ROWS // SUM_ROWS,),
        in_specs=[pl.BlockSpec((4, SUM_ROWS, LANES), lambda i: (0, i, 0))],
        out_specs=pl.BlockSpec((SUM_ROWS, LANES), lambda i: (i, 0)), name="add_chips",
        compiler_params=_params(("parallel",)),
    )(parts)


def _join_halves(mine):
    def body(m_ref, out_ref, send_sem, recv_sem, local_sem):
        x, y, c, _ = _place()
        here = out_ref.at[pl.ds(c * HALF_ROWS, HALF_ROWS), :]
        local = pltpu.make_async_copy(m_ref, here, local_sem)
        local.start()
        cp = pltpu.make_async_remote_copy(src_ref=m_ref, dst_ref=here, send_sem=send_sem, recv_sem=recv_sem,
                                          device_id=(x, y, 1 - c), device_id_type=MESH)
        cp.start()
        pltpu.make_async_remote_copy(
            src_ref=m_ref, dst_ref=out_ref.at[pl.ds((1 - c) * HALF_ROWS, HALF_ROWS), :], send_sem=send_sem,
            recv_sem=recv_sem, device_id=(x, y, 1 - c), device_id_type=MESH).wait_recv()
        cp.wait_send()
        local.wait()

    return pl.pallas_call(
        body, out_shape=jax.ShapeDtypeStruct((PACK_ROWS, LANES), mine.dtype),
        in_specs=[pl.BlockSpec(memory_space=pltpu.HBM)], out_specs=pl.BlockSpec(memory_space=pltpu.HBM),
        scratch_shapes=[pltpu.SemaphoreType.DMA, pltpu.SemaphoreType.DMA, pltpu.SemaphoreType.DMA],
        name="join_halves",
    )(mine)


def _adamw(name, w, g, m, v):
    shape = w.shape
    cols = shape[-1]
    rows = w.size // cols
    tr = rows
    for cand in (512, 256, 128, 64, 32, 16, 8):
        if rows % cand == 0 and cand * cols * 4 <= 4 * 1024 * 1024:
            tr = cand
            break

    def body(w_ref, g_ref, m_ref, v_ref, d_ref, nm_ref, nv_ref):
        gv = g_ref[...]
        nm = ADAM_B1 * m_ref[...] + (1.0 - ADAM_B1) * gv
        nv = ADAM_B2 * v_ref[...] + (1.0 - ADAM_B2) * (gv * gv)
        m_hat = nm / (1.0 - ADAM_B1 ** ADAM_STEP)
        v_hat = nv / (1.0 - ADAM_B2 ** ADAM_STEP)
        d_ref[...] = -ADAM_LR * (m_hat / (jnp.sqrt(v_hat) + ADAM_EPS) + ADAM_WD * w_ref[...])
        nm_ref[...] = nm
        nv_ref[...] = nv

    spec = pl.BlockSpec((tr, cols), lambda i: (i, 0))
    two = lambda a: a.reshape(rows, cols)
    outs = pl.pallas_call(
        body, out_shape=[jax.ShapeDtypeStruct((rows, cols), F32)] * 3, grid=(rows // tr,),
        in_specs=[spec] * 4, out_specs=[spec] * 3, name=name, compiler_params=_params(("parallel",)),
    )(two(w), two(g), two(m), two(v))
    return [o.reshape(shape) for o in outs]


def _full_shape(name, shard_shape):
    shape = list(shard_shape)
    shape[SHARD_AXIS[name]] *= 4
    return tuple(shape)


def kernel(x, c, ada_w, ada_b, norm_g, ffn_wg, ffn_wu, ffn_wd, w_in, attn_wo, conv_w, conv_b, conv_ln_g, conv_ln_b, conv_wo, w_out, final_g, loss_target, m_ada_w, m_ada_b, m_norm_g, m_ffn_wg, m_ffn_wu, m_ffn_wd, m_w_in, m_attn_wo, m_conv_w, m_conv_b, m_conv_ln_g, m_conv_ln_b, m_conv_wo, m_w_out, m_final_g, v_ada_w, v_ada_b, v_norm_g, v_ffn_wg, v_ffn_wu, v_ffn_wd, v_w_in, v_attn_wo, v_conv_w, v_conv_b, v_conv_ln_g, v_conv_ln_b, v_conv_wo, v_w_out, v_final_g):
    weights = dict(ada_w=ada_w, ada_b=ada_b, norm_g=norm_g, ffn_wg=ffn_wg, ffn_wu=ffn_wu, ffn_wd=ffn_wd, w_in=w_in,
                   attn_wo=attn_wo, conv_w=conv_w, conv_b=conv_b, conv_ln_g=conv_ln_g, conv_ln_b=conv_ln_b,
                   conv_wo=conv_wo, w_out=w_out, final_g=final_g)
    moments_m = dict(ada_w=m_ada_w, ada_b=m_ada_b, norm_g=m_norm_g, ffn_wg=m_ffn_wg, ffn_wu=m_ffn_wu,
                     ffn_wd=m_ffn_wd, w_in=m_w_in, attn_wo=m_attn_wo, conv_w=m_conv_w, conv_b=m_conv_b,
                     conv_ln_g=m_conv_ln_g, conv_ln_b=m_conv_ln_b, conv_wo=m_conv_wo, w_out=m_w_out,
                     final_g=m_final_g)
    moments_v = dict(ada_w=v_ada_w, ada_b=v_ada_b, norm_g=v_norm_g, ffn_wg=v_ffn_wg, ffn_wu=v_ffn_wu,
                     ffn_wd=v_ffn_wd, w_in=v_w_in, attn_wo=v_attn_wo, conv_w=v_conv_w, conv_b=v_conv_b,
                     conv_ln_g=v_conv_ln_g, conv_ln_b=v_conv_ln_b, conv_wo=v_conv_wo, w_out=v_w_out,
                     final_g=v_final_g)
    shard_shapes = [weights[n].shape for n in SHARDED]
    tail_shapes = [weights[n].shape for n in REPLICATED]

    sent, sent_names = [], []
    for n in SHARDED:
        if n in EXACT:
            sent += list(_split_bits(weights[n]))
            sent_names += [(n, "hi"), (n, "lo")]
        else:
            sent.append(weights[n].astype(BF16))
            sent_names.append(n)
    gathered = _gather_weights(_pack(sent))
    per_chip = [_unpack(gathered[k], [s.shape for s in sent]) for k in range(4)]
    whole = {n: jnp.concatenate([per_chip[k][i] for k in range(4)],
                                axis=SHARD_AXIS[n if isinstance(n, str) else n[0]])
             for i, n in enumerate(sent_names)}
    full = {n: whole[n] for n in SHARDED if n not in EXACT}
    vectors = dict(ada_b=ada_b, conv_b=conv_b, conv_ln_g=conv_ln_g, conv_ln_b=conv_ln_b, final_g=final_g,
                   **{n: _join_bits(whole[(n, "hi")], whole[(n, "lo")]) for n in EXACT})

    loss_cols, dx, grads = _local_step(x[0], c, loss_target[0], full, vectors)
    loss = lax.psum(jnp.sum(loss_cols), ("x", "y", "c"))

    tail = [leaf for n in REPLICATED for leaf in _leaves(grads[n])]
    blocks = []
    for k in range(4):
        parts = []
        for n, shape in zip(SHARDED, shard_shapes):
            width = shape[SHARD_AXIS[n]]
            for leaf in _leaves(grads[n]):
                ax = SHARD_AXIS[n] - (len(shape) - leaf.ndim)
                parts.append(lax.slice_in_dim(leaf, k * width, (k + 1) * width, axis=ax))
        blocks.append(_pack(parts + tail))
    stacked = jnp.stack(blocks)
    chip_sum = _add_halves(stacked, _swap_halves(stacked))
    reduced = _join_halves(_add_chips(_scatter_chips(chip_sum)))
    summed = dict(zip(SHARDED + REPLICATED, _unpack(reduced, shard_shapes + tail_shapes)))

    deltas, new_m, new_v = {}, {}, {}
    for n in WEIGHTS:
        deltas[n], new_m[n], new_v[n] = _adamw(f"adamw_{n}", weights[n], summed[n], moments_m[n], moments_v[n])

    return (loss, dx[None], *[summed[n] for n in WEIGHTS], *[deltas[n] for n in WEIGHTS],
            *[new_m[n] for n in WEIGHTS], *[new_v[n] for n in WEIGHTS])
```

```python
import functools

import jax
import jax.numpy as jnp
from jax import lax
from jax.experimental import pallas as pl
from jax.experimental.pallas import tpu as pltpu

F32 = jnp.float32
BF16 = jnp.bfloat16

D = 1024
DFF = 2816
HEAD = 64
GW = 256
DILATIONS = (1, 4, 16)
BAND = 128
QKV = 2304
CONV_K = 31
HALO = 32
N_MOD = 9
EPS = 1e-6
NEG_INF = -1e30
DEPTH = 2

LANES = 1024
HALF_ROWS = 8736
PACK_ROWS = 2 * HALF_ROWS
SUM_ROWS = 416

ADAM_LR = 0.001
ADAM_B1 = 0.9
ADAM_B2 = 0.999
ADAM_EPS = 1e-08
ADAM_WD = 0.01
ADAM_STEP = 10

VMEM_LIMIT = 56 * 1024 * 1024

SHARDED = ("ada_w", "norm_g", "ffn_wg", "ffn_wu", "ffn_wd", "w_in", "attn_wo", "conv_w", "conv_wo", "w_out")
SHARD_AXIS = {"ada_w": 2, "norm_g": 2, "ffn_wg": 3, "ffn_wu": 3, "ffn_wd": 2, "w_in": 2, "attn_wo": 2,
              "conv_w": 2, "conv_wo": 1, "w_out": 1}
EXACT = ("norm_g", "conv_w")
REPLICATED = ("ada_b", "conv_b", "conv_ln_g", "conv_ln_b", "final_g")
WEIGHTS = ("ada_w", "ada_b", "norm_g", "ffn_wg", "ffn_wu", "ffn_wd", "w_in", "attn_wo", "conv_w", "conv_b",
           "conv_ln_g", "conv_ln_b", "conv_wo", "w_out", "final_g")

MESH = pl.DeviceIdType.MESH


def _params(sem=None):
    return pltpu.CompilerParams(dimension_semantics=sem, vmem_limit_bytes=VMEM_LIMIT)


def _sigmoid(v):
    return jax.nn.sigmoid(v)


def _mm(name, a_list, b_list, pairs, epilogue, out_dtypes, *, tm, tn, n_out, trans_b=False, extras=(), vecs=(),
        out_widths=None):
    m = a_list[0].shape[0]
    na, nb, ne, nv = len(a_list), len(b_list), len(extras), len(vecs)
    dn = (((1,), (1,)), ((), ())) if trans_b else (((1,), (0,)), ((), ()))

    def body(*refs):
        a_refs = refs[:na]
        b_refs = refs[na:na + nb]
        e_refs = refs[na + nb:na + nb + ne]
        v_refs = refs[na + nb + ne:na + nb + ne + nv]
        o_refs = refs[na + nb + ne + nv:]
        accs = [lax.dot_general(a_refs[ai][...], b_refs[bi][...], dn, preferred_element_type=F32)
                for ai, bi in pairs]
        res = epilogue(accs, [e[...] for e in e_refs], [v[...] for v in v_refs])
        for o_ref, r in zip(o_refs, res):
            o_ref[...] = r.astype(o_ref.dtype)

    in_specs = [pl.BlockSpec((tm, a.shape[1]), lambda j, i: (i, 0)) for a in a_list]
    if trans_b:
        in_specs += [pl.BlockSpec((tn, b.shape[1]), lambda j, i: (j, 0)) for b in b_list]
    else:
        in_specs += [pl.BlockSpec((b.shape[0], tn), lambda j, i: (0, j)) for b in b_list]
    in_specs += [pl.BlockSpec((tm, tn), functools.partial(lambda j, i, off: (i, j + off), off=off))
                 for _, off in extras]
    in_specs += [pl.BlockSpec((1, tn), lambda j, i: (0, j)) for _ in vecs]
    widths = out_widths or [tn] * len(out_dtypes)
    out_specs = [pl.BlockSpec((tm, wd), lambda j, i: (i, j)) for wd in widths]
    out_shape = [jax.ShapeDtypeStruct((m, n_out // tn * wd), dt) for dt, wd in zip(out_dtypes, widths)]
    return pl.pallas_call(
        body, out_shape=out_shape, grid=(n_out // tn, m // tm), in_specs=in_specs, out_specs=out_specs,
        name=name, compiler_params=_params(("parallel", "parallel")),
    )(*a_list, *b_list, *[e for e, _ in extras], *vecs)


def _mm_tn(name, a, b, *, tk, tn, tt, a_fn=None):
    a_list = list(a) if a_fn is not None else [a]
    na = len(a_list)
    t, k = a_list[0].shape
    n = b.shape[1]
    steps = t // tt

    def body(*refs):
        a_refs, b_ref, o_ref, acc_ref = refs[:na], refs[na], refs[na + 1], refs[na + 2]
        s = pl.program_id(2)

        @pl.when(s == 0)
        def _():
            acc_ref[...] = jnp.zeros_like(acc_ref)

        av = a_refs[0][...] if a_fn is None else a_fn([r[...] for r in a_refs])
        acc_ref[...] += lax.dot_general(av, b_ref[...], (((0,), (0,)), ((), ())), preferred_element_type=F32)

        @pl.when(s == steps - 1)
        def _():
            o_ref[...] = acc_ref[...]

    return pl.pallas_call(
        body, out_shape=jax.ShapeDtypeStruct((k, n), F32), grid=(k // tk, n // tn, steps),
        in_specs=[pl.BlockSpec((tt, tk), lambda i, j, s: (s, i))] * na
        + [pl.BlockSpec((tt, tn), lambda i, j, s: (s, j))],
        out_specs=pl.BlockSpec((tk, tn), lambda i, j, s: (i, j)),
        scratch_shapes=[pltpu.VMEM((tk, tn), F32)], name=name,
        compiler_params=_params(("parallel", "parallel", "arbitrary")),
    )(*a_list, b)


def _first(accs, extras, vecs):
    return [accs[0]]


def _total(accs, extras, vecs):
    out = accs[0]
    for r in accs[1:]:
        out = out + r
    return [out]


def _row_spec(tm, width, col=0):
    return pl.BlockSpec((tm, width), functools.partial(lambda i, col: (i, col), col=col))


def _vec_spec(width):
    return pl.BlockSpec((1, width), lambda i: (0, 0))


def _normmod_fwd(name, x, g, scale, shift, tm=512):
    t = x.shape[0]

    def body(x_ref, g_ref, sc_ref, sh_ref, h_ref):
        xv = x_ref[...]
        r = lax.rsqrt(jnp.mean(xv * xv, axis=-1, keepdims=True) + EPS)
        h_ref[...] = ((xv * r) * g_ref[...] * (1.0 + sc_ref[...]) + sh_ref[...]).astype(BF16)

    return pl.pallas_call(
        body, out_shape=jax.ShapeDtypeStruct((t, D), BF16), grid=(t // tm,),
        in_specs=[_row_spec(tm, D), _vec_spec(D), _vec_spec(D), _vec_spec(D)], out_specs=_row_spec(tm, D),
        name=name, compiler_params=_params(("parallel",)),
    )(x, g, scale, shift)


def _normmod_bwd(name, x, dh, dres, g, scale, tm=256):
    t = x.shape[0]
    steps = t // tm

    def body(x_ref, dh_ref, dres_ref, g_ref, sc_ref, dx_ref, dg_ref, dsc_ref, dsh_ref):
        i = pl.program_id(0)

        @pl.when(i == 0)
        def _():
            dg_ref[...] = jnp.zeros_like(dg_ref)
            dsh_ref[...] = jnp.zeros_like(dsh_ref)

        xv = x_ref[...]
        dh = dh_ref[...]
        r = lax.rsqrt(jnp.mean(xv * xv, axis=-1, keepdims=True) + EPS)
        xh = xv * r
        dxh = dh * (g_ref[...] * (1.0 + sc_ref[...]))
        dx_ref[...] = dres_ref[...] + r * (dxh - xh * jnp.mean(dxh * xh, axis=-1, keepdims=True))
        dg_ref[...] += jnp.sum(dh * xh, axis=0, keepdims=True)
        dsh_ref[...] += jnp.sum(dh, axis=0, keepdims=True)

        @pl.when(i == steps - 1)
        def _():
            acc = dg_ref[...]
            dg_ref[...] = acc * (1.0 + sc_ref[...])
            dsc_ref[...] = acc * g_ref[...]

    vec = jax.ShapeDtypeStruct((1, D), F32)
    return pl.pallas_call(
        body, out_shape=[jax.ShapeDtypeStruct((t, D), F32), vec, vec, vec], grid=(steps,),
        in_specs=[_row_spec(tm, D), _row_spec(tm, D), _row_spec(tm, D), _vec_spec(D), _vec_spec(D)],
        out_specs=[_row_spec(tm, D), _vec_spec(D), _vec_spec(D), _vec_spec(D)],
        name=name, compiler_params=_params(("arbitrary",)),
    )(x, dh, dres, g, scale)


def _resgate_bwd(name, dx, f, gate, coef, tm=512):
    t = dx.shape[0]

    def body(dx_ref, f_ref, gate_ref, df_ref, dgate_ref):
        @pl.when(pl.program_id(0) == 0)
        def _():
            dgate_ref[...] = jnp.zeros_like(dgate_ref)

        dxv = dx_ref[...]
        df_ref[...] = ((coef * gate_ref[...]) * dxv).astype(BF16)
        dgate_ref[...] += jnp.sum((coef * f_ref[...].astype(F32)) * dxv, axis=0, keepdims=True)

    return pl.pallas_call(
        body, out_shape=[jax.ShapeDtypeStruct((t, D), BF16), jax.ShapeDtypeStruct((1, D), F32)], grid=(t // tm,),
        in_specs=[_row_spec(tm, D), _row_spec(tm, D), _vec_spec(D)], out_specs=[_row_spec(tm, D), _vec_spec(D)],
        name=name, compiler_params=_params(("arbitrary",)),
    )(dx, f, gate)


def _loss_bwd(name, x, target, g, tm=256):
    t = x.shape[0]

    def body(x_ref, t_ref, g_ref, dx_ref, dg_ref, loss_ref):
        @pl.when(pl.program_id(0) == 0)
        def _():
            dg_ref[...] = jnp.zeros_like(dg_ref)
            loss_ref[...] = jnp.zeros_like(loss_ref)

        xv = x_ref[...]
        r = lax.rsqrt(jnp.mean(xv * xv, axis=-1, keepdims=True) + EPS)
        xh = xv * r
        err = xh * g_ref[...] - t_ref[...]
        dy = err * (1.0 / D)
        dxh = dy * g_ref[...]
        dx_ref[...] = r * (dxh - xh * jnp.mean(dxh * xh, axis=-1, keepdims=True))
        dg_ref[...] += jnp.sum(dy * xh, axis=0, keepdims=True)
        loss_ref[...] += jnp.sum(err * err, axis=0, keepdims=True) * (0.5 / D)

    vec = jax.ShapeDtypeStruct((1, D), F32)
    return pl.pallas_call(
        body, out_shape=[jax.ShapeDtypeStruct((t, D), F32), vec, vec], grid=(t // tm,),
        in_specs=[_row_spec(tm, D), _row_spec(tm, D), _vec_spec(D)],
        out_specs=[_row_spec(tm, D), _vec_spec(D), _vec_spec(D)],
        name=name, compiler_params=_params(("arbitrary",)),
    )(x, target, g)


def _mod_fwd(name, c8, ada_w, ada_b, tn=2304):
    n = ada_w.shape[1]

    def body(c_ref, w_ref, b_ref, o_ref):
        cv = c_ref[...]
        ca = (cv * _sigmoid(cv)).astype(BF16)
        o_ref[...] = jnp.dot(ca, w_ref[...], preferred_element_type=F32) + b_ref[...]

    return pl.pallas_call(
        body, out_shape=jax.ShapeDtypeStruct((8, n), F32), grid=(n // tn,),
        in_specs=[pl.BlockSpec((8, D), lambda j: (0, 0)), pl.BlockSpec((D, tn), lambda j: (0, j)),
                  pl.BlockSpec((1, tn), lambda j: (0, j))],
        out_specs=pl.BlockSpec((8, tn), lambda j: (0, j)), name=name, compiler_params=_params(("parallel",)),
    )(c8, ada_w, ada_b)


def _mod_bwd(name, c_col, dmod, tk=256):
    n = dmod.shape[1]

    def body(c_ref, d_ref, o_ref):
        cv = c_ref[...]
        o_ref[...] = (cv * _sigmoid(cv)) * d_ref[...]

    return pl.pallas_call(
        body, out_shape=jax.ShapeDtypeStruct((D, n), F32), grid=(D // tk,),
        in_specs=[pl.BlockSpec((tk, 1), lambda i: (i, 0)), pl.BlockSpec((1, n), lambda i: (0, 0))],
        out_specs=pl.BlockSpec((tk, n), lambda i: (i, 0)), name=name, compiler_params=_params(("parallel",)),
    )(c_col, dmod)


def _rope_tables(t):
    half = HEAD // 2
    inv_freq = 10000.0 ** (-(jnp.arange(half, dtype=F32) * 2.0 / HEAD))
    ang = jnp.arange(t, dtype=F32)[:, None] * inv_freq[None, :]
    cos, sin = jnp.cos(ang), jnp.sin(ang)
    cos_t = jnp.tile(jnp.concatenate([cos, cos], axis=1), (1, GW // HEAD))
    sin_t = jnp.tile(jnp.concatenate([-sin, sin], axis=1), (1, GW // HEAD))
    return cos_t, sin_t


def _rotate(tv, cos, sin_signed):
    lane = lax.broadcasted_iota(jnp.int32, tv.shape, 1)
    first = (lane % HEAD) < (HEAD // 2)
    partner = jnp.where(first, pltpu.roll(tv, GW - HEAD // 2, 1), pltpu.roll(tv, HEAD // 2, 1))
    return tv * cos + partner * sin_signed


def _rope_fwd(name, zqkv, cos_t, sin_t, tm=512):
    t = zqkv.shape[0]
    ng = len(DILATIONS)
    n = 3 * ng

    def body(*refs):
        z_refs, cos_ref, sin_ref, o_refs = refs[:n], refs[n], refs[n + 1], refs[n + 2:]
        cos, sin = cos_ref[...], sin_ref[...]
        for idx in range(n):
            r = z_refs[idx][...]
            if idx < 2 * ng:
                r = _rotate(r, cos, sin)
            if idx < ng:
                r = r * (HEAD ** -0.5)
            o_refs[idx][...] = r.astype(BF16)

    return pl.pallas_call(
        body, out_shape=[jax.ShapeDtypeStruct((t, GW), BF16)] * n, grid=(t // tm,),
        in_specs=[_row_spec(tm, GW, col) for col in range(n)] + [_row_spec(tm, GW), _row_spec(tm, GW)],
        out_specs=[_row_spec(tm, GW)] * n, name=name, compiler_params=_params(("parallel",)),
    )(*([zqkv] * n), cos_t, sin_t)


def _rope_bwd(name, grads, cos_t, sin_t, tm=512):
    t = grads[0].shape[0]
    ng = len(DILATIONS)
    n = len(grads)

    def body(*refs):
        g_refs, cos_ref, sin_ref, o_ref = refs[:n], refs[n], refs[n + 1], refs[n + 2]
        cos, sin = cos_ref[...], -sin_ref[...]
        for idx in range(n):
            r = g_refs[idx][...]
            if idx < 2 * ng:
                r = _rotate(r, cos, sin)
            if idx < ng:
                r = r * (HEAD ** -0.5)
            o_ref[:, idx * GW:(idx + 1) * GW] = r.astype(BF16)

    return pl.pallas_call(
        body, out_shape=jax.ShapeDtypeStruct((t, n * GW), BF16), grid=(t // tm,),
        in_specs=[_row_spec(tm, GW)] * (n + 2), out_specs=_row_spec(tm, n * GW),
        name=name, compiler_params=_params(("parallel",)),
    )(*grads, cos_t, sin_t)


def _head_cols(h):
    return slice(h * HEAD, (h + 1) * HEAD)


def _band_mask_q(has_prev):
    qi = lax.broadcasted_iota(jnp.int32, (BAND, 2 * BAND), 0)
    kj = lax.broadcasted_iota(jnp.int32, (BAND, 2 * BAND), 1)
    dist = qi + BAND - kj
    return (dist >= 0) & (dist <= BAND) & ((kj >= BAND) | has_prev)


def _attn_fwd(name, q, k, v, group):
    t = q.shape[0]
    d = DILATIONS[group]
    length = t // d
    qb = min(512, length)
    sub = qb // BAND
    nblk = length // qb

    def body(q_ref, kc_ref, kp_ref, vc_ref, vp_ref, o_ref, lse_ref):
        blk = pl.program_id(1)
        k_ext = jnp.concatenate([kp_ref[...], kc_ref[...]], axis=0)
        v_ext = jnp.concatenate([vp_ref[...], vc_ref[...]], axis=0)
        for j in range(sub):
            mask = _band_mask_q((blk * sub + j) > 0)
            qj = q_ref[j * BAND:(j + 1) * BAND, :]
            kj = k_ext[j * BAND:(j + 2) * BAND, :]
            vj = v_ext[j * BAND:(j + 2) * BAND, :]
            outs, lses = [], []
            for h in range(GW // HEAD):
                s = lax.dot_general(qj[:, _head_cols(h)], kj[:, _head_cols(h)], (((1,), (1,)), ((), ())),
                                    preferred_element_type=F32)
                s = jnp.where(mask, s, NEG_INF)
                m = jnp.max(s, axis=-1, keepdims=True)
                p = jnp.exp(s - m)
                den = jnp.sum(p, axis=-1, keepdims=True)
                o = jnp.dot(p.astype(BF16), vj[:, _head_cols(h)], preferred_element_type=F32)
                outs.append(o / den)
                lses.append(jnp.broadcast_to(m + jnp.log(den), (BAND, HEAD)))
            o_ref[j * BAND:(j + 1) * BAND, :] = jnp.concatenate(outs, axis=1)
            lse_ref[j * BAND:(j + 1) * BAND, :] = jnp.concatenate(lses, axis=1)

    prev = qb // BAND
    cur = lambda r, b: (b, r)
    before = lambda r, b: (jnp.maximum(b * prev - 1, 0), r)
    big, halo = pl.BlockSpec((qb, GW), cur), pl.BlockSpec((BAND, GW), before)
    view = lambda a: a.reshape(length, d * GW)
    out = pl.pallas_call(
        body, out_shape=[jax.ShapeDtypeStruct((length, d * GW), F32)] * 2, grid=(d, nblk),
        in_specs=[big, big, halo, big, halo], out_specs=[big] * 2, name=name,
        compiler_params=_params(("parallel", "parallel")),
    )(view(q), view(k), view(k), view(v), view(v))
    return out[0].reshape(t, GW), out[1].reshape(t, GW)


def _attn_merge(name, outs, lses, tm=512):
    t = outs[0].shape[0]
    n = len(outs)

    def body(*refs):
        o_refs, l_refs = refs[:n], refs[n:2 * n]
        ob_ref, of_ref, lj_ref = refs[2 * n:]
        ls = [r[...] for r in l_refs]
        m = ls[0]
        for v in ls[1:]:
            m = jnp.maximum(m, v)
        es = [jnp.exp(v - m) for v in ls]
        tot = es[0]
        for v in es[1:]:
            tot = tot + v
        acc = (es[0] / tot) * o_refs[0][...]
        for e, o_ref in zip(es[1:], o_refs[1:]):
            acc = acc + (e / tot) * o_ref[...]
        ob_ref[...] = acc.astype(BF16)
        of_ref[...] = acc
        lj_ref[...] = m + jnp.log(tot)

    return pl.pallas_call(
        body, out_shape=[jax.ShapeDtypeStruct((t, GW), BF16), jax.ShapeDtypeStruct((t, GW), F32),
                         jax.ShapeDtypeStruct((t, GW), F32)], grid=(t // tm,),
        in_specs=[_row_spec(tm, GW)] * (2 * n), out_specs=[_row_spec(tm, GW)] * 3,
        name=name, compiler_params=_params(("parallel",)),
    )(*outs, *lses)


def _attn_bwd_prep(name, do, o, tm=512):
    t = do.shape[0]

    def body(do_ref, o_ref, dsum_ref, dob_ref):
        dov = do_ref[...]
        prod = dov * o_ref[...]
        parts = [jnp.broadcast_to(jnp.sum(prod[:, _head_cols(h)], axis=-1, keepdims=True), (tm, HEAD))
                 for h in range(GW // HEAD)]
        dsum_ref[...] = jnp.concatenate(parts, axis=1)
        dob_ref[...] = dov.astype(BF16)

    return pl.pallas_call(
        body, out_shape=[jax.ShapeDtypeStruct((t, GW), F32), jax.ShapeDtypeStruct((t, GW), BF16)], grid=(t // tm,),
        in_specs=[_row_spec(tm, GW)] * 2, out_specs=[_row_spec(tm, GW)] * 2,
        name=name, compiler_params=_params(("parallel",)),
    )(do, o)


def _attn_bwd(name, q, k, v, do, lj, dsum, group):
    t = q.shape[0]
    d = DILATIONS[group]
    length = t // d
    qb = min(512, length)
    sub = qb // BAND
    nblk = length // qb
    total = length // BAND

    def body(qc_ref, qn_ref, kc_ref, kp_ref, vc_ref, vp_ref, doc_ref, don_ref, ljc_ref, ljn_ref, dsc_ref, dsn_ref,
             dq_ref, dk_ref, dv_ref):
        blk = pl.program_id(1)
        q_ext = jnp.concatenate([qc_ref[...], qn_ref[...]], axis=0)
        do_ext = jnp.concatenate([doc_ref[...], don_ref[...]], axis=0)
        lj_ext = jnp.concatenate([ljc_ref[...], ljn_ref[...]], axis=0)
        ds_ext = jnp.concatenate([dsc_ref[...], dsn_ref[...]], axis=0)
        k_ext = jnp.concatenate([kp_ref[...], kc_ref[...]], axis=0)
        v_ext = jnp.concatenate([vp_ref[...], vc_ref[...]], axis=0)
        qi2 = lax.broadcasted_iota(jnp.int32, (2 * BAND, BAND), 0)
        kj2 = lax.broadcasted_iota(jnp.int32, (2 * BAND, BAND), 1)
        for j in range(sub):
            gblk = blk * sub + j
            rows = slice(j * BAND, (j + 1) * BAND)
            rows2 = slice(j * BAND, (j + 2) * BAND)
            mask_q = _band_mask_q(gblk > 0)
            mask_k = ((qi2 < BAND) & (kj2 <= qi2)) | ((qi2 >= BAND) & (kj2 >= qi2 - BAND) & (gblk + 1 < total))
            dqs, dks, dvs = [], [], []
            for h in range(GW // HEAD):
                hc = _head_cols(h)
                col = slice(h * HEAD, h * HEAD + 1)
                qh = q_ext[rows, hc]
                kh2 = k_ext[rows2, hc]
                s = lax.dot_general(qh, kh2, (((1,), (1,)), ((), ())), preferred_element_type=F32)
                p = jnp.where(mask_q, jnp.exp(s - lj_ext[rows, col]), 0.0)
                dp = lax.dot_general(do_ext[rows, hc], v_ext[rows2, hc], (((1,), (1,)), ((), ())),
                                     preferred_element_type=F32)
                dsc = p * (dp - ds_ext[rows, col])
                dqs.append(jnp.dot(dsc.astype(BF16), kh2, preferred_element_type=F32))
                qh2 = q_ext[rows2, hc]
                kh = k_ext[BAND + j * BAND:BAND + (j + 1) * BAND, hc]
                vh = v_ext[BAND + j * BAND:BAND + (j + 1) * BAND, hc]
                doh2 = do_ext[rows2, hc]
                s2 = lax.dot_general(qh2, kh, (((1,), (1,)), ((), ())), preferred_element_type=F32)
                p2 = jnp.where(mask_k, jnp.exp(s2 - lj_ext[rows2, col]), 0.0)
                dp2 = lax.dot_general(doh2, vh, (((1,), (1,)), ((), ())), preferred_element_type=F32)
                ds2 = p2 * (dp2 - ds_ext[rows2, col])
                dvs.append(lax.dot_general(p2.astype(BF16), doh2, (((0,), (0,)), ((), ())),
                                           preferred_element_type=F32))
                dks.append(lax.dot_general(ds2.astype(BF16), qh2, (((0,), (0,)), ((), ())),
                                           preferred_element_type=F32))
            dq_ref[rows, :] = jnp.concatenate(dqs, axis=1)
            dk_ref[rows, :] = jnp.concatenate(dks, axis=1)
            dv_ref[rows, :] = jnp.concatenate(dvs, axis=1).astype(BF16)

    prev = qb // BAND
    cur = lambda r, b: (b, r)
    before = lambda r, b: (jnp.maximum(b * prev - 1, 0), r)
    after = lambda r, b: (jnp.minimum((b + 1) * prev, total - 1), r)
    big = pl.BlockSpec((qb, GW), cur)
    nxt = pl.BlockSpec((BAND, GW), after)
    prv = pl.BlockSpec((BAND, GW), before)
    view = lambda a: a.reshape(length, d * GW)
    out = pl.pallas_call(
        body, out_shape=[jax.ShapeDtypeStruct((length, d * GW), F32), jax.ShapeDtypeStruct((length, d * GW), F32),
                         jax.ShapeDtypeStruct((length, d * GW), BF16)], grid=(d, nblk),
        in_specs=[big, nxt, big, prv, big, prv, big, nxt, big, nxt, big, nxt],
        out_specs=[big] * 3, name=name, compiler_params=_params(("parallel", "parallel")),
    )(view(q), view(q), view(k), view(k), view(v), view(v), view(do), view(do), view(lj), view(lj), view(dsum),
      view(dsum))
    return out[0].reshape(t, GW), out[1].reshape(t, GW), out[2].reshape(t, GW)


def _conv_fwd(name, zu, conv_w, conv_b, ln_g, ln_b, tm=256):
    t = zu.shape[0]
    per = tm // HALO

    def body(a_ref, gl_ref, ah_ref, glh_ref, w_ref, b_ref, g_ref, beta_ref, hc_ref, s_ref, ext_ref):
        i = pl.program_id(0)
        halo = ah_ref[...] * _sigmoid(glh_ref[...])
        ext_ref[0:HALO, :] = jnp.where(i > 0, halo, 0.0)
        ext_ref[HALO:, :] = a_ref[...] * _sigmoid(gl_ref[...])
        acc = jnp.zeros((tm, D), F32) + b_ref[...]
        for kk in range(CONV_K):
            acc = acc + w_ref[kk:kk + 1, :] * ext_ref[HALO - CONV_K + 1 + kk:HALO - CONV_K + 1 + kk + tm, :]
        hc_ref[...] = acc
        mu = jnp.mean(acc, axis=-1, keepdims=True)
        xc = acc - mu
        var = jnp.mean(xc * xc, axis=-1, keepdims=True)
        ln = xc * lax.rsqrt(var + EPS) * g_ref[...] + beta_ref[...]
        s_ref[...] = (ln * _sigmoid(ln)).astype(BF16)

    halo_map = lambda col: (lambda i: (jnp.maximum(i * per - 1, 0), col))
    return pl.pallas_call(
        body, out_shape=[jax.ShapeDtypeStruct((t, D), F32), jax.ShapeDtypeStruct((t, D), BF16)], grid=(t // tm,),
        in_specs=[_row_spec(tm, D, 0), _row_spec(tm, D, 1), pl.BlockSpec((HALO, D), halo_map(0)),
                  pl.BlockSpec((HALO, D), halo_map(1)), pl.BlockSpec((HALO, D), lambda i: (0, 0)),
                  _vec_spec(D), _vec_spec(D), _vec_spec(D)],
        out_specs=[_row_spec(tm, D), _row_spec(tm, D)], scratch_shapes=[pltpu.VMEM((tm + HALO, D), F32)],
        name=name, compiler_params=_params(("parallel",)),
    )(zu, zu, zu, zu, conv_w, conv_b, ln_g, ln_b)


def _conv_ln_bwd(name, hc, ds, ln_g, ln_b, tm=256):
    t = hc.shape[0]

    def body(hc_ref, ds_ref, g_ref, beta_ref, dhc_ref, dg_ref, dbeta_ref, dbias_ref):
        @pl.when(pl.program_id(0) == 0)
        def _():
            dg_ref[...] = jnp.zeros_like(dg_ref)
            dbeta_ref[...] = jnp.zeros_like(dbeta_ref)
            dbias_ref[...] = jnp.zeros_like(dbias_ref)

        hv = hc_ref[...]
        mu = jnp.mean(hv, axis=-1, keepdims=True)
        xc = hv - mu
        rstd = lax.rsqrt(jnp.mean(xc * xc, axis=-1, keepdims=True) + EPS)
        xh = xc * rstd
        ln = xh * g_ref[...] + beta_ref[...]
        sg = _sigmoid(ln)
        dln = ds_ref[...] * (sg * (1.0 + ln * (1.0 - sg)))
        dxh = dln * g_ref[...]
        dh = rstd * (dxh - jnp.mean(dxh, axis=-1, keepdims=True) - xh * jnp.mean(dxh * xh, axis=-1, keepdims=True))
        dhc_ref[...] = dh
        dg_ref[...] += jnp.sum(dln * xh, axis=0, keepdims=True)
        dbeta_ref[...] += jnp.sum(dln, axis=0, keepdims=True)
        dbias_ref[...] += jnp.sum(dh, axis=0, keepdims=True)

    vec = jax.ShapeDtypeStruct((1, D), F32)
    return pl.pallas_call(
        body, out_shape=[jax.ShapeDtypeStruct((t, D), F32), vec, vec, vec], grid=(t // tm,),
        in_specs=[_row_spec(tm, D), _row_spec(tm, D), _vec_spec(D), _vec_spec(D)],
        out_specs=[_row_spec(tm, D), _vec_spec(D), _vec_spec(D), _vec_spec(D)],
        name=name, compiler_params=_params(("arbitrary",)),
    )(hc, ds, ln_g, ln_b)


def _conv_bwd(name, zu, dhc, conv_w, tm=256):
    t = zu.shape[0]
    per = tm // HALO
    steps = t // tm

    def body(a_ref, gl_ref, ah_ref, glh_ref, d_ref, dn_ref, w_ref, dz_ref, dw_ref, ext_ref, dext_ref):
        i = pl.program_id(0)

        @pl.when(i == 0)
        def _():
            dw_ref[...] = jnp.zeros_like(dw_ref)

        av, sg = a_ref[...], _sigmoid(gl_ref[...])
        ext_ref[0:HALO, :] = jnp.where(i > 0, ah_ref[...] * _sigmoid(glh_ref[...]), 0.0)
        ext_ref[HALO:, :] = av * sg
        dv = d_ref[...]
        dext_ref[0:tm, :] = dv
        dext_ref[tm:, :] = jnp.where(i < steps - 1, dn_ref[...], 0.0)
        acc = jnp.zeros((tm, D), F32)
        for kk in range(CONV_K):
            acc = acc + w_ref[kk:kk + 1, :] * dext_ref[CONV_K - 1 - kk:CONV_K - 1 - kk + tm, :]
            prod = dv * ext_ref[HALO - CONV_K + 1 + kk:HALO - CONV_K + 1 + kk + tm, :]
            dw_ref[kk:kk + 1, :] += jnp.sum(prod, axis=0, keepdims=True)
        dz_ref[:, 0:D] = (acc * sg).astype(BF16)
        dz_ref[:, D:] = (acc * av * sg * (1.0 - sg)).astype(BF16)

    halo_map = lambda col: (lambda i: (jnp.maximum(i * per - 1, 0), col))
    return pl.pallas_call(
        body, out_shape=[jax.ShapeDtypeStruct((t, 2 * D), BF16), jax.ShapeDtypeStruct((HALO, D), F32)],
        grid=(steps,),
        in_specs=[_row_spec(tm, D, 0), _row_spec(tm, D, 1), pl.BlockSpec((HALO, D), halo_map(0)),
                  pl.BlockSpec((HALO, D), halo_map(1)), _row_spec(tm, D),
                  pl.BlockSpec((HALO, D), lambda i: (jnp.minimum((i + 1) * per, t // HALO - 1), 0)),
                  pl.BlockSpec((HALO, D), lambda i: (0, 0))],
        out_specs=[_row_spec(tm, 2 * D), pl.BlockSpec((HALO, D), lambda i: (0, 0))],
        scratch_shapes=[pltpu.VMEM((tm + HALO, D), F32), pltpu.VMEM((tm + HALO, D), F32)],
        name=name, compiler_params=_params(("arbitrary",)),
    )(zu, zu, zu, zu, dhc, dhc, conv_w)


def _ffn_fwd(tag, x, h, wg, wu, wd, gate):
    def act(accs, extras, vecs):
        g, u = accs
        return [g, u, (g * _sigmoid(g)) * u]

    gv, uv, av = _mm(f"ffn_up_{tag}", [h], [wg, wu], [(0, 0), (0, 1)], act, [BF16, BF16, BF16],
                     tm=512, tn=1408, n_out=DFF)

    def residual(accs, extras, vecs):
        return [extras[0] + (0.5 * vecs[0]) * accs[0], accs[0]]

    x_new, f = _mm(f"ffn_down_{tag}", [av], [wd], [(0, 0)], residual, [F32, BF16], tm=512, tn=512, n_out=D,
                   extras=[(x, 0)], vecs=[gate])
    return x_new, (gv, uv, f)


def _ffn_bwd(tag, dx, x, h, saved, wg, wu, wd, g, scale, gate):
    gv, uv, f = saved
    df, dgate = _resgate_bwd(f"ffn_gate_bwd_{tag}", dx, f, gate, 0.5)

    def act(blocks):
        gf, uf = blocks[0].astype(F32), blocks[1].astype(F32)
        return ((gf * _sigmoid(gf)) * uf).astype(BF16)

    dwd = _mm_tn(f"ffn_dwd_{tag}", [gv, uv], df, tk=1408, tn=1024, tt=512, a_fn=act)

    def act_bwd(accs, extras, vecs):
        da = accs[0]
        gf, uf = extras[0].astype(F32), extras[1].astype(F32)
        sg = _sigmoid(gf)
        return [da * uf * (sg * (1.0 + gf * (1.0 - sg))), da * (gf * sg)]

    dg, du = _mm(f"ffn_da_{tag}", [df], [wd], [(0, 0)], act_bwd, [BF16, BF16], tm=512, tn=1408, n_out=DFF,
                 trans_b=True, extras=[(gv, 0), (uv, 0)])
    dwg = _mm_tn(f"ffn_dwg_{tag}", h, dg, tk=1024, tn=1408, tt=512)
    dwu = _mm_tn(f"ffn_dwu_{tag}", h, du, tk=1024, tn=1408, tt=512)
    dh, = _mm(f"ffn_dh_{tag}", [dg, du], [wg, wu], [(0, 0), (1, 1)], _total, [F32], tm=512, tn=512, n_out=D,
              trans_b=True)
    dx_in, dgn, dscale, dshift = _normmod_bwd(f"ffn_norm_bwd_{tag}", x, dh, dx, g, scale)
    return dx_in, dwg, dwu, dwd, dgn, (dshift, dscale, dgate)


def _mix_fwd(tag, x, h, w_in, attn_wo, conv_w, conv_b, ln_g, ln_b, conv_wo, w_out, gate, cos_t, sin_t):
    w_qkv, w_u, w_g = w_in[:, :QKV], w_in[:, QKV:QKV + 2 * D], w_in[:, QKV + 2 * D:]
    zqkv, = _mm(f"mix_qkv_{tag}", [h], [w_qkv], [(0, 0)], _first, [F32], tm=512, tn=768, n_out=QKV)
    zu, = _mm(f"mix_u_{tag}", [h], [w_u], [(0, 0)], _first, [F32], tm=512, tn=1024, n_out=2 * D)
    zg, = _mm(f"mix_g_{tag}", [h], [w_g], [(0, 0)], _first, [BF16], tm=512, tn=1024, n_out=2 * D)
    qkv = _rope_fwd(f"rope_{tag}", zqkv, cos_t, sin_t)
    n = len(DILATIONS)
    outs, lses = [], []
    for grp in range(n):
        o, lse = _attn_fwd(f"attn_fwd_{tag}_{grp}", qkv[grp], qkv[n + grp], qkv[2 * n + grp], grp)
        outs.append(o)
        lses.append(lse)
    ob, of, lj = _attn_merge(f"attn_merge_{tag}", outs, lses)
    hc, s = _conv_fwd(f"conv_fwd_{tag}", zu, conv_w, conv_b, ln_g, ln_b)

    def gated(accs, extras, vecs):
        ya, yc = accs
        return [_sigmoid(extras[0].astype(F32)) * ya + _sigmoid(extras[1].astype(F32)) * yc, ya, yc]

    y, ya, yc = _mm(f"mix_y_{tag}", [ob, s], [attn_wo, conv_wo], [(0, 0), (1, 1)], gated, [BF16, BF16, BF16],
                    tm=512, tn=1024, n_out=D, extras=[(zg, 0), (zg, 1)])

    def residual(accs, extras, vecs):
        return [extras[0] + vecs[0] * accs[0], accs[0]]

    x_new, f = _mm(f"mix_out_{tag}", [y], [w_out], [(0, 0)], residual, [F32, BF16], tm=512, tn=512, n_out=D,
                   extras=[(x, 0)], vecs=[gate])
    return x_new, (zu, zg, qkv, ob, of, lj, hc, s, y, ya, yc, f, (w_qkv, w_u, w_g))


def _mix_bwd(tag, dx, x, h, saved, attn_wo, conv_w, ln_g, ln_b, conv_wo, w_out, g, scale, gate, cos_t, sin_t):
    zu, zg, qkv, ob, of, lj, hc, s, y, ya, yc, f, w_parts = saved
    n = len(DILATIONS)
    df, dgate = _resgate_bwd(f"mix_gate_bwd_{tag}", dx, f, gate, 1.0)
    dw_out = _mm_tn(f"mix_dwout_{tag}", y, df, tk=1024, tn=1024, tt=512)

    def gated_bwd(accs, extras, vecs):
        dy = accs[0]
        sa, sc = _sigmoid(extras[0].astype(F32)), _sigmoid(extras[1].astype(F32))
        dga = dy * extras[2].astype(F32) * (sa * (1.0 - sa))
        dgc = dy * extras[3].astype(F32) * (sc * (1.0 - sc))
        return [dy * sa, dy * sc, jnp.concatenate([dga, dgc], axis=1)]

    dya, dyc, dzg = _mm(f"mix_dy_{tag}", [df], [w_out], [(0, 0)], gated_bwd, [BF16] * 3, tm=512, tn=1024,
                        n_out=D, trans_b=True, extras=[(zg, 0), (zg, 1), (ya, 0), (yc, 0)],
                        out_widths=[D, D, 2 * D])
    dw_attn = _mm_tn(f"mix_dwattn_{tag}", ob, dya, tk=GW, tn=1024, tt=512)
    dw_conv_o = _mm_tn(f"mix_dwconvo_{tag}", s, dyc, tk=1024, tn=1024, tt=512)
    do, = _mm(f"mix_do_{tag}", [dya], [attn_wo], [(0, 0)], _first, [F32], tm=512, tn=GW, n_out=GW, trans_b=True)
    ds, = _mm(f"mix_ds_{tag}", [dyc], [conv_wo], [(0, 0)], _first, [F32], tm=512, tn=1024, n_out=D, trans_b=True)

    dsum, dob = _attn_bwd_prep(f"attn_prep_{tag}", do, of)
    dqs, dks, dvs = [], [], []
    for grp in range(n):
        dq, dk, dv = _attn_bwd(f"attn_bwd_{tag}_{grp}", qkv[grp], qkv[n + grp], qkv[2 * n + grp], dob, lj, dsum,
                               grp)
        dqs.append(dq)
        dks.append(dk)
        dvs.append(dv)
    dzqkv = _rope_bwd(f"rope_bwd_{tag}", dqs + dks + dvs, cos_t, sin_t)

    dhc, dln_g, dln_b, dconv_b = _conv_ln_bwd(f"conv_ln_bwd_{tag}", hc, ds, ln_g, ln_b)
    dzu, dconv_w = _conv_bwd(f"conv_bwd_{tag}", zu, dhc, conv_w)

    dz_parts = [dzqkv, dzu, dzg]
    dw_in = jnp.concatenate(
        [_mm_tn(f"mix_dwin_{tag}_{i}", h, dzp, tk=1024, tn=dzp.shape[1] // 2, tt=512)
         for i, dzp in enumerate(dz_parts)], axis=1)
    dh, = _mm(f"mix_dh_{tag}", dz_parts, list(w_parts), [(0, 0), (1, 1), (2, 2)], _total, [F32], tm=512, tn=512,
              n_out=D, trans_b=True)
    dx_in, dgn, dscale, dshift = _normmod_bwd(f"mix_norm_bwd_{tag}", x, dh, dx, g, scale)
    grads = dict(w_in=dw_in, attn_wo=dw_attn, conv_w=dconv_w[:CONV_K], conv_b=dconv_b, conv_ln_g=dln_g,
                 conv_ln_b=dln_b, conv_wo=dw_conv_o, w_out=dw_out)
    return dx_in, grads, dgn, (dshift, dscale, dgate)


def _local_step(x, c, target, w, wf):
    t = x.shape[0]
    cos_t, sin_t = _rope_tables(t)
    c8 = jnp.concatenate([c, jnp.zeros((7, D), F32)], axis=0)
    row = lambda v: v.reshape(1, -1)
    conv_w_pad = jnp.concatenate([wf["conv_w"], jnp.zeros((DEPTH, HALO - CONV_K, D), F32)], axis=1)

    saved = []
    for l in range(DEPTH):
        mod = _mod_fwd(f"mod_{l}", c8, w["ada_w"][l], row(wf["ada_b"][l]))[0:1]
        mods = [mod[:, i * D:(i + 1) * D] for i in range(N_MOD)]
        gains = [row(wf["norm_g"][l, i]) for i in range(3)]
        lay = dict(mods=mods, gains=gains)

        lay["x0"] = x
        lay["h0"] = _normmod_fwd(f"norm_a_{l}", x, gains[0], mods[1], mods[0])
        x, lay["ffn0"] = _ffn_fwd(f"a_{l}", x, lay["h0"], w["ffn_wg"][l, 0], w["ffn_wu"][l, 0], w["ffn_wd"][l, 0],
                                  mods[2])
        lay["x1"] = x
        lay["h1"] = _normmod_fwd(f"norm_m_{l}", x, gains[1], mods[4], mods[3])
        x, lay["mix"] = _mix_fwd(f"{l}", x, lay["h1"], w["w_in"][l], w["attn_wo"][l], conv_w_pad[l],
                                 row(wf["conv_b"][l]), row(wf["conv_ln_g"][l]), row(wf["conv_ln_b"][l]),
                                 w["conv_wo"][l], w["w_out"][l], mods[5], cos_t, sin_t)
        lay["x2"] = x
        lay["h2"] = _normmod_fwd(f"norm_b_{l}", x, gains[2], mods[7], mods[6])
        x, lay["ffn1"] = _ffn_fwd(f"b_{l}", x, lay["h2"], w["ffn_wg"][l, 1], w["ffn_wu"][l, 1], w["ffn_wd"][l, 1],
                                  mods[8])
        saved.append(lay)

    dx, dfinal_g, loss_cols = _loss_bwd("loss_head", x, target, row(wf["final_g"]))

    per_layer = []
    for l in reversed(range(DEPTH)):
        lay = saved[l]
        mods, gains = lay["mods"], lay["gains"]
        dx, dwg1, dwu1, dwd1, dgn2, dmod2 = _ffn_bwd(f"b_{l}", dx, lay["x2"], lay["h2"], lay["ffn1"],
                                                     w["ffn_wg"][l, 1], w["ffn_wu"][l, 1], w["ffn_wd"][l, 1],
                                                     gains[2], mods[7], mods[8])
        dx, gm, dgn1, dmod1 = _mix_bwd(f"{l}", dx, lay["x1"], lay["h1"], lay["mix"], w["attn_wo"][l],
                                       conv_w_pad[l], row(wf["conv_ln_g"][l]), row(wf["conv_ln_b"][l]),
                                       w["conv_wo"][l], w["w_out"][l], gains[1], mods[4], mods[5], cos_t, sin_t)
        dx, dwg0, dwu0, dwd0, dgn0, dmod0 = _ffn_bwd(f"a_{l}", dx, lay["x0"], lay["h0"], lay["ffn0"],
                                                     w["ffn_wg"][l, 0], w["ffn_wu"][l, 0], w["ffn_wd"][l, 0],
                                                     gains[0], mods[1], mods[2])
        dmod = jnp.concatenate(list(dmod0) + list(dmod1) + list(dmod2), axis=1)
        g = dict(gm)
        g["ada_w"] = _mod_bwd(f"mod_bwd_{l}", c.reshape(D, 1), dmod)
        g["ada_b"] = dmod[0]
        g["norm_g"] = [dgn0[0], dgn1[0], dgn2[0]]
        g["ffn_wg"] = [dwg0, dwg1]
        g["ffn_wu"] = [dwu0, dwu1]
        g["ffn_wd"] = [dwd0, dwd1]
        for name in ("conv_b", "conv_ln_g", "conv_ln_b"):
            g[name] = g[name][0]
        per_layer.append(g)
    per_layer.reverse()
    grads = {name: [per_layer[l][name] for l in range(DEPTH)] for name in per_layer[0]}
    grads["final_g"] = dfinal_g[0]
    return loss_cols, dx, grads


def _leaves(value):
    if isinstance(value, (list, tuple)):
        return [leaf for v in value for leaf in _leaves(v)]
    return [value]


def _split_bits(w):
    bits = lax.bitcast_convert_type(w, jnp.uint32)
    hi = lax.bitcast_convert_type((bits >> 16).astype(jnp.uint16), BF16)
    lo = lax.bitcast_convert_type((bits & 0xFFFF).astype(jnp.uint16), BF16)
    return hi, lo


def _join_bits(hi, lo):
    h = lax.bitcast_convert_type(hi, jnp.uint16).astype(jnp.uint32)
    l = lax.bitcast_convert_type(lo, jnp.uint16).astype(jnp.uint32)
    return lax.bitcast_convert_type((h << 16) | l, F32)


def _pack(parts):
    flat = jnp.concatenate([p.reshape(-1) for p in parts])
    pad = PACK_ROWS * LANES - flat.shape[0]
    return jnp.concatenate([flat, jnp.zeros((pad,), flat.dtype)]).reshape(PACK_ROWS, LANES)


def _unpack(buf, shapes):
    flat = buf.reshape(-1)
    out, off = [], 0
    for shape in shapes:
        size = 1
        for s in shape:
            size *= s
        out.append(flat[off:off + size].reshape(shape))
        off += size
    return out


def _place():
    x, y, c = lax.axis_index("x"), lax.axis_index("y"), lax.axis_index("c")
    chips = [(1 - x, y), (x, 1 - y), (1 - x, 1 - y)]
    return x, y, c, chips


def _chip_index():
    return (2 * lax.axis_index("x") + lax.axis_index("y")).astype(jnp.int32)


def _gather_weights(slots):
    def body(in_ref, out_ref, send_sems, recv_sems):
        del in_ref
        x, y, c, chips = _place()
        me = 2 * x + y
        sibling = (x, y, 1 - c)

        def half(chip, hf):
            return out_ref.at[chip, pl.ds(hf * HALF_ROWS, HALF_ROWS), :]

        def copy(k, chip, hf, to):
            return pltpu.make_async_remote_copy(
                src_ref=half(chip, hf), dst_ref=half(chip, hf), send_sem=send_sems.at[k], recv_sem=recv_sems.at[k],
                device_id=to, device_id_type=MESH)

        first = [copy(j, me, c, (*chip, c)) for j, chip in enumerate(chips)]
        for cp in first:
            cp.start()
        passed = [copy(3 + j, 2 * chip[0] + chip[1], c, sibling) for j, chip in enumerate(chips)]
        for j, chip in enumerate(chips):
            copy(j, 2 * chip[0] + chip[1], c, sibling).wait_recv()
            passed[j].start()
        for j, chip in enumerate(chips):
            copy(3 + j, 2 * chip[0] + chip[1], 1 - c, sibling).wait_recv()
        for cp in first + passed:
            cp.wait_send()

    return pl.pallas_call(
        body, out_shape=jax.ShapeDtypeStruct(slots.shape, slots.dtype),
        in_specs=[pl.BlockSpec(memory_space=pltpu.HBM)], out_specs=pl.BlockSpec(memory_space=pltpu.HBM),
        scratch_shapes=[pltpu.SemaphoreType.DMA((6,)), pltpu.SemaphoreType.DMA((6,))],
        input_output_aliases={0: 0}, name="gather_weights",
    )(slots)


def _swap_halves(grads):
    def body(g0_ref, g1_ref, g2_ref, g3_ref, out_ref, send_sems, recv_sems):
        x, y, c, _ = _place()
        copies = [pltpu.make_async_remote_copy(
            src_ref=g_ref.at[pl.ds((1 - c) * HALF_ROWS, HALF_ROWS), :], dst_ref=out_ref.at[k],
            send_sem=send_sems.at[k], recv_sem=recv_sems.at[k], device_id=(x, y, 1 - c), device_id_type=MESH)
            for k, g_ref in enumerate((g0_ref, g1_ref, g2_ref, g3_ref))]
        for cp in copies:
            cp.start()
        for cp in copies:
            cp.wait()

    return pl.pallas_call(
        body, out_shape=jax.ShapeDtypeStruct((4, HALF_ROWS, LANES), F32),
        in_specs=[pl.BlockSpec(memory_space=pltpu.HBM)] * 4, out_specs=pl.BlockSpec(memory_space=pltpu.HBM),
        scratch_shapes=[pltpu.SemaphoreType.DMA((4,)), pltpu.SemaphoreType.DMA((4,))], name="swap_halves",
    )(*grads)


def _add_halves(grads, other):
    blocks = HALF_ROWS // SUM_ROWS

    def body(c_ref, g0_ref, g1_ref, g2_ref, g3_ref, o_ref, out_ref):
        for k, g_ref in enumerate((g0_ref, g1_ref, g2_ref, g3_ref)):
            out_ref[k] = (g_ref[...] + o_ref[k]).astype(BF16)

    c = lax.axis_index("c").astype(jnp.int32).reshape(1)
    grid_spec = pltpu.PrefetchScalarGridSpec(
        num_scalar_prefetch=1, grid=(blocks,),
        in_specs=[pl.BlockSpec((SUM_ROWS, LANES), lambda i, c_ref: (c_ref[0] * blocks + i, 0))] * 4
        + [pl.BlockSpec((4, SUM_ROWS, LANES), lambda i, c_ref: (0, i, 0))],
        out_specs=pl.BlockSpec((4, SUM_ROWS, LANES), lambda i, c_ref: (0, i, 0)))
    return pl.pallas_call(
        body, out_shape=jax.ShapeDtypeStruct((4, HALF_ROWS, LANES), BF16), grid_spec=grid_spec,
        name="add_halves", compiler_params=_params(("parallel",)),
    )(c, *grads, other)


def _scatter_chips(part):
    def body(p_ref, out_ref, send_sems, recv_sems):
        x, y, c, chips = _place()
        me = 2 * x + y
        sends = [pltpu.make_async_remote_copy(
            src_ref=p_ref.at[2 * chip[0] + chip[1]], dst_ref=out_ref.at[me], send_sem=send_sems.at[j],
            recv_sem=recv_sems.at[j], device_id=(*chip, c), device_id_type=MESH) for j, chip in enumerate(chips)]
        for cp in sends:
            cp.start()
        for j, chip in enumerate(chips):
            there = 2 * chip[0] + chip[1]
            pltpu.make_async_remote_copy(
                src_ref=p_ref.at[there], dst_ref=out_ref.at[there], send_sem=send_sems.at[j],
                recv_sem=recv_sems.at[j], device_id=(*chip, c), device_id_type=MESH).wait_recv()
        for cp in sends:
            cp.wait_send()

    return pl.pallas_call(
        body, out_shape=jax.ShapeDtypeStruct((4, HALF_ROWS, LANES), part.dtype),
        in_specs=[pl.BlockSpec(memory_space=pltpu.HBM)], out_specs=pl.BlockSpec(memory_space=pltpu.HBM),
        scratch_shapes=[pltpu.SemaphoreType.DMA((3,)), pltpu.SemaphoreType.DMA((3,))],
        name="scatter_chips",
    )(part)


def _add_chips(part, others):
    blocks = HALF_ROWS // SUM_ROWS

    def body(pos_ref, own_ref, r0_ref, r1_ref, r2_ref, r3_ref, out_ref):
        me = pos_ref[0]
        own = own_ref[0].astype(F32)
        total = None
        for k, r_ref in enumerate((r0_ref, r1_ref, r2_ref, r3_ref)):
            term = jnp.where(me == k, own, r_ref[0].astype(F32))
            total = term if total is None else total + term
        out_ref[...] = total

    def other(k):
        return pl.BlockSpec((1, SUM_ROWS, LANES),
                            lambda i, pos, k=k: (jnp.where(pos[0] == k, (k + 1) % 4, k), i, 0))

    pos = jnp.stack([_chip_index(), lax.axis_index("c").astype(jnp.int32)])
    grid_spec = pltpu.PrefetchScalarGridSpec(
        num_scalar_prefetch=1, grid=(blocks,),
        in_specs=[pl.BlockSpec((1, SUM_ROWS, LANES), lambda i, pos: (pos[0], i, 0))] + [other(k) for k in range(4)],
        out_specs=pl.BlockSpec((SUM_ROWS, LANES), lambda i, pos: (pos[1] * blocks + i, 0)))
    return pl.pallas_call(
        body, out_shape=jax.ShapeDtypeStruct((PACK_ROWS, LANES), F32), grid_spec=grid_spec,
        name="add_chips", compiler_params=_params(("parallel",)),
    )(pos, part, others, others, others, others)


def _join_halves(halves):
    def body(in_ref, out_ref, send_sem, recv_sem):
        del in_ref
        x, y, c, _ = _place()
        mine = out_ref.at[pl.ds(c * HALF_ROWS, HALF_ROWS), :]
        cp = pltpu.make_async_remote_copy(src_ref=mine, dst_ref=mine, send_sem=send_sem, recv_sem=recv_sem,
                                          device_id=(x, y, 1 - c), device_id_type=MESH)
        cp.start()
        theirs = out_ref.at[pl.ds((1 - c) * HALF_ROWS, HALF_ROWS), :]
        pltpu.make_async_remote_copy(src_ref=theirs, dst_ref=theirs, send_sem=send_sem, recv_sem=recv_sem,
                                     device_id=(x, y, 1 - c), device_id_type=MESH).wait_recv()
        cp.wait_send()

    return pl.pallas_call(
        body, out_shape=jax.ShapeDtypeStruct(halves.shape, halves.dtype),
        in_specs=[pl.BlockSpec(memory_space=pltpu.HBM)], out_specs=pl.BlockSpec(memory_space=pltpu.HBM),
        scratch_shapes=[pltpu.SemaphoreType.DMA, pltpu.SemaphoreType.DMA],
        input_output_aliases={0: 0}, name="join_halves",
    )(halves)


def _adamw(name, w, g, m, v):
    shape = w.shape
    cols = shape[-1]
    rows = w.size // cols
    tr = rows
    for cand in (512, 256, 128, 64, 32, 16, 8):
        if rows % cand == 0 and cand * cols * 4 <= 4 * 1024 * 1024:
            tr = cand
            break

    def body(w_ref, g_ref, m_ref, v_ref, d_ref, nm_ref, nv_ref):
        gv = g_ref[...]
        nm = ADAM_B1 * m_ref[...] + (1.0 - ADAM_B1) * gv
        nv = ADAM_B2 * v_ref[...] + (1.0 - ADAM_B2) * (gv * gv)
        m_hat = nm / (1.0 - ADAM_B1 ** ADAM_STEP)
        v_hat = nv / (1.0 - ADAM_B2 ** ADAM_STEP)
        d_ref[...] = -ADAM_LR * (m_hat / (jnp.sqrt(v_hat) + ADAM_EPS) + ADAM_WD * w_ref[...])
        nm_ref[...] = nm
        nv_ref[...] = nv

    spec = pl.BlockSpec((tr, cols), lambda i: (i, 0))
    two = lambda a: a.reshape(rows, cols)
    outs = pl.pallas_call(
        body, out_shape=[jax.ShapeDtypeStruct((rows, cols), F32)] * 3, grid=(rows // tr,),
        in_specs=[spec] * 4, out_specs=[spec] * 3, name=name, compiler_params=_params(("parallel",)),
    )(two(w), two(g), two(m), two(v))
    return [o.reshape(shape) for o in outs]


def _full_shape(name, shard_shape):
    shape = list(shard_shape)
    shape[SHARD_AXIS[name]] *= 4
    return tuple(shape)


def kernel(x, c, ada_w, ada_b, norm_g, ffn_wg, ffn_wu, ffn_wd, w_in, attn_wo, conv_w, conv_b, conv_ln_g, conv_ln_b, conv_wo, w_out, final_g, loss_target, m_ada_w, m_ada_b, m_norm_g, m_ffn_wg, m_ffn_wu, m_ffn_wd, m_w_in, m_attn_wo, m_conv_w, m_conv_b, m_conv_ln_g, m_conv_ln_b, m_conv_wo, m_w_out, m_final_g, v_ada_w, v_ada_b, v_norm_g, v_ffn_wg, v_ffn_wu, v_ffn_wd, v_w_in, v_attn_wo, v_conv_w, v_conv_b, v_conv_ln_g, v_conv_ln_b, v_conv_wo, v_w_out, v_final_g):
    weights = dict(ada_w=ada_w, ada_b=ada_b, norm_g=norm_g, ffn_wg=ffn_wg, ffn_wu=ffn_wu, ffn_wd=ffn_wd, w_in=w_in,
                   attn_wo=attn_wo, conv_w=conv_w, conv_b=conv_b, conv_ln_g=conv_ln_g, conv_ln_b=conv_ln_b,
                   conv_wo=conv_wo, w_out=w_out, final_g=final_g)
    moments_m = dict(ada_w=m_ada_w, ada_b=m_ada_b, norm_g=m_norm_g, ffn_wg=m_ffn_wg, ffn_wu=m_ffn_wu,
                     ffn_wd=m_ffn_wd, w_in=m_w_in, attn_wo=m_attn_wo, conv_w=m_conv_w, conv_b=m_conv_b,
                     conv_ln_g=m_conv_ln_g, conv_ln_b=m_conv_ln_b, conv_wo=m_conv_wo, w_out=m_w_out,
                     final_g=m_final_g)
    moments_v = dict(ada_w=v_ada_w, ada_b=v_ada_b, norm_g=v_norm_g, ffn_wg=v_ffn_wg, ffn_wu=v_ffn_wu,
                     ffn_wd=v_ffn_wd, w_in=v_w_in, attn_wo=v_attn_wo, conv_w=v_conv_w, conv_b=v_conv_b,
                     conv_ln_g=v_conv_ln_g, conv_ln_b=v_conv_ln_b, conv_wo=v_conv_wo, w_out=v_w_out,
                     final_g=v_final_g)
    shard_shapes = [weights[n].shape for n in SHARDED]
    tail_shapes = [weights[n].shape for n in REPLICATED]

    sent, sent_names = [], []
    for n in SHARDED:
        if n in EXACT:
            sent += list(_split_bits(weights[n]))
            sent_names += [(n, "hi"), (n, "lo")]
        else:
            sent.append(weights[n].astype(BF16))
            sent_names.append(n)
    slots = lax.dynamic_update_slice(jnp.zeros((4, PACK_ROWS, LANES), BF16), _pack(sent)[None],
                                     (_chip_index(), 0, 0))
    gathered = _gather_weights(slots)
    per_chip = [_unpack(gathered[k], [s.shape for s in sent]) for k in range(4)]
    whole = {n: jnp.concatenate([per_chip[k][i] for k in range(4)],
                                axis=SHARD_AXIS[n if isinstance(n, str) else n[0]])
             for i, n in enumerate(sent_names)}
    full = {n: whole[n] for n in SHARDED if n not in EXACT}
    vectors = dict(ada_b=ada_b, conv_b=conv_b, conv_ln_g=conv_ln_g, conv_ln_b=conv_ln_b, final_g=final_g,
                   **{n: _join_bits(whole[(n, "hi")], whole[(n, "lo")]) for n in EXACT})

    loss_cols, dx, grads = _local_step(x[0], c, loss_target[0], full, vectors)
    loss = lax.psum(jnp.sum(loss_cols), ("x", "y", "c"))

    tail = [leaf for n in REPLICATED for leaf in _leaves(grads[n])]
    blocks = []
    for k in range(4):
        parts = []
        for n, shape in zip(SHARDED, shard_shapes):
            width = shape[SHARD_AXIS[n]]
            for leaf in _leaves(grads[n]):
                ax = SHARD_AXIS[n] - (len(shape) - leaf.ndim)
                parts.append(lax.slice_in_dim(leaf, k * width, (k + 1) * width, axis=ax))
        blocks.append(_pack(parts + tail))
    chip_sum = _add_halves(blocks, _swap_halves(blocks))
    reduced = _join_halves(_add_chips(chip_sum, _scatter_chips(chip_sum)))
    summed = dict(zip(SHARDED + REPLICATED, _unpack(reduced, shard_shapes + tail_shapes)))

    deltas, new_m, new_v = {}, {}, {}
    for n in WEIGHTS:
        deltas[n], new_m[n], new_v[n] = _adamw(f"adamw_{n}", weights[n], summed[n], moments_m[n], moments_v[n])

    return (loss, dx[None], *[summed[n] for n in WEIGHTS], *[deltas[n] for n in WEIGHTS],
            *[new_m[n] for n in WEIGHTS], *[new_v[n] for n in WEIGHTS])
```

```python
import functools

import jax
import jax.numpy as jnp
from jax import lax
from jax.experimental import pallas as pl
from jax.experimental.pallas import tpu as pltpu

F32 = jnp.float32
BF16 = jnp.bfloat16

D = 1024
DFF = 2816
HEAD = 64
GW = 256
DILATIONS = (1, 4, 16)
BAND = 128
QKV = 2304
CONV_K = 31
HALO = 32
N_MOD = 9
EPS = 1e-6
NEG_INF = -1e30
DEPTH = 2

SHARDS = 4
FSH = DFF // SHARDS
LANES = 1024

ADAM_LR = 0.001
ADAM_B1 = 0.9
ADAM_B2 = 0.999
ADAM_EPS = 1e-08
ADAM_WD = 0.01
ADAM_STEP = 10

VMEM_LIMIT = 56 * 1024 * 1024

EXACT = ("norm_g", "conv_w")
WEIGHTS = ("ada_w", "ada_b", "norm_g", "ffn_wg", "ffn_wu", "ffn_wd", "w_in", "attn_wo", "conv_w", "conv_b",
           "conv_ln_g", "conv_ln_b", "conv_wo", "w_out", "final_g")

MESH = pl.DeviceIdType.MESH


def _params(sem=None):
    return pltpu.CompilerParams(dimension_semantics=sem, vmem_limit_bytes=VMEM_LIMIT)


def _sigmoid(v):
    return jax.nn.sigmoid(v)


TILE = "tile"


def _lead_spec(arr, lead, block, index):
    def index_map(j, i):
        return (*[j if e == TILE else e for e in lead], *index(j, i))
    return pl.BlockSpec((None,) * len(lead) + tuple(block), index_map)


def _entry(e):
    return e if isinstance(e, tuple) else (e, ())


def _mm(name, a_list, b_list, pairs, epilogue, out_dtypes, *, tm, tn, n_out, trans_b=False, extras=(), vecs=(),
        out_widths=None, out_tiled=False):
    a_list = [_entry(a) for a in a_list]
    b_list = [_entry(b) for b in b_list]
    m = a_list[0][0].shape[-2]
    na, nb, ne, nv = len(a_list), len(b_list), len(extras), len(vecs)
    dn = (((1,), (1,)), ((), ())) if trans_b else (((1,), (0,)), ((), ()))

    def body(*refs):
        a_refs = refs[:na]
        b_refs = refs[na:na + nb]
        e_refs = refs[na + nb:na + nb + ne]
        v_refs = refs[na + nb + ne:na + nb + ne + nv]
        o_refs = refs[na + nb + ne + nv:]
        accs = [lax.dot_general(a_refs[ai][...], b_refs[bi][...], dn, preferred_element_type=F32)
                for ai, bi in pairs]
        res = epilogue(accs, [e[...] for e in e_refs], [v[...] for v in v_refs])
        for o_ref, r in zip(o_refs, res):
            o_ref[...] = r.astype(o_ref.dtype)

    in_specs = [_lead_spec(a, lead, (tm, a.shape[-1]), lambda j, i: (i, 0)) for a, lead in a_list]
    for b, lead in b_list:
        if TILE in lead:
            in_specs.append(_lead_spec(b, lead, b.shape[-2:], lambda j, i: (0, 0)))
        elif trans_b:
            in_specs.append(_lead_spec(b, lead, (tn, b.shape[-1]), lambda j, i: (j, 0)))
        else:
            in_specs.append(_lead_spec(b, lead, (b.shape[-2], tn), lambda j, i: (0, j)))
    for e, where in extras:
        if isinstance(where, tuple):
            in_specs.append(_lead_spec(e, where, (tm, tn), lambda j, i: (i, 0)))
        else:
            in_specs.append(pl.BlockSpec((tm, tn), functools.partial(lambda j, i, off: (i, j + off), off=where)))
    in_specs += [pl.BlockSpec((1, tn), lambda j, i: (0, j)) for _ in vecs]
    nj = n_out // tn
    widths = out_widths or [tn] * len(out_dtypes)
    if out_tiled:
        out_specs = [pl.BlockSpec((None, tm, tn), lambda j, i: (j, i, 0)) for _ in out_dtypes]
        out_shape = [jax.ShapeDtypeStruct((nj, m, tn), dt) for dt in out_dtypes]
    else:
        out_specs = [pl.BlockSpec((tm, wd), lambda j, i: (i, j)) for wd in widths]
        out_shape = [jax.ShapeDtypeStruct((m, nj * wd), dt) for dt, wd in zip(out_dtypes, widths)]
    return pl.pallas_call(
        body, out_shape=out_shape, grid=(nj, m // tm), in_specs=in_specs, out_specs=out_specs,
        name=name, compiler_params=_params(("parallel", "parallel")),
    )(*[a for a, _ in a_list], *[b for b, _ in b_list], *[e for e, _ in extras], *vecs)


def _mm_tn(name, a, b, *, tk, tn, tt, a_fn=None, a_tiled=False, b_tiled=False, into=None):
    a_list = list(a) if a_fn is not None else [a]
    na = len(a_list)
    t = a_list[0].shape[-2]
    nk = a_list[0].shape[0] if a_tiled else a_list[0].shape[1] // tk
    nn = b.shape[0] if b_tiled else b.shape[1] // tn
    steps = t // tt
    has_into = into is not None

    def body(*refs):
        refs = refs[1:] if has_into else refs
        a_refs, b_ref, o_ref, acc_ref = refs[:na], refs[na], refs[na + 1], refs[na + 2]
        s = pl.program_id(2)

        @pl.when(s == 0)
        def _():
            acc_ref[...] = jnp.zeros_like(acc_ref)

        av = a_refs[0][...] if a_fn is None else a_fn([r[...] for r in a_refs])
        acc_ref[...] += lax.dot_general(av, b_ref[...], (((0,), (0,)), ((), ())), preferred_element_type=F32)

        @pl.when(s == steps - 1)
        def _():
            o_ref[...] = acc_ref[...]

    a_spec = (pl.BlockSpec((None, tt, tk), lambda i, j, s: (i, s, 0)) if a_tiled
              else pl.BlockSpec((tt, tk), lambda i, j, s: (s, i)))
    b_spec = (pl.BlockSpec((None, tt, tn), lambda i, j, s: (j, s, 0)) if b_tiled
              else pl.BlockSpec((tt, tn), lambda i, j, s: (s, j)))
    if a_tiled:
        out_dims, tile_index = (nk, tk, nn * tn), lambda i, j, s: (i, 0, j)
    elif b_tiled:
        out_dims, tile_index = (nn, nk * tk, tn), lambda i, j, s: (j, i, 0)
    else:
        out_dims, tile_index = (nk * tk, nn * tn), lambda i, j, s: (i, j)
    tiled = a_tiled or b_tiled
    if has_into:
        buf, lead = into
        def out_index(i, j, s):
            idx = tile_index(i, j, s)
            return (idx[0], *lead, *idx[1:])
        out_spec = pl.BlockSpec((None,) * (1 + len(lead)) + (tk, tn), out_index)
        out_shape = jax.ShapeDtypeStruct(buf.shape, buf.dtype)
        extra_in, extra_specs, aliases = [buf], [pl.BlockSpec(memory_space=pl.ANY)], {0: 0}
    else:
        out_spec = pl.BlockSpec(((None,) if tiled else ()) + (tk, tn), tile_index)
        out_shape = jax.ShapeDtypeStruct(out_dims, F32)
        extra_in, extra_specs, aliases = [], [], {}
    return pl.pallas_call(
        body, out_shape=out_shape, grid=(nk, nn, steps), in_specs=extra_specs + [a_spec] * na + [b_spec],
        out_specs=out_spec, scratch_shapes=[pltpu.VMEM((tk, tn), F32)], input_output_aliases=aliases, name=name,
        compiler_params=_params(("parallel", "parallel", "arbitrary")),
    )(*extra_in, *a_list, b)


def _first(accs, extras, vecs):
    return [accs[0]]


def _total(accs, extras, vecs):
    out = accs[0]
    for r in accs[1:]:
        out = out + r
    return [out]


def _row_spec(tm, width, col=0):
    return pl.BlockSpec((tm, width), functools.partial(lambda i, col: (i, col), col=col))


def _vec_spec(width):
    return pl.BlockSpec((1, width), lambda i: (0, 0))


def _normmod_fwd(name, x, g, scale, shift, tm=512):
    t = x.shape[0]

    def body(x_ref, g_ref, sc_ref, sh_ref, h_ref):
        xv = x_ref[...]
        r = lax.rsqrt(jnp.mean(xv * xv, axis=-1, keepdims=True) + EPS)
        h_ref[...] = ((xv * r) * g_ref[...] * (1.0 + sc_ref[...]) + sh_ref[...]).astype(BF16)

    return pl.pallas_call(
        body, out_shape=jax.ShapeDtypeStruct((t, D), BF16), grid=(t // tm,),
        in_specs=[_row_spec(tm, D), _vec_spec(D), _vec_spec(D), _vec_spec(D)], out_specs=_row_spec(tm, D),
        name=name, compiler_params=_params(("parallel",)),
    )(x, g, scale, shift)


def _normmod_bwd(name, x, dh, dres, g, scale, tm=256):
    t = x.shape[0]
    steps = t // tm

    def body(x_ref, dh_ref, dres_ref, g_ref, sc_ref, dx_ref, dg_ref, dsc_ref, dsh_ref):
        i = pl.program_id(0)

        @pl.when(i == 0)
        def _():
            dg_ref[...] = jnp.zeros_like(dg_ref)
            dsh_ref[...] = jnp.zeros_like(dsh_ref)

        xv = x_ref[...]
        dh = dh_ref[...]
        r = lax.rsqrt(jnp.mean(xv * xv, axis=-1, keepdims=True) + EPS)
        xh = xv * r
        dxh = dh * (g_ref[...] * (1.0 + sc_ref[...]))
        dx_ref[...] = dres_ref[...] + r * (dxh - xh * jnp.mean(dxh * xh, axis=-1, keepdims=True))
        dg_ref[...] += jnp.sum(dh * xh, axis=0, keepdims=True)
        dsh_ref[...] += jnp.sum(dh, axis=0, keepdims=True)

        @pl.when(i == steps - 1)
        def _():
            acc = dg_ref[...]
            dg_ref[...] = acc * (1.0 + sc_ref[...])
            dsc_ref[...] = acc * g_ref[...]

    vec = jax.ShapeDtypeStruct((1, D), F32)
    return pl.pallas_call(
        body, out_shape=[jax.ShapeDtypeStruct((t, D), F32), vec, vec, vec], grid=(steps,),
        in_specs=[_row_spec(tm, D), _row_spec(tm, D), _row_spec(tm, D), _vec_spec(D), _vec_spec(D)],
        out_specs=[_row_spec(tm, D), _vec_spec(D), _vec_spec(D), _vec_spec(D)],
        name=name, compiler_params=_params(("arbitrary",)),
    )(x, dh, dres, g, scale)


def _resgate_bwd(name, dx, f, gate, coef, tm=512):
    t = dx.shape[0]

    def body(dx_ref, f_ref, gate_ref, df_ref, dgate_ref):
        @pl.when(pl.program_id(0) == 0)
        def _():
            dgate_ref[...] = jnp.zeros_like(dgate_ref)

        dxv = dx_ref[...]
        df_ref[...] = ((coef * gate_ref[...]) * dxv).astype(BF16)
        dgate_ref[...] += jnp.sum((coef * f_ref[...].astype(F32)) * dxv, axis=0, keepdims=True)

    return pl.pallas_call(
        body, out_shape=[jax.ShapeDtypeStruct((t, D), BF16), jax.ShapeDtypeStruct((1, D), F32)], grid=(t // tm,),
        in_specs=[_row_spec(tm, D), _row_spec(tm, D), _vec_spec(D)], out_specs=[_row_spec(tm, D), _vec_spec(D)],
        name=name, compiler_params=_params(("arbitrary",)),
    )(dx, f, gate)


def _loss_bwd(name, x, target, g, tm=256):
    t = x.shape[0]

    def body(x_ref, t_ref, g_ref, dx_ref, dg_ref, loss_ref):
        @pl.when(pl.program_id(0) == 0)
        def _():
            dg_ref[...] = jnp.zeros_like(dg_ref)
            loss_ref[...] = jnp.zeros_like(loss_ref)

        xv = x_ref[...]
        r = lax.rsqrt(jnp.mean(xv * xv, axis=-1, keepdims=True) + EPS)
        xh = xv * r
        err = xh * g_ref[...] - t_ref[...]
        dy = err * (1.0 / D)
        dxh = dy * g_ref[...]
        dx_ref[...] = r * (dxh - xh * jnp.mean(dxh * xh, axis=-1, keepdims=True))
        dg_ref[...] += jnp.sum(dy * xh, axis=0, keepdims=True)
        loss_ref[...] += jnp.sum(err * err, axis=0, keepdims=True) * (0.5 / D)

    vec = jax.ShapeDtypeStruct((1, D), F32)
    return pl.pallas_call(
        body, out_shape=[jax.ShapeDtypeStruct((t, D), F32), vec, vec], grid=(t // tm,),
        in_specs=[_row_spec(tm, D), _row_spec(tm, D), _vec_spec(D)],
        out_specs=[_row_spec(tm, D), _vec_spec(D), _vec_spec(D)],
        name=name, compiler_params=_params(("arbitrary",)),
    )(x, target, g)


def _mod_fwd(name, c8, ada_w, layer, ada_b):
    tn = ada_w.shape[-1]

    def body(c_ref, w_ref, b_ref, o_ref):
        cv = c_ref[...]
        ca = (cv * _sigmoid(cv)).astype(BF16)
        o_ref[...] = jnp.dot(ca, w_ref[...], preferred_element_type=F32) + b_ref[...]

    return pl.pallas_call(
        body, out_shape=jax.ShapeDtypeStruct((8, SHARDS * tn), F32), grid=(SHARDS,),
        in_specs=[pl.BlockSpec((8, D), lambda j: (0, 0)), pl.BlockSpec((None, None, D, tn), lambda j: (j, layer, 0, 0)),
                  pl.BlockSpec((1, tn), lambda j: (0, j))],
        out_specs=pl.BlockSpec((8, tn), lambda j: (0, j)), name=name, compiler_params=_params(("parallel",)),
    )(c8, ada_w, ada_b)


def _mod_bwd(name, c_col, dmods, tk=256):
    tn = dmods.shape[-1] // SHARDS

    def body(c_ref, d_ref, o_ref):
        cv = c_ref[...]
        o_ref[...] = (cv * _sigmoid(cv)) * d_ref[...]

    return pl.pallas_call(
        body, out_shape=jax.ShapeDtypeStruct((SHARDS, DEPTH, D, tn), F32), grid=(SHARDS, DEPTH, D // tk),
        in_specs=[pl.BlockSpec((tk, 1), lambda j, l, i: (i, 0)), pl.BlockSpec((None, 1, tn), lambda j, l, i: (l, 0, j))],
        out_specs=pl.BlockSpec((None, None, tk, tn), lambda j, l, i: (j, l, i, 0)), name=name,
        compiler_params=_params(("parallel", "parallel", "parallel")),
    )(c_col, dmods)


def _rope_tables(t):
    half = HEAD // 2
    inv_freq = 10000.0 ** (-(jnp.arange(half, dtype=F32) * 2.0 / HEAD))
    ang = jnp.arange(t, dtype=F32)[:, None] * inv_freq[None, :]
    cos, sin = jnp.cos(ang), jnp.sin(ang)
    cos_t = jnp.tile(jnp.concatenate([cos, cos], axis=1), (1, GW // HEAD))
    sin_t = jnp.tile(jnp.concatenate([-sin, sin], axis=1), (1, GW // HEAD))
    return cos_t, sin_t


def _rotate(tv, cos, sin_signed):
    lane = lax.broadcasted_iota(jnp.int32, tv.shape, 1)
    first = (lane % HEAD) < (HEAD // 2)
    partner = jnp.where(first, pltpu.roll(tv, GW - HEAD // 2, 1), pltpu.roll(tv, HEAD // 2, 1))
    return tv * cos + partner * sin_signed


def _rope_fwd(name, zqkv, cos_t, sin_t, tm=512):
    t = zqkv.shape[0]
    ng = len(DILATIONS)
    n = 3 * ng

    def body(*refs):
        z_refs, cos_ref, sin_ref, o_refs = refs[:n], refs[n], refs[n + 1], refs[n + 2:]
        cos, sin = cos_ref[...], sin_ref[...]
        for idx in range(n):
            r = z_refs[idx][...]
            if idx < 2 * ng:
                r = _rotate(r, cos, sin)
            if idx < ng:
                r = r * (HEAD ** -0.5)
            o_refs[idx][...] = r.astype(BF16)

    return pl.pallas_call(
        body, out_shape=[jax.ShapeDtypeStruct((t, GW), BF16)] * n, grid=(t // tm,),
        in_specs=[_row_spec(tm, GW, col) for col in range(n)] + [_row_spec(tm, GW), _row_spec(tm, GW)],
        out_specs=[_row_spec(tm, GW)] * n, name=name, compiler_params=_params(("parallel",)),
    )(*([zqkv] * n), cos_t, sin_t)


def _rope_bwd(name, grads, cos_t, sin_t, tm=512):
    t = grads[0].shape[0]
    ng = len(DILATIONS)
    n = len(grads)

    def body(*refs):
        g_refs, cos_ref, sin_ref, o_ref = refs[:n], refs[n], refs[n + 1], refs[n + 2]
        cos, sin = cos_ref[...], -sin_ref[...]
        for idx in range(n):
            r = g_refs[idx][...]
            if idx < 2 * ng:
                r = _rotate(r, cos, sin)
            if idx < ng:
                r = r * (HEAD ** -0.5)
            o_ref[:, idx * GW:(idx + 1) * GW] = r.astype(BF16)

    return pl.pallas_call(
        body, out_shape=jax.ShapeDtypeStruct((t, n * GW), BF16), grid=(t // tm,),
        in_specs=[_row_spec(tm, GW)] * (n + 2), out_specs=_row_spec(tm, n * GW),
        name=name, compiler_params=_params(("parallel",)),
    )(*grads, cos_t, sin_t)


def _head_cols(h):
    return slice(h * HEAD, (h + 1) * HEAD)


def _band_mask_q(has_prev):
    qi = lax.broadcasted_iota(jnp.int32, (BAND, 2 * BAND), 0)
    kj = lax.broadcasted_iota(jnp.int32, (BAND, 2 * BAND), 1)
    dist = qi + BAND - kj
    return (dist >= 0) & (dist <= BAND) & ((kj >= BAND) | has_prev)


def _attn_fwd(name, q, k, v, group):
    t = q.shape[0]
    d = DILATIONS[group]
    length = t // d
    qb = min(512, length)
    sub = qb // BAND
    nblk = length // qb

    def body(q_ref, kc_ref, kp_ref, vc_ref, vp_ref, o_ref, lse_ref):
        blk = pl.program_id(1)
        k_ext = jnp.concatenate([kp_ref[...], kc_ref[...]], axis=0)
        v_ext = jnp.concatenate([vp_ref[...], vc_ref[...]], axis=0)
        for j in range(sub):
            mask = _band_mask_q((blk * sub + j) > 0)
            qj = q_ref[j * BAND:(j + 1) * BAND, :]
            kj = k_ext[j * BAND:(j + 2) * BAND, :]
            vj = v_ext[j * BAND:(j + 2) * BAND, :]
            outs, lses = [], []
            for h in range(GW // HEAD):
                s = lax.dot_general(qj[:, _head_cols(h)], kj[:, _head_cols(h)], (((1,), (1,)), ((), ())),
                                    preferred_element_type=F32)
                s = jnp.where(mask, s, NEG_INF)
                m = jnp.max(s, axis=-1, keepdims=True)
                p = jnp.exp(s - m)
                den = jnp.sum(p, axis=-1, keepdims=True)
                o = jnp.dot(p.astype(BF16), vj[:, _head_cols(h)], preferred_element_type=F32)
                outs.append(o / den)
                lses.append(jnp.broadcast_to(m + jnp.log(den), (BAND, HEAD)))
            o_ref[j * BAND:(j + 1) * BAND, :] = jnp.concatenate(outs, axis=1)
            lse_ref[j * BAND:(j + 1) * BAND, :] = jnp.concatenate(lses, axis=1)

    prev = qb // BAND
    cur = lambda r, b: (b, r)
    before = lambda r, b: (jnp.maximum(b * prev - 1, 0), r)
    big, halo = pl.BlockSpec((qb, GW), cur), pl.BlockSpec((BAND, GW), before)
    view = lambda a: a.reshape(length, d * GW)
    out = pl.pallas_call(
        body, out_shape=[jax.ShapeDtypeStruct((length, d * GW), F32)] * 2, grid=(d, nblk),
        in_specs=[big, big, halo, big, halo], out_specs=[big] * 2, name=name,
        compiler_params=_params(("parallel", "parallel")),
    )(view(q), view(k), view(k), view(v), view(v))
    return out[0].reshape(t, GW), out[1].reshape(t, GW)


def _attn_merge(name, outs, lses, tm=512):
    t = outs[0].shape[0]
    n = len(outs)

    def body(*refs):
        o_refs, l_refs = refs[:n], refs[n:2 * n]
        ob_ref, of_ref, lj_ref = refs[2 * n:]
        ls = [r[...] for r in l_refs]
        m = ls[0]
        for v in ls[1:]:
            m = jnp.maximum(m, v)
        es = [jnp.exp(v - m) for v in ls]
        tot = es[0]
        for v in es[1:]:
            tot = tot + v
        acc = (es[0] / tot) * o_refs[0][...]
        for e, o_ref in zip(es[1:], o_refs[1:]):
            acc = acc + (e / tot) * o_ref[...]
        ob_ref[...] = acc.astype(BF16)
        of_ref[...] = acc
        lj_ref[...] = m + jnp.log(tot)

    return pl.pallas_call(
        body, out_shape=[jax.ShapeDtypeStruct((t, GW), BF16), jax.ShapeDtypeStruct((t, GW), F32),
                         jax.ShapeDtypeStruct((t, GW), F32)], grid=(t // tm,),
        in_specs=[_row_spec(tm, GW)] * (2 * n), out_specs=[_row_spec(tm, GW)] * 3,
        name=name, compiler_params=_params(("parallel",)),
    )(*outs, *lses)


def _attn_bwd_prep(name, do, o, tm=512):
    t = do.shape[0]

    def body(do_ref, o_ref, dsum_ref, dob_ref):
        dov = do_ref[...]
        prod = dov * o_ref[...]
        parts = [jnp.broadcast_to(jnp.sum(prod[:, _head_cols(h)], axis=-1, keepdims=True), (tm, HEAD))
                 for h in range(GW // HEAD)]
        dsum_ref[...] = jnp.concatenate(parts, axis=1)
        dob_ref[...] = dov.astype(BF16)

    return pl.pallas_call(
        body, out_shape=[jax.ShapeDtypeStruct((t, GW), F32), jax.ShapeDtypeStruct((t, GW), BF16)], grid=(t // tm,),
        in_specs=[_row_spec(tm, GW)] * 2, out_specs=[_row_spec(tm, GW)] * 2,
        name=name, compiler_params=_params(("parallel",)),
    )(do, o)


def _attn_bwd(name, q, k, v, do, lj, dsum, group):
    t = q.shape[0]
    d = DILATIONS[group]
    length = t // d
    qb = min(512, length)
    sub = qb // BAND
    nblk = length // qb
    total = length // BAND

    def body(qc_ref, qn_ref, kc_ref, kp_ref, vc_ref, vp_ref, doc_ref, don_ref, ljc_ref, ljn_ref, dsc_ref, dsn_ref,
             dq_ref, dk_ref, dv_ref):
        blk = pl.program_id(1)
        q_ext = jnp.concatenate([qc_ref[...], qn_ref[...]], axis=0)
        do_ext = jnp.concatenate([doc_ref[...], don_ref[...]], axis=0)
        lj_ext = jnp.concatenate([ljc_ref[...], ljn_ref[...]], axis=0)
        ds_ext = jnp.concatenate([dsc_ref[...], dsn_ref[...]], axis=0)
        k_ext = jnp.concatenate([kp_ref[...], kc_ref[...]], axis=0)
        v_ext = jnp.concatenate([vp_ref[...], vc_ref[...]], axis=0)
        qi2 = lax.broadcasted_iota(jnp.int32, (2 * BAND, BAND), 0)
        kj2 = lax.broadcasted_iota(jnp.int32, (2 * BAND, BAND), 1)
        for j in range(sub):
            gblk = blk * sub + j
            rows = slice(j * BAND, (j + 1) * BAND)
            rows2 = slice(j * BAND, (j + 2) * BAND)
            mask_q = _band_mask_q(gblk > 0)
            mask_k = ((qi2 < BAND) & (kj2 <= qi2)) | ((qi2 >= BAND) & (kj2 >= qi2 - BAND) & (gblk + 1 < total))
            dqs, dks, dvs = [], [], []
            for h in range(GW // HEAD):
                hc = _head_cols(h)
                col = slice(h * HEAD, h * HEAD + 1)
                qh = q_ext[rows, hc]
                kh2 = k_ext[rows2, hc]
                s = lax.dot_general(qh, kh2, (((1,), (1,)), ((), ())), preferred_element_type=F32)
                p = jnp.where(mask_q, jnp.exp(s - lj_ext[rows, col]), 0.0)
                dp = lax.dot_general(do_ext[rows, hc], v_ext[rows2, hc], (((1,), (1,)), ((), ())),
                                     preferred_element_type=F32)
                dsc = p * (dp - ds_ext[rows, col])
                dqs.append(jnp.dot(dsc.astype(BF16), kh2, preferred_element_type=F32))
                qh2 = q_ext[rows2, hc]
                kh = k_ext[BAND + j * BAND:BAND + (j + 1) * BAND, hc]
                vh = v_ext[BAND + j * BAND:BAND + (j + 1) * BAND, hc]
                doh2 = do_ext[rows2, hc]
                s2 = lax.dot_general(qh2, kh, (((1,), (1,)), ((), ())), preferred_element_type=F32)
                p2 = jnp.where(mask_k, jnp.exp(s2 - lj_ext[rows2, col]), 0.0)
                dp2 = lax.dot_general(doh2, vh, (((1,), (1,)), ((), ())), preferred_element_type=F32)
                ds2 = p2 * (dp2 - ds_ext[rows2, col])
                dvs.append(lax.dot_general(p2.astype(BF16), doh2, (((0,), (0,)), ((), ())),
                                           preferred_element_type=F32))
                dks.append(lax.dot_general(ds2.astype(BF16), qh2, (((0,), (0,)), ((), ())),
                                           preferred_element_type=F32))
            dq_ref[rows, :] = jnp.concatenate(dqs, axis=1)
            dk_ref[rows, :] = jnp.concatenate(dks, axis=1)
            dv_ref[rows, :] = jnp.concatenate(dvs, axis=1).astype(BF16)

    prev = qb // BAND
    cur = lambda r, b: (b, r)
    before = lambda r, b: (jnp.maximum(b * prev - 1, 0), r)
    after = lambda r, b: (jnp.minimum((b + 1) * prev, total - 1), r)
    big = pl.BlockSpec((qb, GW), cur)
    nxt = pl.BlockSpec((BAND, GW), after)
    prv = pl.BlockSpec((BAND, GW), before)
    view = lambda a: a.reshape(length, d * GW)
    out = pl.pallas_call(
        body, out_shape=[jax.ShapeDtypeStruct((length, d * GW), F32), jax.ShapeDtypeStruct((length, d * GW), F32),
                         jax.ShapeDtypeStruct((length, d * GW), BF16)], grid=(d, nblk),
        in_specs=[big, nxt, big, prv, big, prv, big, nxt, big, nxt, big, nxt],
        out_specs=[big] * 3, name=name, compiler_params=_params(("parallel", "parallel")),
    )(view(q), view(q), view(k), view(k), view(v), view(v), view(do), view(do), view(lj), view(lj), view(dsum),
      view(dsum))
    return out[0].reshape(t, GW), out[1].reshape(t, GW), out[2].reshape(t, GW)


def _conv_fwd(name, zu, conv_w, conv_b, ln_g, ln_b, tm=256):
    t = zu.shape[0]
    per = tm // HALO

    def body(a_ref, gl_ref, ah_ref, glh_ref, w_ref, b_ref, g_ref, beta_ref, hc_ref, s_ref, ext_ref):
        i = pl.program_id(0)
        halo = ah_ref[...] * _sigmoid(glh_ref[...])
        ext_ref[0:HALO, :] = jnp.where(i > 0, halo, 0.0)
        ext_ref[HALO:, :] = a_ref[...] * _sigmoid(gl_ref[...])
        acc = jnp.zeros((tm, D), F32) + b_ref[...]
        for kk in range(CONV_K):
            acc = acc + w_ref[kk:kk + 1, :] * ext_ref[HALO - CONV_K + 1 + kk:HALO - CONV_K + 1 + kk + tm, :]
        hc_ref[...] = acc
        mu = jnp.mean(acc, axis=-1, keepdims=True)
        xc = acc - mu
        var = jnp.mean(xc * xc, axis=-1, keepdims=True)
        ln = xc * lax.rsqrt(var + EPS) * g_ref[...] + beta_ref[...]
        s_ref[...] = (ln * _sigmoid(ln)).astype(BF16)

    halo_map = lambda col: (lambda i: (jnp.maximum(i * per - 1, 0), col))
    return pl.pallas_call(
        body, out_shape=[jax.ShapeDtypeStruct((t, D), F32), jax.ShapeDtypeStruct((t, D), BF16)], grid=(t // tm,),
        in_specs=[_row_spec(tm, D, 0), _row_spec(tm, D, 1), pl.BlockSpec((HALO, D), halo_map(0)),
                  pl.BlockSpec((HALO, D), halo_map(1)), pl.BlockSpec((HALO, D), lambda i: (0, 0)),
                  _vec_spec(D), _vec_spec(D), _vec_spec(D)],
        out_specs=[_row_spec(tm, D), _row_spec(tm, D)], scratch_shapes=[pltpu.VMEM((tm + HALO, D), F32)],
        name=name, compiler_params=_params(("parallel",)),
    )(zu, zu, zu, zu, conv_w, conv_b, ln_g, ln_b)


def _conv_ln_bwd(name, hc, ds, ln_g, ln_b, tm=256):
    t = hc.shape[0]

    def body(hc_ref, ds_ref, g_ref, beta_ref, dhc_ref, dg_ref, dbeta_ref, dbias_ref):
        @pl.when(pl.program_id(0) == 0)
        def _():
            dg_ref[...] = jnp.zeros_like(dg_ref)
            dbeta_ref[...] = jnp.zeros_like(dbeta_ref)
            dbias_ref[...] = jnp.zeros_like(dbias_ref)

        hv = hc_ref[...]
        mu = jnp.mean(hv, axis=-1, keepdims=True)
        xc = hv - mu
        rstd = lax.rsqrt(jnp.mean(xc * xc, axis=-1, keepdims=True) + EPS)
        xh = xc * rstd
        ln = xh * g_ref[...] + beta_ref[...]
        sg = _sigmoid(ln)
        dln = ds_ref[...] * (sg * (1.0 + ln * (1.0 - sg)))
        dxh = dln * g_ref[...]
        dh = rstd * (dxh - jnp.mean(dxh, axis=-1, keepdims=True) - xh * jnp.mean(dxh * xh, axis=-1, keepdims=True))
        dhc_ref[...] = dh
        dg_ref[...] += jnp.sum(dln * xh, axis=0, keepdims=True)
        dbeta_ref[...] += jnp.sum(dln, axis=0, keepdims=True)
        dbias_ref[...] += jnp.sum(dh, axis=0, keepdims=True)

    vec = jax.ShapeDtypeStruct((1, D), F32)
    return pl.pallas_call(
        body, out_shape=[jax.ShapeDtypeStruct((t, D), F32), vec, vec, vec], grid=(t // tm,),
        in_specs=[_row_spec(tm, D), _row_spec(tm, D), _vec_spec(D), _vec_spec(D)],
        out_specs=[_row_spec(tm, D), _vec_spec(D), _vec_spec(D), _vec_spec(D)],
        name=name, compiler_params=_params(("arbitrary",)),
    )(hc, ds, ln_g, ln_b)


def _conv_bwd(name, zu, dhc, conv_w, tm=256):
    t = zu.shape[0]
    per = tm // HALO
    steps = t // tm

    def body(a_ref, gl_ref, ah_ref, glh_ref, d_ref, dn_ref, w_ref, dz_ref, dw_ref, ext_ref, dext_ref):
        i = pl.program_id(0)

        @pl.when(i == 0)
        def _():
            dw_ref[...] = jnp.zeros_like(dw_ref)

        av, sg = a_ref[...], _sigmoid(gl_ref[...])
        ext_ref[0:HALO, :] = jnp.where(i > 0, ah_ref[...] * _sigmoid(glh_ref[...]), 0.0)
        ext_ref[HALO:, :] = av * sg
        dv = d_ref[...]
        dext_ref[0:tm, :] = dv
        dext_ref[tm:, :] = jnp.where(i < steps - 1, dn_ref[...], 0.0)
        acc = jnp.zeros((tm, D), F32)
        for kk in range(CONV_K):
            acc = acc + w_ref[kk:kk + 1, :] * dext_ref[CONV_K - 1 - kk:CONV_K - 1 - kk + tm, :]
            prod = dv * ext_ref[HALO - CONV_K + 1 + kk:HALO - CONV_K + 1 + kk + tm, :]
            dw_ref[kk:kk + 1, :] += jnp.sum(prod, axis=0, keepdims=True)
        dz_ref[:, 0:D] = (acc * sg).astype(BF16)
        dz_ref[:, D:] = (acc * av * sg * (1.0 - sg)).astype(BF16)

    halo_map = lambda col: (lambda i: (jnp.maximum(i * per - 1, 0), col))
    return pl.pallas_call(
        body, out_shape=[jax.ShapeDtypeStruct((t, 2 * D), BF16), jax.ShapeDtypeStruct((HALO, D), F32)],
        grid=(steps,),
        in_specs=[_row_spec(tm, D, 0), _row_spec(tm, D, 1), pl.BlockSpec((HALO, D), halo_map(0)),
                  pl.BlockSpec((HALO, D), halo_map(1)), _row_spec(tm, D),
                  pl.BlockSpec((HALO, D), lambda i: (jnp.minimum((i + 1) * per, t // HALO - 1), 0)),
                  pl.BlockSpec((HALO, D), lambda i: (0, 0))],
        out_specs=[_row_spec(tm, 2 * D), pl.BlockSpec((HALO, D), lambda i: (0, 0))],
        scratch_shapes=[pltpu.VMEM((tm + HALO, D), F32), pltpu.VMEM((tm + HALO, D), F32)],
        name=name, compiler_params=_params(("arbitrary",)),
    )(zu, zu, zu, zu, dhc, dhc, conv_w)


def _ffn_fwd(tag, x, h, w, at, gate):
    def act(accs, extras, vecs):
        g, u = accs
        return [g, u, (g * _sigmoid(g)) * u]

    gv, uv, av = _mm(f"ffn_up_{tag}", [h], [(w["ffn_wg"], (TILE, *at)), (w["ffn_wu"], (TILE, *at))],
                     [(0, 0), (0, 1)], act, [BF16, BF16, BF16], tm=1024, tn=FSH, n_out=DFF, out_tiled=True)

    def residual(accs, extras, vecs):
        tot, = _total(accs, extras, vecs)
        return [extras[0] + (0.5 * vecs[0]) * tot, tot]

    shards = range(SHARDS)
    x_new, f = _mm(f"ffn_down_{tag}", [(av, (k,)) for k in shards], [(w["ffn_wd"], (k, *at)) for k in shards],
                   [(k, k) for k in shards], residual, [F32, BF16], tm=512, tn=512, n_out=D, extras=[(x, 0)],
                   vecs=[gate])
    return x_new, (gv, uv, f)


def _ffn_bwd(tag, dx, x, h, saved, w, at, g, scale, gate, into):
    gv, uv, f = saved
    df, dgate = _resgate_bwd(f"ffn_gate_bwd_{tag}", dx, f, gate, 0.5)

    def act(blocks):
        gf, uf = blocks[0].astype(F32), blocks[1].astype(F32)
        return ((gf * _sigmoid(gf)) * uf).astype(BF16)

    dwd = _mm_tn(f"ffn_dwd_{tag}", [gv, uv], df, tk=FSH, tn=1024, tt=1024, a_fn=act, a_tiled=True,
                 into=(into["ffn_wd"], at))

    def act_bwd(accs, extras, vecs):
        da = accs[0]
        gf, uf = extras[0].astype(F32), extras[1].astype(F32)
        sg = _sigmoid(gf)
        return [da * uf * (sg * (1.0 + gf * (1.0 - sg))), da * (gf * sg)]

    dg, du = _mm(f"ffn_da_{tag}", [df], [(w["ffn_wd"], (TILE, *at))], [(0, 0)], act_bwd, [BF16, BF16], tm=1024,
                 tn=FSH, n_out=DFF, trans_b=True, extras=[(gv, (TILE,)), (uv, (TILE,))], out_tiled=True)
    dwg = _mm_tn(f"ffn_dwg_{tag}", h, dg, tk=1024, tn=FSH, tt=1024, b_tiled=True, into=(into["ffn_wg"], at))
    dwu = _mm_tn(f"ffn_dwu_{tag}", h, du, tk=1024, tn=FSH, tt=1024, b_tiled=True, into=(into["ffn_wu"], at))
    shards = range(SHARDS)
    dh, = _mm(f"ffn_dh_{tag}", [(dg, (k,)) for k in shards] + [(du, (k,)) for k in shards],
              [(w["ffn_wg"], (k, *at)) for k in shards] + [(w["ffn_wu"], (k, *at)) for k in shards],
              [(k, k) for k in range(2 * SHARDS)], _total, [F32], tm=512, tn=512, n_out=D, trans_b=True)
    dx_in, dgn, dscale, dshift = _normmod_bwd(f"ffn_norm_bwd_{tag}", x, dh, dx, g, scale)
    return dx_in, dict(ffn_wg=dwg, ffn_wu=dwu, ffn_wd=dwd), dgn, (dshift, dscale, dgate)


def _mix_fwd(tag, x, h, w_in, attn_wo, conv_w, conv_b, ln_g, ln_b, conv_wo, w_out, gate, cos_t, sin_t):
    w_qkv, w_u, w_g = w_in[:, :QKV], w_in[:, QKV:QKV + 2 * D], w_in[:, QKV + 2 * D:]
    zqkv, = _mm(f"mix_qkv_{tag}", [h], [w_qkv], [(0, 0)], _first, [F32], tm=512, tn=768, n_out=QKV)
    zu, = _mm(f"mix_u_{tag}", [h], [w_u], [(0, 0)], _first, [F32], tm=512, tn=1024, n_out=2 * D)
    zg, = _mm(f"mix_g_{tag}", [h], [w_g], [(0, 0)], _first, [BF16], tm=512, tn=1024, n_out=2 * D)
    qkv = _rope_fwd(f"rope_{tag}", zqkv, cos_t, sin_t)
    n = len(DILATIONS)
    outs, lses = [], []
    for grp in range(n):
        o, lse = _attn_fwd(f"attn_fwd_{tag}_{grp}", qkv[grp], qkv[n + grp], qkv[2 * n + grp], grp)
        outs.append(o)
        lses.append(lse)
    ob, of, lj = _attn_merge(f"attn_merge_{tag}", outs, lses)
    hc, s = _conv_fwd(f"conv_fwd_{tag}", zu, conv_w, conv_b, ln_g, ln_b)

    def gated(accs, extras, vecs):
        ya, yc = accs
        return [_sigmoid(extras[0].astype(F32)) * ya + _sigmoid(extras[1].astype(F32)) * yc, ya, yc]

    y, ya, yc = _mm(f"mix_y_{tag}", [ob, s], [attn_wo, conv_wo], [(0, 0), (1, 1)], gated, [BF16, BF16, BF16],
                    tm=512, tn=1024, n_out=D, extras=[(zg, 0), (zg, 1)])

    def residual(accs, extras, vecs):
        return [extras[0] + vecs[0] * accs[0], accs[0]]

    x_new, f = _mm(f"mix_out_{tag}", [y], [w_out], [(0, 0)], residual, [F32, BF16], tm=512, tn=512, n_out=D,
                   extras=[(x, 0)], vecs=[gate])
    return x_new, (zu, zg, qkv, ob, of, lj, hc, s, y, ya, yc, f, (w_qkv, w_u, w_g))


def _mix_bwd(tag, dx, x, h, saved, attn_wo, conv_w, ln_g, ln_b, conv_wo, w_out, g, scale, gate, cos_t, sin_t):
    zu, zg, qkv, ob, of, lj, hc, s, y, ya, yc, f, w_parts = saved
    n = len(DILATIONS)
    df, dgate = _resgate_bwd(f"mix_gate_bwd_{tag}", dx, f, gate, 1.0)
    dw_out = _mm_tn(f"mix_dwout_{tag}", y, df, tk=1024, tn=1024, tt=512)

    def gated_bwd(accs, extras, vecs):
        dy = accs[0]
        sa, sc = _sigmoid(extras[0].astype(F32)), _sigmoid(extras[1].astype(F32))
        dga = dy * extras[2].astype(F32) * (sa * (1.0 - sa))
        dgc = dy * extras[3].astype(F32) * (sc * (1.0 - sc))
        return [dy * sa, dy * sc, jnp.concatenate([dga, dgc], axis=1)]

    dya, dyc, dzg = _mm(f"mix_dy_{tag}", [df], [w_out], [(0, 0)], gated_bwd, [BF16] * 3, tm=512, tn=1024,
                        n_out=D, trans_b=True, extras=[(zg, 0), (zg, 1), (ya, 0), (yc, 0)],
                        out_widths=[D, D, 2 * D])
    dw_attn = _mm_tn(f"mix_dwattn_{tag}", ob, dya, tk=GW, tn=1024, tt=512)
    dw_conv_o = _mm_tn(f"mix_dwconvo_{tag}", s, dyc, tk=1024, tn=1024, tt=512)
    do, = _mm(f"mix_do_{tag}", [dya], [attn_wo], [(0, 0)], _first, [F32], tm=512, tn=GW, n_out=GW, trans_b=True)
    ds, = _mm(f"mix_ds_{tag}", [dyc], [conv_wo], [(0, 0)], _first, [F32], tm=512, tn=1024, n_out=D, trans_b=True)

    dsum, dob = _attn_bwd_prep(f"attn_prep_{tag}", do, of)
    dqs, dks, dvs = [], [], []
    for grp in range(n):
        dq, dk, dv = _attn_bwd(f"attn_bwd_{tag}_{grp}", qkv[grp], qkv[n + grp], qkv[2 * n + grp], dob, lj, dsum,
                               grp)
        dqs.append(dq)
        dks.append(dk)
        dvs.append(dv)
    dzqkv = _rope_bwd(f"rope_bwd_{tag}", dqs + dks + dvs, cos_t, sin_t)

    dhc, dln_g, dln_b, dconv_b = _conv_ln_bwd(f"conv_ln_bwd_{tag}", hc, ds, ln_g, ln_b)
    dzu, dconv_w = _conv_bwd(f"conv_bwd_{tag}", zu, dhc, conv_w)

    dz_parts = [dzqkv, dzu, dzg]
    dw_in = jnp.concatenate(
        [_mm_tn(f"mix_dwin_{tag}_{i}", h, dzp, tk=1024, tn=dzp.shape[1] // 2, tt=512)
         for i, dzp in enumerate(dz_parts)], axis=1)
    dh, = _mm(f"mix_dh_{tag}", dz_parts, list(w_parts), [(0, 0), (1, 1), (2, 2)], _total, [F32], tm=512, tn=512,
              n_out=D, trans_b=True)
    dx_in, dgn, dscale, dshift = _normmod_bwd(f"mix_norm_bwd_{tag}", x, dh, dx, g, scale)
    grads = dict(w_in=dw_in, attn_wo=dw_attn, conv_w=dconv_w[:CONV_K], conv_b=dconv_b, conv_ln_g=dln_g,
                 conv_ln_b=dln_b, conv_wo=dw_conv_o, w_out=dw_out)
    return dx_in, grads, dgn, (dshift, dscale, dgate)


def _local_step(x, c, target, w, wf):
    t = x.shape[0]
    cos_t, sin_t = _rope_tables(t)
    c8 = jnp.concatenate([c, jnp.zeros((7, D), F32)], axis=0)
    row = lambda v: v.reshape(1, -1)
    conv_w_pad = jnp.concatenate([wf["conv_w"], jnp.zeros((DEPTH, HALO - CONV_K, D), F32)], axis=1)

    saved = []
    for l in range(DEPTH):
        mod = _mod_fwd(f"mod_{l}", c8, w["ada_w"], l, row(wf["ada_b"][l]))[0:1]
        mods = [mod[:, i * D:(i + 1) * D] for i in range(N_MOD)]
        gains = [row(wf["norm_g"][l, i]) for i in range(3)]
        lay = dict(mods=mods, gains=gains)

        lay["x0"] = x
        lay["h0"] = _normmod_fwd(f"norm_a_{l}", x, gains[0], mods[1], mods[0])
        x, lay["ffn0"] = _ffn_fwd(f"a_{l}", x, lay["h0"], w, (l, 0), mods[2])
        lay["x1"] = x
        lay["h1"] = _normmod_fwd(f"norm_m_{l}", x, gains[1], mods[4], mods[3])
        x, lay["mix"] = _mix_fwd(f"{l}", x, lay["h1"], w["w_in"][l], w["attn_wo"][l], conv_w_pad[l],
                                 row(wf["conv_b"][l]), row(wf["conv_ln_g"][l]), row(wf["conv_ln_b"][l]),
                                 w["conv_wo"][l], w["w_out"][l], mods[5], cos_t, sin_t)
        lay["x2"] = x
        lay["h2"] = _normmod_fwd(f"norm_b_{l}", x, gains[2], mods[7], mods[6])
        x, lay["ffn1"] = _ffn_fwd(f"b_{l}", x, lay["h2"], w, (l, 1), mods[8])
        saved.append(lay)

    dx, dfinal_g, loss_cols = _loss_bwd("loss_head", x, target, row(wf["final_g"]))

    ffn_grads = {n: jnp.zeros(w[n].shape, F32) for n in ("ffn_wg", "ffn_wu", "ffn_wd")}
    per_layer = []
    for l in reversed(range(DEPTH)):
        lay = saved[l]
        mods, gains = lay["mods"], lay["gains"]
        dx, ffn_grads, dgn2, dmod2 = _ffn_bwd(f"b_{l}", dx, lay["x2"], lay["h2"], lay["ffn1"], w, (l, 1),
                                              gains[2], mods[7], mods[8], ffn_grads)
        dx, gm, dgn1, dmod1 = _mix_bwd(f"{l}", dx, lay["x1"], lay["h1"], lay["mix"], w["attn_wo"][l],
                                       conv_w_pad[l], row(wf["conv_ln_g"][l]), row(wf["conv_ln_b"][l]),
                                       w["conv_wo"][l], w["w_out"][l], gains[1], mods[4], mods[5], cos_t, sin_t)
        dx, ffn_grads, dgn0, dmod0 = _ffn_bwd(f"a_{l}", dx, lay["x0"], lay["h0"], lay["ffn0"], w, (l, 0),
                                              gains[0], mods[1], mods[2], ffn_grads)
        g = dict(gm)
        g["dmod"] = jnp.concatenate(list(dmod0) + list(dmod1) + list(dmod2), axis=1)
        g["ada_b"] = g["dmod"][0]
        g["norm_g"] = [dgn0[0], dgn1[0], dgn2[0]]
        for name in ("conv_b", "conv_ln_g", "conv_ln_b"):
            g[name] = g[name][0]
        per_layer.append(g)
    per_layer.reverse()
    grads = {name: [per_layer[l][name] for l in range(DEPTH)] for name in per_layer[0]}
    grads["ada_w"] = _mod_bwd("mod_bwd", c.reshape(D, 1), jnp.stack(grads.pop("dmod")))
    grads.update(ffn_grads)
    grads["final_g"] = dfinal_g[0]
    return loss_cols, dx, grads


def _split_bits(w):
    bits = lax.bitcast_convert_type(w, jnp.uint32)
    hi = lax.bitcast_convert_type((bits >> 16).astype(jnp.uint16), BF16)
    lo = lax.bitcast_convert_type((bits & 0xFFFF).astype(jnp.uint16), BF16)
    return hi, lo


def _join_bits(hi, lo):
    h = lax.bitcast_convert_type(hi, jnp.uint16).astype(jnp.uint32)
    l = lax.bitcast_convert_type(lo, jnp.uint16).astype(jnp.uint32)
    return lax.bitcast_convert_type((h << 16) | l, F32)


def _pack(parts, rows):
    out = []
    for p in parts:
        flat = p.reshape(-1)
        pad = -flat.shape[0] % LANES
        out.append(jnp.concatenate([flat, jnp.zeros((pad,), flat.dtype)]) if pad else flat)
    flat = jnp.concatenate(out)
    return jnp.concatenate([flat, jnp.zeros((rows * LANES - flat.shape[0],), flat.dtype)]).reshape(rows, LANES)


def _unpack(buf, shapes):
    out, row = [], 0
    for shape in shapes:
        size = 1
        for s in shape:
            size *= s
        rows = -(-size // LANES)
        out.append(buf[row:row + rows].reshape(-1)[:size].reshape(shape))
        row += rows
    return out


def _place():
    x, y, c = lax.axis_index("x"), lax.axis_index("y"), lax.axis_index("c")
    chips = [(1 - x, y), (x, 1 - y), (1 - x, 1 - y)]
    return x, y, c, chips


def _chip_index():
    return (2 * lax.axis_index("x") + lax.axis_index("y")).astype(jnp.int32)


HBM_SPEC = pl.BlockSpec(memory_space=pltpu.HBM)


def _gather_weights(arrays):
    n = len(arrays)

    def body(*refs):
        outs, send_sems, recv_sems = refs[n:2 * n], refs[2 * n], refs[2 * n + 1]
        x, y, c, chips = _place()
        me = 2 * x + y
        sibling = (x, y, 1 - c)
        there = [2 * chip[0] + chip[1] for chip in chips]

        def copy(a, k, chip, layer, to):
            piece = outs[a].at[chip, layer]
            return pltpu.make_async_remote_copy(
                src_ref=piece, dst_ref=piece, send_sem=send_sems.at[6 * a + k], recv_sem=recv_sems.at[6 * a + k],
                device_id=to, device_id_type=MESH)

        first = [copy(a, j, me, c, (*chip, c)) for a in range(n) for j, chip in enumerate(chips)]
        for cp in first:
            cp.start()
        passed = []
        for a in range(n):
            for j in range(3):
                copy(a, j, there[j], c, sibling).wait_recv()
                passed.append(copy(a, 3 + j, there[j], c, sibling))
                passed[-1].start()
        for a in range(n):
            for j in range(3):
                copy(a, 3 + j, there[j], 1 - c, sibling).wait_recv()
        for cp in first + passed:
            cp.wait_send()

    return pl.pallas_call(
        body, out_shape=[jax.ShapeDtypeStruct(a.shape, a.dtype) for a in arrays],
        in_specs=[HBM_SPEC] * n, out_specs=[HBM_SPEC] * n,
        scratch_shapes=[pltpu.SemaphoreType.DMA((6 * n,)), pltpu.SemaphoreType.DMA((6 * n,))],
        input_output_aliases={i: i for i in range(n)}, name="gather_weights",
    )(*arrays)


def _row_block(rows, cols):
    for cand in (512, 256, 128, 64, 32, 16):
        if rows % cand == 0 and cand * cols * 4 <= 2560 * 1024:
            return cand
    return rows


def _swap_layers(grads):
    n = len(grads)

    def body(*refs):
        g_refs, out_refs, send_sems, recv_sems = refs[:n], refs[n:2 * n], refs[2 * n], refs[2 * n + 1]
        x, y, c, _ = _place()
        copies = [pltpu.make_async_remote_copy(
            src_ref=g_refs[a].at[:, 1 - c], dst_ref=out_refs[a], send_sem=send_sems.at[a], recv_sem=recv_sems.at[a],
            device_id=(x, y, 1 - c), device_id_type=MESH) for a in range(n)]
        for cp in copies:
            cp.start()
        for cp in copies:
            cp.wait()

    return pl.pallas_call(
        body, out_shape=[jax.ShapeDtypeStruct((g.shape[0],) + g.shape[2:], F32) for g in grads],
        in_specs=[HBM_SPEC] * n, out_specs=[HBM_SPEC] * n,
        scratch_shapes=[pltpu.SemaphoreType.DMA((n,)), pltpu.SemaphoreType.DMA((n,))], name="swap_layers",
    )(*grads)


def _add_layers(name, grad, other):
    shards, _, rows, cols = grad.shape
    tr = _row_block(rows, cols)

    def body(c_ref, g_ref, o_ref, out_ref):
        out_ref[...] = (g_ref[...] + o_ref[...]).astype(BF16)

    c = lax.axis_index("c").astype(jnp.int32).reshape(1)
    grid_spec = pltpu.PrefetchScalarGridSpec(
        num_scalar_prefetch=1, grid=(shards, rows // tr),
        in_specs=[pl.BlockSpec((None, None, tr, cols), lambda k, i, c_ref: (k, c_ref[0], i, 0)),
                  pl.BlockSpec((None, tr, cols), lambda k, i, c_ref: (k, i, 0))],
        out_specs=pl.BlockSpec((None, tr, cols), lambda k, i, c_ref: (k, i, 0)))
    return pl.pallas_call(
        body, out_shape=jax.ShapeDtypeStruct((shards, rows, cols), BF16), grid_spec=grid_spec,
        name=name, compiler_params=_params(("parallel", "parallel")),
    )(c, grad, other)


def _scatter_chips(parts):
    n = len(parts)

    def body(*refs):
        p_refs, out_refs, send_sems, recv_sems = refs[:n], refs[n:2 * n], refs[2 * n], refs[2 * n + 1]
        x, y, c, chips = _place()
        me = 2 * x + y
        there = [2 * chip[0] + chip[1] for chip in chips]

        def copy(a, j, slot):
            return pltpu.make_async_remote_copy(
                src_ref=p_refs[a].at[there[j]], dst_ref=out_refs[a].at[slot], send_sem=send_sems.at[3 * a + j],
                recv_sem=recv_sems.at[3 * a + j], device_id=(*chips[j], c), device_id_type=MESH)

        sends = [copy(a, j, me) for a in range(n) for j in range(3)]
        for cp in sends:
            cp.start()
        for a in range(n):
            for j in range(3):
                copy(a, j, there[j]).wait_recv()
        for cp in sends:
            cp.wait_send()

    return pl.pallas_call(
        body, out_shape=[jax.ShapeDtypeStruct(p.shape, p.dtype) for p in parts],
        in_specs=[HBM_SPEC] * n, out_specs=[HBM_SPEC] * n,
        scratch_shapes=[pltpu.SemaphoreType.DMA((3 * n,)), pltpu.SemaphoreType.DMA((3 * n,))],
        name="scatter_chips",
    )(*parts)


def _add_chips(name, part, others):
    shards, rows, cols = part.shape
    tr = _row_block(rows, cols)

    def body(pos_ref, own_ref, r0_ref, r1_ref, r2_ref, r3_ref, out_ref):
        me = pos_ref[0]
        own = own_ref[...].astype(F32)
        total = None
        for k, r_ref in enumerate((r0_ref, r1_ref, r2_ref, r3_ref)):
            term = jnp.where(me == k, own, r_ref[...].astype(F32))
            total = term if total is None else total + term
        out_ref[...] = total

    def other(k):
        return pl.BlockSpec((None, tr, cols),
                            lambda i, pos, k=k: (jnp.where(pos[0] == k, (k + 1) % shards, k), i, 0))

    pos = jnp.stack([_chip_index(), lax.axis_index("c").astype(jnp.int32)])
    grid_spec = pltpu.PrefetchScalarGridSpec(
        num_scalar_prefetch=1, grid=(rows // tr,),
        in_specs=[pl.BlockSpec((None, tr, cols), lambda i, pos: (pos[0], i, 0))] + [other(k) for k in range(shards)],
        out_specs=pl.BlockSpec((None, tr, cols), lambda i, pos: (pos[1], i, 0)))
    return pl.pallas_call(
        body, out_shape=jax.ShapeDtypeStruct((DEPTH, rows, cols), F32), grid_spec=grid_spec,
        name=name, compiler_params=_params(("parallel",)),
    )(pos, part, others, others, others, others)


def _join_layers(arrays):
    n = len(arrays)

    def body(*refs):
        outs, send_sems, recv_sems = refs[n:2 * n], refs[2 * n], refs[2 * n + 1]
        x, y, c, _ = _place()

        def copy(a, layer):
            piece = outs[a].at[layer]
            return pltpu.make_async_remote_copy(src_ref=piece, dst_ref=piece, send_sem=send_sems.at[a],
                                                recv_sem=recv_sems.at[a], device_id=(x, y, 1 - c),
                                                device_id_type=MESH)

        sends = [copy(a, c) for a in range(n)]
        for cp in sends:
            cp.start()
        for a in range(n):
            copy(a, 1 - c).wait_recv()
        for cp in sends:
            cp.wait_send()

    return pl.pallas_call(
        body, out_shape=[jax.ShapeDtypeStruct(a.shape, a.dtype) for a in arrays],
        in_specs=[HBM_SPEC] * n, out_specs=[HBM_SPEC] * n,
        scratch_shapes=[pltpu.SemaphoreType.DMA((n,)), pltpu.SemaphoreType.DMA((n,))],
        input_output_aliases={i: i for i in range(n)}, name="join_layers",
    )(*arrays)


def _reduce_scatter(grads):
    sums = [_add_layers(f"add_layers_{a}", g, o) for a, (g, o) in enumerate(zip(grads, _swap_layers(grads)))]
    others = _scatter_chips(sums)
    return _join_layers([_add_chips(f"add_chips_{a}", p, o) for a, (p, o) in enumerate(zip(sums, others))])


def _adamw(name, w, g, m, v):
    shape = w.shape
    cols = shape[-1]
    rows = w.size // cols
    tr = rows
    for cand in (512, 256, 128, 64, 32, 16, 8):
        if rows % cand == 0 and cand * cols * 4 <= 4 * 1024 * 1024:
            tr = cand
            break

    def body(w_ref, g_ref, m_ref, v_ref, d_ref, nm_ref, nv_ref):
        gv = g_ref[...]
        nm = ADAM_B1 * m_ref[...] + (1.0 - ADAM_B1) * gv
        nv = ADAM_B2 * v_ref[...] + (1.0 - ADAM_B2) * (gv * gv)
        m_hat = nm / (1.0 - ADAM_B1 ** ADAM_STEP)
        v_hat = nv / (1.0 - ADAM_B2 ** ADAM_STEP)
        d_ref[...] = -ADAM_LR * (m_hat / (jnp.sqrt(v_hat) + ADAM_EPS) + ADAM_WD * w_ref[...])
        nm_ref[...] = nm
        nv_ref[...] = nv

    spec = pl.BlockSpec((tr, cols), lambda i: (i, 0))
    two = lambda a: a.reshape(rows, cols)
    outs = pl.pallas_call(
        body, out_shape=[jax.ShapeDtypeStruct((rows, cols), F32)] * 3, grid=(rows // tr,),
        in_specs=[spec] * 4, out_specs=[spec] * 3, name=name, compiler_params=_params(("parallel",)),
    )(two(w), two(g), two(m), two(v))
    return [o.reshape(shape) for o in outs]


BIG = ("ada_w", "ffn_wg", "ffn_wu", "ffn_wd", "w_in", "conv_wo", "w_out")
MISC_ROWS = 96


def _own_slot(shard):
    return lax.dynamic_update_slice(jnp.zeros((SHARDS,) + shard.shape, shard.dtype), shard[None],
                                    (_chip_index(),) + (0,) * shard.ndim)


def _as_matrices(a):
    return a.reshape(a.shape[0], a.shape[1], -1, a.shape[-1])


def kernel(x, c, ada_w, ada_b, norm_g, ffn_wg, ffn_wu, ffn_wd, w_in, attn_wo, conv_w, conv_b, conv_ln_g, conv_ln_b, conv_wo, w_out, final_g, loss_target, m_ada_w, m_ada_b, m_norm_g, m_ffn_wg, m_ffn_wu, m_ffn_wd, m_w_in, m_attn_wo, m_conv_w, m_conv_b, m_conv_ln_g, m_conv_ln_b, m_conv_wo, m_w_out, m_final_g, v_ada_w, v_ada_b, v_norm_g, v_ffn_wg, v_ffn_wu, v_ffn_wd, v_w_in, v_attn_wo, v_conv_w, v_conv_b, v_conv_ln_g, v_conv_ln_b, v_conv_wo, v_w_out, v_final_g):
    weights = dict(ada_w=ada_w, ada_b=ada_b, norm_g=norm_g, ffn_wg=ffn_wg, ffn_wu=ffn_wu, ffn_wd=ffn_wd, w_in=w_in,
                   attn_wo=attn_wo, conv_w=conv_w, conv_b=conv_b, conv_ln_g=conv_ln_g, conv_ln_b=conv_ln_b,
                   conv_wo=conv_wo, w_out=w_out, final_g=final_g)
    moments_m = dict(ada_w=m_ada_w, ada_b=m_ada_b, norm_g=m_norm_g, ffn_wg=m_ffn_wg, ffn_wu=m_ffn_wu,
                     ffn_wd=m_ffn_wd, w_in=m_w_in, attn_wo=m_attn_wo, conv_w=m_conv_w, conv_b=m_conv_b,
                     conv_ln_g=m_conv_ln_g, conv_ln_b=m_conv_ln_b, conv_wo=m_conv_wo, w_out=m_w_out,
                     final_g=m_final_g)
    moments_v = dict(ada_w=v_ada_w, ada_b=v_ada_b, norm_g=v_norm_g, ffn_wg=v_ffn_wg, ffn_wu=v_ffn_wu,
                     ffn_wd=v_ffn_wd, w_in=v_w_in, attn_wo=v_attn_wo, conv_w=v_conv_w, conv_b=v_conv_b,
                     conv_ln_g=v_conv_ln_g, conv_ln_b=v_conv_ln_b, conv_wo=v_conv_wo, w_out=v_w_out,
                     final_g=v_final_g)
    layers, shards = range(DEPTH), range(SHARDS)

    bits = {n: _split_bits(weights[n]) for n in EXACT}
    misc_w = jnp.stack([_pack([attn_wo[l].astype(BF16), bits["norm_g"][0][l], bits["norm_g"][1][l],
                               bits["conv_w"][0][l], bits["conv_w"][1][l]], MISC_ROWS) for l in layers])
    sent = [_own_slot(weights[n].astype(BF16)) for n in BIG] + [_own_slot(misc_w)]
    got = dict(zip(BIG + ("misc",), _gather_weights(sent)))
    w = {n: got[n] for n in ("ada_w", "ffn_wg", "ffn_wu", "ffn_wd")}
    w["w_in"] = got["w_in"].transpose(1, 2, 0, 3).reshape(DEPTH, D, -1)
    for n in ("conv_wo", "w_out"):
        w[n] = got[n].transpose(1, 0, 2, 3).reshape(DEPTH, D, D)
    misc_shapes = [(GW, GW), (3, GW), (3, GW), (CONV_K, GW), (CONV_K, GW)]
    pieces = [[_unpack(got["misc"][k, l], misc_shapes) for k in shards] for l in layers]
    whole = lambda i: jnp.stack([jnp.concatenate([pieces[l][k][i] for k in shards], axis=1) for l in layers])
    w["attn_wo"] = whole(0)
    vectors = dict(ada_b=ada_b, conv_b=conv_b, conv_ln_g=conv_ln_g, conv_ln_b=conv_ln_b, final_g=final_g,
                   norm_g=_join_bits(whole(1), whole(2)), conv_w=_join_bits(whole(3), whole(4)))

    loss_cols, dx, grads = _local_step(x[0], c, loss_target[0], w, vectors)
    loss = lax.psum(jnp.sum(loss_cols), ("x", "y", "c"))

    cols_of = lambda a, k: a[..., k * GW:(k + 1) * GW]
    misc_g = jnp.stack([jnp.stack([_pack(
        [cols_of(grads["attn_wo"][l], k), cols_of(jnp.stack(grads["norm_g"][l]), k), cols_of(grads["conv_w"][l], k),
         grads["ada_b"][l], grads["conv_b"][l], grads["conv_ln_g"][l], grads["conv_ln_b"][l],
         grads["final_g"] if l == 0 else jnp.zeros_like(grads["final_g"])], MISC_ROWS)
        for l in layers]) for k in shards])
    by_chip = dict(
        ada_w=grads["ada_w"], ffn_wg=grads["ffn_wg"], ffn_wu=grads["ffn_wu"], ffn_wd=grads["ffn_wd"],
        w_in=jnp.stack(grads["w_in"]).reshape(DEPTH, D, SHARDS, -1).transpose(2, 0, 1, 3),
        conv_wo=jnp.stack(grads["conv_wo"]).reshape(DEPTH, SHARDS, -1, D).transpose(1, 0, 2, 3),
        w_out=jnp.stack(grads["w_out"]).reshape(DEPTH, SHARDS, -1, D).transpose(1, 0, 2, 3))
    reduced = _reduce_scatter([_as_matrices(by_chip[n]) for n in BIG] + [misc_g])
    summed = {n: r.reshape(weights[n].shape) for n, r in zip(BIG, reduced)}
    small_shapes = [(GW, GW), (3, GW), (CONV_K, GW), (N_MOD * D,), (D,), (D,), (D,), (D,)]
    small = [_unpack(reduced[-1][l], small_shapes) for l in layers]
    for i, n in enumerate(("attn_wo", "norm_g", "conv_w", "ada_b", "conv_b", "conv_ln_g", "conv_ln_b")):
        summed[n] = jnp.stack([small[l][i] for l in layers])
    summed["final_g"] = small[0][7]

    deltas, new_m, new_v = {}, {}, {}
    for n in WEIGHTS:
        deltas[n], new_m[n], new_v[n] = _adamw(f"adamw_{n}", weights[n], summed[n], moments_m[n], moments_v[n])

    return (loss, dx[None], *[summed[n] for n in WEIGHTS], *[deltas[n] for n in WEIGHTS],
            *[new_m[n] for n in WEIGHTS], *[new_v[n] for n in WEIGHTS])
```

```python
import functools

import jax
import jax.numpy as jnp
from jax import lax
from jax.experimental import pallas as pl
from jax.experimental.pallas import tpu as pltpu

F32 = jnp.float32
BF16 = jnp.bfloat16

D = 1024
DFF = 2816
HEAD = 64
GW = 256
DILATIONS = (1, 4, 16)
BAND = 128
QKV = 2304
CONV_K = 31
HALO = 32
N_MOD = 9
EPS = 1e-6
NEG_INF = -1e30
DEPTH = 2

SHARDS = 4
FSH = DFF // SHARDS
LANES = 1024
VL = 128

ADAM_LR = 0.001
ADAM_B1 = 0.9
ADAM_B2 = 0.999
ADAM_EPS = 1e-08
ADAM_WD = 0.01
ADAM_STEP = 10

VMEM_LIMIT = 56 * 1024 * 1024

EXACT = ("norm_g", "conv_w")
WEIGHTS = ("ada_w", "ada_b", "norm_g", "ffn_wg", "ffn_wu", "ffn_wd", "w_in", "attn_wo", "conv_w", "conv_b",
           "conv_ln_g", "conv_ln_b", "conv_wo", "w_out", "final_g")

MESH = pl.DeviceIdType.MESH


def _params(sem=None):
    return pltpu.CompilerParams(dimension_semantics=sem, vmem_limit_bytes=VMEM_LIMIT)


def _sigmoid(v):
    return jax.nn.sigmoid(v)


TILE = "tile"
MM_SLAB = 512


def _lead_spec(arr, lead, block, index):
    def index_map(j, i):
        return (*[j if e == TILE else e for e in lead], *index(j, i))
    return pl.BlockSpec((None,) * len(lead) + tuple(block), index_map)


def _entry(e):
    return e if isinstance(e, tuple) else (e, ())


def _mm(name, a_list, b_list, pairs, epilogue, out_dtypes, *, tm, tn, n_out, trans_b=False, extras=(), vecs=(),
        out_widths=None, out_tiled=False):
    a_list = [_entry(a) for a in a_list]
    b_list = [_entry(b) for b in b_list]
    m = a_list[0][0].shape[-2]
    na, nb, ne, nv = len(a_list), len(b_list), len(extras), len(vecs)
    dn = (((1,), (1,)), ((), ())) if trans_b else (((1,), (0,)), ((), ()))

    def body(*refs):
        a_refs = refs[:na]
        b_refs = refs[na:na + nb]
        e_refs = refs[na + nb:na + nb + ne]
        v_refs = refs[na + nb + ne:na + nb + ne + nv]
        o_refs = refs[na + nb + ne + nv:]
        vec_blocks = [v[...] for v in v_refs]
        for r0 in range(0, tm, min(tm, MM_SLAB)):
            rows = slice(r0, r0 + min(tm, MM_SLAB))
            accs = [lax.dot_general(a_refs[ai][rows, :], b_refs[bi][...], dn, preferred_element_type=F32)
                    for ai, bi in pairs]
            res = epilogue(accs, [e[rows, :] for e in e_refs], vec_blocks)
            for o_ref, r in zip(o_refs, res):
                o_ref[rows, :] = r.astype(o_ref.dtype)

    in_specs = [_lead_spec(a, lead, (tm, a.shape[-1]), lambda j, i: (i, 0)) for a, lead in a_list]
    for b, lead in b_list:
        if TILE in lead:
            in_specs.append(_lead_spec(b, lead, b.shape[-2:], lambda j, i: (0, 0)))
        elif trans_b:
            in_specs.append(_lead_spec(b, lead, (tn, b.shape[-1]), lambda j, i: (j, 0)))
        else:
            in_specs.append(_lead_spec(b, lead, (b.shape[-2], tn), lambda j, i: (0, j)))
    for e, where in extras:
        if isinstance(where, tuple):
            in_specs.append(_lead_spec(e, where, (tm, tn), lambda j, i: (i, 0)))
        else:
            in_specs.append(pl.BlockSpec((tm, tn), functools.partial(lambda j, i, off: (i, j + off), off=where)))
    in_specs += [pl.BlockSpec((1, tn), lambda j, i: (0, j)) for _ in vecs]
    nj = n_out // tn
    widths = out_widths or [tn] * len(out_dtypes)
    if out_tiled:
        out_specs = [pl.BlockSpec((None, tm, tn), lambda j, i: (j, i, 0)) for _ in out_dtypes]
        out_shape = [jax.ShapeDtypeStruct((nj, m, tn), dt) for dt in out_dtypes]
    else:
        out_specs = [pl.BlockSpec((tm, wd), lambda j, i: (i, j)) for wd in widths]
        out_shape = [jax.ShapeDtypeStruct((m, nj * wd), dt) for dt, wd in zip(out_dtypes, widths)]
    return pl.pallas_call(
        body, out_shape=out_shape, grid=(nj, m // tm), in_specs=in_specs, out_specs=out_specs,
        name=name, compiler_params=_params(("parallel", "parallel")),
    )(*[a for a, _ in a_list], *[b for b, _ in b_list], *[e for e, _ in extras], *vecs)


def _mm_tn(name, a, b, *, tk, tn, tt, a_fn=None, a_tiled=False, b_tiled=False, into=None):
    a_list = list(a) if a_fn is not None else [a]
    na = len(a_list)
    t = a_list[0].shape[-2]
    nk = a_list[0].shape[0] if a_tiled else a_list[0].shape[1] // tk
    nn = b.shape[0] if b_tiled else b.shape[1] // tn
    steps = t // tt
    has_into = into is not None

    def body(*refs):
        refs = refs[1:] if has_into else refs
        a_refs, b_ref, o_ref, acc_ref = refs[:na], refs[na], refs[na + 1], refs[na + 2]
        s = pl.program_id(2)

        @pl.when(s == 0)
        def _():
            acc_ref[...] = jnp.zeros_like(acc_ref)

        av = a_refs[0][...] if a_fn is None else a_fn([r[...] for r in a_refs])
        acc_ref[...] += lax.dot_general(av, b_ref[...], (((0,), (0,)), ((), ())), preferred_element_type=F32)

        @pl.when(s == steps - 1)
        def _():
            o_ref[...] = acc_ref[...]

    a_spec = (pl.BlockSpec((None, tt, tk), lambda i, j, s: (i, s, 0)) if a_tiled
              else pl.BlockSpec((tt, tk), lambda i, j, s: (s, i)))
    b_spec = (pl.BlockSpec((None, tt, tn), lambda i, j, s: (j, s, 0)) if b_tiled
              else pl.BlockSpec((tt, tn), lambda i, j, s: (s, j)))
    if a_tiled:
        out_dims, tile_index = (nk, tk, nn * tn), lambda i, j, s: (i, 0, j)
    elif b_tiled:
        out_dims, tile_index = (nn, nk * tk, tn), lambda i, j, s: (j, i, 0)
    else:
        out_dims, tile_index = (nk * tk, nn * tn), lambda i, j, s: (i, j)
    tiled = a_tiled or b_tiled
    if has_into:
        buf, lead = into
        def out_index(i, j, s):
            idx = tile_index(i, j, s)
            return (idx[0], *lead, *idx[1:])
        out_spec = pl.BlockSpec((None,) * (1 + len(lead)) + (tk, tn), out_index)
        out_shape = jax.ShapeDtypeStruct(buf.shape, buf.dtype)
        extra_in, extra_specs, aliases = [buf], [pl.BlockSpec(memory_space=pl.ANY)], {0: 0}
    else:
        out_spec = pl.BlockSpec(((None,) if tiled else ()) + (tk, tn), tile_index)
        out_shape = jax.ShapeDtypeStruct(out_dims, F32)
        extra_in, extra_specs, aliases = [], [], {}
    return pl.pallas_call(
        body, out_shape=out_shape, grid=(nk, nn, steps), in_specs=extra_specs + [a_spec] * na + [b_spec],
        out_specs=out_spec, scratch_shapes=[pltpu.VMEM((tk, tn), F32)], input_output_aliases=aliases, name=name,
        compiler_params=_params(("parallel", "parallel", "arbitrary")),
    )(*extra_in, *a_list, b)


def _first(accs, extras, vecs):
    return [accs[0]]


def _total(accs, extras, vecs):
    out = accs[0]
    for r in accs[1:]:
        out = out + r
    return [out]


def _row_spec(tm, width, col=0):
    return pl.BlockSpec((tm, width), functools.partial(lambda i, col: (i, col), col=col))


def _vec_spec(width):
    return pl.BlockSpec((1, width), lambda i: (0, 0))


def _normmod_fwd(name, x, g, scale, shift, tm=512):
    t = x.shape[0]

    def body(x_ref, g_ref, sc_ref, sh_ref, h_ref):
        xv = x_ref[...]
        r = lax.rsqrt(jnp.mean(xv * xv, axis=-1, keepdims=True) + EPS)
        h_ref[...] = ((xv * r) * g_ref[...] * (1.0 + sc_ref[...]) + sh_ref[...]).astype(BF16)

    return pl.pallas_call(
        body, out_shape=jax.ShapeDtypeStruct((t, D), BF16), grid=(t // tm,),
        in_specs=[_row_spec(tm, D), _vec_spec(D), _vec_spec(D), _vec_spec(D)], out_specs=_row_spec(tm, D),
        name=name, compiler_params=_params(("parallel",)),
    )(x, g, scale, shift)


def _normmod_bwd(name, x, dh, dres, g, scale, tm=256):
    t = x.shape[0]
    steps = t // tm

    def body(x_ref, dh_ref, dres_ref, g_ref, sc_ref, dx_ref, dg_ref, dsc_ref, dsh_ref):
        i = pl.program_id(0)

        @pl.when(i == 0)
        def _():
            dg_ref[...] = jnp.zeros_like(dg_ref)
            dsh_ref[...] = jnp.zeros_like(dsh_ref)

        xv = x_ref[...]
        dh = dh_ref[...]
        r = lax.rsqrt(jnp.mean(xv * xv, axis=-1, keepdims=True) + EPS)
        xh = xv * r
        dxh = dh * (g_ref[...] * (1.0 + sc_ref[...]))
        dx_ref[...] = dres_ref[...] + r * (dxh - xh * jnp.mean(dxh * xh, axis=-1, keepdims=True))
        dg_ref[...] += jnp.sum(dh * xh, axis=0, keepdims=True)
        dsh_ref[...] += jnp.sum(dh, axis=0, keepdims=True)

        @pl.when(i == steps - 1)
        def _():
            acc = dg_ref[...]
            dg_ref[...] = acc * (1.0 + sc_ref[...])
            dsc_ref[...] = acc * g_ref[...]

    vec = jax.ShapeDtypeStruct((1, D), F32)
    return pl.pallas_call(
        body, out_shape=[jax.ShapeDtypeStruct((t, D), F32), vec, vec, vec], grid=(steps,),
        in_specs=[_row_spec(tm, D), _row_spec(tm, D), _row_spec(tm, D), _vec_spec(D), _vec_spec(D)],
        out_specs=[_row_spec(tm, D), _vec_spec(D), _vec_spec(D), _vec_spec(D)],
        name=name, compiler_params=_params(("arbitrary",)),
    )(x, dh, dres, g, scale)


def _resgate_bwd(name, dx, f, gate, coef, tm=512):
    t = dx.shape[0]

    def body(dx_ref, f_ref, gate_ref, df_ref, dgate_ref):
        @pl.when(pl.program_id(0) == 0)
        def _():
            dgate_ref[...] = jnp.zeros_like(dgate_ref)

        dxv = dx_ref[...]
        df_ref[...] = ((coef * gate_ref[...]) * dxv).astype(BF16)
        dgate_ref[...] += jnp.sum((coef * f_ref[...].astype(F32)) * dxv, axis=0, keepdims=True)

    return pl.pallas_call(
        body, out_shape=[jax.ShapeDtypeStruct((t, D), BF16), jax.ShapeDtypeStruct((1, D), F32)], grid=(t // tm,),
        in_specs=[_row_spec(tm, D), _row_spec(tm, D), _vec_spec(D)], out_specs=[_row_spec(tm, D), _vec_spec(D)],
        name=name, compiler_params=_params(("arbitrary",)),
    )(dx, f, gate)


def _loss_bwd(name, x, target, g, tm=256):
    t = x.shape[0]

    def body(x_ref, t_ref, g_ref, dx_ref, dg_ref, loss_ref):
        @pl.when(pl.program_id(0) == 0)
        def _():
            dg_ref[...] = jnp.zeros_like(dg_ref)
            loss_ref[...] = jnp.zeros_like(loss_ref)

        xv = x_ref[...]
        r = lax.rsqrt(jnp.mean(xv * xv, axis=-1, keepdims=True) + EPS)
        xh = xv * r
        err = xh * g_ref[...] - t_ref[...]
        dy = err * (1.0 / D)
        dxh = dy * g_ref[...]
        dx_ref[...] = r * (dxh - xh * jnp.mean(dxh * xh, axis=-1, keepdims=True))
        dg_ref[...] += jnp.sum(dy * xh, axis=0, keepdims=True)
        loss_ref[...] += jnp.sum(err * err, axis=0, keepdims=True) * (0.5 / D)

    vec = jax.ShapeDtypeStruct((1, D), F32)
    return pl.pallas_call(
        body, out_shape=[jax.ShapeDtypeStruct((t, D), F32), vec, vec], grid=(t // tm,),
        in_specs=[_row_spec(tm, D), _row_spec(tm, D), _vec_spec(D)],
        out_specs=[_row_spec(tm, D), _vec_spec(D), _vec_spec(D)],
        name=name, compiler_params=_params(("arbitrary",)),
    )(x, target, g)


def _mod_fwd(name, c8, ada_w, layer, ada_b):
    tn = ada_w.shape[-1]

    def body(c_ref, w_ref, b_ref, o_ref):
        cv = c_ref[...]
        ca = (cv * _sigmoid(cv)).astype(BF16)
        o_ref[...] = jnp.dot(ca, w_ref[...], preferred_element_type=F32) + b_ref[...]

    return pl.pallas_call(
        body, out_shape=jax.ShapeDtypeStruct((8, SHARDS * tn), F32), grid=(SHARDS,),
        in_specs=[pl.BlockSpec((8, D), lambda j: (0, 0)), pl.BlockSpec((None, None, D, tn), lambda j: (j, layer, 0, 0)),
                  pl.BlockSpec((1, tn), lambda j: (0, j))],
        out_specs=pl.BlockSpec((8, tn), lambda j: (0, j)), name=name, compiler_params=_params(("parallel",)),
    )(c8, ada_w, ada_b)


def _mod_bwd(name, c_col, dmods, tk=256):
    tn = dmods.shape[-1] // SHARDS

    def body(c_ref, d_ref, o_ref):
        cv = c_ref[...]
        o_ref[...] = (cv * _sigmoid(cv)) * d_ref[...]

    return pl.pallas_call(
        body, out_shape=jax.ShapeDtypeStruct((SHARDS, DEPTH, D, tn), F32), grid=(SHARDS, DEPTH, D // tk),
        in_specs=[pl.BlockSpec((tk, 1), lambda j, l, i: (i, 0)), pl.BlockSpec((None, 1, tn), lambda j, l, i: (l, 0, j))],
        out_specs=pl.BlockSpec((None, None, tk, tn), lambda j, l, i: (j, l, i, 0)), name=name,
        compiler_params=_params(("parallel", "parallel", "parallel")),
    )(c_col, dmods)


def _rope_tables(t):
    half = HEAD // 2
    inv_freq = 10000.0 ** (-(jnp.arange(half, dtype=F32) * 2.0 / HEAD))
    ang = jnp.arange(t, dtype=F32)[:, None] * inv_freq[None, :]
    cos, sin = jnp.cos(ang), jnp.sin(ang)
    cos_t = jnp.tile(jnp.concatenate([cos, cos], axis=1), (1, VL // HEAD))
    sin_t = jnp.tile(jnp.concatenate([-sin, sin], axis=1), (1, VL // HEAD))
    return cos_t, sin_t


def _rotate(tv, cos, sin_signed):
    lane = lax.broadcasted_iota(jnp.int32, tv.shape, 1)
    first = (lane % HEAD) < (HEAD // 2)
    partner = jnp.where(first, pltpu.roll(tv, tv.shape[1] - HEAD // 2, 1), pltpu.roll(tv, HEAD // 2, 1))
    return tv * cos + partner * sin_signed


def _dilated_spec(tm, d):
    return pl.BlockSpec((tm // d, d * GW), lambda i: (i, 0))


def _dilated_shape(t, d, dtype):
    return jax.ShapeDtypeStruct((t // d, d * GW), dtype)


def _every(d, r, tm):
    return pl.ds(r, tm // d, stride=d) if d > 1 else slice(None)


def _rope_fwd(name, zqkv, cos_t, sin_t, tm=512):
    t = zqkv.shape[0]
    ng = len(DILATIONS)
    n = 3 * ng

    halves = GW // VL

    def body(*refs):
        z_refs, cos_ref, sin_ref, o_refs = refs[:halves * n], refs[halves * n], refs[halves * n + 1], refs[halves * n + 2:]
        for idx in range(n):
            d = DILATIONS[idx % ng]
            for r in range(d):
                rows = _every(d, r, tm)
                for hh in range(halves):
                    piece = z_refs[halves * idx + hh][rows, :]
                    if idx < 2 * ng:
                        piece = _rotate(piece, cos_ref[rows, :], sin_ref[rows, :])
                    if idx < ng:
                        piece = piece * (HEAD ** -0.5)
                    o_refs[idx][:, r * GW + hh * VL:r * GW + (hh + 1) * VL] = piece.astype(BF16)

    dils = [DILATIONS[idx % ng] for idx in range(n)]
    return pl.pallas_call(
        body, out_shape=[_dilated_shape(t, d, BF16) for d in dils], grid=(t // tm,),
        in_specs=[_row_spec(tm, VL, col) for col in range(halves * n)] + [_row_spec(tm, VL), _row_spec(tm, VL)],
        out_specs=[_dilated_spec(tm, d) for d in dils], name=name, compiler_params=_params(("parallel",)),
    )(*([zqkv] * (halves * n)), cos_t, sin_t)


def _rope_bwd(name, grads, cos_t, sin_t, tm=512):
    ng = len(DILATIONS)
    n = len(grads)
    t = grads[0].shape[0] * DILATIONS[0]

    halves = GW // VL

    def body(*refs):
        g_refs, cos_ref, sin_ref, o_ref, rows_ref = refs[:n], refs[n], refs[n + 1], refs[n + 2], refs[n + 3]
        cos, sin = cos_ref[...], -sin_ref[...]
        for idx in range(n):
            d = DILATIONS[idx % ng]
            for hh in range(halves):
                for r in range(d):
                    cols = slice(r * GW + hh * VL, r * GW + (hh + 1) * VL)
                    rows_ref[hh, _every(d, r, tm), :] = g_refs[idx][:, cols].astype(F32)
                piece = rows_ref[hh]
                if idx < 2 * ng:
                    piece = _rotate(piece, cos, sin)
                if idx < ng:
                    piece = piece * (HEAD ** -0.5)
                o_ref[:, idx * GW + hh * VL:idx * GW + (hh + 1) * VL] = piece.astype(BF16)

    dils = [DILATIONS[idx % ng] for idx in range(n)]
    return pl.pallas_call(
        body, out_shape=jax.ShapeDtypeStruct((t, n * GW), BF16), grid=(t // tm,),
        in_specs=[_dilated_spec(tm, d) for d in dils] + [_row_spec(tm, VL)] * 2, out_specs=_row_spec(tm, n * GW),
        scratch_shapes=[pltpu.VMEM((halves, tm, VL), F32)], name=name, compiler_params=_params(("parallel",)),
    )(*grads, cos_t, sin_t)


def _head_cols(h):
    return slice(h * HEAD, (h + 1) * HEAD)


def _band_mask_q(has_prev):
    qi = lax.broadcasted_iota(jnp.int32, (BAND, 2 * BAND), 0)
    kj = lax.broadcasted_iota(jnp.int32, (BAND, 2 * BAND), 1)
    dist = qi + BAND - kj
    return (dist >= 0) & (dist <= BAND) & ((kj >= BAND) | has_prev)


def _attn_fwd(name, q, k, v, group):
    d = DILATIONS[group]
    length = q.shape[0]
    qb = min(512, length)
    sub = qb // BAND
    nblk = length // qb

    def body(q_ref, kc_ref, kp_ref, vc_ref, vp_ref, o_ref, lse_ref):
        blk = pl.program_id(1)
        k_ext = jnp.concatenate([kp_ref[...], kc_ref[...]], axis=0)
        v_ext = jnp.concatenate([vp_ref[...], vc_ref[...]], axis=0)
        for j in range(sub):
            mask = _band_mask_q((blk * sub + j) > 0)
            qj = q_ref[j * BAND:(j + 1) * BAND, :]
            kj = k_ext[j * BAND:(j + 2) * BAND, :]
            vj = v_ext[j * BAND:(j + 2) * BAND, :]
            outs, lses = [], []
            for h in range(GW // HEAD):
                s = lax.dot_general(qj[:, _head_cols(h)], kj[:, _head_cols(h)], (((1,), (1,)), ((), ())),
                                    preferred_element_type=F32)
                s = jnp.where(mask, s, NEG_INF)
                m = jnp.max(s, axis=-1, keepdims=True)
                p = jnp.exp(s - m)
                den = jnp.sum(p, axis=-1, keepdims=True)
                o = jnp.dot(p.astype(BF16), vj[:, _head_cols(h)], preferred_element_type=F32)
                outs.append(o / den)
                lses.append(jnp.broadcast_to(m + jnp.log(den), (BAND, HEAD)))
            o_ref[j * BAND:(j + 1) * BAND, :] = jnp.concatenate(outs, axis=1)
            lse_ref[j * BAND:(j + 1) * BAND, :] = jnp.concatenate(lses, axis=1)

    prev = qb // BAND
    cur = lambda r, b: (b, r)
    before = lambda r, b: (jnp.maximum(b * prev - 1, 0), r)
    big, halo = pl.BlockSpec((qb, GW), cur), pl.BlockSpec((BAND, GW), before)
    return pl.pallas_call(
        body, out_shape=[jax.ShapeDtypeStruct((length, d * GW), F32)] * 2, grid=(d, nblk),
        in_specs=[big, big, halo, big, halo], out_specs=[big] * 2, name=name,
        compiler_params=_params(("parallel", "parallel")),
    )(q, k, k, v, v)


def _attn_merge(name, outs, lses, tm=512):
    n = len(outs)
    t = outs[0].shape[0] * DILATIONS[0]

    halves = GW // VL

    def body(*refs):
        in_refs = refs[:2 * n]
        ob_ref, of_ref, lj_ref, rows_ref = refs[2 * n:]
        for hh in range(halves):
            for idx in range(2 * n):
                d = DILATIONS[idx % n]
                for r in range(d):
                    cols = slice(r * GW + hh * VL, r * GW + (hh + 1) * VL)
                    rows_ref[idx, _every(d, r, tm), :] = in_refs[idx][:, cols]
            ls = [rows_ref[n + g] for g in range(n)]
            m = ls[0]
            for v in ls[1:]:
                m = jnp.maximum(m, v)
            es = [jnp.exp(v - m) for v in ls]
            tot = es[0]
            for v in es[1:]:
                tot = tot + v
            acc = (es[0] / tot) * rows_ref[0]
            for g in range(1, n):
                acc = acc + (es[g] / tot) * rows_ref[g]
            half = slice(hh * VL, (hh + 1) * VL)
            ob_ref[:, half] = acc.astype(BF16)
            of_ref[:, half] = acc
            lj_ref[:, half] = m + jnp.log(tot)

    return pl.pallas_call(
        body, out_shape=[jax.ShapeDtypeStruct((t, GW), BF16), jax.ShapeDtypeStruct((t, GW), F32),
                         jax.ShapeDtypeStruct((t, GW), F32)], grid=(t // tm,),
        in_specs=[_dilated_spec(tm, DILATIONS[idx % n]) for idx in range(2 * n)], out_specs=[_row_spec(tm, GW)] * 3,
        scratch_shapes=[pltpu.VMEM((2 * n, tm, VL), F32)], name=name, compiler_params=_params(("parallel",)),
    )(*outs, *lses)


def _attn_bwd_prep(name, do, o, lj, tm=512):
    t = do.shape[0]
    n = len(DILATIONS)

    halves = GW // VL
    per_half = VL // HEAD

    def body(*refs):
        do_refs, o_refs, lj_refs = refs[:halves], refs[halves:2 * halves], refs[2 * halves:3 * halves]
        outs, dsum_ref = refs[3 * halves:3 * halves + 3 * n], refs[3 * halves + 3 * n]
        for hh in range(halves):
            prod = do_refs[hh][...] * o_refs[hh][...]
            parts = [jnp.broadcast_to(jnp.sum(prod[:, _head_cols(h)], axis=-1, keepdims=True), (tm, HEAD))
                     for h in range(per_half)]
            dsum_ref[...] = jnp.concatenate(parts, axis=1)
            for g, d in enumerate(DILATIONS):
                for r in range(d):
                    rows, cols = _every(d, r, tm), slice(r * GW + hh * VL, r * GW + (hh + 1) * VL)
                    outs[g][:, cols] = dsum_ref[rows, :]
                    outs[n + g][:, cols] = do_refs[hh][rows, :].astype(BF16)
                    outs[2 * n + g][:, cols] = lj_refs[hh][rows, :]

    shapes = [_dilated_shape(t, d, dt) for dt in (F32, BF16, F32) for d in DILATIONS]
    half_specs = [_row_spec(tm, VL, hh) for hh in range(halves)]
    return pl.pallas_call(
        body, out_shape=shapes, grid=(t // tm,), in_specs=half_specs * 3,
        out_specs=[_dilated_spec(tm, d) for d in DILATIONS] * 3, scratch_shapes=[pltpu.VMEM((tm, VL), F32)],
        name=name, compiler_params=_params(("parallel",)),
    )(*([do] * halves), *([o] * halves), *([lj] * halves))


def _attn_bwd(name, q, k, v, do, lj, dsum, group):
    d = DILATIONS[group]
    length = q.shape[0]
    qb = min(512, length)
    sub = qb // BAND
    nblk = length // qb
    total = length // BAND

    def body(qc_ref, qn_ref, kc_ref, kp_ref, vc_ref, vp_ref, doc_ref, don_ref, ljc_ref, ljn_ref, dsc_ref, dsn_ref,
             dq_ref, dk_ref, dv_ref):
        blk = pl.program_id(1)
        q_ext = jnp.concatenate([qc_ref[...], qn_ref[...]], axis=0)
        do_ext = jnp.concatenate([doc_ref[...], don_ref[...]], axis=0)
        lj_ext = jnp.concatenate([ljc_ref[...], ljn_ref[...]], axis=0)
        ds_ext = jnp.concatenate([dsc_ref[...], dsn_ref[...]], axis=0)
        k_ext = jnp.concatenate([kp_ref[...], kc_ref[...]], axis=0)
        v_ext = jnp.concatenate([vp_ref[...], vc_ref[...]], axis=0)
        qi2 = lax.broadcasted_iota(jnp.int32, (2 * BAND, BAND), 0)
        kj2 = lax.broadcasted_iota(jnp.int32, (2 * BAND, BAND), 1)
        for j in range(sub):
            gblk = blk * sub + j
            rows = slice(j * BAND, (j + 1) * BAND)
            rows2 = slice(j * BAND, (j + 2) * BAND)
            mask_q = _band_mask_q(gblk > 0)
            mask_k = ((qi2 < BAND) & (kj2 <= qi2)) | ((qi2 >= BAND) & (kj2 >= qi2 - BAND) & (gblk + 1 < total))
            dqs, dks, dvs = [], [], []
            for h in range(GW // HEAD):
                hc = _head_cols(h)
                col = slice(h * HEAD, h * HEAD + 1)
                qh = q_ext[rows, hc]
                kh2 = k_ext[rows2, hc]
                s = lax.dot_general(qh, kh2, (((1,), (1,)), ((), ())), preferred_element_type=F32)
                p = jnp.where(mask_q, jnp.exp(s - lj_ext[rows, col]), 0.0)
                dp = lax.dot_general(do_ext[rows, hc], v_ext[rows2, hc], (((1,), (1,)), ((), ())),
                                     preferred_element_type=F32)
                dsc = p * (dp - ds_ext[rows, col])
                dqs.append(jnp.dot(dsc.astype(BF16), kh2, preferred_element_type=F32))
                qh2 = q_ext[rows2, hc]
                kh = k_ext[BAND + j * BAND:BAND + (j + 1) * BAND, hc]
                vh = v_ext[BAND + j * BAND:BAND + (j + 1) * BAND, hc]
                doh2 = do_ext[rows2, hc]
                s2 = lax.dot_general(qh2, kh, (((1,), (1,)), ((), ())), preferred_element_type=F32)
                p2 = jnp.where(mask_k, jnp.exp(s2 - lj_ext[rows2, col]), 0.0)
                dp2 = lax.dot_general(doh2, vh, (((1,), (1,)), ((), ())), preferred_element_type=F32)
                ds2 = p2 * (dp2 - ds_ext[rows2, col])
                dvs.append(lax.dot_general(p2.astype(BF16), doh2, (((0,), (0,)), ((), ())),
                                           preferred_element_type=F32))
                dks.append(lax.dot_general(ds2.astype(BF16), qh2, (((0,), (0,)), ((), ())),
                                           preferred_element_type=F32))
            dq_ref[rows, :] = jnp.concatenate(dqs, axis=1)
            dk_ref[rows, :] = jnp.concatenate(dks, axis=1)
            dv_ref[rows, :] = jnp.concatenate(dvs, axis=1).astype(BF16)

    prev = qb // BAND
    cur = lambda r, b: (b, r)
    before = lambda r, b: (jnp.maximum(b * prev - 1, 0), r)
    after = lambda r, b: (jnp.minimum((b + 1) * prev, total - 1), r)
    big = pl.BlockSpec((qb, GW), cur)
    nxt = pl.BlockSpec((BAND, GW), after)
    prv = pl.BlockSpec((BAND, GW), before)
    return pl.pallas_call(
        body, out_shape=[jax.ShapeDtypeStruct((length, d * GW), F32), jax.ShapeDtypeStruct((length, d * GW), F32),
                         jax.ShapeDtypeStruct((length, d * GW), BF16)], grid=(d, nblk),
        in_specs=[big, nxt, big, prv, big, prv, big, nxt, big, nxt, big, nxt],
        out_specs=[big] * 3, name=name, compiler_params=_params(("parallel", "parallel")),
    )(q, q, k, k, v, v, do, do, lj, lj, dsum, dsum)


SUBLANES = 8
CONV_CHUNK = 32
SHIFT_ROWS = HALO - SUBLANES


def _shifted_copies(buf_ref, sh_ref, tm):
    for s in range(1, SUBLANES):
        sh_ref[s - 1] = buf_ref[s:s + tm + SHIFT_ROWS, :]


def _window(buf_ref, sh_ref, offset, r0, rows):
    tiles, shift = divmod(offset, SUBLANES)
    src = buf_ref if shift == 0 else sh_ref.at[shift - 1]
    return src[pl.ds(pl.multiple_of(r0 + tiles * SUBLANES, SUBLANES), rows), :]


def _conv_fwd(name, zu, conv_w, conv_b, ln_g, ln_b, tm=256):
    t = zu.shape[0]
    per = tm // HALO

    def body(a_ref, gl_ref, ah_ref, glh_ref, w_ref, b_ref, g_ref, beta_ref, hc_ref, s_ref, ext_ref, sh_ref):
        i = pl.program_id(0)
        halo = ah_ref[...] * _sigmoid(glh_ref[...])
        ext_ref[0:HALO, :] = jnp.where(i > 0, halo, 0.0)
        ext_ref[HALO:, :] = a_ref[...] * _sigmoid(gl_ref[...])
        _shifted_copies(ext_ref, sh_ref, tm)

        def chunk(r, carry):
            r0 = pl.multiple_of(r * CONV_CHUNK, CONV_CHUNK)
            part = jnp.broadcast_to(b_ref[...], (CONV_CHUNK, D))
            for kk in range(CONV_K):
                part = part + w_ref[kk:kk + 1, :] * _window(ext_ref, sh_ref, HALO - CONV_K + 1 + kk, r0, CONV_CHUNK)
            hc_ref[pl.ds(r0, CONV_CHUNK), :] = part
            return carry

        lax.fori_loop(0, tm // CONV_CHUNK, chunk, 0)
        acc = hc_ref[...]
        mu = jnp.mean(acc, axis=-1, keepdims=True)
        xc = acc - mu
        var = jnp.mean(xc * xc, axis=-1, keepdims=True)
        ln = xc * lax.rsqrt(var + EPS) * g_ref[...] + beta_ref[...]
        s_ref[...] = (ln * _sigmoid(ln)).astype(BF16)

    halo_map = lambda col: (lambda i: (jnp.maximum(i * per - 1, 0), col))
    return pl.pallas_call(
        body, out_shape=[jax.ShapeDtypeStruct((t, D), F32), jax.ShapeDtypeStruct((t, D), BF16)], grid=(t // tm,),
        in_specs=[_row_spec(tm, D, 0), _row_spec(tm, D, 1), pl.BlockSpec((HALO, D), halo_map(0)),
                  pl.BlockSpec((HALO, D), halo_map(1)), pl.BlockSpec((HALO, D), lambda i: (0, 0)),
                  _vec_spec(D), _vec_spec(D), _vec_spec(D)],
        out_specs=[_row_spec(tm, D), _row_spec(tm, D)],
        scratch_shapes=[pltpu.VMEM((tm + HALO, D), F32), pltpu.VMEM((SUBLANES - 1, tm + SHIFT_ROWS, D), F32)],
        name=name, compiler_params=_params(("parallel",)),
    )(zu, zu, zu, zu, conv_w, conv_b, ln_g, ln_b)


def _conv_ln_bwd(name, hc, ds, ln_g, ln_b, tm=256):
    t = hc.shape[0]

    def body(hc_ref, ds_ref, g_ref, beta_ref, dhc_ref, dg_ref, dbeta_ref, dbias_ref):
        @pl.when(pl.program_id(0) == 0)
        def _():
            dg_ref[...] = jnp.zeros_like(dg_ref)
            dbeta_ref[...] = jnp.zeros_like(dbeta_ref)
            dbias_ref[...] = jnp.zeros_like(dbias_ref)

        hv = hc_ref[...]
        mu = jnp.mean(hv, axis=-1, keepdims=True)
        xc = hv - mu
        rstd = lax.rsqrt(jnp.mean(xc * xc, axis=-1, keepdims=True) + EPS)
        xh = xc * rstd
        ln = xh * g_ref[...] + beta_ref[...]
        sg = _sigmoid(ln)
        dln = ds_ref[...] * (sg * (1.0 + ln * (1.0 - sg)))
        dxh = dln * g_ref[...]
        dh = rstd * (dxh - jnp.mean(dxh, axis=-1, keepdims=True) - xh * jnp.mean(dxh * xh, axis=-1, keepdims=True))
        dhc_ref[...] = dh
        dg_ref[...] += jnp.sum(dln * xh, axis=0, keepdims=True)
        dbeta_ref[...] += jnp.sum(dln, axis=0, keepdims=True)
        dbias_ref[...] += jnp.sum(dh, axis=0, keepdims=True)

    vec = jax.ShapeDtypeStruct((1, D), F32)
    return pl.pallas_call(
        body, out_shape=[jax.ShapeDtypeStruct((t, D), F32), vec, vec, vec], grid=(t // tm,),
        in_specs=[_row_spec(tm, D), _row_spec(tm, D), _vec_spec(D), _vec_spec(D)],
        out_specs=[_row_spec(tm, D), _vec_spec(D), _vec_spec(D), _vec_spec(D)],
        name=name, compiler_params=_params(("arbitrary",)),
    )(hc, ds, ln_g, ln_b)


def _conv_bwd(name, zu, dhc, conv_w, tm=256):
    t = zu.shape[0]
    per = tm // HALO
    steps = t // tm

    group = 4

    def body(a_ref, gl_ref, ah_ref, glh_ref, d_ref, dn_ref, w_ref, dz_ref, dw_ref, ext_ref, sh_ref, dext_ref, dsh_ref,
             sg_ref, part_ref):
        i = pl.program_id(0)

        @pl.when(i == 0)
        def _():
            part_ref[...] = jnp.zeros_like(part_ref)

        sg_ref[...] = _sigmoid(gl_ref[...])
        ext_ref[0:HALO, :] = jnp.where(i > 0, ah_ref[...] * _sigmoid(glh_ref[...]), 0.0)
        ext_ref[HALO:, :] = a_ref[...] * sg_ref[...]
        dext_ref[0:tm, :] = d_ref[...]
        dext_ref[tm:, :] = jnp.where(i < steps - 1, dn_ref[...], 0.0)
        _shifted_copies(ext_ref, sh_ref, tm)
        _shifted_copies(dext_ref, dsh_ref, tm)

        def chunk(r, carry):
            r0 = pl.multiple_of(r * CONV_CHUNK, CONV_CHUNK)
            rows = pl.ds(r0, CONV_CHUNK)
            part = jnp.zeros((CONV_CHUNK, D), F32)
            for kk in range(CONV_K):
                part = part + w_ref[kk:kk + 1, :] * _window(dext_ref, dsh_ref, CONV_K - 1 - kk, r0, CONV_CHUNK)
            sg = sg_ref[rows, :]
            dz_ref[rows, 0:D] = (part * sg).astype(BF16)
            dz_ref[rows, D:] = (part * a_ref[rows, :] * sg * (1.0 - sg)).astype(BF16)
            return carry

        lax.fori_loop(0, tm // CONV_CHUNK, chunk, 0)

        for k0 in range(0, CONV_K, group):
            taps = range(k0, min(k0 + group, CONV_K))

            def tile(r, parts, taps=taps):
                r0 = pl.multiple_of(r * CONV_CHUNK, CONV_CHUNK)
                dv = d_ref[pl.ds(r0, CONV_CHUNK), :]
                out = []
                for p, kk in zip(parts, taps):
                    prod = dv * _window(ext_ref, sh_ref, HALO - CONV_K + 1 + kk, r0, CONV_CHUNK)
                    for s in range(0, CONV_CHUNK, SUBLANES):
                        p = p + prod[s:s + SUBLANES, :]
                    out.append(p)
                return tuple(out)

            parts = lax.fori_loop(0, tm // CONV_CHUNK, tile, tuple(jnp.zeros((SUBLANES, D), F32) for _ in taps))
            for p, kk in zip(parts, taps):
                part_ref[kk * SUBLANES:(kk + 1) * SUBLANES, :] += p

        @pl.when(i == steps - 1)
        def _():
            for kk in range(HALO):
                dw_ref[kk:kk + 1, :] = jnp.sum(part_ref[kk * SUBLANES:(kk + 1) * SUBLANES, :], axis=0, keepdims=True)

    halo_map = lambda col: (lambda i: (jnp.maximum(i * per - 1, 0), col))
    shifted = pltpu.VMEM((SUBLANES - 1, tm + SHIFT_ROWS, D), F32)
    return pl.pallas_call(
        body, out_shape=[jax.ShapeDtypeStruct((t, 2 * D), BF16), jax.ShapeDtypeStruct((HALO, D), F32)],
        grid=(steps,),
        in_specs=[_row_spec(tm, D, 0), _row_spec(tm, D, 1), pl.BlockSpec((HALO, D), halo_map(0)),
                  pl.BlockSpec((HALO, D), halo_map(1)), _row_spec(tm, D),
                  pl.BlockSpec((HALO, D), lambda i: (jnp.minimum((i + 1) * per, t // HALO - 1), 0)),
                  pl.BlockSpec((HALO, D), lambda i: (0, 0))],
        out_specs=[_row_spec(tm, 2 * D), pl.BlockSpec((HALO, D), lambda i: (0, 0))],
        scratch_shapes=[pltpu.VMEM((tm + HALO, D), F32), shifted, pltpu.VMEM((tm + HALO, D), F32), shifted,
                        pltpu.VMEM((tm, D), F32), pltpu.VMEM((HALO * SUBLANES, D), F32)],
        name=name, compiler_params=_params(("arbitrary",)),
    )(zu, zu, zu, zu, dhc, dhc, conv_w)


def _ffn_fwd(tag, x, h, w, at, gate):
    def act(accs, extras, vecs):
        g, u = accs
        return [g, u, (g * _sigmoid(g)) * u]

    gv, uv, av = _mm(f"ffn_up_{tag}", [h], [(w["ffn_wg"], (TILE, *at)), (w["ffn_wu"], (TILE, *at))],
                     [(0, 0), (0, 1)], act, [BF16, BF16, BF16], tm=1024, tn=FSH, n_out=DFF, out_tiled=True)

    def residual(accs, extras, vecs):
        tot, = _total(accs, extras, vecs)
        return [extras[0] + (0.5 * vecs[0]) * tot, tot]

    shards = range(SHARDS)
    x_new, f = _mm(f"ffn_down_{tag}", [(av, (k,)) for k in shards], [(w["ffn_wd"], (k, *at)) for k in shards],
                   [(k, k) for k in shards], residual, [F32, BF16], tm=512, tn=512, n_out=D, extras=[(x, 0)],
                   vecs=[gate])
    return x_new, (gv, uv, f)


def _ffn_bwd(tag, dx, x, h, saved, w, at, g, scale, gate, into):
    gv, uv, f = saved
    df, dgate = _resgate_bwd(f"ffn_gate_bwd_{tag}", dx, f, gate, 0.5)

    def act(blocks):
        gf, uf = blocks[0].astype(F32), blocks[1].astype(F32)
        return ((gf * _sigmoid(gf)) * uf).astype(BF16)

    dwd = _mm_tn(f"ffn_dwd_{tag}", [gv, uv], df, tk=FSH, tn=1024, tt=1024, a_fn=act, a_tiled=True,
                 into=(into["ffn_wd"], at))

    def act_bwd(accs, extras, vecs):
        da = accs[0]
        gf, uf = extras[0].astype(F32), extras[1].astype(F32)
        sg = _sigmoid(gf)
        return [da * uf * (sg * (1.0 + gf * (1.0 - sg))), da * (gf * sg)]

    dg, du = _mm(f"ffn_da_{tag}", [df], [(w["ffn_wd"], (TILE, *at))], [(0, 0)], act_bwd, [BF16, BF16], tm=1024,
                 tn=FSH, n_out=DFF, trans_b=True, extras=[(gv, (TILE,)), (uv, (TILE,))], out_tiled=True)
    dwg = _mm_tn(f"ffn_dwg_{tag}", h, dg, tk=1024, tn=FSH, tt=1024, b_tiled=True, into=(into["ffn_wg"], at))
    dwu = _mm_tn(f"ffn_dwu_{tag}", h, du, tk=1024, tn=FSH, tt=1024, b_tiled=True, into=(into["ffn_wu"], at))
    shards = range(SHARDS)
    dh, = _mm(f"ffn_dh_{tag}", [(dg, (k,)) for k in shards] + [(du, (k,)) for k in shards],
              [(w["ffn_wg"], (k, *at)) for k in shards] + [(w["ffn_wu"], (k, *at)) for k in shards],
              [(k, k) for k in range(2 * SHARDS)], _total, [F32], tm=512, tn=512, n_out=D, trans_b=True)
    dx_in, dgn, dscale, dshift = _normmod_bwd(f"ffn_norm_bwd_{tag}", x, dh, dx, g, scale)
    return dx_in, dict(ffn_wg=dwg, ffn_wu=dwu, ffn_wd=dwd), dgn, (dshift, dscale, dgate)


def _mix_fwd(tag, x, h, w_in, attn_wo, conv_w, conv_b, ln_g, ln_b, conv_wo, w_out, gate, cos_t, sin_t):
    w_qkv, w_u, w_g = w_in[:, :QKV], w_in[:, QKV:QKV + 2 * D], w_in[:, QKV + 2 * D:]
    zqkv, = _mm(f"mix_qkv_{tag}", [h], [w_qkv], [(0, 0)], _first, [F32], tm=512, tn=768, n_out=QKV)
    zu, = _mm(f"mix_u_{tag}", [h], [w_u], [(0, 0)], _first, [F32], tm=512, tn=1024, n_out=2 * D)
    zg, = _mm(f"mix_g_{tag}", [h], [w_g], [(0, 0)], _first, [BF16], tm=512, tn=1024, n_out=2 * D)
    qkv = _rope_fwd(f"rope_{tag}", zqkv, cos_t, sin_t)
    n = len(DILATIONS)
    outs, lses = [], []
    for grp in range(n):
        o, lse = _attn_fwd(f"attn_fwd_{tag}_{grp}", qkv[grp], qkv[n + grp], qkv[2 * n + grp], grp)
        outs.append(o)
        lses.append(lse)
    ob, of, lj = _attn_merge(f"attn_merge_{tag}", outs, lses)
    hc, s = _conv_fwd(f"conv_fwd_{tag}", zu, conv_w, conv_b, ln_g, ln_b)

    def gated(accs, extras, vecs):
        ya, yc = accs
        return [_sigmoid(extras[0].astype(F32)) * ya + _sigmoid(extras[1].astype(F32)) * yc, ya, yc]

    y, ya, yc = _mm(f"mix_y_{tag}", [ob, s], [attn_wo, conv_wo], [(0, 0), (1, 1)], gated, [BF16, BF16, BF16],
                    tm=512, tn=1024, n_out=D, extras=[(zg, 0), (zg, 1)])

    def residual(accs, extras, vecs):
        return [extras[0] + vecs[0] * accs[0], accs[0]]

    x_new, f = _mm(f"mix_out_{tag}", [y], [w_out], [(0, 0)], residual, [F32, BF16], tm=512, tn=512, n_out=D,
                   extras=[(x, 0)], vecs=[gate])
    return x_new, (zu, zg, qkv, ob, of, lj, hc, s, y, ya, yc, f, (w_qkv, w_u, w_g))


def _mix_bwd(tag, dx, x, h, saved, attn_wo, conv_w, ln_g, ln_b, conv_wo, w_out, g, scale, gate, cos_t, sin_t):
    zu, zg, qkv, ob, of, lj, hc, s, y, ya, yc, f, w_parts = saved
    n = len(DILATIONS)
    df, dgate = _resgate_bwd(f"mix_gate_bwd_{tag}", dx, f, gate, 1.0)
    dw_out = _mm_tn(f"mix_dwout_{tag}", y, df, tk=1024, tn=1024, tt=512)

    def gated_bwd(accs, extras, vecs):
        dy = accs[0]
        sa, sc = _sigmoid(extras[0].astype(F32)), _sigmoid(extras[1].astype(F32))
        dga = dy * extras[2].astype(F32) * (sa * (1.0 - sa))
        dgc = dy * extras[3].astype(F32) * (sc * (1.0 - sc))
        return [dy * sa, dy * sc, jnp.concatenate([dga, dgc], axis=1)]

    dya, dyc, dzg = _mm(f"mix_dy_{tag}", [df], [w_out], [(0, 0)], gated_bwd, [BF16] * 3, tm=512, tn=1024,
                        n_out=D, trans_b=True, extras=[(zg, 0), (zg, 1), (ya, 0), (yc, 0)],
                        out_widths=[D, D, 2 * D])
    dw_attn = _mm_tn(f"mix_dwattn_{tag}", ob, dya, tk=GW, tn=1024, tt=512)
    dw_conv_o = _mm_tn(f"mix_dwconvo_{tag}", s, dyc, tk=1024, tn=1024, tt=512)
    do, = _mm(f"mix_do_{tag}", [dya], [attn_wo], [(0, 0)], _first, [F32], tm=512, tn=GW, n_out=GW, trans_b=True)
    ds, = _mm(f"mix_ds_{tag}", [dyc], [conv_wo], [(0, 0)], _first, [F32], tm=512, tn=1024, n_out=D, trans_b=True)

    prep = _attn_bwd_prep(f"attn_prep_{tag}", do, of, lj)
    dqs, dks, dvs = [], [], []
    for grp in range(n):
        dq, dk, dv = _attn_bwd(f"attn_bwd_{tag}_{grp}", qkv[grp], qkv[n + grp], qkv[2 * n + grp], prep[n + grp],
                               prep[2 * n + grp], prep[grp], grp)
        dqs.append(dq)
        dks.append(dk)
        dvs.append(dv)
    dzqkv = _rope_bwd(f"rope_bwd_{tag}", dqs + dks + dvs, cos_t, sin_t)

    dhc, dln_g, dln_b, dconv_b = _conv_ln_bwd(f"conv_ln_bwd_{tag}", hc, ds, ln_g, ln_b)
    dzu, dconv_w = _conv_bwd(f"conv_bwd_{tag}", zu, dhc, conv_w)

    dz_parts = [dzqkv, dzu, dzg]
    dw_in = jnp.concatenate(
        [_mm_tn(f"mix_dwin_{tag}_{i}", h, dzp, tk=1024, tn=dzp.shape[1] // 2, tt=512)
         for i, dzp in enumerate(dz_parts)], axis=1)
    dh, = _mm(f"mix_dh_{tag}", dz_parts, list(w_parts), [(0, 0), (1, 1), (2, 2)], _total, [F32], tm=512, tn=512,
              n_out=D, trans_b=True)
    dx_in, dgn, dscale, dshift = _normmod_bwd(f"mix_norm_bwd_{tag}", x, dh, dx, g, scale)
    grads = dict(w_in=dw_in, attn_wo=dw_attn, conv_w=dconv_w[:CONV_K], conv_b=dconv_b, conv_ln_g=dln_g,
                 conv_ln_b=dln_b, conv_wo=dw_conv_o, w_out=dw_out)
    return dx_in, grads, dgn, (dshift, dscale, dgate)


def _local_step(x, c, target, w, wf):
    t = x.shape[0]
    cos_t, sin_t = _rope_tables(t)
    c8 = jnp.concatenate([c, jnp.zeros((7, D), F32)], axis=0)
    row = lambda v: v.reshape(1, -1)
    conv_w_pad = jnp.concatenate([wf["conv_w"], jnp.zeros((DEPTH, HALO - CONV_K, D), F32)], axis=1)

    saved = []
    for l in range(DEPTH):
        mod = _mod_fwd(f"mod_{l}", c8, w["ada_w"], l, row(wf["ada_b"][l]))[0:1]
        mods = [mod[:, i * D:(i + 1) * D] for i in range(N_MOD)]
        gains = [row(wf["norm_g"][l, i]) for i in range(3)]
        lay = dict(mods=mods, gains=gains)

        lay["x0"] = x
        lay["h0"] = _normmod_fwd(f"norm_a_{l}", x, gains[0], mods[1], mods[0])
        x, lay["ffn0"] = _ffn_fwd(f"a_{l}", x, lay["h0"], w, (l, 0), mods[2])
        lay["x1"] = x
        lay["h1"] = _normmod_fwd(f"norm_m_{l}", x, gains[1], mods[4], mods[3])
        x, lay["mix"] = _mix_fwd(f"{l}", x, lay["h1"], w["w_in"][l], w["attn_wo"][l], conv_w_pad[l],
                                 row(wf["conv_b"][l]), row(wf["conv_ln_g"][l]), row(wf["conv_ln_b"][l]),
                                 w["conv_wo"][l], w["w_out"][l], mods[5], cos_t, sin_t)
        lay["x2"] = x
        lay["h2"] = _normmod_fwd(f"norm_b_{l}", x, gains[2], mods[7], mods[6])
        x, lay["ffn1"] = _ffn_fwd(f"b_{l}", x, lay["h2"], w, (l, 1), mods[8])
        saved.append(lay)

    dx, dfinal_g, loss_cols = _loss_bwd("loss_head", x, target, row(wf["final_g"]))

    ffn_grads = {n: jnp.zeros(w[n].shape, F32) for n in ("ffn_wg", "ffn_wu", "ffn_wd")}
    per_layer = []
    for l in reversed(range(DEPTH)):
        lay = saved[l]
        mods, gains = lay["mods"], lay["gains"]
        dx, ffn_grads, dgn2, dmod2 = _ffn_bwd(f"b_{l}", dx, lay["x2"], lay["h2"], lay["ffn1"], w, (l, 1),
                                              gains[2], mods[7], mods[8], ffn_grads)
        dx, gm, dgn1, dmod1 = _mix_bwd(f"{l}", dx, lay["x1"], lay["h1"], lay["mix"], w["attn_wo"][l],
                                       conv_w_pad[l], row(wf["conv_ln_g"][l]), row(wf["conv_ln_b"][l]),
                                       w["conv_wo"][l], w["w_out"][l], gains[1], mods[4], mods[5], cos_t, sin_t)
        dx, ffn_grads, dgn0, dmod0 = _ffn_bwd(f"a_{l}", dx, lay["x0"], lay["h0"], lay["ffn0"], w, (l, 0),
                                              gains[0], mods[1], mods[2], ffn_grads)
        g = dict(gm)
        g["dmod"] = jnp.concatenate(list(dmod0) + list(dmod1) + list(dmod2), axis=1)
        g["ada_b"] = g["dmod"][0]
        g["norm_g"] = [dgn0[0], dgn1[0], dgn2[0]]
        for name in ("conv_b", "conv_ln_g", "conv_ln_b"):
            g[name] = g[name][0]
        per_layer.append(g)
    per_layer.reverse()
    grads = {name: [per_layer[l][name] for l in range(DEPTH)] for name in per_layer[0]}
    grads["ada_w"] = _mod_bwd("mod_bwd", c.reshape(D, 1), jnp.stack(grads.pop("dmod")))
    grads.update(ffn_grads)
    grads["final_g"] = dfinal_g[0]
    return loss_cols, dx, grads


def _split_bits(w):
    bits = lax.bitcast_convert_type(w, jnp.uint32)
    hi = lax.bitcast_convert_type((bits >> 16).astype(jnp.uint16), BF16)
    lo = lax.bitcast_convert_type((bits & 0xFFFF).astype(jnp.uint16), BF16)
    return hi, lo


def _join_bits(hi, lo):
    h = lax.bitcast_convert_type(hi, jnp.uint16).astype(jnp.uint32)
    l = lax.bitcast_convert_type(lo, jnp.uint16).astype(jnp.uint32)
    return lax.bitcast_convert_type((h << 16) | l, F32)


def _pack(parts, rows):
    out = []
    for p in parts:
        flat = p.reshape(-1)
        pad = -flat.shape[0] % LANES
        out.append(jnp.concatenate([flat, jnp.zeros((pad,), flat.dtype)]) if pad else flat)
    flat = jnp.concatenate(out)
    return jnp.concatenate([flat, jnp.zeros((rows * LANES - flat.shape[0],), flat.dtype)]).reshape(rows, LANES)


def _unpack(buf, shapes):
    out, row = [], 0
    for shape in shapes:
        size = 1
        for s in shape:
            size *= s
        rows = -(-size // LANES)
        out.append(buf[row:row + rows].reshape(-1)[:size].reshape(shape))
        row += rows
    return out


def _place():
    x, y, c = lax.axis_index("x"), lax.axis_index("y"), lax.axis_index("c")
    chips = [(1 - x, y), (x, 1 - y), (1 - x, 1 - y)]
    return x, y, c, chips


def _chip_index():
    return (2 * lax.axis_index("x") + lax.axis_index("y")).astype(jnp.int32)


HBM_SPEC = pl.BlockSpec(memory_space=pltpu.HBM)


def _gather_weights(arrays):
    n = len(arrays)

    def body(*refs):
        outs, send_sems, recv_sems = refs[n:2 * n], refs[2 * n], refs[2 * n + 1]
        x, y, c, chips = _place()
        me = 2 * x + y
        sibling = (x, y, 1 - c)
        there = [2 * chip[0] + chip[1] for chip in chips]

        def copy(a, k, chip, layer, to):
            piece = outs[a].at[chip, layer]
            return pltpu.make_async_remote_copy(
                src_ref=piece, dst_ref=piece, send_sem=send_sems.at[6 * a + k], recv_sem=recv_sems.at[6 * a + k],
                device_id=to, device_id_type=MESH)

        first = [copy(a, j, me, c, (*chip, c)) for a in range(n) for j, chip in enumerate(chips)]
        for cp in first:
            cp.start()
        passed = []
        for a in range(n):
            for j in range(3):
                copy(a, j, there[j], c, sibling).wait_recv()
                passed.append(copy(a, 3 + j, there[j], c, sibling))
                passed[-1].start()
        for a in range(n):
            for j in range(3):
                copy(a, 3 + j, there[j], 1 - c, sibling).wait_recv()
        for cp in first + passed:
            cp.wait_send()

    return pl.pallas_call(
        body, out_shape=[jax.ShapeDtypeStruct(a.shape, a.dtype) for a in arrays],
        in_specs=[HBM_SPEC] * n, out_specs=[HBM_SPEC] * n,
        scratch_shapes=[pltpu.SemaphoreType.DMA((6 * n,)), pltpu.SemaphoreType.DMA((6 * n,))],
        input_output_aliases={i: i for i in range(n)}, name="gather_weights",
    )(*arrays)


def _row_block(rows, cols):
    for cand in (512, 256, 128, 64, 32, 16):
        if rows % cand == 0 and cand * cols * 4 <= 2560 * 1024:
            return cand
    return rows


def _swap_layers(grads):
    n = len(grads)

    def body(*refs):
        g_refs, out_refs, send_sems, recv_sems = refs[:n], refs[n:2 * n], refs[2 * n], refs[2 * n + 1]
        x, y, c, _ = _place()
        copies = [pltpu.make_async_remote_copy(
            src_ref=g_refs[a].at[:, 1 - c], dst_ref=out_refs[a], send_sem=send_sems.at[a], recv_sem=recv_sems.at[a],
            device_id=(x, y, 1 - c), device_id_type=MESH) for a in range(n)]
        for cp in copies:
            cp.start()
        for cp in copies:
            cp.wait()

    return pl.pallas_call(
        body, out_shape=[jax.ShapeDtypeStruct((g.shape[0],) + g.shape[2:], F32) for g in grads],
        in_specs=[HBM_SPEC] * n, out_specs=[HBM_SPEC] * n,
        scratch_shapes=[pltpu.SemaphoreType.DMA((n,)), pltpu.SemaphoreType.DMA((n,))], name="swap_layers",
    )(*grads)


def _add_layers(name, grad, other):
    shards, _, rows, cols = grad.shape
    tr = _row_block(rows, cols)

    def body(c_ref, g_ref, o_ref, out_ref):
        out_ref[...] = (g_ref[...] + o_ref[...]).astype(BF16)

    c = lax.axis_index("c").astype(jnp.int32).reshape(1)
    grid_spec = pltpu.PrefetchScalarGridSpec(
        num_scalar_prefetch=1, grid=(shards, rows // tr),
        in_specs=[pl.BlockSpec((None, None, tr, cols), lambda k, i, c_ref: (k, c_ref[0], i, 0)),
                  pl.BlockSpec((None, tr, cols), lambda k, i, c_ref: (k, i, 0))],
        out_specs=pl.BlockSpec((None, tr, cols), lambda k, i, c_ref: (k, i, 0)))
    return pl.pallas_call(
        body, out_shape=jax.ShapeDtypeStruct((shards, rows, cols), BF16), grid_spec=grid_spec,
        name=name, compiler_params=_params(("parallel", "parallel")),
    )(c, grad, other)


def _scatter_chips(parts):
    n = len(parts)

    def body(*refs):
        p_refs, out_refs, send_sems, recv_sems = refs[:n], refs[n:2 * n], refs[2 * n], refs[2 * n + 1]
        x, y, c, chips = _place()
        me = 2 * x + y
        there = [2 * chip[0] + chip[1] for chip in chips]

        def copy(a, j, slot):
            return pltpu.make_async_remote_copy(
                src_ref=p_refs[a].at[there[j]], dst_ref=out_refs[a].at[slot], send_sem=send_sems.at[3 * a + j],
                recv_sem=recv_sems.at[3 * a + j], device_id=(*chips[j], c), device_id_type=MESH)

        sends = [copy(a, j, me) for a in range(n) for j in range(3)]
        for cp in sends:
            cp.start()
        for a in range(n):
            for j in range(3):
                copy(a, j, there[j]).wait_recv()
        for cp in sends:
            cp.wait_send()

    return pl.pallas_call(
        body, out_shape=[jax.ShapeDtypeStruct(p.shape, p.dtype) for p in parts],
        in_specs=[HBM_SPEC] * n, out_specs=[HBM_SPEC] * n,
        scratch_shapes=[pltpu.SemaphoreType.DMA((3 * n,)), pltpu.SemaphoreType.DMA((3 * n,))],
        name="scatter_chips",
    )(*parts)


def _add_chips(name, part, others):
    shards, rows, cols = part.shape
    tr = _row_block(rows, cols)

    def body(pos_ref, own_ref, r0_ref, r1_ref, r2_ref, r3_ref, out_ref):
        me = pos_ref[0]
        own = own_ref[...].astype(F32)
        total = None
        for k, r_ref in enumerate((r0_ref, r1_ref, r2_ref, r3_ref)):
            term = jnp.where(me == k, own, r_ref[...].astype(F32))
            total = term if total is None else total + term
        out_ref[...] = total

    def other(k):
        return pl.BlockSpec((None, tr, cols),
                            lambda i, pos, k=k: (jnp.where(pos[0] == k, (k + 1) % shards, k), i, 0))

    pos = jnp.stack([_chip_index(), lax.axis_index("c").astype(jnp.int32)])
    grid_spec = pltpu.PrefetchScalarGridSpec(
        num_scalar_prefetch=1, grid=(rows // tr,),
        in_specs=[pl.BlockSpec((None, tr, cols), lambda i, pos: (pos[0], i, 0))] + [other(k) for k in range(shards)],
        out_specs=pl.BlockSpec((None, tr, cols), lambda i, pos: (pos[1], i, 0)))
    return pl.pallas_call(
        body, out_shape=jax.ShapeDtypeStruct((DEPTH, rows, cols), F32), grid_spec=grid_spec,
        name=name, compiler_params=_params(("parallel",)),
    )(pos, part, others, others, others, others)


def _join_layers(arrays):
    n = len(arrays)

    def body(*refs):
        outs, send_sems, recv_sems = refs[n:2 * n], refs[2 * n], refs[2 * n + 1]
        x, y, c, _ = _place()

        def copy(a, layer):
            piece = outs[a].at[layer]
            return pltpu.make_async_remote_copy(src_ref=piece, dst_ref=piece, send_sem=send_sems.at[a],
                                                recv_sem=recv_sems.at[a], device_id=(x, y, 1 - c),
                                                device_id_type=MESH)

        sends = [copy(a, c) for a in range(n)]
        for cp in sends:
            cp.start()
        for a in range(n):
            copy(a, 1 - c).wait_recv()
        for cp in sends:
            cp.wait_send()

    return pl.pallas_call(
        body, out_shape=[jax.ShapeDtypeStruct(a.shape, a.dtype) for a in arrays],
        in_specs=[HBM_SPEC] * n, out_specs=[HBM_SPEC] * n,
        scratch_shapes=[pltpu.SemaphoreType.DMA((n,)), pltpu.SemaphoreType.DMA((n,))],
        input_output_aliases={i: i for i in range(n)}, name="join_layers",
    )(*arrays)


def _reduce_scatter(grads):
    sums = [_add_layers(f"add_layers_{a}", g, o) for a, (g, o) in enumerate(zip(grads, _swap_layers(grads)))]
    others = _scatter_chips(sums)
    return _join_layers([_add_chips(f"add_chips_{a}", p, o) for a, (p, o) in enumerate(zip(sums, others))])


def _adamw(name, w, g, m, v):
    shape = w.shape
    cols = shape[-1]
    rows = w.size // cols
    tr = rows
    for cand in (512, 256, 128, 64, 32, 16, 8):
        if rows % cand == 0 and cand * cols * 4 <= 4 * 1024 * 1024:
            tr = cand
            break

    def body(w_ref, g_ref, m_ref, v_ref, d_ref, nm_ref, nv_ref):
        gv = g_ref[...]
        nm = ADAM_B1 * m_ref[...] + (1.0 - ADAM_B1) * gv
        nv = ADAM_B2 * v_ref[...] + (1.0 - ADAM_B2) * (gv * gv)
        m_hat = nm / (1.0 - ADAM_B1 ** ADAM_STEP)
        v_hat = nv / (1.0 - ADAM_B2 ** ADAM_STEP)
        d_ref[...] = -ADAM_LR * (m_hat / (jnp.sqrt(v_hat) + ADAM_EPS) + ADAM_WD * w_ref[...])
        nm_ref[...] = nm
        nv_ref[...] = nv

    spec = pl.BlockSpec((tr, cols), lambda i: (i, 0))
    two = lambda a: a.reshape(rows, cols)
    outs = pl.pallas_call(
        body, out_shape=[jax.ShapeDtypeStruct((rows, cols), F32)] * 3, grid=(rows // tr,),
        in_specs=[spec] * 4, out_specs=[spec] * 3, name=name, compiler_params=_params(("parallel",)),
    )(two(w), two(g), two(m), two(v))
    return [o.reshape(shape) for o in outs]


BIG = ("ada_w", "ffn_wg", "ffn_wu", "ffn_wd", "w_in", "conv_wo", "w_out")
MISC_ROWS = 96


def _own_slot(shard):
    return lax.dynamic_update_slice(jnp.zeros((SHARDS,) + shard.shape, shard.dtype), shard[None],
                                    (_chip_index(),) + (0,) * shard.ndim)


def _as_matrices(a):
    return a.reshape(a.shape[0], a.shape[1], -1, a.shape[-1])


def kernel(x, c, ada_w, ada_b, norm_g, ffn_wg, ffn_wu, ffn_wd, w_in, attn_wo, conv_w, conv_b, conv_ln_g, conv_ln_b, conv_wo, w_out, final_g, loss_target, m_ada_w, m_ada_b, m_norm_g, m_ffn_wg, m_ffn_wu, m_ffn_wd, m_w_in, m_attn_wo, m_conv_w, m_conv_b, m_conv_ln_g, m_conv_ln_b, m_conv_wo, m_w_out, m_final_g, v_ada_w, v_ada_b, v_norm_g, v_ffn_wg, v_ffn_wu, v_ffn_wd, v_w_in, v_attn_wo, v_conv_w, v_conv_b, v_conv_ln_g, v_conv_ln_b, v_conv_wo, v_w_out, v_final_g):
    weights = dict(ada_w=ada_w, ada_b=ada_b, norm_g=norm_g, ffn_wg=ffn_wg, ffn_wu=ffn_wu, ffn_wd=ffn_wd, w_in=w_in,
                   attn_wo=attn_wo, conv_w=conv_w, conv_b=conv_b, conv_ln_g=conv_ln_g, conv_ln_b=conv_ln_b,
                   conv_wo=conv_wo, w_out=w_out, final_g=final_g)
    moments_m = dict(ada_w=m_ada_w, ada_b=m_ada_b, norm_g=m_norm_g, ffn_wg=m_ffn_wg, ffn_wu=m_ffn_wu,
                     ffn_wd=m_ffn_wd, w_in=m_w_in, attn_wo=m_attn_wo, conv_w=m_conv_w, conv_b=m_conv_b,
                     conv_ln_g=m_conv_ln_g, conv_ln_b=m_conv_ln_b, conv_wo=m_conv_wo, w_out=m_w_out,
                     final_g=m_final_g)
    moments_v = dict(ada_w=v_ada_w, ada_b=v_ada_b, norm_g=v_norm_g, ffn_wg=v_ffn_wg, ffn_wu=v_ffn_wu,
                     ffn_wd=v_ffn_wd, w_in=v_w_in, attn_wo=v_attn_wo, conv_w=v_conv_w, conv_b=v_conv_b,
                     conv_ln_g=v_conv_ln_g, conv_ln_b=v_conv_ln_b, conv_wo=v_conv_wo, w_out=v_w_out,
                     final_g=v_final_g)
    layers, shards = range(DEPTH), range(SHARDS)

    bits = {n: _split_bits(weights[n]) for n in EXACT}
    misc_w = jnp.stack([_pack([attn_wo[l].astype(BF16), bits["norm_g"][0][l], bits["norm_g"][1][l],
                               bits["conv_w"][0][l], bits["conv_w"][1][l]], MISC_ROWS) for l in layers])
    sent = [_own_slot(weights[n].astype(BF16)) for n in BIG] + [_own_slot(misc_w)]
    got = dict(zip(BIG + ("misc",), _gather_weights(sent)))
    w = {n: got[n] for n in ("ada_w", "ffn_wg", "ffn_wu", "ffn_wd")}
    w["w_in"] = got["w_in"].transpose(1, 2, 0, 3).reshape(DEPTH, D, -1)
    for n in ("conv_wo", "w_out"):
        w[n] = got[n].transpose(1, 0, 2, 3).reshape(DEPTH, D, D)
    misc_shapes = [(GW, GW), (3, GW), (3, GW), (CONV_K, GW), (CONV_K, GW)]
    pieces = [[_unpack(got["misc"][k, l], misc_shapes) for k in shards] for l in layers]
    whole = lambda i: jnp.stack([jnp.concatenate([pieces[l][k][i] for k in shards], axis=1) for l in layers])
    w["attn_wo"] = whole(0)
    vectors = dict(ada_b=ada_b, conv_b=conv_b, conv_ln_g=conv_ln_g, conv_ln_b=conv_ln_b, final_g=final_g,
                   norm_g=_join_bits(whole(1), whole(2)), conv_w=_join_bits(whole(3), whole(4)))

    loss_cols, dx, grads = _local_step(x[0], c, loss_target[0], w, vectors)
    loss = lax.psum(jnp.sum(loss_cols), ("x", "y", "c"))

    cols_of = lambda a, k: a[..., k * GW:(k + 1) * GW]
    misc_g = jnp.stack([jnp.stack([_pack(
        [cols_of(grads["attn_wo"][l], k), cols_of(jnp.stack(grads["norm_g"][l]), k), cols_of(grads["conv_w"][l], k),
         grads["ada_b"][l], grads["conv_b"][l], grads["conv_ln_g"][l], grads["conv_ln_b"][l],
         grads["final_g"] if l == 0 else jnp.zeros_like(grads["final_g"])], MISC_ROWS)
        for l in layers]) for k in shards])
    by_chip = dict(
        ada_w=grads["ada_w"], ffn_wg=grads["ffn_wg"], ffn_wu=grads["ffn_wu"], ffn_wd=grads["ffn_wd"],
        w_in=jnp.stack(grads["w_in"]).reshape(DEPTH, D, SHARDS, -1).transpose(2, 0, 1, 3),
        conv_wo=jnp.stack(grads["conv_wo"]).reshape(DEPTH, SHARDS, -1, D).transpose(1, 0, 2, 3),
        w_out=jnp.stack(grads["w_out"]).reshape(DEPTH, SHARDS, -1, D).transpose(1, 0, 2, 3))
    reduced = _reduce_scatter([_as_matrices(by_chip[n]) for n in BIG] + [misc_g])
    summed = {n: r.reshape(weights[n].shape) for n, r in zip(BIG, reduced)}
    small_shapes = [(GW, GW), (3, GW), (CONV_K, GW), (N_MOD * D,), (D,), (D,), (D,), (D,)]
    small = [_unpack(reduced[-1][l], small_shapes) for l in layers]
    for i, n in enumerate(("attn_wo", "norm_g", "conv_w", "ada_b", "conv_b", "conv_ln_g", "conv_ln_b")):
        summed[n] = jnp.stack([small[l][i] for l in layers])
    summed["final_g"] = small[0][7]

    deltas, new_m, new_v = {}, {}, {}
    for n in WEIGHTS:
        deltas[n], new_m[n], new_v[n] = _adamw(f"adamw_{n}", weights[n], summed[n], moments_m[n], moments_v[n])

    return (loss, dx[None], *[summed[n] for n in WEIGHTS], *[deltas[n] for n in WEIGHTS],
            *[new_m[n] for n in WEIGHTS], *[new_v[n] for n in WEIGHTS])
```

```python
import functools

import jax
import jax.numpy as jnp
from jax import lax
from jax.experimental import pallas as pl
from jax.experimental.pallas import tpu as pltpu

F32 = jnp.float32
BF16 = jnp.bfloat16

D = 1024
DFF = 2816
HEAD = 64
GW = 256
DILATIONS = (1, 4, 16)
BAND = 128
QKV = 2304
CONV_K = 31
HALO = 32
N_MOD = 9
EPS = 1e-6
NEG_INF = -1e30
DEPTH = 2

SHARDS = 4
FSH = DFF // SHARDS
LANES = 1024
VL = 128

ADAM_LR = 0.001
ADAM_B1 = 0.9
ADAM_B2 = 0.999
ADAM_EPS = 1e-08
ADAM_WD = 0.01
ADAM_STEP = 10

VMEM_LIMIT = 56 * 1024 * 1024

EXACT = ("norm_g", "conv_w")
WEIGHTS = ("ada_w", "ada_b", "norm_g", "ffn_wg", "ffn_wu", "ffn_wd", "w_in", "attn_wo", "conv_w", "conv_b",
           "conv_ln_g", "conv_ln_b", "conv_wo", "w_out", "final_g")

MESH = pl.DeviceIdType.MESH


def _params(sem=None):
    return pltpu.CompilerParams(dimension_semantics=sem, vmem_limit_bytes=VMEM_LIMIT)


def _sigmoid(v):
    return jax.nn.sigmoid(v)


TILE = "tile"
MM_SLAB = 256


def _lead_spec(arr, lead, block, index):
    def index_map(j, i):
        return (*[j if e == TILE else e for e in lead], *index(j, i))
    return pl.BlockSpec((None,) * len(lead) + tuple(block), index_map)


def _entry(e):
    return e if isinstance(e, tuple) else (e, ())


def _mm(name, a_list, b_list, pairs, epilogue, out_dtypes, *, tm, tn, n_out, trans_b=False, extras=(), vecs=(),
        out_widths=None, out_tiled=False):
    a_list = [_entry(a) for a in a_list]
    b_list = [_entry(b) for b in b_list]
    m = a_list[0][0].shape[-2]
    na, nb, ne, nv = len(a_list), len(b_list), len(extras), len(vecs)
    dn = (((1,), (1,)), ((), ())) if trans_b else (((1,), (0,)), ((), ()))

    def body(*refs):
        a_refs = refs[:na]
        b_refs = refs[na:na + nb]
        e_refs = refs[na + nb:na + nb + ne]
        v_refs = refs[na + nb + ne:na + nb + ne + nv]
        o_refs = refs[na + nb + ne + nv:]
        vec_blocks = [v[...] for v in v_refs]
        for r0 in range(0, tm, min(tm, MM_SLAB)):
            rows = slice(r0, r0 + min(tm, MM_SLAB))
            accs = [lax.dot_general(a_refs[ai][rows, :], b_refs[bi][...], dn, preferred_element_type=F32)
                    for ai, bi in pairs]
            res = epilogue(accs, [e[rows, :] for e in e_refs], vec_blocks)
            for o_ref, r in zip(o_refs, res):
                o_ref[rows, :] = r.astype(o_ref.dtype)

    in_specs = [_lead_spec(a, lead, (tm, a.shape[-1]), lambda j, i: (i, 0)) for a, lead in a_list]
    for b, lead in b_list:
        if TILE in lead:
            in_specs.append(_lead_spec(b, lead, b.shape[-2:], lambda j, i: (0, 0)))
        elif trans_b:
            in_specs.append(_lead_spec(b, lead, (tn, b.shape[-1]), lambda j, i: (j, 0)))
        else:
            in_specs.append(_lead_spec(b, lead, (b.shape[-2], tn), lambda j, i: (0, j)))
    for e, where in extras:
        if isinstance(where, tuple):
            in_specs.append(_lead_spec(e, where, (tm, tn), lambda j, i: (i, 0)))
        else:
            in_specs.append(pl.BlockSpec((tm, tn), functools.partial(lambda j, i, off: (i, j + off), off=where)))
    in_specs += [pl.BlockSpec((1, tn), lambda j, i: (0, j)) for _ in vecs]
    nj = n_out // tn
    widths = out_widths or [tn] * len(out_dtypes)
    if out_tiled:
        out_specs = [pl.BlockSpec((None, tm, tn), lambda j, i: (j, i, 0)) for _ in out_dtypes]
        out_shape = [jax.ShapeDtypeStruct((nj, m, tn), dt) for dt in out_dtypes]
    else:
        out_specs = [pl.BlockSpec((tm, wd), lambda j, i: (i, j)) for wd in widths]
        out_shape = [jax.ShapeDtypeStruct((m, nj * wd), dt) for dt, wd in zip(out_dtypes, widths)]
    return pl.pallas_call(
        body, out_shape=out_shape, grid=(nj, m // tm), in_specs=in_specs, out_specs=out_specs,
        name=name, compiler_params=_params(("parallel", "parallel")),
    )(*[a for a, _ in a_list], *[b for b, _ in b_list], *[e for e, _ in extras], *vecs)


def _mm_tn(name, a, b, *, tk, tn, tt, a_fn=None, a_tiled=False, b_tiled=False, into=None):
    a_list = list(a) if a_fn is not None else [a]
    na = len(a_list)
    t = a_list[0].shape[-2]
    nk = a_list[0].shape[0] if a_tiled else a_list[0].shape[1] // tk
    nn = b.shape[0] if b_tiled else b.shape[1] // tn
    steps = t // tt
    has_into = into is not None

    def body(*refs):
        refs = refs[1:] if has_into else refs
        a_refs, b_ref, o_ref, acc_ref = refs[:na], refs[na], refs[na + 1], refs[na + 2]
        s = pl.program_id(2)

        @pl.when(s == 0)
        def _():
            acc_ref[...] = jnp.zeros_like(acc_ref)

        av = a_refs[0][...] if a_fn is None else a_fn([r[...] for r in a_refs])
        acc_ref[...] += lax.dot_general(av, b_ref[...], (((0,), (0,)), ((), ())), preferred_element_type=F32)

        @pl.when(s == steps - 1)
        def _():
            o_ref[...] = acc_ref[...]

    a_spec = (pl.BlockSpec((None, tt, tk), lambda i, j, s: (i, s, 0)) if a_tiled
              else pl.BlockSpec((tt, tk), lambda i, j, s: (s, i)))
    b_spec = (pl.BlockSpec((None, tt, tn), lambda i, j, s: (j, s, 0)) if b_tiled
              else pl.BlockSpec((tt, tn), lambda i, j, s: (s, j)))
    if a_tiled:
        out_dims, tile_index = (nk, tk, nn * tn), lambda i, j, s: (i, 0, j)
    elif b_tiled:
        out_dims, tile_index = (nn, nk * tk, tn), lambda i, j, s: (j, i, 0)
    else:
        out_dims, tile_index = (nk * tk, nn * tn), lambda i, j, s: (i, j)
    tiled = a_tiled or b_tiled
    if has_into:
        buf, lead = into
        def out_index(i, j, s):
            idx = tile_index(i, j, s)
            return (idx[0], *lead, *idx[1:])
        out_spec = pl.BlockSpec((None,) * (1 + len(lead)) + (tk, tn), out_index)
        out_shape = jax.ShapeDtypeStruct(buf.shape, buf.dtype)
        extra_in, extra_specs, aliases = [buf], [pl.BlockSpec(memory_space=pl.ANY)], {0: 0}
    else:
        out_spec = pl.BlockSpec(((None,) if tiled else ()) + (tk, tn), tile_index)
        out_shape = jax.ShapeDtypeStruct(out_dims, F32)
        extra_in, extra_specs, aliases = [], [], {}
    return pl.pallas_call(
        body, out_shape=out_shape, grid=(nk, nn, steps), in_specs=extra_specs + [a_spec] * na + [b_spec],
        out_specs=out_spec, scratch_shapes=[pltpu.VMEM((tk, tn), F32)], input_output_aliases=aliases, name=name,
        compiler_params=_params(("parallel", "parallel", "arbitrary")),
    )(*extra_in, *a_list, b)


def _first(accs, extras, vecs):
    return [accs[0]]


def _total(accs, extras, vecs):
    out = accs[0]
    for r in accs[1:]:
        out = out + r
    return [out]


def _row_spec(tm, width, col=0):
    return pl.BlockSpec((tm, width), functools.partial(lambda i, col: (i, col), col=col))


def _vec_spec(width):
    return pl.BlockSpec((1, width), lambda i: (0, 0))


def _normmod_fwd(name, x, g, scale, shift, tm=512):
    t = x.shape[0]

    def body(x_ref, g_ref, sc_ref, sh_ref, h_ref):
        xv = x_ref[...]
        r = lax.rsqrt(jnp.mean(xv * xv, axis=-1, keepdims=True) + EPS)
        h_ref[...] = ((xv * r) * g_ref[...] * (1.0 + sc_ref[...]) + sh_ref[...]).astype(BF16)

    return pl.pallas_call(
        body, out_shape=jax.ShapeDtypeStruct((t, D), BF16), grid=(t // tm,),
        in_specs=[_row_spec(tm, D), _vec_spec(D), _vec_spec(D), _vec_spec(D)], out_specs=_row_spec(tm, D),
        name=name, compiler_params=_params(("parallel",)),
    )(x, g, scale, shift)


def _normmod_bwd(name, x, dh, dres, g, scale, tm=256):
    t = x.shape[0]
    steps = t // tm

    def body(x_ref, dh_ref, dres_ref, g_ref, sc_ref, dx_ref, dg_ref, dsc_ref, dsh_ref):
        i = pl.program_id(0)

        @pl.when(i == 0)
        def _():
            dg_ref[...] = jnp.zeros_like(dg_ref)
            dsh_ref[...] = jnp.zeros_like(dsh_ref)

        xv = x_ref[...]
        dh = dh_ref[...]
        r = lax.rsqrt(jnp.mean(xv * xv, axis=-1, keepdims=True) + EPS)
        xh = xv * r
        dxh = dh * (g_ref[...] * (1.0 + sc_ref[...]))
        dx_ref[...] = dres_ref[...] + r * (dxh - xh * jnp.mean(dxh * xh, axis=-1, keepdims=True))
        dg_ref[...] += jnp.sum(dh * xh, axis=0, keepdims=True)
        dsh_ref[...] += jnp.sum(dh, axis=0, keepdims=True)

        @pl.when(i == steps - 1)
        def _():
            acc = dg_ref[...]
            dg_ref[...] = acc * (1.0 + sc_ref[...])
            dsc_ref[...] = acc * g_ref[...]

    vec = jax.ShapeDtypeStruct((1, D), F32)
    return pl.pallas_call(
        body, out_shape=[jax.ShapeDtypeStruct((t, D), F32), vec, vec, vec], grid=(steps,),
        in_specs=[_row_spec(tm, D), _row_spec(tm, D), _row_spec(tm, D), _vec_spec(D), _vec_spec(D)],
        out_specs=[_row_spec(tm, D), _vec_spec(D), _vec_spec(D), _vec_spec(D)],
        name=name, compiler_params=_params(("arbitrary",)),
    )(x, dh, dres, g, scale)


def _resgate_bwd(name, dx, f, gate, coef, tm=512):
    t = dx.shape[0]

    def body(dx_ref, f_ref, gate_ref, df_ref, dgate_ref):
        @pl.when(pl.program_id(0) == 0)
        def _():
            dgate_ref[...] = jnp.zeros_like(dgate_ref)

        dxv = dx_ref[...]
        df_ref[...] = ((coef * gate_ref[...]) * dxv).astype(BF16)
        dgate_ref[...] += jnp.sum((coef * f_ref[...].astype(F32)) * dxv, axis=0, keepdims=True)

    return pl.pallas_call(
        body, out_shape=[jax.ShapeDtypeStruct((t, D), BF16), jax.ShapeDtypeStruct((1, D), F32)], grid=(t // tm,),
        in_specs=[_row_spec(tm, D), _row_spec(tm, D), _vec_spec(D)], out_specs=[_row_spec(tm, D), _vec_spec(D)],
        name=name, compiler_params=_params(("arbitrary",)),
    )(dx, f, gate)


def _loss_bwd(name, x, target, g, tm=256):
    t = x.shape[0]

    def body(x_ref, t_ref, g_ref, dx_ref, dg_ref, loss_ref):
        @pl.when(pl.program_id(0) == 0)
        def _():
            dg_ref[...] = jnp.zeros_like(dg_ref)
            loss_ref[...] = jnp.zeros_like(loss_ref)

        xv = x_ref[...]
        r = lax.rsqrt(jnp.mean(xv * xv, axis=-1, keepdims=True) + EPS)
        xh = xv * r
        err = xh * g_ref[...] - t_ref[...]
        dy = err * (1.0 / D)
        dxh = dy * g_ref[...]
        dx_ref[...] = r * (dxh - xh * jnp.mean(dxh * xh, axis=-1, keepdims=True))
        dg_ref[...] += jnp.sum(dy * xh, axis=0, keepdims=True)
        loss_ref[...] += jnp.sum(err * err, axis=0, keepdims=True) * (0.5 / D)

    vec = jax.ShapeDtypeStruct((1, D), F32)
    return pl.pallas_call(
        body, out_shape=[jax.ShapeDtypeStruct((t, D), F32), vec, vec], grid=(t // tm,),
        in_specs=[_row_spec(tm, D), _row_spec(tm, D), _vec_spec(D)],
        out_specs=[_row_spec(tm, D), _vec_spec(D), _vec_spec(D)],
        name=name, compiler_params=_params(("arbitrary",)),
    )(x, target, g)


def _mod_fwd(name, c8, ada_w, layer, ada_b):
    tn = ada_w.shape[-1]

    def body(c_ref, w_ref, b_ref, o_ref):
        cv = c_ref[...]
        ca = (cv * _sigmoid(cv)).astype(BF16)
        o_ref[...] = jnp.dot(ca, w_ref[...], preferred_element_type=F32) + b_ref[...]

    return pl.pallas_call(
        body, out_shape=jax.ShapeDtypeStruct((8, SHARDS * tn), F32), grid=(SHARDS,),
        in_specs=[pl.BlockSpec((8, D), lambda j: (0, 0)), pl.BlockSpec((None, None, D, tn), lambda j: (j, layer, 0, 0)),
                  pl.BlockSpec((1, tn), lambda j: (0, j))],
        out_specs=pl.BlockSpec((8, tn), lambda j: (0, j)), name=name, compiler_params=_params(("parallel",)),
    )(c8, ada_w, ada_b)


def _mod_bwd(name, c_col, dmods, tk=256):
    tn = dmods.shape[-1] // SHARDS

    def body(c_ref, d_ref, o_ref):
        cv = c_ref[...]
        o_ref[...] = (cv * _sigmoid(cv)) * d_ref[...]

    return pl.pallas_call(
        body, out_shape=jax.ShapeDtypeStruct((SHARDS, DEPTH, D, tn), F32), grid=(SHARDS, DEPTH, D // tk),
        in_specs=[pl.BlockSpec((tk, 1), lambda j, l, i: (i, 0)), pl.BlockSpec((None, 1, tn), lambda j, l, i: (l, 0, j))],
        out_specs=pl.BlockSpec((None, None, tk, tn), lambda j, l, i: (j, l, i, 0)), name=name,
        compiler_params=_params(("parallel", "parallel", "parallel")),
    )(c_col, dmods)


def _rope_tables(t):
    half = HEAD // 2
    inv_freq = 10000.0 ** (-(jnp.arange(half, dtype=F32) * 2.0 / HEAD))
    ang = jnp.arange(t, dtype=F32)[:, None] * inv_freq[None, :]
    cos, sin = jnp.cos(ang), jnp.sin(ang)
    cos_t = jnp.tile(jnp.concatenate([cos, cos], axis=1), (1, VL // HEAD))
    sin_t = jnp.tile(jnp.concatenate([-sin, sin], axis=1), (1, VL // HEAD))
    return cos_t, sin_t


def _rotate(tv, cos, sin_signed):
    lane = lax.broadcasted_iota(jnp.int32, tv.shape, 1)
    first = (lane % HEAD) < (HEAD // 2)
    partner = jnp.where(first, pltpu.roll(tv, tv.shape[1] - HEAD // 2, 1), pltpu.roll(tv, HEAD // 2, 1))
    return tv * cos + partner * sin_signed


def _dilated_spec(tm, d):
    return pl.BlockSpec((tm // d, d * GW), lambda i: (i, 0))


def _dilated_shape(t, d, dtype):
    return jax.ShapeDtypeStruct((t // d, d * GW), dtype)


def _every(d, r, tm):
    return pl.ds(r, tm // d, stride=d) if d > 1 else slice(None)


def _rope_fwd(name, zqkv, cos_t, sin_t, tm=512):
    t = zqkv.shape[0]
    ng = len(DILATIONS)
    n = 3 * ng

    halves = GW // VL

    def body(*refs):
        z_refs, cos_ref, sin_ref, o_refs = refs[:halves * n], refs[halves * n], refs[halves * n + 1], refs[halves * n + 2:]
        for idx in range(n):
            d = DILATIONS[idx % ng]
            for r in range(d):
                rows = _every(d, r, tm)
                for hh in range(halves):
                    piece = z_refs[halves * idx + hh][rows, :]
                    if idx < 2 * ng:
                        piece = _rotate(piece, cos_ref[rows, :], sin_ref[rows, :])
                    if idx < ng:
                        piece = piece * (HEAD ** -0.5)
                    o_refs[idx][:, r * GW + hh * VL:r * GW + (hh + 1) * VL] = piece.astype(BF16)

    dils = [DILATIONS[idx % ng] for idx in range(n)]
    return pl.pallas_call(
        body, out_shape=[_dilated_shape(t, d, BF16) for d in dils], grid=(t // tm,),
        in_specs=[_row_spec(tm, VL, col) for col in range(halves * n)] + [_row_spec(tm, VL), _row_spec(tm, VL)],
        out_specs=[_dilated_spec(tm, d) for d in dils], name=name, compiler_params=_params(("parallel",)),
    )(*([zqkv] * (halves * n)), cos_t, sin_t)


def _rope_bwd(name, grads, cos_t, sin_t, tm=512):
    ng = len(DILATIONS)
    n = len(grads)
    t = grads[0].shape[0] * DILATIONS[0]

    halves = GW // VL

    def body(*refs):
        g_refs, cos_ref, sin_ref, o_ref, rows_ref = refs[:n], refs[n], refs[n + 1], refs[n + 2], refs[n + 3]
        cos, sin = cos_ref[...], -sin_ref[...]
        for idx in range(n):
            d = DILATIONS[idx % ng]
            for hh in range(halves):
                for r in range(d):
                    cols = slice(r * GW + hh * VL, r * GW + (hh + 1) * VL)
                    rows_ref[hh, _every(d, r, tm), :] = g_refs[idx][:, cols].astype(F32)
                piece = rows_ref[hh]
                if idx < 2 * ng:
                    piece = _rotate(piece, cos, sin)
                if idx < ng:
                    piece = piece * (HEAD ** -0.5)
                o_ref[:, idx * GW + hh * VL:idx * GW + (hh + 1) * VL] = piece.astype(BF16)

    dils = [DILATIONS[idx % ng] for idx in range(n)]
    return pl.pallas_call(
        body, out_shape=jax.ShapeDtypeStruct((t, n * GW), BF16), grid=(t // tm,),
        in_specs=[_dilated_spec(tm, d) for d in dils] + [_row_spec(tm, VL)] * 2, out_specs=_row_spec(tm, n * GW),
        scratch_shapes=[pltpu.VMEM((halves, tm, VL), F32)], name=name, compiler_params=_params(("parallel",)),
    )(*grads, cos_t, sin_t)


def _head_cols(h):
    return slice(h * HEAD, (h + 1) * HEAD)


def _band_mask_q(has_prev):
    qi = lax.broadcasted_iota(jnp.int32, (BAND, 2 * BAND), 0)
    kj = lax.broadcasted_iota(jnp.int32, (BAND, 2 * BAND), 1)
    dist = qi + BAND - kj
    return (dist >= 0) & (dist <= BAND) & ((kj >= BAND) | has_prev)


def _attn_fwd(name, q, k, v, group):
    d = DILATIONS[group]
    length = q.shape[0]
    qb = min(512, length)
    sub = qb // BAND
    nblk = length // qb

    def body(q_ref, kc_ref, kp_ref, vc_ref, vp_ref, o_ref, lse_ref):
        blk = pl.program_id(1)
        k_ext = jnp.concatenate([kp_ref[...], kc_ref[...]], axis=0)
        v_ext = jnp.concatenate([vp_ref[...], vc_ref[...]], axis=0)
        for j in range(sub):
            mask = _band_mask_q((blk * sub + j) > 0)
            qj = q_ref[j * BAND:(j + 1) * BAND, :]
            kj = k_ext[j * BAND:(j + 2) * BAND, :]
            vj = v_ext[j * BAND:(j + 2) * BAND, :]
            outs, lses = [], []
            for h in range(GW // HEAD):
                s = lax.dot_general(qj[:, _head_cols(h)], kj[:, _head_cols(h)], (((1,), (1,)), ((), ())),
                                    preferred_element_type=F32)
                s = jnp.where(mask, s, NEG_INF)
                m = jnp.max(s, axis=-1, keepdims=True)
                p = jnp.exp(s - m)
                den = jnp.sum(p, axis=-1, keepdims=True)
                o = jnp.dot(p.astype(BF16), vj[:, _head_cols(h)], preferred_element_type=F32)
                outs.append(o / den)
                lses.append(jnp.broadcast_to(m + jnp.log(den), (BAND, HEAD)))
            o_ref[j * BAND:(j + 1) * BAND, :] = jnp.concatenate(outs, axis=1)
            lse_ref[j * BAND:(j + 1) * BAND, :] = jnp.concatenate(lses, axis=1)

    prev = qb // BAND
    cur = lambda r, b: (b, r)
    before = lambda r, b: (jnp.maximum(b * prev - 1, 0), r)
    big, halo = pl.BlockSpec((qb, GW), cur), pl.BlockSpec((BAND, GW), before)
    return pl.pallas_call(
        body, out_shape=[jax.ShapeDtypeStruct((length, d * GW), F32)] * 2, grid=(d, nblk),
        in_specs=[big, big, halo, big, halo], out_specs=[big] * 2, name=name,
        compiler_params=_params(("parallel", "parallel")),
    )(q, k, k, v, v)


def _attn_merge(name, outs, lses, tm=512):
    n = len(outs)
    t = outs[0].shape[0] * DILATIONS[0]

    halves = GW // VL

    def body(*refs):
        in_refs = refs[:2 * n]
        ob_ref, of_ref, lj_ref, rows_ref = refs[2 * n:]
        for hh in range(halves):
            for idx in range(2 * n):
                d = DILATIONS[idx % n]
                for r in range(d):
                    cols = slice(r * GW + hh * VL, r * GW + (hh + 1) * VL)
                    rows_ref[idx, _every(d, r, tm), :] = in_refs[idx][:, cols]
            ls = [rows_ref[n + g] for g in range(n)]
            m = ls[0]
            for v in ls[1:]:
                m = jnp.maximum(m, v)
            es = [jnp.exp(v - m) for v in ls]
            tot = es[0]
            for v in es[1:]:
                tot = tot + v
            acc = (es[0] / tot) * rows_ref[0]
            for g in range(1, n):
                acc = acc + (es[g] / tot) * rows_ref[g]
            half = slice(hh * VL, (hh + 1) * VL)
            ob_ref[:, half] = acc.astype(BF16)
            of_ref[:, half] = acc
            lj_ref[:, half] = m + jnp.log(tot)

    return pl.pallas_call(
        body, out_shape=[jax.ShapeDtypeStruct((t, GW), BF16), jax.ShapeDtypeStruct((t, GW), F32),
                         jax.ShapeDtypeStruct((t, GW), F32)], grid=(t // tm,),
        in_specs=[_dilated_spec(tm, DILATIONS[idx % n]) for idx in range(2 * n)], out_specs=[_row_spec(tm, GW)] * 3,
        scratch_shapes=[pltpu.VMEM((2 * n, tm, VL), F32)], name=name, compiler_params=_params(("parallel",)),
    )(*outs, *lses)


def _attn_bwd_prep(name, do, o, lj, tm=512):
    t = do.shape[0]
    n = len(DILATIONS)

    halves = GW // VL
    per_half = VL // HEAD

    def body(*refs):
        do_refs, o_refs, lj_refs = refs[:halves], refs[halves:2 * halves], refs[2 * halves:3 * halves]
        outs, dsum_ref = refs[3 * halves:3 * halves + 3 * n], refs[3 * halves + 3 * n]
        for hh in range(halves):
            prod = do_refs[hh][...] * o_refs[hh][...]
            parts = [jnp.broadcast_to(jnp.sum(prod[:, _head_cols(h)], axis=-1, keepdims=True), (tm, HEAD))
                     for h in range(per_half)]
            dsum_ref[...] = jnp.concatenate(parts, axis=1)
            for g, d in enumerate(DILATIONS):
                for r in range(d):
                    rows, cols = _every(d, r, tm), slice(r * GW + hh * VL, r * GW + (hh + 1) * VL)
                    outs[g][:, cols] = dsum_ref[rows, :]
                    outs[n + g][:, cols] = do_refs[hh][rows, :].astype(BF16)
                    outs[2 * n + g][:, cols] = lj_refs[hh][rows, :]

    shapes = [_dilated_shape(t, d, dt) for dt in (F32, BF16, F32) for d in DILATIONS]
    half_specs = [_row_spec(tm, VL, hh) for hh in range(halves)]
    return pl.pallas_call(
        body, out_shape=shapes, grid=(t // tm,), in_specs=half_specs * 3,
        out_specs=[_dilated_spec(tm, d) for d in DILATIONS] * 3, scratch_shapes=[pltpu.VMEM((tm, VL), F32)],
        name=name, compiler_params=_params(("parallel",)),
    )(*([do] * halves), *([o] * halves), *([lj] * halves))


def _attn_bwd(name, q, k, v, do, lj, dsum, group):
    d = DILATIONS[group]
    length = q.shape[0]
    qb = min(512, length)
    sub = qb // BAND
    nblk = length // qb
    total = length // BAND

    def body(qc_ref, qn_ref, kc_ref, kp_ref, vc_ref, vp_ref, doc_ref, don_ref, ljc_ref, ljn_ref, dsc_ref, dsn_ref,
             dq_ref, dk_ref, dv_ref):
        blk = pl.program_id(1)
        k_ext = jnp.concatenate([kp_ref[...], kc_ref[...]], axis=0)
        v_ext = jnp.concatenate([vp_ref[...], vc_ref[...]], axis=0)
        heads = range(GW // HEAD)
        nt = (((1,), (1,)), ((), ()))
        tn = (((0,), (0,)), ((), ()))

        def scores(qh, doh, ljh, dsh, kh, vh, mask):
            s = lax.dot_general(qh, kh, nt, preferred_element_type=F32)
            p = jnp.where(mask, jnp.exp(s - ljh), 0.0)
            dp = lax.dot_general(doh, vh, nt, preferred_element_type=F32)
            return p.astype(BF16), (p * (dp - dsh)).astype(BF16)

        held_k, held_v = [None] * len(heads), [None] * len(heads)
        for j in range(sub):
            rows = slice(j * BAND, (j + 1) * BAND)
            rows2 = slice(j * BAND, (j + 2) * BAND)
            mask = _band_mask_q((blk * sub + j) > 0)
            dqs, dks, dvs = [], [], []
            for h in heads:
                hc = _head_cols(h)
                col = slice(h * HEAD, h * HEAD + 1)
                qh, doh, kh2 = qc_ref[rows, hc], doc_ref[rows, hc], k_ext[rows2, hc]
                p, ds = scores(qh, doh, ljc_ref[rows, col], dsc_ref[rows, col], kh2, v_ext[rows2, hc], mask)
                dqs.append(jnp.dot(ds, kh2, preferred_element_type=F32))
                dk2 = lax.dot_general(ds, qh, tn, preferred_element_type=F32)
                dv2 = lax.dot_general(p, doh, tn, preferred_element_type=F32)
                if j > 0:
                    dks.append(held_k[h] + dk2[:BAND])
                    dvs.append(held_v[h] + dv2[:BAND])
                held_k[h], held_v[h] = dk2[BAND:], dv2[BAND:]
            dq_ref[rows, :] = jnp.concatenate(dqs, axis=1)
            if j > 0:
                done = slice((j - 1) * BAND, j * BAND)
                dk_ref[done, :] = jnp.concatenate(dks, axis=1)
                dv_ref[done, :] = jnp.concatenate(dvs, axis=1).astype(BF16)

        last = slice((sub - 1) * BAND, sub * BAND)
        qi = lax.broadcasted_iota(jnp.int32, (BAND, BAND), 0)
        kj = lax.broadcasted_iota(jnp.int32, (BAND, BAND), 1)
        mask = (kj >= qi) & ((blk + 1) * sub < total)
        dks, dvs = [], []
        for h in heads:
            hc = _head_cols(h)
            col = slice(h * HEAD, h * HEAD + 1)
            qh, doh = qn_ref[:, hc], don_ref[:, hc]
            p, ds = scores(qh, doh, ljn_ref[:, col], dsn_ref[:, col], kc_ref[last, hc], vc_ref[last, hc], mask)
            dks.append(held_k[h] + lax.dot_general(ds, qh, tn, preferred_element_type=F32))
            dvs.append(held_v[h] + lax.dot_general(p, doh, tn, preferred_element_type=F32))
        dk_ref[last, :] = jnp.concatenate(dks, axis=1)
        dv_ref[last, :] = jnp.concatenate(dvs, axis=1).astype(BF16)

    prev = qb // BAND
    cur = lambda r, b: (b, r)
    before = lambda r, b: (jnp.maximum(b * prev - 1, 0), r)
    after = lambda r, b: (jnp.minimum((b + 1) * prev, total - 1), r)
    big = pl.BlockSpec((qb, GW), cur)
    nxt = pl.BlockSpec((BAND, GW), after)
    prv = pl.BlockSpec((BAND, GW), before)
    return pl.pallas_call(
        body, out_shape=[jax.ShapeDtypeStruct((length, d * GW), F32), jax.ShapeDtypeStruct((length, d * GW), F32),
                         jax.ShapeDtypeStruct((length, d * GW), BF16)], grid=(d, nblk),
        in_specs=[big, nxt, big, prv, big, prv, big, nxt, big, nxt, big, nxt],
        out_specs=[big] * 3, name=name, compiler_params=_params(("parallel", "parallel")),
    )(q, q, k, k, v, v, do, do, lj, lj, dsum, dsum)


SUBLANES = 8
CONV_CHUNK = 32
SHIFT_ROWS = HALO - SUBLANES


def _shifted_copies(buf_ref, sh_ref, tm):
    for s in range(1, SUBLANES):
        sh_ref[s - 1] = buf_ref[s:s + tm + SHIFT_ROWS, :]


def _window(buf_ref, sh_ref, offset, r0, rows):
    tiles, shift = divmod(offset, SUBLANES)
    src = buf_ref if shift == 0 else sh_ref.at[shift - 1]
    return src[pl.ds(pl.multiple_of(r0 + tiles * SUBLANES, SUBLANES), rows), :]


def _conv_fwd(name, zu, conv_w, conv_b, ln_g, ln_b, tm=256):
    t = zu.shape[0]
    per = tm // HALO

    def body(a_ref, gl_ref, ah_ref, glh_ref, w_ref, b_ref, g_ref, beta_ref, hc_ref, s_ref, ext_ref, sh_ref):
        i = pl.program_id(0)
        halo = ah_ref[...] * _sigmoid(glh_ref[...])
        ext_ref[0:HALO, :] = jnp.where(i > 0, halo, 0.0)
        ext_ref[HALO:, :] = a_ref[...] * _sigmoid(gl_ref[...])
        _shifted_copies(ext_ref, sh_ref, tm)

        def chunk(r, carry):
            r0 = pl.multiple_of(r * CONV_CHUNK, CONV_CHUNK)
            part = jnp.broadcast_to(b_ref[...], (CONV_CHUNK, D))
            for kk in range(CONV_K):
                part = part + w_ref[kk:kk + 1, :] * _window(ext_ref, sh_ref, HALO - CONV_K + 1 + kk, r0, CONV_CHUNK)
            hc_ref[pl.ds(r0, CONV_CHUNK), :] = part
            return carry

        lax.fori_loop(0, tm // CONV_CHUNK, chunk, 0)
        acc = hc_ref[...]
        mu = jnp.mean(acc, axis=-1, keepdims=True)
        xc = acc - mu
        var = jnp.mean(xc * xc, axis=-1, keepdims=True)
        ln = xc * lax.rsqrt(var + EPS) * g_ref[...] + beta_ref[...]
        s_ref[...] = (ln * _sigmoid(ln)).astype(BF16)

    halo_map = lambda col: (lambda i: (jnp.maximum(i * per - 1, 0), col))
    return pl.pallas_call(
        body, out_shape=[jax.ShapeDtypeStruct((t, D), F32), jax.ShapeDtypeStruct((t, D), BF16)], grid=(t // tm,),
        in_specs=[_row_spec(tm, D, 0), _row_spec(tm, D, 1), pl.BlockSpec((HALO, D), halo_map(0)),
                  pl.BlockSpec((HALO, D), halo_map(1)), pl.BlockSpec((HALO, D), lambda i: (0, 0)),
                  _vec_spec(D), _vec_spec(D), _vec_spec(D)],
        out_specs=[_row_spec(tm, D), _row_spec(tm, D)],
        scratch_shapes=[pltpu.VMEM((tm + HALO, D), F32), pltpu.VMEM((SUBLANES - 1, tm + SHIFT_ROWS, D), F32)],
        name=name, compiler_params=_params(("parallel",)),
    )(zu, zu, zu, zu, conv_w, conv_b, ln_g, ln_b)


def _conv_ln_bwd(name, hc, ds, ln_g, ln_b, tm=256):
    t = hc.shape[0]

    def body(hc_ref, ds_ref, g_ref, beta_ref, dhc_ref, dg_ref, dbeta_ref, dbias_ref):
        @pl.when(pl.program_id(0) == 0)
        def _():
            dg_ref[...] = jnp.zeros_like(dg_ref)
            dbeta_ref[...] = jnp.zeros_like(dbeta_ref)
            dbias_ref[...] = jnp.zeros_like(dbias_ref)

        hv = hc_ref[...]
        mu = jnp.mean(hv, axis=-1, keepdims=True)
        xc = hv - mu
        rstd = lax.rsqrt(jnp.mean(xc * xc, axis=-1, keepdims=True) + EPS)
        xh = xc * rstd
        ln = xh * g_ref[...] + beta_ref[...]
        sg = _sigmoid(ln)
        dln = ds_ref[...] * (sg * (1.0 + ln * (1.0 - sg)))
        dxh = dln * g_ref[...]
        dh = rstd * (dxh - jnp.mean(dxh, axis=-1, keepdims=True) - xh * jnp.mean(dxh * xh, axis=-1, keepdims=True))
        dhc_ref[...] = dh
        dg_ref[...] += jnp.sum(dln * xh, axis=0, keepdims=True)
        dbeta_ref[...] += jnp.sum(dln, axis=0, keepdims=True)
        dbias_ref[...] += jnp.sum(dh, axis=0, keepdims=True)

    vec = jax.ShapeDtypeStruct((1, D), F32)
    return pl.pallas_call(
        body, out_shape=[jax.ShapeDtypeStruct((t, D), F32), vec, vec, vec], grid=(t // tm,),
        in_specs=[_row_spec(tm, D), _row_spec(tm, D), _vec_spec(D), _vec_spec(D)],
        out_specs=[_row_spec(tm, D), _vec_spec(D), _vec_spec(D), _vec_spec(D)],
        name=name, compiler_params=_params(("arbitrary",)),
    )(hc, ds, ln_g, ln_b)


def _conv_bwd(name, zu, dhc, conv_w, tm=256):
    t = zu.shape[0]
    per = tm // HALO
    steps = t // tm

    group = 4

    def body(a_ref, gl_ref, ah_ref, glh_ref, d_ref, dn_ref, w_ref, dz_ref, dw_ref, ext_ref, sh_ref, dext_ref, dsh_ref,
             sg_ref, part_ref):
        i = pl.program_id(0)

        @pl.when(i == 0)
        def _():
            part_ref[...] = jnp.zeros_like(part_ref)

        sg_ref[...] = _sigmoid(gl_ref[...])
        ext_ref[0:HALO, :] = jnp.where(i > 0, ah_ref[...] * _sigmoid(glh_ref[...]), 0.0)
        ext_ref[HALO:, :] = a_ref[...] * sg_ref[...]
        dext_ref[0:tm, :] = d_ref[...]
        dext_ref[tm:, :] = jnp.where(i < steps - 1, dn_ref[...], 0.0)
        _shifted_copies(ext_ref, sh_ref, tm)
        _shifted_copies(dext_ref, dsh_ref, tm)

        def chunk(r, carry):
            r0 = pl.multiple_of(r * CONV_CHUNK, CONV_CHUNK)
            rows = pl.ds(r0, CONV_CHUNK)
            part = jnp.zeros((CONV_CHUNK, D), F32)
            for kk in range(CONV_K):
                part = part + w_ref[kk:kk + 1, :] * _window(dext_ref, dsh_ref, CONV_K - 1 - kk, r0, CONV_CHUNK)
            sg = sg_ref[rows, :]
            dz_ref[rows, 0:D] = (part * sg).astype(BF16)
            dz_ref[rows, D:] = (part * a_ref[rows, :] * sg * (1.0 - sg)).astype(BF16)
            return carry

        lax.fori_loop(0, tm // CONV_CHUNK, chunk, 0)

        for k0 in range(0, CONV_K, group):
            taps = range(k0, min(k0 + group, CONV_K))

            def tile(r, parts, taps=taps):
                r0 = pl.multiple_of(r * CONV_CHUNK, CONV_CHUNK)
                dv = d_ref[pl.ds(r0, CONV_CHUNK), :]
                out = []
                for p, kk in zip(parts, taps):
                    prod = dv * _window(ext_ref, sh_ref, HALO - CONV_K + 1 + kk, r0, CONV_CHUNK)
                    for s in range(0, CONV_CHUNK, SUBLANES):
                        p = p + prod[s:s + SUBLANES, :]
                    out.append(p)
                return tuple(out)

            parts = lax.fori_loop(0, tm // CONV_CHUNK, tile, tuple(jnp.zeros((SUBLANES, D), F32) for _ in taps))
            for p, kk in zip(parts, taps):
                part_ref[kk * SUBLANES:(kk + 1) * SUBLANES, :] += p

        @pl.when(i == steps - 1)
        def _():
            for kk in range(HALO):
                dw_ref[kk:kk + 1, :] = jnp.sum(part_ref[kk * SUBLANES:(kk + 1) * SUBLANES, :], axis=0, keepdims=True)

    halo_map = lambda col: (lambda i: (jnp.maximum(i * per - 1, 0), col))
    shifted = pltpu.VMEM((SUBLANES - 1, tm + SHIFT_ROWS, D), F32)
    return pl.pallas_call(
        body, out_shape=[jax.ShapeDtypeStruct((t, 2 * D), BF16), jax.ShapeDtypeStruct((HALO, D), F32)],
        grid=(steps,),
        in_specs=[_row_spec(tm, D, 0), _row_spec(tm, D, 1), pl.BlockSpec((HALO, D), halo_map(0)),
                  pl.BlockSpec((HALO, D), halo_map(1)), _row_spec(tm, D),
                  pl.BlockSpec((HALO, D), lambda i: (jnp.minimum((i + 1) * per, t // HALO - 1), 0)),
                  pl.BlockSpec((HALO, D), lambda i: (0, 0))],
        out_specs=[_row_spec(tm, 2 * D), pl.BlockSpec((HALO, D), lambda i: (0, 0))],
        scratch_shapes=[pltpu.VMEM((tm + HALO, D), F32), shifted, pltpu.VMEM((tm + HALO, D), F32), shifted,
                        pltpu.VMEM((tm, D), F32), pltpu.VMEM((HALO * SUBLANES, D), F32)],
        name=name, compiler_params=_params(("arbitrary",)),
    )(zu, zu, zu, zu, dhc, dhc, conv_w)


def _ffn_fwd(tag, x, h, w, at, gate):
    def act(accs, extras, vecs):
        g, u = accs
        sg = _sigmoid(g)
        silu = g * sg
        return [silu, sg * (1.0 + g * (1.0 - sg)), u, silu * u]

    sv, tv, uv, av = _mm(f"ffn_up_{tag}", [h], [(w["ffn_wg"], (TILE, *at)), (w["ffn_wu"], (TILE, *at))],
                         [(0, 0), (0, 1)], act, [BF16] * 4, tm=1024, tn=FSH, n_out=DFF, out_tiled=True)

    def residual(accs, extras, vecs):
        tot, = _total(accs, extras, vecs)
        return [extras[0] + (0.5 * vecs[0]) * tot, tot]

    shards = range(SHARDS)
    x_new, f = _mm(f"ffn_down_{tag}", [(av, (k,)) for k in shards], [(w["ffn_wd"], (k, *at)) for k in shards],
                   [(k, k) for k in shards], residual, [F32, BF16], tm=512, tn=512, n_out=D, extras=[(x, 0)],
                   vecs=[gate])
    return x_new, (sv, tv, uv, f)


def _ffn_bwd(tag, dx, x, h, saved, w, at, g, scale, gate, into):
    sv, tv, uv, f = saved
    df, dgate = _resgate_bwd(f"ffn_gate_bwd_{tag}", dx, f, gate, 0.5)

    def act(blocks):
        return (blocks[0].astype(F32) * blocks[1].astype(F32)).astype(BF16)

    dwd = _mm_tn(f"ffn_dwd_{tag}", [sv, uv], df, tk=FSH, tn=1024, tt=1024, a_fn=act, a_tiled=True,
                 into=(into["ffn_wd"], at))

    def act_bwd(accs, extras, vecs):
        da = accs[0]
        return [da * extras[1].astype(F32) * extras[2].astype(F32), da * extras[0].astype(F32)]

    dg, du = _mm(f"ffn_da_{tag}", [df], [(w["ffn_wd"], (TILE, *at))], [(0, 0)], act_bwd, [BF16, BF16], tm=1024,
                 tn=FSH, n_out=DFF, trans_b=True, extras=[(sv, (TILE,)), (tv, (TILE,)), (uv, (TILE,))],
                 out_tiled=True)
    dwg = _mm_tn(f"ffn_dwg_{tag}", h, dg, tk=1024, tn=FSH, tt=1024, b_tiled=True, into=(into["ffn_wg"], at))
    dwu = _mm_tn(f"ffn_dwu_{tag}", h, du, tk=1024, tn=FSH, tt=1024, b_tiled=True, into=(into["ffn_wu"], at))
    shards = range(SHARDS)
    dh, = _mm(f"ffn_dh_{tag}", [(dg, (k,)) for k in shards] + [(du, (k,)) for k in shards],
              [(w["ffn_wg"], (k, *at)) for k in shards] + [(w["ffn_wu"], (k, *at)) for k in shards],
              [(k, k) for k in range(2 * SHARDS)], _total, [F32], tm=512, tn=512, n_out=D, trans_b=True)
    dx_in, dgn, dscale, dshift = _normmod_bwd(f"ffn_norm_bwd_{tag}", x, dh, dx, g, scale)
    return dx_in, dict(ffn_wg=dwg, ffn_wu=dwu, ffn_wd=dwd), dgn, (dshift, dscale, dgate)


def _mix_fwd(tag, x, h, w_in, attn_wo, conv_w, conv_b, ln_g, ln_b, conv_wo, w_out, gate, cos_t, sin_t):
    w_qkv, w_u, w_g = w_in[:, :QKV], w_in[:, QKV:QKV + 2 * D], w_in[:, QKV + 2 * D:]
    zqkv, = _mm(f"mix_qkv_{tag}", [h], [w_qkv], [(0, 0)], _first, [F32], tm=512, tn=768, n_out=QKV)
    zu, = _mm(f"mix_u_{tag}", [h], [w_u], [(0, 0)], _first, [F32], tm=512, tn=1024, n_out=2 * D)
    zg, = _mm(f"mix_g_{tag}", [h], [w_g], [(0, 0)], _first, [BF16], tm=512, tn=1024, n_out=2 * D)
    qkv = _rope_fwd(f"rope_{tag}", zqkv, cos_t, sin_t)
    n = len(DILATIONS)
    outs, lses = [], []
    for grp in range(n):
        o, lse = _attn_fwd(f"attn_fwd_{tag}_{grp}", qkv[grp], qkv[n + grp], qkv[2 * n + grp], grp)
        outs.append(o)
        lses.append(lse)
    ob, of, lj = _attn_merge(f"attn_merge_{tag}", outs, lses)
    hc, s = _conv_fwd(f"conv_fwd_{tag}", zu, conv_w, conv_b, ln_g, ln_b)

    def gated(accs, extras, vecs):
        ya, yc = accs
        sa, sc = _sigmoid(extras[0].astype(F32)), _sigmoid(extras[1].astype(F32))
        return [sa * ya + sc * yc, ya, yc, sa, sc]

    y, ya, yc, sa, sc = _mm(f"mix_y_{tag}", [ob, s], [attn_wo, conv_wo], [(0, 0), (1, 1)], gated, [BF16] * 5,
                            tm=512, tn=1024, n_out=D, extras=[(zg, 0), (zg, 1)])

    def residual(accs, extras, vecs):
        return [extras[0] + vecs[0] * accs[0], accs[0]]

    x_new, f = _mm(f"mix_out_{tag}", [y], [w_out], [(0, 0)], residual, [F32, BF16], tm=512, tn=512, n_out=D,
                   extras=[(x, 0)], vecs=[gate])
    return x_new, (zu, sa, sc, qkv, ob, of, lj, hc, s, y, ya, yc, f, (w_qkv, w_u, w_g))


def _mix_bwd(tag, dx, x, h, saved, attn_wo, conv_w, ln_g, ln_b, conv_wo, w_out, g, scale, gate, cos_t, sin_t):
    zu, sa, sc, qkv, ob, of, lj, hc, s, y, ya, yc, f, w_parts = saved
    n = len(DILATIONS)
    df, dgate = _resgate_bwd(f"mix_gate_bwd_{tag}", dx, f, gate, 1.0)
    dw_out = _mm_tn(f"mix_dwout_{tag}", y, df, tk=1024, tn=1024, tt=512)

    def gated_bwd(accs, extras, vecs):
        dy = accs[0]
        ga, gc = extras[0].astype(F32), extras[1].astype(F32)
        dga = dy * extras[2].astype(F32) * (ga * (1.0 - ga))
        dgc = dy * extras[3].astype(F32) * (gc * (1.0 - gc))
        return [dy * ga, dy * gc, jnp.concatenate([dga, dgc], axis=1)]

    dya, dyc, dzg = _mm(f"mix_dy_{tag}", [df], [w_out], [(0, 0)], gated_bwd, [BF16] * 3, tm=512, tn=1024,
                        n_out=D, trans_b=True, extras=[(sa, 0), (sc, 0), (ya, 0), (yc, 0)],
                        out_widths=[D, D, 2 * D])
    dw_attn = _mm_tn(f"mix_dwattn_{tag}", ob, dya, tk=GW, tn=1024, tt=512)
    dw_conv_o = _mm_tn(f"mix_dwconvo_{tag}", s, dyc, tk=1024, tn=1024, tt=512)
    do, = _mm(f"mix_do_{tag}", [dya], [attn_wo], [(0, 0)], _first, [F32], tm=512, tn=GW, n_out=GW, trans_b=True)
    ds, = _mm(f"mix_ds_{tag}", [dyc], [conv_wo], [(0, 0)], _first, [F32], tm=512, tn=1024, n_out=D, trans_b=True)

    prep = _attn_bwd_prep(f"attn_prep_{tag}", do, of, lj)
    dqs, dks, dvs = [], [], []
    for grp in range(n):
        dq, dk, dv = _attn_bwd(f"attn_bwd_{tag}_{grp}", qkv[grp], qkv[n + grp], qkv[2 * n + grp], prep[n + grp],
                               prep[2 * n + grp], prep[grp], grp)
        dqs.append(dq)
        dks.append(dk)
        dvs.append(dv)
    dzqkv = _rope_bwd(f"rope_bwd_{tag}", dqs + dks + dvs, cos_t, sin_t)

    dhc, dln_g, dln_b, dconv_b = _conv_ln_bwd(f"conv_ln_bwd_{tag}", hc, ds, ln_g, ln_b)
    dzu, dconv_w = _conv_bwd(f"conv_bwd_{tag}", zu, dhc, conv_w)

    dz_parts = [dzqkv, dzu, dzg]
    dw_in = jnp.concatenate(
        [_mm_tn(f"mix_dwin_{tag}_{i}", h, dzp, tk=1024, tn=dzp.shape[1] // 2, tt=512)
         for i, dzp in enumerate(dz_parts)], axis=1)
    dh, = _mm(f"mix_dh_{tag}", dz_parts, list(w_parts), [(0, 0), (1, 1), (2, 2)], _total, [F32], tm=512, tn=512,
              n_out=D, trans_b=True)
    dx_in, dgn, dscale, dshift = _normmod_bwd(f"mix_norm_bwd_{tag}", x, dh, dx, g, scale)
    grads = dict(w_in=dw_in, attn_wo=dw_attn, conv_w=dconv_w[:CONV_K], conv_b=dconv_b, conv_ln_g=dln_g,
                 conv_ln_b=dln_b, conv_wo=dw_conv_o, w_out=dw_out)
    return dx_in, grads, dgn, (dshift, dscale, dgate)


def _local_step(x, c, target, w, wf):
    t = x.shape[0]
    cos_t, sin_t = _rope_tables(t)
    c8 = jnp.concatenate([c, jnp.zeros((7, D), F32)], axis=0)
    row = lambda v: v.reshape(1, -1)
    conv_w_pad = jnp.concatenate([wf["conv_w"], jnp.zeros((DEPTH, HALO - CONV_K, D), F32)], axis=1)

    saved = []
    for l in range(DEPTH):
        mod = _mod_fwd(f"mod_{l}", c8, w["ada_w"], l, row(wf["ada_b"][l]))[0:1]
        mods = [mod[:, i * D:(i + 1) * D] for i in range(N_MOD)]
        gains = [row(wf["norm_g"][l, i]) for i in range(3)]
        lay = dict(mods=mods, gains=gains)

        lay["x0"] = x
        lay["h0"] = _normmod_fwd(f"norm_a_{l}", x, gains[0], mods[1], mods[0])
        x, lay["ffn0"] = _ffn_fwd(f"a_{l}", x, lay["h0"], w, (l, 0), mods[2])
        lay["x1"] = x
        lay["h1"] = _normmod_fwd(f"norm_m_{l}", x, gains[1], mods[4], mods[3])
        x, lay["mix"] = _mix_fwd(f"{l}", x, lay["h1"], w["w_in"][l], w["attn_wo"][l], conv_w_pad[l],
                                 row(wf["conv_b"][l]), row(wf["conv_ln_g"][l]), row(wf["conv_ln_b"][l]),
                                 w["conv_wo"][l], w["w_out"][l], mods[5], cos_t, sin_t)
        lay["x2"] = x
        lay["h2"] = _normmod_fwd(f"norm_b_{l}", x, gains[2], mods[7], mods[6])
        x, lay["ffn1"] = _ffn_fwd(f"b_{l}", x, lay["h2"], w, (l, 1), mods[8])
        saved.append(lay)

    dx, dfinal_g, loss_cols = _loss_bwd("loss_head", x, target, row(wf["final_g"]))

    ffn_grads = {n: jnp.zeros(w[n].shape, F32) for n in ("ffn_wg", "ffn_wu", "ffn_wd")}
    per_layer = []
    for l in reversed(range(DEPTH)):
        lay = saved[l]
        mods, gains = lay["mods"], lay["gains"]
        dx, ffn_grads, dgn2, dmod2 = _ffn_bwd(f"b_{l}", dx, lay["x2"], lay["h2"], lay["ffn1"], w, (l, 1),
                                              gains[2], mods[7], mods[8], ffn_grads)
        dx, gm, dgn1, dmod1 = _mix_bwd(f"{l}", dx, lay["x1"], lay["h1"], lay["mix"], w["attn_wo"][l],
                                       conv_w_pad[l], row(wf["conv_ln_g"][l]), row(wf["conv_ln_b"][l]),
                                       w["conv_wo"][l], w["w_out"][l], gains[1], mods[4], mods[5], cos_t, sin_t)
        dx, ffn_grads, dgn0, dmod0 = _ffn_bwd(f"a_{l}", dx, lay["x0"], lay["h0"], lay["ffn0"], w, (l, 0),
                                              gains[0], mods[1], mods[2], ffn_grads)
        g = dict(gm)
        g["dmod"] = jnp.concatenate(list(dmod0) + list(dmod1) + list(dmod2), axis=1)
        g["ada_b"] = g["dmod"][0]
        g["norm_g"] = [dgn0[0], dgn1[0], dgn2[0]]
        for name in ("conv_b", "conv_ln_g", "conv_ln_b"):
            g[name] = g[name][0]
        per_layer.append(g)
    per_layer.reverse()
    grads = {name: [per_layer[l][name] for l in range(DEPTH)] for name in per_layer[0]}
    grads["ada_w"] = _mod_bwd("mod_bwd", c.reshape(D, 1), jnp.stack(grads.pop("dmod")))
    grads.update(ffn_grads)
    grads["final_g"] = dfinal_g[0]
    return loss_cols, dx, grads


def _split_bits(w):
    bits = lax.bitcast_convert_type(w, jnp.uint32)
    hi = lax.bitcast_convert_type((bits >> 16).astype(jnp.uint16), BF16)
    lo = lax.bitcast_convert_type((bits & 0xFFFF).astype(jnp.uint16), BF16)
    return hi, lo


def _join_bits(hi, lo):
    h = lax.bitcast_convert_type(hi, jnp.uint16).astype(jnp.uint32)
    l = lax.bitcast_convert_type(lo, jnp.uint16).astype(jnp.uint32)
    return lax.bitcast_convert_type((h << 16) | l, F32)


def _pack(parts, rows):
    out = []
    for p in parts:
        flat = p.reshape(-1)
        pad = -flat.shape[0] % LANES
        out.append(jnp.concatenate([flat, jnp.zeros((pad,), flat.dtype)]) if pad else flat)
    flat = jnp.concatenate(out)
    return jnp.concatenate([flat, jnp.zeros((rows * LANES - flat.shape[0],), flat.dtype)]).reshape(rows, LANES)


def _unpack(buf, shapes):
    out, row = [], 0
    for shape in shapes:
        size = 1
        for s in shape:
            size *= s
        rows = -(-size // LANES)
        out.append(buf[row:row + rows].reshape(-1)[:size].reshape(shape))
        row += rows
    return out


def _place():
    x, y, c = lax.axis_index("x"), lax.axis_index("y"), lax.axis_index("c")
    chips = [(1 - x, y), (x, 1 - y), (1 - x, 1 - y)]
    return x, y, c, chips


def _chip_index():
    return (2 * lax.axis_index("x") + lax.axis_index("y")).astype(jnp.int32)


HBM_SPEC = pl.BlockSpec(memory_space=pltpu.HBM)


def _gather_weights(arrays):
    n = len(arrays)

    def body(*refs):
        outs, send_sems, recv_sems = refs[n:2 * n], refs[2 * n], refs[2 * n + 1]
        x, y, c, chips = _place()
        me = 2 * x + y
        sibling = (x, y, 1 - c)
        there = [2 * chip[0] + chip[1] for chip in chips]

        def copy(a, k, chip, layer, to):
            piece = outs[a].at[chip, layer]
            return pltpu.make_async_remote_copy(
                src_ref=piece, dst_ref=piece, send_sem=send_sems.at[6 * a + k], recv_sem=recv_sems.at[6 * a + k],
                device_id=to, device_id_type=MESH)

        first = [copy(a, j, me, c, (*chip, c)) for a in range(n) for j, chip in enumerate(chips)]
        for cp in first:
            cp.start()
        passed = []
        for a in range(n):
            for j in range(3):
                copy(a, j, there[j], c, sibling).wait_recv()
                passed.append(copy(a, 3 + j, there[j], c, sibling))
                passed[-1].start()
        for a in range(n):
            for j in range(3):
                copy(a, 3 + j, there[j], 1 - c, sibling).wait_recv()
        for cp in first + passed:
            cp.wait_send()

    return pl.pallas_call(
        body, out_shape=[jax.ShapeDtypeStruct(a.shape, a.dtype) for a in arrays],
        in_specs=[HBM_SPEC] * n, out_specs=[HBM_SPEC] * n,
        scratch_shapes=[pltpu.SemaphoreType.DMA((6 * n,)), pltpu.SemaphoreType.DMA((6 * n,))],
        input_output_aliases={i: i for i in range(n)}, name="gather_weights",
    )(*arrays)


def _row_block(rows, cols):
    for cand in (512, 256, 128, 64, 32, 16):
        if rows % cand == 0 and cand * cols * 4 <= 2560 * 1024:
            return cand
    return rows


def _swap_layers(grads):
    n = len(grads)

    def body(*refs):
        g_refs, out_refs, send_sems, recv_sems = refs[:n], refs[n:2 * n], refs[2 * n], refs[2 * n + 1]
        x, y, c, _ = _place()
        copies = [pltpu.make_async_remote_copy(
            src_ref=g_refs[a].at[:, 1 - c], dst_ref=out_refs[a], send_sem=send_sems.at[a], recv_sem=recv_sems.at[a],
            device_id=(x, y, 1 - c), device_id_type=MESH) for a in range(n)]
        for cp in copies:
            cp.start()
        for cp in copies:
            cp.wait()

    return pl.pallas_call(
        body, out_shape=[jax.ShapeDtypeStruct((g.shape[0],) + g.shape[2:], F32) for g in grads],
        in_specs=[HBM_SPEC] * n, out_specs=[HBM_SPEC] * n,
        scratch_shapes=[pltpu.SemaphoreType.DMA((n,)), pltpu.SemaphoreType.DMA((n,))], name="swap_layers",
    )(*grads)


def _add_layers(name, grad, other):
    shards, _, rows, cols = grad.shape
    tr = _row_block(rows, cols)

    def body(c_ref, g_ref, o_ref, out_ref):
        out_ref[...] = (g_ref[...] + o_ref[...]).astype(BF16)

    c = lax.axis_index("c").astype(jnp.int32).reshape(1)
    grid_spec = pltpu.PrefetchScalarGridSpec(
        num_scalar_prefetch=1, grid=(shards, rows // tr),
        in_specs=[pl.BlockSpec((None, None, tr, cols), lambda k, i, c_ref: (k, c_ref[0], i, 0)),
                  pl.BlockSpec((None, tr, cols), lambda k, i, c_ref: (k, i, 0))],
        out_specs=pl.BlockSpec((None, tr, cols), lambda k, i, c_ref: (k, i, 0)))
    return pl.pallas_call(
        body, out_shape=jax.ShapeDtypeStruct((shards, rows, cols), BF16), grid_spec=grid_spec,
        name=name, compiler_params=_params(("parallel", "parallel")),
    )(c, grad, other)


def _scatter_chips(parts):
    n = len(parts)

    def body(*refs):
        p_refs, out_refs, send_sems, recv_sems = refs[:n], refs[n:2 * n], refs[2 * n], refs[2 * n + 1]
        x, y, c, chips = _place()
        me = 2 * x + y
        there = [2 * chip[0] + chip[1] for chip in chips]

        def copy(a, j, slot):
            return pltpu.make_async_remote_copy(
                src_ref=p_refs[a].at[there[j]], dst_ref=out_refs[a].at[slot], send_sem=send_sems.at[3 * a + j],
                recv_sem=recv_sems.at[3 * a + j], device_id=(*chips[j], c), device_id_type=MESH)

        sends = [copy(a, j, me) for a in range(n) for j in range(3)]
        for cp in sends:
            cp.start()
        for a in range(n):
            for j in range(3):
                copy(a, j, there[j]).wait_recv()
        for cp in sends:
            cp.wait_send()

    return pl.pallas_call(
        body, out_shape=[jax.ShapeDtypeStruct(p.shape, p.dtype) for p in parts],
        in_specs=[HBM_SPEC] * n, out_specs=[HBM_SPEC] * n,
        scratch_shapes=[pltpu.SemaphoreType.DMA((3 * n,)), pltpu.SemaphoreType.DMA((3 * n,))],
        name="scatter_chips",
    )(*parts)


def _add_chips(name, part, others):
    shards, rows, cols = part.shape
    tr = _row_block(rows, cols)

    def body(pos_ref, own_ref, r0_ref, r1_ref, r2_ref, r3_ref, out_ref):
        me = pos_ref[0]
        own = own_ref[...].astype(F32)
        total = None
        for k, r_ref in enumerate((r0_ref, r1_ref, r2_ref, r3_ref)):
            term = jnp.where(me == k, own, r_ref[...].astype(F32))
            total = term if total is None else total + term
        out_ref[...] = total

    def other(k):
        return pl.BlockSpec((None, tr, cols),
                            lambda i, pos, k=k: (jnp.where(pos[0] == k, (k + 1) % shards, k), i, 0))

    pos = jnp.stack([_chip_index(), lax.axis_index("c").astype(jnp.int32)])
    grid_spec = pltpu.PrefetchScalarGridSpec(
        num_scalar_prefetch=1, grid=(rows // tr,),
        in_specs=[pl.BlockSpec((None, tr, cols), lambda i, pos: (pos[0], i, 0))] + [other(k) for k in range(shards)],
        out_specs=pl.BlockSpec((None, tr, cols), lambda i, pos: (pos[1], i, 0)))
    return pl.pallas_call(
        body, out_shape=jax.ShapeDtypeStruct((DEPTH, rows, cols), F32), grid_spec=grid_spec,
        name=name, compiler_params=_params(("parallel",)),
    )(pos, part, others, others, others, others)


def _join_layers(arrays):
    n = len(arrays)

    def body(*refs):
        outs, send_sems, recv_sems = refs[n:2 * n], refs[2 * n], refs[2 * n + 1]
        x, y, c, _ = _place()

        def copy(a, layer):
            piece = outs[a].at[layer]
            return pltpu.make_async_remote_copy(src_ref=piece, dst_ref=piece, send_sem=send_sems.at[a],
                                                recv_sem=recv_sems.at[a], device_id=(x, y, 1 - c),
                                                device_id_type=MESH)

        sends = [copy(a, c) for a in range(n)]
        for cp in sends:
            cp.start()
        for a in range(n):
            copy(a, 1 - c).wait_recv()
        for cp in sends:
            cp.wait_send()

    return pl.pallas_call(
        body, out_shape=[jax.ShapeDtypeStruct(a.shape, a.dtype) for a in arrays],
        in_specs=[HBM_SPEC] * n, out_specs=[HBM_SPEC] * n,
        scratch_shapes=[pltpu.SemaphoreType.DMA((n,)), pltpu.SemaphoreType.DMA((n,))],
        input_output_aliases={i: i for i in range(n)}, name="join_layers",
    )(*arrays)


def _reduce_scatter(grads):
    sums = [_add_layers(f"add_layers_{a}", g, o) for a, (g, o) in enumerate(zip(grads, _swap_layers(grads)))]
    others = _scatter_chips(sums)
    return _join_layers([_add_chips(f"add_chips_{a}", p, o) for a, (p, o) in enumerate(zip(sums, others))])


def _adamw(name, w, g, m, v):
    shape = w.shape
    cols = shape[-1]
    rows = w.size // cols
    tr = rows
    for cand in (512, 256, 128, 64, 32, 16, 8):
        if rows % cand == 0 and cand * cols * 4 <= 4 * 1024 * 1024:
            tr = cand
            break

    def body(w_ref, g_ref, m_ref, v_ref, d_ref, nm_ref, nv_ref):
        gv = g_ref[...]
        nm = ADAM_B1 * m_ref[...] + (1.0 - ADAM_B1) * gv
        nv = ADAM_B2 * v_ref[...] + (1.0 - ADAM_B2) * (gv * gv)
        m_hat = nm / (1.0 - ADAM_B1 ** ADAM_STEP)
        v_hat = nv / (1.0 - ADAM_B2 ** ADAM_STEP)
        d_ref[...] = -ADAM_LR * (m_hat / (jnp.sqrt(v_hat) + ADAM_EPS) + ADAM_WD * w_ref[...])
        nm_ref[...] = nm
        nv_ref[...] = nv

    spec = pl.BlockSpec((tr, cols), lambda i: (i, 0))
    two = lambda a: a.reshape(rows, cols)
    outs = pl.pallas_call(
        body, out_shape=[jax.ShapeDtypeStruct((rows, cols), F32)] * 3, grid=(rows // tr,),
        in_specs=[spec] * 4, out_specs=[spec] * 3, name=name, compiler_params=_params(("parallel",)),
    )(two(w), two(g), two(m), two(v))
    return [o.reshape(shape) for o in outs]


BIG = ("ada_w", "ffn_wg", "ffn_wu", "ffn_wd", "w_in", "conv_wo", "w_out")
MISC_ROWS = 96


def _own_slot(shard):
    return lax.dynamic_update_slice(jnp.zeros((SHARDS,) + shard.shape, shard.dtype), shard[None],
                                    (_chip_index(),) + (0,) * shard.ndim)


def _as_matrices(a):
    return a.reshape(a.shape[0], a.shape[1], -1, a.shape[-1])


def kernel(x, c, ada_w, ada_b, norm_g, ffn_wg, ffn_wu, ffn_wd, w_in, attn_wo, conv_w, conv_b, conv_ln_g, conv_ln_b, conv_wo, w_out, final_g, loss_target, m_ada_w, m_ada_b, m_norm_g, m_ffn_wg, m_ffn_wu, m_ffn_wd, m_w_in, m_attn_wo, m_conv_w, m_conv_b, m_conv_ln_g, m_conv_ln_b, m_conv_wo, m_w_out, m_final_g, v_ada_w, v_ada_b, v_norm_g, v_ffn_wg, v_ffn_wu, v_ffn_wd, v_w_in, v_attn_wo, v_conv_w, v_conv_b, v_conv_ln_g, v_conv_ln_b, v_conv_wo, v_w_out, v_final_g):
    weights = dict(ada_w=ada_w, ada_b=ada_b, norm_g=norm_g, ffn_wg=ffn_wg, ffn_wu=ffn_wu, ffn_wd=ffn_wd, w_in=w_in,
                   attn_wo=attn_wo, conv_w=conv_w, conv_b=conv_b, conv_ln_g=conv_ln_g, conv_ln_b=conv_ln_b,
                   conv_wo=conv_wo, w_out=w_out, final_g=final_g)
    moments_m = dict(ada_w=m_ada_w, ada_b=m_ada_b, norm_g=m_norm_g, ffn_wg=m_ffn_wg, ffn_wu=m_ffn_wu,
                     ffn_wd=m_ffn_wd, w_in=m_w_in, attn_wo=m_attn_wo, conv_w=m_conv_w, conv_b=m_conv_b,
                     conv_ln_g=m_conv_ln_g, conv_ln_b=m_conv_ln_b, conv_wo=m_conv_wo, w_out=m_w_out,
                     final_g=m_final_g)
    moments_v = dict(ada_w=v_ada_w, ada_b=v_ada_b, norm_g=v_norm_g, ffn_wg=v_ffn_wg, ffn_wu=v_ffn_wu,
                     ffn_wd=v_ffn_wd, w_in=v_w_in, attn_wo=v_attn_wo, conv_w=v_conv_w, conv_b=v_conv_b,
                     conv_ln_g=v_conv_ln_g, conv_ln_b=v_conv_ln_b, conv_wo=v_conv_wo, w_out=v_w_out,
                     final_g=v_final_g)
    layers, shards = range(DEPTH), range(SHARDS)

    bits = {n: _split_bits(weights[n]) for n in EXACT}
    misc_w = jnp.stack([_pack([attn_wo[l].astype(BF16), bits["norm_g"][0][l], bits["norm_g"][1][l],
                               bits["conv_w"][0][l], bits["conv_w"][1][l]], MISC_ROWS) for l in layers])
    sent = [_own_slot(weights[n].astype(BF16)) for n in BIG] + [_own_slot(misc_w)]
    got = dict(zip(BIG + ("misc",), _gather_weights(sent)))
    w = {n: got[n] for n in ("ada_w", "ffn_wg", "ffn_wu", "ffn_wd")}
    w["w_in"] = got["w_in"].transpose(1, 2, 0, 3).reshape(DEPTH, D, -1)
    for n in ("conv_wo", "w_out"):
        w[n] = got[n].transpose(1, 0, 2, 3).reshape(DEPTH, D, D)
    misc_shapes = [(GW, GW), (3, GW), (3, GW), (CONV_K, GW), (CONV_K, GW)]
    pieces = [[_unpack(got["misc"][k, l], misc_shapes) for k in shards] for l in layers]
    whole = lambda i: jnp.stack([jnp.concatenate([pieces[l][k][i] for k in shards], axis=1) for l in layers])
    w["attn_wo"] = whole(0)
    vectors = dict(ada_b=ada_b, conv_b=conv_b, conv_ln_g=conv_ln_g, conv_ln_b=conv_ln_b, final_g=final_g,
                   norm_g=_join_bits(whole(1), whole(2)), conv_w=_join_bits(whole(3), whole(4)))

    loss_cols, dx, grads = _local_step(x[0], c, loss_target[0], w, vectors)
    loss = lax.psum(jnp.sum(loss_cols), ("x", "y", "c"))

    cols_of = lambda a, k: a[..., k * GW:(k + 1) * GW]
    misc_g = jnp.stack([jnp.stack([_pack(
        [cols_of(grads["attn_wo"][l], k), cols_of(jnp.stack(grads["norm_g"][l]), k), cols_of(grads["conv_w"][l], k),
         grads["ada_b"][l], grads["conv_b"][l], grads["conv_ln_g"][l], grads["conv_ln_b"][l],
         grads["final_g"] if l == 0 else jnp.zeros_like(grads["final_g"])], MISC_ROWS)
        for l in layers]) for k in shards])
    by_chip = dict(
        ada_w=grads["ada_w"], ffn_wg=grads["ffn_wg"], ffn_wu=grads["ffn_wu"], ffn_wd=grads["ffn_wd"],
        w_in=jnp.stack(grads["w_in"]).reshape(DEPTH, D, SHARDS, -1).transpose(2, 0, 1, 3),
        conv_wo=jnp.stack(grads["conv_wo"]).reshape(DEPTH, SHARDS, -1, D).transpose(1, 0, 2, 3),
        w_out=jnp.stack(grads["w_out"]).reshape(DEPTH, SHARDS, -1, D).transpose(1, 0, 2, 3))
    reduced = _reduce_scatter([_as_matrices(by_chip[n]) for n in BIG] + [misc_g])
    summed = {n: r.reshape(weights[n].shape) for n, r in zip(BIG, reduced)}
    small_shapes = [(GW, GW), (3, GW), (CONV_K, GW), (N_MOD * D,), (D,), (D,), (D,), (D,)]
    small = [_unpack(reduced[-1][l], small_shapes) for l in layers]
    for i, n in enumerate(("attn_wo", "norm_g", "conv_w", "ada_b", "conv_b", "conv_ln_g", "conv_ln_b")):
        summed[n] = jnp.stack([small[l][i] for l in layers])
    summed["final_g"] = small[0][7]

    deltas, new_m, new_v = {}, {}, {}
    for n in WEIGHTS:
        deltas[n], new_m[n], new_v[n] = _adamw(f"adamw_{n}", weights[n], summed[n], moments_m[n], moments_v[n])

    return (loss, dx[None], *[summed[n] for n in WEIGHTS], *[deltas[n] for n in WEIGHTS],
            *[new_m[n] for n in WEIGHTS], *[new_v[n] for n in WEIGHTS])
```

```python
import functools

import jax
import jax.numpy as jnp
from jax import lax
from jax.experimental import pallas as pl
from jax.experimental.pallas import tpu as pltpu

F32 = jnp.float32
BF16 = jnp.bfloat16

D = 1024
DFF = 2816
HEAD = 64
GW = 256
DILATIONS = (1, 4, 16)
BAND = 128
QKV = 2304
CONV_K = 31
HALO = 32
N_MOD = 9
EPS = 1e-6
NEG_INF = -1e30
DEPTH = 2

SHARDS = 4
FSH = DFF // SHARDS
LANES = 1024
VL = 128

ADAM_LR = 0.001
ADAM_B1 = 0.9
ADAM_B2 = 0.999
ADAM_EPS = 1e-08
ADAM_WD = 0.01
ADAM_STEP = 10

VMEM_LIMIT = 56 * 1024 * 1024

EXACT = ("norm_g", "conv_w")
WEIGHTS = ("ada_w", "ada_b", "norm_g", "ffn_wg", "ffn_wu", "ffn_wd", "w_in", "attn_wo", "conv_w", "conv_b",
           "conv_ln_g", "conv_ln_b", "conv_wo", "w_out", "final_g")

MESH = pl.DeviceIdType.MESH


def _params(sem=None):
    return pltpu.CompilerParams(dimension_semantics=sem, vmem_limit_bytes=VMEM_LIMIT)


def _sigmoid(v):
    return jax.nn.sigmoid(v)


TILE = "tile"
MM_SLAB = 256


def _lead_spec(arr, lead, block, index):
    def index_map(j, i):
        return (*[j if e == TILE else e for e in lead], *index(j, i))
    return pl.BlockSpec((None,) * len(lead) + tuple(block), index_map)


def _entry(e):
    return e if isinstance(e, tuple) else (e, ())


def _mm(name, a_list, b_list, pairs, epilogue, out_dtypes, *, tm, tn, n_out, trans_b=False, extras=(), vecs=(),
        out_widths=None, out_tiled=False):
    a_list = [_entry(a) for a in a_list]
    b_list = [_entry(b) for b in b_list]
    m = a_list[0][0].shape[-2]
    na, nb, ne, nv = len(a_list), len(b_list), len(extras), len(vecs)
    dn = (((1,), (1,)), ((), ())) if trans_b else (((1,), (0,)), ((), ()))

    def body(*refs):
        a_refs = refs[:na]
        b_refs = refs[na:na + nb]
        e_refs = refs[na + nb:na + nb + ne]
        v_refs = refs[na + nb + ne:na + nb + ne + nv]
        o_refs = refs[na + nb + ne + nv:]
        vec_blocks = [v[...] for v in v_refs]
        for r0 in range(0, tm, min(tm, MM_SLAB)):
            rows = slice(r0, r0 + min(tm, MM_SLAB))
            accs = [lax.dot_general(a_refs[ai][rows, :], b_refs[bi][...], dn, preferred_element_type=F32)
                    for ai, bi in pairs]
            res = epilogue(accs, [e[rows, :] for e in e_refs], vec_blocks)
            for o_ref, r in zip(o_refs, res):
                o_ref[rows, :] = r.astype(o_ref.dtype)

    in_specs = [_lead_spec(a, lead, (tm, a.shape[-1]), lambda j, i: (i, 0)) for a, lead in a_list]
    for b, lead in b_list:
        if TILE in lead:
            in_specs.append(_lead_spec(b, lead, b.shape[-2:], lambda j, i: (0, 0)))
        elif trans_b:
            in_specs.append(_lead_spec(b, lead, (tn, b.shape[-1]), lambda j, i: (j, 0)))
        else:
            in_specs.append(_lead_spec(b, lead, (b.shape[-2], tn), lambda j, i: (0, j)))
    for e, where in extras:
        if isinstance(where, tuple):
            in_specs.append(_lead_spec(e, where, (tm, tn), lambda j, i: (i, 0)))
        else:
            in_specs.append(pl.BlockSpec((tm, tn), functools.partial(lambda j, i, off: (i, j + off), off=where)))
    in_specs += [pl.BlockSpec((1, tn), lambda j, i: (0, j)) for _ in vecs]
    nj = n_out // tn
    widths = out_widths or [tn] * len(out_dtypes)
    if out_tiled:
        out_specs = [pl.BlockSpec((None, tm, tn), lambda j, i: (j, i, 0)) for _ in out_dtypes]
        out_shape = [jax.ShapeDtypeStruct((nj, m, tn), dt) for dt in out_dtypes]
    else:
        out_specs = [pl.BlockSpec((tm, wd), lambda j, i: (i, j)) for wd in widths]
        out_shape = [jax.ShapeDtypeStruct((m, nj * wd), dt) for dt, wd in zip(out_dtypes, widths)]
    return pl.pallas_call(
        body, out_shape=out_shape, grid=(nj, m // tm), in_specs=in_specs, out_specs=out_specs,
        name=name, compiler_params=_params(("parallel", "parallel")),
    )(*[a for a, _ in a_list], *[b for b, _ in b_list], *[e for e, _ in extras], *vecs)


def _mm_tn(name, a, b, *, tk, tn, tt, a_fn=None, a_tiled=False, b_tiled=False, into=None):
    a_list = list(a) if a_fn is not None else [a]
    na = len(a_list)
    t = a_list[0].shape[-2]
    nk = a_list[0].shape[0] if a_tiled else a_list[0].shape[1] // tk
    nn = b.shape[0] if b_tiled else b.shape[1] // tn
    steps = t // tt
    has_into = into is not None

    def body(*refs):
        refs = refs[1:] if has_into else refs
        a_refs, b_ref, o_ref, acc_ref = refs[:na], refs[na], refs[na + 1], refs[na + 2]
        s = pl.program_id(2)

        @pl.when(s == 0)
        def _():
            acc_ref[...] = jnp.zeros_like(acc_ref)

        av = a_refs[0][...] if a_fn is None else a_fn([r[...] for r in a_refs])
        acc_ref[...] += lax.dot_general(av, b_ref[...], (((0,), (0,)), ((), ())), preferred_element_type=F32)

        @pl.when(s == steps - 1)
        def _():
            o_ref[...] = acc_ref[...]

    a_spec = (pl.BlockSpec((None, tt, tk), lambda i, j, s: (i, s, 0)) if a_tiled
              else pl.BlockSpec((tt, tk), lambda i, j, s: (s, i)))
    b_spec = (pl.BlockSpec((None, tt, tn), lambda i, j, s: (j, s, 0)) if b_tiled
              else pl.BlockSpec((tt, tn), lambda i, j, s: (s, j)))
    if a_tiled:
        out_dims, tile_index = (nk, tk, nn * tn), lambda i, j, s: (i, 0, j)
    elif b_tiled:
        out_dims, tile_index = (nn, nk * tk, tn), lambda i, j, s: (j, i, 0)
    else:
        out_dims, tile_index = (nk * tk, nn * tn), lambda i, j, s: (i, j)
    tiled = a_tiled or b_tiled
    if has_into:
        buf, lead = into
        def out_index(i, j, s):
            idx = tile_index(i, j, s)
            return (idx[0], *lead, *idx[1:])
        out_spec = pl.BlockSpec((None,) * (1 + len(lead)) + (tk, tn), out_index)
        out_shape = jax.ShapeDtypeStruct(buf.shape, buf.dtype)
        extra_in, extra_specs, aliases = [buf], [pl.BlockSpec(memory_space=pl.ANY)], {0: 0}
    else:
        out_spec = pl.BlockSpec(((None,) if tiled else ()) + (tk, tn), tile_index)
        out_shape = jax.ShapeDtypeStruct(out_dims, F32)
        extra_in, extra_specs, aliases = [], [], {}
    return pl.pallas_call(
        body, out_shape=out_shape, grid=(nk, nn, steps), in_specs=extra_specs + [a_spec] * na + [b_spec],
        out_specs=out_spec, scratch_shapes=[pltpu.VMEM((tk, tn), F32)], input_output_aliases=aliases, name=name,
        compiler_params=_params(("parallel", "parallel", "arbitrary")),
    )(*extra_in, *a_list, b)


def _first(accs, extras, vecs):
    return [accs[0]]


def _total(accs, extras, vecs):
    out = accs[0]
    for r in accs[1:]:
        out = out + r
    return [out]


def _row_spec(tm, width, col=0):
    return pl.BlockSpec((tm, width), functools.partial(lambda i, col: (i, col), col=col))


def _vec_spec(width):
    return pl.BlockSpec((1, width), lambda i: (0, 0))


def _normmod_fwd(name, x, g, scale, shift, tm=512):
    t = x.shape[0]

    def body(x_ref, g_ref, sc_ref, sh_ref, h_ref):
        xv = x_ref[...]
        r = lax.rsqrt(jnp.mean(xv * xv, axis=-1, keepdims=True) + EPS)
        h_ref[...] = ((xv * r) * g_ref[...] * (1.0 + sc_ref[...]) + sh_ref[...]).astype(BF16)

    return pl.pallas_call(
        body, out_shape=jax.ShapeDtypeStruct((t, D), BF16), grid=(t // tm,),
        in_specs=[_row_spec(tm, D), _vec_spec(D), _vec_spec(D), _vec_spec(D)], out_specs=_row_spec(tm, D),
        name=name, compiler_params=_params(("parallel",)),
    )(x, g, scale, shift)


def _normmod_bwd(name, x, dh, dres, g, scale, tm=256):
    t = x.shape[0]
    steps = t // tm

    def body(x_ref, dh_ref, dres_ref, g_ref, sc_ref, dx_ref, dg_ref, dsc_ref, dsh_ref):
        i = pl.program_id(0)

        @pl.when(i == 0)
        def _():
            dg_ref[...] = jnp.zeros_like(dg_ref)
            dsh_ref[...] = jnp.zeros_like(dsh_ref)

        xv = x_ref[...]
        dh = dh_ref[...]
        r = lax.rsqrt(jnp.mean(xv * xv, axis=-1, keepdims=True) + EPS)
        xh = xv * r
        dxh = dh * (g_ref[...] * (1.0 + sc_ref[...]))
        dx_ref[...] = dres_ref[...] + r * (dxh - xh * jnp.mean(dxh * xh, axis=-1, keepdims=True))
        dg_ref[...] += jnp.sum(dh * xh, axis=0, keepdims=True)
        dsh_ref[...] += jnp.sum(dh, axis=0, keepdims=True)

        @pl.when(i == steps - 1)
        def _():
            acc = dg_ref[...]
            dg_ref[...] = acc * (1.0 + sc_ref[...])
            dsc_ref[...] = acc * g_ref[...]

    vec = jax.ShapeDtypeStruct((1, D), F32)
    return pl.pallas_call(
        body, out_shape=[jax.ShapeDtypeStruct((t, D), F32), vec, vec, vec], grid=(steps,),
        in_specs=[_row_spec(tm, D), _row_spec(tm, D), _row_spec(tm, D), _vec_spec(D), _vec_spec(D)],
        out_specs=[_row_spec(tm, D), _vec_spec(D), _vec_spec(D), _vec_spec(D)],
        name=name, compiler_params=_params(("arbitrary",)),
    )(x, dh, dres, g, scale)


def _resgate_bwd(name, dx, f, gate, coef, tm=512):
    t = dx.shape[0]

    def body(dx_ref, f_ref, gate_ref, df_ref, dgate_ref):
        @pl.when(pl.program_id(0) == 0)
        def _():
            dgate_ref[...] = jnp.zeros_like(dgate_ref)

        dxv = dx_ref[...]
        df_ref[...] = ((coef * gate_ref[...]) * dxv).astype(BF16)
        dgate_ref[...] += jnp.sum((coef * f_ref[...].astype(F32)) * dxv, axis=0, keepdims=True)

    return pl.pallas_call(
        body, out_shape=[jax.ShapeDtypeStruct((t, D), BF16), jax.ShapeDtypeStruct((1, D), F32)], grid=(t // tm,),
        in_specs=[_row_spec(tm, D), _row_spec(tm, D), _vec_spec(D)], out_specs=[_row_spec(tm, D), _vec_spec(D)],
        name=name, compiler_params=_params(("arbitrary",)),
    )(dx, f, gate)


def _loss_bwd(name, x, target, g, tm=256):
    t = x.shape[0]

    def body(x_ref, t_ref, g_ref, dx_ref, dg_ref, loss_ref):
        @pl.when(pl.program_id(0) == 0)
        def _():
            dg_ref[...] = jnp.zeros_like(dg_ref)
            loss_ref[...] = jnp.zeros_like(loss_ref)

        xv = x_ref[...]
        r = lax.rsqrt(jnp.mean(xv * xv, axis=-1, keepdims=True) + EPS)
        xh = xv * r
        err = xh * g_ref[...] - t_ref[...]
        dy = err * (1.0 / D)
        dxh = dy * g_ref[...]
        dx_ref[...] = r * (dxh - xh * jnp.mean(dxh * xh, axis=-1, keepdims=True))
        dg_ref[...] += jnp.sum(dy * xh, axis=0, keepdims=True)
        loss_ref[...] += jnp.sum(err * err, axis=0, keepdims=True) * (0.5 / D)

    vec = jax.ShapeDtypeStruct((1, D), F32)
    return pl.pallas_call(
        body, out_shape=[jax.ShapeDtypeStruct((t, D), F32), vec, vec], grid=(t // tm,),
        in_specs=[_row_spec(tm, D), _row_spec(tm, D), _vec_spec(D)],
        out_specs=[_row_spec(tm, D), _vec_spec(D), _vec_spec(D)],
        name=name, compiler_params=_params(("arbitrary",)),
    )(x, target, g)


N_DEV = 8


def _mod_fwd(name, c_all, ada_w):
    cols = ada_w.shape[-1]

    def body(c_ref, w_ref, o_ref):
        cv = c_ref[...]
        ca = (cv * _sigmoid(cv)).astype(BF16)
        o_ref[...] = jnp.dot(ca, w_ref[...], preferred_element_type=F32)

    return pl.pallas_call(
        body, out_shape=jax.ShapeDtypeStruct((N_DEV, DEPTH * cols), F32), grid=(DEPTH,),
        in_specs=[pl.BlockSpec((N_DEV, D), lambda l: (0, 0)), pl.BlockSpec((None, D, cols), lambda l: (l, 0, 0))],
        out_specs=pl.BlockSpec((N_DEV, cols), lambda l: (0, l)), name=name, compiler_params=_params(("parallel",)),
    )(c_all, ada_w)


def _mod_bwd(name, c_cols, dmods, tk=256):
    cols = dmods.shape[-1]

    def body(c_ref, d_ref, o_ref):
        cv = c_ref[...]
        ca = cv * _sigmoid(cv)
        total = ca[:, 0:1] * d_ref[0:1, :]
        for b in range(1, N_DEV):
            total = total + ca[:, b:b + 1] * d_ref[b:b + 1, :]
        o_ref[...] = total

    return pl.pallas_call(
        body, out_shape=jax.ShapeDtypeStruct((DEPTH, D, cols), F32), grid=(DEPTH, D // tk),
        in_specs=[pl.BlockSpec((tk, N_DEV), lambda l, i: (i, 0)), pl.BlockSpec((None, N_DEV, cols), lambda l, i: (l, 0, 0))],
        out_specs=pl.BlockSpec((None, tk, cols), lambda l, i: (l, i, 0)), name=name,
        compiler_params=_params(("parallel", "parallel")),
    )(c_cols, dmods)


def _sum_devices(name, rows):
    n = rows.shape[1]

    def body(r_ref, o_ref):
        total = r_ref[0:1, :]
        for b in range(1, N_DEV):
            total = total + r_ref[b:b + 1, :]
        o_ref[...] = total

    whole = pl.BlockSpec(memory_space=pltpu.VMEM)
    return pl.pallas_call(body, out_shape=jax.ShapeDtypeStruct((1, n), F32), in_specs=[whole], out_specs=whole,
                          name=name)(rows)


def _rope_tables(t):
    half = HEAD // 2
    inv_freq = 10000.0 ** (-(jnp.arange(half, dtype=F32) * 2.0 / HEAD))
    ang = jnp.arange(t, dtype=F32)[:, None] * inv_freq[None, :]
    cos, sin = jnp.cos(ang), jnp.sin(ang)
    cos_t = jnp.tile(jnp.concatenate([cos, cos], axis=1), (1, VL // HEAD))
    sin_t = jnp.tile(jnp.concatenate([-sin, sin], axis=1), (1, VL // HEAD))
    return cos_t, sin_t


def _rotate(tv, cos, sin_signed):
    lane = lax.broadcasted_iota(jnp.int32, tv.shape, 1)
    first = (lane % HEAD) < (HEAD // 2)
    partner = jnp.where(first, pltpu.roll(tv, tv.shape[1] - HEAD // 2, 1), pltpu.roll(tv, HEAD // 2, 1))
    return tv * cos + partner * sin_signed


def _dilated_spec(tm, d):
    return pl.BlockSpec((tm // d, d * GW), lambda i: (i, 0))


def _dilated_shape(t, d, dtype):
    return jax.ShapeDtypeStruct((t // d, d * GW), dtype)


def _every(d, r, tm):
    return pl.ds(r, tm // d, stride=d) if d > 1 else slice(None)


def _rope_fwd(name, zqkv, cos_t, sin_t, tm=512):
    t = zqkv.shape[0]
    ng = len(DILATIONS)
    n = 3 * ng

    halves = GW // VL

    def body(*refs):
        z_refs, cos_ref, sin_ref, o_refs = refs[:halves * n], refs[halves * n], refs[halves * n + 1], refs[halves * n + 2:]
        for idx in range(n):
            d = DILATIONS[idx % ng]
            for r in range(d):
                rows = _every(d, r, tm)
                for hh in range(halves):
                    piece = z_refs[halves * idx + hh][rows, :]
                    if idx < 2 * ng:
                        piece = _rotate(piece, cos_ref[rows, :], sin_ref[rows, :])
                    if idx < ng:
                        piece = piece * (HEAD ** -0.5)
                    o_refs[idx][:, r * GW + hh * VL:r * GW + (hh + 1) * VL] = piece.astype(BF16)

    dils = [DILATIONS[idx % ng] for idx in range(n)]
    return pl.pallas_call(
        body, out_shape=[_dilated_shape(t, d, BF16) for d in dils], grid=(t // tm,),
        in_specs=[_row_spec(tm, VL, col) for col in range(halves * n)] + [_row_spec(tm, VL), _row_spec(tm, VL)],
        out_specs=[_dilated_spec(tm, d) for d in dils], name=name, compiler_params=_params(("parallel",)),
    )(*([zqkv] * (halves * n)), cos_t, sin_t)


def _rope_bwd(name, grads, cos_t, sin_t, tm=512):
    ng = len(DILATIONS)
    n = len(grads)
    t = grads[0].shape[0] * DILATIONS[0]

    halves = GW // VL

    def body(*refs):
        g_refs, cos_ref, sin_ref, o_ref, rows_ref = refs[:n], refs[n], refs[n + 1], refs[n + 2], refs[n + 3]
        cos, sin = cos_ref[...], -sin_ref[...]
        for idx in range(n):
            d = DILATIONS[idx % ng]
            for hh in range(halves):
                for r in range(d):
                    cols = slice(r * GW + hh * VL, r * GW + (hh + 1) * VL)
                    rows_ref[hh, _every(d, r, tm), :] = g_refs[idx][:, cols].astype(F32)
                piece = rows_ref[hh]
                if idx < 2 * ng:
                    piece = _rotate(piece, cos, sin)
                if idx < ng:
                    piece = piece * (HEAD ** -0.5)
                o_ref[:, idx * GW + hh * VL:idx * GW + (hh + 1) * VL] = piece.astype(BF16)

    dils = [DILATIONS[idx % ng] for idx in range(n)]
    return pl.pallas_call(
        body, out_shape=jax.ShapeDtypeStruct((t, n * GW), BF16), grid=(t // tm,),
        in_specs=[_dilated_spec(tm, d) for d in dils] + [_row_spec(tm, VL)] * 2, out_specs=_row_spec(tm, n * GW),
        scratch_shapes=[pltpu.VMEM((halves, tm, VL), F32)], name=name, compiler_params=_params(("parallel",)),
    )(*grads, cos_t, sin_t)


def _head_cols(h):
    return slice(h * HEAD, (h + 1) * HEAD)


def _band_mask_q(has_prev):
    qi = lax.broadcasted_iota(jnp.int32, (BAND, 2 * BAND), 0)
    kj = lax.broadcasted_iota(jnp.int32, (BAND, 2 * BAND), 1)
    dist = qi + BAND - kj
    return (dist >= 0) & (dist <= BAND) & ((kj >= BAND) | has_prev)


def _attn_fwd(name, q, k, v, group):
    d = DILATIONS[group]
    length = q.shape[0]
    qb = min(512, length)
    sub = qb // BAND
    nblk = length // qb

    def body(q_ref, kc_ref, kp_ref, vc_ref, vp_ref, o_ref, lse_ref):
        blk = pl.program_id(1)
        k_ext = jnp.concatenate([kp_ref[...], kc_ref[...]], axis=0)
        v_ext = jnp.concatenate([vp_ref[...], vc_ref[...]], axis=0)
        for j in range(sub):
            mask = _band_mask_q((blk * sub + j) > 0)
            qj = q_ref[j * BAND:(j + 1) * BAND, :]
            kj = k_ext[j * BAND:(j + 2) * BAND, :]
            vj = v_ext[j * BAND:(j + 2) * BAND, :]
            outs, lses = [], []
            for h in range(GW // HEAD):
                s = lax.dot_general(qj[:, _head_cols(h)], kj[:, _head_cols(h)], (((1,), (1,)), ((), ())),
                                    preferred_element_type=F32)
                s = jnp.where(mask, s, NEG_INF)
                m = jnp.max(s, axis=-1, keepdims=True)
                p = jnp.exp(s - m)
                den = jnp.sum(p, axis=-1, keepdims=True)
                o = jnp.dot(p.astype(BF16), vj[:, _head_cols(h)], preferred_element_type=F32)
                outs.append(o / den)
                lses.append(jnp.broadcast_to(m + jnp.log(den), (BAND, HEAD)))
            o_ref[j * BAND:(j + 1) * BAND, :] = jnp.concatenate(outs, axis=1)
            lse_ref[j * BAND:(j + 1) * BAND, :] = jnp.concatenate(lses, axis=1)

    prev = qb // BAND
    cur = lambda r, b: (b, r)
    before = lambda r, b: (jnp.maximum(b * prev - 1, 0), r)
    big, halo = pl.BlockSpec((qb, GW), cur), pl.BlockSpec((BAND, GW), before)
    return pl.pallas_call(
        body, out_shape=[jax.ShapeDtypeStruct((length, d * GW), F32)] * 2, grid=(d, nblk),
        in_specs=[big, big, halo, big, halo], out_specs=[big] * 2, name=name,
        compiler_params=_params(("parallel", "parallel")),
    )(q, k, k, v, v)


def _attn_merge(name, outs, lses, tm=512):
    n = len(outs)
    t = outs[0].shape[0] * DILATIONS[0]

    halves = GW // VL

    def body(*refs):
        in_refs = refs[:2 * n]
        ob_ref, of_ref, lj_ref, rows_ref = refs[2 * n:]
        for hh in range(halves):
            for idx in range(2 * n):
                d = DILATIONS[idx % n]
                for r in range(d):
                    cols = slice(r * GW + hh * VL, r * GW + (hh + 1) * VL)
                    rows_ref[idx, _every(d, r, tm), :] = in_refs[idx][:, cols]
            ls = [rows_ref[n + g] for g in range(n)]
            m = ls[0]
            for v in ls[1:]:
                m = jnp.maximum(m, v)
            es = [jnp.exp(v - m) for v in ls]
            tot = es[0]
            for v in es[1:]:
                tot = tot + v
            acc = (es[0] / tot) * rows_ref[0]
            for g in range(1, n):
                acc = acc + (es[g] / tot) * rows_ref[g]
            half = slice(hh * VL, (hh + 1) * VL)
            ob_ref[:, half] = acc.astype(BF16)
            of_ref[:, half] = acc
            lj_ref[:, half] = m + jnp.log(tot)

    return pl.pallas_call(
        body, out_shape=[jax.ShapeDtypeStruct((t, GW), BF16), jax.ShapeDtypeStruct((t, GW), F32),
                         jax.ShapeDtypeStruct((t, GW), F32)], grid=(t // tm,),
        in_specs=[_dilated_spec(tm, DILATIONS[idx % n]) for idx in range(2 * n)], out_specs=[_row_spec(tm, GW)] * 3,
        scratch_shapes=[pltpu.VMEM((2 * n, tm, VL), F32)], name=name, compiler_params=_params(("parallel",)),
    )(*outs, *lses)


def _attn_bwd_prep(name, do, o, lj, tm=512):
    t = do.shape[0]
    n = len(DILATIONS)

    halves = GW // VL
    per_half = VL // HEAD

    def body(*refs):
        do_refs, o_refs, lj_refs = refs[:halves], refs[halves:2 * halves], refs[2 * halves:3 * halves]
        outs, dsum_ref = refs[3 * halves:3 * halves + 3 * n], refs[3 * halves + 3 * n]
        for hh in range(halves):
            prod = do_refs[hh][...] * o_refs[hh][...]
            parts = [jnp.broadcast_to(jnp.sum(prod[:, _head_cols(h)], axis=-1, keepdims=True), (tm, HEAD))
                     for h in range(per_half)]
            dsum_ref[...] = jnp.concatenate(parts, axis=1)
            for g, d in enumerate(DILATIONS):
                for r in range(d):
                    rows, cols = _every(d, r, tm), slice(r * GW + hh * VL, r * GW + (hh + 1) * VL)
                    outs[g][:, cols] = dsum_ref[rows, :]
                    outs[n + g][:, cols] = do_refs[hh][rows, :].astype(BF16)
                    outs[2 * n + g][:, cols] = lj_refs[hh][rows, :]

    shapes = [_dilated_shape(t, d, dt) for dt in (F32, BF16, F32) for d in DILATIONS]
    half_specs = [_row_spec(tm, VL, hh) for hh in range(halves)]
    return pl.pallas_call(
        body, out_shape=shapes, grid=(t // tm,), in_specs=half_specs * 3,
        out_specs=[_dilated_spec(tm, d) for d in DILATIONS] * 3, scratch_shapes=[pltpu.VMEM((tm, VL), F32)],
        name=name, compiler_params=_params(("parallel",)),
    )(*([do] * halves), *([o] * halves), *([lj] * halves))


def _attn_bwd(name, q, k, v, do, lj, dsum, group):
    d = DILATIONS[group]
    length = q.shape[0]
    qb = min(512, length)
    sub = qb // BAND
    nblk = length // qb
    total = length // BAND

    def body(qc_ref, qn_ref, kc_ref, kp_ref, vc_ref, vp_ref, doc_ref, don_ref, ljc_ref, ljn_ref, dsc_ref, dsn_ref,
             dq_ref, dk_ref, dv_ref):
        blk = pl.program_id(1)
        k_ext = jnp.concatenate([kp_ref[...], kc_ref[...]], axis=0)
        v_ext = jnp.concatenate([vp_ref[...], vc_ref[...]], axis=0)
        heads = range(GW // HEAD)
        nt = (((1,), (1,)), ((), ()))
        tn = (((0,), (0,)), ((), ()))

        def scores(qh, doh, ljh, dsh, kh, vh, mask):
            s = lax.dot_general(qh, kh, nt, preferred_element_type=F32)
            p = jnp.where(mask, jnp.exp(s - ljh), 0.0)
            dp = lax.dot_general(doh, vh, nt, preferred_element_type=F32)
            return p.astype(BF16), (p * (dp - dsh)).astype(BF16)

        held_k, held_v = [None] * len(heads), [None] * len(heads)
        for j in range(sub):
            rows = slice(j * BAND, (j + 1) * BAND)
            rows2 = slice(j * BAND, (j + 2) * BAND)
            mask = _band_mask_q((blk * sub + j) > 0)
            dqs, dks, dvs = [], [], []
            for h in heads:
                hc = _head_cols(h)
                col = slice(h * HEAD, h * HEAD + 1)
                qh, doh, kh2 = qc_ref[rows, hc], doc_ref[rows, hc], k_ext[rows2, hc]
                p, ds = scores(qh, doh, ljc_ref[rows, col], dsc_ref[rows, col], kh2, v_ext[rows2, hc], mask)
                dqs.append(jnp.dot(ds, kh2, preferred_element_type=F32))
                dk2 = lax.dot_general(ds, qh, tn, preferred_element_type=F32)
                dv2 = lax.dot_general(p, doh, tn, preferred_element_type=F32)
                if j > 0:
                    dks.append(held_k[h] + dk2[:BAND])
                    dvs.append(held_v[h] + dv2[:BAND])
                held_k[h], held_v[h] = dk2[BAND:], dv2[BAND:]
            dq_ref[rows, :] = jnp.concatenate(dqs, axis=1)
            if j > 0:
                done = slice((j - 1) * BAND, j * BAND)
                dk_ref[done, :] = jnp.concatenate(dks, axis=1)
                dv_ref[done, :] = jnp.concatenate(dvs, axis=1).astype(BF16)

        last = slice((sub - 1) * BAND, sub * BAND)
        qi = lax.broadcasted_iota(jnp.int32, (BAND, BAND), 0)
        kj = lax.broadcasted_iota(jnp.int32, (BAND, BAND), 1)
        mask = (kj >= qi) & ((blk + 1) * sub < total)
        dks, dvs = [], []
        for h in heads:
            hc = _head_cols(h)
            col = slice(h * HEAD, h * HEAD + 1)
            qh, doh = qn_ref[:, hc], don_ref[:, hc]
            p, ds = scores(qh, doh, ljn_ref[:, col], dsn_ref[:, col], kc_ref[last, hc], vc_ref[last, hc], mask)
            dks.append(held_k[h] + lax.dot_general(ds, qh, tn, preferred_element_type=F32))
            dvs.append(held_v[h] + lax.dot_general(p, doh, tn, preferred_element_type=F32))
        dk_ref[last, :] = jnp.concatenate(dks, axis=1)
        dv_ref[last, :] = jnp.concatenate(dvs, axis=1).astype(BF16)

    prev = qb // BAND
    cur = lambda r, b: (b, r)
    before = lambda r, b: (jnp.maximum(b * prev - 1, 0), r)
    after = lambda r, b: (jnp.minimum((b + 1) * prev, total - 1), r)
    big = pl.BlockSpec((qb, GW), cur)
    nxt = pl.BlockSpec((BAND, GW), after)
    prv = pl.BlockSpec((BAND, GW), before)
    return pl.pallas_call(
        body, out_shape=[jax.ShapeDtypeStruct((length, d * GW), F32), jax.ShapeDtypeStruct((length, d * GW), F32),
                         jax.ShapeDtypeStruct((length, d * GW), BF16)], grid=(d, nblk),
        in_specs=[big, nxt, big, prv, big, prv, big, nxt, big, nxt, big, nxt],
        out_specs=[big] * 3, name=name, compiler_params=_params(("parallel", "parallel")),
    )(q, q, k, k, v, v, do, do, lj, lj, dsum, dsum)


SUBLANES = 8
CONV_CHUNK = 32
SHIFT_ROWS = HALO - SUBLANES


def _shifted_copies(buf_ref, sh_ref, tm):
    for s in range(1, SUBLANES):
        sh_ref[s - 1] = buf_ref[s:s + tm + SHIFT_ROWS, :]


def _window(buf_ref, sh_ref, offset, r0, rows):
    tiles, shift = divmod(offset, SUBLANES)
    src = buf_ref if shift == 0 else sh_ref.at[shift - 1]
    return src[pl.ds(pl.multiple_of(r0 + tiles * SUBLANES, SUBLANES), rows), :]


def _conv_fwd(name, zu, conv_w, conv_b, ln_g, ln_b, tm=256):
    t = zu.shape[0]
    per = tm // HALO

    def body(a_ref, gl_ref, ah_ref, glh_ref, w_ref, b_ref, g_ref, beta_ref, hc_ref, s_ref, ext_ref, sh_ref):
        i = pl.program_id(0)
        halo = ah_ref[...] * _sigmoid(glh_ref[...])
        ext_ref[0:HALO, :] = jnp.where(i > 0, halo, 0.0)
        ext_ref[HALO:, :] = a_ref[...] * _sigmoid(gl_ref[...])
        _shifted_copies(ext_ref, sh_ref, tm)

        def chunk(r, carry):
            r0 = pl.multiple_of(r * CONV_CHUNK, CONV_CHUNK)
            part = jnp.broadcast_to(b_ref[...], (CONV_CHUNK, D))
            for kk in range(CONV_K):
                part = part + w_ref[kk:kk + 1, :] * _window(ext_ref, sh_ref, HALO - CONV_K + 1 + kk, r0, CONV_CHUNK)
            hc_ref[pl.ds(r0, CONV_CHUNK), :] = part
            return carry

        lax.fori_loop(0, tm // CONV_CHUNK, chunk, 0)
        acc = hc_ref[...]
        mu = jnp.mean(acc, axis=-1, keepdims=True)
        xc = acc - mu
        var = jnp.mean(xc * xc, axis=-1, keepdims=True)
        ln = xc * lax.rsqrt(var + EPS) * g_ref[...] + beta_ref[...]
        s_ref[...] = (ln * _sigmoid(ln)).astype(BF16)

    halo_map = lambda col: (lambda i: (jnp.maximum(i * per - 1, 0), col))
    return pl.pallas_call(
        body, out_shape=[jax.ShapeDtypeStruct((t, D), F32), jax.ShapeDtypeStruct((t, D), BF16)], grid=(t // tm,),
        in_specs=[_row_spec(tm, D, 0), _row_spec(tm, D, 1), pl.BlockSpec((HALO, D), halo_map(0)),
                  pl.BlockSpec((HALO, D), halo_map(1)), pl.BlockSpec((HALO, D), lambda i: (0, 0)),
                  _vec_spec(D), _vec_spec(D), _vec_spec(D)],
        out_specs=[_row_spec(tm, D), _row_spec(tm, D)],
        scratch_shapes=[pltpu.VMEM((tm + HALO, D), F32), pltpu.VMEM((SUBLANES - 1, tm + SHIFT_ROWS, D), F32)],
        name=name, compiler_params=_params(("parallel",)),
    )(zu, zu, zu, zu, conv_w, conv_b, ln_g, ln_b)


def _conv_ln_bwd(name, hc, ds, ln_g, ln_b, tm=256):
    t = hc.shape[0]

    def body(hc_ref, ds_ref, g_ref, beta_ref, dhc_ref, dg_ref, dbeta_ref, dbias_ref):
        @pl.when(pl.program_id(0) == 0)
        def _():
            dg_ref[...] = jnp.zeros_like(dg_ref)
            dbeta_ref[...] = jnp.zeros_like(dbeta_ref)
            dbias_ref[...] = jnp.zeros_like(dbias_ref)

        hv = hc_ref[...]
        mu = jnp.mean(hv, axis=-1, keepdims=True)
        xc = hv - mu
        rstd = lax.rsqrt(jnp.mean(xc * xc, axis=-1, keepdims=True) + EPS)
        xh = xc * rstd
        ln = xh * g_ref[...] + beta_ref[...]
        sg = _sigmoid(ln)
        dln = ds_ref[...] * (sg * (1.0 + ln * (1.0 - sg)))
        dxh = dln * g_ref[...]
        dh = rstd * (dxh - jnp.mean(dxh, axis=-1, keepdims=True) - xh * jnp.mean(dxh * xh, axis=-1, keepdims=True))
        dhc_ref[...] = dh
        dg_ref[...] += jnp.sum(dln * xh, axis=0, keepdims=True)
        dbeta_ref[...] += jnp.sum(dln, axis=0, keepdims=True)
        dbias_ref[...] += jnp.sum(dh, axis=0, keepdims=True)

    vec = jax.ShapeDtypeStruct((1, D), F32)
    return pl.pallas_call(
        body, out_shape=[jax.ShapeDtypeStruct((t, D), F32), vec, vec, vec], grid=(t // tm,),
        in_specs=[_row_spec(tm, D), _row_spec(tm, D), _vec_spec(D), _vec_spec(D)],
        out_specs=[_row_spec(tm, D), _vec_spec(D), _vec_spec(D), _vec_spec(D)],
        name=name, compiler_params=_params(("arbitrary",)),
    )(hc, ds, ln_g, ln_b)


def _conv_bwd(name, zu, dhc, conv_w, tm=256):
    t = zu.shape[0]
    per = tm // HALO
    steps = t // tm

    group = 4

    def body(a_ref, gl_ref, ah_ref, glh_ref, d_ref, dn_ref, w_ref, dz_ref, dw_ref, ext_ref, sh_ref, dext_ref, dsh_ref,
             sg_ref, part_ref):
        i = pl.program_id(0)

        @pl.when(i == 0)
        def _():
            part_ref[...] = jnp.zeros_like(part_ref)

        sg_ref[...] = _sigmoid(gl_ref[...])
        ext_ref[0:HALO, :] = jnp.where(i > 0, ah_ref[...] * _sigmoid(glh_ref[...]), 0.0)
        ext_ref[HALO:, :] = a_ref[...] * sg_ref[...]
        dext_ref[0:tm, :] = d_ref[...]
        dext_ref[tm:, :] = jnp.where(i < steps - 1, dn_ref[...], 0.0)
        _shifted_copies(ext_ref, sh_ref, tm)
        _shifted_copies(dext_ref, dsh_ref, tm)

        def chunk(r, carry):
            r0 = pl.multiple_of(r * CONV_CHUNK, CONV_CHUNK)
            rows = pl.ds(r0, CONV_CHUNK)
            part = jnp.zeros((CONV_CHUNK, D), F32)
            for kk in range(CONV_K):
                part = part + w_ref[kk:kk + 1, :] * _window(dext_ref, dsh_ref, CONV_K - 1 - kk, r0, CONV_CHUNK)
            sg = sg_ref[rows, :]
            dz_ref[rows, 0:D] = (part * sg).astype(BF16)
            dz_ref[rows, D:] = (part * a_ref[rows, :] * sg * (1.0 - sg)).astype(BF16)
            return carry

        lax.fori_loop(0, tm // CONV_CHUNK, chunk, 0)

        for k0 in range(0, CONV_K, group):
            taps = range(k0, min(k0 + group, CONV_K))

            def tile(r, parts, taps=taps):
                r0 = pl.multiple_of(r * CONV_CHUNK, CONV_CHUNK)
                dv = d_ref[pl.ds(r0, CONV_CHUNK), :]
                out = []
                for p, kk in zip(parts, taps):
                    prod = dv * _window(ext_ref, sh_ref, HALO - CONV_K + 1 + kk, r0, CONV_CHUNK)
                    for s in range(0, CONV_CHUNK, SUBLANES):
                        p = p + prod[s:s + SUBLANES, :]
                    out.append(p)
                return tuple(out)

            parts = lax.fori_loop(0, tm // CONV_CHUNK, tile, tuple(jnp.zeros((SUBLANES, D), F32) for _ in taps))
            for p, kk in zip(parts, taps):
                part_ref[kk * SUBLANES:(kk + 1) * SUBLANES, :] += p

        @pl.when(i == steps - 1)
        def _():
            for kk in range(HALO):
                dw_ref[kk:kk + 1, :] = jnp.sum(part_ref[kk * SUBLANES:(kk + 1) * SUBLANES, :], axis=0, keepdims=True)

    halo_map = lambda col: (lambda i: (jnp.maximum(i * per - 1, 0), col))
    shifted = pltpu.VMEM((SUBLANES - 1, tm + SHIFT_ROWS, D), F32)
    return pl.pallas_call(
        body, out_shape=[jax.ShapeDtypeStruct((t, 2 * D), BF16), jax.ShapeDtypeStruct((HALO, D), F32)],
        grid=(steps,),
        in_specs=[_row_spec(tm, D, 0), _row_spec(tm, D, 1), pl.BlockSpec((HALO, D), halo_map(0)),
                  pl.BlockSpec((HALO, D), halo_map(1)), _row_spec(tm, D),
                  pl.BlockSpec((HALO, D), lambda i: (jnp.minimum((i + 1) * per, t // HALO - 1), 0)),
                  pl.BlockSpec((HALO, D), lambda i: (0, 0))],
        out_specs=[_row_spec(tm, 2 * D), pl.BlockSpec((HALO, D), lambda i: (0, 0))],
        scratch_shapes=[pltpu.VMEM((tm + HALO, D), F32), shifted, pltpu.VMEM((tm + HALO, D), F32), shifted,
                        pltpu.VMEM((tm, D), F32), pltpu.VMEM((HALO * SUBLANES, D), F32)],
        name=name, compiler_params=_params(("arbitrary",)),
    )(zu, zu, zu, zu, dhc, dhc, conv_w)


NT = (((1,), (1,)), ((), ()))


def _resident(w, at):
    block = (None,) * (1 + len(at)) + tuple(w.shape[-2:])
    return [pl.BlockSpec(block, functools.partial(lambda i, k: (k, *at, 0, 0), k=k), pipeline_mode=pl.Buffered(1))
            for k in range(SHARDS)]


def _ffn_fwd(tag, x, h, w, at, gate, tm=512):
    t = x.shape[0]
    slab = min(tm, MM_SLAB)

    def body(*refs):
        h_ref, x_ref, gate_ref = refs[:3]
        wg, wu, wd = refs[3:3 + SHARDS], refs[3 + SHARDS:3 + 2 * SHARDS], refs[3 + 2 * SHARDS:3 + 3 * SHARDS]
        g_ref, u_ref, xn_ref, f_ref = refs[3 + 3 * SHARDS:]
        half_gate = 0.5 * gate_ref[...]
        for r0 in range(0, tm, slab):
            rows = slice(r0, r0 + slab)
            hs = h_ref[rows, :]
            tot = None
            for k in range(SHARDS):
                gk = jnp.dot(hs, wg[k][...], preferred_element_type=F32)
                uk = jnp.dot(hs, wu[k][...], preferred_element_type=F32)
                g_ref[k, rows, :] = gk.astype(BF16)
                u_ref[k, rows, :] = uk.astype(BF16)
                ak = ((gk * _sigmoid(gk)) * uk).astype(BF16)
                part = jnp.dot(ak, wd[k][...], preferred_element_type=F32)
                tot = part if tot is None else tot + part
            xn_ref[rows, :] = x_ref[rows, :] + half_gate * tot
            f_ref[rows, :] = tot.astype(BF16)

    hidden = jax.ShapeDtypeStruct((SHARDS, t, FSH), BF16)
    hidden_spec = pl.BlockSpec((SHARDS, tm, FSH), lambda i: (0, i, 0))
    gv, uv, x_new, f = pl.pallas_call(
        body, out_shape=[hidden, hidden, jax.ShapeDtypeStruct((t, D), F32), jax.ShapeDtypeStruct((t, D), BF16)],
        grid=(t // tm,),
        in_specs=[_row_spec(tm, D), _row_spec(tm, D), _vec_spec(D)] + _resident(w["ffn_wg"], at)
        + _resident(w["ffn_wu"], at) + _resident(w["ffn_wd"], at),
        out_specs=[hidden_spec, hidden_spec, _row_spec(tm, D), _row_spec(tm, D)],
        name=f"ffn_fwd_{tag}", compiler_params=_params(("parallel",)),
    )(h, x, gate, *([w["ffn_wg"]] * SHARDS), *([w["ffn_wu"]] * SHARDS), *([w["ffn_wd"]] * SHARDS))
    return x_new, (gv, uv, f)


def _ffn_hidden_bwd(tag, dx, f, gate, gv, uv, x, gain, scale, w, at, tm=256):
    t = dx.shape[0]
    steps = t // tm
    slab = min(tm, MM_SLAB)

    def body(*refs):
        dx_ref, f_ref, gate_ref, g_ref, u_ref, x_ref, gain_ref, scale_ref = refs[:8]
        wg, wu, wd = refs[8:8 + SHARDS], refs[8 + SHARDS:8 + 2 * SHARDS], refs[8 + 2 * SHARDS:8 + 3 * SHARDS]
        df_ref, dg_ref, du_ref, dxin_ref, dgate_ref, dgain_ref, dscale_ref, dshift_ref = refs[8 + 3 * SHARDS:]
        i = pl.program_id(0)

        @pl.when(i == 0)
        def _():
            dgate_ref[...] = jnp.zeros_like(dgate_ref)
            dgain_ref[...] = jnp.zeros_like(dgain_ref)
            dshift_ref[...] = jnp.zeros_like(dshift_ref)

        half_gate = 0.5 * gate_ref[...]
        norm_w = gain_ref[...] * (1.0 + scale_ref[...])
        for r0 in range(0, tm, slab):
            rows = slice(r0, r0 + slab)
            dxs = dx_ref[rows, :]
            dfs = (half_gate * dxs).astype(BF16)
            df_ref[rows, :] = dfs
            dgate_ref[...] += jnp.sum((0.5 * f_ref[rows, :].astype(F32)) * dxs, axis=0, keepdims=True)
            tot = None
            for k in range(SHARDS):
                da = lax.dot_general(dfs, wd[k][...], NT, preferred_element_type=F32)
                gk, uk = g_ref[k, rows, :].astype(F32), u_ref[k, rows, :].astype(F32)
                sg = _sigmoid(gk)
                dgk = (da * uk * (sg * (1.0 + gk * (1.0 - sg)))).astype(BF16)
                duk = (da * (gk * sg)).astype(BF16)
                dg_ref[k, rows, :] = dgk
                du_ref[k, rows, :] = duk
                part = (lax.dot_general(dgk, wg[k][...], NT, preferred_element_type=F32)
                        + lax.dot_general(duk, wu[k][...], NT, preferred_element_type=F32))
                tot = part if tot is None else tot + part
            xs = x_ref[rows, :]
            r = lax.rsqrt(jnp.mean(xs * xs, axis=-1, keepdims=True) + EPS)
            xh = xs * r
            dxh = tot * norm_w
            dxin_ref[rows, :] = dxs + r * (dxh - xh * jnp.mean(dxh * xh, axis=-1, keepdims=True))
            dgain_ref[...] += jnp.sum(tot * xh, axis=0, keepdims=True)
            dshift_ref[...] += jnp.sum(tot, axis=0, keepdims=True)

        @pl.when(i == steps - 1)
        def _():
            acc = dgain_ref[...]
            dgain_ref[...] = acc * (1.0 + scale_ref[...])
            dscale_ref[...] = acc * gain_ref[...]

    hidden = jax.ShapeDtypeStruct((SHARDS, t, FSH), BF16)
    hidden_spec = pl.BlockSpec((SHARDS, tm, FSH), lambda i: (0, i, 0))
    vec = jax.ShapeDtypeStruct((1, D), F32)
    return pl.pallas_call(
        body, out_shape=[jax.ShapeDtypeStruct((t, D), BF16), hidden, hidden, jax.ShapeDtypeStruct((t, D), F32),
                         vec, vec, vec, vec],
        grid=(steps,),
        in_specs=[_row_spec(tm, D), _row_spec(tm, D), _vec_spec(D), hidden_spec, hidden_spec, _row_spec(tm, D),
                  _vec_spec(D), _vec_spec(D)]
        + _resident(w["ffn_wg"], at) + _resident(w["ffn_wu"], at) + _resident(w["ffn_wd"], at),
        out_specs=[_row_spec(tm, D), hidden_spec, hidden_spec, _row_spec(tm, D)] + [_vec_spec(D)] * 4,
        name=f"ffn_hidden_bwd_{tag}", compiler_params=_params(("arbitrary",)),
    )(dx, f, gate, gv, uv, x, gain, scale, *([w["ffn_wg"]] * SHARDS), *([w["ffn_wu"]] * SHARDS),
      *([w["ffn_wd"]] * SHARDS))


def _ffn_bwd(tag, dx, x, h, saved, w, at, g, scale, gate, into):
    gv, uv, f = saved
    df, dg, du, dx_in, dgate, dgn, dscale, dshift = _ffn_hidden_bwd(tag, dx, f, gate, gv, uv, x, g, scale, w, at)

    def act(blocks):
        gf, uf = blocks[0].astype(F32), blocks[1].astype(F32)
        return ((gf * _sigmoid(gf)) * uf).astype(BF16)

    dwd = _mm_tn(f"ffn_dwd_{tag}", [gv, uv], df, tk=FSH, tn=1024, tt=1024, a_fn=act, a_tiled=True,
                 into=(into["ffn_wd"], at))
    dwg = _mm_tn(f"ffn_dwg_{tag}", h, dg, tk=1024, tn=FSH, tt=1024, b_tiled=True, into=(into["ffn_wg"], at))
    dwu = _mm_tn(f"ffn_dwu_{tag}", h, du, tk=1024, tn=FSH, tt=1024, b_tiled=True, into=(into["ffn_wu"], at))
    return dx_in, dict(ffn_wg=dwg, ffn_wu=dwu, ffn_wd=dwd), dgn, (dshift, dscale, dgate)


def _mix_fwd(tag, x, h, w_in, attn_wo, conv_w, conv_b, ln_g, ln_b, conv_wo, w_out, gate, cos_t, sin_t):
    w_qkv, w_u, w_g = w_in[:, :QKV], w_in[:, QKV:QKV + 2 * D], w_in[:, QKV + 2 * D:]
    zqkv, = _mm(f"mix_qkv_{tag}", [h], [w_qkv], [(0, 0)], _first, [F32], tm=512, tn=768, n_out=QKV)
    zu, = _mm(f"mix_u_{tag}", [h], [w_u], [(0, 0)], _first, [F32], tm=512, tn=1024, n_out=2 * D)
    zg, = _mm(f"mix_g_{tag}", [h], [w_g], [(0, 0)], _first, [BF16], tm=512, tn=1024, n_out=2 * D)
    qkv = _rope_fwd(f"rope_{tag}", zqkv, cos_t, sin_t)
    n = len(DILATIONS)
    outs, lses = [], []
    for grp in range(n):
        o, lse = _attn_fwd(f"attn_fwd_{tag}_{grp}", qkv[grp], qkv[n + grp], qkv[2 * n + grp], grp)
        outs.append(o)
        lses.append(lse)
    ob, of, lj = _attn_merge(f"attn_merge_{tag}", outs, lses)
    hc, s = _conv_fwd(f"conv_fwd_{tag}", zu, conv_w, conv_b, ln_g, ln_b)

    def gated(accs, extras, vecs):
        ya, yc = accs
        sa, sc = _sigmoid(extras[0].astype(F32)), _sigmoid(extras[1].astype(F32))
        return [sa * ya + sc * yc, ya, yc, sa, sc]

    y, ya, yc, sa, sc = _mm(f"mix_y_{tag}", [ob, s], [attn_wo, conv_wo], [(0, 0), (1, 1)], gated, [BF16] * 5,
                            tm=512, tn=1024, n_out=D, extras=[(zg, 0), (zg, 1)])

    def residual(accs, extras, vecs):
        return [extras[0] + vecs[0] * accs[0], accs[0]]

    x_new, f = _mm(f"mix_out_{tag}", [y], [w_out], [(0, 0)], residual, [F32, BF16], tm=512, tn=512, n_out=D,
                   extras=[(x, 0)], vecs=[gate])
    return x_new, (zu, sa, sc, qkv, ob, of, lj, hc, s, y, ya, yc, f, (w_qkv, w_u, w_g))


def _mix_bwd(tag, dx, x, h, saved, attn_wo, conv_w, ln_g, ln_b, conv_wo, w_out, g, scale, gate, cos_t, sin_t):
    zu, sa, sc, qkv, ob, of, lj, hc, s, y, ya, yc, f, w_parts = saved
    n = len(DILATIONS)
    df, dgate = _resgate_bwd(f"mix_gate_bwd_{tag}", dx, f, gate, 1.0)
    dw_out = _mm_tn(f"mix_dwout_{tag}", y, df, tk=1024, tn=1024, tt=512)

    def gated_bwd(accs, extras, vecs):
        dy = accs[0]
        ga, gc = extras[0].astype(F32), extras[1].astype(F32)
        dga = dy * extras[2].astype(F32) * (ga * (1.0 - ga))
        dgc = dy * extras[3].astype(F32) * (gc * (1.0 - gc))
        return [dy * ga, dy * gc, jnp.concatenate([dga, dgc], axis=1)]

    dya, dyc, dzg = _mm(f"mix_dy_{tag}", [df], [w_out], [(0, 0)], gated_bwd, [BF16] * 3, tm=512, tn=1024,
                        n_out=D, trans_b=True, extras=[(sa, 0), (sc, 0), (ya, 0), (yc, 0)],
                        out_widths=[D, D, 2 * D])
    dw_attn = _mm_tn(f"mix_dwattn_{tag}", ob, dya, tk=GW, tn=1024, tt=512)
    dw_conv_o = _mm_tn(f"mix_dwconvo_{tag}", s, dyc, tk=1024, tn=1024, tt=512)
    do, = _mm(f"mix_do_{tag}", [dya], [attn_wo], [(0, 0)], _first, [F32], tm=512, tn=GW, n_out=GW, trans_b=True)
    ds, = _mm(f"mix_ds_{tag}", [dyc], [conv_wo], [(0, 0)], _first, [F32], tm=512, tn=1024, n_out=D, trans_b=True)

    prep = _attn_bwd_prep(f"attn_prep_{tag}", do, of, lj)
    dqs, dks, dvs = [], [], []
    for grp in range(n):
        dq, dk, dv = _attn_bwd(f"attn_bwd_{tag}_{grp}", qkv[grp], qkv[n + grp], qkv[2 * n + grp], prep[n + grp],
                               prep[2 * n + grp], prep[grp], grp)
        dqs.append(dq)
        dks.append(dk)
        dvs.append(dv)
    dzqkv = _rope_bwd(f"rope_bwd_{tag}", dqs + dks + dvs, cos_t, sin_t)

    dhc, dln_g, dln_b, dconv_b = _conv_ln_bwd(f"conv_ln_bwd_{tag}", hc, ds, ln_g, ln_b)
    dzu, dconv_w = _conv_bwd(f"conv_bwd_{tag}", zu, dhc, conv_w)

    dz_parts = [dzqkv, dzu, dzg]
    dw_in = jnp.concatenate(
        [_mm_tn(f"mix_dwin_{tag}_{i}", h, dzp, tk=1024, tn=dzp.shape[1] // 2, tt=512)
         for i, dzp in enumerate(dz_parts)], axis=1)
    dh, = _mm(f"mix_dh_{tag}", dz_parts, list(w_parts), [(0, 0), (1, 1), (2, 2)], _total, [F32], tm=512, tn=512,
              n_out=D, trans_b=True)
    dx_in, dgn, dscale, dshift = _normmod_bwd(f"mix_norm_bwd_{tag}", x, dh, dx, g, scale)
    grads = dict(w_in=dw_in, attn_wo=dw_attn, conv_w=dconv_w[:CONV_K], conv_b=dconv_b, conv_ln_g=dln_g,
                 conv_ln_b=dln_b, conv_wo=dw_conv_o, w_out=dw_out)
    return dx_in, grads, dgn, (dshift, dscale, dgate)


def _local_step(x, mod, target, w, wf):
    t = x.shape[0]
    cos_t, sin_t = _rope_tables(t)
    row = lambda v: v.reshape(1, -1)
    conv_w_pad = jnp.concatenate([wf["conv_w"], jnp.zeros((DEPTH, HALO - CONV_K, D), F32)], axis=1)

    saved = []
    for l in range(DEPTH):
        mods = [mod[l:l + 1, i * D:(i + 1) * D] for i in range(N_MOD)]
        gains = [row(wf["norm_g"][l, i]) for i in range(3)]
        lay = dict(mods=mods, gains=gains)

        lay["x0"] = x
        lay["h0"] = _normmod_fwd(f"norm_a_{l}", x, gains[0], mods[1], mods[0])
        x, lay["ffn0"] = _ffn_fwd(f"a_{l}", x, lay["h0"], w, (l, 0), mods[2])
        lay["x1"] = x
        lay["h1"] = _normmod_fwd(f"norm_m_{l}", x, gains[1], mods[4], mods[3])
        x, lay["mix"] = _mix_fwd(f"{l}", x, lay["h1"], w["w_in"][l], w["attn_wo"][l], conv_w_pad[l],
                                 row(wf["conv_b"][l]), row(wf["conv_ln_g"][l]), row(wf["conv_ln_b"][l]),
                                 w["conv_wo"][l], w["w_out"][l], mods[5], cos_t, sin_t)
        lay["x2"] = x
        lay["h2"] = _normmod_fwd(f"norm_b_{l}", x, gains[2], mods[7], mods[6])
        x, lay["ffn1"] = _ffn_fwd(f"b_{l}", x, lay["h2"], w, (l, 1), mods[8])
        saved.append(lay)

    dx, dfinal_g, loss_cols = _loss_bwd("loss_head", x, target, row(wf["final_g"]))

    ffn_grads = {n: jnp.zeros(w[n].shape, F32) for n in ("ffn_wg", "ffn_wu", "ffn_wd")}
    per_layer = []
    for l in reversed(range(DEPTH)):
        lay = saved[l]
        mods, gains = lay["mods"], lay["gains"]
        dx, ffn_grads, dgn2, dmod2 = _ffn_bwd(f"b_{l}", dx, lay["x2"], lay["h2"], lay["ffn1"], w, (l, 1),
                                              gains[2], mods[7], mods[8], ffn_grads)
        dx, gm, dgn1, dmod1 = _mix_bwd(f"{l}", dx, lay["x1"], lay["h1"], lay["mix"], w["attn_wo"][l],
                                       conv_w_pad[l], row(wf["conv_ln_g"][l]), row(wf["conv_ln_b"][l]),
                                       w["conv_wo"][l], w["w_out"][l], gains[1], mods[4], mods[5], cos_t, sin_t)
        dx, ffn_grads, dgn0, dmod0 = _ffn_bwd(f"a_{l}", dx, lay["x0"], lay["h0"], lay["ffn0"], w, (l, 0),
                                              gains[0], mods[1], mods[2], ffn_grads)
        g = dict(gm)
        g["dmod"] = jnp.concatenate(list(dmod0) + list(dmod1) + list(dmod2), axis=1)
        g["norm_g"] = [dgn0[0], dgn1[0], dgn2[0]]
        for name in ("conv_b", "conv_ln_g", "conv_ln_b"):
            g[name] = g[name][0]
        per_layer.append(g)
    per_layer.reverse()
    grads = {name: [per_layer[l][name] for l in range(DEPTH)] for name in per_layer[0]}
    grads["dmod"] = jnp.concatenate(grads["dmod"], axis=0)
    grads.update(ffn_grads)
    grads["final_g"] = dfinal_g[0]
    return loss_cols, dx, grads


def _split_bits(w):
    bits = lax.bitcast_convert_type(w, jnp.uint32)
    hi = lax.bitcast_convert_type((bits >> 16).astype(jnp.uint16), BF16)
    lo = lax.bitcast_convert_type((bits & 0xFFFF).astype(jnp.uint16), BF16)
    return hi, lo


def _join_bits(hi, lo):
    h = lax.bitcast_convert_type(hi, jnp.uint16).astype(jnp.uint32)
    l = lax.bitcast_convert_type(lo, jnp.uint16).astype(jnp.uint32)
    return lax.bitcast_convert_type((h << 16) | l, F32)


def _pack(parts, rows):
    out = []
    for p in parts:
        flat = p.reshape(-1)
        pad = -flat.shape[0] % LANES
        out.append(jnp.concatenate([flat, jnp.zeros((pad,), flat.dtype)]) if pad else flat)
    flat = jnp.concatenate(out)
    return jnp.concatenate([flat, jnp.zeros((rows * LANES - flat.shape[0],), flat.dtype)]).reshape(rows, LANES)


def _unpack(buf, shapes):
    out, row = [], 0
    for shape in shapes:
        size = 1
        for s in shape:
            size *= s
        rows = -(-size // LANES)
        out.append(buf[row:row + rows].reshape(-1)[:size].reshape(shape))
        row += rows
    return out


def _place():
    x, y, c = lax.axis_index("x"), lax.axis_index("y"), lax.axis_index("c")
    chips = [(1 - x, y), (x, 1 - y), (1 - x, 1 - y)]
    return x, y, c, chips


def _chip_index():
    return (2 * lax.axis_index("x") + lax.axis_index("y")).astype(jnp.int32)


HBM_SPEC = pl.BlockSpec(memory_space=pltpu.HBM)


def _gather_rows(name, block):
    m, n = block.shape

    def body(x_ref, out_ref, send_sems, recv_sems, local_sem):
        x, y, c, chips = _place()
        me, sibling = (x, y, c), (x, y, 1 - c)

        def rows(px, py, pc):
            return out_ref.at[pl.ds((4 * px + 2 * py + pc) * m, m), :]

        def copy(k, owner, to, src=None):
            return pltpu.make_async_remote_copy(
                src_ref=rows(*owner) if src is None else src, dst_ref=rows(*owner), send_sem=send_sems.at[k],
                recv_sem=recv_sems.at[k], device_id=to, device_id_type=MESH)

        mine = pltpu.make_async_copy(x_ref, rows(*me), local_sem)
        mine.start()
        first = [copy(0, me, sibling, src=x_ref)] + [copy(1 + j, me, (*chip, c), src=x_ref)
                                                     for j, chip in enumerate(chips)]
        for cp in first:
            cp.start()
        passed = [copy(4 + j, (*chip, c), sibling) for j, chip in enumerate(chips)]
        for j, chip in enumerate(chips):
            copy(1 + j, (*chip, c), me).wait_recv()
            passed[j].start()
        copy(0, sibling, me).wait_recv()
        for j, chip in enumerate(chips):
            copy(4 + j, (*chip, 1 - c), me).wait_recv()
        for cp in first + passed:
            cp.wait_send()
        mine.wait()

    whole = pl.BlockSpec(memory_space=pltpu.VMEM)
    return pl.pallas_call(
        body, out_shape=jax.ShapeDtypeStruct((N_DEV * m, n), block.dtype), in_specs=[whole], out_specs=whole,
        scratch_shapes=[pltpu.SemaphoreType.DMA((7,)), pltpu.SemaphoreType.DMA((7,)), pltpu.SemaphoreType.DMA],
        name=name,
    )(block)


def _gather_weights(arrays):
    n = len(arrays)

    def body(*refs):
        outs, send_sems, recv_sems = refs[n:2 * n], refs[2 * n], refs[2 * n + 1]
        x, y, c, chips = _place()
        me = 2 * x + y
        sibling = (x, y, 1 - c)
        there = [2 * chip[0] + chip[1] for chip in chips]

        def copy(a, k, chip, layer, to):
            piece = outs[a].at[chip, layer]
            return pltpu.make_async_remote_copy(
                src_ref=piece, dst_ref=piece, send_sem=send_sems.at[6 * a + k], recv_sem=recv_sems.at[6 * a + k],
                device_id=to, device_id_type=MESH)

        first = [copy(a, j, me, c, (*chip, c)) for a in range(n) for j, chip in enumerate(chips)]
        for cp in first:
            cp.start()
        passed = []
        for a in range(n):
            for j in range(3):
                copy(a, j, there[j], c, sibling).wait_recv()
                passed.append(copy(a, 3 + j, there[j], c, sibling))
                passed[-1].start()
        for a in range(n):
            for j in range(3):
                copy(a, 3 + j, there[j], 1 - c, sibling).wait_recv()
        for cp in first + passed:
            cp.wait_send()

    return pl.pallas_call(
        body, out_shape=[jax.ShapeDtypeStruct(a.shape, a.dtype) for a in arrays],
        in_specs=[HBM_SPEC] * n, out_specs=[HBM_SPEC] * n,
        scratch_shapes=[pltpu.SemaphoreType.DMA((6 * n,)), pltpu.SemaphoreType.DMA((6 * n,))],
        input_output_aliases={i: i for i in range(n)}, name="gather_weights",
    )(*arrays)


def _row_block(rows, cols):
    for cand in (512, 256, 128, 64, 32, 16):
        if rows % cand == 0 and cand * cols * 4 <= 2560 * 1024:
            return cand
    return rows


def _swap_layers(grads):
    n = len(grads)

    def body(*refs):
        g_refs, out_refs, send_sems, recv_sems = refs[:n], refs[n:2 * n], refs[2 * n], refs[2 * n + 1]
        x, y, c, _ = _place()
        copies = [pltpu.make_async_remote_copy(
            src_ref=g_refs[a].at[:, 1 - c], dst_ref=out_refs[a], send_sem=send_sems.at[a], recv_sem=recv_sems.at[a],
            device_id=(x, y, 1 - c), device_id_type=MESH) for a in range(n)]
        for cp in copies:
            cp.start()
        for cp in copies:
            cp.wait()

    return pl.pallas_call(
        body, out_shape=[jax.ShapeDtypeStruct((g.shape[0],) + g.shape[2:], F32) for g in grads],
        in_specs=[HBM_SPEC] * n, out_specs=[HBM_SPEC] * n,
        scratch_shapes=[pltpu.SemaphoreType.DMA((n,)), pltpu.SemaphoreType.DMA((n,))], name="swap_layers",
    )(*grads)


def _add_layers(name, grad, other):
    shards, _, rows, cols = grad.shape
    tr = _row_block(rows, cols)

    def body(c_ref, g_ref, o_ref, out_ref):
        out_ref[...] = (g_ref[...] + o_ref[...]).astype(BF16)

    c = lax.axis_index("c").astype(jnp.int32).reshape(1)
    grid_spec = pltpu.PrefetchScalarGridSpec(
        num_scalar_prefetch=1, grid=(shards, rows // tr),
        in_specs=[pl.BlockSpec((None, None, tr, cols), lambda k, i, c_ref: (k, c_ref[0], i, 0)),
                  pl.BlockSpec((None, tr, cols), lambda k, i, c_ref: (k, i, 0))],
        out_specs=pl.BlockSpec((None, tr, cols), lambda k, i, c_ref: (k, i, 0)))
    return pl.pallas_call(
        body, out_shape=jax.ShapeDtypeStruct((shards, rows, cols), BF16), grid_spec=grid_spec,
        name=name, compiler_params=_params(("parallel", "parallel")),
    )(c, grad, other)


def _scatter_chips(parts):
    n = len(parts)

    def body(*refs):
        p_refs, out_refs, send_sems, recv_sems = refs[:n], refs[n:2 * n], refs[2 * n], refs[2 * n + 1]
        x, y, c, chips = _place()
        me = 2 * x + y
        there = [2 * chip[0] + chip[1] for chip in chips]

        def copy(a, j, slot):
            return pltpu.make_async_remote_copy(
                src_ref=p_refs[a].at[there[j]], dst_ref=out_refs[a].at[slot], send_sem=send_sems.at[3 * a + j],
                recv_sem=recv_sems.at[3 * a + j], device_id=(*chips[j], c), device_id_type=MESH)

        sends = [copy(a, j, me) for a in range(n) for j in range(3)]
        for cp in sends:
            cp.start()
        for a in range(n):
            for j in range(3):
                copy(a, j, there[j]).wait_recv()
        for cp in sends:
            cp.wait_send()

    return pl.pallas_call(
        body, out_shape=[jax.ShapeDtypeStruct(p.shape, p.dtype) for p in parts],
        in_specs=[HBM_SPEC] * n, out_specs=[HBM_SPEC] * n,
        scratch_shapes=[pltpu.SemaphoreType.DMA((3 * n,)), pltpu.SemaphoreType.DMA((3 * n,))],
        name="scatter_chips",
    )(*parts)


def _add_chips(name, part, others):
    shards, rows, cols = part.shape
    tr = _row_block(rows, cols)

    def body(pos_ref, own_ref, r0_ref, r1_ref, r2_ref, r3_ref, out_ref):
        me = pos_ref[0]
        own = own_ref[...].astype(F32)
        total = None
        for k, r_ref in enumerate((r0_ref, r1_ref, r2_ref, r3_ref)):
            term = jnp.where(me == k, own, r_ref[...].astype(F32))
            total = term if total is None else total + term
        out_ref[...] = total

    def other(k):
        return pl.BlockSpec((None, tr, cols),
                            lambda i, pos, k=k: (jnp.where(pos[0] == k, (k + 1) % shards, k), i, 0))

    pos = jnp.stack([_chip_index(), lax.axis_index("c").astype(jnp.int32)])
    grid_spec = pltpu.PrefetchScalarGridSpec(
        num_scalar_prefetch=1, grid=(rows // tr,),
        in_specs=[pl.BlockSpec((None, tr, cols), lambda i, pos: (pos[0], i, 0))] + [other(k) for k in range(shards)],
        out_specs=pl.BlockSpec((None, tr, cols), lambda i, pos: (pos[1], i, 0)))
    return pl.pallas_call(
        body, out_shape=jax.ShapeDtypeStruct((DEPTH, rows, cols), F32), grid_spec=grid_spec,
        name=name, compiler_params=_params(("parallel",)),
    )(pos, part, others, others, others, others)


def _join_layers(arrays):
    n = len(arrays)

    def body(*refs):
        outs, send_sems, recv_sems = refs[n:2 * n], refs[2 * n], refs[2 * n + 1]
        x, y, c, _ = _place()

        def copy(a, layer):
            piece = outs[a].at[layer]
            return pltpu.make_async_remote_copy(src_ref=piece, dst_ref=piece, send_sem=send_sems.at[a],
                                                recv_sem=recv_sems.at[a], device_id=(x, y, 1 - c),
                                                device_id_type=MESH)

        sends = [copy(a, c) for a in range(n)]
        for cp in sends:
            cp.start()
        for a in range(n):
            copy(a, 1 - c).wait_recv()
        for cp in sends:
            cp.wait_send()

    return pl.pallas_call(
        body, out_shape=[jax.ShapeDtypeStruct(a.shape, a.dtype) for a in arrays],
        in_specs=[HBM_SPEC] * n, out_specs=[HBM_SPEC] * n,
        scratch_shapes=[pltpu.SemaphoreType.DMA((n,)), pltpu.SemaphoreType.DMA((n,))],
        input_output_aliases={i: i for i in range(n)}, name="join_layers",
    )(*arrays)


def _reduce_scatter(grads):
    sums = [_add_layers(f"add_layers_{a}", g, o) for a, (g, o) in enumerate(zip(grads, _swap_layers(grads)))]
    others = _scatter_chips(sums)
    return _join_layers([_add_chips(f"add_chips_{a}", p, o) for a, (p, o) in enumerate(zip(sums, others))])


def _adamw(name, w, g, m, v):
    shape = w.shape
    cols = shape[-1]
    rows = w.size // cols
    tr = rows
    for cand in (512, 256, 128, 64, 32, 16, 8):
        if rows % cand == 0 and cand * cols * 4 <= 4 * 1024 * 1024:
            tr = cand
            break

    def body(w_ref, g_ref, m_ref, v_ref, d_ref, nm_ref, nv_ref):
        gv = g_ref[...]
        nm = ADAM_B1 * m_ref[...] + (1.0 - ADAM_B1) * gv
        nv = ADAM_B2 * v_ref[...] + (1.0 - ADAM_B2) * (gv * gv)
        m_hat = nm / (1.0 - ADAM_B1 ** ADAM_STEP)
        v_hat = nv / (1.0 - ADAM_B2 ** ADAM_STEP)
        d_ref[...] = -ADAM_LR * (m_hat / (jnp.sqrt(v_hat) + ADAM_EPS) + ADAM_WD * w_ref[...])
        nm_ref[...] = nm
        nv_ref[...] = nv

    spec = pl.BlockSpec((tr, cols), lambda i: (i, 0))
    two = lambda a: a.reshape(rows, cols)
    outs = pl.pallas_call(
        body, out_shape=[jax.ShapeDtypeStruct((rows, cols), F32)] * 3, grid=(rows // tr,),
        in_specs=[spec] * 4, out_specs=[spec] * 3, name=name, compiler_params=_params(("parallel",)),
    )(two(w), two(g), two(m), two(v))
    return [o.reshape(shape) for o in outs]


BIG = ("ffn_wg", "ffn_wu", "ffn_wd", "w_in", "conv_wo", "w_out")
MISC_ROWS = 96


def _own_slot(shard):
    return lax.dynamic_update_slice(jnp.zeros((SHARDS,) + shard.shape, shard.dtype), shard[None],
                                    (_chip_index(),) + (0,) * shard.ndim)


def _as_matrices(a):
    return a.reshape(a.shape[0], a.shape[1], -1, a.shape[-1])


def kernel(x, c, ada_w, ada_b, norm_g, ffn_wg, ffn_wu, ffn_wd, w_in, attn_wo, conv_w, conv_b, conv_ln_g, conv_ln_b, conv_wo, w_out, final_g, loss_target, m_ada_w, m_ada_b, m_norm_g, m_ffn_wg, m_ffn_wu, m_ffn_wd, m_w_in, m_attn_wo, m_conv_w, m_conv_b, m_conv_ln_g, m_conv_ln_b, m_conv_wo, m_w_out, m_final_g, v_ada_w, v_ada_b, v_norm_g, v_ffn_wg, v_ffn_wu, v_ffn_wd, v_w_in, v_attn_wo, v_conv_w, v_conv_b, v_conv_ln_g, v_conv_ln_b, v_conv_wo, v_w_out, v_final_g):
    weights = dict(ada_w=ada_w, ada_b=ada_b, norm_g=norm_g, ffn_wg=ffn_wg, ffn_wu=ffn_wu, ffn_wd=ffn_wd, w_in=w_in,
                   attn_wo=attn_wo, conv_w=conv_w, conv_b=conv_b, conv_ln_g=conv_ln_g, conv_ln_b=conv_ln_b,
                   conv_wo=conv_wo, w_out=w_out, final_g=final_g)
    moments_m = dict(ada_w=m_ada_w, ada_b=m_ada_b, norm_g=m_norm_g, ffn_wg=m_ffn_wg, ffn_wu=m_ffn_wu,
                     ffn_wd=m_ffn_wd, w_in=m_w_in, attn_wo=m_attn_wo, conv_w=m_conv_w, conv_b=m_conv_b,
                     conv_ln_g=m_conv_ln_g, conv_ln_b=m_conv_ln_b, conv_wo=m_conv_wo, w_out=m_w_out,
                     final_g=m_final_g)
    moments_v = dict(ada_w=v_ada_w, ada_b=v_ada_b, norm_g=v_norm_g, ffn_wg=v_ffn_wg, ffn_wu=v_ffn_wu,
                     ffn_wd=v_ffn_wd, w_in=v_w_in, attn_wo=v_attn_wo, conv_w=v_conv_w, conv_b=v_conv_b,
                     conv_ln_g=v_conv_ln_g, conv_ln_b=v_conv_ln_b, conv_wo=v_conv_wo, w_out=v_w_out,
                     final_g=v_final_g)
    layers, shards = range(DEPTH), range(SHARDS)

    bits = {n: _split_bits(weights[n]) for n in EXACT}
    misc_w = jnp.stack([_pack([attn_wo[l].astype(BF16), bits["norm_g"][0][l], bits["norm_g"][1][l],
                               bits["conv_w"][0][l], bits["conv_w"][1][l]], MISC_ROWS) for l in layers])
    sent = [_own_slot(weights[n].astype(BF16)) for n in BIG] + [_own_slot(misc_w)]
    got = dict(zip(BIG + ("misc",), _gather_weights(sent)))
    w = {n: got[n] for n in ("ffn_wg", "ffn_wu", "ffn_wd")}
    w["w_in"] = got["w_in"].transpose(1, 2, 0, 3).reshape(DEPTH, D, -1)
    for n in ("conv_wo", "w_out"):
        w[n] = got[n].transpose(1, 0, 2, 3).reshape(DEPTH, D, D)
    misc_shapes = [(GW, GW), (3, GW), (3, GW), (CONV_K, GW), (CONV_K, GW)]
    pieces = [[_unpack(got["misc"][k, l], misc_shapes) for k in shards] for l in layers]
    whole = lambda i: jnp.stack([jnp.concatenate([pieces[l][k][i] for k in shards], axis=1) for l in layers])
    w["attn_wo"] = whole(0)
    vectors = dict(ada_b=ada_b, conv_b=conv_b, conv_ln_g=conv_ln_g, conv_ln_b=conv_ln_b, final_g=final_g,
                   norm_g=_join_bits(whole(1), whole(2)), conv_w=_join_bits(whole(3), whole(4)))

    me = 2 * _chip_index() + lax.axis_index("c").astype(jnp.int32)
    pad_rows = lambda a, rows: jnp.concatenate([a, jnp.zeros((rows - a.shape[0], a.shape[1]), a.dtype)])
    c_all = _gather_rows("gather_c", pad_rows(c, SUBLANES)).reshape(N_DEV, SUBLANES, D)[:, 0]
    mod_cols = _mod_fwd("mod_fwd", c_all, ada_w.astype(BF16))
    by_dev = _gather_rows("gather_mod", mod_cols).reshape(N_DEV, N_DEV, DEPTH, -1)
    mine = lax.dynamic_index_in_dim(by_dev[0::2], me, axis=1, keepdims=False)
    mod = mine.transpose(1, 0, 2).reshape(DEPTH, -1) + ada_b

    loss_cols, dx, grads = _local_step(x[0], mod, loss_target[0], w, vectors)
    loss = lax.psum(jnp.sum(loss_cols), ("x", "y", "c"))

    dmod_rows = DEPTH * N_MOD * D // LANES
    dmod_all = _gather_rows("gather_dmod", pad_rows(grads["dmod"].reshape(dmod_rows, LANES), 3 * SUBLANES))
    dmod_all = dmod_all.reshape(N_DEV, 3 * SUBLANES, LANES)[:, :dmod_rows].reshape(N_DEV, DEPTH, -1)
    grad_ada_b = _sum_devices("ada_b_grad", dmod_all.reshape(N_DEV, -1)).reshape(DEPTH, -1)
    cols = ada_w.shape[-1]
    dmod_cols = lax.dynamic_slice_in_dim(dmod_all, _chip_index() * cols, cols, axis=2).transpose(1, 0, 2)
    grad_ada_w = _mod_bwd("mod_bwd", c_all.T, dmod_cols)

    cols_of = lambda a, k: a[..., k * GW:(k + 1) * GW]
    misc_g = jnp.stack([jnp.stack([_pack(
        [cols_of(grads["attn_wo"][l], k), cols_of(jnp.stack(grads["norm_g"][l]), k), cols_of(grads["conv_w"][l], k),
         grads["conv_b"][l], grads["conv_ln_g"][l], grads["conv_ln_b"][l],
         grads["final_g"] if l == 0 else jnp.zeros_like(grads["final_g"])], MISC_ROWS)
        for l in layers]) for k in shards])
    by_chip = dict(
        ffn_wg=grads["ffn_wg"], ffn_wu=grads["ffn_wu"], ffn_wd=grads["ffn_wd"],
        w_in=jnp.stack(grads["w_in"]).reshape(DEPTH, D, SHARDS, -1).transpose(2, 0, 1, 3),
        conv_wo=jnp.stack(grads["conv_wo"]).reshape(DEPTH, SHARDS, -1, D).transpose(1, 0, 2, 3),
        w_out=jnp.stack(grads["w_out"]).reshape(DEPTH, SHARDS, -1, D).transpose(1, 0, 2, 3))
    reduced = _reduce_scatter([_as_matrices(by_chip[n]) for n in BIG] + [misc_g])
    summed = {n: r.reshape(weights[n].shape) for n, r in zip(BIG, reduced)}
    small_shapes = [(GW, GW), (3, GW), (CONV_K, GW), (D,), (D,), (D,), (D,)]
    small = [_unpack(reduced[-1][l], small_shapes) for l in layers]
    for i, n in enumerate(("attn_wo", "norm_g", "conv_w", "conv_b", "conv_ln_g", "conv_ln_b")):
        summed[n] = jnp.stack([small[l][i] for l in layers])
    summed["final_g"] = small[0][6]
    summed["ada_w"], summed["ada_b"] = grad_ada_w, grad_ada_b

    deltas, new_m, new_v = {}, {}, {}
    for n in WEIGHTS:
        deltas[n], new_m[n], new_v[n] = _adamw(f"adamw_{n}", weights[n], summed[n], moments_m[n], moments_v[n])

    return (loss, dx[None], *[summed[n] for n in WEIGHTS], *[deltas[n] for n in WEIGHTS],
            *[new_m[n] for n in WEIGHTS], *[new_v[n] for n in WEIGHTS])
```

```python
import functools

import jax
import jax.numpy as jnp
from jax import lax
from jax.experimental import pallas as pl
from jax.experimental.pallas import tpu as pltpu

F32 = jnp.float32
BF16 = jnp.bfloat16

D = 1024
DFF = 2816
HEAD = 64
GW = 256
DILATIONS = (1, 4, 16)
BAND = 128
QKV = 2304
CONV_K = 31
HALO = 32
N_MOD = 9
EPS = 1e-6
NEG_INF = -1e30
DEPTH = 2

SHARDS = 4
FSH = DFF // SHARDS
LANES = 1024
VL = 128

ADAM_LR = 0.001
ADAM_B1 = 0.9
ADAM_B2 = 0.999
ADAM_EPS = 1e-08
ADAM_WD = 0.01
ADAM_STEP = 10

VMEM_LIMIT = 56 * 1024 * 1024

EXACT = ("norm_g", "conv_w")
WEIGHTS = ("ada_w", "ada_b", "norm_g", "ffn_wg", "ffn_wu", "ffn_wd", "w_in", "attn_wo", "conv_w", "conv_b",
           "conv_ln_g", "conv_ln_b", "conv_wo", "w_out", "final_g")

MESH = pl.DeviceIdType.MESH


def _params(sem=None):
    return pltpu.CompilerParams(dimension_semantics=sem, vmem_limit_bytes=VMEM_LIMIT)


def _sigmoid(v):
    return jax.nn.sigmoid(v)


TILE = "tile"
MM_SLAB = 256


def _lead_spec(arr, lead, block, index):
    def index_map(j, i):
        return (*[j if e == TILE else e for e in lead], *index(j, i))
    return pl.BlockSpec((None,) * len(lead) + tuple(block), index_map)


def _entry(e):
    return e if isinstance(e, tuple) else (e, ())


def _mm(name, a_list, b_list, pairs, epilogue, out_dtypes, *, tm, tn, n_out, trans_b=False, extras=(), vecs=(),
        out_widths=None, out_tiled=False):
    a_list = [_entry(a) for a in a_list]
    b_list = [_entry(b) for b in b_list]
    m = a_list[0][0].shape[-2]
    na, nb, ne, nv = len(a_list), len(b_list), len(extras), len(vecs)
    dn = (((1,), (1,)), ((), ())) if trans_b else (((1,), (0,)), ((), ()))

    def body(*refs):
        a_refs = refs[:na]
        b_refs = refs[na:na + nb]
        e_refs = refs[na + nb:na + nb + ne]
        v_refs = refs[na + nb + ne:na + nb + ne + nv]
        o_refs = refs[na + nb + ne + nv:]
        vec_blocks = [v[...] for v in v_refs]
        for r0 in range(0, tm, min(tm, MM_SLAB)):
            rows = slice(r0, r0 + min(tm, MM_SLAB))
            accs = [lax.dot_general(a_refs[ai][rows, :], b_refs[bi][...], dn, preferred_element_type=F32)
                    for ai, bi in pairs]
            res = epilogue(accs, [e[rows, :] for e in e_refs], vec_blocks)
            for o_ref, r in zip(o_refs, res):
                o_ref[rows, :] = r.astype(o_ref.dtype)

    in_specs = [_lead_spec(a, lead, (tm, a.shape[-1]), lambda j, i: (i, 0)) for a, lead in a_list]
    for b, lead in b_list:
        if TILE in lead:
            in_specs.append(_lead_spec(b, lead, b.shape[-2:], lambda j, i: (0, 0)))
        elif trans_b:
            in_specs.append(_lead_spec(b, lead, (tn, b.shape[-1]), lambda j, i: (j, 0)))
        else:
            in_specs.append(_lead_spec(b, lead, (b.shape[-2], tn), lambda j, i: (0, j)))
    for e, where in extras:
        if isinstance(where, tuple):
            in_specs.append(_lead_spec(e, where, (tm, tn), lambda j, i: (i, 0)))
        else:
            in_specs.append(pl.BlockSpec((tm, tn), functools.partial(lambda j, i, off: (i, j + off), off=where)))
    in_specs += [pl.BlockSpec((1, tn), lambda j, i: (0, j)) for _ in vecs]
    nj = n_out // tn
    widths = out_widths or [tn] * len(out_dtypes)
    if out_tiled:
        out_specs = [pl.BlockSpec((None, tm, tn), lambda j, i: (j, i, 0)) for _ in out_dtypes]
        out_shape = [jax.ShapeDtypeStruct((nj, m, tn), dt) for dt in out_dtypes]
    else:
        out_specs = [pl.BlockSpec((tm, wd), lambda j, i: (i, j)) for wd in widths]
        out_shape = [jax.ShapeDtypeStruct((m, nj * wd), dt) for dt, wd in zip(out_dtypes, widths)]
    return pl.pallas_call(
        body, out_shape=out_shape, grid=(nj, m // tm), in_specs=in_specs, out_specs=out_specs,
        name=name, compiler_params=_params(("parallel", "parallel")),
    )(*[a for a, _ in a_list], *[b for b, _ in b_list], *[e for e, _ in extras], *vecs)


def _mm_tn(name, a, b, *, tk, tn, tt, a_fn=None, a_tiled=False, b_tiled=False, into=None):
    a_list = list(a) if a_fn is not None else [a]
    na = len(a_list)
    t = a_list[0].shape[-2]
    nk = a_list[0].shape[0] if a_tiled else a_list[0].shape[1] // tk
    nn = b.shape[0] if b_tiled else b.shape[1] // tn
    steps = t // tt
    has_into = into is not None

    def body(*refs):
        refs = refs[1:] if has_into else refs
        a_refs, b_ref, o_ref, acc_ref = refs[:na], refs[na], refs[na + 1], refs[na + 2]
        s = pl.program_id(2)

        @pl.when(s == 0)
        def _():
            acc_ref[...] = jnp.zeros_like(acc_ref)

        av = a_refs[0][...] if a_fn is None else a_fn([r[...] for r in a_refs])
        acc_ref[...] += lax.dot_general(av, b_ref[...], (((0,), (0,)), ((), ())), preferred_element_type=F32)

        @pl.when(s == steps - 1)
        def _():
            o_ref[...] = acc_ref[...]

    a_spec = (pl.BlockSpec((None, tt, tk), lambda i, j, s: (i, s, 0)) if a_tiled
              else pl.BlockSpec((tt, tk), lambda i, j, s: (s, i)))
    b_spec = (pl.BlockSpec((None, tt, tn), lambda i, j, s: (j, s, 0)) if b_tiled
              else pl.BlockSpec((tt, tn), lambda i, j, s: (s, j)))
    if a_tiled:
        out_dims, tile_index = (nk, tk, nn * tn), lambda i, j, s: (i, 0, j)
    elif b_tiled:
        out_dims, tile_index = (nn, nk * tk, tn), lambda i, j, s: (j, i, 0)
    else:
        out_dims, tile_index = (nk * tk, nn * tn), lambda i, j, s: (i, j)
    tiled = a_tiled or b_tiled
    if has_into:
        buf, lead = into
        def out_index(i, j, s):
            idx = tile_index(i, j, s)
            return (idx[0], *lead, *idx[1:])
        out_spec = pl.BlockSpec((None,) * (1 + len(lead)) + (tk, tn), out_index)
        out_shape = jax.ShapeDtypeStruct(buf.shape, buf.dtype)
        extra_in, extra_specs, aliases = [buf], [pl.BlockSpec(memory_space=pl.ANY)], {0: 0}
    else:
        out_spec = pl.BlockSpec(((None,) if tiled else ()) + (tk, tn), tile_index)
        out_shape = jax.ShapeDtypeStruct(out_dims, F32)
        extra_in, extra_specs, aliases = [], [], {}
    return pl.pallas_call(
        body, out_shape=out_shape, grid=(nk, nn, steps), in_specs=extra_specs + [a_spec] * na + [b_spec],
        out_specs=out_spec, scratch_shapes=[pltpu.VMEM((tk, tn), F32)], input_output_aliases=aliases, name=name,
        compiler_params=_params(("parallel", "parallel", "arbitrary")),
    )(*extra_in, *a_list, b)


def _first(accs, extras, vecs):
    return [accs[0]]


def _total(accs, extras, vecs):
    out = accs[0]
    for r in accs[1:]:
        out = out + r
    return [out]


def _row_spec(tm, width, col=0):
    return pl.BlockSpec((tm, width), functools.partial(lambda i, col: (i, col), col=col))


def _vec_spec(width):
    return pl.BlockSpec((1, width), lambda i: (0, 0))


def _normmod_fwd(name, x, g, scale, shift, tm=512):
    t = x.shape[0]

    def body(x_ref, g_ref, sc_ref, sh_ref, h_ref):
        xv = x_ref[...]
        r = lax.rsqrt(jnp.mean(xv * xv, axis=-1, keepdims=True) + EPS)
        h_ref[...] = ((xv * r) * g_ref[...] * (1.0 + sc_ref[...]) + sh_ref[...]).astype(BF16)

    return pl.pallas_call(
        body, out_shape=jax.ShapeDtypeStruct((t, D), BF16), grid=(t // tm,),
        in_specs=[_row_spec(tm, D), _vec_spec(D), _vec_spec(D), _vec_spec(D)], out_specs=_row_spec(tm, D),
        name=name, compiler_params=_params(("parallel",)),
    )(x, g, scale, shift)


def _proj_norm_bwd(name, dz_parts, w_parts, x, dres, g, scale, tm=256):
    t = x.shape[0]
    steps = t // tm
    n = len(dz_parts)

    def body(*refs):
        dz_refs, w_refs = refs[:n], refs[n:2 * n]
        x_ref, dres_ref, g_ref, sc_ref, dx_ref, dg_ref, dsc_ref, dsh_ref = refs[2 * n:]
        i = pl.program_id(0)

        @pl.when(i == 0)
        def _():
            dg_ref[...] = jnp.zeros_like(dg_ref)
            dsh_ref[...] = jnp.zeros_like(dsh_ref)

        dh = None
        for dz_ref, w_ref in zip(dz_refs, w_refs):
            part = lax.dot_general(dz_ref[...], w_ref[...], (((1,), (1,)), ((), ())), preferred_element_type=F32)
            dh = part if dh is None else dh + part
        xv = x_ref[...]
        r = lax.rsqrt(jnp.mean(xv * xv, axis=-1, keepdims=True) + EPS)
        xh = xv * r
        dxh = dh * (g_ref[...] * (1.0 + sc_ref[...]))
        dx_ref[...] = dres_ref[...] + r * (dxh - xh * jnp.mean(dxh * xh, axis=-1, keepdims=True))
        dg_ref[...] += jnp.sum(dh * xh, axis=0, keepdims=True)
        dsh_ref[...] += jnp.sum(dh, axis=0, keepdims=True)

        @pl.when(i == steps - 1)
        def _():
            acc = dg_ref[...]
            dg_ref[...] = acc * (1.0 + sc_ref[...])
            dsc_ref[...] = acc * g_ref[...]

    vec = jax.ShapeDtypeStruct((1, D), F32)
    resident = [pl.BlockSpec(wp.shape, lambda i: (0, 0), pipeline_mode=pl.Buffered(1)) for wp in w_parts]
    return pl.pallas_call(
        body, out_shape=[jax.ShapeDtypeStruct((t, D), F32), vec, vec, vec], grid=(steps,),
        in_specs=[_row_spec(tm, dz.shape[1]) for dz in dz_parts] + resident
        + [_row_spec(tm, D), _row_spec(tm, D), _vec_spec(D), _vec_spec(D)],
        out_specs=[_row_spec(tm, D), _vec_spec(D), _vec_spec(D), _vec_spec(D)],
        name=name, compiler_params=_params(("arbitrary",)),
    )(*dz_parts, *w_parts, x, dres, g, scale)


def _resgate_bwd(name, dx, f, gate, coef, tm=512):
    t = dx.shape[0]

    def body(dx_ref, f_ref, gate_ref, df_ref, dgate_ref):
        @pl.when(pl.program_id(0) == 0)
        def _():
            dgate_ref[...] = jnp.zeros_like(dgate_ref)

        dxv = dx_ref[...]
        df_ref[...] = ((coef * gate_ref[...]) * dxv).astype(BF16)
        dgate_ref[...] += jnp.sum((coef * f_ref[...].astype(F32)) * dxv, axis=0, keepdims=True)

    return pl.pallas_call(
        body, out_shape=[jax.ShapeDtypeStruct((t, D), BF16), jax.ShapeDtypeStruct((1, D), F32)], grid=(t // tm,),
        in_specs=[_row_spec(tm, D), _row_spec(tm, D), _vec_spec(D)], out_specs=[_row_spec(tm, D), _vec_spec(D)],
        name=name, compiler_params=_params(("arbitrary",)),
    )(dx, f, gate)


def _loss_bwd(name, x, target, g, tm=256):
    t = x.shape[0]

    def body(x_ref, t_ref, g_ref, dx_ref, dg_ref, loss_ref):
        @pl.when(pl.program_id(0) == 0)
        def _():
            dg_ref[...] = jnp.zeros_like(dg_ref)
            loss_ref[...] = jnp.zeros_like(loss_ref)

        xv = x_ref[...]
        r = lax.rsqrt(jnp.mean(xv * xv, axis=-1, keepdims=True) + EPS)
        xh = xv * r
        err = xh * g_ref[...] - t_ref[...]
        dy = err * (1.0 / D)
        dxh = dy * g_ref[...]
        dx_ref[...] = r * (dxh - xh * jnp.mean(dxh * xh, axis=-1, keepdims=True))
        dg_ref[...] += jnp.sum(dy * xh, axis=0, keepdims=True)
        loss_ref[...] += jnp.sum(err * err, axis=0, keepdims=True) * (0.5 / D)

    vec = jax.ShapeDtypeStruct((1, D), F32)
    return pl.pallas_call(
        body, out_shape=[jax.ShapeDtypeStruct((t, D), F32), vec, vec], grid=(t // tm,),
        in_specs=[_row_spec(tm, D), _row_spec(tm, D), _vec_spec(D)],
        out_specs=[_row_spec(tm, D), _vec_spec(D), _vec_spec(D)],
        name=name, compiler_params=_params(("arbitrary",)),
    )(x, target, g)


N_DEV = 8


def _mod_fwd(name, c_all, ada_w):
    cols = ada_w.shape[-1]

    def body(c_ref, w_ref, o_ref):
        cv = c_ref[...]
        ca = (cv * _sigmoid(cv)).astype(BF16)
        o_ref[...] = jnp.dot(ca, w_ref[...], preferred_element_type=F32)

    return pl.pallas_call(
        body, out_shape=jax.ShapeDtypeStruct((N_DEV, DEPTH * cols), F32), grid=(DEPTH,),
        in_specs=[pl.BlockSpec((N_DEV, D), lambda l: (0, 0)), pl.BlockSpec((None, D, cols), lambda l: (l, 0, 0))],
        out_specs=pl.BlockSpec((N_DEV, cols), lambda l: (0, l)), name=name, compiler_params=_params(("parallel",)),
    )(c_all, ada_w)


def _mod_bwd(name, c_cols, dmods, tk=256):
    cols = dmods.shape[-1]

    def body(c_ref, d_ref, o_ref):
        cv = c_ref[...]
        ca = cv * _sigmoid(cv)
        total = ca[:, 0:1] * d_ref[0:1, :]
        for b in range(1, N_DEV):
            total = total + ca[:, b:b + 1] * d_ref[b:b + 1, :]
        o_ref[...] = total

    return pl.pallas_call(
        body, out_shape=jax.ShapeDtypeStruct((DEPTH, D, cols), F32), grid=(DEPTH, D // tk),
        in_specs=[pl.BlockSpec((tk, N_DEV), lambda l, i: (i, 0)), pl.BlockSpec((None, N_DEV, cols), lambda l, i: (l, 0, 0))],
        out_specs=pl.BlockSpec((None, tk, cols), lambda l, i: (l, i, 0)), name=name,
        compiler_params=_params(("parallel", "parallel")),
    )(c_cols, dmods)


def _sum_devices(name, rows):
    n = rows.shape[1]

    def body(r_ref, o_ref):
        total = r_ref[0:1, :]
        for b in range(1, N_DEV):
            total = total + r_ref[b:b + 1, :]
        o_ref[...] = total

    whole = pl.BlockSpec(memory_space=pltpu.VMEM)
    return pl.pallas_call(body, out_shape=jax.ShapeDtypeStruct((1, n), F32), in_specs=[whole], out_specs=whole,
                          name=name)(rows)


def _rope_tables(t):
    half = HEAD // 2
    inv_freq = 10000.0 ** (-(jnp.arange(half, dtype=F32) * 2.0 / HEAD))
    ang = jnp.arange(t, dtype=F32)[:, None] * inv_freq[None, :]
    cos, sin = jnp.cos(ang), jnp.sin(ang)
    cos_t = jnp.tile(jnp.concatenate([cos, cos], axis=1), (1, VL // HEAD))
    sin_t = jnp.tile(jnp.concatenate([-sin, sin], axis=1), (1, VL // HEAD))
    return cos_t, sin_t


def _rotate(tv, cos, sin_signed):
    lane = lax.broadcasted_iota(jnp.int32, tv.shape, 1)
    first = (lane % HEAD) < (HEAD // 2)
    partner = jnp.where(first, pltpu.roll(tv, tv.shape[1] - HEAD // 2, 1), pltpu.roll(tv, HEAD // 2, 1))
    return tv * cos + partner * sin_signed


def _dilated_spec(tm, d):
    return pl.BlockSpec((tm // d, d * GW), lambda i: (i, 0))


def _dilated_shape(t, d, dtype):
    return jax.ShapeDtypeStruct((t // d, d * GW), dtype)


def _every(d, r, tm):
    return pl.ds(r, tm // d, stride=d) if d > 1 else slice(None)


def _rope_fwd(name, zqkv, cos_t, sin_t, tm=512):
    t = zqkv.shape[0]
    ng = len(DILATIONS)
    n = 3 * ng

    halves = GW // VL

    def body(*refs):
        z_refs, cos_ref, sin_ref, o_refs = refs[:halves * n], refs[halves * n], refs[halves * n + 1], refs[halves * n + 2:]
        for idx in range(n):
            d = DILATIONS[idx % ng]
            for r in range(d):
                rows = _every(d, r, tm)
                for hh in range(halves):
                    piece = z_refs[halves * idx + hh][rows, :]
                    if idx < 2 * ng:
                        piece = _rotate(piece, cos_ref[rows, :], sin_ref[rows, :])
                    if idx < ng:
                        piece = piece * (HEAD ** -0.5)
                    o_refs[idx][:, r * GW + hh * VL:r * GW + (hh + 1) * VL] = piece.astype(BF16)

    dils = [DILATIONS[idx % ng] for idx in range(n)]
    return pl.pallas_call(
        body, out_shape=[_dilated_shape(t, d, BF16) for d in dils], grid=(t // tm,),
        in_specs=[_row_spec(tm, VL, col) for col in range(halves * n)] + [_row_spec(tm, VL), _row_spec(tm, VL)],
        out_specs=[_dilated_spec(tm, d) for d in dils], name=name, compiler_params=_params(("parallel",)),
    )(*([zqkv] * (halves * n)), cos_t, sin_t)


def _rope_bwd(name, grads, cos_t, sin_t, tm=512):
    ng = len(DILATIONS)
    n = len(grads)
    t = grads[0].shape[0] * DILATIONS[0]

    halves = GW // VL

    def body(*refs):
        g_refs, cos_ref, sin_ref, o_ref, rows_ref = refs[:n], refs[n], refs[n + 1], refs[n + 2], refs[n + 3]
        cos, sin = cos_ref[...], -sin_ref[...]
        for idx in range(n):
            d = DILATIONS[idx % ng]
            for hh in range(halves):
                for r in range(d):
                    cols = slice(r * GW + hh * VL, r * GW + (hh + 1) * VL)
                    rows_ref[hh, _every(d, r, tm), :] = g_refs[idx][:, cols].astype(F32)
                piece = rows_ref[hh]
                if idx < 2 * ng:
                    piece = _rotate(piece, cos, sin)
                if idx < ng:
                    piece = piece * (HEAD ** -0.5)
                o_ref[:, idx * GW + hh * VL:idx * GW + (hh + 1) * VL] = piece.astype(BF16)

    dils = [DILATIONS[idx % ng] for idx in range(n)]
    return pl.pallas_call(
        body, out_shape=jax.ShapeDtypeStruct((t, n * GW), BF16), grid=(t // tm,),
        in_specs=[_dilated_spec(tm, d) for d in dils] + [_row_spec(tm, VL)] * 2, out_specs=_row_spec(tm, n * GW),
        scratch_shapes=[pltpu.VMEM((halves, tm, VL), F32)], name=name, compiler_params=_params(("parallel",)),
    )(*grads, cos_t, sin_t)


def _head_cols(h):
    return slice(h * HEAD, (h + 1) * HEAD)


def _band_mask_q(has_prev):
    qi = lax.broadcasted_iota(jnp.int32, (BAND, 2 * BAND), 0)
    kj = lax.broadcasted_iota(jnp.int32, (BAND, 2 * BAND), 1)
    dist = qi + BAND - kj
    return (dist >= 0) & (dist <= BAND) & ((kj >= BAND) | has_prev)


def _attn_fwd(name, q, k, v, group):
    d = DILATIONS[group]
    length = q.shape[0]
    qb = min(512, length)
    sub = qb // BAND
    nblk = length // qb

    def body(q_ref, kc_ref, kp_ref, vc_ref, vp_ref, o_ref, lse_ref):
        blk = pl.program_id(1)
        k_ext = jnp.concatenate([kp_ref[...], kc_ref[...]], axis=0)
        v_ext = jnp.concatenate([vp_ref[...], vc_ref[...]], axis=0)
        for j in range(sub):
            mask = _band_mask_q((blk * sub + j) > 0)
            qj = q_ref[j * BAND:(j + 1) * BAND, :]
            kj = k_ext[j * BAND:(j + 2) * BAND, :]
            vj = v_ext[j * BAND:(j + 2) * BAND, :]
            outs, lses = [], []
            for h in range(GW // HEAD):
                s = lax.dot_general(qj[:, _head_cols(h)], kj[:, _head_cols(h)], (((1,), (1,)), ((), ())),
                                    preferred_element_type=F32)
                s = jnp.where(mask, s, NEG_INF)
                m = jnp.max(s, axis=-1, keepdims=True)
                p = jnp.exp(s - m)
                den = jnp.sum(p, axis=-1, keepdims=True)
                o = jnp.dot(p.astype(BF16), vj[:, _head_cols(h)], preferred_element_type=F32)
                outs.append(o / den)
                lses.append(jnp.broadcast_to(m + jnp.log(den), (BAND, HEAD)))
            o_ref[j * BAND:(j + 1) * BAND, :] = jnp.concatenate(outs, axis=1)
            lse_ref[j * BAND:(j + 1) * BAND, :] = jnp.concatenate(lses, axis=1)

    prev = qb // BAND
    cur = lambda r, b: (b, r)
    before = lambda r, b: (jnp.maximum(b * prev - 1, 0), r)
    big, halo = pl.BlockSpec((qb, GW), cur), pl.BlockSpec((BAND, GW), before)
    return pl.pallas_call(
        body, out_shape=[jax.ShapeDtypeStruct((length, d * GW), F32)] * 2, grid=(d, nblk),
        in_specs=[big, big, halo, big, halo], out_specs=[big] * 2, name=name,
        compiler_params=_params(("parallel", "parallel")),
    )(q, k, k, v, v)


def _attn_merge(name, outs, lses, tm=512):
    n = len(outs)
    t = outs[0].shape[0] * DILATIONS[0]

    halves = GW // VL

    def body(*refs):
        in_refs = refs[:2 * n]
        ob_ref, of_ref, lj_ref, rows_ref = refs[2 * n:]
        for hh in range(halves):
            for idx in range(2 * n):
                d = DILATIONS[idx % n]
                for r in range(d):
                    cols = slice(r * GW + hh * VL, r * GW + (hh + 1) * VL)
                    rows_ref[idx, _every(d, r, tm), :] = in_refs[idx][:, cols]
            ls = [rows_ref[n + g] for g in range(n)]
            m = ls[0]
            for v in ls[1:]:
                m = jnp.maximum(m, v)
            es = [jnp.exp(v - m) for v in ls]
            tot = es[0]
            for v in es[1:]:
                tot = tot + v
            acc = (es[0] / tot) * rows_ref[0]
            for g in range(1, n):
                acc = acc + (es[g] / tot) * rows_ref[g]
            half = slice(hh * VL, (hh + 1) * VL)
            ob_ref[:, half] = acc.astype(BF16)
            of_ref[:, half] = acc
            lj_ref[:, half] = m + jnp.log(tot)

    return pl.pallas_call(
        body, out_shape=[jax.ShapeDtypeStruct((t, GW), BF16), jax.ShapeDtypeStruct((t, GW), F32),
                         jax.ShapeDtypeStruct((t, GW), F32)], grid=(t // tm,),
        in_specs=[_dilated_spec(tm, DILATIONS[idx % n]) for idx in range(2 * n)], out_specs=[_row_spec(tm, GW)] * 3,
        scratch_shapes=[pltpu.VMEM((2 * n, tm, VL), F32)], name=name, compiler_params=_params(("parallel",)),
    )(*outs, *lses)


def _attn_bwd_prep(name, do, o, lj, tm=512):
    t = do.shape[0]
    n = len(DILATIONS)

    halves = GW // VL
    per_half = VL // HEAD

    def body(*refs):
        do_refs, o_refs, lj_refs = refs[:halves], refs[halves:2 * halves], refs[2 * halves:3 * halves]
        outs, dsum_ref = refs[3 * halves:3 * halves + 3 * n], refs[3 * halves + 3 * n]
        for hh in range(halves):
            prod = do_refs[hh][...] * o_refs[hh][...]
            parts = [jnp.broadcast_to(jnp.sum(prod[:, _head_cols(h)], axis=-1, keepdims=True), (tm, HEAD))
                     for h in range(per_half)]
            dsum_ref[...] = jnp.concatenate(parts, axis=1)
            for g, d in enumerate(DILATIONS):
                for r in range(d):
                    rows, cols = _every(d, r, tm), slice(r * GW + hh * VL, r * GW + (hh + 1) * VL)
                    outs[g][:, cols] = dsum_ref[rows, :]
                    outs[n + g][:, cols] = do_refs[hh][rows, :].astype(BF16)
                    outs[2 * n + g][:, cols] = lj_refs[hh][rows, :]

    shapes = [_dilated_shape(t, d, dt) for dt in (F32, BF16, F32) for d in DILATIONS]
    half_specs = [_row_spec(tm, VL, hh) for hh in range(halves)]
    return pl.pallas_call(
        body, out_shape=shapes, grid=(t // tm,), in_specs=half_specs * 3,
        out_specs=[_dilated_spec(tm, d) for d in DILATIONS] * 3, scratch_shapes=[pltpu.VMEM((tm, VL), F32)],
        name=name, compiler_params=_params(("parallel",)),
    )(*([do] * halves), *([o] * halves), *([lj] * halves))


def _attn_bwd(name, q, k, v, do, lj, dsum, group):
    d = DILATIONS[group]
    length = q.shape[0]
    qb = min(512, length)
    sub = qb // BAND
    nblk = length // qb
    total = length // BAND

    def body(qc_ref, qn_ref, kc_ref, kp_ref, vc_ref, vp_ref, doc_ref, don_ref, ljc_ref, ljn_ref, dsc_ref, dsn_ref,
             dq_ref, dk_ref, dv_ref):
        blk = pl.program_id(1)
        k_ext = jnp.concatenate([kp_ref[...], kc_ref[...]], axis=0)
        v_ext = jnp.concatenate([vp_ref[...], vc_ref[...]], axis=0)
        heads = range(GW // HEAD)
        nt = (((1,), (1,)), ((), ()))
        tn = (((0,), (0,)), ((), ()))

        def scores(qh, doh, ljh, dsh, kh, vh, mask):
            s = lax.dot_general(qh, kh, nt, preferred_element_type=F32)
            p = jnp.where(mask, jnp.exp(s - ljh), 0.0)
            dp = lax.dot_general(doh, vh, nt, preferred_element_type=F32)
            return p.astype(BF16), (p * (dp - dsh)).astype(BF16)

        held_k, held_v = [None] * len(heads), [None] * len(heads)
        for j in range(sub):
            rows = slice(j * BAND, (j + 1) * BAND)
            rows2 = slice(j * BAND, (j + 2) * BAND)
            mask = _band_mask_q((blk * sub + j) > 0)
            dqs, dks, dvs = [], [], []
            for h in heads:
                hc = _head_cols(h)
                col = slice(h * HEAD, h * HEAD + 1)
                qh, doh, kh2 = qc_ref[rows, hc], doc_ref[rows, hc], k_ext[rows2, hc]
                p, ds = scores(qh, doh, ljc_ref[rows, col], dsc_ref[rows, col], kh2, v_ext[rows2, hc], mask)
                dqs.append(jnp.dot(ds, kh2, preferred_element_type=F32))
                dk2 = lax.dot_general(ds, qh, tn, preferred_element_type=F32)
                dv2 = lax.dot_general(p, doh, tn, preferred_element_type=F32)
                if j > 0:
                    dks.append(held_k[h] + dk2[:BAND])
                    dvs.append(held_v[h] + dv2[:BAND])
                held_k[h], held_v[h] = dk2[BAND:], dv2[BAND:]
            dq_ref[rows, :] = jnp.concatenate(dqs, axis=1)
            if j > 0:
                done = slice((j - 1) * BAND, j * BAND)
                dk_ref[done, :] = jnp.concatenate(dks, axis=1)
                dv_ref[done, :] = jnp.concatenate(dvs, axis=1).astype(BF16)

        last = slice((sub - 1) * BAND, sub * BAND)
        qi = lax.broadcasted_iota(jnp.int32, (BAND, BAND), 0)
        kj = lax.broadcasted_iota(jnp.int32, (BAND, BAND), 1)
        mask = (kj >= qi) & ((blk + 1) * sub < total)
        dks, dvs = [], []
        for h in heads:
            hc = _head_cols(h)
            col = slice(h * HEAD, h * HEAD + 1)
            qh, doh = qn_ref[:, hc], don_ref[:, hc]
            p, ds = scores(qh, doh, ljn_ref[:, col], dsn_ref[:, col], kc_ref[last, hc], vc_ref[last, hc], mask)
            dks.append(held_k[h] + lax.dot_general(ds, qh, tn, preferred_element_type=F32))
            dvs.append(held_v[h] + lax.dot_general(p, doh, tn, preferred_element_type=F32))
        dk_ref[last, :] = jnp.concatenate(dks, axis=1)
        dv_ref[last, :] = jnp.concatenate(dvs, axis=1).astype(BF16)

    prev = qb // BAND
    cur = lambda r, b: (b, r)
    before = lambda r, b: (jnp.maximum(b * prev - 1, 0), r)
    after = lambda r, b: (jnp.minimum((b + 1) * prev, total - 1), r)
    big = pl.BlockSpec((qb, GW), cur)
    nxt = pl.BlockSpec((BAND, GW), after)
    prv = pl.BlockSpec((BAND, GW), before)
    return pl.pallas_call(
        body, out_shape=[jax.ShapeDtypeStruct((length, d * GW), F32), jax.ShapeDtypeStruct((length, d * GW), F32),
                         jax.ShapeDtypeStruct((length, d * GW), BF16)], grid=(d, nblk),
        in_specs=[big, nxt, big, prv, big, prv, big, nxt, big, nxt, big, nxt],
        out_specs=[big] * 3, name=name, compiler_params=_params(("parallel", "parallel")),
    )(q, q, k, k, v, v, do, do, lj, lj, dsum, dsum)


SUBLANES = 8
CONV_CHUNK = 32
SHIFT_ROWS = HALO - SUBLANES


def _shifted_copies(buf_ref, sh_ref, tm):
    for s in range(1, SUBLANES):
        sh_ref[s - 1] = buf_ref[s:s + tm + SHIFT_ROWS, :]


def _window(buf_ref, sh_ref, offset, r0, rows):
    tiles, shift = divmod(offset, SUBLANES)
    src = buf_ref if shift == 0 else sh_ref.at[shift - 1]
    return src[pl.ds(pl.multiple_of(r0 + tiles * SUBLANES, SUBLANES), rows), :]


def _conv_fwd(name, zu, conv_w, conv_b, ln_g, ln_b, tm=256):
    t = zu.shape[0]
    per = tm // HALO

    def body(a_ref, gl_ref, ah_ref, glh_ref, w_ref, b_ref, g_ref, beta_ref, hc_ref, s_ref, ext_ref, sh_ref):
        i = pl.program_id(0)
        halo = ah_ref[...] * _sigmoid(glh_ref[...])
        ext_ref[0:HALO, :] = jnp.where(i > 0, halo, 0.0)
        ext_ref[HALO:, :] = a_ref[...] * _sigmoid(gl_ref[...])
        _shifted_copies(ext_ref, sh_ref, tm)

        def chunk(r, carry):
            r0 = pl.multiple_of(r * CONV_CHUNK, CONV_CHUNK)
            part = jnp.broadcast_to(b_ref[...], (CONV_CHUNK, D))
            for kk in range(CONV_K):
                part = part + w_ref[kk:kk + 1, :] * _window(ext_ref, sh_ref, HALO - CONV_K + 1 + kk, r0, CONV_CHUNK)
            hc_ref[pl.ds(r0, CONV_CHUNK), :] = part
            return carry

        lax.fori_loop(0, tm // CONV_CHUNK, chunk, 0)
        acc = hc_ref[...]
        mu = jnp.mean(acc, axis=-1, keepdims=True)
        xc = acc - mu
        var = jnp.mean(xc * xc, axis=-1, keepdims=True)
        ln = xc * lax.rsqrt(var + EPS) * g_ref[...] + beta_ref[...]
        s_ref[...] = (ln * _sigmoid(ln)).astype(BF16)

    halo_map = lambda col: (lambda i: (jnp.maximum(i * per - 1, 0), col))
    return pl.pallas_call(
        body, out_shape=[jax.ShapeDtypeStruct((t, D), F32), jax.ShapeDtypeStruct((t, D), BF16)], grid=(t // tm,),
        in_specs=[_row_spec(tm, D, 0), _row_spec(tm, D, 1), pl.BlockSpec((HALO, D), halo_map(0)),
                  pl.BlockSpec((HALO, D), halo_map(1)), pl.BlockSpec((HALO, D), lambda i: (0, 0)),
                  _vec_spec(D), _vec_spec(D), _vec_spec(D)],
        out_specs=[_row_spec(tm, D), _row_spec(tm, D)],
        scratch_shapes=[pltpu.VMEM((tm + HALO, D), F32), pltpu.VMEM((SUBLANES - 1, tm + SHIFT_ROWS, D), F32)],
        name=name, compiler_params=_params(("parallel",)),
    )(zu, zu, zu, zu, conv_w, conv_b, ln_g, ln_b)


def _conv_ln_bwd(name, hc, ds, ln_g, ln_b, tm=256):
    t = hc.shape[0]

    def body(hc_ref, ds_ref, g_ref, beta_ref, dhc_ref, dg_ref, dbeta_ref, dbias_ref):
        @pl.when(pl.program_id(0) == 0)
        def _():
            dg_ref[...] = jnp.zeros_like(dg_ref)
            dbeta_ref[...] = jnp.zeros_like(dbeta_ref)
            dbias_ref[...] = jnp.zeros_like(dbias_ref)

        hv = hc_ref[...]
        mu = jnp.mean(hv, axis=-1, keepdims=True)
        xc = hv - mu
        rstd = lax.rsqrt(jnp.mean(xc * xc, axis=-1, keepdims=True) + EPS)
        xh = xc * rstd
        ln = xh * g_ref[...] + beta_ref[...]
        sg = _sigmoid(ln)
        dln = ds_ref[...] * (sg * (1.0 + ln * (1.0 - sg)))
        dxh = dln * g_ref[...]
        dh = rstd * (dxh - jnp.mean(dxh, axis=-1, keepdims=True) - xh * jnp.mean(dxh * xh, axis=-1, keepdims=True))
        dhc_ref[...] = dh
        dg_ref[...] += jnp.sum(dln * xh, axis=0, keepdims=True)
        dbeta_ref[...] += jnp.sum(dln, axis=0, keepdims=True)
        dbias_ref[...] += jnp.sum(dh, axis=0, keepdims=True)

    vec = jax.ShapeDtypeStruct((1, D), F32)
    return pl.pallas_call(
        body, out_shape=[jax.ShapeDtypeStruct((t, D), F32), vec, vec, vec], grid=(t // tm,),
        in_specs=[_row_spec(tm, D), _row_spec(tm, D), _vec_spec(D), _vec_spec(D)],
        out_specs=[_row_spec(tm, D), _vec_spec(D), _vec_spec(D), _vec_spec(D)],
        name=name, compiler_params=_params(("arbitrary",)),
    )(hc, ds, ln_g, ln_b)


def _conv_bwd(name, zu, dhc, conv_w, tm=256):
    t = zu.shape[0]
    per = tm // HALO
    steps = t // tm

    group = 4

    def body(a_ref, gl_ref, ah_ref, glh_ref, d_ref, dn_ref, w_ref, dz_ref, dw_ref, ext_ref, sh_ref, dext_ref, dsh_ref,
             sg_ref, part_ref):
        i = pl.program_id(0)

        @pl.when(i == 0)
        def _():
            part_ref[...] = jnp.zeros_like(part_ref)

        sg_ref[...] = _sigmoid(gl_ref[...])
        ext_ref[0:HALO, :] = jnp.where(i > 0, ah_ref[...] * _sigmoid(glh_ref[...]), 0.0)
        ext_ref[HALO:, :] = a_ref[...] * sg_ref[...]
        dext_ref[0:tm, :] = d_ref[...]
        dext_ref[tm:, :] = jnp.where(i < steps - 1, dn_ref[...], 0.0)
        _shifted_copies(ext_ref, sh_ref, tm)
        _shifted_copies(dext_ref, dsh_ref, tm)

        def chunk(r, carry):
            r0 = pl.multiple_of(r * CONV_CHUNK, CONV_CHUNK)
            rows = pl.ds(r0, CONV_CHUNK)
            part = jnp.zeros((CONV_CHUNK, D), F32)
            for kk in range(CONV_K):
                part = part + w_ref[kk:kk + 1, :] * _window(dext_ref, dsh_ref, CONV_K - 1 - kk, r0, CONV_CHUNK)
            sg = sg_ref[rows, :]
            dz_ref[rows, 0:D] = (part * sg).astype(BF16)
            dz_ref[rows, D:] = (part * a_ref[rows, :] * sg * (1.0 - sg)).astype(BF16)
            return carry

        lax.fori_loop(0, tm // CONV_CHUNK, chunk, 0)

        for k0 in range(0, CONV_K, group):
            taps = range(k0, min(k0 + group, CONV_K))

            def tile(r, parts, taps=taps):
                r0 = pl.multiple_of(r * CONV_CHUNK, CONV_CHUNK)
                dv = d_ref[pl.ds(r0, CONV_CHUNK), :]
                out = []
                for p, kk in zip(parts, taps):
                    prod = dv * _window(ext_ref, sh_ref, HALO - CONV_K + 1 + kk, r0, CONV_CHUNK)
                    for s in range(0, CONV_CHUNK, SUBLANES):
                        p = p + prod[s:s + SUBLANES, :]
                    out.append(p)
                return tuple(out)

            parts = lax.fori_loop(0, tm // CONV_CHUNK, tile, tuple(jnp.zeros((SUBLANES, D), F32) for _ in taps))
            for p, kk in zip(parts, taps):
                part_ref[kk * SUBLANES:(kk + 1) * SUBLANES, :] += p

        @pl.when(i == steps - 1)
        def _():
            for kk in range(HALO):
                dw_ref[kk:kk + 1, :] = jnp.sum(part_ref[kk * SUBLANES:(kk + 1) * SUBLANES, :], axis=0, keepdims=True)

    halo_map = lambda col: (lambda i: (jnp.maximum(i * per - 1, 0), col))
    shifted = pltpu.VMEM((SUBLANES - 1, tm + SHIFT_ROWS, D), F32)
    return pl.pallas_call(
        body, out_shape=[jax.ShapeDtypeStruct((t, 2 * D), BF16), jax.ShapeDtypeStruct((HALO, D), F32)],
        grid=(steps,),
        in_specs=[_row_spec(tm, D, 0), _row_spec(tm, D, 1), pl.BlockSpec((HALO, D), halo_map(0)),
                  pl.BlockSpec((HALO, D), halo_map(1)), _row_spec(tm, D),
                  pl.BlockSpec((HALO, D), lambda i: (jnp.minimum((i + 1) * per, t // HALO - 1), 0)),
                  pl.BlockSpec((HALO, D), lambda i: (0, 0))],
        out_specs=[_row_spec(tm, 2 * D), pl.BlockSpec((HALO, D), lambda i: (0, 0))],
        scratch_shapes=[pltpu.VMEM((tm + HALO, D), F32), shifted, pltpu.VMEM((tm + HALO, D), F32), shifted,
                        pltpu.VMEM((tm, D), F32), pltpu.VMEM((HALO * SUBLANES, D), F32)],
        name=name, compiler_params=_params(("arbitrary",)),
    )(zu, zu, zu, zu, dhc, dhc, conv_w)


NT = (((1,), (1,)), ((), ()))


def _resident(w, at):
    block = (None,) * (1 + len(at)) + tuple(w.shape[-2:])
    return [pl.BlockSpec(block, functools.partial(lambda i, k: (k, *at, 0, 0), k=k), pipeline_mode=pl.Buffered(1))
            for k in range(SHARDS)]


def _ffn_fwd(tag, x, h, w, at, gate, tm=512):
    t = x.shape[0]
    slab = min(tm, MM_SLAB)

    def body(*refs):
        h_ref, x_ref, gate_ref = refs[:3]
        wg, wu, wd = refs[3:3 + SHARDS], refs[3 + SHARDS:3 + 2 * SHARDS], refs[3 + 2 * SHARDS:3 + 3 * SHARDS]
        g_ref, u_ref, xn_ref, f_ref = refs[3 + 3 * SHARDS:]
        half_gate = 0.5 * gate_ref[...]
        for r0 in range(0, tm, slab):
            rows = slice(r0, r0 + slab)
            hs = h_ref[rows, :]
            tot = None
            for k in range(SHARDS):
                gk = jnp.dot(hs, wg[k][...], preferred_element_type=F32)
                uk = jnp.dot(hs, wu[k][...], preferred_element_type=F32)
                g_ref[k, rows, :] = gk.astype(BF16)
                u_ref[k, rows, :] = uk.astype(BF16)
                ak = ((gk * _sigmoid(gk)) * uk).astype(BF16)
                part = jnp.dot(ak, wd[k][...], preferred_element_type=F32)
                tot = part if tot is None else tot + part
            xn_ref[rows, :] = x_ref[rows, :] + half_gate * tot
            f_ref[rows, :] = tot.astype(BF16)

    hidden = jax.ShapeDtypeStruct((SHARDS, t, FSH), BF16)
    hidden_spec = pl.BlockSpec((SHARDS, tm, FSH), lambda i: (0, i, 0))
    gv, uv, x_new, f = pl.pallas_call(
        body, out_shape=[hidden, hidden, jax.ShapeDtypeStruct((t, D), F32), jax.ShapeDtypeStruct((t, D), BF16)],
        grid=(t // tm,),
        in_specs=[_row_spec(tm, D), _row_spec(tm, D), _vec_spec(D)] + _resident(w["ffn_wg"], at)
        + _resident(w["ffn_wu"], at) + _resident(w["ffn_wd"], at),
        out_specs=[hidden_spec, hidden_spec, _row_spec(tm, D), _row_spec(tm, D)],
        name=f"ffn_fwd_{tag}", compiler_params=_params(("parallel",)),
    )(h, x, gate, *([w["ffn_wg"]] * SHARDS), *([w["ffn_wu"]] * SHARDS), *([w["ffn_wd"]] * SHARDS))
    return x_new, (gv, uv, f)


def _ffn_hidden_bwd(tag, dx, f, gate, gv, uv, x, gain, scale, w, at, tm=256):
    t = dx.shape[0]
    steps = t // tm
    slab = min(tm, MM_SLAB)

    def body(*refs):
        dx_ref, f_ref, gate_ref, g_ref, u_ref, x_ref, gain_ref, scale_ref = refs[:8]
        wg, wu, wd = refs[8:8 + SHARDS], refs[8 + SHARDS:8 + 2 * SHARDS], refs[8 + 2 * SHARDS:8 + 3 * SHARDS]
        df_ref, dg_ref, du_ref, dxin_ref, dgate_ref, dgain_ref, dscale_ref, dshift_ref = refs[8 + 3 * SHARDS:]
        i = pl.program_id(0)

        @pl.when(i == 0)
        def _():
            dgate_ref[...] = jnp.zeros_like(dgate_ref)
            dgain_ref[...] = jnp.zeros_like(dgain_ref)
            dshift_ref[...] = jnp.zeros_like(dshift_ref)

        half_gate = 0.5 * gate_ref[...]
        norm_w = gain_ref[...] * (1.0 + scale_ref[...])
        for r0 in range(0, tm, slab):
            rows = slice(r0, r0 + slab)
            dxs = dx_ref[rows, :]
            dfs = (half_gate * dxs).astype(BF16)
            df_ref[rows, :] = dfs
            dgate_ref[...] += jnp.sum((0.5 * f_ref[rows, :].astype(F32)) * dxs, axis=0, keepdims=True)
            tot = None
            for k in range(SHARDS):
                da = lax.dot_general(dfs, wd[k][...], NT, preferred_element_type=F32)
                gk, uk = g_ref[k, rows, :].astype(F32), u_ref[k, rows, :].astype(F32)
                sg = _sigmoid(gk)
                dgk = (da * uk * (sg * (1.0 + gk * (1.0 - sg)))).astype(BF16)
                duk = (da * (gk * sg)).astype(BF16)
                dg_ref[k, rows, :] = dgk
                du_ref[k, rows, :] = duk
                part = (lax.dot_general(dgk, wg[k][...], NT, preferred_element_type=F32)
                        + lax.dot_general(duk, wu[k][...], NT, preferred_element_type=F32))
                tot = part if tot is None else tot + part
            xs = x_ref[rows, :]
            r = lax.rsqrt(jnp.mean(xs * xs, axis=-1, keepdims=True) + EPS)
            xh = xs * r
            dxh = tot * norm_w
            dxin_ref[rows, :] = dxs + r * (dxh - xh * jnp.mean(dxh * xh, axis=-1, keepdims=True))
            dgain_ref[...] += jnp.sum(tot * xh, axis=0, keepdims=True)
            dshift_ref[...] += jnp.sum(tot, axis=0, keepdims=True)

        @pl.when(i == steps - 1)
        def _():
            acc = dgain_ref[...]
            dgain_ref[...] = acc * (1.0 + scale_ref[...])
            dscale_ref[...] = acc * gain_ref[...]

    hidden = jax.ShapeDtypeStruct((SHARDS, t, FSH), BF16)
    hidden_spec = pl.BlockSpec((SHARDS, tm, FSH), lambda i: (0, i, 0))
    vec = jax.ShapeDtypeStruct((1, D), F32)
    return pl.pallas_call(
        body, out_shape=[jax.ShapeDtypeStruct((t, D), BF16), hidden, hidden, jax.ShapeDtypeStruct((t, D), F32),
                         vec, vec, vec, vec],
        grid=(steps,),
        in_specs=[_row_spec(tm, D), _row_spec(tm, D), _vec_spec(D), hidden_spec, hidden_spec, _row_spec(tm, D),
                  _vec_spec(D), _vec_spec(D)]
        + _resident(w["ffn_wg"], at) + _resident(w["ffn_wu"], at) + _resident(w["ffn_wd"], at),
        out_specs=[_row_spec(tm, D), hidden_spec, hidden_spec, _row_spec(tm, D)] + [_vec_spec(D)] * 4,
        name=f"ffn_hidden_bwd_{tag}", compiler_params=_params(("arbitrary",)),
    )(dx, f, gate, gv, uv, x, gain, scale, *([w["ffn_wg"]] * SHARDS), *([w["ffn_wu"]] * SHARDS),
      *([w["ffn_wd"]] * SHARDS))


def _ffn_bwd(tag, dx, x, h, saved, w, at, g, scale, gate, into):
    gv, uv, f = saved
    df, dg, du, dx_in, dgate, dgn, dscale, dshift = _ffn_hidden_bwd(tag, dx, f, gate, gv, uv, x, g, scale, w, at)

    def act(blocks):
        gf, uf = blocks[0].astype(F32), blocks[1].astype(F32)
        return ((gf * _sigmoid(gf)) * uf).astype(BF16)

    dwd = _mm_tn(f"ffn_dwd_{tag}", [gv, uv], df, tk=FSH, tn=1024, tt=2048, a_fn=act, a_tiled=True,
                 into=(into["ffn_wd"], at))
    dwg = _mm_tn(f"ffn_dwg_{tag}", h, dg, tk=1024, tn=FSH, tt=2048, b_tiled=True, into=(into["ffn_wg"], at))
    dwu = _mm_tn(f"ffn_dwu_{tag}", h, du, tk=1024, tn=FSH, tt=2048, b_tiled=True, into=(into["ffn_wu"], at))
    return dx_in, dict(ffn_wg=dwg, ffn_wu=dwu, ffn_wd=dwd), dgn, (dshift, dscale, dgate)


def _mix_fwd(tag, x, h, w_in, attn_wo, conv_w, conv_b, ln_g, ln_b, conv_wo, w_out, gate, cos_t, sin_t):
    w_qkv, w_u, w_g = w_in[:, :QKV], w_in[:, QKV:QKV + 2 * D], w_in[:, QKV + 2 * D:]
    zqkv, = _mm(f"mix_qkv_{tag}", [h], [w_qkv], [(0, 0)], _first, [F32], tm=512, tn=768, n_out=QKV)
    zu, = _mm(f"mix_u_{tag}", [h], [w_u], [(0, 0)], _first, [F32], tm=512, tn=1024, n_out=2 * D)
    zg, = _mm(f"mix_g_{tag}", [h], [w_g], [(0, 0)], _first, [BF16], tm=512, tn=1024, n_out=2 * D)
    qkv = _rope_fwd(f"rope_{tag}", zqkv, cos_t, sin_t)
    n = len(DILATIONS)
    outs, lses = [], []
    for grp in range(n):
        o, lse = _attn_fwd(f"attn_fwd_{tag}_{grp}", qkv[grp], qkv[n + grp], qkv[2 * n + grp], grp)
        outs.append(o)
        lses.append(lse)
    ob, of, lj = _attn_merge(f"attn_merge_{tag}", outs, lses)
    hc, s = _conv_fwd(f"conv_fwd_{tag}", zu, conv_w, conv_b, ln_g, ln_b)

    def gated(accs, extras, vecs):
        ya, yc = accs
        sa, sc = _sigmoid(extras[0].astype(F32)), _sigmoid(extras[1].astype(F32))
        return [sa * ya + sc * yc, ya, yc, sa, sc]

    y, ya, yc, sa, sc = _mm(f"mix_y_{tag}", [ob, s], [attn_wo, conv_wo], [(0, 0), (1, 1)], gated, [BF16] * 5,
                            tm=512, tn=1024, n_out=D, extras=[(zg, 0), (zg, 1)])

    def residual(accs, extras, vecs):
        return [extras[0] + vecs[0] * accs[0], accs[0]]

    x_new, f = _mm(f"mix_out_{tag}", [y], [w_out], [(0, 0)], residual, [F32, BF16], tm=512, tn=512, n_out=D,
                   extras=[(x, 0)], vecs=[gate])
    return x_new, (zu, sa, sc, qkv, ob, of, lj, hc, s, y, ya, yc, f, (w_qkv, w_u, w_g))


def _mix_bwd(tag, dx, x, h, saved, attn_wo, conv_w, ln_g, ln_b, conv_wo, w_out, g, scale, gate, cos_t, sin_t):
    zu, sa, sc, qkv, ob, of, lj, hc, s, y, ya, yc, f, w_parts = saved
    n = len(DILATIONS)
    df, dgate = _resgate_bwd(f"mix_gate_bwd_{tag}", dx, f, gate, 1.0)
    dw_out = _mm_tn(f"mix_dwout_{tag}", y, df, tk=1024, tn=1024, tt=2048)

    def gated_bwd(accs, extras, vecs):
        dy = accs[0]
        ga, gc = extras[0].astype(F32), extras[1].astype(F32)
        dga = dy * extras[2].astype(F32) * (ga * (1.0 - ga))
        dgc = dy * extras[3].astype(F32) * (gc * (1.0 - gc))
        return [dy * ga, dy * gc, jnp.concatenate([dga, dgc], axis=1)]

    dya, dyc, dzg = _mm(f"mix_dy_{tag}", [df], [w_out], [(0, 0)], gated_bwd, [BF16] * 3, tm=512, tn=1024,
                        n_out=D, trans_b=True, extras=[(sa, 0), (sc, 0), (ya, 0), (yc, 0)],
                        out_widths=[D, D, 2 * D])
    dw_attn = _mm_tn(f"mix_dwattn_{tag}", ob, dya, tk=GW, tn=1024, tt=2048)
    dw_conv_o = _mm_tn(f"mix_dwconvo_{tag}", s, dyc, tk=1024, tn=1024, tt=2048)
    do, = _mm(f"mix_do_{tag}", [dya], [attn_wo], [(0, 0)], _first, [F32], tm=512, tn=GW, n_out=GW, trans_b=True)
    ds, = _mm(f"mix_ds_{tag}", [dyc], [conv_wo], [(0, 0)], _first, [F32], tm=512, tn=1024, n_out=D, trans_b=True)

    prep = _attn_bwd_prep(f"attn_prep_{tag}", do, of, lj)
    dqs, dks, dvs = [], [], []
    for grp in range(n):
        dq, dk, dv = _attn_bwd(f"attn_bwd_{tag}_{grp}", qkv[grp], qkv[n + grp], qkv[2 * n + grp], prep[n + grp],
                               prep[2 * n + grp], prep[grp], grp)
        dqs.append(dq)
        dks.append(dk)
        dvs.append(dv)
    dzqkv = _rope_bwd(f"rope_bwd_{tag}", dqs + dks + dvs, cos_t, sin_t)

    dhc, dln_g, dln_b, dconv_b = _conv_ln_bwd(f"conv_ln_bwd_{tag}", hc, ds, ln_g, ln_b)
    dzu, dconv_w = _conv_bwd(f"conv_bwd_{tag}", zu, dhc, conv_w)

    dz_parts = [dzqkv, dzu, dzg]
    dw_in = jnp.concatenate(
        [_mm_tn(f"mix_dwin_{tag}_{i}", h, dzp, tk=1024, tn=dzp.shape[1] // 2, tt=2048)
         for i, dzp in enumerate(dz_parts)], axis=1)
    dx_in, dgn, dscale, dshift = _proj_norm_bwd(f"mix_dh_{tag}", dz_parts, list(w_parts), x, dx, g, scale)
    grads = dict(w_in=dw_in, attn_wo=dw_attn, conv_w=dconv_w[:CONV_K], conv_b=dconv_b, conv_ln_g=dln_g,
                 conv_ln_b=dln_b, conv_wo=dw_conv_o, w_out=dw_out)
    return dx_in, grads, dgn, (dshift, dscale, dgate)


def _local_step(x, mod, target, w, wf):
    t = x.shape[0]
    cos_t, sin_t = _rope_tables(t)
    row = lambda v: v.reshape(1, -1)
    conv_w_pad = jnp.concatenate([wf["conv_w"], jnp.zeros((DEPTH, HALO - CONV_K, D), F32)], axis=1)

    saved = []
    for l in range(DEPTH):
        mods = [mod[l:l + 1, i * D:(i + 1) * D] for i in range(N_MOD)]
        gains = [row(wf["norm_g"][l, i]) for i in range(3)]
        lay = dict(mods=mods, gains=gains)

        lay["x0"] = x
        lay["h0"] = _normmod_fwd(f"norm_a_{l}", x, gains[0], mods[1], mods[0])
        x, lay["ffn0"] = _ffn_fwd(f"a_{l}", x, lay["h0"], w, (l, 0), mods[2])
        lay["x1"] = x
        lay["h1"] = _normmod_fwd(f"norm_m_{l}", x, gains[1], mods[4], mods[3])
        x, lay["mix"] = _mix_fwd(f"{l}", x, lay["h1"], w["w_in"][l], w["attn_wo"][l], conv_w_pad[l],
                                 row(wf["conv_b"][l]), row(wf["conv_ln_g"][l]), row(wf["conv_ln_b"][l]),
                                 w["conv_wo"][l], w["w_out"][l], mods[5], cos_t, sin_t)
        lay["x2"] = x
        lay["h2"] = _normmod_fwd(f"norm_b_{l}", x, gains[2], mods[7], mods[6])
        x, lay["ffn1"] = _ffn_fwd(f"b_{l}", x, lay["h2"], w, (l, 1), mods[8])
        saved.append(lay)

    dx, dfinal_g, loss_cols = _loss_bwd("loss_head", x, target, row(wf["final_g"]))

    ffn_grads = {n: jnp.zeros(w[n].shape, F32) for n in ("ffn_wg", "ffn_wu", "ffn_wd")}
    per_layer = []
    for l in reversed(range(DEPTH)):
        lay = saved[l]
        mods, gains = lay["mods"], lay["gains"]
        dx, ffn_grads, dgn2, dmod2 = _ffn_bwd(f"b_{l}", dx, lay["x2"], lay["h2"], lay["ffn1"], w, (l, 1),
                                              gains[2], mods[7], mods[8], ffn_grads)
        dx, gm, dgn1, dmod1 = _mix_bwd(f"{l}", dx, lay["x1"], lay["h1"], lay["mix"], w["attn_wo"][l],
                                       conv_w_pad[l], row(wf["conv_ln_g"][l]), row(wf["conv_ln_b"][l]),
                                       w["conv_wo"][l], w["w_out"][l], gains[1], mods[4], mods[5], cos_t, sin_t)
        dx, ffn_grads, dgn0, dmod0 = _ffn_bwd(f"a_{l}", dx, lay["x0"], lay["h0"], lay["ffn0"], w, (l, 0),
                                              gains[0], mods[1], mods[2], ffn_grads)
        g = dict(gm)
        g["dmod"] = jnp.concatenate(list(dmod0) + list(dmod1) + list(dmod2), axis=1)
        g["norm_g"] = [dgn0[0], dgn1[0], dgn2[0]]
        for name in ("conv_b", "conv_ln_g", "conv_ln_b"):
            g[name] = g[name][0]
        per_layer.append(g)
    per_layer.reverse()
    grads = {name: [per_layer[l][name] for l in range(DEPTH)] for name in per_layer[0]}
    grads["dmod"] = jnp.concatenate(grads["dmod"], axis=0)
    grads.update(ffn_grads)
    grads["final_g"] = dfinal_g[0]
    return loss_cols, dx, grads


def _split_bits(w):
    bits = lax.bitcast_convert_type(w, jnp.uint32)
    hi = lax.bitcast_convert_type((bits >> 16).astype(jnp.uint16), BF16)
    lo = lax.bitcast_convert_type((bits & 0xFFFF).astype(jnp.uint16), BF16)
    return hi, lo


def _join_bits(hi, lo):
    h = lax.bitcast_convert_type(hi, jnp.uint16).astype(jnp.uint32)
    l = lax.bitcast_convert_type(lo, jnp.uint16).astype(jnp.uint32)
    return lax.bitcast_convert_type((h << 16) | l, F32)


def _pack(parts, rows):
    out = []
    for p in parts:
        flat = p.reshape(-1)
        pad = -flat.shape[0] % LANES
        out.append(jnp.concatenate([flat, jnp.zeros((pad,), flat.dtype)]) if pad else flat)
    flat = jnp.concatenate(out)
    return jnp.concatenate([flat, jnp.zeros((rows * LANES - flat.shape[0],), flat.dtype)]).reshape(rows, LANES)


def _unpack(buf, shapes):
    out, row = [], 0
    for shape in shapes:
        size = 1
        for s in shape:
            size *= s
        rows = -(-size // LANES)
        out.append(buf[row:row + rows].reshape(-1)[:size].reshape(shape))
        row += rows
    return out


def _place():
    x, y, c = lax.axis_index("x"), lax.axis_index("y"), lax.axis_index("c")
    chips = [(1 - x, y), (x, 1 - y), (1 - x, 1 - y)]
    return x, y, c, chips


def _chip_index():
    return (2 * lax.axis_index("x") + lax.axis_index("y")).astype(jnp.int32)


HBM_SPEC = pl.BlockSpec(memory_space=pltpu.HBM)


def _gather_rows(name, block):
    m, n = block.shape

    def body(x_ref, out_ref, send_sems, recv_sems, local_sem):
        x, y, c, chips = _place()
        me, sibling = (x, y, c), (x, y, 1 - c)

        def rows(px, py, pc):
            return out_ref.at[pl.ds((4 * px + 2 * py + pc) * m, m), :]

        def copy(k, owner, to, src=None):
            return pltpu.make_async_remote_copy(
                src_ref=rows(*owner) if src is None else src, dst_ref=rows(*owner), send_sem=send_sems.at[k],
                recv_sem=recv_sems.at[k], device_id=to, device_id_type=MESH)

        mine = pltpu.make_async_copy(x_ref, rows(*me), local_sem)
        mine.start()
        first = [copy(0, me, sibling, src=x_ref)] + [copy(1 + j, me, (*chip, c), src=x_ref)
                                                     for j, chip in enumerate(chips)]
        for cp in first:
            cp.start()
        passed = [copy(4 + j, (*chip, c), sibling) for j, chip in enumerate(chips)]
        for j, chip in enumerate(chips):
            copy(1 + j, (*chip, c), me).wait_recv()
            passed[j].start()
        copy(0, sibling, me).wait_recv()
        for j, chip in enumerate(chips):
            copy(4 + j, (*chip, 1 - c), me).wait_recv()
        for cp in first + passed:
            cp.wait_send()
        mine.wait()

    whole = pl.BlockSpec(memory_space=pltpu.VMEM)
    return pl.pallas_call(
        body, out_shape=jax.ShapeDtypeStruct((N_DEV * m, n), block.dtype), in_specs=[whole], out_specs=whole,
        scratch_shapes=[pltpu.SemaphoreType.DMA((7,)), pltpu.SemaphoreType.DMA((7,)), pltpu.SemaphoreType.DMA],
        name=name,
    )(block)


def _gather_weights(arrays):
    n = len(arrays)

    def body(*refs):
        outs, send_sems, recv_sems = refs[n:2 * n], refs[2 * n], refs[2 * n + 1]
        x, y, c, chips = _place()
        me = 2 * x + y
        sibling = (x, y, 1 - c)
        there = [2 * chip[0] + chip[1] for chip in chips]

        def copy(a, k, chip, layer, to):
            piece = outs[a].at[chip, layer]
            return pltpu.make_async_remote_copy(
                src_ref=piece, dst_ref=piece, send_sem=send_sems.at[6 * a + k], recv_sem=recv_sems.at[6 * a + k],
                device_id=to, device_id_type=MESH)

        first = [copy(a, j, me, c, (*chip, c)) for a in range(n) for j, chip in enumerate(chips)]
        for cp in first:
            cp.start()
        passed = []
        for a in range(n):
            for j in range(3):
                copy(a, j, there[j], c, sibling).wait_recv()
                passed.append(copy(a, 3 + j, there[j], c, sibling))
                passed[-1].start()
        for a in range(n):
            for j in range(3):
                copy(a, 3 + j, there[j], 1 - c, sibling).wait_recv()
        for cp in first + passed:
            cp.wait_send()

    return pl.pallas_call(
        body, out_shape=[jax.ShapeDtypeStruct(a.shape, a.dtype) for a in arrays],
        in_specs=[HBM_SPEC] * n, out_specs=[HBM_SPEC] * n,
        scratch_shapes=[pltpu.SemaphoreType.DMA((6 * n,)), pltpu.SemaphoreType.DMA((6 * n,))],
        input_output_aliases={i: i for i in range(n)}, name="gather_weights",
    )(*arrays)


def _row_block(rows, cols):
    for cand in (512, 256, 128, 64, 32, 16):
        if rows % cand == 0 and cand * cols * 4 <= 2560 * 1024:
            return cand
    return rows


def _swap_layers(grads):
    n = len(grads)

    def body(*refs):
        g_refs, out_refs, send_sems, recv_sems = refs[:n], refs[n:2 * n], refs[2 * n], refs[2 * n + 1]
        x, y, c, _ = _place()
        copies = [pltpu.make_async_remote_copy(
            src_ref=g_refs[a].at[:, 1 - c], dst_ref=out_refs[a], send_sem=send_sems.at[a], recv_sem=recv_sems.at[a],
            device_id=(x, y, 1 - c), device_id_type=MESH) for a in range(n)]
        for cp in copies:
            cp.start()
        for cp in copies:
            cp.wait()

    return pl.pallas_call(
        body, out_shape=[jax.ShapeDtypeStruct((g.shape[0],) + g.shape[2:], F32) for g in grads],
        in_specs=[HBM_SPEC] * n, out_specs=[HBM_SPEC] * n,
        scratch_shapes=[pltpu.SemaphoreType.DMA((n,)), pltpu.SemaphoreType.DMA((n,))], name="swap_layers",
    )(*grads)


def _add_layers(name, grad, other):
    shards, _, rows, cols = grad.shape
    tr = _row_block(rows, cols)

    def body(c_ref, g_ref, o_ref, out_ref):
        out_ref[...] = (g_ref[...] + o_ref[...]).astype(BF16)

    c = lax.axis_index("c").astype(jnp.int32).reshape(1)
    grid_spec = pltpu.PrefetchScalarGridSpec(
        num_scalar_prefetch=1, grid=(shards, rows // tr),
        in_specs=[pl.BlockSpec((None, None, tr, cols), lambda k, i, c_ref: (k, c_ref[0], i, 0)),
                  pl.BlockSpec((None, tr, cols), lambda k, i, c_ref: (k, i, 0))],
        out_specs=pl.BlockSpec((None, tr, cols), lambda k, i, c_ref: (k, i, 0)))
    return pl.pallas_call(
        body, out_shape=jax.ShapeDtypeStruct((shards, rows, cols), BF16), grid_spec=grid_spec,
        name=name, compiler_params=_params(("parallel", "parallel")),
    )(c, grad, other)


def _scatter_chips(parts):
    n = len(parts)

    def body(*refs):
        p_refs, out_refs, send_sems, recv_sems = refs[:n], refs[n:2 * n], refs[2 * n], refs[2 * n + 1]
        x, y, c, chips = _place()
        me = 2 * x + y
        there = [2 * chip[0] + chip[1] for chip in chips]

        def copy(a, j, slot):
            return pltpu.make_async_remote_copy(
                src_ref=p_refs[a].at[there[j]], dst_ref=out_refs[a].at[slot], send_sem=send_sems.at[3 * a + j],
                recv_sem=recv_sems.at[3 * a + j], device_id=(*chips[j], c), device_id_type=MESH)

        sends = [copy(a, j, me) for a in range(n) for j in range(3)]
        for cp in sends:
            cp.start()
        for a in range(n):
            for j in range(3):
                copy(a, j, there[j]).wait_recv()
        for cp in sends:
            cp.wait_send()

    return pl.pallas_call(
        body, out_shape=[jax.ShapeDtypeStruct(p.shape, p.dtype) for p in parts],
        in_specs=[HBM_SPEC] * n, out_specs=[HBM_SPEC] * n,
        scratch_shapes=[pltpu.SemaphoreType.DMA((3 * n,)), pltpu.SemaphoreType.DMA((3 * n,))],
        name="scatter_chips",
    )(*parts)


def _add_chips(name, part, others):
    shards, rows, cols = part.shape
    tr = _row_block(rows, cols)

    def body(pos_ref, own_ref, r0_ref, r1_ref, r2_ref, r3_ref, out_ref):
        me = pos_ref[0]
        own = own_ref[...].astype(F32)
        total = None
        for k, r_ref in enumerate((r0_ref, r1_ref, r2_ref, r3_ref)):
            term = jnp.where(me == k, own, r_ref[...].astype(F32))
            total = term if total is None else total + term
        out_ref[...] = total

    def other(k):
        return pl.BlockSpec((None, tr, cols),
                            lambda i, pos, k=k: (jnp.where(pos[0] == k, (k + 1) % shards, k), i, 0))

    pos = jnp.stack([_chip_index(), lax.axis_index("c").astype(jnp.int32)])
    grid_spec = pltpu.PrefetchScalarGridSpec(
        num_scalar_prefetch=1, grid=(rows // tr,),
        in_specs=[pl.BlockSpec((None, tr, cols), lambda i, pos: (pos[0], i, 0))] + [other(k) for k in range(shards)],
        out_specs=pl.BlockSpec((None, tr, cols), lambda i, pos: (pos[1], i, 0)))
    return pl.pallas_call(
        body, out_shape=jax.ShapeDtypeStruct((DEPTH, rows, cols), F32), grid_spec=grid_spec,
        name=name, compiler_params=_params(("parallel",)),
    )(pos, part, others, others, others, others)


def _join_layers(arrays):
    n = len(arrays)

    def body(*refs):
        outs, send_sems, recv_sems = refs[n:2 * n], refs[2 * n], refs[2 * n + 1]
        x, y, c, _ = _place()

        def copy(a, layer):
            piece = outs[a].at[layer]
            return pltpu.make_async_remote_copy(src_ref=piece, dst_ref=piece, send_sem=send_sems.at[a],
                                                recv_sem=recv_sems.at[a], device_id=(x, y, 1 - c),
                                                device_id_type=MESH)

        sends = [copy(a, c) for a in range(n)]
        for cp in sends:
            cp.start()
        for a in range(n):
            copy(a, 1 - c).wait_recv()
        for cp in sends:
            cp.wait_send()

    return pl.pallas_call(
        body, out_shape=[jax.ShapeDtypeStruct(a.shape, a.dtype) for a in arrays],
        in_specs=[HBM_SPEC] * n, out_specs=[HBM_SPEC] * n,
        scratch_shapes=[pltpu.SemaphoreType.DMA((n,)), pltpu.SemaphoreType.DMA((n,))],
        input_output_aliases={i: i for i in range(n)}, name="join_layers",
    )(*arrays)


def _reduce_scatter(grads):
    sums = [_add_layers(f"add_layers_{a}", g, o) for a, (g, o) in enumerate(zip(grads, _swap_layers(grads)))]
    others = _scatter_chips(sums)
    return _join_layers([_add_chips(f"add_chips_{a}", p, o) for a, (p, o) in enumerate(zip(sums, others))])


def _adamw(name, w, g, m, v):
    shape = w.shape
    cols = shape[-1]
    rows = w.size // cols
    tr = rows
    for cand in (512, 256, 128, 64, 32, 16, 8):
        if rows % cand == 0 and cand * cols * 4 <= 4 * 1024 * 1024:
            tr = cand
            break

    def body(w_ref, g_ref, m_ref, v_ref, d_ref, nm_ref, nv_ref):
        gv = g_ref[...]
        nm = ADAM_B1 * m_ref[...] + (1.0 - ADAM_B1) * gv
        nv = ADAM_B2 * v_ref[...] + (1.0 - ADAM_B2) * (gv * gv)
        m_hat = nm / (1.0 - ADAM_B1 ** ADAM_STEP)
        v_hat = nv / (1.0 - ADAM_B2 ** ADAM_STEP)
        d_ref[...] = -ADAM_LR * (m_hat / (jnp.sqrt(v_hat) + ADAM_EPS) + ADAM_WD * w_ref[...])
        nm_ref[...] = nm
        nv_ref[...] = nv

    spec = pl.BlockSpec((tr, cols), lambda i: (i, 0))
    two = lambda a: a.reshape(rows, cols)
    outs = pl.pallas_call(
        body, out_shape=[jax.ShapeDtypeStruct((rows, cols), F32)] * 3, grid=(rows // tr,),
        in_specs=[spec] * 4, out_specs=[spec] * 3, name=name, compiler_params=_params(("parallel",)),
    )(two(w), two(g), two(m), two(v))
    return [o.reshape(shape) for o in outs]


BIG = ("ffn_wg", "ffn_wu", "ffn_wd", "w_in", "conv_wo", "w_out")
MISC_ROWS = 96


def _own_slot(shard):
    return lax.dynamic_update_slice(jnp.zeros((SHARDS,) + shard.shape, shard.dtype), shard[None],
                                    (_chip_index(),) + (0,) * shard.ndim)


def _as_matrices(a):
    return a.reshape(a.shape[0], a.shape[1], -1, a.shape[-1])


def kernel(x, c, ada_w, ada_b, norm_g, ffn_wg, ffn_wu, ffn_wd, w_in, attn_wo, conv_w, conv_b, conv_ln_g, conv_ln_b, conv_wo, w_out, final_g, loss_target, m_ada_w, m_ada_b, m_norm_g, m_ffn_wg, m_ffn_wu, m_ffn_wd, m_w_in, m_attn_wo, m_conv_w, m_conv_b, m_conv_ln_g, m_conv_ln_b, m_conv_wo, m_w_out, m_final_g, v_ada_w, v_ada_b, v_norm_g, v_ffn_wg, v_ffn_wu, v_ffn_wd, v_w_in, v_attn_wo, v_conv_w, v_conv_b, v_conv_ln_g, v_conv_ln_b, v_conv_wo, v_w_out, v_final_g):
    weights = dict(ada_w=ada_w, ada_b=ada_b, norm_g=norm_g, ffn_wg=ffn_wg, ffn_wu=ffn_wu, ffn_wd=ffn_wd, w_in=w_in,
                   attn_wo=attn_wo, conv_w=conv_w, conv_b=conv_b, conv_ln_g=conv_ln_g, conv_ln_b=conv_ln_b,
                   conv_wo=conv_wo, w_out=w_out, final_g=final_g)
    moments_m = dict(ada_w=m_ada_w, ada_b=m_ada_b, norm_g=m_norm_g, ffn_wg=m_ffn_wg, ffn_wu=m_ffn_wu,
                     ffn_wd=m_ffn_wd, w_in=m_w_in, attn_wo=m_attn_wo, conv_w=m_conv_w, conv_b=m_conv_b,
                     conv_ln_g=m_conv_ln_g, conv_ln_b=m_conv_ln_b, conv_wo=m_conv_wo, w_out=m_w_out,
                     final_g=m_final_g)
    moments_v = dict(ada_w=v_ada_w, ada_b=v_ada_b, norm_g=v_norm_g, ffn_wg=v_ffn_wg, ffn_wu=v_ffn_wu,
                     ffn_wd=v_ffn_wd, w_in=v_w_in, attn_wo=v_attn_wo, conv_w=v_conv_w, conv_b=v_conv_b,
                     conv_ln_g=v_conv_ln_g, conv_ln_b=v_conv_ln_b, conv_wo=v_conv_wo, w_out=v_w_out,
                     final_g=v_final_g)
    layers, shards = range(DEPTH), range(SHARDS)

    bits = {n: _split_bits(weights[n]) for n in EXACT}
    misc_w = jnp.stack([_pack([attn_wo[l].astype(BF16), bits["norm_g"][0][l], bits["norm_g"][1][l],
                               bits["conv_w"][0][l], bits["conv_w"][1][l]], MISC_ROWS) for l in layers])
    sent = [_own_slot(weights[n].astype(BF16)) for n in BIG] + [_own_slot(misc_w)]
    got = dict(zip(BIG + ("misc",), _gather_weights(sent)))
    w = {n: got[n] for n in ("ffn_wg", "ffn_wu", "ffn_wd")}
    w["w_in"] = got["w_in"].transpose(1, 2, 0, 3).reshape(DEPTH, D, -1)
    for n in ("conv_wo", "w_out"):
        w[n] = got[n].transpose(1, 0, 2, 3).reshape(DEPTH, D, D)
    misc_shapes = [(GW, GW), (3, GW), (3, GW), (CONV_K, GW), (CONV_K, GW)]
    pieces = [[_unpack(got["misc"][k, l], misc_shapes) for k in shards] for l in layers]
    whole = lambda i: jnp.stack([jnp.concatenate([pieces[l][k][i] for k in shards], axis=1) for l in layers])
    w["attn_wo"] = whole(0)
    vectors = dict(ada_b=ada_b, conv_b=conv_b, conv_ln_g=conv_ln_g, conv_ln_b=conv_ln_b, final_g=final_g,
                   norm_g=_join_bits(whole(1), whole(2)), conv_w=_join_bits(whole(3), whole(4)))

    me = 2 * _chip_index() + lax.axis_index("c").astype(jnp.int32)
    pad_rows = lambda a, rows: jnp.concatenate([a, jnp.zeros((rows - a.shape[0], a.shape[1]), a.dtype)])
    c_all = _gather_rows("gather_c", pad_rows(c, SUBLANES)).reshape(N_DEV, SUBLANES, D)[:, 0]
    mod_cols = _mod_fwd("mod_fwd", c_all, ada_w.astype(BF16))
    by_dev = _gather_rows("gather_mod", mod_cols).reshape(N_DEV, N_DEV, DEPTH, -1)
    mine = lax.dynamic_index_in_dim(by_dev[0::2], me, axis=1, keepdims=False)
    mod = mine.transpose(1, 0, 2).reshape(DEPTH, -1) + ada_b

    loss_cols, dx, grads = _local_step(x[0], mod, loss_target[0], w, vectors)
    loss = lax.psum(jnp.sum(loss_cols), ("x", "y", "c"))

    dmod_rows = DEPTH * N_MOD * D // LANES
    dmod_all = _gather_rows("gather_dmod", pad_rows(grads["dmod"].reshape(dmod_rows, LANES), 3 * SUBLANES))
    dmod_all = dmod_all.reshape(N_DEV, 3 * SUBLANES, LANES)[:, :dmod_rows].reshape(N_DEV, DEPTH, -1)
    grad_ada_b = _sum_devices("ada_b_grad", dmod_all.reshape(N_DEV, -1)).reshape(DEPTH, -1)
    cols = ada_w.shape[-1]
    dmod_cols = lax.dynamic_slice_in_dim(dmod_all, _chip_index() * cols, cols, axis=2).transpose(1, 0, 2)
    grad_ada_w = _mod_bwd("mod_bwd", c_all.T, dmod_cols)

    cols_of = lambda a, k: a[..., k * GW:(k + 1) * GW]
    misc_g = jnp.stack([jnp.stack([_pack(
        [cols_of(grads["attn_wo"][l], k), cols_of(jnp.stack(grads["norm_g"][l]), k), cols_of(grads["conv_w"][l], k),
         grads["conv_b"][l], grads["conv_ln_g"][l], grads["conv_ln_b"][l],
         grads["final_g"] if l == 0 else jnp.zeros_like(grads["final_g"])], MISC_ROWS)
        for l in layers]) for k in shards])
    by_chip = dict(
        ffn_wg=grads["ffn_wg"], ffn_wu=grads["ffn_wu"], ffn_wd=grads["ffn_wd"],
        w_in=jnp.stack(grads["w_in"]).reshape(DEPTH, D, SHARDS, -1).transpose(2, 0, 1, 3),
        conv_wo=jnp.stack(grads["conv_wo"]).reshape(DEPTH, SHARDS, -1, D).transpose(1, 0, 2, 3),
        w_out=jnp.stack(grads["w_out"]).reshape(DEPTH, SHARDS, -1, D).transpose(1, 0, 2, 3))
    reduced = _reduce_scatter([_as_matrices(by_chip[n]) for n in BIG] + [misc_g])
    summed = {n: r.reshape(weights[n].shape) for n, r in zip(BIG, reduced)}
    small_shapes = [(GW, GW), (3, GW), (CONV_K, GW), (D,), (D,), (D,), (D,)]
    small = [_unpack(reduced[-1][l], small_shapes) for l in layers]
    for i, n in enumerate(("attn_wo", "norm_g", "conv_w", "conv_b", "conv_ln_g", "conv_ln_b")):
        summed[n] = jnp.stack([small[l][i] for l in layers])
    summed["final_g"] = small[0][6]
    summed["ada_w"], summed["ada_b"] = grad_ada_w, grad_ada_b

    deltas, new_m, new_v = {}, {}, {}
    for n in WEIGHTS:
        deltas[n], new_m[n], new_v[n] = _adamw(f"adamw_{n}", weights[n], summed[n], moments_m[n], moments_v[n])

    return (loss, dx[None], *[summed[n] for n in WEIGHTS], *[deltas[n] for n in WEIGHTS],
            *[new_m[n] for n in WEIGHTS], *[new_v[n] for n in WEIGHTS])
```

```python
import functools

import jax
import jax.numpy as jnp
from jax import lax
from jax.experimental import pallas as pl
from jax.experimental.pallas import tpu as pltpu

F32 = jnp.float32
BF16 = jnp.bfloat16

D = 1024
DFF = 2816
HEAD = 64
GW = 256
DILATIONS = (1, 4, 16)
BAND = 128
QKV = 2304
CONV_K = 31
HALO = 32
N_MOD = 9
EPS = 1e-6
NEG_INF = -1e30
DEPTH = 2

SHARDS = 4
FSH = DFF // SHARDS
LANES = 1024
VL = 128

ADAM_LR = 0.001
ADAM_B1 = 0.9
ADAM_B2 = 0.999
ADAM_EPS = 1e-08
ADAM_WD = 0.01
ADAM_STEP = 10

VMEM_LIMIT = 56 * 1024 * 1024

EXACT = ("norm_g", "conv_w")
WEIGHTS = ("ada_w", "ada_b", "norm_g", "ffn_wg", "ffn_wu", "ffn_wd", "w_in", "attn_wo", "conv_w", "conv_b",
           "conv_ln_g", "conv_ln_b", "conv_wo", "w_out", "final_g")

MESH = pl.DeviceIdType.MESH


def _params(sem=None):
    return pltpu.CompilerParams(dimension_semantics=sem, vmem_limit_bytes=VMEM_LIMIT)


def _sigmoid(v):
    return jax.nn.sigmoid(v)


MM_SLAB = 256


def _mm_tn(name, a, b, *, tk, tn, tt, a_fn=None, a_tiled=False, b_tiled=False, into=None):
    a_list = list(a) if a_fn is not None else [a]
    na = len(a_list)
    t = a_list[0].shape[-2]
    nk = a_list[0].shape[0] if a_tiled else a_list[0].shape[1] // tk
    nn = b.shape[0] if b_tiled else b.shape[1] // tn
    steps = t // tt
    has_into = into is not None

    def body(*refs):
        refs = refs[1:] if has_into else refs
        a_refs, b_ref, o_ref, acc_ref = refs[:na], refs[na], refs[na + 1], refs[na + 2]
        s = pl.program_id(2)

        @pl.when(s == 0)
        def _():
            acc_ref[...] = jnp.zeros_like(acc_ref)

        av = a_refs[0][...] if a_fn is None else a_fn([r[...] for r in a_refs])
        acc_ref[...] += lax.dot_general(av, b_ref[...], (((0,), (0,)), ((), ())), preferred_element_type=F32)

        @pl.when(s == steps - 1)
        def _():
            o_ref[...] = acc_ref[...]

    a_spec = (pl.BlockSpec((None, tt, tk), lambda i, j, s: (i, s, 0)) if a_tiled
              else pl.BlockSpec((tt, tk), lambda i, j, s: (s, i)))
    b_spec = (pl.BlockSpec((None, tt, tn), lambda i, j, s: (j, s, 0)) if b_tiled
              else pl.BlockSpec((tt, tn), lambda i, j, s: (s, j)))
    if a_tiled:
        out_dims, tile_index = (nk, tk, nn * tn), lambda i, j, s: (i, 0, j)
    elif b_tiled:
        out_dims, tile_index = (nn, nk * tk, tn), lambda i, j, s: (j, i, 0)
    else:
        out_dims, tile_index = (nk * tk, nn * tn), lambda i, j, s: (i, j)
    tiled = a_tiled or b_tiled
    if has_into:
        buf, lead = into
        def out_index(i, j, s):
            idx = tile_index(i, j, s)
            return (idx[0], *lead, *idx[1:])
        out_spec = pl.BlockSpec((None,) * (1 + len(lead)) + (tk, tn), out_index)
        out_shape = jax.ShapeDtypeStruct(buf.shape, buf.dtype)
        extra_in, extra_specs, aliases = [buf], [pl.BlockSpec(memory_space=pl.ANY)], {0: 0}
    else:
        out_spec = pl.BlockSpec(((None,) if tiled else ()) + (tk, tn), tile_index)
        out_shape = jax.ShapeDtypeStruct(out_dims, F32)
        extra_in, extra_specs, aliases = [], [], {}
    return pl.pallas_call(
        body, out_shape=out_shape, grid=(nk, nn, steps), in_specs=extra_specs + [a_spec] * na + [b_spec],
        out_specs=out_spec, scratch_shapes=[pltpu.VMEM((tk, tn), F32)], input_output_aliases=aliases, name=name,
        compiler_params=_params(("parallel", "parallel", "arbitrary")),
    )(*extra_in, *a_list, b)


def _row_spec(tm, width, col=0):
    return pl.BlockSpec((tm, width), functools.partial(lambda i, col: (i, col), col=col))


def _vec_spec(width):
    return pl.BlockSpec((1, width), lambda i: (0, 0))


def _proj_norm_bwd(name, dz_parts, w_parts, x, dres, g, scale, tm=256):
    t = x.shape[0]
    steps = t // tm
    n = len(dz_parts)

    def body(*refs):
        dz_refs, w_refs = refs[:n], refs[n:2 * n]
        x_ref, dres_ref, g_ref, sc_ref, dx_ref, dg_ref, dsc_ref, dsh_ref = refs[2 * n:]
        i = pl.program_id(0)

        @pl.when(i == 0)
        def _():
            dg_ref[...] = jnp.zeros_like(dg_ref)
            dsh_ref[...] = jnp.zeros_like(dsh_ref)

        dh = None
        for dz_ref, w_ref in zip(dz_refs, w_refs):
            part = lax.dot_general(dz_ref[...], w_ref[...], (((1,), (1,)), ((), ())), preferred_element_type=F32)
            dh = part if dh is None else dh + part
        xv = x_ref[...]
        r = lax.rsqrt(jnp.mean(xv * xv, axis=-1, keepdims=True) + EPS)
        xh = xv * r
        dxh = dh * (g_ref[...] * (1.0 + sc_ref[...]))
        dx_ref[...] = dres_ref[...] + r * (dxh - xh * jnp.mean(dxh * xh, axis=-1, keepdims=True))
        dg_ref[...] += jnp.sum(dh * xh, axis=0, keepdims=True)
        dsh_ref[...] += jnp.sum(dh, axis=0, keepdims=True)

        @pl.when(i == steps - 1)
        def _():
            acc = dg_ref[...]
            dg_ref[...] = acc * (1.0 + sc_ref[...])
            dsc_ref[...] = acc * g_ref[...]

    vec = jax.ShapeDtypeStruct((1, D), F32)
    resident = [pl.BlockSpec(wp.shape, lambda i: (0, 0), pipeline_mode=pl.Buffered(1)) for wp in w_parts]
    return pl.pallas_call(
        body, out_shape=[jax.ShapeDtypeStruct((t, D), F32), vec, vec, vec], grid=(steps,),
        in_specs=[_row_spec(tm, dz.shape[1]) for dz in dz_parts] + resident
        + [_row_spec(tm, D), _row_spec(tm, D), _vec_spec(D), _vec_spec(D)],
        out_specs=[_row_spec(tm, D), _vec_spec(D), _vec_spec(D), _vec_spec(D)],
        name=name, compiler_params=_params(("arbitrary",)),
    )(*dz_parts, *w_parts, x, dres, g, scale)


def _loss_bwd(name, x, target, g, tm=256):
    t = x.shape[0]

    def body(x_ref, t_ref, g_ref, dx_ref, dg_ref, loss_ref):
        @pl.when(pl.program_id(0) == 0)
        def _():
            dg_ref[...] = jnp.zeros_like(dg_ref)
            loss_ref[...] = jnp.zeros_like(loss_ref)

        xv = x_ref[...]
        r = lax.rsqrt(jnp.mean(xv * xv, axis=-1, keepdims=True) + EPS)
        xh = xv * r
        err = xh * g_ref[...] - t_ref[...]
        dy = err * (1.0 / D)
        dxh = dy * g_ref[...]
        dx_ref[...] = r * (dxh - xh * jnp.mean(dxh * xh, axis=-1, keepdims=True))
        dg_ref[...] += jnp.sum(dy * xh, axis=0, keepdims=True)
        loss_ref[...] += jnp.sum(err * err, axis=0, keepdims=True) * (0.5 / D)

    vec = jax.ShapeDtypeStruct((1, D), F32)
    return pl.pallas_call(
        body, out_shape=[jax.ShapeDtypeStruct((t, D), F32), vec, vec], grid=(t // tm,),
        in_specs=[_row_spec(tm, D), _row_spec(tm, D), _vec_spec(D)],
        out_specs=[_row_spec(tm, D), _vec_spec(D), _vec_spec(D)],
        name=name, compiler_params=_params(("arbitrary",)),
    )(x, target, g)


N_DEV = 8


def _mod_fwd(name, c_all, ada_w):
    cols = ada_w.shape[-1]

    def body(c_ref, w_ref, o_ref):
        cv = c_ref[...]
        ca = (cv * _sigmoid(cv)).astype(BF16)
        o_ref[...] = jnp.dot(ca, w_ref[...], preferred_element_type=F32)

    return pl.pallas_call(
        body, out_shape=jax.ShapeDtypeStruct((N_DEV, DEPTH * cols), F32), grid=(DEPTH,),
        in_specs=[pl.BlockSpec((N_DEV, D), lambda l: (0, 0)), pl.BlockSpec((None, D, cols), lambda l: (l, 0, 0))],
        out_specs=pl.BlockSpec((N_DEV, cols), lambda l: (0, l)), name=name, compiler_params=_params(("parallel",)),
    )(c_all, ada_w)


def _mod_bwd(name, c_cols, dmods, tk=256):
    cols = dmods.shape[-1]

    def body(c_ref, d_ref, o_ref):
        cv = c_ref[...]
        ca = cv * _sigmoid(cv)
        total = ca[:, 0:1] * d_ref[0:1, :]
        for b in range(1, N_DEV):
            total = total + ca[:, b:b + 1] * d_ref[b:b + 1, :]
        o_ref[...] = total

    return pl.pallas_call(
        body, out_shape=jax.ShapeDtypeStruct((DEPTH, D, cols), F32), grid=(DEPTH, D // tk),
        in_specs=[pl.BlockSpec((tk, N_DEV), lambda l, i: (i, 0)), pl.BlockSpec((None, N_DEV, cols), lambda l, i: (l, 0, 0))],
        out_specs=pl.BlockSpec((None, tk, cols), lambda l, i: (l, i, 0)), name=name,
        compiler_params=_params(("parallel", "parallel")),
    )(c_cols, dmods)


def _sum_devices(name, rows):
    n = rows.shape[1]

    def body(r_ref, o_ref):
        total = r_ref[0:1, :]
        for b in range(1, N_DEV):
            total = total + r_ref[b:b + 1, :]
        o_ref[...] = total

    whole = pl.BlockSpec(memory_space=pltpu.VMEM)
    return pl.pallas_call(body, out_shape=jax.ShapeDtypeStruct((1, n), F32), in_specs=[whole], out_specs=whole,
                          name=name)(rows)


def _rope_tables(t):
    half = HEAD // 2
    inv_freq = 10000.0 ** (-(jnp.arange(half, dtype=F32) * 2.0 / HEAD))
    ang = jnp.arange(t, dtype=F32)[:, None] * inv_freq[None, :]
    cos, sin = jnp.cos(ang), jnp.sin(ang)
    cos_t = jnp.tile(jnp.concatenate([cos, cos], axis=1), (1, VL // HEAD))
    sin_t = jnp.tile(jnp.concatenate([-sin, sin], axis=1), (1, VL // HEAD))
    return cos_t, sin_t


def _rotate(tv, cos, sin_signed):
    lane = lax.broadcasted_iota(jnp.int32, tv.shape, 1)
    first = (lane % HEAD) < (HEAD // 2)
    partner = jnp.where(first, pltpu.roll(tv, tv.shape[1] - HEAD // 2, 1), pltpu.roll(tv, HEAD // 2, 1))
    return tv * cos + partner * sin_signed


def _dilated_spec(tm, d):
    return pl.BlockSpec((tm // d, d * GW), lambda i: (i, 0))


def _dilated_shape(t, d, dtype):
    return jax.ShapeDtypeStruct((t // d, d * GW), dtype)


def _every(d, r, tm):
    return pl.ds(r, tm // d, stride=d) if d > 1 else slice(None)


def _rope_fwd(name, zqkv, cos_t, sin_t, tm=512):
    t = zqkv.shape[0]
    ng = len(DILATIONS)
    n = 3 * ng

    halves = GW // VL

    def body(*refs):
        z_refs, cos_ref, sin_ref, o_refs = refs[:halves * n], refs[halves * n], refs[halves * n + 1], refs[halves * n + 2:]
        for idx in range(n):
            d = DILATIONS[idx % ng]
            for r in range(d):
                rows = _every(d, r, tm)
                for hh in range(halves):
                    piece = z_refs[halves * idx + hh][rows, :]
                    if idx < 2 * ng:
                        piece = _rotate(piece, cos_ref[rows, :], sin_ref[rows, :])
                    if idx < ng:
                        piece = piece * (HEAD ** -0.5)
                    o_refs[idx][:, r * GW + hh * VL:r * GW + (hh + 1) * VL] = piece.astype(BF16)

    dils = [DILATIONS[idx % ng] for idx in range(n)]
    return pl.pallas_call(
        body, out_shape=[_dilated_shape(t, d, BF16) for d in dils], grid=(t // tm,),
        in_specs=[_row_spec(tm, VL, col) for col in range(halves * n)] + [_row_spec(tm, VL), _row_spec(tm, VL)],
        out_specs=[_dilated_spec(tm, d) for d in dils], name=name, compiler_params=_params(("parallel",)),
    )(*([zqkv] * (halves * n)), cos_t, sin_t)


def _rope_bwd(name, grads, cos_t, sin_t, tm=512):
    ng = len(DILATIONS)
    n = len(grads)
    t = grads[0].shape[0] * DILATIONS[0]

    halves = GW // VL

    def body(*refs):
        g_refs, cos_ref, sin_ref, o_ref, rows_ref = refs[:n], refs[n], refs[n + 1], refs[n + 2], refs[n + 3]
        cos, sin = cos_ref[...], -sin_ref[...]
        for idx in range(n):
            d = DILATIONS[idx % ng]
            for hh in range(halves):
                for r in range(d):
                    cols = slice(r * GW + hh * VL, r * GW + (hh + 1) * VL)
                    rows_ref[hh, _every(d, r, tm), :] = g_refs[idx][:, cols].astype(F32)
                piece = rows_ref[hh]
                if idx < 2 * ng:
                    piece = _rotate(piece, cos, sin)
                if idx < ng:
                    piece = piece * (HEAD ** -0.5)
                o_ref[:, idx * GW + hh * VL:idx * GW + (hh + 1) * VL] = piece.astype(BF16)

    dils = [DILATIONS[idx % ng] for idx in range(n)]
    return pl.pallas_call(
        body, out_shape=jax.ShapeDtypeStruct((t, n * GW), BF16), grid=(t // tm,),
        in_specs=[_dilated_spec(tm, d) for d in dils] + [_row_spec(tm, VL)] * 2, out_specs=_row_spec(tm, n * GW),
        scratch_shapes=[pltpu.VMEM((halves, tm, VL), F32)], name=name, compiler_params=_params(("parallel",)),
    )(*grads, cos_t, sin_t)


def _head_cols(h):
    return slice(h * HEAD, (h + 1) * HEAD)


def _band_mask_q(has_prev):
    qi = lax.broadcasted_iota(jnp.int32, (BAND, 2 * BAND), 0)
    kj = lax.broadcasted_iota(jnp.int32, (BAND, 2 * BAND), 1)
    dist = qi + BAND - kj
    return (dist >= 0) & (dist <= BAND) & ((kj >= BAND) | has_prev)


def _attn_fwd(name, q, k, v, group):
    d = DILATIONS[group]
    length = q.shape[0]
    qb = min(512, length)
    sub = qb // BAND
    nblk = length // qb

    def body(q_ref, kc_ref, kp_ref, vc_ref, vp_ref, o_ref, lse_ref):
        blk = pl.program_id(1)
        k_ext = jnp.concatenate([kp_ref[...], kc_ref[...]], axis=0)
        v_ext = jnp.concatenate([vp_ref[...], vc_ref[...]], axis=0)
        for j in range(sub):
            mask = _band_mask_q((blk * sub + j) > 0)
            qj = q_ref[j * BAND:(j + 1) * BAND, :]
            kj = k_ext[j * BAND:(j + 2) * BAND, :]
            vj = v_ext[j * BAND:(j + 2) * BAND, :]
            outs, lses = [], []
            for h in range(GW // HEAD):
                s = lax.dot_general(qj[:, _head_cols(h)], kj[:, _head_cols(h)], (((1,), (1,)), ((), ())),
                                    preferred_element_type=F32)
                s = jnp.where(mask, s, NEG_INF)
                m = jnp.max(s, axis=-1, keepdims=True)
                p = jnp.exp(s - m)
                den = jnp.sum(p, axis=-1, keepdims=True)
                o = jnp.dot(p.astype(BF16), vj[:, _head_cols(h)], preferred_element_type=F32)
                outs.append(o / den)
                lses.append(jnp.broadcast_to(m + jnp.log(den), (BAND, HEAD)))
            o_ref[j * BAND:(j + 1) * BAND, :] = jnp.concatenate(outs, axis=1)
            lse_ref[j * BAND:(j + 1) * BAND, :] = jnp.concatenate(lses, axis=1)

    prev = qb // BAND
    cur = lambda r, b: (b, r)
    before = lambda r, b: (jnp.maximum(b * prev - 1, 0), r)
    big, halo = pl.BlockSpec((qb, GW), cur), pl.BlockSpec((BAND, GW), before)
    return pl.pallas_call(
        body, out_shape=[jax.ShapeDtypeStruct((length, d * GW), F32)] * 2, grid=(d, nblk),
        in_specs=[big, big, halo, big, halo], out_specs=[big] * 2, name=name,
        compiler_params=_params(("parallel", "parallel")),
    )(q, k, k, v, v)


def _attn_merge(name, outs, lses, tm=512):
    n = len(outs)
    t = outs[0].shape[0] * DILATIONS[0]

    halves = GW // VL

    def body(*refs):
        in_refs = refs[:2 * n]
        ob_ref, of_ref, lj_ref, rows_ref = refs[2 * n:]
        for hh in range(halves):
            for idx in range(2 * n):
                d = DILATIONS[idx % n]
                for r in range(d):
                    cols = slice(r * GW + hh * VL, r * GW + (hh + 1) * VL)
                    rows_ref[idx, _every(d, r, tm), :] = in_refs[idx][:, cols]
            ls = [rows_ref[n + g] for g in range(n)]
            m = ls[0]
            for v in ls[1:]:
                m = jnp.maximum(m, v)
            es = [jnp.exp(v - m) for v in ls]
            tot = es[0]
            for v in es[1:]:
                tot = tot + v
            acc = (es[0] / tot) * rows_ref[0]
            for g in range(1, n):
                acc = acc + (es[g] / tot) * rows_ref[g]
            half = slice(hh * VL, (hh + 1) * VL)
            ob_ref[:, half] = acc.astype(BF16)
            of_ref[:, half] = acc
            lj_ref[:, half] = m + jnp.log(tot)

    return pl.pallas_call(
        body, out_shape=[jax.ShapeDtypeStruct((t, GW), BF16), jax.ShapeDtypeStruct((t, GW), F32),
                         jax.ShapeDtypeStruct((t, GW), F32)], grid=(t // tm,),
        in_specs=[_dilated_spec(tm, DILATIONS[idx % n]) for idx in range(2 * n)], out_specs=[_row_spec(tm, GW)] * 3,
        scratch_shapes=[pltpu.VMEM((2 * n, tm, VL), F32)], name=name, compiler_params=_params(("parallel",)),
    )(*outs, *lses)


def _attn_bwd_prep(name, do, o, lj, tm=512):
    t = do.shape[0]
    n = len(DILATIONS)

    halves = GW // VL
    per_half = VL // HEAD

    def body(*refs):
        do_refs, o_refs, lj_refs = refs[:halves], refs[halves:2 * halves], refs[2 * halves:3 * halves]
        outs, dsum_ref = refs[3 * halves:3 * halves + 3 * n], refs[3 * halves + 3 * n]
        for hh in range(halves):
            prod = do_refs[hh][...] * o_refs[hh][...]
            parts = [jnp.broadcast_to(jnp.sum(prod[:, _head_cols(h)], axis=-1, keepdims=True), (tm, HEAD))
                     for h in range(per_half)]
            dsum_ref[...] = jnp.concatenate(parts, axis=1)
            for g, d in enumerate(DILATIONS):
                for r in range(d):
                    rows, cols = _every(d, r, tm), slice(r * GW + hh * VL, r * GW + (hh + 1) * VL)
                    outs[g][:, cols] = dsum_ref[rows, :]
                    outs[n + g][:, cols] = do_refs[hh][rows, :].astype(BF16)
                    outs[2 * n + g][:, cols] = lj_refs[hh][rows, :]

    shapes = [_dilated_shape(t, d, dt) for dt in (F32, BF16, F32) for d in DILATIONS]
    half_specs = [_row_spec(tm, VL, hh) for hh in range(halves)]
    return pl.pallas_call(
        body, out_shape=shapes, grid=(t // tm,), in_specs=half_specs * 3,
        out_specs=[_dilated_spec(tm, d) for d in DILATIONS] * 3, scratch_shapes=[pltpu.VMEM((tm, VL), F32)],
        name=name, compiler_params=_params(("parallel",)),
    )(*([do] * halves), *([o] * halves), *([lj] * halves))


def _attn_bwd(name, q, k, v, do, lj, dsum, group):
    d = DILATIONS[group]
    length = q.shape[0]
    qb = min(512, length)
    sub = qb // BAND
    nblk = length // qb
    total = length // BAND

    def body(qc_ref, qn_ref, kc_ref, kp_ref, vc_ref, vp_ref, doc_ref, don_ref, ljc_ref, ljn_ref, dsc_ref, dsn_ref,
             dq_ref, dk_ref, dv_ref):
        blk = pl.program_id(1)
        k_ext = jnp.concatenate([kp_ref[...], kc_ref[...]], axis=0)
        v_ext = jnp.concatenate([vp_ref[...], vc_ref[...]], axis=0)
        heads = range(GW // HEAD)
        nt = (((1,), (1,)), ((), ()))
        tn = (((0,), (0,)), ((), ()))

        def scores(qh, doh, ljh, dsh, kh, vh, mask):
            s = lax.dot_general(qh, kh, nt, preferred_element_type=F32)
            p = jnp.where(mask, jnp.exp(s - ljh), 0.0)
            dp = lax.dot_general(doh, vh, nt, preferred_element_type=F32)
            return p.astype(BF16), (p * (dp - dsh)).astype(BF16)

        held_k, held_v = [None] * len(heads), [None] * len(heads)
        for j in range(sub):
            rows = slice(j * BAND, (j + 1) * BAND)
            rows2 = slice(j * BAND, (j + 2) * BAND)
            mask = _band_mask_q((blk * sub + j) > 0)
            dqs, dks, dvs = [], [], []
            for h in heads:
                hc = _head_cols(h)
                col = slice(h * HEAD, h * HEAD + 1)
                qh, doh, kh2 = qc_ref[rows, hc], doc_ref[rows, hc], k_ext[rows2, hc]
                p, ds = scores(qh, doh, ljc_ref[rows, col], dsc_ref[rows, col], kh2, v_ext[rows2, hc], mask)
                dqs.append(jnp.dot(ds, kh2, preferred_element_type=F32))
                dk2 = lax.dot_general(ds, qh, tn, preferred_element_type=F32)
                dv2 = lax.dot_general(p, doh, tn, preferred_element_type=F32)
                if j > 0:
                    dks.append(held_k[h] + dk2[:BAND])
                    dvs.append(held_v[h] + dv2[:BAND])
                held_k[h], held_v[h] = dk2[BAND:], dv2[BAND:]
            dq_ref[rows, :] = jnp.concatenate(dqs, axis=1)
            if j > 0:
                done = slice((j - 1) * BAND, j * BAND)
                dk_ref[done, :] = jnp.concatenate(dks, axis=1)
                dv_ref[done, :] = jnp.concatenate(dvs, axis=1).astype(BF16)

        last = slice((sub - 1) * BAND, sub * BAND)
        qi = lax.broadcasted_iota(jnp.int32, (BAND, BAND), 0)
        kj = lax.broadcasted_iota(jnp.int32, (BAND, BAND), 1)
        mask = (kj >= qi) & ((blk + 1) * sub < total)
        dks, dvs = [], []
        for h in heads:
            hc = _head_cols(h)
            col = slice(h * HEAD, h * HEAD + 1)
            qh, doh = qn_ref[:, hc], don_ref[:, hc]
            p, ds = scores(qh, doh, ljn_ref[:, col], dsn_ref[:, col], kc_ref[last, hc], vc_ref[last, hc], mask)
            dks.append(held_k[h] + lax.dot_general(ds, qh, tn, preferred_element_type=F32))
            dvs.append(held_v[h] + lax.dot_general(p, doh, tn, preferred_element_type=F32))
        dk_ref[last, :] = jnp.concatenate(dks, axis=1)
        dv_ref[last, :] = jnp.concatenate(dvs, axis=1).astype(BF16)

    prev = qb // BAND
    cur = lambda r, b: (b, r)
    before = lambda r, b: (jnp.maximum(b * prev - 1, 0), r)
    after = lambda r, b: (jnp.minimum((b + 1) * prev, total - 1), r)
    big = pl.BlockSpec((qb, GW), cur)
    nxt = pl.BlockSpec((BAND, GW), after)
    prv = pl.BlockSpec((BAND, GW), before)
    return pl.pallas_call(
        body, out_shape=[jax.ShapeDtypeStruct((length, d * GW), F32), jax.ShapeDtypeStruct((length, d * GW), F32),
                         jax.ShapeDtypeStruct((length, d * GW), BF16)], grid=(d, nblk),
        in_specs=[big, nxt, big, prv, big, prv, big, nxt, big, nxt, big, nxt],
        out_specs=[big] * 3, name=name, compiler_params=_params(("parallel", "parallel")),
    )(q, q, k, k, v, v, do, do, lj, lj, dsum, dsum)


SUBLANES = 8
CONV_CHUNK = 32
SHIFT_ROWS = HALO - SUBLANES


def _shifted_copies(buf_ref, sh_ref, tm):
    for s in range(1, SUBLANES):
        sh_ref[s - 1] = buf_ref[s:s + tm + SHIFT_ROWS, :]


def _window(buf_ref, sh_ref, offset, r0, rows):
    tiles, shift = divmod(offset, SUBLANES)
    src = buf_ref if shift == 0 else sh_ref.at[shift - 1]
    return src[pl.ds(pl.multiple_of(r0 + tiles * SUBLANES, SUBLANES), rows), :]


def _conv_fwd(name, zu, conv_w, conv_b, ln_g, ln_b, tm=256):
    t = zu.shape[0]
    per = tm // HALO

    def body(a_ref, gl_ref, ah_ref, glh_ref, w_ref, b_ref, g_ref, beta_ref, hc_ref, s_ref, ext_ref, sh_ref):
        i = pl.program_id(0)
        halo = ah_ref[...] * _sigmoid(glh_ref[...])
        ext_ref[0:HALO, :] = jnp.where(i > 0, halo, 0.0)
        ext_ref[HALO:, :] = a_ref[...] * _sigmoid(gl_ref[...])
        _shifted_copies(ext_ref, sh_ref, tm)

        def chunk(r, carry):
            r0 = pl.multiple_of(r * CONV_CHUNK, CONV_CHUNK)
            part = jnp.broadcast_to(b_ref[...], (CONV_CHUNK, D))
            for kk in range(CONV_K):
                part = part + w_ref[kk:kk + 1, :] * _window(ext_ref, sh_ref, HALO - CONV_K + 1 + kk, r0, CONV_CHUNK)
            hc_ref[pl.ds(r0, CONV_CHUNK), :] = part
            return carry

        lax.fori_loop(0, tm // CONV_CHUNK, chunk, 0)
        acc = hc_ref[...]
        mu = jnp.mean(acc, axis=-1, keepdims=True)
        xc = acc - mu
        var = jnp.mean(xc * xc, axis=-1, keepdims=True)
        ln = xc * lax.rsqrt(var + EPS) * g_ref[...] + beta_ref[...]
        s_ref[...] = (ln * _sigmoid(ln)).astype(BF16)

    halo_map = lambda col: (lambda i: (jnp.maximum(i * per - 1, 0), col))
    return pl.pallas_call(
        body, out_shape=[jax.ShapeDtypeStruct((t, D), F32), jax.ShapeDtypeStruct((t, D), BF16)], grid=(t // tm,),
        in_specs=[_row_spec(tm, D, 0), _row_spec(tm, D, 1), pl.BlockSpec((HALO, D), halo_map(0)),
                  pl.BlockSpec((HALO, D), halo_map(1)), pl.BlockSpec((HALO, D), lambda i: (0, 0)),
                  _vec_spec(D), _vec_spec(D), _vec_spec(D)],
        out_specs=[_row_spec(tm, D), _row_spec(tm, D)],
        scratch_shapes=[pltpu.VMEM((tm + HALO, D), F32), pltpu.VMEM((SUBLANES - 1, tm + SHIFT_ROWS, D), F32)],
        name=name, compiler_params=_params(("parallel",)),
    )(zu, zu, zu, zu, conv_w, conv_b, ln_g, ln_b)


def _conv_ln_bwd(name, hc, ds, ln_g, ln_b, tm=256):
    t = hc.shape[0]

    def body(hc_ref, ds_ref, g_ref, beta_ref, dhc_ref, dg_ref, dbeta_ref, dbias_ref):
        @pl.when(pl.program_id(0) == 0)
        def _():
            dg_ref[...] = jnp.zeros_like(dg_ref)
            dbeta_ref[...] = jnp.zeros_like(dbeta_ref)
            dbias_ref[...] = jnp.zeros_like(dbias_ref)

        hv = hc_ref[...]
        mu = jnp.mean(hv, axis=-1, keepdims=True)
        xc = hv - mu
        rstd = lax.rsqrt(jnp.mean(xc * xc, axis=-1, keepdims=True) + EPS)
        xh = xc * rstd
        ln = xh * g_ref[...] + beta_ref[...]
        sg = _sigmoid(ln)
        dln = ds_ref[...] * (sg * (1.0 + ln * (1.0 - sg)))
        dxh = dln * g_ref[...]
        dh = rstd * (dxh - jnp.mean(dxh, axis=-1, keepdims=True) - xh * jnp.mean(dxh * xh, axis=-1, keepdims=True))
        dhc_ref[...] = dh
        dg_ref[...] += jnp.sum(dln * xh, axis=0, keepdims=True)
        dbeta_ref[...] += jnp.sum(dln, axis=0, keepdims=True)
        dbias_ref[...] += jnp.sum(dh, axis=0, keepdims=True)

    vec = jax.ShapeDtypeStruct((1, D), F32)
    return pl.pallas_call(
        body, out_shape=[jax.ShapeDtypeStruct((t, D), F32), vec, vec, vec], grid=(t // tm,),
        in_specs=[_row_spec(tm, D), _row_spec(tm, D), _vec_spec(D), _vec_spec(D)],
        out_specs=[_row_spec(tm, D), _vec_spec(D), _vec_spec(D), _vec_spec(D)],
        name=name, compiler_params=_params(("arbitrary",)),
    )(hc, ds, ln_g, ln_b)


def _conv_bwd(name, zu, dhc, conv_w, tm=256):
    t = zu.shape[0]
    per = tm // HALO
    steps = t // tm

    group = 4

    def body(a_ref, gl_ref, ah_ref, glh_ref, d_ref, dn_ref, w_ref, dz_ref, dw_ref, ext_ref, sh_ref, dext_ref, dsh_ref,
             sg_ref, part_ref):
        i = pl.program_id(0)

        @pl.when(i == 0)
        def _():
            part_ref[...] = jnp.zeros_like(part_ref)

        sg_ref[...] = _sigmoid(gl_ref[...])
        ext_ref[0:HALO, :] = jnp.where(i > 0, ah_ref[...] * _sigmoid(glh_ref[...]), 0.0)
        ext_ref[HALO:, :] = a_ref[...] * sg_ref[...]
        dext_ref[0:tm, :] = d_ref[...]
        dext_ref[tm:, :] = jnp.where(i < steps - 1, dn_ref[...], 0.0)
        _shifted_copies(ext_ref, sh_ref, tm)
        _shifted_copies(dext_ref, dsh_ref, tm)

        def chunk(r, carry):
            r0 = pl.multiple_of(r * CONV_CHUNK, CONV_CHUNK)
            rows = pl.ds(r0, CONV_CHUNK)
            part = jnp.zeros((CONV_CHUNK, D), F32)
            for kk in range(CONV_K):
                part = part + w_ref[kk:kk + 1, :] * _window(dext_ref, dsh_ref, CONV_K - 1 - kk, r0, CONV_CHUNK)
            sg = sg_ref[rows, :]
            dz_ref[rows, 0:D] = (part * sg).astype(BF16)
            dz_ref[rows, D:] = (part * a_ref[rows, :] * sg * (1.0 - sg)).astype(BF16)
            return carry

        lax.fori_loop(0, tm // CONV_CHUNK, chunk, 0)

        for k0 in range(0, CONV_K, group):
            taps = range(k0, min(k0 + group, CONV_K))

            def tile(r, parts, taps=taps):
                r0 = pl.multiple_of(r * CONV_CHUNK, CONV_CHUNK)
                dv = d_ref[pl.ds(r0, CONV_CHUNK), :]
                out = []
                for p, kk in zip(parts, taps):
                    prod = dv * _window(ext_ref, sh_ref, HALO - CONV_K + 1 + kk, r0, CONV_CHUNK)
                    for s in range(0, CONV_CHUNK, SUBLANES):
                        p = p + prod[s:s + SUBLANES, :]
                    out.append(p)
                return tuple(out)

            parts = lax.fori_loop(0, tm // CONV_CHUNK, tile, tuple(jnp.zeros((SUBLANES, D), F32) for _ in taps))
            for p, kk in zip(parts, taps):
                part_ref[kk * SUBLANES:(kk + 1) * SUBLANES, :] += p

        @pl.when(i == steps - 1)
        def _():
            for kk in range(HALO):
                dw_ref[kk:kk + 1, :] = jnp.sum(part_ref[kk * SUBLANES:(kk + 1) * SUBLANES, :], axis=0, keepdims=True)

    halo_map = lambda col: (lambda i: (jnp.maximum(i * per - 1, 0), col))
    shifted = pltpu.VMEM((SUBLANES - 1, tm + SHIFT_ROWS, D), F32)
    return pl.pallas_call(
        body, out_shape=[jax.ShapeDtypeStruct((t, 2 * D), BF16), jax.ShapeDtypeStruct((HALO, D), F32)],
        grid=(steps,),
        in_specs=[_row_spec(tm, D, 0), _row_spec(tm, D, 1), pl.BlockSpec((HALO, D), halo_map(0)),
                  pl.BlockSpec((HALO, D), halo_map(1)), _row_spec(tm, D),
                  pl.BlockSpec((HALO, D), lambda i: (jnp.minimum((i + 1) * per, t // HALO - 1), 0)),
                  pl.BlockSpec((HALO, D), lambda i: (0, 0))],
        out_specs=[_row_spec(tm, 2 * D), pl.BlockSpec((HALO, D), lambda i: (0, 0))],
        scratch_shapes=[pltpu.VMEM((tm + HALO, D), F32), shifted, pltpu.VMEM((tm + HALO, D), F32), shifted,
                        pltpu.VMEM((tm, D), F32), pltpu.VMEM((HALO * SUBLANES, D), F32)],
        name=name, compiler_params=_params(("arbitrary",)),
    )(zu, zu, zu, zu, dhc, dhc, conv_w)


NT = (((1,), (1,)), ((), ()))


def _resident(w, at):
    block = (None,) * (1 + len(at)) + tuple(w.shape[-2:])
    return [pl.BlockSpec(block, functools.partial(lambda i, k: (k, *at, 0, 0), k=k), pipeline_mode=pl.Buffered(1))
            for k in range(SHARDS)]


def _ffn_fwd(tag, x, norm, w, at, gate, tm=512):
    t = x.shape[0]
    slab = min(tm, MM_SLAB)

    def body(*refs):
        x_ref, gain_ref, scale_ref, shift_ref, gate_ref = refs[:5]
        wg, wu, wd = refs[5:5 + SHARDS], refs[5 + SHARDS:5 + 2 * SHARDS], refs[5 + 2 * SHARDS:5 + 3 * SHARDS]
        h_ref, g_ref, u_ref, xn_ref, f_ref = refs[5 + 3 * SHARDS:]
        half_gate = 0.5 * gate_ref[...]
        for r0 in range(0, tm, slab):
            rows = slice(r0, r0 + slab)
            xs = x_ref[rows, :]
            r = lax.rsqrt(jnp.mean(xs * xs, axis=-1, keepdims=True) + EPS)
            hs = ((xs * r) * gain_ref[...] * (1.0 + scale_ref[...]) + shift_ref[...]).astype(BF16)
            h_ref[rows, :] = hs
            tot = None
            for k in range(SHARDS):
                gk = jnp.dot(hs, wg[k][...], preferred_element_type=F32)
                uk = jnp.dot(hs, wu[k][...], preferred_element_type=F32)
                g_ref[k, rows, :] = gk.astype(BF16)
                u_ref[k, rows, :] = uk.astype(BF16)
                ak = ((gk * _sigmoid(gk)) * uk).astype(BF16)
                part = jnp.dot(ak, wd[k][...], preferred_element_type=F32)
                tot = part if tot is None else tot + part
            xn_ref[rows, :] = xs + half_gate * tot
            f_ref[rows, :] = tot.astype(BF16)

    hidden = jax.ShapeDtypeStruct((SHARDS, t, FSH), BF16)
    hidden_spec = pl.BlockSpec((SHARDS, tm, FSH), lambda i: (0, i, 0))
    half = jax.ShapeDtypeStruct((t, D), BF16)
    h, gv, uv, x_new, f = pl.pallas_call(
        body, out_shape=[half, hidden, hidden, jax.ShapeDtypeStruct((t, D), F32), half], grid=(t // tm,),
        in_specs=[_row_spec(tm, D)] + [_vec_spec(D)] * 4 + _resident(w["ffn_wg"], at)
        + _resident(w["ffn_wu"], at) + _resident(w["ffn_wd"], at),
        out_specs=[_row_spec(tm, D), hidden_spec, hidden_spec, _row_spec(tm, D), _row_spec(tm, D)],
        name=f"ffn_fwd_{tag}", compiler_params=_params(("parallel",)),
    )(x, *norm, gate, *([w["ffn_wg"]] * SHARDS), *([w["ffn_wu"]] * SHARDS), *([w["ffn_wd"]] * SHARDS))
    return x_new, h, (gv, uv, f)


def _ffn_hidden_bwd(tag, dx, f, gate, gv, uv, x, gain, scale, w, at, tm=256):
    t = dx.shape[0]
    steps = t // tm
    slab = min(tm, MM_SLAB)

    def body(*refs):
        dx_ref, f_ref, gate_ref, g_ref, u_ref, x_ref, gain_ref, scale_ref = refs[:8]
        wg, wu, wd = refs[8:8 + SHARDS], refs[8 + SHARDS:8 + 2 * SHARDS], refs[8 + 2 * SHARDS:8 + 3 * SHARDS]
        df_ref, dg_ref, du_ref, dxin_ref, dgate_ref, dgain_ref, dscale_ref, dshift_ref = refs[8 + 3 * SHARDS:]
        i = pl.program_id(0)

        @pl.when(i == 0)
        def _():
            dgate_ref[...] = jnp.zeros_like(dgate_ref)
            dgain_ref[...] = jnp.zeros_like(dgain_ref)
            dshift_ref[...] = jnp.zeros_like(dshift_ref)

        half_gate = 0.5 * gate_ref[...]
        norm_w = gain_ref[...] * (1.0 + scale_ref[...])
        for r0 in range(0, tm, slab):
            rows = slice(r0, r0 + slab)
            dxs = dx_ref[rows, :]
            dfs = (half_gate * dxs).astype(BF16)
            df_ref[rows, :] = dfs
            dgate_ref[...] += jnp.sum((0.5 * f_ref[rows, :].astype(F32)) * dxs, axis=0, keepdims=True)
            tot = None
            for k in range(SHARDS):
                da = lax.dot_general(dfs, wd[k][...], NT, preferred_element_type=F32)
                gk, uk = g_ref[k, rows, :].astype(F32), u_ref[k, rows, :].astype(F32)
                sg = _sigmoid(gk)
                dgk = (da * uk * (sg * (1.0 + gk * (1.0 - sg)))).astype(BF16)
                duk = (da * (gk * sg)).astype(BF16)
                dg_ref[k, rows, :] = dgk
                du_ref[k, rows, :] = duk
                part = (lax.dot_general(dgk, wg[k][...], NT, preferred_element_type=F32)
                        + lax.dot_general(duk, wu[k][...], NT, preferred_element_type=F32))
                tot = part if tot is None else tot + part
            xs = x_ref[rows, :]
            r = lax.rsqrt(jnp.mean(xs * xs, axis=-1, keepdims=True) + EPS)
            xh = xs * r
            dxh = tot * norm_w
            dxin_ref[rows, :] = dxs + r * (dxh - xh * jnp.mean(dxh * xh, axis=-1, keepdims=True))
            dgain_ref[...] += jnp.sum(tot * xh, axis=0, keepdims=True)
            dshift_ref[...] += jnp.sum(tot, axis=0, keepdims=True)

        @pl.when(i == steps - 1)
        def _():
            acc = dgain_ref[...]
            dgain_ref[...] = acc * (1.0 + scale_ref[...])
            dscale_ref[...] = acc * gain_ref[...]

    hidden = jax.ShapeDtypeStruct((SHARDS, t, FSH), BF16)
    hidden_spec = pl.BlockSpec((SHARDS, tm, FSH), lambda i: (0, i, 0))
    vec = jax.ShapeDtypeStruct((1, D), F32)
    return pl.pallas_call(
        body, out_shape=[jax.ShapeDtypeStruct((t, D), BF16), hidden, hidden, jax.ShapeDtypeStruct((t, D), F32),
                         vec, vec, vec, vec],
        grid=(steps,),
        in_specs=[_row_spec(tm, D), _row_spec(tm, D), _vec_spec(D), hidden_spec, hidden_spec, _row_spec(tm, D),
                  _vec_spec(D), _vec_spec(D)]
        + _resident(w["ffn_wg"], at) + _resident(w["ffn_wu"], at) + _resident(w["ffn_wd"], at),
        out_specs=[_row_spec(tm, D), hidden_spec, hidden_spec, _row_spec(tm, D)] + [_vec_spec(D)] * 4,
        name=f"ffn_hidden_bwd_{tag}", compiler_params=_params(("arbitrary",)),
    )(dx, f, gate, gv, uv, x, gain, scale, *([w["ffn_wg"]] * SHARDS), *([w["ffn_wu"]] * SHARDS),
      *([w["ffn_wd"]] * SHARDS))


def _ffn_bwd(tag, dx, x, h, saved, w, at, g, scale, gate, into):
    gv, uv, f = saved
    df, dg, du, dx_in, dgate, dgn, dscale, dshift = _ffn_hidden_bwd(tag, dx, f, gate, gv, uv, x, g, scale, w, at)

    def act(blocks):
        gf, uf = blocks[0].astype(F32), blocks[1].astype(F32)
        return ((gf * _sigmoid(gf)) * uf).astype(BF16)

    dwd = _mm_tn(f"ffn_dwd_{tag}", [gv, uv], df, tk=FSH, tn=1024, tt=2048, a_fn=act, a_tiled=True,
                 into=(into["ffn_wd"], at))
    dwg = _mm_tn(f"ffn_dwg_{tag}", h, dg, tk=1024, tn=FSH, tt=2048, b_tiled=True, into=(into["ffn_wg"], at))
    dwu = _mm_tn(f"ffn_dwu_{tag}", h, du, tk=1024, tn=FSH, tt=2048, b_tiled=True, into=(into["ffn_wu"], at))
    return dx_in, dict(ffn_wg=dwg, ffn_wu=dwu, ffn_wd=dwd), dgn, (dshift, dscale, dgate)


def _mix_in_fwd(name, x, g, scale, shift, w_parts, out_dtypes, tm=512):
    t = x.shape[0]
    n = len(w_parts)

    def body(*refs):
        x_ref, g_ref, sc_ref, sh_ref = refs[:4]
        w_refs, h_ref, z_refs = refs[4:4 + n], refs[4 + n], refs[5 + n:]
        for r0 in range(0, tm, MM_SLAB):
            rows = slice(r0, r0 + MM_SLAB)
            xv = x_ref[rows, :]
            r = lax.rsqrt(jnp.mean(xv * xv, axis=-1, keepdims=True) + EPS)
            hv = ((xv * r) * g_ref[...] * (1.0 + sc_ref[...]) + sh_ref[...]).astype(BF16)
            h_ref[rows, :] = hv
            for w_ref, z_ref in zip(w_refs, z_refs):
                z_ref[rows, :] = jnp.dot(hv, w_ref[...], preferred_element_type=F32).astype(z_ref.dtype)

    return pl.pallas_call(
        body, out_shape=[jax.ShapeDtypeStruct((t, D), BF16)]
        + [jax.ShapeDtypeStruct((t, wp.shape[1]), dt) for wp, dt in zip(w_parts, out_dtypes)], grid=(t // tm,),
        in_specs=[_row_spec(tm, D), _vec_spec(D), _vec_spec(D), _vec_spec(D)] + [_whole(wp) for wp in w_parts],
        out_specs=[_row_spec(tm, D)] + [_row_spec(tm, wp.shape[1]) for wp in w_parts],
        name=name, compiler_params=_params(("parallel",)),
    )(x, g, scale, shift, *w_parts)


def _mix_fwd(tag, x, norm, w_in, attn_wo, conv_w, conv_b, ln_g, ln_b, conv_wo, w_out, gate, cos_t, sin_t):
    w_qkv, w_u, w_g = w_in[:, :QKV], w_in[:, QKV:QKV + 2 * D], w_in[:, QKV + 2 * D:]
    h, zqkv, zu, zg = _mix_in_fwd(f"mix_in_{tag}", x, *norm, [w_qkv, w_u, w_g], [F32, F32, BF16])
    qkv = _rope_fwd(f"rope_{tag}", zqkv, cos_t, sin_t)
    n = len(DILATIONS)
    outs, lses = [], []
    for grp in range(n):
        o, lse = _attn_fwd(f"attn_fwd_{tag}_{grp}", qkv[grp], qkv[n + grp], qkv[2 * n + grp], grp)
        outs.append(o)
        lses.append(lse)
    ob, of, lj = _attn_merge(f"attn_merge_{tag}", outs, lses)
    hc, s = _conv_fwd(f"conv_fwd_{tag}", zu, conv_w, conv_b, ln_g, ln_b)
    y, ya, yc, sa, sc, x_new, f = _mix_out_fwd(f"mix_out_{tag}", ob, s, zg, x, gate, attn_wo, conv_wo, w_out)
    return x_new, h, (zu, sa, sc, qkv, ob, of, lj, hc, s, y, ya, yc, f, (w_qkv, w_u, w_g))


def _whole(w):
    return pl.BlockSpec(w.shape, lambda i: (0, 0), pipeline_mode=pl.Buffered(1))


def _mix_out_fwd(name, ob, s, zg, x, gate, attn_wo, conv_wo, w_out, tm=512):
    t = x.shape[0]

    def body(ob_ref, s_ref, za_ref, zc_ref, x_ref, gate_ref, wa_ref, wc_ref, wo_ref,
             y_ref, ya_ref, yc_ref, sa_ref, sc_ref, xn_ref, f_ref):
        for r0 in range(0, tm, MM_SLAB):
            rows = slice(r0, r0 + MM_SLAB)
            ya = jnp.dot(ob_ref[rows, :], wa_ref[...], preferred_element_type=F32)
            yc = jnp.dot(s_ref[rows, :], wc_ref[...], preferred_element_type=F32)
            sa, sc = _sigmoid(za_ref[rows, :].astype(F32)), _sigmoid(zc_ref[rows, :].astype(F32))
            y = (sa * ya + sc * yc).astype(BF16)
            out = jnp.dot(y, wo_ref[...], preferred_element_type=F32)
            y_ref[rows, :], ya_ref[rows, :], yc_ref[rows, :] = y, ya.astype(BF16), yc.astype(BF16)
            sa_ref[rows, :], sc_ref[rows, :] = sa.astype(BF16), sc.astype(BF16)
            xn_ref[rows, :] = x_ref[rows, :] + gate_ref[...] * out
            f_ref[rows, :] = out.astype(BF16)

    half = jax.ShapeDtypeStruct((t, D), BF16)
    return pl.pallas_call(
        body, out_shape=[half] * 5 + [jax.ShapeDtypeStruct((t, D), F32), half], grid=(t // tm,),
        in_specs=[_row_spec(tm, GW), _row_spec(tm, D), _row_spec(tm, D, 0), _row_spec(tm, D, 1), _row_spec(tm, D),
                  _vec_spec(D), _whole(attn_wo), _whole(conv_wo), _whole(w_out)],
        out_specs=[_row_spec(tm, D)] * 7, name=name, compiler_params=_params(("parallel",)),
    )(ob, s, zg, zg, x, gate, attn_wo, conv_wo, w_out)


def _mix_out_bwd(name, dx, f, gate, sa, sc, ya, yc, attn_wo, conv_wo, w_out, tm=512):
    t = dx.shape[0]

    def body(dx_ref, f_ref, gate_ref, sa_ref, sc_ref, ya_ref, yc_ref, wa_ref, wc_ref, wo_ref,
             df_ref, dya_ref, dyc_ref, dzg_ref, do_ref, ds_ref, dgate_ref):
        @pl.when(pl.program_id(0) == 0)
        def _():
            dgate_ref[...] = jnp.zeros_like(dgate_ref)

        for r0 in range(0, tm, MM_SLAB):
            rows = slice(r0, r0 + MM_SLAB)
            dxs = dx_ref[rows, :]
            dfs = (gate_ref[...] * dxs).astype(BF16)
            df_ref[rows, :] = dfs
            dgate_ref[...] += jnp.sum(f_ref[rows, :].astype(F32) * dxs, axis=0, keepdims=True)
            dy = lax.dot_general(dfs, wo_ref[...], NT, preferred_element_type=F32)
            ga, gc = sa_ref[rows, :].astype(F32), sc_ref[rows, :].astype(F32)
            dya, dyc = (dy * ga).astype(BF16), (dy * gc).astype(BF16)
            dya_ref[rows, :], dyc_ref[rows, :] = dya, dyc
            dzg_ref[rows, 0:D] = (dy * ya_ref[rows, :].astype(F32) * (ga * (1.0 - ga))).astype(BF16)
            dzg_ref[rows, D:] = (dy * yc_ref[rows, :].astype(F32) * (gc * (1.0 - gc))).astype(BF16)
            do_ref[rows, :] = lax.dot_general(dya, wa_ref[...], NT, preferred_element_type=F32)
            ds_ref[rows, :] = lax.dot_general(dyc, wc_ref[...], NT, preferred_element_type=F32)

    half = jax.ShapeDtypeStruct((t, D), BF16)
    return pl.pallas_call(
        body, out_shape=[half, half, half, jax.ShapeDtypeStruct((t, 2 * D), BF16), jax.ShapeDtypeStruct((t, GW), F32),
                         jax.ShapeDtypeStruct((t, D), F32), jax.ShapeDtypeStruct((1, D), F32)], grid=(t // tm,),
        in_specs=[_row_spec(tm, D), _row_spec(tm, D), _vec_spec(D)] + [_row_spec(tm, D)] * 4
        + [_whole(attn_wo), _whole(conv_wo), _whole(w_out)],
        out_specs=[_row_spec(tm, D)] * 3 + [_row_spec(tm, 2 * D), _row_spec(tm, GW), _row_spec(tm, D), _vec_spec(D)],
        name=name, compiler_params=_params(("arbitrary",)),
    )(dx, f, gate, sa, sc, ya, yc, attn_wo, conv_wo, w_out)


def _mix_bwd(tag, dx, x, h, saved, attn_wo, conv_w, ln_g, ln_b, conv_wo, w_out, g, scale, gate, cos_t, sin_t):
    zu, sa, sc, qkv, ob, of, lj, hc, s, y, ya, yc, f, w_parts = saved
    n = len(DILATIONS)
    df, dya, dyc, dzg, do, ds, dgate = _mix_out_bwd(f"mix_out_bwd_{tag}", dx, f, gate, sa, sc, ya, yc, attn_wo,
                                                    conv_wo, w_out)
    dw_out = _mm_tn(f"mix_dwout_{tag}", y, df, tk=1024, tn=1024, tt=2048)
    dw_attn = _mm_tn(f"mix_dwattn_{tag}", ob, dya, tk=GW, tn=1024, tt=2048)
    dw_conv_o = _mm_tn(f"mix_dwconvo_{tag}", s, dyc, tk=1024, tn=1024, tt=2048)

    prep = _attn_bwd_prep(f"attn_prep_{tag}", do, of, lj)
    dqs, dks, dvs = [], [], []
    for grp in range(n):
        dq, dk, dv = _attn_bwd(f"attn_bwd_{tag}_{grp}", qkv[grp], qkv[n + grp], qkv[2 * n + grp], prep[n + grp],
                               prep[2 * n + grp], prep[grp], grp)
        dqs.append(dq)
        dks.append(dk)
        dvs.append(dv)
    dzqkv = _rope_bwd(f"rope_bwd_{tag}", dqs + dks + dvs, cos_t, sin_t)

    dhc, dln_g, dln_b, dconv_b = _conv_ln_bwd(f"conv_ln_bwd_{tag}", hc, ds, ln_g, ln_b)
    dzu, dconv_w = _conv_bwd(f"conv_bwd_{tag}", zu, dhc, conv_w)

    dz_parts = [dzqkv, dzu, dzg]
    dw_in = jnp.concatenate(
        [_mm_tn(f"mix_dwin_{tag}_{i}", h, dzp, tk=1024, tn=dzp.shape[1] // 2, tt=2048)
         for i, dzp in enumerate(dz_parts)], axis=1)
    dx_in, dgn, dscale, dshift = _proj_norm_bwd(f"mix_dh_{tag}", dz_parts, list(w_parts), x, dx, g, scale)
    grads = dict(w_in=dw_in, attn_wo=dw_attn, conv_w=dconv_w[:CONV_K], conv_b=dconv_b, conv_ln_g=dln_g,
                 conv_ln_b=dln_b, conv_wo=dw_conv_o, w_out=dw_out)
    return dx_in, grads, dgn, (dshift, dscale, dgate)


def _local_step(x, mod, target, w, wf):
    t = x.shape[0]
    cos_t, sin_t = _rope_tables(t)
    row = lambda v: v.reshape(1, -1)
    conv_w_pad = jnp.concatenate([wf["conv_w"], jnp.zeros((DEPTH, HALO - CONV_K, D), F32)], axis=1)

    saved = []
    for l in range(DEPTH):
        mods = [mod[l:l + 1, i * D:(i + 1) * D] for i in range(N_MOD)]
        gains = [row(wf["norm_g"][l, i]) for i in range(3)]
        lay = dict(mods=mods, gains=gains)

        lay["x0"] = x
        x, lay["h0"], lay["ffn0"] = _ffn_fwd(f"a_{l}", x, (gains[0], mods[1], mods[0]), w, (l, 0), mods[2])
        lay["x1"] = x
        x, lay["h1"], lay["mix"] = _mix_fwd(f"{l}", x, (gains[1], mods[4], mods[3]), w["w_in"][l], w["attn_wo"][l],
                                            conv_w_pad[l],
                                 row(wf["conv_b"][l]), row(wf["conv_ln_g"][l]), row(wf["conv_ln_b"][l]),
                                 w["conv_wo"][l], w["w_out"][l], mods[5], cos_t, sin_t)
        lay["x2"] = x
        x, lay["h2"], lay["ffn1"] = _ffn_fwd(f"b_{l}", x, (gains[2], mods[7], mods[6]), w, (l, 1), mods[8])
        saved.append(lay)

    dx, dfinal_g, loss_cols = _loss_bwd("loss_head", x, target, row(wf["final_g"]))

    ffn_grads = {n: jnp.zeros(w[n].shape, F32) for n in ("ffn_wg", "ffn_wu", "ffn_wd")}
    per_layer = []
    for l in reversed(range(DEPTH)):
        lay = saved[l]
        mods, gains = lay["mods"], lay["gains"]
        dx, ffn_grads, dgn2, dmod2 = _ffn_bwd(f"b_{l}", dx, lay["x2"], lay["h2"], lay["ffn1"], w, (l, 1),
                                              gains[2], mods[7], mods[8], ffn_grads)
        dx, gm, dgn1, dmod1 = _mix_bwd(f"{l}", dx, lay["x1"], lay["h1"], lay["mix"], w["attn_wo"][l],
                                       conv_w_pad[l], row(wf["conv_ln_g"][l]), row(wf["conv_ln_b"][l]),
                                       w["conv_wo"][l], w["w_out"][l], gains[1], mods[4], mods[5], cos_t, sin_t)
        dx, ffn_grads, dgn0, dmod0 = _ffn_bwd(f"a_{l}", dx, lay["x0"], lay["h0"], lay["ffn0"], w, (l, 0),
                                              gains[0], mods[1], mods[2], ffn_grads)
        g = dict(gm)
        g["dmod"] = jnp.concatenate(list(dmod0) + list(dmod1) + list(dmod2), axis=1)
        g["norm_g"] = [dgn0[0], dgn1[0], dgn2[0]]
        for name in ("conv_b", "conv_ln_g", "conv_ln_b"):
            g[name] = g[name][0]
        per_layer.append(g)
    per_layer.reverse()
    grads = {name: [per_layer[l][name] for l in range(DEPTH)] for name in per_layer[0]}
    grads["dmod"] = jnp.concatenate(grads["dmod"], axis=0)
    grads.update(ffn_grads)
    grads["final_g"] = dfinal_g[0]
    return loss_cols, dx, grads


def _split_bits(w):
    bits = lax.bitcast_convert_type(w, jnp.uint32)
    hi = lax.bitcast_convert_type((bits >> 16).astype(jnp.uint16), BF16)
    lo = lax.bitcast_convert_type((bits & 0xFFFF).astype(jnp.uint16), BF16)
    return hi, lo


def _join_bits(hi, lo):
    h = lax.bitcast_convert_type(hi, jnp.uint16).astype(jnp.uint32)
    l = lax.bitcast_convert_type(lo, jnp.uint16).astype(jnp.uint32)
    return lax.bitcast_convert_type((h << 16) | l, F32)


def _pack(parts, rows):
    out = []
    for p in parts:
        flat = p.reshape(-1)
        pad = -flat.shape[0] % LANES
        out.append(jnp.concatenate([flat, jnp.zeros((pad,), flat.dtype)]) if pad else flat)
    flat = jnp.concatenate(out)
    return jnp.concatenate([flat, jnp.zeros((rows * LANES - flat.shape[0],), flat.dtype)]).reshape(rows, LANES)


def _unpack(buf, shapes):
    out, row = [], 0
    for shape in shapes:
        size = 1
        for s in shape:
            size *= s
        rows = -(-size // LANES)
        out.append(buf[row:row + rows].reshape(-1)[:size].reshape(shape))
        row += rows
    return out


def _place():
    x, y, c = lax.axis_index("x"), lax.axis_index("y"), lax.axis_index("c")
    chips = [(1 - x, y), (x, 1 - y), (1 - x, 1 - y)]
    return x, y, c, chips


def _chip_index():
    return (2 * lax.axis_index("x") + lax.axis_index("y")).astype(jnp.int32)


HBM_SPEC = pl.BlockSpec(memory_space=pltpu.HBM)


def _gather_rows(name, block):
    m, n = block.shape

    def body(x_ref, out_ref, send_sems, recv_sems, local_sem):
        x, y, c, chips = _place()
        me, sibling = (x, y, c), (x, y, 1 - c)

        def rows(px, py, pc):
            return out_ref.at[pl.ds((4 * px + 2 * py + pc) * m, m), :]

        def copy(k, owner, to, src=None):
            return pltpu.make_async_remote_copy(
                src_ref=rows(*owner) if src is None else src, dst_ref=rows(*owner), send_sem=send_sems.at[k],
                recv_sem=recv_sems.at[k], device_id=to, device_id_type=MESH)

        mine = pltpu.make_async_copy(x_ref, rows(*me), local_sem)
        mine.start()
        first = [copy(0, me, sibling, src=x_ref)] + [copy(1 + j, me, (*chip, c), src=x_ref)
                                                     for j, chip in enumerate(chips)]
        for cp in first:
            cp.start()
        passed = [copy(4 + j, (*chip, c), sibling) for j, chip in enumerate(chips)]
        for j, chip in enumerate(chips):
            copy(1 + j, (*chip, c), me).wait_recv()
            passed[j].start()
        copy(0, sibling, me).wait_recv()
        for j, chip in enumerate(chips):
            copy(4 + j, (*chip, 1 - c), me).wait_recv()
        for cp in first + passed:
            cp.wait_send()
        mine.wait()

    whole = pl.BlockSpec(memory_space=pltpu.VMEM)
    return pl.pallas_call(
        body, out_shape=jax.ShapeDtypeStruct((N_DEV * m, n), block.dtype), in_specs=[whole], out_specs=whole,
        scratch_shapes=[pltpu.SemaphoreType.DMA((7,)), pltpu.SemaphoreType.DMA((7,)), pltpu.SemaphoreType.DMA],
        name=name,
    )(block)


def _gather_weights(arrays):
    n = len(arrays)

    def body(*refs):
        outs, send_sems, recv_sems = refs[n:2 * n], refs[2 * n], refs[2 * n + 1]
        x, y, c, chips = _place()
        me = 2 * x + y
        sibling = (x, y, 1 - c)
        there = [2 * chip[0] + chip[1] for chip in chips]

        def copy(a, k, chip, layer, to):
            piece = outs[a].at[chip, layer]
            return pltpu.make_async_remote_copy(
                src_ref=piece, dst_ref=piece, send_sem=send_sems.at[6 * a + k], recv_sem=recv_sems.at[6 * a + k],
                device_id=to, device_id_type=MESH)

        first = [copy(a, j, me, c, (*chip, c)) for a in range(n) for j, chip in enumerate(chips)]
        for cp in first:
            cp.start()
        passed = []
        for a in range(n):
            for j in range(3):
                copy(a, j, there[j], c, sibling).wait_recv()
                passed.append(copy(a, 3 + j, there[j], c, sibling))
                passed[-1].start()
        for a in range(n):
            for j in range(3):
                copy(a, 3 + j, there[j], 1 - c, sibling).wait_recv()
        for cp in first + passed:
            cp.wait_send()

    return pl.pallas_call(
        body, out_shape=[jax.ShapeDtypeStruct(a.shape, a.dtype) for a in arrays],
        in_specs=[HBM_SPEC] * n, out_specs=[HBM_SPEC] * n,
        scratch_shapes=[pltpu.SemaphoreType.DMA((6 * n,)), pltpu.SemaphoreType.DMA((6 * n,))],
        input_output_aliases={i: i for i in range(n)}, name="gather_weights",
    )(*arrays)


def _row_block(rows, cols):
    for cand in (512, 256, 128, 64, 32, 16):
        if rows % cand == 0 and cand * cols * 4 <= 2560 * 1024:
            return cand
    return rows


def _swap_layers(grads):
    n = len(grads)

    def body(*refs):
        g_refs, out_refs, send_sems, recv_sems = refs[:n], refs[n:2 * n], refs[2 * n], refs[2 * n + 1]
        x, y, c, _ = _place()
        copies = [pltpu.make_async_remote_copy(
            src_ref=g_refs[a].at[:, 1 - c], dst_ref=out_refs[a], send_sem=send_sems.at[a], recv_sem=recv_sems.at[a],
            device_id=(x, y, 1 - c), device_id_type=MESH) for a in range(n)]
        for cp in copies:
            cp.start()
        for cp in copies:
            cp.wait()

    return pl.pallas_call(
        body, out_shape=[jax.ShapeDtypeStruct((g.shape[0],) + g.shape[2:], F32) for g in grads],
        in_specs=[HBM_SPEC] * n, out_specs=[HBM_SPEC] * n,
        scratch_shapes=[pltpu.SemaphoreType.DMA((n,)), pltpu.SemaphoreType.DMA((n,))], name="swap_layers",
    )(*grads)


def _add_layers(name, grad, other):
    shards, _, rows, cols = grad.shape
    tr = _row_block(rows, cols)

    def body(c_ref, g_ref, o_ref, out_ref):
        out_ref[...] = (g_ref[...] + o_ref[...]).astype(BF16)

    c = lax.axis_index("c").astype(jnp.int32).reshape(1)
    grid_spec = pltpu.PrefetchScalarGridSpec(
        num_scalar_prefetch=1, grid=(shards, rows // tr),
        in_specs=[pl.BlockSpec((None, None, tr, cols), lambda k, i, c_ref: (k, c_ref[0], i, 0)),
                  pl.BlockSpec((None, tr, cols), lambda k, i, c_ref: (k, i, 0))],
        out_specs=pl.BlockSpec((None, tr, cols), lambda k, i, c_ref: (k, i, 0)))
    return pl.pallas_call(
        body, out_shape=jax.ShapeDtypeStruct((shards, rows, cols), BF16), grid_spec=grid_spec,
        name=name, compiler_params=_params(("parallel", "parallel")),
    )(c, grad, other)


def _scatter_chips(parts):
    n = len(parts)

    def body(*refs):
        p_refs, out_refs, send_sems, recv_sems = refs[:n], refs[n:2 * n], refs[2 * n], refs[2 * n + 1]
        x, y, c, chips = _place()
        me = 2 * x + y
        there = [2 * chip[0] + chip[1] for chip in chips]

        def copy(a, j, slot):
            return pltpu.make_async_remote_copy(
                src_ref=p_refs[a].at[there[j]], dst_ref=out_refs[a].at[slot], send_sem=send_sems.at[3 * a + j],
                recv_sem=recv_sems.at[3 * a + j], device_id=(*chips[j], c), device_id_type=MESH)

        sends = [copy(a, j, me) for a in range(n) for j in range(3)]
        for cp in sends:
            cp.start()
        for a in range(n):
            for j in range(3):
                copy(a, j, there[j]).wait_recv()
        for cp in sends:
            cp.wait_send()

    return pl.pallas_call(
        body, out_shape=[jax.ShapeDtypeStruct(p.shape, p.dtype) for p in parts],
        in_specs=[HBM_SPEC] * n, out_specs=[HBM_SPEC] * n,
        scratch_shapes=[pltpu.SemaphoreType.DMA((3 * n,)), pltpu.SemaphoreType.DMA((3 * n,))],
        name="scatter_chips",
    )(*parts)


def _add_chips(name, part, others):
    shards, rows, cols = part.shape
    tr = _row_block(rows, cols)

    def body(pos_ref, own_ref, r0_ref, r1_ref, r2_ref, r3_ref, out_ref):
        me = pos_ref[0]
        own = own_ref[...].astype(F32)
        total = None
        for k, r_ref in enumerate((r0_ref, r1_ref, r2_ref, r3_ref)):
            term = jnp.where(me == k, own, r_ref[...].astype(F32))
            total = term if total is None else total + term
        out_ref[...] = total

    def other(k):
        return pl.BlockSpec((None, tr, cols),
                            lambda i, pos, k=k: (jnp.where(pos[0] == k, (k + 1) % shards, k), i, 0))

    pos = jnp.stack([_chip_index(), lax.axis_index("c").astype(jnp.int32)])
    grid_spec = pltpu.PrefetchScalarGridSpec(
        num_scalar_prefetch=1, grid=(rows // tr,),
        in_specs=[pl.BlockSpec((None, tr, cols), lambda i, pos: (pos[0], i, 0))] + [other(k) for k in range(shards)],
        out_specs=pl.BlockSpec((None, tr, cols), lambda i, pos: (pos[1], i, 0)))
    return pl.pallas_call(
        body, out_shape=jax.ShapeDtypeStruct((DEPTH, rows, cols), F32), grid_spec=grid_spec,
        name=name, compiler_params=_params(("parallel",)),
    )(pos, part, others, others, others, others)


def _join_layers(arrays):
    n = len(arrays)

    def body(*refs):
        outs, send_sems, recv_sems = refs[n:2 * n], refs[2 * n], refs[2 * n + 1]
        x, y, c, _ = _place()

        def copy(a, layer):
            piece = outs[a].at[layer]
            return pltpu.make_async_remote_copy(src_ref=piece, dst_ref=piece, send_sem=send_sems.at[a],
                                                recv_sem=recv_sems.at[a], device_id=(x, y, 1 - c),
                                                device_id_type=MESH)

        sends = [copy(a, c) for a in range(n)]
        for cp in sends:
            cp.start()
        for a in range(n):
            copy(a, 1 - c).wait_recv()
        for cp in sends:
            cp.wait_send()

    return pl.pallas_call(
        body, out_shape=[jax.ShapeDtypeStruct(a.shape, a.dtype) for a in arrays],
        in_specs=[HBM_SPEC] * n, out_specs=[HBM_SPEC] * n,
        scratch_shapes=[pltpu.SemaphoreType.DMA((n,)), pltpu.SemaphoreType.DMA((n,))],
        input_output_aliases={i: i for i in range(n)}, name="join_layers",
    )(*arrays)


def _reduce_scatter(grads):
    sums = [_add_layers(f"add_layers_{a}", g, o) for a, (g, o) in enumerate(zip(grads, _swap_layers(grads)))]
    others = _scatter_chips(sums)
    return _join_layers([_add_chips(f"add_chips_{a}", p, o) for a, (p, o) in enumerate(zip(sums, others))])


def _adamw(name, w, g, m, v):
    shape = w.shape
    cols = shape[-1]
    rows = w.size // cols
    tr = rows
    for cand in (512, 256, 128, 64, 32, 16, 8):
        if rows % cand == 0 and cand * cols * 4 <= 4 * 1024 * 1024:
            tr = cand
            break

    def body(w_ref, g_ref, m_ref, v_ref, d_ref, nm_ref, nv_ref):
        gv = g_ref[...]
        nm = ADAM_B1 * m_ref[...] + (1.0 - ADAM_B1) * gv
        nv = ADAM_B2 * v_ref[...] + (1.0 - ADAM_B2) * (gv * gv)
        m_hat = nm / (1.0 - ADAM_B1 ** ADAM_STEP)
        v_hat = nv / (1.0 - ADAM_B2 ** ADAM_STEP)
        d_ref[...] = -ADAM_LR * (m_hat / (jnp.sqrt(v_hat) + ADAM_EPS) + ADAM_WD * w_ref[...])
        nm_ref[...] = nm
        nv_ref[...] = nv

    spec = pl.BlockSpec((tr, cols), lambda i: (i, 0))
    two = lambda a: a.reshape(rows, cols)
    outs = pl.pallas_call(
        body, out_shape=[jax.ShapeDtypeStruct((rows, cols), F32)] * 3, grid=(rows // tr,),
        in_specs=[spec] * 4, out_specs=[spec] * 3, name=name, compiler_params=_params(("parallel",)),
    )(two(w), two(g), two(m), two(v))
    return [o.reshape(shape) for o in outs]


BIG = ("ffn_wg", "ffn_wu", "ffn_wd", "w_in", "conv_wo", "w_out")
MISC_ROWS = 96


def _own_slot(shard):
    return lax.dynamic_update_slice(jnp.zeros((SHARDS,) + shard.shape, shard.dtype), shard[None],
                                    (_chip_index(),) + (0,) * shard.ndim)


def _as_matrices(a):
    return a.reshape(a.shape[0], a.shape[1], -1, a.shape[-1])


def kernel(x, c, ada_w, ada_b, norm_g, ffn_wg, ffn_wu, ffn_wd, w_in, attn_wo, conv_w, conv_b, conv_ln_g, conv_ln_b, conv_wo, w_out, final_g, loss_target, m_ada_w, m_ada_b, m_norm_g, m_ffn_wg, m_ffn_wu, m_ffn_wd, m_w_in, m_attn_wo, m_conv_w, m_conv_b, m_conv_ln_g, m_conv_ln_b, m_conv_wo, m_w_out, m_final_g, v_ada_w, v_ada_b, v_norm_g, v_ffn_wg, v_ffn_wu, v_ffn_wd, v_w_in, v_attn_wo, v_conv_w, v_conv_b, v_conv_ln_g, v_conv_ln_b, v_conv_wo, v_w_out, v_final_g):
    weights = dict(ada_w=ada_w, ada_b=ada_b, norm_g=norm_g, ffn_wg=ffn_wg, ffn_wu=ffn_wu, ffn_wd=ffn_wd, w_in=w_in,
                   attn_wo=attn_wo, conv_w=conv_w, conv_b=conv_b, conv_ln_g=conv_ln_g, conv_ln_b=conv_ln_b,
                   conv_wo=conv_wo, w_out=w_out, final_g=final_g)
    moments_m = dict(ada_w=m_ada_w, ada_b=m_ada_b, norm_g=m_norm_g, ffn_wg=m_ffn_wg, ffn_wu=m_ffn_wu,
                     ffn_wd=m_ffn_wd, w_in=m_w_in, attn_wo=m_attn_wo, conv_w=m_conv_w, conv_b=m_conv_b,
                     conv_ln_g=m_conv_ln_g, conv_ln_b=m_conv_ln_b, conv_wo=m_conv_wo, w_out=m_w_out,
                     final_g=m_final_g)
    moments_v = dict(ada_w=v_ada_w, ada_b=v_ada_b, norm_g=v_norm_g, ffn_wg=v_ffn_wg, ffn_wu=v_ffn_wu,
                     ffn_wd=v_ffn_wd, w_in=v_w_in, attn_wo=v_attn_wo, conv_w=v_conv_w, conv_b=v_conv_b,
                     conv_ln_g=v_conv_ln_g, conv_ln_b=v_conv_ln_b, conv_wo=v_conv_wo, w_out=v_w_out,
                     final_g=v_final_g)
    layers, shards = range(DEPTH), range(SHARDS)

    bits = {n: _split_bits(weights[n]) for n in EXACT}
    misc_w = jnp.stack([_pack([attn_wo[l].astype(BF16), bits["norm_g"][0][l], bits["norm_g"][1][l],
                               bits["conv_w"][0][l], bits["conv_w"][1][l]], MISC_ROWS) for l in layers])
    sent = [_own_slot(weights[n].astype(BF16)) for n in BIG] + [_own_slot(misc_w)]
    got = dict(zip(BIG + ("misc",), _gather_weights(sent)))
    w = {n: got[n] for n in ("ffn_wg", "ffn_wu", "ffn_wd")}
    w["w_in"] = got["w_in"].transpose(1, 2, 0, 3).reshape(DEPTH, D, -1)
    for n in ("conv_wo", "w_out"):
        w[n] = got[n].transpose(1, 0, 2, 3).reshape(DEPTH, D, D)
    misc_shapes = [(GW, GW), (3, GW), (3, GW), (CONV_K, GW), (CONV_K, GW)]
    pieces = [[_unpack(got["misc"][k, l], misc_shapes) for k in shards] for l in layers]
    whole = lambda i: jnp.stack([jnp.concatenate([pieces[l][k][i] for k in shards], axis=1) for l in layers])
    w["attn_wo"] = whole(0)
    vectors = dict(ada_b=ada_b, conv_b=conv_b, conv_ln_g=conv_ln_g, conv_ln_b=conv_ln_b, final_g=final_g,
                   norm_g=_join_bits(whole(1), whole(2)), conv_w=_join_bits(whole(3), whole(4)))

    me = 2 * _chip_index() + lax.axis_index("c").astype(jnp.int32)
    pad_rows = lambda a, rows: jnp.concatenate([a, jnp.zeros((rows - a.shape[0], a.shape[1]), a.dtype)])
    c_all = _gather_rows("gather_c", pad_rows(c, SUBLANES)).reshape(N_DEV, SUBLANES, D)[:, 0]
    mod_cols = _mod_fwd("mod_fwd", c_all, ada_w.astype(BF16))
    by_dev = _gather_rows("gather_mod", mod_cols).reshape(N_DEV, N_DEV, DEPTH, -1)
    mine = lax.dynamic_index_in_dim(by_dev[0::2], me, axis=1, keepdims=False)
    mod = mine.transpose(1, 0, 2).reshape(DEPTH, -1) + ada_b

    loss_cols, dx, grads = _local_step(x[0], mod, loss_target[0], w, vectors)
    loss = lax.psum(jnp.sum(loss_cols), ("x", "y", "c"))

    dmod_rows = DEPTH * N_MOD * D // LANES
    dmod_all = _gather_rows("gather_dmod", pad_rows(grads["dmod"].reshape(dmod_rows, LANES), 3 * SUBLANES))
    dmod_all = dmod_all.reshape(N_DEV, 3 * SUBLANES, LANES)[:, :dmod_rows].reshape(N_DEV, DEPTH, -1)
    grad_ada_b = _sum_devices("ada_b_grad", dmod_all.reshape(N_DEV, -1)).reshape(DEPTH, -1)
    cols = ada_w.shape[-1]
    dmod_cols = lax.dynamic_slice_in_dim(dmod_all, _chip_index() * cols, cols, axis=2).transpose(1, 0, 2)
    grad_ada_w = _mod_bwd("mod_bwd", c_all.T, dmod_cols)

    cols_of = lambda a, k: a[..., k * GW:(k + 1) * GW]
    misc_g = jnp.stack([jnp.stack([_pack(
        [cols_of(grads["attn_wo"][l], k), cols_of(jnp.stack(grads["norm_g"][l]), k), cols_of(grads["conv_w"][l], k),
         grads["conv_b"][l], grads["conv_ln_g"][l], grads["conv_ln_b"][l],
         grads["final_g"] if l == 0 else jnp.zeros_like(grads["final_g"])], MISC_ROWS)
        for l in layers]) for k in shards])
    by_chip = dict(
        ffn_wg=grads["ffn_wg"], ffn_wu=grads["ffn_wu"], ffn_wd=grads["ffn_wd"],
        w_in=jnp.stack(grads["w_in"]).reshape(DEPTH, D, SHARDS, -1).transpose(2, 0, 1, 3),
        conv_wo=jnp.stack(grads["conv_wo"]).reshape(DEPTH, SHARDS, -1, D).transpose(1, 0, 2, 3),
        w_out=jnp.stack(grads["w_out"]).reshape(DEPTH, SHARDS, -1, D).transpose(1, 0, 2, 3))
    reduced = _reduce_scatter([_as_matrices(by_chip[n]) for n in BIG] + [misc_g])
    summed = {n: r.reshape(weights[n].shape) for n, r in zip(BIG, reduced)}
    small_shapes = [(GW, GW), (3, GW), (CONV_K, GW), (D,), (D,), (D,), (D,)]
    small = [_unpack(reduced[-1][l], small_shapes) for l in layers]
    for i, n in enumerate(("attn_wo", "norm_g", "conv_w", "conv_b", "conv_ln_g", "conv_ln_b")):
        summed[n] = jnp.stack([small[l][i] for l in layers])
    summed["final_g"] = small[0][6]
    summed["ada_w"], summed["ada_b"] = grad_ada_w, grad_ada_b

    deltas, new_m, new_v = {}, {}, {}
    for n in WEIGHTS:
        deltas[n], new_m[n], new_v[n] = _adamw(f"adamw_{n}", weights[n], summed[n], moments_m[n], moments_v[n])

    return (loss, dx[None], *[summed[n] for n in WEIGHTS], *[deltas[n] for n in WEIGHTS],
            *[new_m[n] for n in WEIGHTS], *[new_v[n] for n in WEIGHTS])
```

```python
import functools

import jax
import jax.numpy as jnp
from jax import lax
from jax.experimental import pallas as pl
from jax.experimental.pallas import tpu as pltpu

F32 = jnp.float32
BF16 = jnp.bfloat16

D = 1024
DFF = 2816
HEAD = 64
GW = 256
DILATIONS = (1, 4, 16)
BAND = 128
QKV = 2304
CONV_K = 31
HALO = 32
N_MOD = 9
EPS = 1e-6
NEG_INF = -1e30
DEPTH = 2

SHARDS = 4
FSH = DFF // SHARDS
LANES = 1024
VL = 128

ADAM_LR = 0.001
ADAM_B1 = 0.9
ADAM_B2 = 0.999
ADAM_EPS = 1e-08
ADAM_WD = 0.01
ADAM_STEP = 10

VMEM_LIMIT = 56 * 1024 * 1024

EXACT = ("norm_g", "conv_w")
WEIGHTS = ("ada_w", "ada_b", "norm_g", "ffn_wg", "ffn_wu", "ffn_wd", "w_in", "attn_wo", "conv_w", "conv_b",
           "conv_ln_g", "conv_ln_b", "conv_wo", "w_out", "final_g")

MESH = pl.DeviceIdType.MESH


def _params(sem=None):
    return pltpu.CompilerParams(dimension_semantics=sem, vmem_limit_bytes=VMEM_LIMIT)


def _sigmoid(v):
    return jax.nn.sigmoid(v)


MM_SLAB = 256


def _mm_tn(name, a, b, *, tk, tn, tt, a_fn=None, a_tiled=False, b_tiled=False, into=None):
    a_list = list(a) if a_fn is not None else [a]
    na = len(a_list)
    t = a_list[0].shape[-2]
    nk = a_list[0].shape[0] if a_tiled else a_list[0].shape[1] // tk
    nn = b.shape[0] if b_tiled else b.shape[1] // tn
    steps = t // tt
    has_into = into is not None

    def body(*refs):
        refs = refs[1:] if has_into else refs
        a_refs, b_ref, o_ref, acc_ref = refs[:na], refs[na], refs[na + 1], refs[na + 2]
        s = pl.program_id(2)

        @pl.when(s == 0)
        def _():
            acc_ref[...] = jnp.zeros_like(acc_ref)

        av = a_refs[0][...] if a_fn is None else a_fn([r[...] for r in a_refs])
        acc_ref[...] += lax.dot_general(av, b_ref[...], (((0,), (0,)), ((), ())), preferred_element_type=F32)

        @pl.when(s == steps - 1)
        def _():
            o_ref[...] = acc_ref[...]

    a_spec = (pl.BlockSpec((None, tt, tk), lambda i, j, s: (i, s, 0)) if a_tiled
              else pl.BlockSpec((tt, tk), lambda i, j, s: (s, i)))
    b_spec = (pl.BlockSpec((None, tt, tn), lambda i, j, s: (j, s, 0)) if b_tiled
              else pl.BlockSpec((tt, tn), lambda i, j, s: (s, j)))
    if a_tiled:
        out_dims, tile_index = (nk, tk, nn * tn), lambda i, j, s: (i, 0, j)
    elif b_tiled:
        out_dims, tile_index = (nn, nk * tk, tn), lambda i, j, s: (j, i, 0)
    else:
        out_dims, tile_index = (nk * tk, nn * tn), lambda i, j, s: (i, j)
    tiled = a_tiled or b_tiled
    if has_into:
        buf, lead = into
        def out_index(i, j, s):
            idx = tile_index(i, j, s)
            return (idx[0], *lead, *idx[1:])
        out_spec = pl.BlockSpec((None,) * (1 + len(lead)) + (tk, tn), out_index)
        out_shape = jax.ShapeDtypeStruct(buf.shape, buf.dtype)
        extra_in, extra_specs, aliases = [buf], [pl.BlockSpec(memory_space=pl.ANY)], {0: 0}
    else:
        out_spec = pl.BlockSpec(((None,) if tiled else ()) + (tk, tn), tile_index)
        out_shape = jax.ShapeDtypeStruct(out_dims, F32)
        extra_in, extra_specs, aliases = [], [], {}
    return pl.pallas_call(
        body, out_shape=out_shape, grid=(nk, nn, steps), in_specs=extra_specs + [a_spec] * na + [b_spec],
        out_specs=out_spec, scratch_shapes=[pltpu.VMEM((tk, tn), F32)], input_output_aliases=aliases, name=name,
        compiler_params=_params(("parallel", "parallel", "arbitrary")),
    )(*extra_in, *a_list, b)


def _row_spec(tm, width, col=0):
    return pl.BlockSpec((tm, width), functools.partial(lambda i, col: (i, col), col=col))


def _vec_spec(width):
    return pl.BlockSpec((1, width), lambda i: (0, 0))


def _proj_norm_bwd(name, dz_parts, w_parts, x, dres, g, scale, tm=256):
    t = x.shape[0]
    steps = t // tm
    n = len(dz_parts)

    def body(*refs):
        dz_refs, w_refs = refs[:n], refs[n:2 * n]
        x_ref, dres_ref, g_ref, sc_ref, dx_ref, dg_ref, dsc_ref, dsh_ref = refs[2 * n:]
        i = pl.program_id(0)

        @pl.when(i == 0)
        def _():
            dg_ref[...] = jnp.zeros_like(dg_ref)
            dsh_ref[...] = jnp.zeros_like(dsh_ref)

        dh = None
        for dz_ref, w_ref in zip(dz_refs, w_refs):
            part = lax.dot_general(dz_ref[...], w_ref[...], (((1,), (1,)), ((), ())), preferred_element_type=F32)
            dh = part if dh is None else dh + part
        xv = x_ref[...]
        r = lax.rsqrt(jnp.mean(xv * xv, axis=-1, keepdims=True) + EPS)
        xh = xv * r
        dxh = dh * (g_ref[...] * (1.0 + sc_ref[...]))
        dx_ref[...] = dres_ref[...] + r * (dxh - xh * jnp.mean(dxh * xh, axis=-1, keepdims=True))
        dg_ref[...] += jnp.sum(dh * xh, axis=0, keepdims=True)
        dsh_ref[...] += jnp.sum(dh, axis=0, keepdims=True)

        @pl.when(i == steps - 1)
        def _():
            acc = dg_ref[...]
            dg_ref[...] = acc * (1.0 + sc_ref[...])
            dsc_ref[...] = acc * g_ref[...]

    vec = jax.ShapeDtypeStruct((1, D), F32)
    resident = [pl.BlockSpec(wp.shape, lambda i: (0, 0), pipeline_mode=pl.Buffered(1)) for wp in w_parts]
    return pl.pallas_call(
        body, out_shape=[jax.ShapeDtypeStruct((t, D), F32), vec, vec, vec], grid=(steps,),
        in_specs=[_row_spec(tm, dz.shape[1]) for dz in dz_parts] + resident
        + [_row_spec(tm, D), _row_spec(tm, D), _vec_spec(D), _vec_spec(D)],
        out_specs=[_row_spec(tm, D), _vec_spec(D), _vec_spec(D), _vec_spec(D)],
        name=name, compiler_params=_params(("arbitrary",)),
    )(*dz_parts, *w_parts, x, dres, g, scale)


def _loss_bwd(name, x, target, g, tm=256):
    t = x.shape[0]

    def body(x_ref, t_ref, g_ref, dx_ref, dg_ref, loss_ref):
        @pl.when(pl.program_id(0) == 0)
        def _():
            dg_ref[...] = jnp.zeros_like(dg_ref)
            loss_ref[...] = jnp.zeros_like(loss_ref)

        xv = x_ref[...]
        r = lax.rsqrt(jnp.mean(xv * xv, axis=-1, keepdims=True) + EPS)
        xh = xv * r
        err = xh * g_ref[...] - t_ref[...]
        dy = err * (1.0 / D)
        dxh = dy * g_ref[...]
        dx_ref[...] = r * (dxh - xh * jnp.mean(dxh * xh, axis=-1, keepdims=True))
        dg_ref[...] += jnp.sum(dy * xh, axis=0, keepdims=True)
        loss_ref[...] += jnp.sum(err * err, axis=0, keepdims=True) * (0.5 / D)

    vec = jax.ShapeDtypeStruct((1, D), F32)
    return pl.pallas_call(
        body, out_shape=[jax.ShapeDtypeStruct((t, D), F32), vec, vec], grid=(t // tm,),
        in_specs=[_row_spec(tm, D), _row_spec(tm, D), _vec_spec(D)],
        out_specs=[_row_spec(tm, D), _vec_spec(D), _vec_spec(D)],
        name=name, compiler_params=_params(("arbitrary",)),
    )(x, target, g)


N_DEV = 8


def _mod_fwd(name, c_all, ada_w):
    cols = ada_w.shape[-1]

    def body(c_ref, w_ref, o_ref):
        cv = c_ref[...]
        ca = (cv * _sigmoid(cv)).astype(BF16)
        o_ref[...] = jnp.dot(ca, w_ref[...], preferred_element_type=F32)

    return pl.pallas_call(
        body, out_shape=jax.ShapeDtypeStruct((N_DEV, DEPTH * cols), F32), grid=(DEPTH,),
        in_specs=[pl.BlockSpec((N_DEV, D), lambda l: (0, 0)), pl.BlockSpec((None, D, cols), lambda l: (l, 0, 0))],
        out_specs=pl.BlockSpec((N_DEV, cols), lambda l: (0, l)), name=name, compiler_params=_params(("parallel",)),
    )(c_all, ada_w)


def _mod_bwd(name, c_cols, dmods, tk=256):
    cols = dmods.shape[-1]

    def body(c_ref, d_ref, o_ref):
        cv = c_ref[...]
        ca = cv * _sigmoid(cv)
        total = ca[:, 0:1] * d_ref[0:1, :]
        for b in range(1, N_DEV):
            total = total + ca[:, b:b + 1] * d_ref[b:b + 1, :]
        o_ref[...] = total

    return pl.pallas_call(
        body, out_shape=jax.ShapeDtypeStruct((DEPTH, D, cols), F32), grid=(DEPTH, D // tk),
        in_specs=[pl.BlockSpec((tk, N_DEV), lambda l, i: (i, 0)), pl.BlockSpec((None, N_DEV, cols), lambda l, i: (l, 0, 0))],
        out_specs=pl.BlockSpec((None, tk, cols), lambda l, i: (l, i, 0)), name=name,
        compiler_params=_params(("parallel", "parallel")),
    )(c_cols, dmods)


def _sum_devices(name, rows):
    n = rows.shape[1]

    def body(r_ref, o_ref):
        total = r_ref[0:1, :]
        for b in range(1, N_DEV):
            total = total + r_ref[b:b + 1, :]
        o_ref[...] = total

    whole = pl.BlockSpec(memory_space=pltpu.VMEM)
    return pl.pallas_call(body, out_shape=jax.ShapeDtypeStruct((1, n), F32), in_specs=[whole], out_specs=whole,
                          name=name)(rows)


def _rope_tables(t):
    half = HEAD // 2
    inv_freq = 10000.0 ** (-(jnp.arange(half, dtype=F32) * 2.0 / HEAD))
    ang = jnp.arange(t, dtype=F32)[:, None] * inv_freq[None, :]
    cos, sin = jnp.cos(ang), jnp.sin(ang)
    cos_t = jnp.tile(jnp.concatenate([cos, cos], axis=1), (1, VL // HEAD))
    sin_t = jnp.tile(jnp.concatenate([-sin, sin], axis=1), (1, VL // HEAD))
    return cos_t, sin_t


def _rotate(tv, cos, sin_signed):
    lane = lax.broadcasted_iota(jnp.int32, tv.shape, 1)
    first = (lane % HEAD) < (HEAD // 2)
    partner = jnp.where(first, pltpu.roll(tv, tv.shape[1] - HEAD // 2, 1), pltpu.roll(tv, HEAD // 2, 1))
    return tv * cos + partner * sin_signed


def _dilated_spec(tm, d):
    return pl.BlockSpec((tm // d, d * GW), lambda i: (i, 0))


def _dilated_shape(t, d, dtype):
    return jax.ShapeDtypeStruct((t // d, d * GW), dtype)


def _every(d, r, tm):
    return pl.ds(r, tm // d, stride=d) if d > 1 else slice(None)


def _rope_bwd(name, grads, cos_t, sin_t, tm=512):
    ng = len(DILATIONS)
    n = len(grads)
    t = grads[0].shape[0] * DILATIONS[0]

    halves = GW // VL

    def body(*refs):
        g_refs, cos_ref, sin_ref, o_ref, rows_ref = refs[:n], refs[n], refs[n + 1], refs[n + 2], refs[n + 3]
        cos, sin = cos_ref[...], -sin_ref[...]
        for idx in range(n):
            d = DILATIONS[idx % ng]
            for hh in range(halves):
                for r in range(d):
                    cols = slice(r * GW + hh * VL, r * GW + (hh + 1) * VL)
                    rows_ref[hh, _every(d, r, tm), :] = g_refs[idx][:, cols].astype(F32)
                piece = rows_ref[hh]
                if idx < 2 * ng:
                    piece = _rotate(piece, cos, sin)
                if idx < ng:
                    piece = piece * (HEAD ** -0.5)
                o_ref[:, idx * GW + hh * VL:idx * GW + (hh + 1) * VL] = piece.astype(BF16)

    dils = [DILATIONS[idx % ng] for idx in range(n)]
    return pl.pallas_call(
        body, out_shape=jax.ShapeDtypeStruct((t, n * GW), BF16), grid=(t // tm,),
        in_specs=[_dilated_spec(tm, d) for d in dils] + [_row_spec(tm, VL)] * 2, out_specs=_row_spec(tm, n * GW),
        scratch_shapes=[pltpu.VMEM((halves, tm, VL), F32)], name=name, compiler_params=_params(("parallel",)),
    )(*grads, cos_t, sin_t)


def _head_cols(h):
    return slice(h * HEAD, (h + 1) * HEAD)


def _band_mask_q(has_prev):
    qi = lax.broadcasted_iota(jnp.int32, (BAND, 2 * BAND), 0)
    kj = lax.broadcasted_iota(jnp.int32, (BAND, 2 * BAND), 1)
    dist = qi + BAND - kj
    return (dist >= 0) & (dist <= BAND) & ((kj >= BAND) | has_prev)


def _attn_fwd(name, q, k, v, group):
    d = DILATIONS[group]
    length = q.shape[0]
    qb = min(512, length)
    sub = qb // BAND
    nblk = length // qb

    def body(q_ref, kc_ref, kp_ref, vc_ref, vp_ref, o_ref, lse_ref):
        blk = pl.program_id(1)
        k_ext = jnp.concatenate([kp_ref[...], kc_ref[...]], axis=0)
        v_ext = jnp.concatenate([vp_ref[...], vc_ref[...]], axis=0)
        for j in range(sub):
            mask = _band_mask_q((blk * sub + j) > 0)
            qj = q_ref[j * BAND:(j + 1) * BAND, :]
            kj = k_ext[j * BAND:(j + 2) * BAND, :]
            vj = v_ext[j * BAND:(j + 2) * BAND, :]
            outs, lses = [], []
            for h in range(GW // HEAD):
                s = lax.dot_general(qj[:, _head_cols(h)], kj[:, _head_cols(h)], (((1,), (1,)), ((), ())),
                                    preferred_element_type=F32)
                s = jnp.where(mask, s, NEG_INF)
                m = jnp.max(s, axis=-1, keepdims=True)
                p = jnp.exp(s - m)
                den = jnp.sum(p, axis=-1, keepdims=True)
                o = jnp.dot(p.astype(BF16), vj[:, _head_cols(h)], preferred_element_type=F32)
                outs.append(o / den)
                lses.append(jnp.broadcast_to(m + jnp.log(den), (BAND, HEAD)))
            o_ref[j * BAND:(j + 1) * BAND, :] = jnp.concatenate(outs, axis=1)
            lse_ref[j * BAND:(j + 1) * BAND, :] = jnp.concatenate(lses, axis=1)

    prev = qb // BAND
    cur = lambda r, b: (b, r)
    before = lambda r, b: (jnp.maximum(b * prev - 1, 0), r)
    big, halo = pl.BlockSpec((qb, GW), cur), pl.BlockSpec((BAND, GW), before)
    return pl.pallas_call(
        body, out_shape=[jax.ShapeDtypeStruct((length, d * GW), F32)] * 2, grid=(d, nblk),
        in_specs=[big, big, halo, big, halo], out_specs=[big] * 2, name=name,
        compiler_params=_params(("parallel", "parallel")),
    )(q, k, k, v, v)


def _attn_merge(name, outs, lses, tm=512):
    n = len(outs)
    t = outs[0].shape[0] * DILATIONS[0]

    halves = GW // VL

    def body(*refs):
        in_refs = refs[:2 * n]
        ob_ref, of_ref, lj_ref, rows_ref = refs[2 * n:]
        for hh in range(halves):
            for idx in range(2 * n):
                d = DILATIONS[idx % n]
                for r in range(d):
                    cols = slice(r * GW + hh * VL, r * GW + (hh + 1) * VL)
                    rows_ref[idx, _every(d, r, tm), :] = in_refs[idx][:, cols]
            ls = [rows_ref[n + g] for g in range(n)]
            m = ls[0]
            for v in ls[1:]:
                m = jnp.maximum(m, v)
            es = [jnp.exp(v - m) for v in ls]
            tot = es[0]
            for v in es[1:]:
                tot = tot + v
            acc = (es[0] / tot) * rows_ref[0]
            for g in range(1, n):
                acc = acc + (es[g] / tot) * rows_ref[g]
            half = slice(hh * VL, (hh + 1) * VL)
            ob_ref[:, half] = acc.astype(BF16)
            of_ref[:, half] = acc
            lj_ref[:, half] = m + jnp.log(tot)

    return pl.pallas_call(
        body, out_shape=[jax.ShapeDtypeStruct((t, GW), BF16), jax.ShapeDtypeStruct((t, GW), F32),
                         jax.ShapeDtypeStruct((t, GW), F32)], grid=(t // tm,),
        in_specs=[_dilated_spec(tm, DILATIONS[idx % n]) for idx in range(2 * n)], out_specs=[_row_spec(tm, GW)] * 3,
        scratch_shapes=[pltpu.VMEM((2 * n, tm, VL), F32)], name=name, compiler_params=_params(("parallel",)),
    )(*outs, *lses)


def _attn_bwd_prep(name, do, o, lj, tm=512):
    t = do.shape[0]
    n = len(DILATIONS)

    halves = GW // VL
    per_half = VL // HEAD

    def body(*refs):
        do_refs, o_refs, lj_refs = refs[:halves], refs[halves:2 * halves], refs[2 * halves:3 * halves]
        outs, dsum_ref = refs[3 * halves:3 * halves + 3 * n], refs[3 * halves + 3 * n]
        for hh in range(halves):
            prod = do_refs[hh][...] * o_refs[hh][...]
            parts = [jnp.broadcast_to(jnp.sum(prod[:, _head_cols(h)], axis=-1, keepdims=True), (tm, HEAD))
                     for h in range(per_half)]
            dsum_ref[...] = jnp.concatenate(parts, axis=1)
            for g, d in enumerate(DILATIONS):
                for r in range(d):
                    rows, cols = _every(d, r, tm), slice(r * GW + hh * VL, r * GW + (hh + 1) * VL)
                    outs[g][:, cols] = dsum_ref[rows, :]
                    outs[n + g][:, cols] = do_refs[hh][rows, :].astype(BF16)
                    outs[2 * n + g][:, cols] = lj_refs[hh][rows, :]

    shapes = [_dilated_shape(t, d, dt) for dt in (F32, BF16, F32) for d in DILATIONS]
    half_specs = [_row_spec(tm, VL, hh) for hh in range(halves)]
    return pl.pallas_call(
        body, out_shape=shapes, grid=(t // tm,), in_specs=half_specs * 3,
        out_specs=[_dilated_spec(tm, d) for d in DILATIONS] * 3, scratch_shapes=[pltpu.VMEM((tm, VL), F32)],
        name=name, compiler_params=_params(("parallel",)),
    )(*([do] * halves), *([o] * halves), *([lj] * halves))


def _attn_bwd(name, q, k, v, do, lj, dsum, group):
    d = DILATIONS[group]
    length = q.shape[0]
    qb = min(512, length)
    sub = qb // BAND
    nblk = length // qb
    total = length // BAND

    def body(qc_ref, qn_ref, kc_ref, kp_ref, vc_ref, vp_ref, doc_ref, don_ref, ljc_ref, ljn_ref, dsc_ref, dsn_ref,
             dq_ref, dk_ref, dv_ref):
        blk = pl.program_id(1)
        k_ext = jnp.concatenate([kp_ref[...], kc_ref[...]], axis=0)
        v_ext = jnp.concatenate([vp_ref[...], vc_ref[...]], axis=0)
        heads = range(GW // HEAD)
        nt = (((1,), (1,)), ((), ()))
        tn = (((0,), (0,)), ((), ()))

        def scores(qh, doh, ljh, dsh, kh, vh, mask):
            s = lax.dot_general(qh, kh, nt, preferred_element_type=F32)
            p = jnp.where(mask, jnp.exp(s - ljh), 0.0)
            dp = lax.dot_general(doh, vh, nt, preferred_element_type=F32)
            return p.astype(BF16), (p * (dp - dsh)).astype(BF16)

        held_k, held_v = [None] * len(heads), [None] * len(heads)
        for j in range(sub):
            rows = slice(j * BAND, (j + 1) * BAND)
            rows2 = slice(j * BAND, (j + 2) * BAND)
            mask = _band_mask_q((blk * sub + j) > 0)
            dqs, dks, dvs = [], [], []
            for h in heads:
                hc = _head_cols(h)
                col = slice(h * HEAD, h * HEAD + 1)
                qh, doh, kh2 = qc_ref[rows, hc], doc_ref[rows, hc], k_ext[rows2, hc]
                p, ds = scores(qh, doh, ljc_ref[rows, col], dsc_ref[rows, col], kh2, v_ext[rows2, hc], mask)
                dqs.append(jnp.dot(ds, kh2, preferred_element_type=F32))
                dk2 = lax.dot_general(ds, qh, tn, preferred_element_type=F32)
                dv2 = lax.dot_general(p, doh, tn, preferred_element_type=F32)
                if j > 0:
                    dks.append(held_k[h] + dk2[:BAND])
                    dvs.append(held_v[h] + dv2[:BAND])
                held_k[h], held_v[h] = dk2[BAND:], dv2[BAND:]
            dq_ref[rows, :] = jnp.concatenate(dqs, axis=1)
            if j > 0:
                done = slice((j - 1) * BAND, j * BAND)
                dk_ref[done, :] = jnp.concatenate(dks, axis=1)
                dv_ref[done, :] = jnp.concatenate(dvs, axis=1).astype(BF16)

        last = slice((sub - 1) * BAND, sub * BAND)
        qi = lax.broadcasted_iota(jnp.int32, (BAND, BAND), 0)
        kj = lax.broadcasted_iota(jnp.int32, (BAND, BAND), 1)
        mask = (kj >= qi) & ((blk + 1) * sub < total)
        dks, dvs = [], []
        for h in heads:
            hc = _head_cols(h)
            col = slice(h * HEAD, h * HEAD + 1)
            qh, doh = qn_ref[:, hc], don_ref[:, hc]
            p, ds = scores(qh, doh, ljn_ref[:, col], dsn_ref[:, col], kc_ref[last, hc], vc_ref[last, hc], mask)
            dks.append(held_k[h] + lax.dot_general(ds, qh, tn, preferred_element_type=F32))
            dvs.append(held_v[h] + lax.dot_general(p, doh, tn, preferred_element_type=F32))
        dk_ref[last, :] = jnp.concatenate(dks, axis=1)
        dv_ref[last, :] = jnp.concatenate(dvs, axis=1).astype(BF16)

    prev = qb // BAND
    cur = lambda r, b: (b, r)
    before = lambda r, b: (jnp.maximum(b * prev - 1, 0), r)
    after = lambda r, b: (jnp.minimum((b + 1) * prev, total - 1), r)
    big = pl.BlockSpec((qb, GW), cur)
    nxt = pl.BlockSpec((BAND, GW), after)
    prv = pl.BlockSpec((BAND, GW), before)
    return pl.pallas_call(
        body, out_shape=[jax.ShapeDtypeStruct((length, d * GW), F32), jax.ShapeDtypeStruct((length, d * GW), F32),
                         jax.ShapeDtypeStruct((length, d * GW), BF16)], grid=(d, nblk),
        in_specs=[big, nxt, big, prv, big, prv, big, nxt, big, nxt, big, nxt],
        out_specs=[big] * 3, name=name, compiler_params=_params(("parallel", "parallel")),
    )(q, q, k, k, v, v, do, do, lj, lj, dsum, dsum)


SUBLANES = 8
CONV_CHUNK = 32
SHIFT_ROWS = HALO - SUBLANES


def _shifted_copies(buf_ref, sh_ref, tm):
    for s in range(1, SUBLANES):
        sh_ref[s - 1] = buf_ref[s:s + tm + SHIFT_ROWS, :]


def _window(buf_ref, sh_ref, offset, r0, rows):
    tiles, shift = divmod(offset, SUBLANES)
    src = buf_ref if shift == 0 else sh_ref.at[shift - 1]
    return src[pl.ds(pl.multiple_of(r0 + tiles * SUBLANES, SUBLANES), rows), :]


def _conv_fwd(name, zu, conv_w, conv_b, ln_g, ln_b, tm=256):
    t = zu.shape[0]
    per = tm // HALO

    def body(a_ref, gl_ref, ah_ref, glh_ref, w_ref, b_ref, g_ref, beta_ref, hc_ref, s_ref, ext_ref, sh_ref):
        i = pl.program_id(0)
        halo = ah_ref[...] * _sigmoid(glh_ref[...])
        ext_ref[0:HALO, :] = jnp.where(i > 0, halo, 0.0)
        ext_ref[HALO:, :] = a_ref[...] * _sigmoid(gl_ref[...])
        _shifted_copies(ext_ref, sh_ref, tm)

        def chunk(r, carry):
            r0 = pl.multiple_of(r * CONV_CHUNK, CONV_CHUNK)
            part = jnp.broadcast_to(b_ref[...], (CONV_CHUNK, D))
            for kk in range(CONV_K):
                part = part + w_ref[kk:kk + 1, :] * _window(ext_ref, sh_ref, HALO - CONV_K + 1 + kk, r0, CONV_CHUNK)
            hc_ref[pl.ds(r0, CONV_CHUNK), :] = part
            return carry

        lax.fori_loop(0, tm // CONV_CHUNK, chunk, 0)
        acc = hc_ref[...]
        mu = jnp.mean(acc, axis=-1, keepdims=True)
        xc = acc - mu
        var = jnp.mean(xc * xc, axis=-1, keepdims=True)
        ln = xc * lax.rsqrt(var + EPS) * g_ref[...] + beta_ref[...]
        s_ref[...] = (ln * _sigmoid(ln)).astype(BF16)

    halo_map = lambda col: (lambda i: (jnp.maximum(i * per - 1, 0), col))
    return pl.pallas_call(
        body, out_shape=[jax.ShapeDtypeStruct((t, D), F32), jax.ShapeDtypeStruct((t, D), BF16)], grid=(t // tm,),
        in_specs=[_row_spec(tm, D, 0), _row_spec(tm, D, 1), pl.BlockSpec((HALO, D), halo_map(0)),
                  pl.BlockSpec((HALO, D), halo_map(1)), pl.BlockSpec((HALO, D), lambda i: (0, 0)),
                  _vec_spec(D), _vec_spec(D), _vec_spec(D)],
        out_specs=[_row_spec(tm, D), _row_spec(tm, D)],
        scratch_shapes=[pltpu.VMEM((tm + HALO, D), F32), pltpu.VMEM((SUBLANES - 1, tm + SHIFT_ROWS, D), F32)],
        name=name, compiler_params=_params(("parallel",)),
    )(zu, zu, zu, zu, conv_w, conv_b, ln_g, ln_b)


def _conv_ln_bwd(name, hc, ds, ln_g, ln_b, tm=256):
    t = hc.shape[0]

    def body(hc_ref, ds_ref, g_ref, beta_ref, dhc_ref, dg_ref, dbeta_ref, dbias_ref):
        @pl.when(pl.program_id(0) == 0)
        def _():
            dg_ref[...] = jnp.zeros_like(dg_ref)
            dbeta_ref[...] = jnp.zeros_like(dbeta_ref)
            dbias_ref[...] = jnp.zeros_like(dbias_ref)

        hv = hc_ref[...]
        mu = jnp.mean(hv, axis=-1, keepdims=True)
        xc = hv - mu
        rstd = lax.rsqrt(jnp.mean(xc * xc, axis=-1, keepdims=True) + EPS)
        xh = xc * rstd
        ln = xh * g_ref[...] + beta_ref[...]
        sg = _sigmoid(ln)
        dln = ds_ref[...] * (sg * (1.0 + ln * (1.0 - sg)))
        dxh = dln * g_ref[...]
        dh = rstd * (dxh - jnp.mean(dxh, axis=-1, keepdims=True) - xh * jnp.mean(dxh * xh, axis=-1, keepdims=True))
        dhc_ref[...] = dh
        dg_ref[...] += jnp.sum(dln * xh, axis=0, keepdims=True)
        dbeta_ref[...] += jnp.sum(dln, axis=0, keepdims=True)
        dbias_ref[...] += jnp.sum(dh, axis=0, keepdims=True)

    vec = jax.ShapeDtypeStruct((1, D), F32)
    return pl.pallas_call(
        body, out_shape=[jax.ShapeDtypeStruct((t, D), F32), vec, vec, vec], grid=(t // tm,),
        in_specs=[_row_spec(tm, D), _row_spec(tm, D), _vec_spec(D), _vec_spec(D)],
        out_specs=[_row_spec(tm, D), _vec_spec(D), _vec_spec(D), _vec_spec(D)],
        name=name, compiler_params=_params(("arbitrary",)),
    )(hc, ds, ln_g, ln_b)


def _conv_bwd(name, zu, dhc, conv_w, tm=256):
    t = zu.shape[0]
    per = tm // HALO
    steps = t // tm

    group = 4

    def body(a_ref, gl_ref, ah_ref, glh_ref, d_ref, dn_ref, w_ref, dz_ref, dw_ref, ext_ref, sh_ref, dext_ref, dsh_ref,
             sg_ref, part_ref):
        i = pl.program_id(0)

        @pl.when(i == 0)
        def _():
            part_ref[...] = jnp.zeros_like(part_ref)

        sg_ref[...] = _sigmoid(gl_ref[...])
        ext_ref[0:HALO, :] = jnp.where(i > 0, ah_ref[...] * _sigmoid(glh_ref[...]), 0.0)
        ext_ref[HALO:, :] = a_ref[...] * sg_ref[...]
        dext_ref[0:tm, :] = d_ref[...]
        dext_ref[tm:, :] = jnp.where(i < steps - 1, dn_ref[...], 0.0)
        _shifted_copies(ext_ref, sh_ref, tm)
        _shifted_copies(dext_ref, dsh_ref, tm)

        def chunk(r, carry):
            r0 = pl.multiple_of(r * CONV_CHUNK, CONV_CHUNK)
            rows = pl.ds(r0, CONV_CHUNK)
            part = jnp.zeros((CONV_CHUNK, D), F32)
            for kk in range(CONV_K):
                part = part + w_ref[kk:kk + 1, :] * _window(dext_ref, dsh_ref, CONV_K - 1 - kk, r0, CONV_CHUNK)
            sg = sg_ref[rows, :]
            dz_ref[rows, 0:D] = (part * sg).astype(BF16)
            dz_ref[rows, D:] = (part * a_ref[rows, :] * sg * (1.0 - sg)).astype(BF16)
            return carry

        lax.fori_loop(0, tm // CONV_CHUNK, chunk, 0)

        for k0 in range(0, CONV_K, group):
            taps = range(k0, min(k0 + group, CONV_K))

            def tile(r, parts, taps=taps):
                r0 = pl.multiple_of(r * CONV_CHUNK, CONV_CHUNK)
                dv = d_ref[pl.ds(r0, CONV_CHUNK), :]
                out = []
                for p, kk in zip(parts, taps):
                    prod = dv * _window(ext_ref, sh_ref, HALO - CONV_K + 1 + kk, r0, CONV_CHUNK)
                    for s in range(0, CONV_CHUNK, SUBLANES):
                        p = p + prod[s:s + SUBLANES, :]
                    out.append(p)
                return tuple(out)

            parts = lax.fori_loop(0, tm // CONV_CHUNK, tile, tuple(jnp.zeros((SUBLANES, D), F32) for _ in taps))
            for p, kk in zip(parts, taps):
                part_ref[kk * SUBLANES:(kk + 1) * SUBLANES, :] += p

        @pl.when(i == steps - 1)
        def _():
            for kk in range(HALO):
                dw_ref[kk:kk + 1, :] = jnp.sum(part_ref[kk * SUBLANES:(kk + 1) * SUBLANES, :], axis=0, keepdims=True)

    halo_map = lambda col: (lambda i: (jnp.maximum(i * per - 1, 0), col))
    shifted = pltpu.VMEM((SUBLANES - 1, tm + SHIFT_ROWS, D), F32)
    return pl.pallas_call(
        body, out_shape=[jax.ShapeDtypeStruct((t, 2 * D), BF16), jax.ShapeDtypeStruct((HALO, D), F32)],
        grid=(steps,),
        in_specs=[_row_spec(tm, D, 0), _row_spec(tm, D, 1), pl.BlockSpec((HALO, D), halo_map(0)),
                  pl.BlockSpec((HALO, D), halo_map(1)), _row_spec(tm, D),
                  pl.BlockSpec((HALO, D), lambda i: (jnp.minimum((i + 1) * per, t // HALO - 1), 0)),
                  pl.BlockSpec((HALO, D), lambda i: (0, 0))],
        out_specs=[_row_spec(tm, 2 * D), pl.BlockSpec((HALO, D), lambda i: (0, 0))],
        scratch_shapes=[pltpu.VMEM((tm + HALO, D), F32), shifted, pltpu.VMEM((tm + HALO, D), F32), shifted,
                        pltpu.VMEM((tm, D), F32), pltpu.VMEM((HALO * SUBLANES, D), F32)],
        name=name, compiler_params=_params(("arbitrary",)),
    )(zu, zu, zu, zu, dhc, dhc, conv_w)


NT = (((1,), (1,)), ((), ()))


def _resident(w, at):
    block = (None,) * (1 + len(at)) + tuple(w.shape[-2:])
    return [pl.BlockSpec(block, functools.partial(lambda i, k: (k, *at, 0, 0), k=k), pipeline_mode=pl.Buffered(1))
            for k in range(SHARDS)]


def _ffn_fwd(tag, x, norm, w, at, gate, tm=512):
    t = x.shape[0]
    slab = min(tm, MM_SLAB)

    def body(*refs):
        x_ref, gain_ref, scale_ref, shift_ref, gate_ref = refs[:5]
        wg, wu, wd = refs[5:5 + SHARDS], refs[5 + SHARDS:5 + 2 * SHARDS], refs[5 + 2 * SHARDS:5 + 3 * SHARDS]
        h_ref, g_ref, u_ref, xn_ref, f_ref = refs[5 + 3 * SHARDS:]
        half_gate = 0.5 * gate_ref[...]
        for r0 in range(0, tm, slab):
            rows = slice(r0, r0 + slab)
            xs = x_ref[rows, :]
            r = lax.rsqrt(jnp.mean(xs * xs, axis=-1, keepdims=True) + EPS)
            hs = ((xs * r) * gain_ref[...] * (1.0 + scale_ref[...]) + shift_ref[...]).astype(BF16)
            h_ref[rows, :] = hs
            tot = None
            for k in range(SHARDS):
                gk = jnp.dot(hs, wg[k][...], preferred_element_type=F32)
                uk = jnp.dot(hs, wu[k][...], preferred_element_type=F32)
                g_ref[k, rows, :] = gk.astype(BF16)
                u_ref[k, rows, :] = uk.astype(BF16)
                ak = ((gk * _sigmoid(gk)) * uk).astype(BF16)
                part = jnp.dot(ak, wd[k][...], preferred_element_type=F32)
                tot = part if tot is None else tot + part
            xn_ref[rows, :] = xs + half_gate * tot
            f_ref[rows, :] = tot.astype(BF16)

    hidden = jax.ShapeDtypeStruct((SHARDS, t, FSH), BF16)
    hidden_spec = pl.BlockSpec((SHARDS, tm, FSH), lambda i: (0, i, 0))
    half = jax.ShapeDtypeStruct((t, D), BF16)
    h, gv, uv, x_new, f = pl.pallas_call(
        body, out_shape=[half, hidden, hidden, jax.ShapeDtypeStruct((t, D), F32), half], grid=(t // tm,),
        in_specs=[_row_spec(tm, D)] + [_vec_spec(D)] * 4 + _resident(w["ffn_wg"], at)
        + _resident(w["ffn_wu"], at) + _resident(w["ffn_wd"], at),
        out_specs=[_row_spec(tm, D), hidden_spec, hidden_spec, _row_spec(tm, D), _row_spec(tm, D)],
        name=f"ffn_fwd_{tag}", compiler_params=_params(("parallel",)),
    )(x, *norm, gate, *([w["ffn_wg"]] * SHARDS), *([w["ffn_wu"]] * SHARDS), *([w["ffn_wd"]] * SHARDS))
    return x_new, h, (gv, uv, f)


def _ffn_hidden_bwd(tag, dx, f, gate, gv, uv, x, gain, scale, w, at, tm=256):
    t = dx.shape[0]
    steps = t // tm
    slab = min(tm, MM_SLAB)

    def body(*refs):
        dx_ref, f_ref, gate_ref, g_ref, u_ref, x_ref, gain_ref, scale_ref = refs[:8]
        wg, wu, wd = refs[8:8 + SHARDS], refs[8 + SHARDS:8 + 2 * SHARDS], refs[8 + 2 * SHARDS:8 + 3 * SHARDS]
        df_ref, dg_ref, du_ref, dxin_ref, dgate_ref, dgain_ref, dscale_ref, dshift_ref = refs[8 + 3 * SHARDS:]
        i = pl.program_id(0)

        @pl.when(i == 0)
        def _():
            dgate_ref[...] = jnp.zeros_like(dgate_ref)
            dgain_ref[...] = jnp.zeros_like(dgain_ref)
            dshift_ref[...] = jnp.zeros_like(dshift_ref)

        half_gate = 0.5 * gate_ref[...]
        norm_w = gain_ref[...] * (1.0 + scale_ref[...])
        for r0 in range(0, tm, slab):
            rows = slice(r0, r0 + slab)
            dxs = dx_ref[rows, :]
            dfs = (half_gate * dxs).astype(BF16)
            df_ref[rows, :] = dfs
            dgate_ref[...] += jnp.sum((0.5 * f_ref[rows, :].astype(F32)) * dxs, axis=0, keepdims=True)
            tot = None
            for k in range(SHARDS):
                da = lax.dot_general(dfs, wd[k][...], NT, preferred_element_type=F32)
                gk, uk = g_ref[k, rows, :].astype(F32), u_ref[k, rows, :].astype(F32)
                sg = _sigmoid(gk)
                dgk = (da * uk * (sg * (1.0 + gk * (1.0 - sg)))).astype(BF16)
                duk = (da * (gk * sg)).astype(BF16)
                dg_ref[k, rows, :] = dgk
                du_ref[k, rows, :] = duk
                part = (lax.dot_general(dgk, wg[k][...], NT, preferred_element_type=F32)
                        + lax.dot_general(duk, wu[k][...], NT, preferred_element_type=F32))
                tot = part if tot is None else tot + part
            xs = x_ref[rows, :]
            r = lax.rsqrt(jnp.mean(xs * xs, axis=-1, keepdims=True) + EPS)
            xh = xs * r
            dxh = tot * norm_w
            dxin_ref[rows, :] = dxs + r * (dxh - xh * jnp.mean(dxh * xh, axis=-1, keepdims=True))
            dgain_ref[...] += jnp.sum(tot * xh, axis=0, keepdims=True)
            dshift_ref[...] += jnp.sum(tot, axis=0, keepdims=True)

        @pl.when(i == steps - 1)
        def _():
            acc = dgain_ref[...]
            dgain_ref[...] = acc * (1.0 + scale_ref[...])
            dscale_ref[...] = acc * gain_ref[...]

    hidden = jax.ShapeDtypeStruct((SHARDS, t, FSH), BF16)
    hidden_spec = pl.BlockSpec((SHARDS, tm, FSH), lambda i: (0, i, 0))
    vec = jax.ShapeDtypeStruct((1, D), F32)
    return pl.pallas_call(
        body, out_shape=[jax.ShapeDtypeStruct((t, D), BF16), hidden, hidden, jax.ShapeDtypeStruct((t, D), F32),
                         vec, vec, vec, vec],
        grid=(steps,),
        in_specs=[_row_spec(tm, D), _row_spec(tm, D), _vec_spec(D), hidden_spec, hidden_spec, _row_spec(tm, D),
                  _vec_spec(D), _vec_spec(D)]
        + _resident(w["ffn_wg"], at) + _resident(w["ffn_wu"], at) + _resident(w["ffn_wd"], at),
        out_specs=[_row_spec(tm, D), hidden_spec, hidden_spec, _row_spec(tm, D)] + [_vec_spec(D)] * 4,
        name=f"ffn_hidden_bwd_{tag}", compiler_params=_params(("arbitrary",)),
    )(dx, f, gate, gv, uv, x, gain, scale, *([w["ffn_wg"]] * SHARDS), *([w["ffn_wu"]] * SHARDS),
      *([w["ffn_wd"]] * SHARDS))


def _ffn_bwd(tag, dx, x, h, saved, w, at, g, scale, gate, into):
    gv, uv, f = saved
    df, dg, du, dx_in, dgate, dgn, dscale, dshift = _ffn_hidden_bwd(tag, dx, f, gate, gv, uv, x, g, scale, w, at)

    def act(blocks):
        gf, uf = blocks[0].astype(F32), blocks[1].astype(F32)
        return ((gf * _sigmoid(gf)) * uf).astype(BF16)

    dwd = _mm_tn(f"ffn_dwd_{tag}", [gv, uv], df, tk=FSH, tn=1024, tt=2048, a_fn=act, a_tiled=True,
                 into=(into["ffn_wd"], at))
    dwg = _mm_tn(f"ffn_dwg_{tag}", h, dg, tk=1024, tn=FSH, tt=2048, b_tiled=True, into=(into["ffn_wg"], at))
    dwu = _mm_tn(f"ffn_dwu_{tag}", h, du, tk=1024, tn=FSH, tt=2048, b_tiled=True, into=(into["ffn_wu"], at))
    return dx_in, dict(ffn_wg=dwg, ffn_wu=dwu, ffn_wd=dwd), dgn, (dshift, dscale, dgate)


def _mix_in_fwd(name, x, g, scale, shift, w_parts, cos_t, sin_t, tm=512):
    t = x.shape[0]
    ng = len(DILATIONS)
    n = 3 * ng
    halves = GW // VL
    strips = QKV // VL

    def body(x_ref, g_ref, sc_ref, sh_ref, cos_ref, sin_ref, wq_ref, wu_ref, wg_ref, h_ref, *rest):
        o_refs, zu_ref, zg_ref, strip_ref = rest[:n], rest[n], rest[n + 1], rest[n + 2]
        for r0 in range(0, tm, MM_SLAB):
            rows = slice(r0, r0 + MM_SLAB)
            xv = x_ref[rows, :]
            r = lax.rsqrt(jnp.mean(xv * xv, axis=-1, keepdims=True) + EPS)
            hv = ((xv * r) * g_ref[...] * (1.0 + sc_ref[...]) + sh_ref[...]).astype(BF16)
            h_ref[rows, :] = hv
            zu_ref[rows, :] = jnp.dot(hv, wu_ref[...], preferred_element_type=F32)
            zg_ref[rows, :] = jnp.dot(hv, wg_ref[...], preferred_element_type=F32).astype(BF16)
            zq = jnp.dot(hv, wq_ref[...], preferred_element_type=F32)
            for j in range(strips):
                strip_ref[j, rows, :] = zq[:, j * VL:(j + 1) * VL]
        for idx in range(n):
            d = DILATIONS[idx % ng]
            for r in range(d):
                rows = _every(d, r, tm)
                for hh in range(halves):
                    piece = strip_ref[halves * idx + hh, rows, :]
                    if idx < 2 * ng:
                        piece = _rotate(piece, cos_ref[rows, :], sin_ref[rows, :])
                    if idx < ng:
                        piece = piece * (HEAD ** -0.5)
                    o_refs[idx][:, r * GW + hh * VL:r * GW + (hh + 1) * VL] = piece.astype(BF16)

    dils = [DILATIONS[idx % ng] for idx in range(n)]
    return pl.pallas_call(
        body, out_shape=[jax.ShapeDtypeStruct((t, D), BF16)] + [_dilated_shape(t, d, BF16) for d in dils]
        + [jax.ShapeDtypeStruct((t, 2 * D), F32), jax.ShapeDtypeStruct((t, 2 * D), BF16)], grid=(t // tm,),
        in_specs=[_row_spec(tm, D), _vec_spec(D), _vec_spec(D), _vec_spec(D), _row_spec(tm, VL), _row_spec(tm, VL)]
        + [_whole(wp) for wp in w_parts],
        out_specs=[_row_spec(tm, D)] + [_dilated_spec(tm, d) for d in dils] + [_row_spec(tm, 2 * D)] * 2,
        scratch_shapes=[pltpu.VMEM((strips, tm, VL), F32)], name=name, compiler_params=_params(("parallel",)),
    )(x, g, scale, shift, cos_t, sin_t, *w_parts)


def _mix_fwd(tag, x, norm, w_in, attn_wo, conv_w, conv_b, ln_g, ln_b, conv_wo, w_out, gate, cos_t, sin_t):
    w_qkv, w_u, w_g = w_in[:, :QKV], w_in[:, QKV:QKV + 2 * D], w_in[:, QKV + 2 * D:]
    h, *qkv, zu, zg = _mix_in_fwd(f"mix_in_{tag}", x, *norm, [w_qkv, w_u, w_g], cos_t, sin_t)
    n = len(DILATIONS)
    outs, lses = [], []
    for grp in range(n):
        o, lse = _attn_fwd(f"attn_fwd_{tag}_{grp}", qkv[grp], qkv[n + grp], qkv[2 * n + grp], grp)
        outs.append(o)
        lses.append(lse)
    ob, of, lj = _attn_merge(f"attn_merge_{tag}", outs, lses)
    hc, s = _conv_fwd(f"conv_fwd_{tag}", zu, conv_w, conv_b, ln_g, ln_b)
    y, ya, yc, sa, sc, x_new, f = _mix_out_fwd(f"mix_out_{tag}", ob, s, zg, x, gate, attn_wo, conv_wo, w_out)
    return x_new, h, (zu, sa, sc, qkv, ob, of, lj, hc, s, y, ya, yc, f, (w_qkv, w_u, w_g))


def _whole(w):
    return pl.BlockSpec(w.shape, lambda i: (0, 0), pipeline_mode=pl.Buffered(1))


def _mix_out_fwd(name, ob, s, zg, x, gate, attn_wo, conv_wo, w_out, tm=512):
    t = x.shape[0]

    def body(ob_ref, s_ref, za_ref, zc_ref, x_ref, gate_ref, wa_ref, wc_ref, wo_ref,
             y_ref, ya_ref, yc_ref, sa_ref, sc_ref, xn_ref, f_ref):
        for r0 in range(0, tm, MM_SLAB):
            rows = slice(r0, r0 + MM_SLAB)
            ya = jnp.dot(ob_ref[rows, :], wa_ref[...], preferred_element_type=F32)
            yc = jnp.dot(s_ref[rows, :], wc_ref[...], preferred_element_type=F32)
            sa, sc = _sigmoid(za_ref[rows, :].astype(F32)), _sigmoid(zc_ref[rows, :].astype(F32))
            y = (sa * ya + sc * yc).astype(BF16)
            out = jnp.dot(y, wo_ref[...], preferred_element_type=F32)
            y_ref[rows, :], ya_ref[rows, :], yc_ref[rows, :] = y, ya.astype(BF16), yc.astype(BF16)
            sa_ref[rows, :], sc_ref[rows, :] = sa.astype(BF16), sc.astype(BF16)
            xn_ref[rows, :] = x_ref[rows, :] + gate_ref[...] * out
            f_ref[rows, :] = out.astype(BF16)

    half = jax.ShapeDtypeStruct((t, D), BF16)
    return pl.pallas_call(
        body, out_shape=[half] * 5 + [jax.ShapeDtypeStruct((t, D), F32), half], grid=(t // tm,),
        in_specs=[_row_spec(tm, GW), _row_spec(tm, D), _row_spec(tm, D, 0), _row_spec(tm, D, 1), _row_spec(tm, D),
                  _vec_spec(D), _whole(attn_wo), _whole(conv_wo), _whole(w_out)],
        out_specs=[_row_spec(tm, D)] * 7, name=name, compiler_params=_params(("parallel",)),
    )(ob, s, zg, zg, x, gate, attn_wo, conv_wo, w_out)


def _mix_out_bwd(name, dx, f, gate, sa, sc, ya, yc, attn_wo, conv_wo, w_out, tm=512):
    t = dx.shape[0]

    def body(dx_ref, f_ref, gate_ref, sa_ref, sc_ref, ya_ref, yc_ref, wa_ref, wc_ref, wo_ref,
             df_ref, dya_ref, dyc_ref, dzg_ref, do_ref, ds_ref, dgate_ref):
        @pl.when(pl.program_id(0) == 0)
        def _():
            dgate_ref[...] = jnp.zeros_like(dgate_ref)

        for r0 in range(0, tm, MM_SLAB):
            rows = slice(r0, r0 + MM_SLAB)
            dxs = dx_ref[rows, :]
            dfs = (gate_ref[...] * dxs).astype(BF16)
            df_ref[rows, :] = dfs
            dgate_ref[...] += jnp.sum(f_ref[rows, :].astype(F32) * dxs, axis=0, keepdims=True)
            dy = lax.dot_general(dfs, wo_ref[...], NT, preferred_element_type=F32)
            ga, gc = sa_ref[rows, :].astype(F32), sc_ref[rows, :].astype(F32)
            dya, dyc = (dy * ga).astype(BF16), (dy * gc).astype(BF16)
            dya_ref[rows, :], dyc_ref[rows, :] = dya, dyc
            dzg_ref[rows, 0:D] = (dy * ya_ref[rows, :].astype(F32) * (ga * (1.0 - ga))).astype(BF16)
            dzg_ref[rows, D:] = (dy * yc_ref[rows, :].astype(F32) * (gc * (1.0 - gc))).astype(BF16)
            do_ref[rows, :] = lax.dot_general(dya, wa_ref[...], NT, preferred_element_type=F32)
            ds_ref[rows, :] = lax.dot_general(dyc, wc_ref[...], NT, preferred_element_type=F32)

    half = jax.ShapeDtypeStruct((t, D), BF16)
    return pl.pallas_call(
        body, out_shape=[half, half, half, jax.ShapeDtypeStruct((t, 2 * D), BF16), jax.ShapeDtypeStruct((t, GW), F32),
                         jax.ShapeDtypeStruct((t, D), F32), jax.ShapeDtypeStruct((1, D), F32)], grid=(t // tm,),
        in_specs=[_row_spec(tm, D), _row_spec(tm, D), _vec_spec(D)] + [_row_spec(tm, D)] * 4
        + [_whole(attn_wo), _whole(conv_wo), _whole(w_out)],
        out_specs=[_row_spec(tm, D)] * 3 + [_row_spec(tm, 2 * D), _row_spec(tm, GW), _row_spec(tm, D), _vec_spec(D)],
        name=name, compiler_params=_params(("arbitrary",)),
    )(dx, f, gate, sa, sc, ya, yc, attn_wo, conv_wo, w_out)


def _mix_bwd(tag, dx, x, h, saved, attn_wo, conv_w, ln_g, ln_b, conv_wo, w_out, g, scale, gate, cos_t, sin_t):
    zu, sa, sc, qkv, ob, of, lj, hc, s, y, ya, yc, f, w_parts = saved
    n = len(DILATIONS)
    df, dya, dyc, dzg, do, ds, dgate = _mix_out_bwd(f"mix_out_bwd_{tag}", dx, f, gate, sa, sc, ya, yc, attn_wo,
                                                    conv_wo, w_out)
    dw_out = _mm_tn(f"mix_dwout_{tag}", y, df, tk=1024, tn=1024, tt=2048)
    dw_attn = _mm_tn(f"mix_dwattn_{tag}", ob, dya, tk=GW, tn=1024, tt=2048)
    dw_conv_o = _mm_tn(f"mix_dwconvo_{tag}", s, dyc, tk=1024, tn=1024, tt=2048)

    prep = _attn_bwd_prep(f"attn_prep_{tag}", do, of, lj)
    dqs, dks, dvs = [], [], []
    for grp in range(n):
        dq, dk, dv = _attn_bwd(f"attn_bwd_{tag}_{grp}", qkv[grp], qkv[n + grp], qkv[2 * n + grp], prep[n + grp],
                               prep[2 * n + grp], prep[grp], grp)
        dqs.append(dq)
        dks.append(dk)
        dvs.append(dv)
    dzqkv = _rope_bwd(f"rope_bwd_{tag}", dqs + dks + dvs, cos_t, sin_t)

    dhc, dln_g, dln_b, dconv_b = _conv_ln_bwd(f"conv_ln_bwd_{tag}", hc, ds, ln_g, ln_b)
    dzu, dconv_w = _conv_bwd(f"conv_bwd_{tag}", zu, dhc, conv_w)

    dz_parts = [dzqkv, dzu, dzg]
    dw_in = jnp.concatenate(
        [_mm_tn(f"mix_dwin_{tag}_{i}", h, dzp, tk=1024, tn=dzp.shape[1] // 2, tt=2048)
         for i, dzp in enumerate(dz_parts)], axis=1)
    dx_in, dgn, dscale, dshift = _proj_norm_bwd(f"mix_dh_{tag}", dz_parts, list(w_parts), x, dx, g, scale)
    grads = dict(w_in=dw_in, attn_wo=dw_attn, conv_w=dconv_w[:CONV_K], conv_b=dconv_b, conv_ln_g=dln_g,
                 conv_ln_b=dln_b, conv_wo=dw_conv_o, w_out=dw_out)
    return dx_in, grads, dgn, (dshift, dscale, dgate)


def _local_step(x, mod, target, w, wf):
    t = x.shape[0]
    cos_t, sin_t = _rope_tables(t)
    row = lambda v: v.reshape(1, -1)
    conv_w_pad = jnp.concatenate([wf["conv_w"], jnp.zeros((DEPTH, HALO - CONV_K, D), F32)], axis=1)

    saved = []
    for l in range(DEPTH):
        mods = [mod[l:l + 1, i * D:(i + 1) * D] for i in range(N_MOD)]
        gains = [row(wf["norm_g"][l, i]) for i in range(3)]
        lay = dict(mods=mods, gains=gains)

        lay["x0"] = x
        x, lay["h0"], lay["ffn0"] = _ffn_fwd(f"a_{l}", x, (gains[0], mods[1], mods[0]), w, (l, 0), mods[2])
        lay["x1"] = x
        x, lay["h1"], lay["mix"] = _mix_fwd(f"{l}", x, (gains[1], mods[4], mods[3]), w["w_in"][l], w["attn_wo"][l],
                                            conv_w_pad[l],
                                 row(wf["conv_b"][l]), row(wf["conv_ln_g"][l]), row(wf["conv_ln_b"][l]),
                                 w["conv_wo"][l], w["w_out"][l], mods[5], cos_t, sin_t)
        lay["x2"] = x
        x, lay["h2"], lay["ffn1"] = _ffn_fwd(f"b_{l}", x, (gains[2], mods[7], mods[6]), w, (l, 1), mods[8])
        saved.append(lay)

    dx, dfinal_g, loss_cols = _loss_bwd("loss_head", x, target, row(wf["final_g"]))

    ffn_grads = {n: jnp.zeros(w[n].shape, F32) for n in ("ffn_wg", "ffn_wu", "ffn_wd")}
    per_layer = []
    for l in reversed(range(DEPTH)):
        lay = saved[l]
        mods, gains = lay["mods"], lay["gains"]
        dx, ffn_grads, dgn2, dmod2 = _ffn_bwd(f"b_{l}", dx, lay["x2"], lay["h2"], lay["ffn1"], w, (l, 1),
                                              gains[2], mods[7], mods[8], ffn_grads)
        dx, gm, dgn1, dmod1 = _mix_bwd(f"{l}", dx, lay["x1"], lay["h1"], lay["mix"], w["attn_wo"][l],
                                       conv_w_pad[l], row(wf["conv_ln_g"][l]), row(wf["conv_ln_b"][l]),
                                       w["conv_wo"][l], w["w_out"][l], gains[1], mods[4], mods[5], cos_t, sin_t)
        dx, ffn_grads, dgn0, dmod0 = _ffn_bwd(f"a_{l}", dx, lay["x0"], lay["h0"], lay["ffn0"], w, (l, 0),
                                              gains[0], mods[1], mods[2], ffn_grads)
        g = dict(gm)
        g["dmod"] = jnp.concatenate(list(dmod0) + list(dmod1) + list(dmod2), axis=1)
        g["norm_g"] = [dgn0[0], dgn1[0], dgn2[0]]
        for name in ("conv_b", "conv_ln_g", "conv_ln_b"):
            g[name] = g[name][0]
        per_layer.append(g)
    per_layer.reverse()
    grads = {name: [per_layer[l][name] for l in range(DEPTH)] for name in per_layer[0]}
    grads["dmod"] = jnp.concatenate(grads["dmod"], axis=0)
    grads.update(ffn_grads)
    grads["final_g"] = dfinal_g[0]
    return loss_cols, dx, grads


def _split_bits(w):
    bits = lax.bitcast_convert_type(w, jnp.uint32)
    hi = lax.bitcast_convert_type((bits >> 16).astype(jnp.uint16), BF16)
    lo = lax.bitcast_convert_type((bits & 0xFFFF).astype(jnp.uint16), BF16)
    return hi, lo


def _join_bits(hi, lo):
    h = lax.bitcast_convert_type(hi, jnp.uint16).astype(jnp.uint32)
    l = lax.bitcast_convert_type(lo, jnp.uint16).astype(jnp.uint32)
    return lax.bitcast_convert_type((h << 16) | l, F32)


def _pack(parts, rows):
    out = []
    for p in parts:
        flat = p.reshape(-1)
        pad = -flat.shape[0] % LANES
        out.append(jnp.concatenate([flat, jnp.zeros((pad,), flat.dtype)]) if pad else flat)
    flat = jnp.concatenate(out)
    return jnp.concatenate([flat, jnp.zeros((rows * LANES - flat.shape[0],), flat.dtype)]).reshape(rows, LANES)


def _unpack(buf, shapes):
    out, row = [], 0
    for shape in shapes:
        size = 1
        for s in shape:
            size *= s
        rows = -(-size // LANES)
        out.append(buf[row:row + rows].reshape(-1)[:size].reshape(shape))
        row += rows
    return out


def _place():
    x, y, c = lax.axis_index("x"), lax.axis_index("y"), lax.axis_index("c")
    chips = [(1 - x, y), (x, 1 - y), (1 - x, 1 - y)]
    return x, y, c, chips


def _chip_index():
    return (2 * lax.axis_index("x") + lax.axis_index("y")).astype(jnp.int32)


HBM_SPEC = pl.BlockSpec(memory_space=pltpu.HBM)


def _gather_rows(name, block):
    m, n = block.shape

    def body(x_ref, out_ref, send_sems, recv_sems, local_sem):
        x, y, c, chips = _place()
        me, sibling = (x, y, c), (x, y, 1 - c)

        def rows(px, py, pc):
            return out_ref.at[pl.ds((4 * px + 2 * py + pc) * m, m), :]

        def copy(k, owner, to, src=None):
            return pltpu.make_async_remote_copy(
                src_ref=rows(*owner) if src is None else src, dst_ref=rows(*owner), send_sem=send_sems.at[k],
                recv_sem=recv_sems.at[k], device_id=to, device_id_type=MESH)

        mine = pltpu.make_async_copy(x_ref, rows(*me), local_sem)
        mine.start()
        first = [copy(0, me, sibling, src=x_ref)] + [copy(1 + j, me, (*chip, c), src=x_ref)
                                                     for j, chip in enumerate(chips)]
        for cp in first:
            cp.start()
        passed = [copy(4 + j, (*chip, c), sibling) for j, chip in enumerate(chips)]
        for j, chip in enumerate(chips):
            copy(1 + j, (*chip, c), me).wait_recv()
            passed[j].start()
        copy(0, sibling, me).wait_recv()
        for j, chip in enumerate(chips):
            copy(4 + j, (*chip, 1 - c), me).wait_recv()
        for cp in first + passed:
            cp.wait_send()
        mine.wait()

    whole = pl.BlockSpec(memory_space=pltpu.VMEM)
    return pl.pallas_call(
        body, out_shape=jax.ShapeDtypeStruct((N_DEV * m, n), block.dtype), in_specs=[whole], out_specs=whole,
        scratch_shapes=[pltpu.SemaphoreType.DMA((7,)), pltpu.SemaphoreType.DMA((7,)), pltpu.SemaphoreType.DMA],
        name=name,
    )(block)


def _gather_weights(arrays):
    n = len(arrays)

    def body(*refs):
        outs, send_sems, recv_sems = refs[n:2 * n], refs[2 * n], refs[2 * n + 1]
        x, y, c, chips = _place()
        me = 2 * x + y
        sibling = (x, y, 1 - c)
        there = [2 * chip[0] + chip[1] for chip in chips]

        def copy(a, k, chip, layer, to):
            piece = outs[a].at[chip, layer]
            return pltpu.make_async_remote_copy(
                src_ref=piece, dst_ref=piece, send_sem=send_sems.at[6 * a + k], recv_sem=recv_sems.at[6 * a + k],
                device_id=to, device_id_type=MESH)

        first = [copy(a, j, me, c, (*chip, c)) for a in range(n) for j, chip in enumerate(chips)]
        for cp in first:
            cp.start()
        passed = []
        for a in range(n):
            for j in range(3):
                copy(a, j, there[j], c, sibling).wait_recv()
                passed.append(copy(a, 3 + j, there[j], c, sibling))
                passed[-1].start()
        for a in range(n):
            for j in range(3):
                copy(a, 3 + j, there[j], 1 - c, sibling).wait_recv()
        for cp in first + passed:
            cp.wait_send()

    return pl.pallas_call(
        body, out_shape=[jax.ShapeDtypeStruct(a.shape, a.dtype) for a in arrays],
        in_specs=[HBM_SPEC] * n, out_specs=[HBM_SPEC] * n,
        scratch_shapes=[pltpu.SemaphoreType.DMA((6 * n,)), pltpu.SemaphoreType.DMA((6 * n,))],
        input_output_aliases={i: i for i in range(n)}, name="gather_weights",
    )(*arrays)


def _row_block(rows, cols):
    for cand in (512, 256, 128, 64, 32, 16):
        if rows % cand == 0 and cand * cols * 4 <= 2560 * 1024:
            return cand
    return rows


def _swap_layers(grads):
    n = len(grads)

    def body(*refs):
        g_refs, out_refs, send_sems, recv_sems = refs[:n], refs[n:2 * n], refs[2 * n], refs[2 * n + 1]
        x, y, c, _ = _place()
        copies = [pltpu.make_async_remote_copy(
            src_ref=g_refs[a].at[:, 1 - c], dst_ref=out_refs[a], send_sem=send_sems.at[a], recv_sem=recv_sems.at[a],
            device_id=(x, y, 1 - c), device_id_type=MESH) for a in range(n)]
        for cp in copies:
            cp.start()
        for cp in copies:
            cp.wait()

    return pl.pallas_call(
        body, out_shape=[jax.ShapeDtypeStruct((g.shape[0],) + g.shape[2:], F32) for g in grads],
        in_specs=[HBM_SPEC] * n, out_specs=[HBM_SPEC] * n,
        scratch_shapes=[pltpu.SemaphoreType.DMA((n,)), pltpu.SemaphoreType.DMA((n,))], name="swap_layers",
    )(*grads)


def _add_layers(name, grad, other):
    shards, _, rows, cols = grad.shape
    tr = _row_block(rows, cols)

    def body(c_ref, g_ref, o_ref, out_ref):
        out_ref[...] = (g_ref[...] + o_ref[...]).astype(BF16)

    c = lax.axis_index("c").astype(jnp.int32).reshape(1)
    grid_spec = pltpu.PrefetchScalarGridSpec(
        num_scalar_prefetch=1, grid=(shards, rows // tr),
        in_specs=[pl.BlockSpec((None, None, tr, cols), lambda k, i, c_ref: (k, c_ref[0], i, 0)),
                  pl.BlockSpec((None, tr, cols), lambda k, i, c_ref: (k, i, 0))],
        out_specs=pl.BlockSpec((None, tr, cols), lambda k, i, c_ref: (k, i, 0)))
    return pl.pallas_call(
        body, out_shape=jax.ShapeDtypeStruct((shards, rows, cols), BF16), grid_spec=grid_spec,
        name=name, compiler_params=_params(("parallel", "parallel")),
    )(c, grad, other)


def _scatter_chips(parts):
    n = len(parts)

    def body(*refs):
        p_refs, out_refs, send_sems, recv_sems = refs[:n], refs[n:2 * n], refs[2 * n], refs[2 * n + 1]
        x, y, c, chips = _place()
        me = 2 * x + y
        there = [2 * chip[0] + chip[1] for chip in chips]

        def copy(a, j, slot):
            return pltpu.make_async_remote_copy(
                src_ref=p_refs[a].at[there[j]], dst_ref=out_refs[a].at[slot], send_sem=send_sems.at[3 * a + j],
                recv_sem=recv_sems.at[3 * a + j], device_id=(*chips[j], c), device_id_type=MESH)

        sends = [copy(a, j, me) for a in range(n) for j in range(3)]
        for cp in sends:
            cp.start()
        for a in range(n):
            for j in range(3):
                copy(a, j, there[j]).wait_recv()
        for cp in sends:
            cp.wait_send()

    return pl.pallas_call(
        body, out_shape=[jax.ShapeDtypeStruct(p.shape, p.dtype) for p in parts],
        in_specs=[HBM_SPEC] * n, out_specs=[HBM_SPEC] * n,
        scratch_shapes=[pltpu.SemaphoreType.DMA((3 * n,)), pltpu.SemaphoreType.DMA((3 * n,))],
        name="scatter_chips",
    )(*parts)


def _add_chips(name, part, others):
    shards, rows, cols = part.shape
    tr = _row_block(rows, cols)

    def body(pos_ref, own_ref, r0_ref, r1_ref, r2_ref, r3_ref, out_ref):
        me = pos_ref[0]
        own = own_ref[...].astype(F32)
        total = None
        for k, r_ref in enumerate((r0_ref, r1_ref, r2_ref, r3_ref)):
            term = jnp.where(me == k, own, r_ref[...].astype(F32))
            total = term if total is None else total + term
        out_ref[...] = total

    def other(k):
        return pl.BlockSpec((None, tr, cols),
                            lambda i, pos, k=k: (jnp.where(pos[0] == k, (k + 1) % shards, k), i, 0))

    pos = jnp.stack([_chip_index(), lax.axis_index("c").astype(jnp.int32)])
    grid_spec = pltpu.PrefetchScalarGridSpec(
        num_scalar_prefetch=1, grid=(rows // tr,),
        in_specs=[pl.BlockSpec((None, tr, cols), lambda i, pos: (pos[0], i, 0))] + [other(k) for k in range(shards)],
        out_specs=pl.BlockSpec((None, tr, cols), lambda i, pos: (pos[1], i, 0)))
    return pl.pallas_call(
        body, out_shape=jax.ShapeDtypeStruct((DEPTH, rows, cols), F32), grid_spec=grid_spec,
        name=name, compiler_params=_params(("parallel",)),
    )(pos, part, others, others, others, others)


def _join_layers(arrays):
    n = len(arrays)

    def body(*refs):
        outs, send_sems, recv_sems = refs[n:2 * n], refs[2 * n], refs[2 * n + 1]
        x, y, c, _ = _place()

        def copy(a, layer):
            piece = outs[a].at[layer]
            return pltpu.make_async_remote_copy(src_ref=piece, dst_ref=piece, send_sem=send_sems.at[a],
                                                recv_sem=recv_sems.at[a], device_id=(x, y, 1 - c),
                                                device_id_type=MESH)

        sends = [copy(a, c) for a in range(n)]
        for cp in sends:
            cp.start()
        for a in range(n):
            copy(a, 1 - c).wait_recv()
        for cp in sends:
            cp.wait_send()

    return pl.pallas_call(
        body, out_shape=[jax.ShapeDtypeStruct(a.shape, a.dtype) for a in arrays],
        in_specs=[HBM_SPEC] * n, out_specs=[HBM_SPEC] * n,
        scratch_shapes=[pltpu.SemaphoreType.DMA((n,)), pltpu.SemaphoreType.DMA((n,))],
        input_output_aliases={i: i for i in range(n)}, name="join_layers",
    )(*arrays)


def _reduce_scatter(grads):
    sums = [_add_layers(f"add_layers_{a}", g, o) for a, (g, o) in enumerate(zip(grads, _swap_layers(grads)))]
    others = _scatter_chips(sums)
    return _join_layers([_add_chips(f"add_chips_{a}", p, o) for a, (p, o) in enumerate(zip(sums, others))])


def _adamw(name, w, g, m, v):
    shape = w.shape
    cols = shape[-1]
    rows = w.size // cols
    tr = rows
    for cand in (512, 256, 128, 64, 32, 16, 8):
        if rows % cand == 0 and cand * cols * 4 <= 2 * 1024 * 1024:
            tr = cand
            break

    def body(w_ref, g_ref, m_ref, v_ref, go_ref, d_ref, nm_ref, nv_ref):
        gv = g_ref[...]
        go_ref[...] = gv
        nm = ADAM_B1 * m_ref[...] + (1.0 - ADAM_B1) * gv
        nv = ADAM_B2 * v_ref[...] + (1.0 - ADAM_B2) * (gv * gv)
        m_hat = nm / (1.0 - ADAM_B1 ** ADAM_STEP)
        v_hat = nv / (1.0 - ADAM_B2 ** ADAM_STEP)
        d_ref[...] = -ADAM_LR * (m_hat / (jnp.sqrt(v_hat) + ADAM_EPS) + ADAM_WD * w_ref[...])
        nm_ref[...] = nm
        nv_ref[...] = nv

    spec = pl.BlockSpec((tr, cols), lambda i: (i, 0))
    two = lambda a: a.reshape(rows, cols)
    outs = pl.pallas_call(
        body, out_shape=[jax.ShapeDtypeStruct((rows, cols), F32)] * 4, grid=(rows // tr,),
        in_specs=[spec] * 4, out_specs=[spec] * 4, name=name, compiler_params=_params(("parallel",)),
    )(two(w), two(g), two(m), two(v))
    return [o.reshape(shape) for o in outs]


BIG = ("ffn_wg", "ffn_wu", "ffn_wd", "w_in", "conv_wo", "w_out")
MISC_ROWS = 96


def _own_slot(shard):
    return lax.dynamic_update_slice(jnp.zeros((SHARDS,) + shard.shape, shard.dtype), shard[None],
                                    (_chip_index(),) + (0,) * shard.ndim)


def _as_matrices(a):
    return a.reshape(a.shape[0], a.shape[1], -1, a.shape[-1])


def kernel(x, c, ada_w, ada_b, norm_g, ffn_wg, ffn_wu, ffn_wd, w_in, attn_wo, conv_w, conv_b, conv_ln_g, conv_ln_b, conv_wo, w_out, final_g, loss_target, m_ada_w, m_ada_b, m_norm_g, m_ffn_wg, m_ffn_wu, m_ffn_wd, m_w_in, m_attn_wo, m_conv_w, m_conv_b, m_conv_ln_g, m_conv_ln_b, m_conv_wo, m_w_out, m_final_g, v_ada_w, v_ada_b, v_norm_g, v_ffn_wg, v_ffn_wu, v_ffn_wd, v_w_in, v_attn_wo, v_conv_w, v_conv_b, v_conv_ln_g, v_conv_ln_b, v_conv_wo, v_w_out, v_final_g):
    weights = dict(ada_w=ada_w, ada_b=ada_b, norm_g=norm_g, ffn_wg=ffn_wg, ffn_wu=ffn_wu, ffn_wd=ffn_wd, w_in=w_in,
                   attn_wo=attn_wo, conv_w=conv_w, conv_b=conv_b, conv_ln_g=conv_ln_g, conv_ln_b=conv_ln_b,
                   conv_wo=conv_wo, w_out=w_out, final_g=final_g)
    moments_m = dict(ada_w=m_ada_w, ada_b=m_ada_b, norm_g=m_norm_g, ffn_wg=m_ffn_wg, ffn_wu=m_ffn_wu,
                     ffn_wd=m_ffn_wd, w_in=m_w_in, attn_wo=m_attn_wo, conv_w=m_conv_w, conv_b=m_conv_b,
                     conv_ln_g=m_conv_ln_g, conv_ln_b=m_conv_ln_b, conv_wo=m_conv_wo, w_out=m_w_out,
                     final_g=m_final_g)
    moments_v = dict(ada_w=v_ada_w, ada_b=v_ada_b, norm_g=v_norm_g, ffn_wg=v_ffn_wg, ffn_wu=v_ffn_wu,
                     ffn_wd=v_ffn_wd, w_in=v_w_in, attn_wo=v_attn_wo, conv_w=v_conv_w, conv_b=v_conv_b,
                     conv_ln_g=v_conv_ln_g, conv_ln_b=v_conv_ln_b, conv_wo=v_conv_wo, w_out=v_w_out,
                     final_g=v_final_g)
    layers, shards = range(DEPTH), range(SHARDS)

    bits = {n: _split_bits(weights[n]) for n in EXACT}
    misc_w = jnp.stack([_pack([attn_wo[l].astype(BF16), bits["norm_g"][0][l], bits["norm_g"][1][l],
                               bits["conv_w"][0][l], bits["conv_w"][1][l]], MISC_ROWS) for l in layers])
    sent = [_own_slot(weights[n].astype(BF16)) for n in BIG] + [_own_slot(misc_w)]
    got = dict(zip(BIG + ("misc",), _gather_weights(sent)))
    w = {n: got[n] for n in ("ffn_wg", "ffn_wu", "ffn_wd")}
    w["w_in"] = got["w_in"].transpose(1, 2, 0, 3).reshape(DEPTH, D, -1)
    for n in ("conv_wo", "w_out"):
        w[n] = got[n].transpose(1, 0, 2, 3).reshape(DEPTH, D, D)
    misc_shapes = [(GW, GW), (3, GW), (3, GW), (CONV_K, GW), (CONV_K, GW)]
    pieces = [[_unpack(got["misc"][k, l], misc_shapes) for k in shards] for l in layers]
    whole = lambda i: jnp.stack([jnp.concatenate([pieces[l][k][i] for k in shards], axis=1) for l in layers])
    w["attn_wo"] = whole(0)
    vectors = dict(ada_b=ada_b, conv_b=conv_b, conv_ln_g=conv_ln_g, conv_ln_b=conv_ln_b, final_g=final_g,
                   norm_g=_join_bits(whole(1), whole(2)), conv_w=_join_bits(whole(3), whole(4)))

    me = 2 * _chip_index() + lax.axis_index("c").astype(jnp.int32)
    pad_rows = lambda a, rows: jnp.concatenate([a, jnp.zeros((rows - a.shape[0], a.shape[1]), a.dtype)])
    c_all = _gather_rows("gather_c", pad_rows(c, SUBLANES)).reshape(N_DEV, SUBLANES, D)[:, 0]
    mod_cols = _mod_fwd("mod_fwd", c_all, ada_w.astype(BF16))
    by_dev = _gather_rows("gather_mod", mod_cols).reshape(N_DEV, N_DEV, DEPTH, -1)
    mine = lax.dynamic_index_in_dim(by_dev[0::2], me, axis=1, keepdims=False)
    mod = mine.transpose(1, 0, 2).reshape(DEPTH, -1) + ada_b

    loss_cols, dx, grads = _local_step(x[0], mod, loss_target[0], w, vectors)
    loss = lax.psum(jnp.sum(loss_cols), ("x", "y", "c"))

    dmod_rows = DEPTH * N_MOD * D // LANES
    dmod_all = _gather_rows("gather_dmod", pad_rows(grads["dmod"].reshape(dmod_rows, LANES), 3 * SUBLANES))
    dmod_all = dmod_all.reshape(N_DEV, 3 * SUBLANES, LANES)[:, :dmod_rows].reshape(N_DEV, DEPTH, -1)
    grad_ada_b = _sum_devices("ada_b_grad", dmod_all.reshape(N_DEV, -1)).reshape(DEPTH, -1)
    cols = ada_w.shape[-1]
    dmod_cols = lax.dynamic_slice_in_dim(dmod_all, _chip_index() * cols, cols, axis=2).transpose(1, 0, 2)
    grad_ada_w = _mod_bwd("mod_bwd", c_all.T, dmod_cols)

    cols_of = lambda a, k: a[..., k * GW:(k + 1) * GW]
    misc_g = jnp.stack([jnp.stack([_pack(
        [cols_of(grads["attn_wo"][l], k), cols_of(jnp.stack(grads["norm_g"][l]), k), cols_of(grads["conv_w"][l], k),
         grads["conv_b"][l], grads["conv_ln_g"][l], grads["conv_ln_b"][l],
         grads["final_g"] if l == 0 else jnp.zeros_like(grads["final_g"])], MISC_ROWS)
        for l in layers]) for k in shards])
    by_chip = dict(
        ffn_wg=grads["ffn_wg"], ffn_wu=grads["ffn_wu"], ffn_wd=grads["ffn_wd"],
        w_in=jnp.stack(grads["w_in"]).reshape(DEPTH, D, SHARDS, -1).transpose(2, 0, 1, 3),
        conv_wo=jnp.stack(grads["conv_wo"]).reshape(DEPTH, SHARDS, -1, D).transpose(1, 0, 2, 3),
        w_out=jnp.stack(grads["w_out"]).reshape(DEPTH, SHARDS, -1, D).transpose(1, 0, 2, 3))
    reduced = _reduce_scatter([_as_matrices(by_chip[n]) for n in BIG] + [misc_g])
    summed = {n: r.reshape(weights[n].shape) for n, r in zip(BIG, reduced)}
    small_shapes = [(GW, GW), (3, GW), (CONV_K, GW), (D,), (D,), (D,), (D,)]
    small = [_unpack(reduced[-1][l], small_shapes) for l in layers]
    for i, n in enumerate(("attn_wo", "norm_g", "conv_w", "conv_b", "conv_ln_g", "conv_ln_b")):
        summed[n] = jnp.stack([small[l][i] for l in layers])
    summed["final_g"] = small[0][6]
    summed["ada_w"], summed["ada_b"] = grad_ada_w, grad_ada_b

    grad_out, deltas, new_m, new_v = {}, {}, {}, {}
    for n in WEIGHTS:
        grad_out[n], deltas[n], new_m[n], new_v[n] = _adamw(f"adamw_{n}", weights[n], summed[n], moments_m[n],
                                                           moments_v[n])

    return (loss, dx[None], *[grad_out[n] for n in WEIGHTS], *[deltas[n] for n in WEIGHTS],
            *[new_m[n] for n in WEIGHTS], *[new_v[n] for n in WEIGHTS])
```

```python
import functools

import jax
import jax.numpy as jnp
from jax import lax
from jax.experimental import pallas as pl
from jax.experimental.pallas import tpu as pltpu

F32 = jnp.float32
BF16 = jnp.bfloat16

D = 1024
DFF = 2816
HEAD = 64
GW = 256
DILATIONS = (1, 4, 16)
BAND = 128
QKV = 2304
CONV_K = 31
HALO = 32
N_MOD = 9
EPS = 1e-6
NEG_INF = -1e30
DEPTH = 2

SHARDS = 4
FSH = DFF // SHARDS
LANES = 1024
VL = 128

ADAM_LR = 0.001
ADAM_B1 = 0.9
ADAM_B2 = 0.999
ADAM_EPS = 1e-08
ADAM_WD = 0.01
ADAM_STEP = 10

VMEM_LIMIT = 56 * 1024 * 1024

EXACT = ("norm_g", "conv_w")
WEIGHTS = ("ada_w", "ada_b", "norm_g", "ffn_wg", "ffn_wu", "ffn_wd", "w_in", "attn_wo", "conv_w", "conv_b",
           "conv_ln_g", "conv_ln_b", "conv_wo", "w_out", "final_g")

MESH = pl.DeviceIdType.MESH


def _params(sem=None):
    return pltpu.CompilerParams(dimension_semantics=sem, vmem_limit_bytes=VMEM_LIMIT)


def _sigmoid(v):
    return jax.nn.sigmoid(v)


MM_SLAB = 256


def _mm_tn(name, a, b, *, tk, tn, tt, a_fn=None, a_tiled=False, b_tiled=False, into=None):
    a_list = list(a) if a_fn is not None else [a]
    na = len(a_list)
    t = a_list[0].shape[-2]
    nk = a_list[0].shape[0] if a_tiled else a_list[0].shape[1] // tk
    nn = b.shape[0] if b_tiled else b.shape[1] // tn
    steps = t // tt
    has_into = into is not None

    def body(*refs):
        refs = refs[1:] if has_into else refs
        a_refs, b_ref, o_ref, acc_ref = refs[:na], refs[na], refs[na + 1], refs[na + 2]
        s = pl.program_id(2)

        @pl.when(s == 0)
        def _():
            acc_ref[...] = jnp.zeros_like(acc_ref)

        av = a_refs[0][...] if a_fn is None else a_fn([r[...] for r in a_refs])
        acc_ref[...] += lax.dot_general(av, b_ref[...], (((0,), (0,)), ((), ())), preferred_element_type=F32)

        @pl.when(s == steps - 1)
        def _():
            o_ref[...] = acc_ref[...]

    a_spec = (pl.BlockSpec((None, tt, tk), lambda i, j, s: (i, s, 0)) if a_tiled
              else pl.BlockSpec((tt, tk), lambda i, j, s: (s, i)))
    b_spec = (pl.BlockSpec((None, tt, tn), lambda i, j, s: (j, s, 0)) if b_tiled
              else pl.BlockSpec((tt, tn), lambda i, j, s: (s, j)))
    if a_tiled:
        out_dims, tile_index = (nk, tk, nn * tn), lambda i, j, s: (i, 0, j)
    elif b_tiled:
        out_dims, tile_index = (nn, nk * tk, tn), lambda i, j, s: (j, i, 0)
    else:
        out_dims, tile_index = (nk * tk, nn * tn), lambda i, j, s: (i, j)
    tiled = a_tiled or b_tiled
    if has_into:
        buf, lead = into
        def out_index(i, j, s):
            idx = tile_index(i, j, s)
            return (idx[0], *lead, *idx[1:])
        out_spec = pl.BlockSpec((None,) * (1 + len(lead)) + (tk, tn), out_index)
        out_shape = jax.ShapeDtypeStruct(buf.shape, buf.dtype)
        extra_in, extra_specs, aliases = [buf], [pl.BlockSpec(memory_space=pl.ANY)], {0: 0}
    else:
        out_spec = pl.BlockSpec(((None,) if tiled else ()) + (tk, tn), tile_index)
        out_shape = jax.ShapeDtypeStruct(out_dims, F32)
        extra_in, extra_specs, aliases = [], [], {}
    return pl.pallas_call(
        body, out_shape=out_shape, grid=(nk, nn, steps), in_specs=extra_specs + [a_spec] * na + [b_spec],
        out_specs=out_spec, scratch_shapes=[pltpu.VMEM((tk, tn), F32)], input_output_aliases=aliases, name=name,
        compiler_params=_params(("parallel", "parallel", "arbitrary")),
    )(*extra_in, *a_list, b)


def _row_spec(tm, width, col=0):
    return pl.BlockSpec((tm, width), functools.partial(lambda i, col: (i, col), col=col))


def _vec_spec(width):
    return pl.BlockSpec((1, width), lambda i: (0, 0))


def _proj_norm_bwd(name, dz_parts, w_parts, x, dres, g, scale, tm=256):
    t = x.shape[0]
    steps = t // tm
    n = len(dz_parts)

    def body(*refs):
        dz_refs, w_refs = refs[:n], refs[n:2 * n]
        x_ref, dres_ref, g_ref, sc_ref, dx_ref, dg_ref, dsc_ref, dsh_ref = refs[2 * n:]
        i = pl.program_id(0)

        @pl.when(i == 0)
        def _():
            dg_ref[...] = jnp.zeros_like(dg_ref)
            dsh_ref[...] = jnp.zeros_like(dsh_ref)

        dh = None
        for dz_ref, w_ref in zip(dz_refs, w_refs):
            part = lax.dot_general(dz_ref[...], w_ref[...], (((1,), (1,)), ((), ())), preferred_element_type=F32)
            dh = part if dh is None else dh + part
        xv = x_ref[...]
        r = lax.rsqrt(jnp.mean(xv * xv, axis=-1, keepdims=True) + EPS)
        xh = xv * r
        dxh = dh * (g_ref[...] * (1.0 + sc_ref[...]))
        dx_ref[...] = dres_ref[...] + r * (dxh - xh * jnp.mean(dxh * xh, axis=-1, keepdims=True))
        dg_ref[...] += jnp.sum(dh * xh, axis=0, keepdims=True)
        dsh_ref[...] += jnp.sum(dh, axis=0, keepdims=True)

        @pl.when(i == steps - 1)
        def _():
            acc = dg_ref[...]
            dg_ref[...] = acc * (1.0 + sc_ref[...])
            dsc_ref[...] = acc * g_ref[...]

    vec = jax.ShapeDtypeStruct((1, D), F32)
    resident = [pl.BlockSpec(wp.shape, lambda i: (0, 0), pipeline_mode=pl.Buffered(1)) for wp in w_parts]
    return pl.pallas_call(
        body, out_shape=[jax.ShapeDtypeStruct((t, D), F32), vec, vec, vec], grid=(steps,),
        in_specs=[_row_spec(tm, dz.shape[1]) for dz in dz_parts] + resident
        + [_row_spec(tm, D), _row_spec(tm, D), _vec_spec(D), _vec_spec(D)],
        out_specs=[_row_spec(tm, D), _vec_spec(D), _vec_spec(D), _vec_spec(D)],
        name=name, compiler_params=_params(("arbitrary",)),
    )(*dz_parts, *w_parts, x, dres, g, scale)


def _loss_bwd(name, x, target, g, tm=256):
    t = x.shape[0]

    def body(x_ref, t_ref, g_ref, dx_ref, dg_ref, loss_ref):
        @pl.when(pl.program_id(0) == 0)
        def _():
            dg_ref[...] = jnp.zeros_like(dg_ref)
            loss_ref[...] = jnp.zeros_like(loss_ref)

        xv = x_ref[...]
        r = lax.rsqrt(jnp.mean(xv * xv, axis=-1, keepdims=True) + EPS)
        xh = xv * r
        err = xh * g_ref[...] - t_ref[...]
        dy = err * (1.0 / D)
        dxh = dy * g_ref[...]
        dx_ref[...] = r * (dxh - xh * jnp.mean(dxh * xh, axis=-1, keepdims=True))
        dg_ref[...] += jnp.sum(dy * xh, axis=0, keepdims=True)
        loss_ref[...] += jnp.sum(err * err, axis=0, keepdims=True) * (0.5 / D)

    vec = jax.ShapeDtypeStruct((1, D), F32)
    return pl.pallas_call(
        body, out_shape=[jax.ShapeDtypeStruct((t, D), F32), vec, vec], grid=(t // tm,),
        in_specs=[_row_spec(tm, D), _row_spec(tm, D), _vec_spec(D)],
        out_specs=[_row_spec(tm, D), _vec_spec(D), _vec_spec(D)],
        name=name, compiler_params=_params(("arbitrary",)),
    )(x, target, g)


N_DEV = 8


def _mod_fwd(name, c_all, ada_w):
    cols = ada_w.shape[-1]

    def body(c_ref, w_ref, o_ref):
        cv = c_ref[...]
        ca = (cv * _sigmoid(cv)).astype(BF16)
        o_ref[...] = jnp.dot(ca, w_ref[...], preferred_element_type=F32)

    return pl.pallas_call(
        body, out_shape=jax.ShapeDtypeStruct((N_DEV, DEPTH * cols), F32), grid=(DEPTH,),
        in_specs=[pl.BlockSpec((N_DEV, D), lambda l: (0, 0)), pl.BlockSpec((None, D, cols), lambda l: (l, 0, 0))],
        out_specs=pl.BlockSpec((N_DEV, cols), lambda l: (0, l)), name=name, compiler_params=_params(("parallel",)),
    )(c_all, ada_w)


def _mod_bwd(name, c_cols, dmods, tk=256):
    cols = dmods.shape[-1]

    def body(c_ref, d_ref, o_ref):
        cv = c_ref[...]
        ca = cv * _sigmoid(cv)
        total = ca[:, 0:1] * d_ref[0:1, :]
        for b in range(1, N_DEV):
            total = total + ca[:, b:b + 1] * d_ref[b:b + 1, :]
        o_ref[...] = total

    return pl.pallas_call(
        body, out_shape=jax.ShapeDtypeStruct((DEPTH, D, cols), F32), grid=(DEPTH, D // tk),
        in_specs=[pl.BlockSpec((tk, N_DEV), lambda l, i: (i, 0)), pl.BlockSpec((None, N_DEV, cols), lambda l, i: (l, 0, 0))],
        out_specs=pl.BlockSpec((None, tk, cols), lambda l, i: (l, i, 0)), name=name,
        compiler_params=_params(("parallel", "parallel")),
    )(c_cols, dmods)


def _sum_devices(name, rows):
    n = rows.shape[1]

    def body(r_ref, o_ref):
        total = r_ref[0:1, :]
        for b in range(1, N_DEV):
            total = total + r_ref[b:b + 1, :]
        o_ref[...] = total

    whole = pl.BlockSpec(memory_space=pltpu.VMEM)
    return pl.pallas_call(body, out_shape=jax.ShapeDtypeStruct((1, n), F32), in_specs=[whole], out_specs=whole,
                          name=name)(rows)


def _rope_tables(t):
    half = HEAD // 2
    inv_freq = 10000.0 ** (-(jnp.arange(half, dtype=F32) * 2.0 / HEAD))
    ang = jnp.arange(t, dtype=F32)[:, None] * inv_freq[None, :]
    cos, sin = jnp.cos(ang), jnp.sin(ang)
    cos_t = jnp.tile(jnp.concatenate([cos, cos], axis=1), (1, VL // HEAD))
    sin_t = jnp.tile(jnp.concatenate([-sin, sin], axis=1), (1, VL // HEAD))
    return cos_t, sin_t


def _rotate(tv, cos, sin_signed):
    lane = lax.broadcasted_iota(jnp.int32, tv.shape, 1)
    first = (lane % HEAD) < (HEAD // 2)
    partner = jnp.where(first, pltpu.roll(tv, tv.shape[1] - HEAD // 2, 1), pltpu.roll(tv, HEAD // 2, 1))
    return tv * cos + partner * sin_signed


def _dilated_spec(tm, d):
    return pl.BlockSpec((tm // d, d * GW), lambda i: (i, 0))


def _dilated_shape(t, d, dtype):
    return jax.ShapeDtypeStruct((t // d, d * GW), dtype)


def _every(d, r, tm):
    return pl.ds(r, tm // d, stride=d) if d > 1 else slice(None)


def _rope_bwd(name, grads, cos_t, sin_t, tm=512):
    ng = len(DILATIONS)
    n = len(grads)
    t = grads[0].shape[0] * DILATIONS[0]

    halves = GW // VL

    def body(*refs):
        g_refs, cos_ref, sin_ref, o_ref, rows_ref = refs[:n], refs[n], refs[n + 1], refs[n + 2], refs[n + 3]
        cos, sin = cos_ref[...], -sin_ref[...]
        for idx in range(n):
            d = DILATIONS[idx % ng]
            for hh in range(halves):
                for r in range(d):
                    cols = slice(r * GW + hh * VL, r * GW + (hh + 1) * VL)
                    rows_ref[hh, _every(d, r, tm), :] = g_refs[idx][:, cols].astype(F32)
                piece = rows_ref[hh]
                if idx < 2 * ng:
                    piece = _rotate(piece, cos, sin)
                if idx < ng:
                    piece = piece * (HEAD ** -0.5)
                o_ref[:, idx * GW + hh * VL:idx * GW + (hh + 1) * VL] = piece.astype(BF16)

    dils = [DILATIONS[idx % ng] for idx in range(n)]
    return pl.pallas_call(
        body, out_shape=jax.ShapeDtypeStruct((t, n * GW), BF16), grid=(t // tm,),
        in_specs=[_dilated_spec(tm, d) for d in dils] + [_row_spec(tm, VL)] * 2, out_specs=_row_spec(tm, n * GW),
        scratch_shapes=[pltpu.VMEM((halves, tm, VL), F32)], name=name, compiler_params=_params(("parallel",)),
    )(*grads, cos_t, sin_t)


def _head_cols(h):
    return slice(h * HEAD, (h + 1) * HEAD)


def _band_mask_q(has_prev):
    qi = lax.broadcasted_iota(jnp.int32, (BAND, 2 * BAND), 0)
    kj = lax.broadcasted_iota(jnp.int32, (BAND, 2 * BAND), 1)
    dist = qi + BAND - kj
    return (dist >= 0) & (dist <= BAND) & ((kj >= BAND) | has_prev)


def _attn_fwd(name, q, k, v, group):
    d = DILATIONS[group]
    length = q.shape[0]
    qb = min(512, length)
    sub = qb // BAND
    nblk = length // qb

    def body(q_ref, kc_ref, kp_ref, vc_ref, vp_ref, o_ref, lse_ref):
        blk = pl.program_id(1)
        k_ext = jnp.concatenate([kp_ref[...], kc_ref[...]], axis=0)
        v_ext = jnp.concatenate([vp_ref[...], vc_ref[...]], axis=0)
        for j in range(sub):
            mask = _band_mask_q((blk * sub + j) > 0)
            qj = q_ref[j * BAND:(j + 1) * BAND, :]
            kj = k_ext[j * BAND:(j + 2) * BAND, :]
            vj = v_ext[j * BAND:(j + 2) * BAND, :]
            outs, lses = [], []
            for h in range(GW // HEAD):
                s = lax.dot_general(qj[:, _head_cols(h)], kj[:, _head_cols(h)], (((1,), (1,)), ((), ())),
                                    preferred_element_type=F32)
                s = jnp.where(mask, s, NEG_INF)
                m = jnp.max(s, axis=-1, keepdims=True)
                p = jnp.exp(s - m)
                den = jnp.sum(p, axis=-1, keepdims=True)
                o = jnp.dot(p.astype(BF16), vj[:, _head_cols(h)], preferred_element_type=F32)
                outs.append(o / den)
                lses.append(jnp.broadcast_to(m + jnp.log(den), (BAND, HEAD)))
            o_ref[j * BAND:(j + 1) * BAND, :] = jnp.concatenate(outs, axis=1)
            lse_ref[j * BAND:(j + 1) * BAND, :] = jnp.concatenate(lses, axis=1)

    prev = qb // BAND
    cur = lambda r, b: (b, r)
    before = lambda r, b: (jnp.maximum(b * prev - 1, 0), r)
    big, halo = pl.BlockSpec((qb, GW), cur), pl.BlockSpec((BAND, GW), before)
    return pl.pallas_call(
        body, out_shape=[jax.ShapeDtypeStruct((length, d * GW), F32)] * 2, grid=(d, nblk),
        in_specs=[big, big, halo, big, halo], out_specs=[big] * 2, name=name,
        compiler_params=_params(("parallel", "parallel")),
    )(q, k, k, v, v)


def _attn_bwd(name, q, k, v, do, lj, dsum, group):
    d = DILATIONS[group]
    length = q.shape[0]
    qb = min(512, length)
    sub = qb // BAND
    nblk = length // qb
    total = length // BAND

    def body(qc_ref, qn_ref, kc_ref, kp_ref, vc_ref, vp_ref, doc_ref, don_ref, ljc_ref, ljn_ref, dsc_ref, dsn_ref,
             dq_ref, dk_ref, dv_ref):
        blk = pl.program_id(1)
        k_ext = jnp.concatenate([kp_ref[...], kc_ref[...]], axis=0)
        v_ext = jnp.concatenate([vp_ref[...], vc_ref[...]], axis=0)
        heads = range(GW // HEAD)
        nt = (((1,), (1,)), ((), ()))
        tn = (((0,), (0,)), ((), ()))

        def scores(qh, doh, ljh, dsh, kh, vh, mask):
            s = lax.dot_general(qh, kh, nt, preferred_element_type=F32)
            p = jnp.where(mask, jnp.exp(s - ljh), 0.0)
            dp = lax.dot_general(doh, vh, nt, preferred_element_type=F32)
            return p.astype(BF16), (p * (dp - dsh)).astype(BF16)

        held_k, held_v = [None] * len(heads), [None] * len(heads)
        for j in range(sub):
            rows = slice(j * BAND, (j + 1) * BAND)
            rows2 = slice(j * BAND, (j + 2) * BAND)
            mask = _band_mask_q((blk * sub + j) > 0)
            dqs, dks, dvs = [], [], []
            for h in heads:
                hc = _head_cols(h)
                col = slice(h * HEAD, h * HEAD + 1)
                qh, doh, kh2 = qc_ref[rows, hc], doc_ref[rows, hc], k_ext[rows2, hc]
                p, ds = scores(qh, doh, ljc_ref[rows, col], dsc_ref[rows, col], kh2, v_ext[rows2, hc], mask)
                dqs.append(jnp.dot(ds, kh2, preferred_element_type=F32))
                dk2 = lax.dot_general(ds, qh, tn, preferred_element_type=F32)
                dv2 = lax.dot_general(p, doh, tn, preferred_element_type=F32)
                if j > 0:
                    dks.append(held_k[h] + dk2[:BAND])
                    dvs.append(held_v[h] + dv2[:BAND])
                held_k[h], held_v[h] = dk2[BAND:], dv2[BAND:]
            dq_ref[rows, :] = jnp.concatenate(dqs, axis=1)
            if j > 0:
                done = slice((j - 1) * BAND, j * BAND)
                dk_ref[done, :] = jnp.concatenate(dks, axis=1)
                dv_ref[done, :] = jnp.concatenate(dvs, axis=1).astype(BF16)

        last = slice((sub - 1) * BAND, sub * BAND)
        qi = lax.broadcasted_iota(jnp.int32, (BAND, BAND), 0)
        kj = lax.broadcasted_iota(jnp.int32, (BAND, BAND), 1)
        mask = (kj >= qi) & ((blk + 1) * sub < total)
        dks, dvs = [], []
        for h in heads:
            hc = _head_cols(h)
            col = slice(h * HEAD, h * HEAD + 1)
            qh, doh = qn_ref[:, hc], don_ref[:, hc]
            p, ds = scores(qh, doh, ljn_ref[:, col], dsn_ref[:, col], kc_ref[last, hc], vc_ref[last, hc], mask)
            dks.append(held_k[h] + lax.dot_general(ds, qh, tn, preferred_element_type=F32))
            dvs.append(held_v[h] + lax.dot_general(p, doh, tn, preferred_element_type=F32))
        dk_ref[last, :] = jnp.concatenate(dks, axis=1)
        dv_ref[last, :] = jnp.concatenate(dvs, axis=1).astype(BF16)

    prev = qb // BAND
    cur = lambda r, b: (b, r)
    before = lambda r, b: (jnp.maximum(b * prev - 1, 0), r)
    after = lambda r, b: (jnp.minimum((b + 1) * prev, total - 1), r)
    big = pl.BlockSpec((qb, GW), cur)
    nxt = pl.BlockSpec((BAND, GW), after)
    prv = pl.BlockSpec((BAND, GW), before)
    return pl.pallas_call(
        body, out_shape=[jax.ShapeDtypeStruct((length, d * GW), F32), jax.ShapeDtypeStruct((length, d * GW), F32),
                         jax.ShapeDtypeStruct((length, d * GW), BF16)], grid=(d, nblk),
        in_specs=[big, nxt, big, prv, big, prv, big, nxt, big, nxt, big, nxt],
        out_specs=[big] * 3, name=name, compiler_params=_params(("parallel", "parallel")),
    )(q, q, k, k, v, v, do, do, lj, lj, dsum, dsum)


SUBLANES = 8
CONV_CHUNK = 32
SHIFT_ROWS = HALO - SUBLANES


def _shifted_copies(buf_ref, sh_ref, tm):
    for s in range(1, SUBLANES):
        sh_ref[s - 1] = buf_ref[s:s + tm + SHIFT_ROWS, :]


def _window(buf_ref, sh_ref, offset, r0, rows):
    tiles, shift = divmod(offset, SUBLANES)
    src = buf_ref if shift == 0 else sh_ref.at[shift - 1]
    return src[pl.ds(pl.multiple_of(r0 + tiles * SUBLANES, SUBLANES), rows), :]


def _conv_fwd(name, zu, conv_w, conv_b, ln_g, ln_b, tm=256):
    t = zu.shape[0]
    per = tm // HALO

    def body(a_ref, gl_ref, ah_ref, glh_ref, w_ref, b_ref, g_ref, beta_ref, hc_ref, s_ref, ext_ref, sh_ref):
        i = pl.program_id(0)
        halo = ah_ref[...] * _sigmoid(glh_ref[...])
        ext_ref[0:HALO, :] = jnp.where(i > 0, halo, 0.0)
        ext_ref[HALO:, :] = a_ref[...] * _sigmoid(gl_ref[...])
        _shifted_copies(ext_ref, sh_ref, tm)

        def chunk(r, carry):
            r0 = pl.multiple_of(r * CONV_CHUNK, CONV_CHUNK)
            part = jnp.broadcast_to(b_ref[...], (CONV_CHUNK, D))
            for kk in range(CONV_K):
                part = part + w_ref[kk:kk + 1, :] * _window(ext_ref, sh_ref, HALO - CONV_K + 1 + kk, r0, CONV_CHUNK)
            hc_ref[pl.ds(r0, CONV_CHUNK), :] = part
            return carry

        lax.fori_loop(0, tm // CONV_CHUNK, chunk, 0)
        acc = hc_ref[...]
        mu = jnp.mean(acc, axis=-1, keepdims=True)
        xc = acc - mu
        var = jnp.mean(xc * xc, axis=-1, keepdims=True)
        ln = xc * lax.rsqrt(var + EPS) * g_ref[...] + beta_ref[...]
        s_ref[...] = (ln * _sigmoid(ln)).astype(BF16)

    halo_map = lambda col: (lambda i: (jnp.maximum(i * per - 1, 0), col))
    return pl.pallas_call(
        body, out_shape=[jax.ShapeDtypeStruct((t, D), F32), jax.ShapeDtypeStruct((t, D), BF16)], grid=(t // tm,),
        in_specs=[_row_spec(tm, D, 0), _row_spec(tm, D, 1), pl.BlockSpec((HALO, D), halo_map(0)),
                  pl.BlockSpec((HALO, D), halo_map(1)), pl.BlockSpec((HALO, D), lambda i: (0, 0)),
                  _vec_spec(D), _vec_spec(D), _vec_spec(D)],
        out_specs=[_row_spec(tm, D), _row_spec(tm, D)],
        scratch_shapes=[pltpu.VMEM((tm + HALO, D), F32), pltpu.VMEM((SUBLANES - 1, tm + SHIFT_ROWS, D), F32)],
        name=name, compiler_params=_params(("parallel",)),
    )(zu, zu, zu, zu, conv_w, conv_b, ln_g, ln_b)


def _ln_swish_bwd(hv, dsv, g, beta):
    mu = jnp.mean(hv, axis=-1, keepdims=True)
    xc = hv - mu
    rstd = lax.rsqrt(jnp.mean(xc * xc, axis=-1, keepdims=True) + EPS)
    xh = xc * rstd
    ln = xh * g + beta
    sg = _sigmoid(ln)
    dln = dsv * (sg * (1.0 + ln * (1.0 - sg)))
    dxh = dln * g
    dh = rstd * (dxh - jnp.mean(dxh, axis=-1, keepdims=True) - xh * jnp.mean(dxh * xh, axis=-1, keepdims=True))
    return dh, dln, xh


def _conv_bwd(name, zu, hc, ds, conv_w, ln_g, ln_b, tm=256):
    t = zu.shape[0]
    per = tm // HALO
    steps = t // tm

    group = 4

    def body(a_ref, gl_ref, ah_ref, glh_ref, hc_ref, hcn_ref, ds_ref, dsn_ref, w_ref, g_ref, beta_ref,
             dz_ref, dw_ref, dg_ref, dbeta_ref, dbias_ref, ext_ref, sh_ref, dext_ref, dsh_ref, sg_ref, part_ref):
        i = pl.program_id(0)

        @pl.when(i == 0)
        def _():
            part_ref[...] = jnp.zeros_like(part_ref)
            dg_ref[...] = jnp.zeros_like(dg_ref)
            dbeta_ref[...] = jnp.zeros_like(dbeta_ref)
            dbias_ref[...] = jnp.zeros_like(dbias_ref)

        sg_ref[...] = _sigmoid(gl_ref[...])
        ext_ref[0:HALO, :] = jnp.where(i > 0, ah_ref[...] * _sigmoid(glh_ref[...]), 0.0)
        ext_ref[HALO:, :] = a_ref[...] * sg_ref[...]
        dh, dln, xh = _ln_swish_bwd(hc_ref[...], ds_ref[...], g_ref[...], beta_ref[...])
        dext_ref[0:tm, :] = dh
        dg_ref[...] += jnp.sum(dln * xh, axis=0, keepdims=True)
        dbeta_ref[...] += jnp.sum(dln, axis=0, keepdims=True)
        dbias_ref[...] += jnp.sum(dh, axis=0, keepdims=True)
        dh_next, _, _ = _ln_swish_bwd(hcn_ref[...], dsn_ref[...], g_ref[...], beta_ref[...])
        dext_ref[tm:, :] = jnp.where(i < steps - 1, dh_next, 0.0)
        _shifted_copies(ext_ref, sh_ref, tm)
        _shifted_copies(dext_ref, dsh_ref, tm)

        def chunk(r, carry):
            r0 = pl.multiple_of(r * CONV_CHUNK, CONV_CHUNK)
            rows = pl.ds(r0, CONV_CHUNK)
            part = jnp.zeros((CONV_CHUNK, D), F32)
            for kk in range(CONV_K):
                part = part + w_ref[kk:kk + 1, :] * _window(dext_ref, dsh_ref, CONV_K - 1 - kk, r0, CONV_CHUNK)
            sg = sg_ref[rows, :]
            dz_ref[rows, 0:D] = (part * sg).astype(BF16)
            dz_ref[rows, D:] = (part * a_ref[rows, :] * sg * (1.0 - sg)).astype(BF16)
            return carry

        lax.fori_loop(0, tm // CONV_CHUNK, chunk, 0)

        for k0 in range(0, CONV_K, group):
            taps = range(k0, min(k0 + group, CONV_K))

            def tile(r, parts, taps=taps):
                r0 = pl.multiple_of(r * CONV_CHUNK, CONV_CHUNK)
                dv = dext_ref[pl.ds(r0, CONV_CHUNK), :]
                out = []
                for p, kk in zip(parts, taps):
                    prod = dv * _window(ext_ref, sh_ref, HALO - CONV_K + 1 + kk, r0, CONV_CHUNK)
                    for s in range(0, CONV_CHUNK, SUBLANES):
                        p = p + prod[s:s + SUBLANES, :]
                    out.append(p)
                return tuple(out)

            parts = lax.fori_loop(0, tm // CONV_CHUNK, tile, tuple(jnp.zeros((SUBLANES, D), F32) for _ in taps))
            for p, kk in zip(parts, taps):
                part_ref[kk * SUBLANES:(kk + 1) * SUBLANES, :] += p

        @pl.when(i == steps - 1)
        def _():
            for kk in range(HALO):
                dw_ref[kk:kk + 1, :] = jnp.sum(part_ref[kk * SUBLANES:(kk + 1) * SUBLANES, :], axis=0, keepdims=True)

    halo_map = lambda col: (lambda i: (jnp.maximum(i * per - 1, 0), col))
    next_rows = pl.BlockSpec((HALO, D), lambda i: (jnp.minimum((i + 1) * per, t // HALO - 1), 0))
    shifted = pltpu.VMEM((SUBLANES - 1, tm + SHIFT_ROWS, D), F32)
    vec = jax.ShapeDtypeStruct((1, D), F32)
    return pl.pallas_call(
        body, out_shape=[jax.ShapeDtypeStruct((t, 2 * D), BF16), jax.ShapeDtypeStruct((HALO, D), F32), vec, vec, vec],
        grid=(steps,),
        in_specs=[_row_spec(tm, D, 0), _row_spec(tm, D, 1), pl.BlockSpec((HALO, D), halo_map(0)),
                  pl.BlockSpec((HALO, D), halo_map(1)), _row_spec(tm, D), next_rows, _row_spec(tm, D), next_rows,
                  pl.BlockSpec((HALO, D), lambda i: (0, 0)), _vec_spec(D), _vec_spec(D)],
        out_specs=[_row_spec(tm, 2 * D), pl.BlockSpec((HALO, D), lambda i: (0, 0))] + [_vec_spec(D)] * 3,
        scratch_shapes=[pltpu.VMEM((tm + HALO, D), F32), shifted, pltpu.VMEM((tm + HALO, D), F32), shifted,
                        pltpu.VMEM((tm, D), F32), pltpu.VMEM((HALO * SUBLANES, D), F32)],
        name=name, compiler_params=_params(("arbitrary",)),
    )(zu, zu, zu, zu, hc, hc, ds, ds, conv_w, ln_g, ln_b)


NT = (((1,), (1,)), ((), ()))


def _resident(w, at):
    block = (None,) * (1 + len(at)) + tuple(w.shape[-2:])
    return [pl.BlockSpec(block, functools.partial(lambda i, k: (k, *at, 0, 0), k=k), pipeline_mode=pl.Buffered(1))
            for k in range(SHARDS)]


def _ffn_fwd(tag, x, norm, w, at, gate, tm=512):
    t = x.shape[0]
    slab = min(tm, MM_SLAB)

    def body(*refs):
        x_ref, gain_ref, scale_ref, shift_ref, gate_ref = refs[:5]
        wg, wu, wd = refs[5:5 + SHARDS], refs[5 + SHARDS:5 + 2 * SHARDS], refs[5 + 2 * SHARDS:5 + 3 * SHARDS]
        h_ref, g_ref, u_ref, xn_ref, f_ref = refs[5 + 3 * SHARDS:]
        half_gate = 0.5 * gate_ref[...]
        for r0 in range(0, tm, slab):
            rows = slice(r0, r0 + slab)
            xs = x_ref[rows, :]
            r = lax.rsqrt(jnp.mean(xs * xs, axis=-1, keepdims=True) + EPS)
            hs = ((xs * r) * gain_ref[...] * (1.0 + scale_ref[...]) + shift_ref[...]).astype(BF16)
            h_ref[rows, :] = hs
            tot = None
            for k in range(SHARDS):
                gk = jnp.dot(hs, wg[k][...], preferred_element_type=F32)
                uk = jnp.dot(hs, wu[k][...], preferred_element_type=F32)
                g_ref[k, rows, :] = gk.astype(BF16)
                u_ref[k, rows, :] = uk.astype(BF16)
                ak = ((gk * _sigmoid(gk)) * uk).astype(BF16)
                part = jnp.dot(ak, wd[k][...], preferred_element_type=F32)
                tot = part if tot is None else tot + part
            xn_ref[rows, :] = xs + half_gate * tot
            f_ref[rows, :] = tot.astype(BF16)

    hidden = jax.ShapeDtypeStruct((SHARDS, t, FSH), BF16)
    hidden_spec = pl.BlockSpec((SHARDS, tm, FSH), lambda i: (0, i, 0))
    half = jax.ShapeDtypeStruct((t, D), BF16)
    h, gv, uv, x_new, f = pl.pallas_call(
        body, out_shape=[half, hidden, hidden, jax.ShapeDtypeStruct((t, D), F32), half], grid=(t // tm,),
        in_specs=[_row_spec(tm, D)] + [_vec_spec(D)] * 4 + _resident(w["ffn_wg"], at)
        + _resident(w["ffn_wu"], at) + _resident(w["ffn_wd"], at),
        out_specs=[_row_spec(tm, D), hidden_spec, hidden_spec, _row_spec(tm, D), _row_spec(tm, D)],
        name=f"ffn_fwd_{tag}", compiler_params=_params(("parallel",)),
    )(x, *norm, gate, *([w["ffn_wg"]] * SHARDS), *([w["ffn_wu"]] * SHARDS), *([w["ffn_wd"]] * SHARDS))
    return x_new, h, (gv, uv, f)


def _ffn_hidden_bwd(tag, dx, f, gate, gv, uv, x, gain, scale, w, at, tm=256):
    t = dx.shape[0]
    steps = t // tm
    slab = min(tm, MM_SLAB)

    def body(*refs):
        dx_ref, f_ref, gate_ref, g_ref, u_ref, x_ref, gain_ref, scale_ref = refs[:8]
        wg, wu, wd = refs[8:8 + SHARDS], refs[8 + SHARDS:8 + 2 * SHARDS], refs[8 + 2 * SHARDS:8 + 3 * SHARDS]
        df_ref, dg_ref, du_ref, dxin_ref, dgate_ref, dgain_ref, dscale_ref, dshift_ref = refs[8 + 3 * SHARDS:]
        i = pl.program_id(0)

        @pl.when(i == 0)
        def _():
            dgate_ref[...] = jnp.zeros_like(dgate_ref)
            dgain_ref[...] = jnp.zeros_like(dgain_ref)
            dshift_ref[...] = jnp.zeros_like(dshift_ref)

        half_gate = 0.5 * gate_ref[...]
        norm_w = gain_ref[...] * (1.0 + scale_ref[...])
        for r0 in range(0, tm, slab):
            rows = slice(r0, r0 + slab)
            dxs = dx_ref[rows, :]
            dfs = (half_gate * dxs).astype(BF16)
            df_ref[rows, :] = dfs
            dgate_ref[...] += jnp.sum((0.5 * f_ref[rows, :].astype(F32)) * dxs, axis=0, keepdims=True)
            tot = None
            for k in range(SHARDS):
                da = lax.dot_general(dfs, wd[k][...], NT, preferred_element_type=F32)
                gk, uk = g_ref[k, rows, :].astype(F32), u_ref[k, rows, :].astype(F32)
                sg = _sigmoid(gk)
                dgk = (da * uk * (sg * (1.0 + gk * (1.0 - sg)))).astype(BF16)
                duk = (da * (gk * sg)).astype(BF16)
                dg_ref[k, rows, :] = dgk
                du_ref[k, rows, :] = duk
                part = (lax.dot_general(dgk, wg[k][...], NT, preferred_element_type=F32)
                        + lax.dot_general(duk, wu[k][...], NT, preferred_element_type=F32))
                tot = part if tot is None else tot + part
            xs = x_ref[rows, :]
            r = lax.rsqrt(jnp.mean(xs * xs, axis=-1, keepdims=True) + EPS)
            xh = xs * r
            dxh = tot * norm_w
            dxin_ref[rows, :] = dxs + r * (dxh - xh * jnp.mean(dxh * xh, axis=-1, keepdims=True))
            dgain_ref[...] += jnp.sum(tot * xh, axis=0, keepdims=True)
            dshift_ref[...] += jnp.sum(tot, axis=0, keepdims=True)

        @pl.when(i == steps - 1)
        def _():
            acc = dgain_ref[...]
            dgain_ref[...] = acc * (1.0 + scale_ref[...])
            dscale_ref[...] = acc * gain_ref[...]

    hidden = jax.ShapeDtypeStruct((SHARDS, t, FSH), BF16)
    hidden_spec = pl.BlockSpec((SHARDS, tm, FSH), lambda i: (0, i, 0))
    vec = jax.ShapeDtypeStruct((1, D), F32)
    return pl.pallas_call(
        body, out_shape=[jax.ShapeDtypeStruct((t, D), BF16), hidden, hidden, jax.ShapeDtypeStruct((t, D), F32),
                         vec, vec, vec, vec],
        grid=(steps,),
        in_specs=[_row_spec(tm, D), _row_spec(tm, D), _vec_spec(D), hidden_spec, hidden_spec, _row_spec(tm, D),
                  _vec_spec(D), _vec_spec(D)]
        + _resident(w["ffn_wg"], at) + _resident(w["ffn_wu"], at) + _resident(w["ffn_wd"], at),
        out_specs=[_row_spec(tm, D), hidden_spec, hidden_spec, _row_spec(tm, D)] + [_vec_spec(D)] * 4,
        name=f"ffn_hidden_bwd_{tag}", compiler_params=_params(("arbitrary",)),
    )(dx, f, gate, gv, uv, x, gain, scale, *([w["ffn_wg"]] * SHARDS), *([w["ffn_wu"]] * SHARDS),
      *([w["ffn_wd"]] * SHARDS))


def _ffn_bwd(tag, dx, x, h, saved, w, at, g, scale, gate, into):
    gv, uv, f = saved
    df, dg, du, dx_in, dgate, dgn, dscale, dshift = _ffn_hidden_bwd(tag, dx, f, gate, gv, uv, x, g, scale, w, at)

    def act(blocks):
        gf, uf = blocks[0].astype(F32), blocks[1].astype(F32)
        return ((gf * _sigmoid(gf)) * uf).astype(BF16)

    dwd = _mm_tn(f"ffn_dwd_{tag}", [gv, uv], df, tk=FSH, tn=1024, tt=2048, a_fn=act, a_tiled=True,
                 into=(into["ffn_wd"], at))
    dwg = _mm_tn(f"ffn_dwg_{tag}", h, dg, tk=1024, tn=FSH, tt=2048, b_tiled=True, into=(into["ffn_wg"], at))
    dwu = _mm_tn(f"ffn_dwu_{tag}", h, du, tk=1024, tn=FSH, tt=2048, b_tiled=True, into=(into["ffn_wu"], at))
    return dx_in, dict(ffn_wg=dwg, ffn_wu=dwu, ffn_wd=dwd), dgn, (dshift, dscale, dgate)


def _mix_in_fwd(name, x, g, scale, shift, w_parts, cos_t, sin_t, tm=512):
    t = x.shape[0]
    ng = len(DILATIONS)
    n = 3 * ng
    halves = GW // VL
    strips = QKV // VL

    def body(x_ref, g_ref, sc_ref, sh_ref, cos_ref, sin_ref, wq_ref, wu_ref, wg_ref, h_ref, *rest):
        o_refs, zu_ref, zg_ref, strip_ref = rest[:n], rest[n], rest[n + 1], rest[n + 2]
        for r0 in range(0, tm, MM_SLAB):
            rows = slice(r0, r0 + MM_SLAB)
            xv = x_ref[rows, :]
            r = lax.rsqrt(jnp.mean(xv * xv, axis=-1, keepdims=True) + EPS)
            hv = ((xv * r) * g_ref[...] * (1.0 + sc_ref[...]) + sh_ref[...]).astype(BF16)
            h_ref[rows, :] = hv
            zu_ref[rows, :] = jnp.dot(hv, wu_ref[...], preferred_element_type=F32)
            zg_ref[rows, :] = jnp.dot(hv, wg_ref[...], preferred_element_type=F32).astype(BF16)
            zq = jnp.dot(hv, wq_ref[...], preferred_element_type=F32)
            for j in range(strips):
                strip_ref[j, rows, :] = zq[:, j * VL:(j + 1) * VL]
        for idx in range(n):
            d = DILATIONS[idx % ng]
            for r in range(d):
                rows = _every(d, r, tm)
                for hh in range(halves):
                    piece = strip_ref[halves * idx + hh, rows, :]
                    if idx < 2 * ng:
                        piece = _rotate(piece, cos_ref[rows, :], sin_ref[rows, :])
                    if idx < ng:
                        piece = piece * (HEAD ** -0.5)
                    o_refs[idx][:, r * GW + hh * VL:r * GW + (hh + 1) * VL] = piece.astype(BF16)

    dils = [DILATIONS[idx % ng] for idx in range(n)]
    return pl.pallas_call(
        body, out_shape=[jax.ShapeDtypeStruct((t, D), BF16)] + [_dilated_shape(t, d, BF16) for d in dils]
        + [jax.ShapeDtypeStruct((t, 2 * D), F32), jax.ShapeDtypeStruct((t, 2 * D), BF16)], grid=(t // tm,),
        in_specs=[_row_spec(tm, D), _vec_spec(D), _vec_spec(D), _vec_spec(D), _row_spec(tm, VL), _row_spec(tm, VL)]
        + [_whole(wp) for wp in w_parts],
        out_specs=[_row_spec(tm, D)] + [_dilated_spec(tm, d) for d in dils] + [_row_spec(tm, 2 * D)] * 2,
        scratch_shapes=[pltpu.VMEM((strips, tm, VL), F32)], name=name, compiler_params=_params(("parallel",)),
    )(x, g, scale, shift, cos_t, sin_t, *w_parts)


def _mix_fwd(tag, x, norm, w_in, attn_wo, conv_w, conv_b, ln_g, ln_b, conv_wo, w_out, gate, cos_t, sin_t):
    w_qkv, w_u, w_g = w_in[:, :QKV], w_in[:, QKV:QKV + 2 * D], w_in[:, QKV + 2 * D:]
    h, *qkv, zu, zg = _mix_in_fwd(f"mix_in_{tag}", x, *norm, [w_qkv, w_u, w_g], cos_t, sin_t)
    n = len(DILATIONS)
    outs, lses = [], []
    for grp in range(n):
        o, lse = _attn_fwd(f"attn_fwd_{tag}_{grp}", qkv[grp], qkv[n + grp], qkv[2 * n + grp], grp)
        outs.append(o)
        lses.append(lse)
    hc, s = _conv_fwd(f"conv_fwd_{tag}", zu, conv_w, conv_b, ln_g, ln_b)
    ob, of, lj, y, ya, yc, sa, sc, x_new, f = _mix_out_fwd(f"mix_out_{tag}", outs, lses, s, zg, x, gate, attn_wo,
                                                          conv_wo, w_out)
    return x_new, h, (zu, sa, sc, qkv, ob, of, lj, hc, s, y, ya, yc, f, (w_qkv, w_u, w_g))


def _whole(w):
    return pl.BlockSpec(w.shape, lambda i: (0, 0), pipeline_mode=pl.Buffered(1))


def _merge_groups(in_refs, rows_ref, ob_ref, of_ref, lj_ref, tm):
    n = len(in_refs) // 2
    for hh in range(GW // VL):
        for idx in range(2 * n):
            d = DILATIONS[idx % n]
            for r in range(d):
                cols = slice(r * GW + hh * VL, r * GW + (hh + 1) * VL)
                rows_ref[idx, _every(d, r, tm), :] = in_refs[idx][:, cols]
        ls = [rows_ref[n + g] for g in range(n)]
        m = ls[0]
        for v in ls[1:]:
            m = jnp.maximum(m, v)
        es = [jnp.exp(v - m) for v in ls]
        tot = es[0]
        for v in es[1:]:
            tot = tot + v
        acc = (es[0] / tot) * rows_ref[0]
        for g in range(1, n):
            acc = acc + (es[g] / tot) * rows_ref[g]
        half = slice(hh * VL, (hh + 1) * VL)
        ob_ref[:, half] = acc.astype(BF16)
        of_ref[:, half] = acc
        lj_ref[:, half] = m + jnp.log(tot)


def _mix_out_fwd(name, outs, lses, s, zg, x, gate, attn_wo, conv_wo, w_out, tm=512):
    t = x.shape[0]
    n = len(outs)

    def body(*refs):
        group_refs = refs[:2 * n]
        s_ref, za_ref, zc_ref, x_ref, gate_ref, wa_ref, wc_ref, wo_ref = refs[2 * n:2 * n + 8]
        ob_ref, of_ref, lj_ref, y_ref, ya_ref, yc_ref, sa_ref, sc_ref, xn_ref, f_ref, rows_ref = refs[2 * n + 8:]
        _merge_groups(group_refs, rows_ref, ob_ref, of_ref, lj_ref, tm)
        for r0 in range(0, tm, MM_SLAB):
            rows = slice(r0, r0 + MM_SLAB)
            ya = jnp.dot(ob_ref[rows, :], wa_ref[...], preferred_element_type=F32)
            yc = jnp.dot(s_ref[rows, :], wc_ref[...], preferred_element_type=F32)
            sa, sc = _sigmoid(za_ref[rows, :].astype(F32)), _sigmoid(zc_ref[rows, :].astype(F32))
            y = (sa * ya + sc * yc).astype(BF16)
            out = jnp.dot(y, wo_ref[...], preferred_element_type=F32)
            y_ref[rows, :], ya_ref[rows, :], yc_ref[rows, :] = y, ya.astype(BF16), yc.astype(BF16)
            sa_ref[rows, :], sc_ref[rows, :] = sa.astype(BF16), sc.astype(BF16)
            xn_ref[rows, :] = x_ref[rows, :] + gate_ref[...] * out
            f_ref[rows, :] = out.astype(BF16)

    half = jax.ShapeDtypeStruct((t, D), BF16)
    group = lambda dt: jax.ShapeDtypeStruct((t, GW), dt)
    return pl.pallas_call(
        body, out_shape=[group(BF16), group(F32), group(F32)] + [half] * 5 + [jax.ShapeDtypeStruct((t, D), F32), half],
        grid=(t // tm,),
        in_specs=[_dilated_spec(tm, DILATIONS[idx % n]) for idx in range(2 * n)]
        + [_row_spec(tm, D), _row_spec(tm, D, 0), _row_spec(tm, D, 1), _row_spec(tm, D), _vec_spec(D),
           _whole(attn_wo), _whole(conv_wo), _whole(w_out)],
        out_specs=[_row_spec(tm, GW)] * 3 + [_row_spec(tm, D)] * 7,
        scratch_shapes=[pltpu.VMEM((2 * n, tm, VL), F32)], name=name, compiler_params=_params(("parallel",)),
    )(*outs, *lses, s, zg, zg, x, gate, attn_wo, conv_wo, w_out)


def _mix_out_bwd(name, dx, f, gate, sa, sc, ya, yc, o, lj, attn_wo, conv_wo, w_out, tm=512):
    t = dx.shape[0]
    n = len(DILATIONS)
    halves = GW // VL

    def body(*refs):
        dx_ref, f_ref, gate_ref, sa_ref, sc_ref, ya_ref, yc_ref = refs[:7]
        o_refs, lj_refs = refs[7:7 + halves], refs[7 + halves:7 + 2 * halves]
        wa_ref, wc_ref, wo_ref = refs[7 + 2 * halves:10 + 2 * halves]
        df_ref, dya_ref, dyc_ref, dzg_ref, ds_ref, dgate_ref = refs[10 + 2 * halves:16 + 2 * halves]
        prep_refs = refs[16 + 2 * halves:16 + 2 * halves + 3 * n]
        do_ref, dsum_ref = refs[16 + 2 * halves + 3 * n:]

        @pl.when(pl.program_id(0) == 0)
        def _():
            dgate_ref[...] = jnp.zeros_like(dgate_ref)

        for r0 in range(0, tm, MM_SLAB):
            rows = slice(r0, r0 + MM_SLAB)
            dxs = dx_ref[rows, :]
            dfs = (gate_ref[...] * dxs).astype(BF16)
            df_ref[rows, :] = dfs
            dgate_ref[...] += jnp.sum(f_ref[rows, :].astype(F32) * dxs, axis=0, keepdims=True)
            dy = lax.dot_general(dfs, wo_ref[...], NT, preferred_element_type=F32)
            ga, gc = sa_ref[rows, :].astype(F32), sc_ref[rows, :].astype(F32)
            dya, dyc = (dy * ga).astype(BF16), (dy * gc).astype(BF16)
            dya_ref[rows, :], dyc_ref[rows, :] = dya, dyc
            dzg_ref[rows, 0:D] = (dy * ya_ref[rows, :].astype(F32) * (ga * (1.0 - ga))).astype(BF16)
            dzg_ref[rows, D:] = (dy * yc_ref[rows, :].astype(F32) * (gc * (1.0 - gc))).astype(BF16)
            do = lax.dot_general(dya, wa_ref[...], NT, preferred_element_type=F32)
            for hh in range(halves):
                do_ref[hh, rows, :] = do[:, hh * VL:(hh + 1) * VL]
            ds_ref[rows, :] = lax.dot_general(dyc, wc_ref[...], NT, preferred_element_type=F32)

        for hh in range(halves):
            prod = do_ref[hh] * o_refs[hh][...]
            parts = [jnp.broadcast_to(jnp.sum(prod[:, _head_cols(h)], axis=-1, keepdims=True), (tm, HEAD))
                     for h in range(VL // HEAD)]
            dsum_ref[...] = jnp.concatenate(parts, axis=1)
            for g, d in enumerate(DILATIONS):
                for r in range(d):
                    rows, cols = _every(d, r, tm), slice(r * GW + hh * VL, r * GW + (hh + 1) * VL)
                    prep_refs[g][:, cols] = dsum_ref[rows, :]
                    prep_refs[n + g][:, cols] = do_ref[hh, rows, :].astype(BF16)
                    prep_refs[2 * n + g][:, cols] = lj_refs[hh][rows, :]

    half = jax.ShapeDtypeStruct((t, D), BF16)
    prep_shapes = [_dilated_shape(t, d, dt) for dt in (F32, BF16, F32) for d in DILATIONS]
    half_specs = [_row_spec(tm, VL, hh) for hh in range(halves)]
    outs = pl.pallas_call(
        body, out_shape=[half, half, half, jax.ShapeDtypeStruct((t, 2 * D), BF16), jax.ShapeDtypeStruct((t, D), F32),
                         jax.ShapeDtypeStruct((1, D), F32)] + prep_shapes, grid=(t // tm,),
        in_specs=[_row_spec(tm, D), _row_spec(tm, D), _vec_spec(D)] + [_row_spec(tm, D)] * 4 + half_specs * 2
        + [_whole(attn_wo), _whole(conv_wo), _whole(w_out)],
        out_specs=[_row_spec(tm, D)] * 3 + [_row_spec(tm, 2 * D), _row_spec(tm, D), _vec_spec(D)]
        + [_dilated_spec(tm, d) for d in DILATIONS] * 3,
        scratch_shapes=[pltpu.VMEM((halves, tm, VL), F32), pltpu.VMEM((tm, VL), F32)],
        name=name, compiler_params=_params(("arbitrary",)),
    )(dx, f, gate, sa, sc, ya, yc, *([o] * halves), *([lj] * halves), attn_wo, conv_wo, w_out)
    return outs[:6], outs[6:]


def _mix_bwd(tag, dx, x, h, saved, attn_wo, conv_w, ln_g, ln_b, conv_wo, w_out, g, scale, gate, cos_t, sin_t):
    zu, sa, sc, qkv, ob, of, lj, hc, s, y, ya, yc, f, w_parts = saved
    n = len(DILATIONS)
    (df, dya, dyc, dzg, ds, dgate), prep = _mix_out_bwd(f"mix_out_bwd_{tag}", dx, f, gate, sa, sc, ya, yc, of, lj,
                                                         attn_wo, conv_wo, w_out)
    dw_out = _mm_tn(f"mix_dwout_{tag}", y, df, tk=1024, tn=1024, tt=2048)
    dw_attn = _mm_tn(f"mix_dwattn_{tag}", ob, dya, tk=GW, tn=1024, tt=2048)
    dw_conv_o = _mm_tn(f"mix_dwconvo_{tag}", s, dyc, tk=1024, tn=1024, tt=2048)

    dqs, dks, dvs = [], [], []
    for grp in range(n):
        dq, dk, dv = _attn_bwd(f"attn_bwd_{tag}_{grp}", qkv[grp], qkv[n + grp], qkv[2 * n + grp], prep[n + grp],
                               prep[2 * n + grp], prep[grp], grp)
        dqs.append(dq)
        dks.append(dk)
        dvs.append(dv)
    dzqkv = _rope_bwd(f"rope_bwd_{tag}", dqs + dks + dvs, cos_t, sin_t)

    dzu, dconv_w, dln_g, dln_b, dconv_b = _conv_bwd(f"conv_bwd_{tag}", zu, hc, ds, conv_w, ln_g, ln_b)

    dz_parts = [dzqkv, dzu, dzg]
    dw_in = jnp.concatenate(
        [_mm_tn(f"mix_dwin_{tag}_{i}", h, dzp, tk=1024, tn=dzp.shape[1] // 2, tt=2048)
         for i, dzp in enumerate(dz_parts)], axis=1)
    dx_in, dgn, dscale, dshift = _proj_norm_bwd(f"mix_dh_{tag}", dz_parts, list(w_parts), x, dx, g, scale)
    grads = dict(w_in=dw_in, attn_wo=dw_attn, conv_w=dconv_w[:CONV_K], conv_b=dconv_b, conv_ln_g=dln_g,
                 conv_ln_b=dln_b, conv_wo=dw_conv_o, w_out=dw_out)
    return dx_in, grads, dgn, (dshift, dscale, dgate)


def _local_step(x, mod, target, w, wf):
    t = x.shape[0]
    cos_t, sin_t = _rope_tables(t)
    row = lambda v: v.reshape(1, -1)
    conv_w_pad = jnp.concatenate([wf["conv_w"], jnp.zeros((DEPTH, HALO - CONV_K, D), F32)], axis=1)

    saved = []
    for l in range(DEPTH):
        mods = [mod[l:l + 1, i * D:(i + 1) * D] for i in range(N_MOD)]
        gains = [row(wf["norm_g"][l, i]) for i in range(3)]
        lay = dict(mods=mods, gains=gains)

        lay["x0"] = x
        x, lay["h0"], lay["ffn0"] = _ffn_fwd(f"a_{l}", x, (gains[0], mods[1], mods[0]), w, (l, 0), mods[2])
        lay["x1"] = x
        x, lay["h1"], lay["mix"] = _mix_fwd(f"{l}", x, (gains[1], mods[4], mods[3]), w["w_in"][l], w["attn_wo"][l],
                                            conv_w_pad[l],
                                 row(wf["conv_b"][l]), row(wf["conv_ln_g"][l]), row(wf["conv_ln_b"][l]),
                                 w["conv_wo"][l], w["w_out"][l], mods[5], cos_t, sin_t)
        lay["x2"] = x
        x, lay["h2"], lay["ffn1"] = _ffn_fwd(f"b_{l}", x, (gains[2], mods[7], mods[6]), w, (l, 1), mods[8])
        saved.append(lay)

    dx, dfinal_g, loss_cols = _loss_bwd("loss_head", x, target, row(wf["final_g"]))

    ffn_grads = {n: jnp.zeros(w[n].shape, F32) for n in ("ffn_wg", "ffn_wu", "ffn_wd")}
    per_layer = []
    for l in reversed(range(DEPTH)):
        lay = saved[l]
        mods, gains = lay["mods"], lay["gains"]
        dx, ffn_grads, dgn2, dmod2 = _ffn_bwd(f"b_{l}", dx, lay["x2"], lay["h2"], lay["ffn1"], w, (l, 1),
                                              gains[2], mods[7], mods[8], ffn_grads)
        dx, gm, dgn1, dmod1 = _mix_bwd(f"{l}", dx, lay["x1"], lay["h1"], lay["mix"], w["attn_wo"][l],
                                       conv_w_pad[l], row(wf["conv_ln_g"][l]), row(wf["conv_ln_b"][l]),
                                       w["conv_wo"][l], w["w_out"][l], gains[1], mods[4], mods[5], cos_t, sin_t)
        dx, ffn_grads, dgn0, dmod0 = _ffn_bwd(f"a_{l}", dx, lay["x0"], lay["h0"], lay["ffn0"], w, (l, 0),
                                              gains[0], mods[1], mods[2], ffn_grads)
        g = dict(gm)
        g["dmod"] = jnp.concatenate(list(dmod0) + list(dmod1) + list(dmod2), axis=1)
        g["norm_g"] = [dgn0[0], dgn1[0], dgn2[0]]
        for name in ("conv_b", "conv_ln_g", "conv_ln_b"):
            g[name] = g[name][0]
        per_layer.append(g)
    per_layer.reverse()
    grads = {name: [per_layer[l][name] for l in range(DEPTH)] for name in per_layer[0]}
    grads["dmod"] = jnp.concatenate(grads["dmod"], axis=0)
    grads.update(ffn_grads)
    grads["final_g"] = dfinal_g[0]
    return loss_cols, dx, grads


def _split_bits(w):
    bits = lax.bitcast_convert_type(w, jnp.uint32)
    hi = lax.bitcast_convert_type((bits >> 16).astype(jnp.uint16), BF16)
    lo = lax.bitcast_convert_type((bits & 0xFFFF).astype(jnp.uint16), BF16)
    return hi, lo


def _join_bits(hi, lo):
    h = lax.bitcast_convert_type(hi, jnp.uint16).astype(jnp.uint32)
    l = lax.bitcast_convert_type(lo, jnp.uint16).astype(jnp.uint32)
    return lax.bitcast_convert_type((h << 16) | l, F32)


def _pack(parts, rows):
    out = []
    for p in parts:
        flat = p.reshape(-1)
        pad = -flat.shape[0] % LANES
        out.append(jnp.concatenate([flat, jnp.zeros((pad,), flat.dtype)]) if pad else flat)
    flat = jnp.concatenate(out)
    return jnp.concatenate([flat, jnp.zeros((rows * LANES - flat.shape[0],), flat.dtype)]).reshape(rows, LANES)


def _unpack(buf, shapes):
    out, row = [], 0
    for shape in shapes:
        size = 1
        for s in shape:
            size *= s
        rows = -(-size // LANES)
        out.append(buf[row:row + rows].reshape(-1)[:size].reshape(shape))
        row += rows
    return out


def _place():
    x, y, c = lax.axis_index("x"), lax.axis_index("y"), lax.axis_index("c")
    chips = [(1 - x, y), (x, 1 - y), (1 - x, 1 - y)]
    return x, y, c, chips


def _chip_index():
    return (2 * lax.axis_index("x") + lax.axis_index("y")).astype(jnp.int32)


HBM_SPEC = pl.BlockSpec(memory_space=pltpu.HBM)


def _gather_rows(name, block):
    m, n = block.shape

    def body(x_ref, out_ref, send_sems, recv_sems, local_sem):
        x, y, c, chips = _place()
        me, sibling = (x, y, c), (x, y, 1 - c)

        def rows(px, py, pc):
            return out_ref.at[pl.ds((4 * px + 2 * py + pc) * m, m), :]

        def copy(k, owner, to, src=None):
            return pltpu.make_async_remote_copy(
                src_ref=rows(*owner) if src is None else src, dst_ref=rows(*owner), send_sem=send_sems.at[k],
                recv_sem=recv_sems.at[k], device_id=to, device_id_type=MESH)

        mine = pltpu.make_async_copy(x_ref, rows(*me), local_sem)
        mine.start()
        first = [copy(0, me, sibling, src=x_ref)] + [copy(1 + j, me, (*chip, c), src=x_ref)
                                                     for j, chip in enumerate(chips)]
        for cp in first:
            cp.start()
        passed = [copy(4 + j, (*chip, c), sibling) for j, chip in enumerate(chips)]
        for j, chip in enumerate(chips):
            copy(1 + j, (*chip, c), me).wait_recv()
            passed[j].start()
        copy(0, sibling, me).wait_recv()
        for j, chip in enumerate(chips):
            copy(4 + j, (*chip, 1 - c), me).wait_recv()
        for cp in first + passed:
            cp.wait_send()
        mine.wait()

    whole = pl.BlockSpec(memory_space=pltpu.VMEM)
    return pl.pallas_call(
        body, out_shape=jax.ShapeDtypeStruct((N_DEV * m, n), block.dtype), in_specs=[whole], out_specs=whole,
        scratch_shapes=[pltpu.SemaphoreType.DMA((7,)), pltpu.SemaphoreType.DMA((7,)), pltpu.SemaphoreType.DMA],
        name=name,
    )(block)


def _gather_weights(arrays):
    n = len(arrays)

    def body(*refs):
        outs, send_sems, recv_sems = refs[n:2 * n], refs[2 * n], refs[2 * n + 1]
        x, y, c, chips = _place()
        me = 2 * x + y
        sibling = (x, y, 1 - c)
        there = [2 * chip[0] + chip[1] for chip in chips]

        def copy(a, k, chip, layer, to):
            piece = outs[a].at[chip, layer]
            return pltpu.make_async_remote_copy(
                src_ref=piece, dst_ref=piece, send_sem=send_sems.at[6 * a + k], recv_sem=recv_sems.at[6 * a + k],
                device_id=to, device_id_type=MESH)

        first = [copy(a, j, me, c, (*chip, c)) for a in range(n) for j, chip in enumerate(chips)]
        for cp in first:
            cp.start()
        passed = []
        for a in range(n):
            for j in range(3):
                copy(a, j, there[j], c, sibling).wait_recv()
                passed.append(copy(a, 3 + j, there[j], c, sibling))
                passed[-1].start()
        for a in range(n):
            for j in range(3):
                copy(a, 3 + j, there[j], 1 - c, sibling).wait_recv()
        for cp in first + passed:
            cp.wait_send()

    return pl.pallas_call(
        body, out_shape=[jax.ShapeDtypeStruct(a.shape, a.dtype) for a in arrays],
        in_specs=[HBM_SPEC] * n, out_specs=[HBM_SPEC] * n,
        scratch_shapes=[pltpu.SemaphoreType.DMA((6 * n,)), pltpu.SemaphoreType.DMA((6 * n,))],
        input_output_aliases={i: i for i in range(n)}, name="gather_weights",
    )(*arrays)


def _row_block(rows, cols):
    for cand in (512, 256, 128, 64, 32, 16):
        if rows % cand == 0 and cand * cols * 4 <= 2560 * 1024:
            return cand
    return rows


def _swap_layers(grads):
    n = len(grads)

    def body(*refs):
        g_refs, out_refs, send_sems, recv_sems = refs[:n], refs[n:2 * n], refs[2 * n], refs[2 * n + 1]
        x, y, c, _ = _place()
        copies = [pltpu.make_async_remote_copy(
            src_ref=g_refs[a].at[:, 1 - c], dst_ref=out_refs[a], send_sem=send_sems.at[a], recv_sem=recv_sems.at[a],
            device_id=(x, y, 1 - c), device_id_type=MESH) for a in range(n)]
        for cp in copies:
            cp.start()
        for cp in copies:
            cp.wait()

    return pl.pallas_call(
        body, out_shape=[jax.ShapeDtypeStruct((g.shape[0],) + g.shape[2:], F32) for g in grads],
        in_specs=[HBM_SPEC] * n, out_specs=[HBM_SPEC] * n,
        scratch_shapes=[pltpu.SemaphoreType.DMA((n,)), pltpu.SemaphoreType.DMA((n,))], name="swap_layers",
    )(*grads)


def _add_layers(name, grad, other):
    shards, _, rows, cols = grad.shape
    tr = _row_block(rows, cols)

    def body(c_ref, g_ref, o_ref, out_ref):
        out_ref[...] = (g_ref[...] + o_ref[...]).astype(BF16)

    c = lax.axis_index("c").astype(jnp.int32).reshape(1)
    grid_spec = pltpu.PrefetchScalarGridSpec(
        num_scalar_prefetch=1, grid=(shards, rows // tr),
        in_specs=[pl.BlockSpec((None, None, tr, cols), lambda k, i, c_ref: (k, c_ref[0], i, 0)),
                  pl.BlockSpec((None, tr, cols), lambda k, i, c_ref: (k, i, 0))],
        out_specs=pl.BlockSpec((None, tr, cols), lambda k, i, c_ref: (k, i, 0)))
    return pl.pallas_call(
        body, out_shape=jax.ShapeDtypeStruct((shards, rows, cols), BF16), grid_spec=grid_spec,
        name=name, compiler_params=_params(("parallel", "parallel")),
    )(c, grad, other)


def _scatter_chips(parts):
    n = len(parts)

    def body(*refs):
        p_refs, out_refs, send_sems, recv_sems = refs[:n], refs[n:2 * n], refs[2 * n], refs[2 * n + 1]
        x, y, c, chips = _place()
        me = 2 * x + y
        there = [2 * chip[0] + chip[1] for chip in chips]

        def copy(a, j, slot):
            return pltpu.make_async_remote_copy(
                src_ref=p_refs[a].at[there[j]], dst_ref=out_refs[a].at[slot], send_sem=send_sems.at[3 * a + j],
                recv_sem=recv_sems.at[3 * a + j], device_id=(*chips[j], c), device_id_type=MESH)

        sends = [copy(a, j, me) for a in range(n) for j in range(3)]
        for cp in sends:
            cp.start()
        for a in range(n):
            for j in range(3):
                copy(a, j, there[j]).wait_recv()
        for cp in sends:
            cp.wait_send()

    return pl.pallas_call(
        body, out_shape=[jax.ShapeDtypeStruct(p.shape, p.dtype) for p in parts],
        in_specs=[HBM_SPEC] * n, out_specs=[HBM_SPEC] * n,
        scratch_shapes=[pltpu.SemaphoreType.DMA((3 * n,)), pltpu.SemaphoreType.DMA((3 * n,))],
        name="scatter_chips",
    )(*parts)


def _add_chips(name, part, others):
    shards, rows, cols = part.shape
    tr = _row_block(rows, cols)

    def body(pos_ref, own_ref, r0_ref, r1_ref, r2_ref, r3_ref, out_ref):
        me = pos_ref[0]
        own = own_ref[...].astype(F32)
        total = None
        for k, r_ref in enumerate((r0_ref, r1_ref, r2_ref, r3_ref)):
            term = jnp.where(me == k, own, r_ref[...].astype(F32))
            total = term if total is None else total + term
        out_ref[...] = total

    def other(k):
        return pl.BlockSpec((None, tr, cols),
                            lambda i, pos, k=k: (jnp.where(pos[0] == k, (k + 1) % shards, k), i, 0))

    pos = jnp.stack([_chip_index(), lax.axis_index("c").astype(jnp.int32)])
    grid_spec = pltpu.PrefetchScalarGridSpec(
        num_scalar_prefetch=1, grid=(rows // tr,),
        in_specs=[pl.BlockSpec((None, tr, cols), lambda i, pos: (pos[0], i, 0))] + [other(k) for k in range(shards)],
        out_specs=pl.BlockSpec((None, tr, cols), lambda i, pos: (pos[1], i, 0)))
    return pl.pallas_call(
        body, out_shape=jax.ShapeDtypeStruct((DEPTH, rows, cols), F32), grid_spec=grid_spec,
        name=name, compiler_params=_params(("parallel",)),
    )(pos, part, others, others, others, others)


def _join_layers(arrays):
    n = len(arrays)

    def body(*refs):
        outs, send_sems, recv_sems = refs[n:2 * n], refs[2 * n], refs[2 * n + 1]
        x, y, c, _ = _place()

        def copy(a, layer):
            piece = outs[a].at[layer]
            return pltpu.make_async_remote_copy(src_ref=piece, dst_ref=piece, send_sem=send_sems.at[a],
                                                recv_sem=recv_sems.at[a], device_id=(x, y, 1 - c),
                                                device_id_type=MESH)

        sends = [copy(a, c) for a in range(n)]
        for cp in sends:
            cp.start()
        for a in range(n):
            copy(a, 1 - c).wait_recv()
        for cp in sends:
            cp.wait_send()

    return pl.pallas_call(
        body, out_shape=[jax.ShapeDtypeStruct(a.shape, a.dtype) for a in arrays],
        in_specs=[HBM_SPEC] * n, out_specs=[HBM_SPEC] * n,
        scratch_shapes=[pltpu.SemaphoreType.DMA((n,)), pltpu.SemaphoreType.DMA((n,))],
        input_output_aliases={i: i for i in range(n)}, name="join_layers",
    )(*arrays)


def _reduce_scatter(grads):
    sums = [_add_layers(f"add_layers_{a}", g, o) for a, (g, o) in enumerate(zip(grads, _swap_layers(grads)))]
    others = _scatter_chips(sums)
    return _join_layers([_add_chips(f"add_chips_{a}", p, o) for a, (p, o) in enumerate(zip(sums, others))])


def _adamw(name, w, g, m, v):
    shape = w.shape
    cols = shape[-1]
    rows = w.size // cols
    tr = rows
    for cand in (512, 256, 128, 64, 32, 16, 8):
        if rows % cand == 0 and cand * cols * 4 <= 2 * 1024 * 1024:
            tr = cand
            break

    def body(w_ref, g_ref, m_ref, v_ref, go_ref, d_ref, nm_ref, nv_ref):
        gv = g_ref[...]
        go_ref[...] = gv
        nm = ADAM_B1 * m_ref[...] + (1.0 - ADAM_B1) * gv
        nv = ADAM_B2 * v_ref[...] + (1.0 - ADAM_B2) * (gv * gv)
        m_hat = nm / (1.0 - ADAM_B1 ** ADAM_STEP)
        v_hat = nv / (1.0 - ADAM_B2 ** ADAM_STEP)
        d_ref[...] = -ADAM_LR * (m_hat / (jnp.sqrt(v_hat) + ADAM_EPS) + ADAM_WD * w_ref[...])
        nm_ref[...] = nm
        nv_ref[...] = nv

    spec = pl.BlockSpec((tr, cols), lambda i: (i, 0))
    two = lambda a: a.reshape(rows, cols)
    outs = pl.pallas_call(
        body, out_shape=[jax.ShapeDtypeStruct((rows, cols), F32)] * 4, grid=(rows // tr,),
        in_specs=[spec] * 4, out_specs=[spec] * 4, name=name, compiler_params=_params(("parallel",)),
    )(two(w), two(g), two(m), two(v))
    return [o.reshape(shape) for o in outs]


BIG = ("ffn_wg", "ffn_wu", "ffn_wd", "w_in", "conv_wo", "w_out")
MISC_ROWS = 96


def _own_slot(shard):
    return lax.dynamic_update_slice(jnp.zeros((SHARDS,) + shard.shape, shard.dtype), shard[None],
                                    (_chip_index(),) + (0,) * shard.ndim)


def _as_matrices(a):
    return a.reshape(a.shape[0], a.shape[1], -1, a.shape[-1])


def kernel(x, c, ada_w, ada_b, norm_g, ffn_wg, ffn_wu, ffn_wd, w_in, attn_wo, conv_w, conv_b, conv_ln_g, conv_ln_b, conv_wo, w_out, final_g, loss_target, m_ada_w, m_ada_b, m_norm_g, m_ffn_wg, m_ffn_wu, m_ffn_wd, m_w_in, m_attn_wo, m_conv_w, m_conv_b, m_conv_ln_g, m_conv_ln_b, m_conv_wo, m_w_out, m_final_g, v_ada_w, v_ada_b, v_norm_g, v_ffn_wg, v_ffn_wu, v_ffn_wd, v_w_in, v_attn_wo, v_conv_w, v_conv_b, v_conv_ln_g, v_conv_ln_b, v_conv_wo, v_w_out, v_final_g):
    weights = dict(ada_w=ada_w, ada_b=ada_b, norm_g=norm_g, ffn_wg=ffn_wg, ffn_wu=ffn_wu, ffn_wd=ffn_wd, w_in=w_in,
                   attn_wo=attn_wo, conv_w=conv_w, conv_b=conv_b, conv_ln_g=conv_ln_g, conv_ln_b=conv_ln_b,
                   conv_wo=conv_wo, w_out=w_out, final_g=final_g)
    moments_m = dict(ada_w=m_ada_w, ada_b=m_ada_b, norm_g=m_norm_g, ffn_wg=m_ffn_wg, ffn_wu=m_ffn_wu,
                     ffn_wd=m_ffn_wd, w_in=m_w_in, attn_wo=m_attn_wo, conv_w=m_conv_w, conv_b=m_conv_b,
                     conv_ln_g=m_conv_ln_g, conv_ln_b=m_conv_ln_b, conv_wo=m_conv_wo, w_out=m_w_out,
                     final_g=m_final_g)
    moments_v = dict(ada_w=v_ada_w, ada_b=v_ada_b, norm_g=v_norm_g, ffn_wg=v_ffn_wg, ffn_wu=v_ffn_wu,
                     ffn_wd=v_ffn_wd, w_in=v_w_in, attn_wo=v_attn_wo, conv_w=v_conv_w, conv_b=v_conv_b,
                     conv_ln_g=v_conv_ln_g, conv_ln_b=v_conv_ln_b, conv_wo=v_conv_wo, w_out=v_w_out,
                     final_g=v_final_g)
    layers, shards = range(DEPTH), range(SHARDS)

    bits = {n: _split_bits(weights[n]) for n in EXACT}
    misc_w = jnp.stack([_pack([attn_wo[l].astype(BF16), bits["norm_g"][0][l], bits["norm_g"][1][l],
                               bits["conv_w"][0][l], bits["conv_w"][1][l]], MISC_ROWS) for l in layers])
    sent = [_own_slot(weights[n].astype(BF16)) for n in BIG] + [_own_slot(misc_w)]
    got = dict(zip(BIG + ("misc",), _gather_weights(sent)))
    w = {n: got[n] for n in ("ffn_wg", "ffn_wu", "ffn_wd")}
    w["w_in"] = got["w_in"].transpose(1, 2, 0, 3).reshape(DEPTH, D, -1)
    for n in ("conv_wo", "w_out"):
        w[n] = got[n].transpose(1, 0, 2, 3).reshape(DEPTH, D, D)
    misc_shapes = [(GW, GW), (3, GW), (3, GW), (CONV_K, GW), (CONV_K, GW)]
    pieces = [[_unpack(got["misc"][k, l], misc_shapes) for k in shards] for l in layers]
    whole = lambda i: jnp.stack([jnp.concatenate([pieces[l][k][i] for k in shards], axis=1) for l in layers])
    w["attn_wo"] = whole(0)
    vectors = dict(ada_b=ada_b, conv_b=conv_b, conv_ln_g=conv_ln_g, conv_ln_b=conv_ln_b, final_g=final_g,
                   norm_g=_join_bits(whole(1), whole(2)), conv_w=_join_bits(whole(3), whole(4)))

    me = 2 * _chip_index() + lax.axis_index("c").astype(jnp.int32)
    pad_rows = lambda a, rows: jnp.concatenate([a, jnp.zeros((rows - a.shape[0], a.shape[1]), a.dtype)])
    c_all = _gather_rows("gather_c", pad_rows(c, SUBLANES)).reshape(N_DEV, SUBLANES, D)[:, 0]
    mod_cols = _mod_fwd("mod_fwd", c_all, ada_w.astype(BF16))
    by_dev = _gather_rows("gather_mod", mod_cols).reshape(N_DEV, N_DEV, DEPTH, -1)
    mine = lax.dynamic_index_in_dim(by_dev[0::2], me, axis=1, keepdims=False)
    mod = mine.transpose(1, 0, 2).reshape(DEPTH, -1) + ada_b

    loss_cols, dx, grads = _local_step(x[0], mod, loss_target[0], w, vectors)
    loss = lax.psum(jnp.sum(loss_cols), ("x", "y", "c"))

    dmod_rows = DEPTH * N_MOD * D // LANES
    dmod_all = _gather_rows("gather_dmod", pad_rows(grads["dmod"].reshape(dmod_rows, LANES), 3 * SUBLANES))
    dmod_all = dmod_all.reshape(N_DEV, 3 * SUBLANES, LANES)[:, :dmod_rows].reshape(N_DEV, DEPTH, -1)
    grad_ada_b = _sum_devices("ada_b_grad", dmod_all.reshape(N_DEV, -1)).reshape(DEPTH, -1)
    cols = ada_w.shape[-1]
    dmod_cols = lax.dynamic_slice_in_dim(dmod_all, _chip_index() * cols, cols, axis=2).transpose(1, 0, 2)
    grad_ada_w = _mod_bwd("mod_bwd", c_all.T, dmod_cols)

    cols_of = lambda a, k: a[..., k * GW:(k + 1) * GW]
    misc_g = jnp.stack([jnp.stack([_pack(
        [cols_of(grads["attn_wo"][l], k), cols_of(jnp.stack(grads["norm_g"][l]), k), cols_of(grads["conv_w"][l], k),
         grads["conv_b"][l], grads["conv_ln_g"][l], grads["conv_ln_b"][l],
         grads["final_g"] if l == 0 else jnp.zeros_like(grads["final_g"])], MISC_ROWS)
        for l in layers]) for k in shards])
    by_chip = dict(
        ffn_wg=grads["ffn_wg"], ffn_wu=grads["ffn_wu"], ffn_wd=grads["ffn_wd"],
        w_in=jnp.stack(grads["w_in"]).reshape(DEPTH, D, SHARDS, -1).transpose(2, 0, 1, 3),
        conv_wo=jnp.stack(grads["conv_wo"]).reshape(DEPTH, SHARDS, -1, D).transpose(1, 0, 2, 3),
        w_out=jnp.stack(grads["w_out"]).reshape(DEPTH, SHARDS, -1, D).transpose(1, 0, 2, 3))
    reduced = _reduce_scatter([_as_matrices(by_chip[n]) for n in BIG] + [misc_g])
    summed = {n: r.reshape(weights[n].shape) for n, r in zip(BIG, reduced)}
    small_shapes = [(GW, GW), (3, GW), (CONV_K, GW), (D,), (D,), (D,), (D,)]
    small = [_unpack(reduced[-1][l], small_shapes) for l in layers]
    for i, n in enumerate(("attn_wo", "norm_g", "conv_w", "conv_b", "conv_ln_g", "conv_ln_b")):
        summed[n] = jnp.stack([small[l][i] for l in layers])
    summed["final_g"] = small[0][6]
    summed["ada_w"], summed["ada_b"] = grad_ada_w, grad_ada_b

    grad_out, deltas, new_m, new_v = {}, {}, {}, {}
    for n in WEIGHTS:
        grad_out[n], deltas[n], new_m[n], new_v[n] = _adamw(f"adamw_{n}", weights[n], summed[n], moments_m[n],
                                                           moments_v[n])

    return (loss, dx[None], *[grad_out[n] for n in WEIGHTS], *[deltas[n] for n in WEIGHTS],
            *[new_m[n] for n in WEIGHTS], *[new_v[n] for n in WEIGHTS])
```

```python
import functools

import jax
import jax.numpy as jnp
from jax import lax
from jax.experimental import pallas as pl
from jax.experimental.pallas import tpu as pltpu

F32 = jnp.float32
BF16 = jnp.bfloat16

D = 1024
DFF = 2816
HEAD = 64
GW = 256
DILATIONS = (1, 4, 16)
BAND = 128
QKV = 2304
CONV_K = 31
HALO = 32
N_MOD = 9
EPS = 1e-6
NEG_INF = -1e30
DEPTH = 2

SHARDS = 4
FSH = DFF // SHARDS
LANES = 1024
VL = 128

ADAM_LR = 0.001
ADAM_B1 = 0.9
ADAM_B2 = 0.999
ADAM_EPS = 1e-08
ADAM_WD = 0.01
ADAM_STEP = 10

VMEM_LIMIT = 56 * 1024 * 1024

EXACT = ("norm_g", "conv_w")
WEIGHTS = ("ada_w", "ada_b", "norm_g", "ffn_wg", "ffn_wu", "ffn_wd", "w_in", "attn_wo", "conv_w", "conv_b",
           "conv_ln_g", "conv_ln_b", "conv_wo", "w_out", "final_g")

MESH = pl.DeviceIdType.MESH


def _params(sem=None):
    return pltpu.CompilerParams(dimension_semantics=sem, vmem_limit_bytes=VMEM_LIMIT)


def _sigmoid(v):
    return jax.nn.sigmoid(v)


MM_SLAB = 256


def _mm_tn(name, a, b, *, tk, tn, tt, a_fn=None, a_tiled=False, b_tiled=False, into=None):
    a_list = list(a) if a_fn is not None else [a]
    na = len(a_list)
    t = a_list[0].shape[-2]
    nk = a_list[0].shape[0] if a_tiled else a_list[0].shape[1] // tk
    nn = b.shape[0] if b_tiled else b.shape[1] // tn
    tt = min(tt, t)
    steps = t // tt
    has_into = into is not None

    def body(*refs):
        refs = refs[1:] if has_into else refs
        a_refs, b_ref, o_ref, acc_ref = refs[:na], refs[na], refs[na + 1], refs[na + 2]
        s = pl.program_id(2)

        @pl.when(s == 0)
        def _():
            acc_ref[...] = jnp.zeros_like(acc_ref)

        av = a_refs[0][...] if a_fn is None else a_fn([r[...] for r in a_refs])
        acc_ref[...] += lax.dot_general(av, b_ref[...], (((0,), (0,)), ((), ())), preferred_element_type=F32)

        @pl.when(s == steps - 1)
        def _():
            o_ref[...] = acc_ref[...]

    a_spec = (pl.BlockSpec((None, tt, tk), lambda i, j, s: (i, s, 0)) if a_tiled
              else pl.BlockSpec((tt, tk), lambda i, j, s: (s, i)))
    b_spec = (pl.BlockSpec((None, tt, tn), lambda i, j, s: (j, s, 0)) if b_tiled
              else pl.BlockSpec((tt, tn), lambda i, j, s: (s, j)))
    if a_tiled:
        out_dims, tile_index = (nk, tk, nn * tn), lambda i, j, s: (i, 0, j)
    elif b_tiled:
        out_dims, tile_index = (nn, nk * tk, tn), lambda i, j, s: (j, i, 0)
    else:
        out_dims, tile_index = (nk * tk, nn * tn), lambda i, j, s: (i, j)
    tiled = a_tiled or b_tiled
    if has_into:
        buf, lead = into
        def out_index(i, j, s):
            idx = tile_index(i, j, s)
            return (idx[0], *lead, *idx[1:])
        out_spec = pl.BlockSpec((None,) * (1 + len(lead)) + (tk, tn), out_index)
        out_shape = jax.ShapeDtypeStruct(buf.shape, buf.dtype)
        extra_in, extra_specs, aliases = [buf], [pl.BlockSpec(memory_space=pl.ANY)], {0: 0}
    else:
        out_spec = pl.BlockSpec(((None,) if tiled else ()) + (tk, tn), tile_index)
        out_shape = jax.ShapeDtypeStruct(out_dims, F32)
        extra_in, extra_specs, aliases = [], [], {}
    return pl.pallas_call(
        body, out_shape=out_shape, grid=(nk, nn, steps), in_specs=extra_specs + [a_spec] * na + [b_spec],
        out_specs=out_spec, scratch_shapes=[pltpu.VMEM((tk, tn), F32)], input_output_aliases=aliases, name=name,
        compiler_params=_params(("parallel", "parallel", "arbitrary")),
    )(*extra_in, *a_list, b)


def _row_spec(tm, width, col=0):
    return pl.BlockSpec((tm, width), functools.partial(lambda i, col: (i, col), col=col))


def _vec_spec(width):
    return pl.BlockSpec((1, width), lambda i: (0, 0))


def _proj_norm_bwd(name, dz_parts, w_parts, x, dres, g, scale, tm=256):
    t = x.shape[0]
    steps = t // tm
    n = len(dz_parts)

    def body(*refs):
        dz_refs, w_refs = refs[:n], refs[n:2 * n]
        x_ref, dres_ref, g_ref, sc_ref, dx_ref, dg_ref, dsc_ref, dsh_ref = refs[2 * n:]
        i = pl.program_id(0)

        @pl.when(i == 0)
        def _():
            dg_ref[...] = jnp.zeros_like(dg_ref)
            dsh_ref[...] = jnp.zeros_like(dsh_ref)

        dh = None
        for dz_ref, w_ref in zip(dz_refs, w_refs):
            part = lax.dot_general(dz_ref[...], w_ref[...], (((1,), (1,)), ((), ())), preferred_element_type=F32)
            dh = part if dh is None else dh + part
        xv = x_ref[...]
        r = lax.rsqrt(jnp.mean(xv * xv, axis=-1, keepdims=True) + EPS)
        xh = xv * r
        dxh = dh * (g_ref[...] * (1.0 + sc_ref[...]))
        dx_ref[...] = dres_ref[...] + r * (dxh - xh * jnp.mean(dxh * xh, axis=-1, keepdims=True))
        dg_ref[...] += jnp.sum(dh * xh, axis=0, keepdims=True)
        dsh_ref[...] += jnp.sum(dh, axis=0, keepdims=True)

        @pl.when(i == steps - 1)
        def _():
            acc = dg_ref[...]
            dg_ref[...] = acc * (1.0 + sc_ref[...])
            dsc_ref[...] = acc * g_ref[...]

    vec = jax.ShapeDtypeStruct((1, D), F32)
    resident = [pl.BlockSpec(wp.shape, lambda i: (0, 0), pipeline_mode=pl.Buffered(1)) for wp in w_parts]
    return pl.pallas_call(
        body, out_shape=[jax.ShapeDtypeStruct((t, D), F32), vec, vec, vec], grid=(steps,),
        in_specs=[_row_spec(tm, dz.shape[1]) for dz in dz_parts] + resident
        + [_row_spec(tm, D), _row_spec(tm, D), _vec_spec(D), _vec_spec(D)],
        out_specs=[_row_spec(tm, D), _vec_spec(D), _vec_spec(D), _vec_spec(D)],
        name=name, compiler_params=_params(("arbitrary",)),
    )(*dz_parts, *w_parts, x, dres, g, scale)


def _loss_bwd(name, x, target, g, tm=256):
    t = x.shape[0]

    def body(x_ref, t_ref, g_ref, dx_ref, dg_ref, loss_ref):
        @pl.when(pl.program_id(0) == 0)
        def _():
            dg_ref[...] = jnp.zeros_like(dg_ref)
            loss_ref[...] = jnp.zeros_like(loss_ref)

        xv = x_ref[...]
        r = lax.rsqrt(jnp.mean(xv * xv, axis=-1, keepdims=True) + EPS)
        xh = xv * r
        err = xh * g_ref[...] - t_ref[...]
        dy = err * (1.0 / D)
        dxh = dy * g_ref[...]
        dx_ref[...] = r * (dxh - xh * jnp.mean(dxh * xh, axis=-1, keepdims=True))
        dg_ref[...] += jnp.sum(dy * xh, axis=0, keepdims=True)
        loss_ref[...] += jnp.sum(err * err, axis=0, keepdims=True) * (0.5 / D)

    vec = jax.ShapeDtypeStruct((1, D), F32)
    return pl.pallas_call(
        body, out_shape=[jax.ShapeDtypeStruct((t, D), F32), vec, vec], grid=(t // tm,),
        in_specs=[_row_spec(tm, D), _row_spec(tm, D), _vec_spec(D)],
        out_specs=[_row_spec(tm, D), _vec_spec(D), _vec_spec(D)],
        name=name, compiler_params=_params(("arbitrary",)),
    )(x, target, g)


N_DEV = 8


def _mod_fwd(name, c_all, ada_w):
    cols = ada_w.shape[-1]

    def body(c_ref, w_ref, o_ref):
        cv = c_ref[...]
        ca = (cv * _sigmoid(cv)).astype(BF16)
        o_ref[...] = jnp.dot(ca, w_ref[...], preferred_element_type=F32)

    return pl.pallas_call(
        body, out_shape=jax.ShapeDtypeStruct((N_DEV, DEPTH * cols), F32), grid=(DEPTH,),
        in_specs=[pl.BlockSpec((N_DEV, D), lambda l: (0, 0)), pl.BlockSpec((None, D, cols), lambda l: (l, 0, 0))],
        out_specs=pl.BlockSpec((N_DEV, cols), lambda l: (0, l)), name=name, compiler_params=_params(("parallel",)),
    )(c_all, ada_w)


def _mod_bwd(name, c_cols, dmods, tk=256):
    cols = dmods.shape[-1]

    def body(c_ref, d_ref, o_ref):
        cv = c_ref[...]
        ca = cv * _sigmoid(cv)
        total = ca[:, 0:1] * d_ref[0:1, :]
        for b in range(1, N_DEV):
            total = total + ca[:, b:b + 1] * d_ref[b:b + 1, :]
        o_ref[...] = total

    return pl.pallas_call(
        body, out_shape=jax.ShapeDtypeStruct((DEPTH, D, cols), F32), grid=(DEPTH, D // tk),
        in_specs=[pl.BlockSpec((tk, N_DEV), lambda l, i: (i, 0)), pl.BlockSpec((None, N_DEV, cols), lambda l, i: (l, 0, 0))],
        out_specs=pl.BlockSpec((None, tk, cols), lambda l, i: (l, i, 0)), name=name,
        compiler_params=_params(("parallel", "parallel")),
    )(c_cols, dmods)


def _sum_devices(name, rows):
    n = rows.shape[1]

    def body(r_ref, o_ref):
        total = r_ref[0:1, :]
        for b in range(1, N_DEV):
            total = total + r_ref[b:b + 1, :]
        o_ref[...] = total

    whole = pl.BlockSpec(memory_space=pltpu.VMEM)
    return pl.pallas_call(body, out_shape=jax.ShapeDtypeStruct((1, n), F32), in_specs=[whole], out_specs=whole,
                          name=name)(rows)


def _rope_tables(t):
    half = HEAD // 2
    inv_freq = 10000.0 ** (-(jnp.arange(half, dtype=F32) * 2.0 / HEAD))
    ang = jnp.arange(t, dtype=F32)[:, None] * inv_freq[None, :]
    cos, sin = jnp.cos(ang), jnp.sin(ang)
    cos_t = jnp.tile(jnp.concatenate([cos, cos], axis=1), (1, VL // HEAD))
    sin_t = jnp.tile(jnp.concatenate([-sin, sin], axis=1), (1, VL // HEAD))
    return cos_t, sin_t


def _rotate(tv, cos, sin_signed):
    lane = lax.broadcasted_iota(jnp.int32, tv.shape, 1)
    first = (lane % HEAD) < (HEAD // 2)
    partner = jnp.where(first, pltpu.roll(tv, tv.shape[1] - HEAD // 2, 1), pltpu.roll(tv, HEAD // 2, 1))
    return tv * cos + partner * sin_signed


def _dilated_spec(tm, d):
    return pl.BlockSpec((tm // d, d * GW), lambda i: (i, 0))


def _dilated_shape(t, d, dtype):
    return jax.ShapeDtypeStruct((t // d, d * GW), dtype)


def _every(d, r, tm):
    return pl.ds(r, tm // d, stride=d) if d > 1 else slice(None)


def _rope_bwd(name, grads, cos_t, sin_t, tm=512):
    ng = len(DILATIONS)
    n = len(grads)
    t = grads[0].shape[0] * DILATIONS[0]

    halves = GW // VL

    def body(*refs):
        g_refs, cos_ref, sin_ref, o_ref, rows_ref = refs[:n], refs[n], refs[n + 1], refs[n + 2], refs[n + 3]
        cos, sin = cos_ref[...], -sin_ref[...]
        for idx in range(n):
            d = DILATIONS[idx % ng]
            for hh in range(halves):
                for r in range(d):
                    cols = slice(r * GW + hh * VL, r * GW + (hh + 1) * VL)
                    rows_ref[hh, _every(d, r, tm), :] = g_refs[idx][:, cols].astype(F32)
                piece = rows_ref[hh]
                if idx < 2 * ng:
                    piece = _rotate(piece, cos, sin)
                if idx < ng:
                    piece = piece * (HEAD ** -0.5)
                o_ref[:, idx * GW + hh * VL:idx * GW + (hh + 1) * VL] = piece.astype(BF16)

    dils = [DILATIONS[idx % ng] for idx in range(n)]
    return pl.pallas_call(
        body, out_shape=jax.ShapeDtypeStruct((t, n * GW), BF16), grid=(t // tm,),
        in_specs=[_dilated_spec(tm, d) for d in dils] + [_row_spec(tm, VL)] * 2, out_specs=_row_spec(tm, n * GW),
        scratch_shapes=[pltpu.VMEM((halves, tm, VL), F32)], name=name, compiler_params=_params(("parallel",)),
    )(*grads, cos_t, sin_t)


def _head_cols(h):
    return slice(h * HEAD, (h + 1) * HEAD)


def _band_mask_q(has_prev):
    qi = lax.broadcasted_iota(jnp.int32, (BAND, 2 * BAND), 0)
    kj = lax.broadcasted_iota(jnp.int32, (BAND, 2 * BAND), 1)
    dist = qi + BAND - kj
    return (dist >= 0) & (dist <= BAND) & ((kj >= BAND) | has_prev)


def _attn_fwd(name, q, k, v, group):
    d = DILATIONS[group]
    length = q.shape[0]
    qb = min(512, length)
    sub = qb // BAND
    nblk = length // qb

    def body(q_ref, kc_ref, kp_ref, vc_ref, vp_ref, o_ref, lse_ref):
        blk = pl.program_id(1)
        k_ext = jnp.concatenate([kp_ref[...], kc_ref[...]], axis=0)
        v_ext = jnp.concatenate([vp_ref[...], vc_ref[...]], axis=0)
        for j in range(sub):
            mask = _band_mask_q((blk * sub + j) > 0)
            qj = q_ref[j * BAND:(j + 1) * BAND, :]
            kj = k_ext[j * BAND:(j + 2) * BAND, :]
            vj = v_ext[j * BAND:(j + 2) * BAND, :]
            outs, lses = [], []
            for h in range(GW // HEAD):
                s = lax.dot_general(qj[:, _head_cols(h)], kj[:, _head_cols(h)], (((1,), (1,)), ((), ())),
                                    preferred_element_type=F32)
                s = jnp.where(mask, s, NEG_INF)
                m = jnp.max(s, axis=-1, keepdims=True)
                p = jnp.exp(s - m)
                den = jnp.sum(p, axis=-1, keepdims=True)
                o = jnp.dot(p.astype(BF16), vj[:, _head_cols(h)], preferred_element_type=F32)
                outs.append(o / den)
                lses.append(jnp.broadcast_to(m + jnp.log(den), (BAND, HEAD)))
            o_ref[j * BAND:(j + 1) * BAND, :] = jnp.concatenate(outs, axis=1)
            lse_ref[j * BAND:(j + 1) * BAND, :] = jnp.concatenate(lses, axis=1)

    prev = qb // BAND
    cur = lambda r, b: (b, r)
    before = lambda r, b: (jnp.maximum(b * prev - 1, 0), r)
    big, halo = pl.BlockSpec((qb, GW), cur), pl.BlockSpec((BAND, GW), before)
    return pl.pallas_call(
        body, out_shape=[jax.ShapeDtypeStruct((length, d * GW), F32)] * 2, grid=(d, nblk),
        in_specs=[big, big, halo, big, halo], out_specs=[big] * 2, name=name,
        compiler_params=_params(("parallel", "parallel")),
    )(q, k, k, v, v)


def _attn_bwd(name, q, k, v, do, lj, dsum, group):
    d = DILATIONS[group]
    length = q.shape[0]
    qb = min(1024, length)
    sub = qb // BAND
    nblk = length // qb
    total = length // BAND

    def body(qc_ref, qn_ref, kc_ref, kp_ref, vc_ref, vp_ref, doc_ref, don_ref, ljc_ref, ljn_ref, dsc_ref, dsn_ref,
             dq_ref, dk_ref, dv_ref):
        blk = pl.program_id(1)
        k_ext = jnp.concatenate([kp_ref[...], kc_ref[...]], axis=0)
        v_ext = jnp.concatenate([vp_ref[...], vc_ref[...]], axis=0)
        heads = range(GW // HEAD)
        nt = (((1,), (1,)), ((), ()))
        tn = (((0,), (0,)), ((), ()))

        def scores(qh, doh, ljh, dsh, kh, vh, mask):
            s = lax.dot_general(qh, kh, nt, preferred_element_type=F32)
            p = jnp.where(mask, jnp.exp(s - ljh), 0.0)
            dp = lax.dot_general(doh, vh, nt, preferred_element_type=F32)
            return p.astype(BF16), (p * (dp - dsh)).astype(BF16)

        held_k, held_v = [None] * len(heads), [None] * len(heads)
        for j in range(sub):
            rows = slice(j * BAND, (j + 1) * BAND)
            rows2 = slice(j * BAND, (j + 2) * BAND)
            mask = _band_mask_q((blk * sub + j) > 0)
            dqs, dks, dvs = [], [], []
            for h in heads:
                hc = _head_cols(h)
                col = slice(h * HEAD, h * HEAD + 1)
                qh, doh, kh2 = qc_ref[rows, hc], doc_ref[rows, hc], k_ext[rows2, hc]
                p, ds = scores(qh, doh, ljc_ref[rows, col], dsc_ref[rows, col], kh2, v_ext[rows2, hc], mask)
                dqs.append(jnp.dot(ds, kh2, preferred_element_type=F32))
                dk2 = lax.dot_general(ds, qh, tn, preferred_element_type=F32)
                dv2 = lax.dot_general(p, doh, tn, preferred_element_type=F32)
                if j > 0:
                    dks.append(held_k[h] + dk2[:BAND])
                    dvs.append(held_v[h] + dv2[:BAND])
                held_k[h], held_v[h] = dk2[BAND:], dv2[BAND:]
            dq_ref[rows, :] = jnp.concatenate(dqs, axis=1)
            if j > 0:
                done = slice((j - 1) * BAND, j * BAND)
                dk_ref[done, :] = jnp.concatenate(dks, axis=1)
                dv_ref[done, :] = jnp.concatenate(dvs, axis=1).astype(BF16)

        last = slice((sub - 1) * BAND, sub * BAND)
        qi = lax.broadcasted_iota(jnp.int32, (BAND, BAND), 0)
        kj = lax.broadcasted_iota(jnp.int32, (BAND, BAND), 1)
        mask = (kj >= qi) & ((blk + 1) * sub < total)
        dks, dvs = [], []
        for h in heads:
            hc = _head_cols(h)
            col = slice(h * HEAD, h * HEAD + 1)
            qh, doh = qn_ref[:, hc], don_ref[:, hc]
            p, ds = scores(qh, doh, ljn_ref[:, col], dsn_ref[:, col], kc_ref[last, hc], vc_ref[last, hc], mask)
            dks.append(held_k[h] + lax.dot_general(ds, qh, tn, preferred_element_type=F32))
            dvs.append(held_v[h] + lax.dot_general(p, doh, tn, preferred_element_type=F32))
        dk_ref[last, :] = jnp.concatenate(dks, axis=1)
        dv_ref[last, :] = jnp.concatenate(dvs, axis=1).astype(BF16)

    prev = qb // BAND
    cur = lambda r, b: (b, r)
    before = lambda r, b: (jnp.maximum(b * prev - 1, 0), r)
    after = lambda r, b: (jnp.minimum((b + 1) * prev, total - 1), r)
    big = pl.BlockSpec((qb, GW), cur)
    nxt = pl.BlockSpec((BAND, GW), after)
    prv = pl.BlockSpec((BAND, GW), before)
    return pl.pallas_call(
        body, out_shape=[jax.ShapeDtypeStruct((length, d * GW), F32), jax.ShapeDtypeStruct((length, d * GW), F32),
                         jax.ShapeDtypeStruct((length, d * GW), BF16)], grid=(d, nblk),
        in_specs=[big, nxt, big, prv, big, prv, big, nxt, big, nxt, big, nxt],
        out_specs=[big] * 3, name=name, compiler_params=_params(("parallel", "parallel")),
    )(q, q, k, k, v, v, do, do, lj, lj, dsum, dsum)


SUBLANES = 8
CONV_CHUNK = 32
SHIFT_ROWS = HALO - SUBLANES


def _shifted_copies(buf_ref, sh_ref, tm):
    for s in range(1, SUBLANES):
        sh_ref[s - 1] = buf_ref[s:s + tm + SHIFT_ROWS, :]


def _window(buf_ref, sh_ref, offset, r0, rows):
    tiles, shift = divmod(offset, SUBLANES)
    src = buf_ref if shift == 0 else sh_ref.at[shift - 1]
    return src[pl.ds(pl.multiple_of(r0 + tiles * SUBLANES, SUBLANES), rows), :]


def _conv_fwd(name, zu, conv_w, conv_b, ln_g, ln_b, tm=256):
    t = zu.shape[0]
    per = tm // HALO

    def body(a_ref, gl_ref, ah_ref, glh_ref, w_ref, b_ref, g_ref, beta_ref, hc_ref, s_ref, ext_ref, sh_ref):
        i = pl.program_id(0)
        halo = ah_ref[...] * _sigmoid(glh_ref[...])
        ext_ref[0:HALO, :] = jnp.where(i > 0, halo, 0.0)
        ext_ref[HALO:, :] = a_ref[...] * _sigmoid(gl_ref[...])
        _shifted_copies(ext_ref, sh_ref, tm)

        def chunk(r, carry):
            r0 = pl.multiple_of(r * CONV_CHUNK, CONV_CHUNK)
            part = jnp.broadcast_to(b_ref[...], (CONV_CHUNK, D))
            for kk in range(CONV_K):
                part = part + w_ref[kk:kk + 1, :] * _window(ext_ref, sh_ref, HALO - CONV_K + 1 + kk, r0, CONV_CHUNK)
            hc_ref[pl.ds(r0, CONV_CHUNK), :] = part
            return carry

        lax.fori_loop(0, tm // CONV_CHUNK, chunk, 0)
        acc = hc_ref[...]
        mu = jnp.mean(acc, axis=-1, keepdims=True)
        xc = acc - mu
        var = jnp.mean(xc * xc, axis=-1, keepdims=True)
        ln = xc * lax.rsqrt(var + EPS) * g_ref[...] + beta_ref[...]
        s_ref[...] = (ln * _sigmoid(ln)).astype(BF16)

    halo_map = lambda col: (lambda i: (jnp.maximum(i * per - 1, 0), col))
    return pl.pallas_call(
        body, out_shape=[jax.ShapeDtypeStruct((t, D), F32), jax.ShapeDtypeStruct((t, D), BF16)], grid=(t // tm,),
        in_specs=[_row_spec(tm, D, 0), _row_spec(tm, D, 1), pl.BlockSpec((HALO, D), halo_map(0)),
                  pl.BlockSpec((HALO, D), halo_map(1)), pl.BlockSpec((HALO, D), lambda i: (0, 0)),
                  _vec_spec(D), _vec_spec(D), _vec_spec(D)],
        out_specs=[_row_spec(tm, D), _row_spec(tm, D)],
        scratch_shapes=[pltpu.VMEM((tm + HALO, D), F32), pltpu.VMEM((SUBLANES - 1, tm + SHIFT_ROWS, D), F32)],
        name=name, compiler_params=_params(("parallel",)),
    )(zu, zu, zu, zu, conv_w, conv_b, ln_g, ln_b)


def _ln_swish_bwd(hv, dsv, g, beta):
    mu = jnp.mean(hv, axis=-1, keepdims=True)
    xc = hv - mu
    rstd = lax.rsqrt(jnp.mean(xc * xc, axis=-1, keepdims=True) + EPS)
    xh = xc * rstd
    ln = xh * g + beta
    sg = _sigmoid(ln)
    dln = dsv * (sg * (1.0 + ln * (1.0 - sg)))
    dxh = dln * g
    dh = rstd * (dxh - jnp.mean(dxh, axis=-1, keepdims=True) - xh * jnp.mean(dxh * xh, axis=-1, keepdims=True))
    return dh, dln, xh


def _conv_bwd(name, zu, hc, ds, conv_w, ln_g, ln_b, tm=256):
    t = zu.shape[0]
    per = tm // HALO
    steps = t // tm

    group = 4

    def body(a_ref, gl_ref, ah_ref, glh_ref, hc_ref, hcn_ref, ds_ref, dsn_ref, w_ref, g_ref, beta_ref,
             dz_ref, dw_ref, dg_ref, dbeta_ref, dbias_ref, ext_ref, sh_ref, dext_ref, dsh_ref, sg_ref, part_ref):
        i = pl.program_id(0)

        @pl.when(i == 0)
        def _():
            part_ref[...] = jnp.zeros_like(part_ref)
            dg_ref[...] = jnp.zeros_like(dg_ref)
            dbeta_ref[...] = jnp.zeros_like(dbeta_ref)
            dbias_ref[...] = jnp.zeros_like(dbias_ref)

        sg_ref[...] = _sigmoid(gl_ref[...])
        ext_ref[0:HALO, :] = jnp.where(i > 0, ah_ref[...] * _sigmoid(glh_ref[...]), 0.0)
        ext_ref[HALO:, :] = a_ref[...] * sg_ref[...]
        dh, dln, xh = _ln_swish_bwd(hc_ref[...], ds_ref[...], g_ref[...], beta_ref[...])
        dext_ref[0:tm, :] = dh
        dg_ref[...] += jnp.sum(dln * xh, axis=0, keepdims=True)
        dbeta_ref[...] += jnp.sum(dln, axis=0, keepdims=True)
        dbias_ref[...] += jnp.sum(dh, axis=0, keepdims=True)
        dh_next, _, _ = _ln_swish_bwd(hcn_ref[...], dsn_ref[...], g_ref[...], beta_ref[...])
        dext_ref[tm:, :] = jnp.where(i < steps - 1, dh_next, 0.0)
        _shifted_copies(ext_ref, sh_ref, tm)
        _shifted_copies(dext_ref, dsh_ref, tm)

        def chunk(r, carry):
            r0 = pl.multiple_of(r * CONV_CHUNK, CONV_CHUNK)
            rows = pl.ds(r0, CONV_CHUNK)
            part = jnp.zeros((CONV_CHUNK, D), F32)
            for kk in range(CONV_K):
                part = part + w_ref[kk:kk + 1, :] * _window(dext_ref, dsh_ref, CONV_K - 1 - kk, r0, CONV_CHUNK)
            sg = sg_ref[rows, :]
            dz_ref[rows, 0:D] = (part * sg).astype(BF16)
            dz_ref[rows, D:] = (part * a_ref[rows, :] * sg * (1.0 - sg)).astype(BF16)
            return carry

        lax.fori_loop(0, tm // CONV_CHUNK, chunk, 0)

        for k0 in range(0, CONV_K, group):
            taps = range(k0, min(k0 + group, CONV_K))

            def tile(r, parts, taps=taps):
                r0 = pl.multiple_of(r * CONV_CHUNK, CONV_CHUNK)
                dv = dext_ref[pl.ds(r0, CONV_CHUNK), :]
                out = []
                for p, kk in zip(parts, taps):
                    prod = dv * _window(ext_ref, sh_ref, HALO - CONV_K + 1 + kk, r0, CONV_CHUNK)
                    for s in range(0, CONV_CHUNK, SUBLANES):
                        p = p + prod[s:s + SUBLANES, :]
                    out.append(p)
                return tuple(out)

            parts = lax.fori_loop(0, tm // CONV_CHUNK, tile, tuple(jnp.zeros((SUBLANES, D), F32) for _ in taps))
            for p, kk in zip(parts, taps):
                part_ref[kk * SUBLANES:(kk + 1) * SUBLANES, :] += p

        @pl.when(i == steps - 1)
        def _():
            for kk in range(HALO):
                dw_ref[kk:kk + 1, :] = jnp.sum(part_ref[kk * SUBLANES:(kk + 1) * SUBLANES, :], axis=0, keepdims=True)

    halo_map = lambda col: (lambda i: (jnp.maximum(i * per - 1, 0), col))
    next_rows = pl.BlockSpec((HALO, D), lambda i: (jnp.minimum((i + 1) * per, t // HALO - 1), 0))
    shifted = pltpu.VMEM((SUBLANES - 1, tm + SHIFT_ROWS, D), F32)
    vec = jax.ShapeDtypeStruct((1, D), F32)
    return pl.pallas_call(
        body, out_shape=[jax.ShapeDtypeStruct((t, 2 * D), BF16), jax.ShapeDtypeStruct((HALO, D), F32), vec, vec, vec],
        grid=(steps,),
        in_specs=[_row_spec(tm, D, 0), _row_spec(tm, D, 1), pl.BlockSpec((HALO, D), halo_map(0)),
                  pl.BlockSpec((HALO, D), halo_map(1)), _row_spec(tm, D), next_rows, _row_spec(tm, D), next_rows,
                  pl.BlockSpec((HALO, D), lambda i: (0, 0)), _vec_spec(D), _vec_spec(D)],
        out_specs=[_row_spec(tm, 2 * D), pl.BlockSpec((HALO, D), lambda i: (0, 0))] + [_vec_spec(D)] * 3,
        scratch_shapes=[pltpu.VMEM((tm + HALO, D), F32), shifted, pltpu.VMEM((tm + HALO, D), F32), shifted,
                        pltpu.VMEM((tm, D), F32), pltpu.VMEM((HALO * SUBLANES, D), F32)],
        name=name, compiler_params=_params(("arbitrary",)),
    )(zu, zu, zu, zu, hc, hc, ds, ds, conv_w, ln_g, ln_b)


NT = (((1,), (1,)), ((), ()))


def _resident(w, at):
    block = (None,) * (1 + len(at)) + tuple(w.shape[-2:])
    return [pl.BlockSpec(block, functools.partial(lambda i, k: (k, *at, 0, 0), k=k), pipeline_mode=pl.Buffered(1))
            for k in range(SHARDS)]


def _ffn_fwd(tag, x, norm, w, at, gate, tm=512):
    t = x.shape[0]
    slab = min(tm, MM_SLAB)

    def body(*refs):
        x_ref, gain_ref, scale_ref, shift_ref, gate_ref = refs[:5]
        wg, wu, wd = refs[5:5 + SHARDS], refs[5 + SHARDS:5 + 2 * SHARDS], refs[5 + 2 * SHARDS:5 + 3 * SHARDS]
        h_ref, g_ref, u_ref, xn_ref, f_ref = refs[5 + 3 * SHARDS:]
        half_gate = 0.5 * gate_ref[...]
        for r0 in range(0, tm, slab):
            rows = slice(r0, r0 + slab)
            xs = x_ref[rows, :]
            r = lax.rsqrt(jnp.mean(xs * xs, axis=-1, keepdims=True) + EPS)
            hs = ((xs * r) * gain_ref[...] * (1.0 + scale_ref[...]) + shift_ref[...]).astype(BF16)
            h_ref[rows, :] = hs
            tot = None
            for k in range(SHARDS):
                gk = jnp.dot(hs, wg[k][...], preferred_element_type=F32)
                uk = jnp.dot(hs, wu[k][...], preferred_element_type=F32)
                g_ref[k, rows, :] = gk.astype(BF16)
                u_ref[k, rows, :] = uk.astype(BF16)
                ak = ((gk * _sigmoid(gk)) * uk).astype(BF16)
                part = jnp.dot(ak, wd[k][...], preferred_element_type=F32)
                tot = part if tot is None else tot + part
            xn_ref[rows, :] = xs + half_gate * tot
            f_ref[rows, :] = tot.astype(BF16)

    hidden = jax.ShapeDtypeStruct((SHARDS, t, FSH), BF16)
    hidden_spec = pl.BlockSpec((SHARDS, tm, FSH), lambda i: (0, i, 0))
    half = jax.ShapeDtypeStruct((t, D), BF16)
    h, gv, uv, x_new, f = pl.pallas_call(
        body, out_shape=[half, hidden, hidden, jax.ShapeDtypeStruct((t, D), F32), half], grid=(t // tm,),
        in_specs=[_row_spec(tm, D)] + [_vec_spec(D)] * 4 + _resident(w["ffn_wg"], at)
        + _resident(w["ffn_wu"], at) + _resident(w["ffn_wd"], at),
        out_specs=[_row_spec(tm, D), hidden_spec, hidden_spec, _row_spec(tm, D), _row_spec(tm, D)],
        name=f"ffn_fwd_{tag}", compiler_params=_params(("parallel",)),
    )(x, *norm, gate, *([w["ffn_wg"]] * SHARDS), *([w["ffn_wu"]] * SHARDS), *([w["ffn_wd"]] * SHARDS))
    return x_new, h, (gv, uv, f)


def _ffn_hidden_bwd(tag, dx, f, gate, gv, uv, x, gain, scale, w, at, tm=256):
    t = dx.shape[0]
    steps = t // tm
    slab = min(tm, MM_SLAB)

    def body(*refs):
        dx_ref, f_ref, gate_ref, g_ref, u_ref, x_ref, gain_ref, scale_ref = refs[:8]
        wg, wu, wd = refs[8:8 + SHARDS], refs[8 + SHARDS:8 + 2 * SHARDS], refs[8 + 2 * SHARDS:8 + 3 * SHARDS]
        df_ref, dg_ref, du_ref, dxin_ref, dgate_ref, dgain_ref, dscale_ref, dshift_ref = refs[8 + 3 * SHARDS:]
        i = pl.program_id(0)

        @pl.when(i == 0)
        def _():
            dgate_ref[...] = jnp.zeros_like(dgate_ref)
            dgain_ref[...] = jnp.zeros_like(dgain_ref)
            dshift_ref[...] = jnp.zeros_like(dshift_ref)

        half_gate = 0.5 * gate_ref[...]
        norm_w = gain_ref[...] * (1.0 + scale_ref[...])
        for r0 in range(0, tm, slab):
            rows = slice(r0, r0 + slab)
            dxs = dx_ref[rows, :]
            dfs = (half_gate * dxs).astype(BF16)
            df_ref[rows, :] = dfs
            dgate_ref[...] += jnp.sum((0.5 * f_ref[rows, :].astype(F32)) * dxs, axis=0, keepdims=True)
            tot = None
            for k in range(SHARDS):
                da = lax.dot_general(dfs, wd[k][...], NT, preferred_element_type=F32)
                gk, uk = g_ref[k, rows, :].astype(F32), u_ref[k, rows, :].astype(F32)
                sg = _sigmoid(gk)
                dgk = (da * uk * (sg * (1.0 + gk * (1.0 - sg)))).astype(BF16)
                duk = (da * (gk * sg)).astype(BF16)
                dg_ref[k, rows, :] = dgk
                du_ref[k, rows, :] = duk
                part = (lax.dot_general(dgk, wg[k][...], NT, preferred_element_type=F32)
                        + lax.dot_general(duk, wu[k][...], NT, preferred_element_type=F32))
                tot = part if tot is None else tot + part
            xs = x_ref[rows, :]
            r = lax.rsqrt(jnp.mean(xs * xs, axis=-1, keepdims=True) + EPS)
            xh = xs * r
            dxh = tot * norm_w
            dxin_ref[rows, :] = dxs + r * (dxh - xh * jnp.mean(dxh * xh, axis=-1, keepdims=True))
            dgain_ref[...] += jnp.sum(tot * xh, axis=0, keepdims=True)
            dshift_ref[...] += jnp.sum(tot, axis=0, keepdims=True)

        @pl.when(i == steps - 1)
        def _():
            acc = dgain_ref[...]
            dgain_ref[...] = acc * (1.0 + scale_ref[...])
            dscale_ref[...] = acc * gain_ref[...]

    hidden = jax.ShapeDtypeStruct((SHARDS, t, FSH), BF16)
    hidden_spec = pl.BlockSpec((SHARDS, tm, FSH), lambda i: (0, i, 0))
    vec = jax.ShapeDtypeStruct((1, D), F32)
    return pl.pallas_call(
        body, out_shape=[jax.ShapeDtypeStruct((t, D), BF16), hidden, hidden, jax.ShapeDtypeStruct((t, D), F32),
                         vec, vec, vec, vec],
        grid=(steps,),
        in_specs=[_row_spec(tm, D), _row_spec(tm, D), _vec_spec(D), hidden_spec, hidden_spec, _row_spec(tm, D),
                  _vec_spec(D), _vec_spec(D)]
        + _resident(w["ffn_wg"], at) + _resident(w["ffn_wu"], at) + _resident(w["ffn_wd"], at),
        out_specs=[_row_spec(tm, D), hidden_spec, hidden_spec, _row_spec(tm, D)] + [_vec_spec(D)] * 4,
        name=f"ffn_hidden_bwd_{tag}", compiler_params=_params(("arbitrary",)),
    )(dx, f, gate, gv, uv, x, gain, scale, *([w["ffn_wg"]] * SHARDS), *([w["ffn_wu"]] * SHARDS),
      *([w["ffn_wd"]] * SHARDS))


def _ffn_bwd(tag, dx, x, h, saved, w, at, g, scale, gate, into):
    gv, uv, f = saved
    df, dg, du, dx_in, dgate, dgn, dscale, dshift = _ffn_hidden_bwd(tag, dx, f, gate, gv, uv, x, g, scale, w, at)

    def act(blocks):
        gf, uf = blocks[0].astype(F32), blocks[1].astype(F32)
        return ((gf * _sigmoid(gf)) * uf).astype(BF16)

    dwd = _mm_tn(f"ffn_dwd_{tag}", [gv, uv], df, tk=FSH, tn=1024, tt=2048, a_fn=act, a_tiled=True,
                 into=(into["ffn_wd"], at))
    dwg = _mm_tn(f"ffn_dwg_{tag}", h, dg, tk=1024, tn=FSH, tt=4096, b_tiled=True, into=(into["ffn_wg"], at))
    dwu = _mm_tn(f"ffn_dwu_{tag}", h, du, tk=1024, tn=FSH, tt=4096, b_tiled=True, into=(into["ffn_wu"], at))
    return dx_in, dict(ffn_wg=dwg, ffn_wu=dwu, ffn_wd=dwd), dgn, (dshift, dscale, dgate)


def _mix_in_fwd(name, x, g, scale, shift, w_parts, cos_t, sin_t, tm=512):
    t = x.shape[0]
    ng = len(DILATIONS)
    n = 3 * ng
    halves = GW // VL
    strips = QKV // VL

    def body(x_ref, g_ref, sc_ref, sh_ref, cos_ref, sin_ref, wq_ref, wu_ref, wg_ref, h_ref, *rest):
        o_refs, zu_ref, zg_ref, strip_ref = rest[:n], rest[n], rest[n + 1], rest[n + 2]
        for r0 in range(0, tm, MM_SLAB):
            rows = slice(r0, r0 + MM_SLAB)
            xv = x_ref[rows, :]
            r = lax.rsqrt(jnp.mean(xv * xv, axis=-1, keepdims=True) + EPS)
            hv = ((xv * r) * g_ref[...] * (1.0 + sc_ref[...]) + sh_ref[...]).astype(BF16)
            h_ref[rows, :] = hv
            zu_ref[rows, :] = jnp.dot(hv, wu_ref[...], preferred_element_type=F32)
            zg_ref[rows, :] = jnp.dot(hv, wg_ref[...], preferred_element_type=F32).astype(BF16)
            zq = jnp.dot(hv, wq_ref[...], preferred_element_type=F32)
            for j in range(strips):
                strip_ref[j, rows, :] = zq[:, j * VL:(j + 1) * VL]
        for idx in range(n):
            d = DILATIONS[idx % ng]
            for r in range(d):
                rows = _every(d, r, tm)
                for hh in range(halves):
                    piece = strip_ref[halves * idx + hh, rows, :]
                    if idx < 2 * ng:
                        piece = _rotate(piece, cos_ref[rows, :], sin_ref[rows, :])
                    if idx < ng:
                        piece = piece * (HEAD ** -0.5)
                    o_refs[idx][:, r * GW + hh * VL:r * GW + (hh + 1) * VL] = piece.astype(BF16)

    dils = [DILATIONS[idx % ng] for idx in range(n)]
    return pl.pallas_call(
        body, out_shape=[jax.ShapeDtypeStruct((t, D), BF16)] + [_dilated_shape(t, d, BF16) for d in dils]
        + [jax.ShapeDtypeStruct((t, 2 * D), F32), jax.ShapeDtypeStruct((t, 2 * D), BF16)], grid=(t // tm,),
        in_specs=[_row_spec(tm, D), _vec_spec(D), _vec_spec(D), _vec_spec(D), _row_spec(tm, VL), _row_spec(tm, VL)]
        + [_whole(wp) for wp in w_parts],
        out_specs=[_row_spec(tm, D)] + [_dilated_spec(tm, d) for d in dils] + [_row_spec(tm, 2 * D)] * 2,
        scratch_shapes=[pltpu.VMEM((strips, tm, VL), F32)], name=name, compiler_params=_params(("parallel",)),
    )(x, g, scale, shift, cos_t, sin_t, *w_parts)


def _mix_fwd(tag, x, norm, w_in, attn_wo, conv_w, conv_b, ln_g, ln_b, conv_wo, w_out, gate, cos_t, sin_t):
    w_qkv, w_u, w_g = w_in[:, :QKV], w_in[:, QKV:QKV + 2 * D], w_in[:, QKV + 2 * D:]
    h, *qkv, zu, zg = _mix_in_fwd(f"mix_in_{tag}", x, *norm, [w_qkv, w_u, w_g], cos_t, sin_t)
    n = len(DILATIONS)
    outs, lses = [], []
    for grp in range(n):
        o, lse = _attn_fwd(f"attn_fwd_{tag}_{grp}", qkv[grp], qkv[n + grp], qkv[2 * n + grp], grp)
        outs.append(o)
        lses.append(lse)
    hc, s = _conv_fwd(f"conv_fwd_{tag}", zu, conv_w, conv_b, ln_g, ln_b)
    ob, of, lj, y, ya, yc, sa, sc, x_new, f = _mix_out_fwd(f"mix_out_{tag}", outs, lses, s, zg, x, gate, attn_wo,
                                                          conv_wo, w_out)
    return x_new, h, (zu, sa, sc, qkv, ob, of, lj, hc, s, y, ya, yc, f, (w_qkv, w_u, w_g))


def _whole(w):
    return pl.BlockSpec(w.shape, lambda i: (0, 0), pipeline_mode=pl.Buffered(1))


def _merge_groups(in_refs, rows_ref, ob_ref, of_ref, lj_ref, tm):
    n = len(in_refs) // 2
    for hh in range(GW // VL):
        for idx in range(2 * n):
            d = DILATIONS[idx % n]
            for r in range(d):
                cols = slice(r * GW + hh * VL, r * GW + (hh + 1) * VL)
                rows_ref[idx, _every(d, r, tm), :] = in_refs[idx][:, cols]
        ls = [rows_ref[n + g] for g in range(n)]
        m = ls[0]
        for v in ls[1:]:
            m = jnp.maximum(m, v)
        es = [jnp.exp(v - m) for v in ls]
        tot = es[0]
        for v in es[1:]:
            tot = tot + v
        acc = (es[0] / tot) * rows_ref[0]
        for g in range(1, n):
            acc = acc + (es[g] / tot) * rows_ref[g]
        half = slice(hh * VL, (hh + 1) * VL)
        ob_ref[:, half] = acc.astype(BF16)
        of_ref[:, half] = acc
        lj_ref[:, half] = m + jnp.log(tot)


def _mix_out_fwd(name, outs, lses, s, zg, x, gate, attn_wo, conv_wo, w_out, tm=512):
    t = x.shape[0]
    n = len(outs)

    def body(*refs):
        group_refs = refs[:2 * n]
        s_ref, za_ref, zc_ref, x_ref, gate_ref, wa_ref, wc_ref, wo_ref = refs[2 * n:2 * n + 8]
        ob_ref, of_ref, lj_ref, y_ref, ya_ref, yc_ref, sa_ref, sc_ref, xn_ref, f_ref, rows_ref = refs[2 * n + 8:]
        _merge_groups(group_refs, rows_ref, ob_ref, of_ref, lj_ref, tm)
        for r0 in range(0, tm, MM_SLAB):
            rows = slice(r0, r0 + MM_SLAB)
            ya = jnp.dot(ob_ref[rows, :], wa_ref[...], preferred_element_type=F32)
            yc = jnp.dot(s_ref[rows, :], wc_ref[...], preferred_element_type=F32)
            sa, sc = _sigmoid(za_ref[rows, :].astype(F32)), _sigmoid(zc_ref[rows, :].astype(F32))
            y = (sa * ya + sc * yc).astype(BF16)
            out = jnp.dot(y, wo_ref[...], preferred_element_type=F32)
            y_ref[rows, :], ya_ref[rows, :], yc_ref[rows, :] = y, ya.astype(BF16), yc.astype(BF16)
            sa_ref[rows, :], sc_ref[rows, :] = sa.astype(BF16), sc.astype(BF16)
            xn_ref[rows, :] = x_ref[rows, :] + gate_ref[...] * out
            f_ref[rows, :] = out.astype(BF16)

    half = jax.ShapeDtypeStruct((t, D), BF16)
    group = lambda dt: jax.ShapeDtypeStruct((t, GW), dt)
    return pl.pallas_call(
        body, out_shape=[group(BF16), group(F32), group(F32)] + [half] * 5 + [jax.ShapeDtypeStruct((t, D), F32), half],
        grid=(t // tm,),
        in_specs=[_dilated_spec(tm, DILATIONS[idx % n]) for idx in range(2 * n)]
        + [_row_spec(tm, D), _row_spec(tm, D, 0), _row_spec(tm, D, 1), _row_spec(tm, D), _vec_spec(D),
           _whole(attn_wo), _whole(conv_wo), _whole(w_out)],
        out_specs=[_row_spec(tm, GW)] * 3 + [_row_spec(tm, D)] * 7,
        scratch_shapes=[pltpu.VMEM((2 * n, tm, VL), F32)], name=name, compiler_params=_params(("parallel",)),
    )(*outs, *lses, s, zg, zg, x, gate, attn_wo, conv_wo, w_out)


def _mix_out_bwd(name, dx, f, gate, sa, sc, ya, yc, o, lj, attn_wo, conv_wo, w_out, tm=512):
    t = dx.shape[0]
    n = len(DILATIONS)
    halves = GW // VL

    def body(*refs):
        dx_ref, f_ref, gate_ref, sa_ref, sc_ref, ya_ref, yc_ref = refs[:7]
        o_refs, lj_refs = refs[7:7 + halves], refs[7 + halves:7 + 2 * halves]
        wa_ref, wc_ref, wo_ref = refs[7 + 2 * halves:10 + 2 * halves]
        df_ref, dya_ref, dyc_ref, dzg_ref, ds_ref, dgate_ref = refs[10 + 2 * halves:16 + 2 * halves]
        prep_refs = refs[16 + 2 * halves:16 + 2 * halves + 3 * n]
        do_ref, dsum_ref = refs[16 + 2 * halves + 3 * n:]

        @pl.when(pl.program_id(0) == 0)
        def _():
            dgate_ref[...] = jnp.zeros_like(dgate_ref)

        for r0 in range(0, tm, MM_SLAB):
            rows = slice(r0, r0 + MM_SLAB)
            dxs = dx_ref[rows, :]
            dfs = (gate_ref[...] * dxs).astype(BF16)
            df_ref[rows, :] = dfs
            dgate_ref[...] += jnp.sum(f_ref[rows, :].astype(F32) * dxs, axis=0, keepdims=True)
            dy = lax.dot_general(dfs, wo_ref[...], NT, preferred_element_type=F32)
            ga, gc = sa_ref[rows, :].astype(F32), sc_ref[rows, :].astype(F32)
            dya, dyc = (dy * ga).astype(BF16), (dy * gc).astype(BF16)
            dya_ref[rows, :], dyc_ref[rows, :] = dya, dyc
            dzg_ref[rows, 0:D] = (dy * ya_ref[rows, :].astype(F32) * (ga * (1.0 - ga))).astype(BF16)
            dzg_ref[rows, D:] = (dy * yc_ref[rows, :].astype(F32) * (gc * (1.0 - gc))).astype(BF16)
            do = lax.dot_general(dya, wa_ref[...], NT, preferred_element_type=F32)
            for hh in range(halves):
                do_ref[hh, rows, :] = do[:, hh * VL:(hh + 1) * VL]
            ds_ref[rows, :] = lax.dot_general(dyc, wc_ref[...], NT, preferred_element_type=F32)

        for hh in range(halves):
            prod = do_ref[hh] * o_refs[hh][...]
            parts = [jnp.broadcast_to(jnp.sum(prod[:, _head_cols(h)], axis=-1, keepdims=True), (tm, HEAD))
                     for h in range(VL // HEAD)]
            dsum_ref[...] = jnp.concatenate(parts, axis=1)
            for g, d in enumerate(DILATIONS):
                for r in range(d):
                    rows, cols = _every(d, r, tm), slice(r * GW + hh * VL, r * GW + (hh + 1) * VL)
                    prep_refs[g][:, cols] = dsum_ref[rows, :]
                    prep_refs[n + g][:, cols] = do_ref[hh, rows, :].astype(BF16)
                    prep_refs[2 * n + g][:, cols] = lj_refs[hh][rows, :]

    half = jax.ShapeDtypeStruct((t, D), BF16)
    prep_shapes = [_dilated_shape(t, d, dt) for dt in (F32, BF16, F32) for d in DILATIONS]
    half_specs = [_row_spec(tm, VL, hh) for hh in range(halves)]
    outs = pl.pallas_call(
        body, out_shape=[half, half, half, jax.ShapeDtypeStruct((t, 2 * D), BF16), jax.ShapeDtypeStruct((t, D), F32),
                         jax.ShapeDtypeStruct((1, D), F32)] + prep_shapes, grid=(t // tm,),
        in_specs=[_row_spec(tm, D), _row_spec(tm, D), _vec_spec(D)] + [_row_spec(tm, D)] * 4 + half_specs * 2
        + [_whole(attn_wo), _whole(conv_wo), _whole(w_out)],
        out_specs=[_row_spec(tm, D)] * 3 + [_row_spec(tm, 2 * D), _row_spec(tm, D), _vec_spec(D)]
        + [_dilated_spec(tm, d) for d in DILATIONS] * 3,
        scratch_shapes=[pltpu.VMEM((halves, tm, VL), F32), pltpu.VMEM((tm, VL), F32)],
        name=name, compiler_params=_params(("arbitrary",)),
    )(dx, f, gate, sa, sc, ya, yc, *([o] * halves), *([lj] * halves), attn_wo, conv_wo, w_out)
    return outs[:6], outs[6:]


def _mix_bwd(tag, dx, x, h, saved, attn_wo, conv_w, ln_g, ln_b, conv_wo, w_out, g, scale, gate, cos_t, sin_t):
    zu, sa, sc, qkv, ob, of, lj, hc, s, y, ya, yc, f, w_parts = saved
    n = len(DILATIONS)
    (df, dya, dyc, dzg, ds, dgate), prep = _mix_out_bwd(f"mix_out_bwd_{tag}", dx, f, gate, sa, sc, ya, yc, of, lj,
                                                         attn_wo, conv_wo, w_out)
    dw_out = _mm_tn(f"mix_dwout_{tag}", y, df, tk=1024, tn=1024, tt=2048)
    dw_attn = _mm_tn(f"mix_dwattn_{tag}", ob, dya, tk=GW, tn=1024, tt=2048)
    dw_conv_o = _mm_tn(f"mix_dwconvo_{tag}", s, dyc, tk=1024, tn=1024, tt=2048)

    dqs, dks, dvs = [], [], []
    for grp in range(n):
        dq, dk, dv = _attn_bwd(f"attn_bwd_{tag}_{grp}", qkv[grp], qkv[n + grp], qkv[2 * n + grp], prep[n + grp],
                               prep[2 * n + grp], prep[grp], grp)
        dqs.append(dq)
        dks.append(dk)
        dvs.append(dv)
    dzqkv = _rope_bwd(f"rope_bwd_{tag}", dqs + dks + dvs, cos_t, sin_t)

    dzu, dconv_w, dln_g, dln_b, dconv_b = _conv_bwd(f"conv_bwd_{tag}", zu, hc, ds, conv_w, ln_g, ln_b)

    dz_parts = [dzqkv, dzu, dzg]
    dw_in = jnp.concatenate(
        [_mm_tn(f"mix_dwin_{tag}_{i}", h, dzp, tk=1024, tn=dzp.shape[1] // 2, tt=2048)
         for i, dzp in enumerate(dz_parts)], axis=1)
    dx_in, dgn, dscale, dshift = _proj_norm_bwd(f"mix_dh_{tag}", dz_parts, list(w_parts), x, dx, g, scale)
    grads = dict(w_in=dw_in, attn_wo=dw_attn, conv_w=dconv_w[:CONV_K], conv_b=dconv_b, conv_ln_g=dln_g,
                 conv_ln_b=dln_b, conv_wo=dw_conv_o, w_out=dw_out)
    return dx_in, grads, dgn, (dshift, dscale, dgate)


def _local_step(x, mod, target, w, wf):
    t = x.shape[0]
    cos_t, sin_t = _rope_tables(t)
    row = lambda v: v.reshape(1, -1)
    conv_w_pad = jnp.concatenate([wf["conv_w"], jnp.zeros((DEPTH, HALO - CONV_K, D), F32)], axis=1)

    saved = []
    for l in range(DEPTH):
        mods = [mod[l:l + 1, i * D:(i + 1) * D] for i in range(N_MOD)]
        gains = [row(wf["norm_g"][l, i]) for i in range(3)]
        lay = dict(mods=mods, gains=gains)

        lay["x0"] = x
        x, lay["h0"], lay["ffn0"] = _ffn_fwd(f"a_{l}", x, (gains[0], mods[1], mods[0]), w, (l, 0), mods[2])
        lay["x1"] = x
        x, lay["h1"], lay["mix"] = _mix_fwd(f"{l}", x, (gains[1], mods[4], mods[3]), w["w_in"][l], w["attn_wo"][l],
                                            conv_w_pad[l],
                                 row(wf["conv_b"][l]), row(wf["conv_ln_g"][l]), row(wf["conv_ln_b"][l]),
                                 w["conv_wo"][l], w["w_out"][l], mods[5], cos_t, sin_t)
        lay["x2"] = x
        x, lay["h2"], lay["ffn1"] = _ffn_fwd(f"b_{l}", x, (gains[2], mods[7], mods[6]), w, (l, 1), mods[8])
        saved.append(lay)

    dx, dfinal_g, loss_cols = _loss_bwd("loss_head", x, target, row(wf["final_g"]))

    ffn_grads = {n: jnp.zeros(w[n].shape, F32) for n in ("ffn_wg", "ffn_wu", "ffn_wd")}
    per_layer = []
    for l in reversed(range(DEPTH)):
        lay = saved[l]
        mods, gains = lay["mods"], lay["gains"]
        dx, ffn_grads, dgn2, dmod2 = _ffn_bwd(f"b_{l}", dx, lay["x2"], lay["h2"], lay["ffn1"], w, (l, 1),
                                              gains[2], mods[7], mods[8], ffn_grads)
        dx, gm, dgn1, dmod1 = _mix_bwd(f"{l}", dx, lay["x1"], lay["h1"], lay["mix"], w["attn_wo"][l],
                                       conv_w_pad[l], row(wf["conv_ln_g"][l]), row(wf["conv_ln_b"][l]),
                                       w["conv_wo"][l], w["w_out"][l], gains[1], mods[4], mods[5], cos_t, sin_t)
        dx, ffn_grads, dgn0, dmod0 = _ffn_bwd(f"a_{l}", dx, lay["x0"], lay["h0"], lay["ffn0"], w, (l, 0),
                                              gains[0], mods[1], mods[2], ffn_grads)
        g = dict(gm)
        g["dmod"] = jnp.concatenate(list(dmod0) + list(dmod1) + list(dmod2), axis=1)
        g["norm_g"] = [dgn0[0], dgn1[0], dgn2[0]]
        for name in ("conv_b", "conv_ln_g", "conv_ln_b"):
            g[name] = g[name][0]
        per_layer.append(g)
    per_layer.reverse()
    grads = {name: [per_layer[l][name] for l in range(DEPTH)] for name in per_layer[0]}
    grads["dmod"] = jnp.concatenate(grads["dmod"], axis=0)
    grads.update(ffn_grads)
    grads["final_g"] = dfinal_g[0]
    return loss_cols, dx, grads


def _split_bits(w):
    bits = lax.bitcast_convert_type(w, jnp.uint32)
    hi = lax.bitcast_convert_type((bits >> 16).astype(jnp.uint16), BF16)
    lo = lax.bitcast_convert_type((bits & 0xFFFF).astype(jnp.uint16), BF16)
    return hi, lo


def _join_bits(hi, lo):
    h = lax.bitcast_convert_type(hi, jnp.uint16).astype(jnp.uint32)
    l = lax.bitcast_convert_type(lo, jnp.uint16).astype(jnp.uint32)
    return lax.bitcast_convert_type((h << 16) | l, F32)


def _pack(parts, rows):
    out = []
    for p in parts:
        flat = p.reshape(-1)
        pad = -flat.shape[0] % LANES
        out.append(jnp.concatenate([flat, jnp.zeros((pad,), flat.dtype)]) if pad else flat)
    flat = jnp.concatenate(out)
    return jnp.concatenate([flat, jnp.zeros((rows * LANES - flat.shape[0],), flat.dtype)]).reshape(rows, LANES)


def _unpack(buf, shapes):
    out, row = [], 0
    for shape in shapes:
        size = 1
        for s in shape:
            size *= s
        rows = -(-size // LANES)
        out.append(buf[row:row + rows].reshape(-1)[:size].reshape(shape))
        row += rows
    return out


def _place():
    x, y, c = lax.axis_index("x"), lax.axis_index("y"), lax.axis_index("c")
    chips = [(1 - x, y), (x, 1 - y), (1 - x, 1 - y)]
    return x, y, c, chips


def _chip_index():
    return (2 * lax.axis_index("x") + lax.axis_index("y")).astype(jnp.int32)


HBM_SPEC = pl.BlockSpec(memory_space=pltpu.HBM)


def _gather_rows(name, block):
    m, n = block.shape

    def body(x_ref, out_ref, send_sems, recv_sems, local_sem):
        x, y, c, chips = _place()
        me, sibling = (x, y, c), (x, y, 1 - c)

        def rows(px, py, pc):
            return out_ref.at[pl.ds((4 * px + 2 * py + pc) * m, m), :]

        def copy(k, owner, to, src=None):
            return pltpu.make_async_remote_copy(
                src_ref=rows(*owner) if src is None else src, dst_ref=rows(*owner), send_sem=send_sems.at[k],
                recv_sem=recv_sems.at[k], device_id=to, device_id_type=MESH)

        mine = pltpu.make_async_copy(x_ref, rows(*me), local_sem)
        mine.start()
        first = [copy(0, me, sibling, src=x_ref)] + [copy(1 + j, me, (*chip, c), src=x_ref)
                                                     for j, chip in enumerate(chips)]
        for cp in first:
            cp.start()
        passed = [copy(4 + j, (*chip, c), sibling) for j, chip in enumerate(chips)]
        for j, chip in enumerate(chips):
            copy(1 + j, (*chip, c), me).wait_recv()
            passed[j].start()
        copy(0, sibling, me).wait_recv()
        for j, chip in enumerate(chips):
            copy(4 + j, (*chip, 1 - c), me).wait_recv()
        for cp in first + passed:
            cp.wait_send()
        mine.wait()

    whole = pl.BlockSpec(memory_space=pltpu.VMEM)
    return pl.pallas_call(
        body, out_shape=jax.ShapeDtypeStruct((N_DEV * m, n), block.dtype), in_specs=[whole], out_specs=whole,
        scratch_shapes=[pltpu.SemaphoreType.DMA((7,)), pltpu.SemaphoreType.DMA((7,)), pltpu.SemaphoreType.DMA],
        name=name,
    )(block)


def _gather_weights(arrays):
    n = len(arrays)

    def body(*refs):
        outs, send_sems, recv_sems = refs[n:2 * n], refs[2 * n], refs[2 * n + 1]
        x, y, c, chips = _place()
        me = 2 * x + y
        sibling = (x, y, 1 - c)
        there = [2 * chip[0] + chip[1] for chip in chips]

        def copy(a, k, chip, layer, to):
            piece = outs[a].at[chip, layer]
            return pltpu.make_async_remote_copy(
                src_ref=piece, dst_ref=piece, send_sem=send_sems.at[6 * a + k], recv_sem=recv_sems.at[6 * a + k],
                device_id=to, device_id_type=MESH)

        first = [copy(a, j, me, c, (*chip, c)) for a in range(n) for j, chip in enumerate(chips)]
        for cp in first:
            cp.start()
        passed = []
        for a in range(n):
            for j in range(3):
                copy(a, j, there[j], c, sibling).wait_recv()
                passed.append(copy(a, 3 + j, there[j], c, sibling))
                passed[-1].start()
        for a in range(n):
            for j in range(3):
                copy(a, 3 + j, there[j], 1 - c, sibling).wait_recv()
        for cp in first + passed:
            cp.wait_send()

    return pl.pallas_call(
        body, out_shape=[jax.ShapeDtypeStruct(a.shape, a.dtype) for a in arrays],
        in_specs=[HBM_SPEC] * n, out_specs=[HBM_SPEC] * n,
        scratch_shapes=[pltpu.SemaphoreType.DMA((6 * n,)), pltpu.SemaphoreType.DMA((6 * n,))],
        input_output_aliases={i: i for i in range(n)}, name="gather_weights",
    )(*arrays)


def _row_block(rows, cols):
    for cand in (512, 256, 128, 64, 32, 16):
        if rows % cand == 0 and cand * cols * 4 <= 2560 * 1024:
            return cand
    return rows


def _swap_layers(grads):
    n = len(grads)

    def body(*refs):
        g_refs, out_refs, send_sems, recv_sems = refs[:n], refs[n:2 * n], refs[2 * n], refs[2 * n + 1]
        x, y, c, _ = _place()
        copies = [pltpu.make_async_remote_copy(
            src_ref=g_refs[a].at[:, 1 - c], dst_ref=out_refs[a], send_sem=send_sems.at[a], recv_sem=recv_sems.at[a],
            device_id=(x, y, 1 - c), device_id_type=MESH) for a in range(n)]
        for cp in copies:
            cp.start()
        for cp in copies:
            cp.wait()

    return pl.pallas_call(
        body, out_shape=[jax.ShapeDtypeStruct((g.shape[0],) + g.shape[2:], F32) for g in grads],
        in_specs=[HBM_SPEC] * n, out_specs=[HBM_SPEC] * n,
        scratch_shapes=[pltpu.SemaphoreType.DMA((n,)), pltpu.SemaphoreType.DMA((n,))], name="swap_layers",
    )(*grads)


def _add_layers(name, grad, other):
    shards, _, rows, cols = grad.shape
    tr = _row_block(rows, cols)

    def body(c_ref, g_ref, o_ref, out_ref):
        out_ref[...] = (g_ref[...] + o_ref[...]).astype(BF16)

    c = lax.axis_index("c").astype(jnp.int32).reshape(1)
    grid_spec = pltpu.PrefetchScalarGridSpec(
        num_scalar_prefetch=1, grid=(shards, rows // tr),
        in_specs=[pl.BlockSpec((None, None, tr, cols), lambda k, i, c_ref: (k, c_ref[0], i, 0)),
                  pl.BlockSpec((None, tr, cols), lambda k, i, c_ref: (k, i, 0))],
        out_specs=pl.BlockSpec((None, tr, cols), lambda k, i, c_ref: (k, i, 0)))
    return pl.pallas_call(
        body, out_shape=jax.ShapeDtypeStruct((shards, rows, cols), BF16), grid_spec=grid_spec,
        name=name, compiler_params=_params(("parallel", "parallel")),
    )(c, grad, other)


def _scatter_chips(parts):
    n = len(parts)

    def body(*refs):
        p_refs, out_refs, send_sems, recv_sems = refs[:n], refs[n:2 * n], refs[2 * n], refs[2 * n + 1]
        x, y, c, chips = _place()
        me = 2 * x + y
        there = [2 * chip[0] + chip[1] for chip in chips]

        def copy(a, j, slot):
            return pltpu.make_async_remote_copy(
                src_ref=p_refs[a].at[there[j]], dst_ref=out_refs[a].at[slot], send_sem=send_sems.at[3 * a + j],
                recv_sem=recv_sems.at[3 * a + j], device_id=(*chips[j], c), device_id_type=MESH)

        sends = [copy(a, j, me) for a in range(n) for j in range(3)]
        for cp in sends:
            cp.start()
        for a in range(n):
            for j in range(3):
                copy(a, j, there[j]).wait_recv()
        for cp in sends:
            cp.wait_send()

    return pl.pallas_call(
        body, out_shape=[jax.ShapeDtypeStruct(p.shape, p.dtype) for p in parts],
        in_specs=[HBM_SPEC] * n, out_specs=[HBM_SPEC] * n,
        scratch_shapes=[pltpu.SemaphoreType.DMA((3 * n,)), pltpu.SemaphoreType.DMA((3 * n,))],
        name="scatter_chips",
    )(*parts)


def _add_chips(name, part, others):
    shards, rows, cols = part.shape
    tr = _row_block(rows, cols)

    def body(pos_ref, own_ref, r0_ref, r1_ref, r2_ref, r3_ref, out_ref):
        me = pos_ref[0]
        own = own_ref[...].astype(F32)
        total = None
        for k, r_ref in enumerate((r0_ref, r1_ref, r2_ref, r3_ref)):
            term = jnp.where(me == k, own, r_ref[...].astype(F32))
            total = term if total is None else total + term
        out_ref[...] = total

    def other(k):
        return pl.BlockSpec((None, tr, cols),
                            lambda i, pos, k=k: (jnp.where(pos[0] == k, (k + 1) % shards, k), i, 0))

    pos = jnp.stack([_chip_index(), lax.axis_index("c").astype(jnp.int32)])
    grid_spec = pltpu.PrefetchScalarGridSpec(
        num_scalar_prefetch=1, grid=(rows // tr,),
        in_specs=[pl.BlockSpec((None, tr, cols), lambda i, pos: (pos[0], i, 0))] + [other(k) for k in range(shards)],
        out_specs=pl.BlockSpec((None, tr, cols), lambda i, pos: (pos[1], i, 0)))
    return pl.pallas_call(
        body, out_shape=jax.ShapeDtypeStruct((DEPTH, rows, cols), F32), grid_spec=grid_spec,
        name=name, compiler_params=_params(("parallel",)),
    )(pos, part, others, others, others, others)


def _join_layers(arrays):
    n = len(arrays)

    def body(*refs):
        outs, send_sems, recv_sems = refs[n:2 * n], refs[2 * n], refs[2 * n + 1]
        x, y, c, _ = _place()

        def copy(a, layer):
            piece = outs[a].at[layer]
            return pltpu.make_async_remote_copy(src_ref=piece, dst_ref=piece, send_sem=send_sems.at[a],
                                                recv_sem=recv_sems.at[a], device_id=(x, y, 1 - c),
                                                device_id_type=MESH)

        sends = [copy(a, c) for a in range(n)]
        for cp in sends:
            cp.start()
        for a in range(n):
            copy(a, 1 - c).wait_recv()
        for cp in sends:
            cp.wait_send()

    return pl.pallas_call(
        body, out_shape=[jax.ShapeDtypeStruct(a.shape, a.dtype) for a in arrays],
        in_specs=[HBM_SPEC] * n, out_specs=[HBM_SPEC] * n,
        scratch_shapes=[pltpu.SemaphoreType.DMA((n,)), pltpu.SemaphoreType.DMA((n,))],
        input_output_aliases={i: i for i in range(n)}, name="join_layers",
    )(*arrays)


def _reduce_scatter(grads):
    sums = [_add_layers(f"add_layers_{a}", g, o) for a, (g, o) in enumerate(zip(grads, _swap_layers(grads)))]
    others = _scatter_chips(sums)
    return _join_layers([_add_chips(f"add_chips_{a}", p, o) for a, (p, o) in enumerate(zip(sums, others))])


def _adamw(name, w, g, m, v):
    shape = w.shape
    cols = shape[-1]
    rows = w.size // cols
    tr = rows
    for cand in (512, 256, 128, 64, 32, 16, 8):
        if rows % cand == 0 and cand * cols * 4 <= 2 * 1024 * 1024:
            tr = cand
            break

    def body(w_ref, g_ref, m_ref, v_ref, go_ref, d_ref, nm_ref, nv_ref):
        gv = g_ref[...]
        go_ref[...] = gv
        nm = ADAM_B1 * m_ref[...] + (1.0 - ADAM_B1) * gv
        nv = ADAM_B2 * v_ref[...] + (1.0 - ADAM_B2) * (gv * gv)
        m_hat = nm / (1.0 - ADAM_B1 ** ADAM_STEP)
        v_hat = nv / (1.0 - ADAM_B2 ** ADAM_STEP)
        d_ref[...] = -ADAM_LR * (m_hat / (jnp.sqrt(v_hat) + ADAM_EPS) + ADAM_WD * w_ref[...])
        nm_ref[...] = nm
        nv_ref[...] = nv

    spec = pl.BlockSpec((tr, cols), lambda i: (i, 0))
    two = lambda a: a.reshape(rows, cols)
    outs = pl.pallas_call(
        body, out_shape=[jax.ShapeDtypeStruct((rows, cols), F32)] * 4, grid=(rows // tr,),
        in_specs=[spec] * 4, out_specs=[spec] * 4, name=name, compiler_params=_params(("parallel",)),
    )(two(w), two(g), two(m), two(v))
    return [o.reshape(shape) for o in outs]


BIG = ("ffn_wg", "ffn_wu", "ffn_wd", "w_in", "conv_wo", "w_out")
MISC_ROWS = 96


def _own_slot(shard):
    return lax.dynamic_update_slice(jnp.zeros((SHARDS,) + shard.shape, shard.dtype), shard[None],
                                    (_chip_index(),) + (0,) * shard.ndim)


def _as_matrices(a):
    return a.reshape(a.shape[0], a.shape[1], -1, a.shape[-1])


def kernel(x, c, ada_w, ada_b, norm_g, ffn_wg, ffn_wu, ffn_wd, w_in, attn_wo, conv_w, conv_b, conv_ln_g, conv_ln_b, conv_wo, w_out, final_g, loss_target, m_ada_w, m_ada_b, m_norm_g, m_ffn_wg, m_ffn_wu, m_ffn_wd, m_w_in, m_attn_wo, m_conv_w, m_conv_b, m_conv_ln_g, m_conv_ln_b, m_conv_wo, m_w_out, m_final_g, v_ada_w, v_ada_b, v_norm_g, v_ffn_wg, v_ffn_wu, v_ffn_wd, v_w_in, v_attn_wo, v_conv_w, v_conv_b, v_conv_ln_g, v_conv_ln_b, v_conv_wo, v_w_out, v_final_g):
    weights = dict(ada_w=ada_w, ada_b=ada_b, norm_g=norm_g, ffn_wg=ffn_wg, ffn_wu=ffn_wu, ffn_wd=ffn_wd, w_in=w_in,
                   attn_wo=attn_wo, conv_w=conv_w, conv_b=conv_b, conv_ln_g=conv_ln_g, conv_ln_b=conv_ln_b,
                   conv_wo=conv_wo, w_out=w_out, final_g=final_g)
    moments_m = dict(ada_w=m_ada_w, ada_b=m_ada_b, norm_g=m_norm_g, ffn_wg=m_ffn_wg, ffn_wu=m_ffn_wu,
                     ffn_wd=m_ffn_wd, w_in=m_w_in, attn_wo=m_attn_wo, conv_w=m_conv_w, conv_b=m_conv_b,
                     conv_ln_g=m_conv_ln_g, conv_ln_b=m_conv_ln_b, conv_wo=m_conv_wo, w_out=m_w_out,
                     final_g=m_final_g)
    moments_v = dict(ada_w=v_ada_w, ada_b=v_ada_b, norm_g=v_norm_g, ffn_wg=v_ffn_wg, ffn_wu=v_ffn_wu,
                     ffn_wd=v_ffn_wd, w_in=v_w_in, attn_wo=v_attn_wo, conv_w=v_conv_w, conv_b=v_conv_b,
                     conv_ln_g=v_conv_ln_g, conv_ln_b=v_conv_ln_b, conv_wo=v_conv_wo, w_out=v_w_out,
                     final_g=v_final_g)
    layers, shards = range(DEPTH), range(SHARDS)

    bits = {n: _split_bits(weights[n]) for n in EXACT}
    misc_w = jnp.stack([_pack([attn_wo[l].astype(BF16), bits["norm_g"][0][l], bits["norm_g"][1][l],
                               bits["conv_w"][0][l], bits["conv_w"][1][l]], MISC_ROWS) for l in layers])
    sent = [_own_slot(weights[n].astype(BF16)) for n in BIG] + [_own_slot(misc_w)]
    got = dict(zip(BIG + ("misc",), _gather_weights(sent)))
    w = {n: got[n] for n in ("ffn_wg", "ffn_wu", "ffn_wd")}
    w["w_in"] = got["w_in"].transpose(1, 2, 0, 3).reshape(DEPTH, D, -1)
    for n in ("conv_wo", "w_out"):
        w[n] = got[n].transpose(1, 0, 2, 3).reshape(DEPTH, D, D)
    misc_shapes = [(GW, GW), (3, GW), (3, GW), (CONV_K, GW), (CONV_K, GW)]
    pieces = [[_unpack(got["misc"][k, l], misc_shapes) for k in shards] for l in layers]
    whole = lambda i: jnp.stack([jnp.concatenate([pieces[l][k][i] for k in shards], axis=1) for l in layers])
    w["attn_wo"] = whole(0)
    vectors = dict(ada_b=ada_b, conv_b=conv_b, conv_ln_g=conv_ln_g, conv_ln_b=conv_ln_b, final_g=final_g,
                   norm_g=_join_bits(whole(1), whole(2)), conv_w=_join_bits(whole(3), whole(4)))

    me = 2 * _chip_index() + lax.axis_index("c").astype(jnp.int32)
    pad_rows = lambda a, rows: jnp.concatenate([a, jnp.zeros((rows - a.shape[0], a.shape[1]), a.dtype)])
    c_all = _gather_rows("gather_c", pad_rows(c, SUBLANES)).reshape(N_DEV, SUBLANES, D)[:, 0]
    mod_cols = _mod_fwd("mod_fwd", c_all, ada_w.astype(BF16))
    by_dev = _gather_rows("gather_mod", mod_cols).reshape(N_DEV, N_DEV, DEPTH, -1)
    mine = lax.dynamic_index_in_dim(by_dev[0::2], me, axis=1, keepdims=False)
    mod = mine.transpose(1, 0, 2).reshape(DEPTH, -1) + ada_b

    loss_cols, dx, grads = _local_step(x[0], mod, loss_target[0], w, vectors)
    loss = lax.psum(jnp.sum(loss_cols), ("x", "y", "c"))

    dmod_rows = DEPTH * N_MOD * D // LANES
    dmod_all = _gather_rows("gather_dmod", pad_rows(grads["dmod"].reshape(dmod_rows, LANES), 3 * SUBLANES))
    dmod_all = dmod_all.reshape(N_DEV, 3 * SUBLANES, LANES)[:, :dmod_rows].reshape(N_DEV, DEPTH, -1)
    grad_ada_b = _sum_devices("ada_b_grad", dmod_all.reshape(N_DEV, -1)).reshape(DEPTH, -1)
    cols = ada_w.shape[-1]
    dmod_cols = lax.dynamic_slice_in_dim(dmod_all, _chip_index() * cols, cols, axis=2).transpose(1, 0, 2)
    grad_ada_w = _mod_bwd("mod_bwd", c_all.T, dmod_cols)

    cols_of = lambda a, k: a[..., k * GW:(k + 1) * GW]
    misc_g = jnp.stack([jnp.stack([_pack(
        [cols_of(grads["attn_wo"][l], k), cols_of(jnp.stack(grads["norm_g"][l]), k), cols_of(grads["conv_w"][l], k),
         grads["conv_b"][l], grads["conv_ln_g"][l], grads["conv_ln_b"][l],
         grads["final_g"] if l == 0 else jnp.zeros_like(grads["final_g"])], MISC_ROWS)
        for l in layers]) for k in shards])
    by_chip = dict(
        ffn_wg=grads["ffn_wg"], ffn_wu=grads["ffn_wu"], ffn_wd=grads["ffn_wd"],
        w_in=jnp.stack(grads["w_in"]).reshape(DEPTH, D, SHARDS, -1).transpose(2, 0, 1, 3),
        conv_wo=jnp.stack(grads["conv_wo"]).reshape(DEPTH, SHARDS, -1, D).transpose(1, 0, 2, 3),
        w_out=jnp.stack(grads["w_out"]).reshape(DEPTH, SHARDS, -1, D).transpose(1, 0, 2, 3))
    reduced = _reduce_scatter([_as_matrices(by_chip[n]) for n in BIG] + [misc_g])
    summed = {n: r.reshape(weights[n].shape) for n, r in zip(BIG, reduced)}
    small_shapes = [(GW, GW), (3, GW), (CONV_K, GW), (D,), (D,), (D,), (D,)]
    small = [_unpack(reduced[-1][l], small_shapes) for l in layers]
    for i, n in enumerate(("attn_wo", "norm_g", "conv_w", "conv_b", "conv_ln_g", "conv_ln_b")):
        summed[n] = jnp.stack([small[l][i] for l in layers])
    summed["final_g"] = small[0][6]
    summed["ada_w"], summed["ada_b"] = grad_ada_w, grad_ada_b

    grad_out, deltas, new_m, new_v = {}, {}, {}, {}
    for n in WEIGHTS:
        grad_out[n], deltas[n], new_m[n], new_v[n] = _adamw(f"adamw_{n}", weights[n], summed[n], moments_m[n],
                                                           moments_v[n])

    return (loss, dx[None], *[grad_out[n] for n in WEIGHTS], *[deltas[n] for n in WEIGHTS],
            *[new_m[n] for n in WEIGHTS], *[new_v[n] for n in WEIGHTS])
```

```python
import functools

import jax
import jax.numpy as jnp
from jax import lax
from jax.experimental import pallas as pl
from jax.experimental.pallas import tpu as pltpu

F32 = jnp.float32
BF16 = jnp.bfloat16

D = 1024
DFF = 2816
HEAD = 64
GW = 256
DILATIONS = (1, 4, 16)
BAND = 128
QKV = 2304
CONV_K = 31
HALO = 32
N_MOD = 9
EPS = 1e-6
NEG_INF = -1e30
DEPTH = 2

SHARDS = 4
FSH = DFF // SHARDS
LANES = 1024
VL = 128

ADAM_LR = 0.001
ADAM_B1 = 0.9
ADAM_B2 = 0.999
ADAM_EPS = 1e-08
ADAM_WD = 0.01
ADAM_STEP = 10

VMEM_LIMIT = 56 * 1024 * 1024

EXACT = ("norm_g", "conv_w")
WEIGHTS = ("ada_w", "ada_b", "norm_g", "ffn_wg", "ffn_wu", "ffn_wd", "w_in", "attn_wo", "conv_w", "conv_b",
           "conv_ln_g", "conv_ln_b", "conv_wo", "w_out", "final_g")

MESH = pl.DeviceIdType.MESH


def _params(sem=None):
    return pltpu.CompilerParams(dimension_semantics=sem, vmem_limit_bytes=VMEM_LIMIT)


def _sigmoid(v):
    return jax.nn.sigmoid(v)


MM_SLAB = 256


def _mm_tn(name, a, b, *, tk, tn, tt, a_fn=None, a_tiled=False, b_tiled=False, into=None):
    a_list = list(a) if a_fn is not None else [a]
    na = len(a_list)
    t = a_list[0].shape[-2]
    nk = a_list[0].shape[0] if a_tiled else a_list[0].shape[1] // tk
    nn = b.shape[0] if b_tiled else b.shape[1] // tn
    tt = min(tt, t)
    steps = t // tt
    has_into = into is not None

    def body(*refs):
        refs = refs[1:] if has_into else refs
        a_refs, b_ref, o_ref, acc_ref = refs[:na], refs[na], refs[na + 1], refs[na + 2]
        s = pl.program_id(2)

        @pl.when(s == 0)
        def _():
            acc_ref[...] = jnp.zeros_like(acc_ref)

        av = a_refs[0][...] if a_fn is None else a_fn([r[...] for r in a_refs])
        acc_ref[...] += lax.dot_general(av, b_ref[...], (((0,), (0,)), ((), ())), preferred_element_type=F32)

        @pl.when(s == steps - 1)
        def _():
            o_ref[...] = acc_ref[...]

    a_spec = (pl.BlockSpec((None, tt, tk), lambda i, j, s: (i, s, 0)) if a_tiled
              else pl.BlockSpec((tt, tk), lambda i, j, s: (s, i)))
    b_spec = (pl.BlockSpec((None, tt, tn), lambda i, j, s: (j, s, 0)) if b_tiled
              else pl.BlockSpec((tt, tn), lambda i, j, s: (s, j)))
    if a_tiled:
        out_dims, tile_index = (nk, tk, nn * tn), lambda i, j, s: (i, 0, j)
    elif b_tiled:
        out_dims, tile_index = (nn, nk * tk, tn), lambda i, j, s: (j, i, 0)
    else:
        out_dims, tile_index = (nk * tk, nn * tn), lambda i, j, s: (i, j)
    tiled = a_tiled or b_tiled
    if has_into:
        buf, lead = into
        def out_index(i, j, s):
            idx = tile_index(i, j, s)
            return (idx[0], *lead, *idx[1:])
        out_spec = pl.BlockSpec((None,) * (1 + len(lead)) + (tk, tn), out_index)
        out_shape = jax.ShapeDtypeStruct(buf.shape, buf.dtype)
        extra_in, extra_specs, aliases = [buf], [pl.BlockSpec(memory_space=pl.ANY)], {0: 0}
    else:
        out_spec = pl.BlockSpec(((None,) if tiled else ()) + (tk, tn), tile_index)
        out_shape = jax.ShapeDtypeStruct(out_dims, F32)
        extra_in, extra_specs, aliases = [], [], {}
    return pl.pallas_call(
        body, out_shape=out_shape, grid=(nk, nn, steps), in_specs=extra_specs + [a_spec] * na + [b_spec],
        out_specs=out_spec, scratch_shapes=[pltpu.VMEM((tk, tn), F32)], input_output_aliases=aliases, name=name,
        compiler_params=_params(("parallel", "parallel", "arbitrary")),
    )(*extra_in, *a_list, b)


def _row_spec(tm, width, col=0):
    return pl.BlockSpec((tm, width), functools.partial(lambda i, col: (i, col), col=col))


def _vec_spec(width):
    return pl.BlockSpec((1, width), lambda i: (0, 0))


def _proj_norm_bwd(name, dz_parts, w_parts, x, dres, g, scale, tm=256):
    t = x.shape[0]
    steps = t // tm
    n = len(dz_parts)

    def body(*refs):
        dz_refs, w_refs = refs[:n], refs[n:2 * n]
        x_ref, dres_ref, g_ref, sc_ref, dx_ref, dg_ref, dsc_ref, dsh_ref = refs[2 * n:]
        i = pl.program_id(0)

        @pl.when(i == 0)
        def _():
            dg_ref[...] = jnp.zeros_like(dg_ref)
            dsh_ref[...] = jnp.zeros_like(dsh_ref)

        dh = None
        for dz_ref, w_ref in zip(dz_refs, w_refs):
            part = lax.dot_general(dz_ref[...], w_ref[...], (((1,), (1,)), ((), ())), preferred_element_type=F32)
            dh = part if dh is None else dh + part
        xv = x_ref[...]
        r = lax.rsqrt(jnp.mean(xv * xv, axis=-1, keepdims=True) + EPS)
        xh = xv * r
        dxh = dh * (g_ref[...] * (1.0 + sc_ref[...]))
        dx_ref[...] = dres_ref[...] + r * (dxh - xh * jnp.mean(dxh * xh, axis=-1, keepdims=True))
        dg_ref[...] += jnp.sum(dh * xh, axis=0, keepdims=True)
        dsh_ref[...] += jnp.sum(dh, axis=0, keepdims=True)

        @pl.when(i == steps - 1)
        def _():
            acc = dg_ref[...]
            dg_ref[...] = acc * (1.0 + sc_ref[...])
            dsc_ref[...] = acc * g_ref[...]

    vec = jax.ShapeDtypeStruct((1, D), F32)
    resident = [pl.BlockSpec(wp.shape, lambda i: (0, 0), pipeline_mode=pl.Buffered(1)) for wp in w_parts]
    return pl.pallas_call(
        body, out_shape=[jax.ShapeDtypeStruct((t, D), F32), vec, vec, vec], grid=(steps,),
        in_specs=[_row_spec(tm, dz.shape[1]) for dz in dz_parts] + resident
        + [_row_spec(tm, D), _row_spec(tm, D), _vec_spec(D), _vec_spec(D)],
        out_specs=[_row_spec(tm, D), _vec_spec(D), _vec_spec(D), _vec_spec(D)],
        name=name, compiler_params=_params(("arbitrary",)),
    )(*dz_parts, *w_parts, x, dres, g, scale)


def _loss_bwd(name, x, target, g, tm=256):
    t = x.shape[0]

    def body(x_ref, t_ref, g_ref, dx_ref, dg_ref, loss_ref):
        @pl.when(pl.program_id(0) == 0)
        def _():
            dg_ref[...] = jnp.zeros_like(dg_ref)
            loss_ref[...] = jnp.zeros_like(loss_ref)

        xv = x_ref[...]
        r = lax.rsqrt(jnp.mean(xv * xv, axis=-1, keepdims=True) + EPS)
        xh = xv * r
        err = xh * g_ref[...] - t_ref[...]
        dy = err * (1.0 / D)
        dxh = dy * g_ref[...]
        dx_ref[...] = r * (dxh - xh * jnp.mean(dxh * xh, axis=-1, keepdims=True))
        dg_ref[...] += jnp.sum(dy * xh, axis=0, keepdims=True)
        loss_ref[...] += jnp.sum(err * err, axis=0, keepdims=True) * (0.5 / D)

    vec = jax.ShapeDtypeStruct((1, D), F32)
    return pl.pallas_call(
        body, out_shape=[jax.ShapeDtypeStruct((t, D), F32), vec, vec], grid=(t // tm,),
        in_specs=[_row_spec(tm, D), _row_spec(tm, D), _vec_spec(D)],
        out_specs=[_row_spec(tm, D), _vec_spec(D), _vec_spec(D)],
        name=name, compiler_params=_params(("arbitrary",)),
    )(x, target, g)


N_DEV = 8


def _mod_fwd(name, c_all, ada_w):
    cols = ada_w.shape[-1]

    def body(c_ref, w_ref, o_ref):
        cv = c_ref[...]
        ca = (cv * _sigmoid(cv)).astype(BF16)
        o_ref[...] = jnp.dot(ca, w_ref[...], preferred_element_type=F32)

    return pl.pallas_call(
        body, out_shape=jax.ShapeDtypeStruct((N_DEV, DEPTH * cols), F32), grid=(DEPTH,),
        in_specs=[pl.BlockSpec((N_DEV, D), lambda l: (0, 0)), pl.BlockSpec((None, D, cols), lambda l: (l, 0, 0))],
        out_specs=pl.BlockSpec((N_DEV, cols), lambda l: (0, l)), name=name, compiler_params=_params(("parallel",)),
    )(c_all, ada_w)


def _mod_bwd(name, c_cols, dmods, tk=256):
    cols = dmods.shape[-1]

    def body(c_ref, d_ref, o_ref):
        cv = c_ref[...]
        ca = cv * _sigmoid(cv)
        total = ca[:, 0:1] * d_ref[0:1, :]
        for b in range(1, N_DEV):
            total = total + ca[:, b:b + 1] * d_ref[b:b + 1, :]
        o_ref[...] = total

    return pl.pallas_call(
        body, out_shape=jax.ShapeDtypeStruct((DEPTH, D, cols), F32), grid=(DEPTH, D // tk),
        in_specs=[pl.BlockSpec((tk, N_DEV), lambda l, i: (i, 0)), pl.BlockSpec((None, N_DEV, cols), lambda l, i: (l, 0, 0))],
        out_specs=pl.BlockSpec((None, tk, cols), lambda l, i: (l, i, 0)), name=name,
        compiler_params=_params(("parallel", "parallel")),
    )(c_cols, dmods)


def _sum_devices(name, rows):
    n = rows.shape[1]

    def body(r_ref, o_ref):
        total = r_ref[0:1, :]
        for b in range(1, N_DEV):
            total = total + r_ref[b:b + 1, :]
        o_ref[...] = total

    whole = pl.BlockSpec(memory_space=pltpu.VMEM)
    return pl.pallas_call(body, out_shape=jax.ShapeDtypeStruct((1, n), F32), in_specs=[whole], out_specs=whole,
                          name=name)(rows)


def _rope_tables(t):
    half = HEAD // 2
    inv_freq = 10000.0 ** (-(jnp.arange(half, dtype=F32) * 2.0 / HEAD))
    ang = jnp.arange(t, dtype=F32)[:, None] * inv_freq[None, :]
    cos, sin = jnp.cos(ang), jnp.sin(ang)
    cos_t = jnp.tile(jnp.concatenate([cos, cos], axis=1), (1, VL // HEAD))
    sin_t = jnp.tile(jnp.concatenate([-sin, sin], axis=1), (1, VL // HEAD))
    return cos_t, sin_t


def _rotate(tv, cos, sin_signed):
    lane = lax.broadcasted_iota(jnp.int32, tv.shape, 1)
    first = (lane % HEAD) < (HEAD // 2)
    partner = jnp.where(first, pltpu.roll(tv, tv.shape[1] - HEAD // 2, 1), pltpu.roll(tv, HEAD // 2, 1))
    return tv * cos + partner * sin_signed


def _dilated_spec(tm, d):
    return pl.BlockSpec((tm // d, d * GW), lambda i: (i, 0))


def _dilated_shape(t, d, dtype):
    return jax.ShapeDtypeStruct((t // d, d * GW), dtype)


def _every(d, r, tm):
    return pl.ds(r, tm // d, stride=d) if d > 1 else slice(None)


def _rope_bwd(name, grads, cos_t, sin_t, tm=512):
    ng = len(DILATIONS)
    n = len(grads)
    t = grads[0].shape[0] * DILATIONS[0]

    halves = GW // VL

    def body(*refs):
        g_refs, cos_ref, sin_ref, o_ref, rows_ref = refs[:n], refs[n], refs[n + 1], refs[n + 2], refs[n + 3]
        cos, sin = cos_ref[...], -sin_ref[...]
        for idx in range(n):
            d = DILATIONS[idx % ng]
            for hh in range(halves):
                for r in range(d):
                    cols = slice(r * GW + hh * VL, r * GW + (hh + 1) * VL)
                    rows_ref[hh, _every(d, r, tm), :] = g_refs[idx][:, cols].astype(F32)
                piece = rows_ref[hh]
                if idx < 2 * ng:
                    piece = _rotate(piece, cos, sin)
                if idx < ng:
                    piece = piece * (HEAD ** -0.5)
                o_ref[:, idx * GW + hh * VL:idx * GW + (hh + 1) * VL] = piece.astype(BF16)

    dils = [DILATIONS[idx % ng] for idx in range(n)]
    return pl.pallas_call(
        body, out_shape=jax.ShapeDtypeStruct((t, n * GW), BF16), grid=(t // tm,),
        in_specs=[_dilated_spec(tm, d) for d in dils] + [_row_spec(tm, VL)] * 2, out_specs=_row_spec(tm, n * GW),
        scratch_shapes=[pltpu.VMEM((halves, tm, VL), F32)], name=name, compiler_params=_params(("parallel",)),
    )(*grads, cos_t, sin_t)


def _head_cols(h):
    return slice(h * HEAD, (h + 1) * HEAD)


def _band_mask_q(has_prev):
    qi = lax.broadcasted_iota(jnp.int32, (BAND, 2 * BAND), 0)
    kj = lax.broadcasted_iota(jnp.int32, (BAND, 2 * BAND), 1)
    dist = qi + BAND - kj
    return (dist >= 0) & (dist <= BAND) & ((kj >= BAND) | has_prev)


def _attn_fwd(name, q, k, v, group):
    d = DILATIONS[group]
    length = q.shape[0]
    qb = min(512, length)
    sub = qb // BAND
    nblk = length // qb

    def body(q_ref, kc_ref, kp_ref, vc_ref, vp_ref, o_ref, lse_ref):
        blk = pl.program_id(1)
        k_ext = jnp.concatenate([kp_ref[...], kc_ref[...]], axis=0)
        v_ext = jnp.concatenate([vp_ref[...], vc_ref[...]], axis=0)
        for j in range(sub):
            mask = _band_mask_q((blk * sub + j) > 0)
            qj = q_ref[j * BAND:(j + 1) * BAND, :]
            kj = k_ext[j * BAND:(j + 2) * BAND, :]
            vj = v_ext[j * BAND:(j + 2) * BAND, :]
            outs, lses = [], []
            for h in range(GW // HEAD):
                s = lax.dot_general(qj[:, _head_cols(h)], kj[:, _head_cols(h)], (((1,), (1,)), ((), ())),
                                    preferred_element_type=F32)
                s = jnp.where(mask, s, NEG_INF)
                m = jnp.max(s, axis=-1, keepdims=True)
                p = jnp.exp(s - m)
                den = jnp.sum(p, axis=-1, keepdims=True)
                o = jnp.dot(p.astype(BF16), vj[:, _head_cols(h)], preferred_element_type=F32)
                outs.append(o / den)
                lses.append(jnp.broadcast_to(m + jnp.log(den), (BAND, HEAD)))
            o_ref[j * BAND:(j + 1) * BAND, :] = jnp.concatenate(outs, axis=1)
            lse_ref[j * BAND:(j + 1) * BAND, :] = jnp.concatenate(lses, axis=1)

    prev = qb // BAND
    cur = lambda r, b: (b, r)
    before = lambda r, b: (jnp.maximum(b * prev - 1, 0), r)
    big, halo = pl.BlockSpec((qb, GW), cur), pl.BlockSpec((BAND, GW), before)
    return pl.pallas_call(
        body, out_shape=[jax.ShapeDtypeStruct((length, d * GW), F32)] * 2, grid=(d, nblk),
        in_specs=[big, big, halo, big, halo], out_specs=[big] * 2, name=name,
        compiler_params=_params(("parallel", "parallel")),
    )(q, k, k, v, v)


def _attn_bwd(name, q, k, v, do, lj, dsum, group):
    d = DILATIONS[group]
    length = q.shape[0]
    qb = min(1024, length)
    sub = qb // BAND
    nblk = length // qb
    total = length // BAND

    def body(qc_ref, qn_ref, kc_ref, kp_ref, vc_ref, vp_ref, doc_ref, don_ref, ljc_ref, ljn_ref, dsc_ref, dsn_ref,
             dq_ref, dk_ref, dv_ref):
        blk = pl.program_id(1)
        k_ext = jnp.concatenate([kp_ref[...], kc_ref[...]], axis=0)
        v_ext = jnp.concatenate([vp_ref[...], vc_ref[...]], axis=0)
        heads = range(GW // HEAD)
        nt = (((1,), (1,)), ((), ()))
        tn = (((0,), (0,)), ((), ()))

        def scores(qh, doh, ljh, dsh, kh, vh, mask):
            s = lax.dot_general(qh, kh, nt, preferred_element_type=F32)
            p = jnp.where(mask, jnp.exp(s - ljh), 0.0)
            dp = lax.dot_general(doh, vh, nt, preferred_element_type=F32)
            return p.astype(BF16), (p * (dp - dsh)).astype(BF16)

        held_k, held_v = [None] * len(heads), [None] * len(heads)
        for j in range(sub):
            rows = slice(j * BAND, (j + 1) * BAND)
            rows2 = slice(j * BAND, (j + 2) * BAND)
            mask = _band_mask_q((blk * sub + j) > 0)
            dqs, dks, dvs = [], [], []
            for h in heads:
                hc = _head_cols(h)
                col = slice(h * HEAD, h * HEAD + 1)
                qh, doh, kh2 = qc_ref[rows, hc], doc_ref[rows, hc], k_ext[rows2, hc]
                p, ds = scores(qh, doh, ljc_ref[rows, col], dsc_ref[rows, col], kh2, v_ext[rows2, hc], mask)
                dqs.append(jnp.dot(ds, kh2, preferred_element_type=F32))
                dk2 = lax.dot_general(ds, qh, tn, preferred_element_type=F32)
                dv2 = lax.dot_general(p, doh, tn, preferred_element_type=F32)
                if j > 0:
                    dks.append(held_k[h] + dk2[:BAND])
                    dvs.append(held_v[h] + dv2[:BAND])
                held_k[h], held_v[h] = dk2[BAND:], dv2[BAND:]
            dq_ref[rows, :] = jnp.concatenate(dqs, axis=1)
            if j > 0:
                done = slice((j - 1) * BAND, j * BAND)
                dk_ref[done, :] = jnp.concatenate(dks, axis=1)
                dv_ref[done, :] = jnp.concatenate(dvs, axis=1).astype(BF16)

        last = slice((sub - 1) * BAND, sub * BAND)
        qi = lax.broadcasted_iota(jnp.int32, (BAND, BAND), 0)
        kj = lax.broadcasted_iota(jnp.int32, (BAND, BAND), 1)
        mask = (kj >= qi) & ((blk + 1) * sub < total)
        dks, dvs = [], []
        for h in heads:
            hc = _head_cols(h)
            col = slice(h * HEAD, h * HEAD + 1)
            qh, doh = qn_ref[:, hc], don_ref[:, hc]
            p, ds = scores(qh, doh, ljn_ref[:, col], dsn_ref[:, col], kc_ref[last, hc], vc_ref[last, hc], mask)
            dks.append(held_k[h] + lax.dot_general(ds, qh, tn, preferred_element_type=F32))
            dvs.append(held_v[h] + lax.dot_general(p, doh, tn, preferred_element_type=F32))
        dk_ref[last, :] = jnp.concatenate(dks, axis=1)
        dv_ref[last, :] = jnp.concatenate(dvs, axis=1).astype(BF16)

    prev = qb // BAND
    cur = lambda r, b: (b, r)
    before = lambda r, b: (jnp.maximum(b * prev - 1, 0), r)
    after = lambda r, b: (jnp.minimum((b + 1) * prev, total - 1), r)
    big = pl.BlockSpec((qb, GW), cur)
    nxt = pl.BlockSpec((BAND, GW), after)
    prv = pl.BlockSpec((BAND, GW), before)
    return pl.pallas_call(
        body, out_shape=[jax.ShapeDtypeStruct((length, d * GW), F32), jax.ShapeDtypeStruct((length, d * GW), F32),
                         jax.ShapeDtypeStruct((length, d * GW), BF16)], grid=(d, nblk),
        in_specs=[big, nxt, big, prv, big, prv, big, nxt, big, nxt, big, nxt],
        out_specs=[big] * 3, name=name, compiler_params=_params(("parallel", "parallel")),
    )(q, q, k, k, v, v, do, do, lj, lj, dsum, dsum)


SUBLANES = 8
CONV_CHUNK = 32
SHIFT_ROWS = HALO - SUBLANES


def _shifted_copies(buf_ref, sh_ref, tm):
    for s in range(1, SUBLANES):
        sh_ref[s - 1] = buf_ref[s:s + tm + SHIFT_ROWS, :]


def _window(buf_ref, sh_ref, offset, r0, rows):
    tiles, shift = divmod(offset, SUBLANES)
    src = buf_ref if shift == 0 else sh_ref.at[shift - 1]
    return src[pl.ds(pl.multiple_of(r0 + tiles * SUBLANES, SUBLANES), rows), :]


def _conv_fwd(name, zu, conv_w, conv_b, ln_g, ln_b, tm=256):
    t = zu.shape[0]
    per = tm // HALO

    def body(a_ref, gl_ref, ah_ref, glh_ref, w_ref, b_ref, g_ref, beta_ref, hc_ref, s_ref, ext_ref, sh_ref):
        i = pl.program_id(0)
        halo = ah_ref[...] * _sigmoid(glh_ref[...])
        ext_ref[0:HALO, :] = jnp.where(i > 0, halo, 0.0)
        ext_ref[HALO:, :] = a_ref[...] * _sigmoid(gl_ref[...])
        _shifted_copies(ext_ref, sh_ref, tm)

        def chunk(r, carry):
            r0 = pl.multiple_of(r * CONV_CHUNK, CONV_CHUNK)
            part = jnp.broadcast_to(b_ref[...], (CONV_CHUNK, D))
            for kk in range(CONV_K):
                part = part + w_ref[kk:kk + 1, :] * _window(ext_ref, sh_ref, HALO - CONV_K + 1 + kk, r0, CONV_CHUNK)
            hc_ref[pl.ds(r0, CONV_CHUNK), :] = part
            return carry

        lax.fori_loop(0, tm // CONV_CHUNK, chunk, 0)
        acc = hc_ref[...]
        mu = jnp.mean(acc, axis=-1, keepdims=True)
        xc = acc - mu
        var = jnp.mean(xc * xc, axis=-1, keepdims=True)
        ln = xc * lax.rsqrt(var + EPS) * g_ref[...] + beta_ref[...]
        s_ref[...] = (ln * _sigmoid(ln)).astype(BF16)

    halo_map = lambda col: (lambda i: (jnp.maximum(i * per - 1, 0), col))
    return pl.pallas_call(
        body, out_shape=[jax.ShapeDtypeStruct((t, D), F32), jax.ShapeDtypeStruct((t, D), BF16)], grid=(t // tm,),
        in_specs=[_row_spec(tm, D, 0), _row_spec(tm, D, 1), pl.BlockSpec((HALO, D), halo_map(0)),
                  pl.BlockSpec((HALO, D), halo_map(1)), pl.BlockSpec((HALO, D), lambda i: (0, 0)),
                  _vec_spec(D), _vec_spec(D), _vec_spec(D)],
        out_specs=[_row_spec(tm, D), _row_spec(tm, D)],
        scratch_shapes=[pltpu.VMEM((tm + HALO, D), F32), pltpu.VMEM((SUBLANES - 1, tm + SHIFT_ROWS, D), F32)],
        name=name, compiler_params=_params(("parallel",)),
    )(zu, zu, zu, zu, conv_w, conv_b, ln_g, ln_b)


def _ln_swish_bwd(hv, dsv, g, beta):
    mu = jnp.mean(hv, axis=-1, keepdims=True)
    xc = hv - mu
    rstd = lax.rsqrt(jnp.mean(xc * xc, axis=-1, keepdims=True) + EPS)
    xh = xc * rstd
    ln = xh * g + beta
    sg = _sigmoid(ln)
    dln = dsv * (sg * (1.0 + ln * (1.0 - sg)))
    dxh = dln * g
    dh = rstd * (dxh - jnp.mean(dxh, axis=-1, keepdims=True) - xh * jnp.mean(dxh * xh, axis=-1, keepdims=True))
    return dh, dln, xh


def _conv_bwd(name, zu, hc, ds, conv_w, ln_g, ln_b, tm=256):
    t = zu.shape[0]
    per = tm // HALO
    steps = t // tm

    group = 4

    def body(a_ref, gl_ref, ah_ref, glh_ref, hc_ref, hcn_ref, ds_ref, dsn_ref, w_ref, g_ref, beta_ref,
             dz_ref, dw_ref, dg_ref, dbeta_ref, dbias_ref, ext_ref, sh_ref, dext_ref, dsh_ref, sg_ref, part_ref):
        i = pl.program_id(0)

        @pl.when(i == 0)
        def _():
            part_ref[...] = jnp.zeros_like(part_ref)
            dg_ref[...] = jnp.zeros_like(dg_ref)
            dbeta_ref[...] = jnp.zeros_like(dbeta_ref)
            dbias_ref[...] = jnp.zeros_like(dbias_ref)

        sg_ref[...] = _sigmoid(gl_ref[...])
        ext_ref[0:HALO, :] = jnp.where(i > 0, ah_ref[...] * _sigmoid(glh_ref[...]), 0.0)
        ext_ref[HALO:, :] = a_ref[...] * sg_ref[...]
        dh, dln, xh = _ln_swish_bwd(hc_ref[...], ds_ref[...], g_ref[...], beta_ref[...])
        dext_ref[0:tm, :] = dh
        dg_ref[...] += jnp.sum(dln * xh, axis=0, keepdims=True)
        dbeta_ref[...] += jnp.sum(dln, axis=0, keepdims=True)
        dbias_ref[...] += jnp.sum(dh, axis=0, keepdims=True)
        dh_next, _, _ = _ln_swish_bwd(hcn_ref[...], dsn_ref[...], g_ref[...], beta_ref[...])
        dext_ref[tm:, :] = jnp.where(i < steps - 1, dh_next, 0.0)
        _shifted_copies(ext_ref, sh_ref, tm)
        _shifted_copies(dext_ref, dsh_ref, tm)

        def chunk(r, carry):
            r0 = pl.multiple_of(r * CONV_CHUNK, CONV_CHUNK)
            rows = pl.ds(r0, CONV_CHUNK)
            part = jnp.zeros((CONV_CHUNK, D), F32)
            for kk in range(CONV_K):
                part = part + w_ref[kk:kk + 1, :] * _window(dext_ref, dsh_ref, CONV_K - 1 - kk, r0, CONV_CHUNK)
            sg = sg_ref[rows, :]
            dz_ref[rows, 0:D] = (part * sg).astype(BF16)
            dz_ref[rows, D:] = (part * a_ref[rows, :] * sg * (1.0 - sg)).astype(BF16)
            return carry

        lax.fori_loop(0, tm // CONV_CHUNK, chunk, 0)

        for k0 in range(0, CONV_K, group):
            taps = range(k0, min(k0 + group, CONV_K))

            def tile(r, parts, taps=taps):
                r0 = pl.multiple_of(r * CONV_CHUNK, CONV_CHUNK)
                dv = dext_ref[pl.ds(r0, CONV_CHUNK), :]
                out = []
                for p, kk in zip(parts, taps):
                    prod = dv * _window(ext_ref, sh_ref, HALO - CONV_K + 1 + kk, r0, CONV_CHUNK)
                    for s in range(0, CONV_CHUNK, SUBLANES):
                        p = p + prod[s:s + SUBLANES, :]
                    out.append(p)
                return tuple(out)

            parts = lax.fori_loop(0, tm // CONV_CHUNK, tile, tuple(jnp.zeros((SUBLANES, D), F32) for _ in taps))
            for p, kk in zip(parts, taps):
                part_ref[kk * SUBLANES:(kk + 1) * SUBLANES, :] += p

        @pl.when(i == steps - 1)
        def _():
            for kk in range(HALO):
                dw_ref[kk:kk + 1, :] = jnp.sum(part_ref[kk * SUBLANES:(kk + 1) * SUBLANES, :], axis=0, keepdims=True)

    halo_map = lambda col: (lambda i: (jnp.maximum(i * per - 1, 0), col))
    next_rows = pl.BlockSpec((HALO, D), lambda i: (jnp.minimum((i + 1) * per, t // HALO - 1), 0))
    shifted = pltpu.VMEM((SUBLANES - 1, tm + SHIFT_ROWS, D), F32)
    vec = jax.ShapeDtypeStruct((1, D), F32)
    return pl.pallas_call(
        body, out_shape=[jax.ShapeDtypeStruct((t, 2 * D), BF16), jax.ShapeDtypeStruct((HALO, D), F32), vec, vec, vec],
        grid=(steps,),
        in_specs=[_row_spec(tm, D, 0), _row_spec(tm, D, 1), pl.BlockSpec((HALO, D), halo_map(0)),
                  pl.BlockSpec((HALO, D), halo_map(1)), _row_spec(tm, D), next_rows, _row_spec(tm, D), next_rows,
                  pl.BlockSpec((HALO, D), lambda i: (0, 0)), _vec_spec(D), _vec_spec(D)],
        out_specs=[_row_spec(tm, 2 * D), pl.BlockSpec((HALO, D), lambda i: (0, 0))] + [_vec_spec(D)] * 3,
        scratch_shapes=[pltpu.VMEM((tm + HALO, D), F32), shifted, pltpu.VMEM((tm + HALO, D), F32), shifted,
                        pltpu.VMEM((tm, D), F32), pltpu.VMEM((HALO * SUBLANES, D), F32)],
        name=name, compiler_params=_params(("arbitrary",)),
    )(zu, zu, zu, zu, hc, hc, ds, ds, conv_w, ln_g, ln_b)


NT = (((1,), (1,)), ((), ()))


def _resident(w, at):
    block = (None,) * (1 + len(at)) + tuple(w.shape[-2:])
    return [pl.BlockSpec(block, functools.partial(lambda i, k: (k, *at, 0, 0), k=k), pipeline_mode=pl.Buffered(1))
            for k in range(SHARDS)]


def _ffn_fwd(tag, x, norm, w, at, gate, tm=512):
    t = x.shape[0]
    slab = min(tm, MM_SLAB)

    def body(*refs):
        x_ref, gain_ref, scale_ref, shift_ref, gate_ref = refs[:5]
        wg, wu, wd = refs[5:5 + SHARDS], refs[5 + SHARDS:5 + 2 * SHARDS], refs[5 + 2 * SHARDS:5 + 3 * SHARDS]
        h_ref, g_ref, u_ref, xn_ref, f_ref = refs[5 + 3 * SHARDS:]
        half_gate = 0.5 * gate_ref[...]
        for r0 in range(0, tm, slab):
            rows = slice(r0, r0 + slab)
            xs = x_ref[rows, :]
            r = lax.rsqrt(jnp.mean(xs * xs, axis=-1, keepdims=True) + EPS)
            hs = ((xs * r) * gain_ref[...] * (1.0 + scale_ref[...]) + shift_ref[...]).astype(BF16)
            h_ref[rows, :] = hs
            tot = None
            for k in range(SHARDS):
                gk = jnp.dot(hs, wg[k][...], preferred_element_type=F32)
                uk = jnp.dot(hs, wu[k][...], preferred_element_type=F32)
                g_ref[k, rows, :] = gk.astype(BF16)
                u_ref[k, rows, :] = uk.astype(BF16)
                ak = ((gk * _sigmoid(gk)) * uk).astype(BF16)
                part = jnp.dot(ak, wd[k][...], preferred_element_type=F32)
                tot = part if tot is None else tot + part
            xn_ref[rows, :] = xs + half_gate * tot
            f_ref[rows, :] = tot.astype(BF16)

    hidden = jax.ShapeDtypeStruct((SHARDS, t, FSH), BF16)
    hidden_spec = pl.BlockSpec((SHARDS, tm, FSH), lambda i: (0, i, 0))
    half = jax.ShapeDtypeStruct((t, D), BF16)
    h, gv, uv, x_new, f = pl.pallas_call(
        body, out_shape=[half, hidden, hidden, jax.ShapeDtypeStruct((t, D), F32), half], grid=(t // tm,),
        in_specs=[_row_spec(tm, D)] + [_vec_spec(D)] * 4 + _resident(w["ffn_wg"], at)
        + _resident(w["ffn_wu"], at) + _resident(w["ffn_wd"], at),
        out_specs=[_row_spec(tm, D), hidden_spec, hidden_spec, _row_spec(tm, D), _row_spec(tm, D)],
        name=f"ffn_fwd_{tag}", compiler_params=_params(("parallel",)),
    )(x, *norm, gate, *([w["ffn_wg"]] * SHARDS), *([w["ffn_wu"]] * SHARDS), *([w["ffn_wd"]] * SHARDS))
    return x_new, h, (gv, uv, f)


def _ffn_hidden_bwd(tag, dx, f, gate, gv, uv, x, gain, scale, w, at, tm=256):
    t = dx.shape[0]
    steps = t // tm
    slab = min(tm, MM_SLAB)

    def body(*refs):
        dx_ref, f_ref, gate_ref, g_ref, u_ref, x_ref, gain_ref, scale_ref = refs[:8]
        wg, wu, wd = refs[8:8 + SHARDS], refs[8 + SHARDS:8 + 2 * SHARDS], refs[8 + 2 * SHARDS:8 + 3 * SHARDS]
        df_ref, dg_ref, du_ref, dxin_ref, dgate_ref, dgain_ref, dscale_ref, dshift_ref = refs[8 + 3 * SHARDS:]
        i = pl.program_id(0)

        @pl.when(i == 0)
        def _():
            dgate_ref[...] = jnp.zeros_like(dgate_ref)
            dgain_ref[...] = jnp.zeros_like(dgain_ref)
            dshift_ref[...] = jnp.zeros_like(dshift_ref)

        half_gate = 0.5 * gate_ref[...]
        norm_w = gain_ref[...] * (1.0 + scale_ref[...])
        for r0 in range(0, tm, slab):
            rows = slice(r0, r0 + slab)
            dxs = dx_ref[rows, :]
            dfs = (half_gate * dxs).astype(BF16)
            df_ref[rows, :] = dfs
            dgate_ref[...] += jnp.sum((0.5 * f_ref[rows, :].astype(F32)) * dxs, axis=0, keepdims=True)
            tot = None
            for k in range(SHARDS):
                da = lax.dot_general(dfs, wd[k][...], NT, preferred_element_type=F32)
                gk, uk = g_ref[k, rows, :].astype(F32), u_ref[k, rows, :].astype(F32)
                sg = _sigmoid(gk)
                dgk = (da * uk * (sg * (1.0 + gk * (1.0 - sg)))).astype(BF16)
                duk = (da * (gk * sg)).astype(BF16)
                dg_ref[k, rows, :] = dgk
                du_ref[k, rows, :] = duk
                part = (lax.dot_general(dgk, wg[k][...], NT, preferred_element_type=F32)
                        + lax.dot_general(duk, wu[k][...], NT, preferred_element_type=F32))
                tot = part if tot is None else tot + part
            xs = x_ref[rows, :]
            r = lax.rsqrt(jnp.mean(xs * xs, axis=-1, keepdims=True) + EPS)
            xh = xs * r
            dxh = tot * norm_w
            dxin_ref[rows, :] = dxs + r * (dxh - xh * jnp.mean(dxh * xh, axis=-1, keepdims=True))
            dgain_ref[...] += jnp.sum(tot * xh, axis=0, keepdims=True)
            dshift_ref[...] += jnp.sum(tot, axis=0, keepdims=True)

        @pl.when(i == steps - 1)
        def _():
            acc = dgain_ref[...]
            dgain_ref[...] = acc * (1.0 + scale_ref[...])
            dscale_ref[...] = acc * gain_ref[...]

    hidden = jax.ShapeDtypeStruct((SHARDS, t, FSH), BF16)
    hidden_spec = pl.BlockSpec((SHARDS, tm, FSH), lambda i: (0, i, 0))
    vec = jax.ShapeDtypeStruct((1, D), F32)
    return pl.pallas_call(
        body, out_shape=[jax.ShapeDtypeStruct((t, D), BF16), hidden, hidden, jax.ShapeDtypeStruct((t, D), F32),
                         vec, vec, vec, vec],
        grid=(steps,),
        in_specs=[_row_spec(tm, D), _row_spec(tm, D), _vec_spec(D), hidden_spec, hidden_spec, _row_spec(tm, D),
                  _vec_spec(D), _vec_spec(D)]
        + _resident(w["ffn_wg"], at) + _resident(w["ffn_wu"], at) + _resident(w["ffn_wd"], at),
        out_specs=[_row_spec(tm, D), hidden_spec, hidden_spec, _row_spec(tm, D)] + [_vec_spec(D)] * 4,
        name=f"ffn_hidden_bwd_{tag}", compiler_params=_params(("arbitrary",)),
    )(dx, f, gate, gv, uv, x, gain, scale, *([w["ffn_wg"]] * SHARDS), *([w["ffn_wu"]] * SHARDS),
      *([w["ffn_wd"]] * SHARDS))


def _ffn_bwd(tag, dx, x, h, saved, w, at, g, scale, gate, into):
    gv, uv, f = saved
    df, dg, du, dx_in, dgate, dgn, dscale, dshift = _ffn_hidden_bwd(tag, dx, f, gate, gv, uv, x, g, scale, w, at)

    def act(blocks):
        gf, uf = blocks[0].astype(F32), blocks[1].astype(F32)
        return ((gf * _sigmoid(gf)) * uf).astype(BF16)

    dwd = _mm_tn(f"ffn_dwd_{tag}", [gv, uv], df, tk=FSH, tn=1024, tt=4096, a_fn=act, a_tiled=True,
                 into=(into["ffn_wd"], at))
    dwg = _mm_tn(f"ffn_dwg_{tag}", h, dg, tk=1024, tn=FSH, tt=4096, b_tiled=True, into=(into["ffn_wg"], at))
    dwu = _mm_tn(f"ffn_dwu_{tag}", h, du, tk=1024, tn=FSH, tt=4096, b_tiled=True, into=(into["ffn_wu"], at))
    return dx_in, dict(ffn_wg=dwg, ffn_wu=dwu, ffn_wd=dwd), dgn, (dshift, dscale, dgate)


def _mix_in_fwd(name, x, g, scale, shift, w_parts, cos_t, sin_t, tm=512):
    t = x.shape[0]
    ng = len(DILATIONS)
    n = 3 * ng
    halves = GW // VL
    strips = QKV // VL

    def body(x_ref, g_ref, sc_ref, sh_ref, cos_ref, sin_ref, wq_ref, wu_ref, wg_ref, h_ref, *rest):
        o_refs, zu_ref, zg_ref, strip_ref = rest[:n], rest[n], rest[n + 1], rest[n + 2]
        for r0 in range(0, tm, MM_SLAB):
            rows = slice(r0, r0 + MM_SLAB)
            xv = x_ref[rows, :]
            r = lax.rsqrt(jnp.mean(xv * xv, axis=-1, keepdims=True) + EPS)
            hv = ((xv * r) * g_ref[...] * (1.0 + sc_ref[...]) + sh_ref[...]).astype(BF16)
            h_ref[rows, :] = hv
            zu_ref[rows, :] = jnp.dot(hv, wu_ref[...], preferred_element_type=F32)
            zg_ref[rows, :] = jnp.dot(hv, wg_ref[...], preferred_element_type=F32).astype(BF16)
            zq = jnp.dot(hv, wq_ref[...], preferred_element_type=F32)
            for j in range(strips):
                strip_ref[j, rows, :] = zq[:, j * VL:(j + 1) * VL]
        for idx in range(n):
            d = DILATIONS[idx % ng]
            for r in range(d):
                rows = _every(d, r, tm)
                for hh in range(halves):
                    piece = strip_ref[halves * idx + hh, rows, :]
                    if idx < 2 * ng:
                        piece = _rotate(piece, cos_ref[rows, :], sin_ref[rows, :])
                    if idx < ng:
                        piece = piece * (HEAD ** -0.5)
                    o_refs[idx][:, r * GW + hh * VL:r * GW + (hh + 1) * VL] = piece.astype(BF16)

    dils = [DILATIONS[idx % ng] for idx in range(n)]
    return pl.pallas_call(
        body, out_shape=[jax.ShapeDtypeStruct((t, D), BF16)] + [_dilated_shape(t, d, BF16) for d in dils]
        + [jax.ShapeDtypeStruct((t, 2 * D), F32), jax.ShapeDtypeStruct((t, 2 * D), BF16)], grid=(t // tm,),
        in_specs=[_row_spec(tm, D), _vec_spec(D), _vec_spec(D), _vec_spec(D), _row_spec(tm, VL), _row_spec(tm, VL)]
        + [_whole(wp) for wp in w_parts],
        out_specs=[_row_spec(tm, D)] + [_dilated_spec(tm, d) for d in dils] + [_row_spec(tm, 2 * D)] * 2,
        scratch_shapes=[pltpu.VMEM((strips, tm, VL), F32)], name=name, compiler_params=_params(("parallel",)),
    )(x, g, scale, shift, cos_t, sin_t, *w_parts)


def _mix_fwd(tag, x, norm, w_in, attn_wo, conv_w, conv_b, ln_g, ln_b, conv_wo, w_out, gate, cos_t, sin_t):
    w_qkv, w_u, w_g = w_in[:, :QKV], w_in[:, QKV:QKV + 2 * D], w_in[:, QKV + 2 * D:]
    h, *qkv, zu, zg = _mix_in_fwd(f"mix_in_{tag}", x, *norm, [w_qkv, w_u, w_g], cos_t, sin_t)
    n = len(DILATIONS)
    outs, lses = [], []
    for grp in range(n):
        o, lse = _attn_fwd(f"attn_fwd_{tag}_{grp}", qkv[grp], qkv[n + grp], qkv[2 * n + grp], grp)
        outs.append(o)
        lses.append(lse)
    hc, s = _conv_fwd(f"conv_fwd_{tag}", zu, conv_w, conv_b, ln_g, ln_b)
    ob, of, lj, y, ya, yc, sa, sc, x_new, f = _mix_out_fwd(f"mix_out_{tag}", outs, lses, s, zg, x, gate, attn_wo,
                                                          conv_wo, w_out)
    return x_new, h, (zu, sa, sc, qkv, ob, of, lj, hc, s, y, ya, yc, f, (w_qkv, w_u, w_g))


def _whole(w):
    return pl.BlockSpec(w.shape, lambda i: (0, 0), pipeline_mode=pl.Buffered(1))


def _merge_groups(in_refs, rows_ref, ob_ref, of_ref, lj_ref, tm):
    n = len(in_refs) // 2
    for hh in range(GW // VL):
        for idx in range(2 * n):
            d = DILATIONS[idx % n]
            for r in range(d):
                cols = slice(r * GW + hh * VL, r * GW + (hh + 1) * VL)
                rows_ref[idx, _every(d, r, tm), :] = in_refs[idx][:, cols]
        ls = [rows_ref[n + g] for g in range(n)]
        m = ls[0]
        for v in ls[1:]:
            m = jnp.maximum(m, v)
        es = [jnp.exp(v - m) for v in ls]
        tot = es[0]
        for v in es[1:]:
            tot = tot + v
        acc = (es[0] / tot) * rows_ref[0]
        for g in range(1, n):
            acc = acc + (es[g] / tot) * rows_ref[g]
        half = slice(hh * VL, (hh + 1) * VL)
        ob_ref[:, half] = acc.astype(BF16)
        of_ref[:, half] = acc
        lj_ref[:, half] = m + jnp.log(tot)


def _mix_out_fwd(name, outs, lses, s, zg, x, gate, attn_wo, conv_wo, w_out, tm=512):
    t = x.shape[0]
    n = len(outs)

    def body(*refs):
        group_refs = refs[:2 * n]
        s_ref, za_ref, zc_ref, x_ref, gate_ref, wa_ref, wc_ref, wo_ref = refs[2 * n:2 * n + 8]
        ob_ref, of_ref, lj_ref, y_ref, ya_ref, yc_ref, sa_ref, sc_ref, xn_ref, f_ref, rows_ref = refs[2 * n + 8:]
        _merge_groups(group_refs, rows_ref, ob_ref, of_ref, lj_ref, tm)
        for r0 in range(0, tm, MM_SLAB):
            rows = slice(r0, r0 + MM_SLAB)
            ya = jnp.dot(ob_ref[rows, :], wa_ref[...], preferred_element_type=F32)
            yc = jnp.dot(s_ref[rows, :], wc_ref[...], preferred_element_type=F32)
            sa, sc = _sigmoid(za_ref[rows, :].astype(F32)), _sigmoid(zc_ref[rows, :].astype(F32))
            y = (sa * ya + sc * yc).astype(BF16)
            out = jnp.dot(y, wo_ref[...], preferred_element_type=F32)
            y_ref[rows, :], ya_ref[rows, :], yc_ref[rows, :] = y, ya.astype(BF16), yc.astype(BF16)
            sa_ref[rows, :], sc_ref[rows, :] = sa.astype(BF16), sc.astype(BF16)
            xn_ref[rows, :] = x_ref[rows, :] + gate_ref[...] * out
            f_ref[rows, :] = out.astype(BF16)

    half = jax.ShapeDtypeStruct((t, D), BF16)
    group = lambda dt: jax.ShapeDtypeStruct((t, GW), dt)
    return pl.pallas_call(
        body, out_shape=[group(BF16), group(F32), group(F32)] + [half] * 5 + [jax.ShapeDtypeStruct((t, D), F32), half],
        grid=(t // tm,),
        in_specs=[_dilated_spec(tm, DILATIONS[idx % n]) for idx in range(2 * n)]
        + [_row_spec(tm, D), _row_spec(tm, D, 0), _row_spec(tm, D, 1), _row_spec(tm, D), _vec_spec(D),
           _whole(attn_wo), _whole(conv_wo), _whole(w_out)],
        out_specs=[_row_spec(tm, GW)] * 3 + [_row_spec(tm, D)] * 7,
        scratch_shapes=[pltpu.VMEM((2 * n, tm, VL), F32)], name=name, compiler_params=_params(("parallel",)),
    )(*outs, *lses, s, zg, zg, x, gate, attn_wo, conv_wo, w_out)


def _mix_out_bwd(name, dx, f, gate, sa, sc, ya, yc, o, lj, attn_wo, conv_wo, w_out, tm=512):
    t = dx.shape[0]
    n = len(DILATIONS)
    halves = GW // VL

    def body(*refs):
        dx_ref, f_ref, gate_ref, sa_ref, sc_ref, ya_ref, yc_ref = refs[:7]
        o_refs, lj_refs = refs[7:7 + halves], refs[7 + halves:7 + 2 * halves]
        wa_ref, wc_ref, wo_ref = refs[7 + 2 * halves:10 + 2 * halves]
        df_ref, dya_ref, dyc_ref, dzg_ref, ds_ref, dgate_ref = refs[10 + 2 * halves:16 + 2 * halves]
        prep_refs = refs[16 + 2 * halves:16 + 2 * halves + 3 * n]
        do_ref, dsum_ref = refs[16 + 2 * halves + 3 * n:]

        @pl.when(pl.program_id(0) == 0)
        def _():
            dgate_ref[...] = jnp.zeros_like(dgate_ref)

        for r0 in range(0, tm, MM_SLAB):
            rows = slice(r0, r0 + MM_SLAB)
            dxs = dx_ref[rows, :]
            dfs = (gate_ref[...] * dxs).astype(BF16)
            df_ref[rows, :] = dfs
            dgate_ref[...] += jnp.sum(f_ref[rows, :].astype(F32) * dxs, axis=0, keepdims=True)
            dy = lax.dot_general(dfs, wo_ref[...], NT, preferred_element_type=F32)
            ga, gc = sa_ref[rows, :].astype(F32), sc_ref[rows, :].astype(F32)
            dya, dyc = (dy * ga).astype(BF16), (dy * gc).astype(BF16)
            dya_ref[rows, :], dyc_ref[rows, :] = dya, dyc
            dzg_ref[rows, 0:D] = (dy * ya_ref[rows, :].astype(F32) * (ga * (1.0 - ga))).astype(BF16)
            dzg_ref[rows, D:] = (dy * yc_ref[rows, :].astype(F32) * (gc * (1.0 - gc))).astype(BF16)
            do = lax.dot_general(dya, wa_ref[...], NT, preferred_element_type=F32)
            for hh in range(halves):
                do_ref[hh, rows, :] = do[:, hh * VL:(hh + 1) * VL]
            ds_ref[rows, :] = lax.dot_general(dyc, wc_ref[...], NT, preferred_element_type=F32)

        for hh in range(halves):
            prod = do_ref[hh] * o_refs[hh][...]
            parts = [jnp.broadcast_to(jnp.sum(prod[:, _head_cols(h)], axis=-1, keepdims=True), (tm, HEAD))
                     for h in range(VL // HEAD)]
            dsum_ref[...] = jnp.concatenate(parts, axis=1)
            for g, d in enumerate(DILATIONS):
                for r in range(d):
                    rows, cols = _every(d, r, tm), slice(r * GW + hh * VL, r * GW + (hh + 1) * VL)
                    prep_refs[g][:, cols] = dsum_ref[rows, :]
                    prep_refs[n + g][:, cols] = do_ref[hh, rows, :].astype(BF16)
                    prep_refs[2 * n + g][:, cols] = lj_refs[hh][rows, :]

    half = jax.ShapeDtypeStruct((t, D), BF16)
    prep_shapes = [_dilated_shape(t, d, dt) for dt in (F32, BF16, F32) for d in DILATIONS]
    half_specs = [_row_spec(tm, VL, hh) for hh in range(halves)]
    outs = pl.pallas_call(
        body, out_shape=[half, half, half, jax.ShapeDtypeStruct((t, 2 * D), BF16), jax.ShapeDtypeStruct((t, D), F32),
                         jax.ShapeDtypeStruct((1, D), F32)] + prep_shapes, grid=(t // tm,),
        in_specs=[_row_spec(tm, D), _row_spec(tm, D), _vec_spec(D)] + [_row_spec(tm, D)] * 4 + half_specs * 2
        + [_whole(attn_wo), _whole(conv_wo), _whole(w_out)],
        out_specs=[_row_spec(tm, D)] * 3 + [_row_spec(tm, 2 * D), _row_spec(tm, D), _vec_spec(D)]
        + [_dilated_spec(tm, d) for d in DILATIONS] * 3,
        scratch_shapes=[pltpu.VMEM((halves, tm, VL), F32), pltpu.VMEM((tm, VL), F32)],
        name=name, compiler_params=_params(("arbitrary",)),
    )(dx, f, gate, sa, sc, ya, yc, *([o] * halves), *([lj] * halves), attn_wo, conv_wo, w_out)
    return outs[:6], outs[6:]


def _mix_bwd(tag, dx, x, h, saved, attn_wo, conv_w, ln_g, ln_b, conv_wo, w_out, g, scale, gate, cos_t, sin_t):
    zu, sa, sc, qkv, ob, of, lj, hc, s, y, ya, yc, f, w_parts = saved
    n = len(DILATIONS)
    (df, dya, dyc, dzg, ds, dgate), prep = _mix_out_bwd(f"mix_out_bwd_{tag}", dx, f, gate, sa, sc, ya, yc, of, lj,
                                                         attn_wo, conv_wo, w_out)
    dw_out = _mm_tn(f"mix_dwout_{tag}", y, df, tk=1024, tn=1024, tt=4096)
    dw_attn = _mm_tn(f"mix_dwattn_{tag}", ob, dya, tk=GW, tn=1024, tt=4096)
    dw_conv_o = _mm_tn(f"mix_dwconvo_{tag}", s, dyc, tk=1024, tn=1024, tt=4096)

    dqs, dks, dvs = [], [], []
    for grp in range(n):
        dq, dk, dv = _attn_bwd(f"attn_bwd_{tag}_{grp}", qkv[grp], qkv[n + grp], qkv[2 * n + grp], prep[n + grp],
                               prep[2 * n + grp], prep[grp], grp)
        dqs.append(dq)
        dks.append(dk)
        dvs.append(dv)
    dzqkv = _rope_bwd(f"rope_bwd_{tag}", dqs + dks + dvs, cos_t, sin_t)

    dzu, dconv_w, dln_g, dln_b, dconv_b = _conv_bwd(f"conv_bwd_{tag}", zu, hc, ds, conv_w, ln_g, ln_b)

    dz_parts = [dzqkv, dzu, dzg]
    dw_in = jnp.concatenate(
        [_mm_tn(f"mix_dwin_{tag}_{i}", h, dzp, tk=1024, tn=dzp.shape[1] // 2, tt=2048)
         for i, dzp in enumerate(dz_parts)], axis=1)
    dx_in, dgn, dscale, dshift = _proj_norm_bwd(f"mix_dh_{tag}", dz_parts, list(w_parts), x, dx, g, scale)
    grads = dict(w_in=dw_in, attn_wo=dw_attn, conv_w=dconv_w[:CONV_K], conv_b=dconv_b, conv_ln_g=dln_g,
                 conv_ln_b=dln_b, conv_wo=dw_conv_o, w_out=dw_out)
    return dx_in, grads, dgn, (dshift, dscale, dgate)


def _local_step(x, mod, target, w, wf):
    t = x.shape[0]
    cos_t, sin_t = _rope_tables(t)
    row = lambda v: v.reshape(1, -1)
    conv_w_pad = jnp.concatenate([wf["conv_w"], jnp.zeros((DEPTH, HALO - CONV_K, D), F32)], axis=1)

    saved = []
    for l in range(DEPTH):
        mods = [mod[l:l + 1, i * D:(i + 1) * D] for i in range(N_MOD)]
        gains = [row(wf["norm_g"][l, i]) for i in range(3)]
        lay = dict(mods=mods, gains=gains)

        lay["x0"] = x
        x, lay["h0"], lay["ffn0"] = _ffn_fwd(f"a_{l}", x, (gains[0], mods[1], mods[0]), w, (l, 0), mods[2])
        lay["x1"] = x
        x, lay["h1"], lay["mix"] = _mix_fwd(f"{l}", x, (gains[1], mods[4], mods[3]), w["w_in"][l], w["attn_wo"][l],
                                            conv_w_pad[l],
                                 row(wf["conv_b"][l]), row(wf["conv_ln_g"][l]), row(wf["conv_ln_b"][l]),
                                 w["conv_wo"][l], w["w_out"][l], mods[5], cos_t, sin_t)
        lay["x2"] = x
        x, lay["h2"], lay["ffn1"] = _ffn_fwd(f"b_{l}", x, (gains[2], mods[7], mods[6]), w, (l, 1), mods[8])
        saved.append(lay)

    dx, dfinal_g, loss_cols = _loss_bwd("loss_head", x, target, row(wf["final_g"]))

    ffn_grads = {n: jnp.zeros(w[n].shape, F32) for n in ("ffn_wg", "ffn_wu", "ffn_wd")}
    per_layer = []
    for l in reversed(range(DEPTH)):
        lay = saved[l]
        mods, gains = lay["mods"], lay["gains"]
        dx, ffn_grads, dgn2, dmod2 = _ffn_bwd(f"b_{l}", dx, lay["x2"], lay["h2"], lay["ffn1"], w, (l, 1),
                                              gains[2], mods[7], mods[8], ffn_grads)
        dx, gm, dgn1, dmod1 = _mix_bwd(f"{l}", dx, lay["x1"], lay["h1"], lay["mix"], w["attn_wo"][l],
                                       conv_w_pad[l], row(wf["conv_ln_g"][l]), row(wf["conv_ln_b"][l]),
                                       w["conv_wo"][l], w["w_out"][l], gains[1], mods[4], mods[5], cos_t, sin_t)
        dx, ffn_grads, dgn0, dmod0 = _ffn_bwd(f"a_{l}", dx, lay["x0"], lay["h0"], lay["ffn0"], w, (l, 0),
                                              gains[0], mods[1], mods[2], ffn_grads)
        g = dict(gm)
        g["dmod"] = jnp.concatenate(list(dmod0) + list(dmod1) + list(dmod2), axis=1)
        g["norm_g"] = [dgn0[0], dgn1[0], dgn2[0]]
        for name in ("conv_b", "conv_ln_g", "conv_ln_b"):
            g[name] = g[name][0]
        per_layer.append(g)
    per_layer.reverse()
    grads = {name: [per_layer[l][name] for l in range(DEPTH)] for name in per_layer[0]}
    grads["dmod"] = jnp.concatenate(grads["dmod"], axis=0)
    grads.update(ffn_grads)
    grads["final_g"] = dfinal_g[0]
    return loss_cols, dx, grads


def _split_bits(w):
    bits = lax.bitcast_convert_type(w, jnp.uint32)
    hi = lax.bitcast_convert_type((bits >> 16).astype(jnp.uint16), BF16)
    lo = lax.bitcast_convert_type((bits & 0xFFFF).astype(jnp.uint16), BF16)
    return hi, lo


def _join_bits(hi, lo):
    h = lax.bitcast_convert_type(hi, jnp.uint16).astype(jnp.uint32)
    l = lax.bitcast_convert_type(lo, jnp.uint16).astype(jnp.uint32)
    return lax.bitcast_convert_type((h << 16) | l, F32)


def _pack(parts, rows):
    out = []
    for p in parts:
        flat = p.reshape(-1)
        pad = -flat.shape[0] % LANES
        out.append(jnp.concatenate([flat, jnp.zeros((pad,), flat.dtype)]) if pad else flat)
    flat = jnp.concatenate(out)
    return jnp.concatenate([flat, jnp.zeros((rows * LANES - flat.shape[0],), flat.dtype)]).reshape(rows, LANES)


def _unpack(buf, shapes):
    out, row = [], 0
    for shape in shapes:
        size = 1
        for s in shape:
            size *= s
        rows = -(-size // LANES)
        out.append(buf[row:row + rows].reshape(-1)[:size].reshape(shape))
        row += rows
    return out


def _place():
    x, y, c = lax.axis_index("x"), lax.axis_index("y"), lax.axis_index("c")
    chips = [(1 - x, y), (x, 1 - y), (1 - x, 1 - y)]
    return x, y, c, chips


def _chip_index():
    return (2 * lax.axis_index("x") + lax.axis_index("y")).astype(jnp.int32)


HBM_SPEC = pl.BlockSpec(memory_space=pltpu.HBM)


def _gather_rows(name, block):
    m, n = block.shape

    def body(x_ref, out_ref, send_sems, recv_sems, local_sem):
        x, y, c, chips = _place()
        me, sibling = (x, y, c), (x, y, 1 - c)

        def rows(px, py, pc):
            return out_ref.at[pl.ds((4 * px + 2 * py + pc) * m, m), :]

        def copy(k, owner, to, src=None):
            return pltpu.make_async_remote_copy(
                src_ref=rows(*owner) if src is None else src, dst_ref=rows(*owner), send_sem=send_sems.at[k],
                recv_sem=recv_sems.at[k], device_id=to, device_id_type=MESH)

        mine = pltpu.make_async_copy(x_ref, rows(*me), local_sem)
        mine.start()
        first = [copy(0, me, sibling, src=x_ref)] + [copy(1 + j, me, (*chip, c), src=x_ref)
                                                     for j, chip in enumerate(chips)]
        for cp in first:
            cp.start()
        passed = [copy(4 + j, (*chip, c), sibling) for j, chip in enumerate(chips)]
        for j, chip in enumerate(chips):
            copy(1 + j, (*chip, c), me).wait_recv()
            passed[j].start()
        copy(0, sibling, me).wait_recv()
        for j, chip in enumerate(chips):
            copy(4 + j, (*chip, 1 - c), me).wait_recv()
        for cp in first + passed:
            cp.wait_send()
        mine.wait()

    whole = pl.BlockSpec(memory_space=pltpu.VMEM)
    return pl.pallas_call(
        body, out_shape=jax.ShapeDtypeStruct((N_DEV * m, n), block.dtype), in_specs=[whole], out_specs=whole,
        scratch_shapes=[pltpu.SemaphoreType.DMA((7,)), pltpu.SemaphoreType.DMA((7,)), pltpu.SemaphoreType.DMA],
        name=name,
    )(block)


def _gather_weights(arrays):
    n = len(arrays)

    def body(*refs):
        outs, send_sems, recv_sems = refs[n:2 * n], refs[2 * n], refs[2 * n + 1]
        x, y, c, chips = _place()
        me = 2 * x + y
        sibling = (x, y, 1 - c)
        there = [2 * chip[0] + chip[1] for chip in chips]

        def copy(a, k, chip, layer, to):
            piece = outs[a].at[chip, layer]
            return pltpu.make_async_remote_copy(
                src_ref=piece, dst_ref=piece, send_sem=send_sems.at[6 * a + k], recv_sem=recv_sems.at[6 * a + k],
                device_id=to, device_id_type=MESH)

        first = [copy(a, j, me, c, (*chip, c)) for a in range(n) for j, chip in enumerate(chips)]
        for cp in first:
            cp.start()
        passed = []
        for a in range(n):
            for j in range(3):
                copy(a, j, there[j], c, sibling).wait_recv()
                passed.append(copy(a, 3 + j, there[j], c, sibling))
                passed[-1].start()
        for a in range(n):
            for j in range(3):
                copy(a, 3 + j, there[j], 1 - c, sibling).wait_recv()
        for cp in first + passed:
            cp.wait_send()

    return pl.pallas_call(
        body, out_shape=[jax.ShapeDtypeStruct(a.shape, a.dtype) for a in arrays],
        in_specs=[HBM_SPEC] * n, out_specs=[HBM_SPEC] * n,
        scratch_shapes=[pltpu.SemaphoreType.DMA((6 * n,)), pltpu.SemaphoreType.DMA((6 * n,))],
        input_output_aliases={i: i for i in range(n)}, name="gather_weights",
    )(*arrays)


def _row_block(rows, cols):
    for cand in (512, 256, 128, 64, 32, 16):
        if rows % cand == 0 and cand * cols * 4 <= 2560 * 1024:
            return cand
    return rows


def _swap_layers(grads):
    n = len(grads)

    def body(*refs):
        g_refs, out_refs, send_sems, recv_sems = refs[:n], refs[n:2 * n], refs[2 * n], refs[2 * n + 1]
        x, y, c, _ = _place()
        copies = [pltpu.make_async_remote_copy(
            src_ref=g_refs[a].at[:, 1 - c], dst_ref=out_refs[a], send_sem=send_sems.at[a], recv_sem=recv_sems.at[a],
            device_id=(x, y, 1 - c), device_id_type=MESH) for a in range(n)]
        for cp in copies:
            cp.start()
        for cp in copies:
            cp.wait()

    return pl.pallas_call(
        body, out_shape=[jax.ShapeDtypeStruct((g.shape[0],) + g.shape[2:], F32) for g in grads],
        in_specs=[HBM_SPEC] * n, out_specs=[HBM_SPEC] * n,
        scratch_shapes=[pltpu.SemaphoreType.DMA((n,)), pltpu.SemaphoreType.DMA((n,))], name="swap_layers",
    )(*grads)


def _add_layers(name, grad, other):
    shards, _, rows, cols = grad.shape
    tr = _row_block(rows, cols)

    def body(c_ref, g_ref, o_ref, out_ref):
        out_ref[...] = (g_ref[...] + o_ref[...]).astype(BF16)

    c = lax.axis_index("c").astype(jnp.int32).reshape(1)
    grid_spec = pltpu.PrefetchScalarGridSpec(
        num_scalar_prefetch=1, grid=(shards, rows // tr),
        in_specs=[pl.BlockSpec((None, None, tr, cols), lambda k, i, c_ref: (k, c_ref[0], i, 0)),
                  pl.BlockSpec((None, tr, cols), lambda k, i, c_ref: (k, i, 0))],
        out_specs=pl.BlockSpec((None, tr, cols), lambda k, i, c_ref: (k, i, 0)))
    return pl.pallas_call(
        body, out_shape=jax.ShapeDtypeStruct((shards, rows, cols), BF16), grid_spec=grid_spec,
        name=name, compiler_params=_params(("parallel", "parallel")),
    )(c, grad, other)


def _scatter_chips(parts):
    n = len(parts)

    def body(*refs):
        p_refs, out_refs, send_sems, recv_sems = refs[:n], refs[n:2 * n], refs[2 * n], refs[2 * n + 1]
        x, y, c, chips = _place()
        me = 2 * x + y
        there = [2 * chip[0] + chip[1] for chip in chips]

        def copy(a, j, slot):
            return pltpu.make_async_remote_copy(
                src_ref=p_refs[a].at[there[j]], dst_ref=out_refs[a].at[slot], send_sem=send_sems.at[3 * a + j],
                recv_sem=recv_sems.at[3 * a + j], device_id=(*chips[j], c), device_id_type=MESH)

        sends = [copy(a, j, me) for a in range(n) for j in range(3)]
        for cp in sends:
            cp.start()
        for a in range(n):
            for j in range(3):
                copy(a, j, there[j]).wait_recv()
        for cp in sends:
            cp.wait_send()

    return pl.pallas_call(
        body, out_shape=[jax.ShapeDtypeStruct(p.shape, p.dtype) for p in parts],
        in_specs=[HBM_SPEC] * n, out_specs=[HBM_SPEC] * n,
        scratch_shapes=[pltpu.SemaphoreType.DMA((3 * n,)), pltpu.SemaphoreType.DMA((3 * n,))],
        name="scatter_chips",
    )(*parts)


def _add_chips(name, part, others):
    shards, rows, cols = part.shape
    tr = _row_block(rows, cols)

    def body(pos_ref, own_ref, r0_ref, r1_ref, r2_ref, r3_ref, out_ref):
        me = pos_ref[0]
        own = own_ref[...].astype(F32)
        total = None
        for k, r_ref in enumerate((r0_ref, r1_ref, r2_ref, r3_ref)):
            term = jnp.where(me == k, own, r_ref[...].astype(F32))
            total = term if total is None else total + term
        out_ref[...] = total

    def other(k):
        return pl.BlockSpec((None, tr, cols),
                            lambda i, pos, k=k: (jnp.where(pos[0] == k, (k + 1) % shards, k), i, 0))

    pos = jnp.stack([_chip_index(), lax.axis_index("c").astype(jnp.int32)])
    grid_spec = pltpu.PrefetchScalarGridSpec(
        num_scalar_prefetch=1, grid=(rows // tr,),
        in_specs=[pl.BlockSpec((None, tr, cols), lambda i, pos: (pos[0], i, 0))] + [other(k) for k in range(shards)],
        out_specs=pl.BlockSpec((None, tr, cols), lambda i, pos: (pos[1], i, 0)))
    return pl.pallas_call(
        body, out_shape=jax.ShapeDtypeStruct((DEPTH, rows, cols), F32), grid_spec=grid_spec,
        name=name, compiler_params=_params(("parallel",)),
    )(pos, part, others, others, others, others)


def _join_layers(arrays):
    n = len(arrays)

    def body(*refs):
        outs, send_sems, recv_sems = refs[n:2 * n], refs[2 * n], refs[2 * n + 1]
        x, y, c, _ = _place()

        def copy(a, layer):
            piece = outs[a].at[layer]
            return pltpu.make_async_remote_copy(src_ref=piece, dst_ref=piece, send_sem=send_sems.at[a],
                                                recv_sem=recv_sems.at[a], device_id=(x, y, 1 - c),
                                                device_id_type=MESH)

        sends = [copy(a, c) for a in range(n)]
        for cp in sends:
            cp.start()
        for a in range(n):
            copy(a, 1 - c).wait_recv()
        for cp in sends:
            cp.wait_send()

    return pl.pallas_call(
        body, out_shape=[jax.ShapeDtypeStruct(a.shape, a.dtype) for a in arrays],
        in_specs=[HBM_SPEC] * n, out_specs=[HBM_SPEC] * n,
        scratch_shapes=[pltpu.SemaphoreType.DMA((n,)), pltpu.SemaphoreType.DMA((n,))],
        input_output_aliases={i: i for i in range(n)}, name="join_layers",
    )(*arrays)


def _reduce_scatter(grads):
    sums = [_add_layers(f"add_layers_{a}", g, o) for a, (g, o) in enumerate(zip(grads, _swap_layers(grads)))]
    others = _scatter_chips(sums)
    return _join_layers([_add_chips(f"add_chips_{a}", p, o) for a, (p, o) in enumerate(zip(sums, others))])


def _adamw(name, w, g, m, v):
    shape = w.shape
    cols = shape[-1]
    rows = w.size // cols
    tr = rows
    for cand in (512, 256, 128, 64, 32, 16, 8):
        if rows % cand == 0 and cand * cols * 4 <= 2 * 1024 * 1024:
            tr = cand
            break

    def body(w_ref, g_ref, m_ref, v_ref, go_ref, d_ref, nm_ref, nv_ref):
        gv = g_ref[...]
        go_ref[...] = gv
        nm = ADAM_B1 * m_ref[...] + (1.0 - ADAM_B1) * gv
        nv = ADAM_B2 * v_ref[...] + (1.0 - ADAM_B2) * (gv * gv)
        m_hat = nm / (1.0 - ADAM_B1 ** ADAM_STEP)
        v_hat = nv / (1.0 - ADAM_B2 ** ADAM_STEP)
        d_ref[...] = -ADAM_LR * (m_hat / (jnp.sqrt(v_hat) + ADAM_EPS) + ADAM_WD * w_ref[...])
        nm_ref[...] = nm
        nv_ref[...] = nv

    spec = pl.BlockSpec((tr, cols), lambda i: (i, 0))
    two = lambda a: a.reshape(rows, cols)
    outs = pl.pallas_call(
        body, out_shape=[jax.ShapeDtypeStruct((rows, cols), F32)] * 4, grid=(rows // tr,),
        in_specs=[spec] * 4, out_specs=[spec] * 4, name=name, compiler_params=_params(("parallel",)),
    )(two(w), two(g), two(m), two(v))
    return [o.reshape(shape) for o in outs]


BIG = ("ffn_wg", "ffn_wu", "ffn_wd", "w_in", "conv_wo", "w_out")
MISC_ROWS = 96


def _own_slot(shard):
    return lax.dynamic_update_slice(jnp.zeros((SHARDS,) + shard.shape, shard.dtype), shard[None],
                                    (_chip_index(),) + (0,) * shard.ndim)


def _as_matrices(a):
    return a.reshape(a.shape[0], a.shape[1], -1, a.shape[-1])


def kernel(x, c, ada_w, ada_b, norm_g, ffn_wg, ffn_wu, ffn_wd, w_in, attn_wo, conv_w, conv_b, conv_ln_g, conv_ln_b, conv_wo, w_out, final_g, loss_target, m_ada_w, m_ada_b, m_norm_g, m_ffn_wg, m_ffn_wu, m_ffn_wd, m_w_in, m_attn_wo, m_conv_w, m_conv_b, m_conv_ln_g, m_conv_ln_b, m_conv_wo, m_w_out, m_final_g, v_ada_w, v_ada_b, v_norm_g, v_ffn_wg, v_ffn_wu, v_ffn_wd, v_w_in, v_attn_wo, v_conv_w, v_conv_b, v_conv_ln_g, v_conv_ln_b, v_conv_wo, v_w_out, v_final_g):
    weights = dict(ada_w=ada_w, ada_b=ada_b, norm_g=norm_g, ffn_wg=ffn_wg, ffn_wu=ffn_wu, ffn_wd=ffn_wd, w_in=w_in,
                   attn_wo=attn_wo, conv_w=conv_w, conv_b=conv_b, conv_ln_g=conv_ln_g, conv_ln_b=conv_ln_b,
                   conv_wo=conv_wo, w_out=w_out, final_g=final_g)
    moments_m = dict(ada_w=m_ada_w, ada_b=m_ada_b, norm_g=m_norm_g, ffn_wg=m_ffn_wg, ffn_wu=m_ffn_wu,
                     ffn_wd=m_ffn_wd, w_in=m_w_in, attn_wo=m_attn_wo, conv_w=m_conv_w, conv_b=m_conv_b,
                     conv_ln_g=m_conv_ln_g, conv_ln_b=m_conv_ln_b, conv_wo=m_conv_wo, w_out=m_w_out,
                     final_g=m_final_g)
    moments_v = dict(ada_w=v_ada_w, ada_b=v_ada_b, norm_g=v_norm_g, ffn_wg=v_ffn_wg, ffn_wu=v_ffn_wu,
                     ffn_wd=v_ffn_wd, w_in=v_w_in, attn_wo=v_attn_wo, conv_w=v_conv_w, conv_b=v_conv_b,
                     conv_ln_g=v_conv_ln_g, conv_ln_b=v_conv_ln_b, conv_wo=v_conv_wo, w_out=v_w_out,
                     final_g=v_final_g)
    layers, shards = range(DEPTH), range(SHARDS)

    bits = {n: _split_bits(weights[n]) for n in EXACT}
    misc_w = jnp.stack([_pack([attn_wo[l].astype(BF16), bits["norm_g"][0][l], bits["norm_g"][1][l],
                               bits["conv_w"][0][l], bits["conv_w"][1][l]], MISC_ROWS) for l in layers])
    sent = [_own_slot(weights[n].astype(BF16)) for n in BIG] + [_own_slot(misc_w)]
    got = dict(zip(BIG + ("misc",), _gather_weights(sent)))
    w = {n: got[n] for n in ("ffn_wg", "ffn_wu", "ffn_wd")}
    w["w_in"] = got["w_in"].transpose(1, 2, 0, 3).reshape(DEPTH, D, -1)
    for n in ("conv_wo", "w_out"):
        w[n] = got[n].transpose(1, 0, 2, 3).reshape(DEPTH, D, D)
    misc_shapes = [(GW, GW), (3, GW), (3, GW), (CONV_K, GW), (CONV_K, GW)]
    pieces = [[_unpack(got["misc"][k, l], misc_shapes) for k in shards] for l in layers]
    whole = lambda i: jnp.stack([jnp.concatenate([pieces[l][k][i] for k in shards], axis=1) for l in layers])
    w["attn_wo"] = whole(0)
    vectors = dict(ada_b=ada_b, conv_b=conv_b, conv_ln_g=conv_ln_g, conv_ln_b=conv_ln_b, final_g=final_g,
                   norm_g=_join_bits(whole(1), whole(2)), conv_w=_join_bits(whole(3), whole(4)))

    me = 2 * _chip_index() + lax.axis_index("c").astype(jnp.int32)
    pad_rows = lambda a, rows: jnp.concatenate([a, jnp.zeros((rows - a.shape[0], a.shape[1]), a.dtype)])
    c_all = _gather_rows("gather_c", pad_rows(c, SUBLANES)).reshape(N_DEV, SUBLANES, D)[:, 0]
    mod_cols = _mod_fwd("mod_fwd", c_all, ada_w.astype(BF16))
    by_dev = _gather_rows("gather_mod", mod_cols).reshape(N_DEV, N_DEV, DEPTH, -1)
    mine = lax.dynamic_index_in_dim(by_dev[0::2], me, axis=1, keepdims=False)
    mod = mine.transpose(1, 0, 2).reshape(DEPTH, -1) + ada_b

    loss_cols, dx, grads = _local_step(x[0], mod, loss_target[0], w, vectors)
    loss = lax.psum(jnp.sum(loss_cols), ("x", "y", "c"))

    dmod_rows = DEPTH * N_MOD * D // LANES
    dmod_all = _gather_rows("gather_dmod", pad_rows(grads["dmod"].reshape(dmod_rows, LANES), 3 * SUBLANES))
    dmod_all = dmod_all.reshape(N_DEV, 3 * SUBLANES, LANES)[:, :dmod_rows].reshape(N_DEV, DEPTH, -1)
    grad_ada_b = _sum_devices("ada_b_grad", dmod_all.reshape(N_DEV, -1)).reshape(DEPTH, -1)
    cols = ada_w.shape[-1]
    dmod_cols = lax.dynamic_slice_in_dim(dmod_all, _chip_index() * cols, cols, axis=2).transpose(1, 0, 2)
    grad_ada_w = _mod_bwd("mod_bwd", c_all.T, dmod_cols)

    cols_of = lambda a, k: a[..., k * GW:(k + 1) * GW]
    misc_g = jnp.stack([jnp.stack([_pack(
        [cols_of(grads["attn_wo"][l], k), cols_of(jnp.stack(grads["norm_g"][l]), k), cols_of(grads["conv_w"][l], k),
         grads["conv_b"][l], grads["conv_ln_g"][l], grads["conv_ln_b"][l],
         grads["final_g"] if l == 0 else jnp.zeros_like(grads["final_g"])], MISC_ROWS)
        for l in layers]) for k in shards])
    by_chip = dict(
        ffn_wg=grads["ffn_wg"], ffn_wu=grads["ffn_wu"], ffn_wd=grads["ffn_wd"],
        w_in=jnp.stack(grads["w_in"]).reshape(DEPTH, D, SHARDS, -1).transpose(2, 0, 1, 3),
        conv_wo=jnp.stack(grads["conv_wo"]).reshape(DEPTH, SHARDS, -1, D).transpose(1, 0, 2, 3),
        w_out=jnp.stack(grads["w_out"]).reshape(DEPTH, SHARDS, -1, D).transpose(1, 0, 2, 3))
    reduced = _reduce_scatter([_as_matrices(by_chip[n]) for n in BIG] + [misc_g])
    summed = {n: r.reshape(weights[n].shape) for n, r in zip(BIG, reduced)}
    small_shapes = [(GW, GW), (3, GW), (CONV_K, GW), (D,), (D,), (D,), (D,)]
    small = [_unpack(reduced[-1][l], small_shapes) for l in layers]
    for i, n in enumerate(("attn_wo", "norm_g", "conv_w", "conv_b", "conv_ln_g", "conv_ln_b")):
        summed[n] = jnp.stack([small[l][i] for l in layers])
    summed["final_g"] = small[0][6]
    summed["ada_w"], summed["ada_b"] = grad_ada_w, grad_ada_b

    grad_out, deltas, new_m, new_v = {}, {}, {}, {}
    for n in WEIGHTS:
        grad_out[n], deltas[n], new_m[n], new_v[n] = _adamw(f"adamw_{n}", weights[n], summed[n], moments_m[n],
                                                           moments_v[n])

    return (loss, dx[None], *[grad_out[n] for n in WEIGHTS], *[deltas[n] for n in WEIGHTS],
            *[new_m[n] for n in WEIGHTS], *[new_v[n] for n in WEIGHTS])
```

```python
import functools

import jax
import jax.numpy as jnp
from jax import lax
from jax.experimental import pallas as pl
from jax.experimental.pallas import tpu as pltpu

F32 = jnp.float32
BF16 = jnp.bfloat16

D = 1024
DFF = 2816
HEAD = 64
GW = 256
DILATIONS = (1, 4, 16)
BAND = 128
QKV = 2304
CONV_K = 31
HALO = 32
N_MOD = 9
EPS = 1e-6
NEG_INF = -1e30
DEPTH = 2

SHARDS = 4
FSH = DFF // SHARDS
LANES = 1024
VL = 128

ADAM_LR = 0.001
ADAM_B1 = 0.9
ADAM_B2 = 0.999
ADAM_EPS = 1e-08
ADAM_WD = 0.01
ADAM_STEP = 10

VMEM_LIMIT = 56 * 1024 * 1024

EXACT = ("norm_g", "conv_w")
WEIGHTS = ("ada_w", "ada_b", "norm_g", "ffn_wg", "ffn_wu", "ffn_wd", "w_in", "attn_wo", "conv_w", "conv_b",
           "conv_ln_g", "conv_ln_b", "conv_wo", "w_out", "final_g")

MESH = pl.DeviceIdType.MESH


def _params(sem=None):
    return pltpu.CompilerParams(dimension_semantics=sem, vmem_limit_bytes=VMEM_LIMIT)


def _sigmoid(v):
    return jax.nn.sigmoid(v)


MM_SLAB = 256


def _mm_tn(name, a, b, *, tk, tn, tt, a_fn=None, a_tiled=False, b_tiled=False, into=None):
    a_list = list(a) if a_fn is not None else [a]
    na = len(a_list)
    t = a_list[0].shape[-2]
    nk = a_list[0].shape[0] if a_tiled else a_list[0].shape[1] // tk
    nn = b.shape[0] if b_tiled else b.shape[1] // tn
    tt = min(tt, t)
    steps = t // tt
    has_into = into is not None

    def body(*refs):
        refs = refs[1:] if has_into else refs
        a_refs, b_ref, o_ref, acc_ref = refs[:na], refs[na], refs[na + 1], refs[na + 2]
        s = pl.program_id(2)

        @pl.when(s == 0)
        def _():
            acc_ref[...] = jnp.zeros_like(acc_ref)

        av = a_refs[0][...] if a_fn is None else a_fn([r[...] for r in a_refs])
        acc_ref[...] += lax.dot_general(av, b_ref[...], (((0,), (0,)), ((), ())), preferred_element_type=F32)

        @pl.when(s == steps - 1)
        def _():
            o_ref[...] = acc_ref[...]

    a_spec = (pl.BlockSpec((None, tt, tk), lambda i, j, s: (i, s, 0)) if a_tiled
              else pl.BlockSpec((tt, tk), lambda i, j, s: (s, i)))
    b_spec = (pl.BlockSpec((None, tt, tn), lambda i, j, s: (j, s, 0)) if b_tiled
              else pl.BlockSpec((tt, tn), lambda i, j, s: (s, j)))
    if a_tiled:
        out_dims, tile_index = (nk, tk, nn * tn), lambda i, j, s: (i, 0, j)
    elif b_tiled:
        out_dims, tile_index = (nn, nk * tk, tn), lambda i, j, s: (j, i, 0)
    else:
        out_dims, tile_index = (nk * tk, nn * tn), lambda i, j, s: (i, j)
    tiled = a_tiled or b_tiled
    if has_into:
        buf, lead = into
        def out_index(i, j, s):
            idx = tile_index(i, j, s)
            return (idx[0], *lead, *idx[1:])
        out_spec = pl.BlockSpec((None,) * (1 + len(lead)) + (tk, tn), out_index)
        out_shape = jax.ShapeDtypeStruct(buf.shape, buf.dtype)
        extra_in, extra_specs, aliases = [buf], [pl.BlockSpec(memory_space=pl.ANY)], {0: 0}
    else:
        out_spec = pl.BlockSpec(((None,) if tiled else ()) + (tk, tn), tile_index)
        out_shape = jax.ShapeDtypeStruct(out_dims, F32)
        extra_in, extra_specs, aliases = [], [], {}
    return pl.pallas_call(
        body, out_shape=out_shape, grid=(nk, nn, steps), in_specs=extra_specs + [a_spec] * na + [b_spec],
        out_specs=out_spec, scratch_shapes=[pltpu.VMEM((tk, tn), F32)], input_output_aliases=aliases, name=name,
        compiler_params=_params(("parallel", "parallel", "arbitrary")),
    )(*extra_in, *a_list, b)


def _row_spec(tm, width, col=0):
    return pl.BlockSpec((tm, width), functools.partial(lambda i, col: (i, col), col=col))


def _vec_spec(width):
    return pl.BlockSpec((1, width), lambda i: (0, 0))


def _proj_norm_bwd(name, dz_parts, w_parts, x, dres, g, scale, tm=256):
    t = x.shape[0]
    steps = t // tm
    n = len(dz_parts)

    def body(*refs):
        dz_refs, w_refs = refs[:n], refs[n:2 * n]
        x_ref, dres_ref, g_ref, sc_ref, dx_ref, dg_ref, dsc_ref, dsh_ref = refs[2 * n:]
        i = pl.program_id(0)

        @pl.when(i == 0)
        def _():
            dg_ref[...] = jnp.zeros_like(dg_ref)
            dsh_ref[...] = jnp.zeros_like(dsh_ref)

        dh = None
        for dz_ref, w_ref in zip(dz_refs, w_refs):
            part = lax.dot_general(dz_ref[...], w_ref[...], (((1,), (1,)), ((), ())), preferred_element_type=F32)
            dh = part if dh is None else dh + part
        xv = x_ref[...]
        r = lax.rsqrt(jnp.mean(xv * xv, axis=-1, keepdims=True) + EPS)
        xh = xv * r
        dxh = dh * (g_ref[...] * (1.0 + sc_ref[...]))
        dx_ref[...] = dres_ref[...] + r * (dxh - xh * jnp.mean(dxh * xh, axis=-1, keepdims=True))
        dg_ref[...] += jnp.sum(dh * xh, axis=0, keepdims=True)
        dsh_ref[...] += jnp.sum(dh, axis=0, keepdims=True)

        @pl.when(i == steps - 1)
        def _():
            acc = dg_ref[...]
            dg_ref[...] = acc * (1.0 + sc_ref[...])
            dsc_ref[...] = acc * g_ref[...]

    vec = jax.ShapeDtypeStruct((1, D), F32)
    resident = [pl.BlockSpec(wp.shape, lambda i: (0, 0), pipeline_mode=pl.Buffered(1)) for wp in w_parts]
    return pl.pallas_call(
        body, out_shape=[jax.ShapeDtypeStruct((t, D), F32), vec, vec, vec], grid=(steps,),
        in_specs=[_row_spec(tm, dz.shape[1]) for dz in dz_parts] + resident
        + [_row_spec(tm, D), _row_spec(tm, D), _vec_spec(D), _vec_spec(D)],
        out_specs=[_row_spec(tm, D), _vec_spec(D), _vec_spec(D), _vec_spec(D)],
        name=name, compiler_params=_params(("arbitrary",)),
    )(*dz_parts, *w_parts, x, dres, g, scale)


def _loss_bwd(name, x, target, g, tm=256):
    t = x.shape[0]

    def body(x_ref, t_ref, g_ref, dx_ref, dg_ref, loss_ref):
        @pl.when(pl.program_id(0) == 0)
        def _():
            dg_ref[...] = jnp.zeros_like(dg_ref)
            loss_ref[...] = jnp.zeros_like(loss_ref)

        xv = x_ref[...]
        r = lax.rsqrt(jnp.mean(xv * xv, axis=-1, keepdims=True) + EPS)
        xh = xv * r
        err = xh * g_ref[...] - t_ref[...]
        dy = err * (1.0 / D)
        dxh = dy * g_ref[...]
        dx_ref[...] = r * (dxh - xh * jnp.mean(dxh * xh, axis=-1, keepdims=True))
        dg_ref[...] += jnp.sum(dy * xh, axis=0, keepdims=True)
        loss_ref[...] += jnp.sum(err * err, axis=0, keepdims=True) * (0.5 / D)

    vec = jax.ShapeDtypeStruct((1, D), F32)
    return pl.pallas_call(
        body, out_shape=[jax.ShapeDtypeStruct((t, D), F32), vec, vec], grid=(t // tm,),
        in_specs=[_row_spec(tm, D), _row_spec(tm, D), _vec_spec(D)],
        out_specs=[_row_spec(tm, D), _vec_spec(D), _vec_spec(D)],
        name=name, compiler_params=_params(("arbitrary",)),
    )(x, target, g)


N_DEV = 8


def _mod_fwd(name, c_all, ada_w):
    cols = ada_w.shape[-1]

    def body(c_ref, w_ref, o_ref):
        cv = c_ref[...]
        ca = (cv * _sigmoid(cv)).astype(BF16)
        o_ref[...] = jnp.dot(ca, w_ref[...], preferred_element_type=F32)

    return pl.pallas_call(
        body, out_shape=jax.ShapeDtypeStruct((N_DEV, DEPTH * cols), F32), grid=(DEPTH,),
        in_specs=[pl.BlockSpec((N_DEV, D), lambda l: (0, 0)), pl.BlockSpec((None, D, cols), lambda l: (l, 0, 0))],
        out_specs=pl.BlockSpec((N_DEV, cols), lambda l: (0, l)), name=name, compiler_params=_params(("parallel",)),
    )(c_all, ada_w)


def _mod_bwd(name, c_cols, dmods, tk=256):
    cols = dmods.shape[-1]

    def body(c_ref, d_ref, o_ref):
        cv = c_ref[...]
        ca = cv * _sigmoid(cv)
        total = ca[:, 0:1] * d_ref[0:1, :]
        for b in range(1, N_DEV):
            total = total + ca[:, b:b + 1] * d_ref[b:b + 1, :]
        o_ref[...] = total

    return pl.pallas_call(
        body, out_shape=jax.ShapeDtypeStruct((DEPTH, D, cols), F32), grid=(DEPTH, D // tk),
        in_specs=[pl.BlockSpec((tk, N_DEV), lambda l, i: (i, 0)), pl.BlockSpec((None, N_DEV, cols), lambda l, i: (l, 0, 0))],
        out_specs=pl.BlockSpec((None, tk, cols), lambda l, i: (l, i, 0)), name=name,
        compiler_params=_params(("parallel", "parallel")),
    )(c_cols, dmods)


def _sum_devices(name, rows):
    n = rows.shape[1]

    def body(r_ref, o_ref):
        total = r_ref[0:1, :]
        for b in range(1, N_DEV):
            total = total + r_ref[b:b + 1, :]
        o_ref[...] = total

    whole = pl.BlockSpec(memory_space=pltpu.VMEM)
    return pl.pallas_call(body, out_shape=jax.ShapeDtypeStruct((1, n), F32), in_specs=[whole], out_specs=whole,
                          name=name)(rows)


def _rope_tables(t):
    half = HEAD // 2
    inv_freq = 10000.0 ** (-(jnp.arange(half, dtype=F32) * 2.0 / HEAD))
    ang = jnp.arange(t, dtype=F32)[:, None] * inv_freq[None, :]
    cos, sin = jnp.cos(ang), jnp.sin(ang)
    cos_t = jnp.tile(jnp.concatenate([cos, cos], axis=1), (1, VL // HEAD))
    sin_t = jnp.tile(jnp.concatenate([-sin, sin], axis=1), (1, VL // HEAD))
    return cos_t, sin_t


def _rotate(tv, cos, sin_signed):
    lane = lax.broadcasted_iota(jnp.int32, tv.shape, 1)
    first = (lane % HEAD) < (HEAD // 2)
    partner = jnp.where(first, pltpu.roll(tv, tv.shape[1] - HEAD // 2, 1), pltpu.roll(tv, HEAD // 2, 1))
    return tv * cos + partner * sin_signed


def _dilated_spec(tm, d):
    return pl.BlockSpec((tm // d, d * GW), lambda i: (i, 0))


def _dilated_shape(t, d, dtype):
    return jax.ShapeDtypeStruct((t // d, d * GW), dtype)


def _every(d, r, tm):
    return pl.ds(r, tm // d, stride=d) if d > 1 else slice(None)


def _rope_bwd(name, grads, cos_t, sin_t, tm=512):
    ng = len(DILATIONS)
    n = len(grads)
    t = grads[0].shape[0] * DILATIONS[0]

    halves = GW // VL

    def body(*refs):
        g_refs, cos_ref, sin_ref, o_ref, rows_ref = refs[:n], refs[n], refs[n + 1], refs[n + 2], refs[n + 3]
        cos, sin = cos_ref[...], -sin_ref[...]
        for idx in range(n):
            d = DILATIONS[idx % ng]
            for hh in range(halves):
                for r in range(d):
                    cols = slice(r * GW + hh * VL, r * GW + (hh + 1) * VL)
                    rows_ref[hh, _every(d, r, tm), :] = g_refs[idx][:, cols].astype(F32)
                piece = rows_ref[hh]
                if idx < 2 * ng:
                    piece = _rotate(piece, cos, sin)
                if idx < ng:
                    piece = piece * (HEAD ** -0.5)
                o_ref[:, idx * GW + hh * VL:idx * GW + (hh + 1) * VL] = piece.astype(BF16)

    dils = [DILATIONS[idx % ng] for idx in range(n)]
    return pl.pallas_call(
        body, out_shape=jax.ShapeDtypeStruct((t, n * GW), BF16), grid=(t // tm,),
        in_specs=[_dilated_spec(tm, d) for d in dils] + [_row_spec(tm, VL)] * 2, out_specs=_row_spec(tm, n * GW),
        scratch_shapes=[pltpu.VMEM((halves, tm, VL), F32)], name=name, compiler_params=_params(("parallel",)),
    )(*grads, cos_t, sin_t)


def _head_cols(h):
    return slice(h * HEAD, (h + 1) * HEAD)


def _band_mask_q(has_prev):
    qi = lax.broadcasted_iota(jnp.int32, (BAND, 2 * BAND), 0)
    kj = lax.broadcasted_iota(jnp.int32, (BAND, 2 * BAND), 1)
    dist = qi + BAND - kj
    return (dist >= 0) & (dist <= BAND) & ((kj >= BAND) | has_prev)


def _attn_fwd(name, q, k, v, group):
    d = DILATIONS[group]
    length = q.shape[0]
    qb = min(512, length)
    sub = qb // BAND
    nblk = length // qb

    def body(q_ref, kc_ref, kp_ref, vc_ref, vp_ref, o_ref, lse_ref):
        blk = pl.program_id(1)
        k_ext = jnp.concatenate([kp_ref[...], kc_ref[...]], axis=0)
        v_ext = jnp.concatenate([vp_ref[...], vc_ref[...]], axis=0)
        for j in range(sub):
            mask = _band_mask_q((blk * sub + j) > 0)
            qj = q_ref[j * BAND:(j + 1) * BAND, :]
            kj = k_ext[j * BAND:(j + 2) * BAND, :]
            vj = v_ext[j * BAND:(j + 2) * BAND, :]
            first = lax.broadcasted_iota(jnp.int32, (BAND, VL), 1) < HEAD
            for pair in range(GW // VL):
                cols = slice(pair * VL, (pair + 1) * VL)
                q2, k2, v2 = qj[:, cols], kj[:, cols], vj[:, cols]
                o_pair, lse_pair = None, None
                for own in (first, ~first):
                    s = lax.dot_general(jnp.where(own, q2, jnp.zeros_like(q2)), k2, (((1,), (1,)), ((), ())),
                                        preferred_element_type=F32)
                    s = jnp.where(mask, s, NEG_INF)
                    m = jnp.max(s, axis=-1, keepdims=True)
                    p = jnp.exp(s - m)
                    den = jnp.sum(p, axis=-1, keepdims=True)
                    o = jnp.dot(p.astype(BF16), v2, preferred_element_type=F32) / den
                    lse = jnp.broadcast_to(m + jnp.log(den), (BAND, VL))
                    o_pair = o if o_pair is None else jnp.where(first, o_pair, o)
                    lse_pair = lse if lse_pair is None else jnp.where(first, lse_pair, lse)
                o_ref[j * BAND:(j + 1) * BAND, cols] = o_pair
                lse_ref[j * BAND:(j + 1) * BAND, cols] = lse_pair

    prev = qb // BAND
    cur = lambda r, b: (b, r)
    before = lambda r, b: (jnp.maximum(b * prev - 1, 0), r)
    big, halo = pl.BlockSpec((qb, GW), cur), pl.BlockSpec((BAND, GW), before)
    return pl.pallas_call(
        body, out_shape=[jax.ShapeDtypeStruct((length, d * GW), F32)] * 2, grid=(d, nblk),
        in_specs=[big, big, halo, big, halo], out_specs=[big] * 2, name=name,
        compiler_params=_params(("parallel", "parallel")),
    )(q, k, k, v, v)


def _attn_bwd(name, q, k, v, do, lj, dsum, group):
    d = DILATIONS[group]
    length = q.shape[0]
    qb = min(1024, length)
    sub = qb // BAND
    nblk = length // qb
    total = length // BAND

    def body(qc_ref, qn_ref, kc_ref, kp_ref, vc_ref, vp_ref, doc_ref, don_ref, ljc_ref, ljn_ref, dsc_ref, dsn_ref,
             dq_ref, dk_ref, dv_ref):
        blk = pl.program_id(1)
        k_ext = jnp.concatenate([kp_ref[...], kc_ref[...]], axis=0)
        v_ext = jnp.concatenate([vp_ref[...], vc_ref[...]], axis=0)
        nt = (((1,), (1,)), ((), ()))
        tn = (((0,), (0,)), ((), ()))

        def scores(qh, doh, ljh, dsh, kh, vh, mask):
            s = lax.dot_general(qh, kh, nt, preferred_element_type=F32)
            p = jnp.where(mask, jnp.exp(s - ljh), 0.0)
            dp = lax.dot_general(doh, vh, nt, preferred_element_type=F32)
            return p.astype(BF16), (p * (dp - dsh)).astype(BF16)

        pairs = range(GW // VL)
        first = lax.broadcasted_iota(jnp.int32, (BAND, VL), 1) < HEAD

        def pair_grads(q2, do2, lj_ref, ds_ref, qrows, base, k2, v2, mask):
            dq_pair, dk_sum, dv_sum = None, None, None
            for idx, own in enumerate((first, ~first)):
                col = slice(base + idx * HEAD, base + idx * HEAD + 1)
                qm = jnp.where(own, q2, jnp.zeros_like(q2))
                dom = jnp.where(own, do2, jnp.zeros_like(do2))
                p, ds = scores(qm, dom, lj_ref[qrows, col], ds_ref[qrows, col], k2, v2, mask)
                dq = jnp.dot(ds, k2, preferred_element_type=F32)
                dk2 = lax.dot_general(ds, qm, tn, preferred_element_type=F32)
                dv2 = lax.dot_general(p, dom, tn, preferred_element_type=F32)
                dq_pair = dq if dq_pair is None else jnp.where(first, dq_pair, dq)
                dk_sum = dk2 if dk_sum is None else dk_sum + dk2
                dv_sum = dv2 if dv_sum is None else dv_sum + dv2
            return dq_pair, dk_sum, dv_sum

        held_k, held_v = [None] * len(pairs), [None] * len(pairs)
        for j in range(sub):
            rows = slice(j * BAND, (j + 1) * BAND)
            rows2 = slice(j * BAND, (j + 2) * BAND)
            mask = _band_mask_q((blk * sub + j) > 0)
            for pr in pairs:
                pc = slice(pr * VL, (pr + 1) * VL)
                dq, dk2, dv2 = pair_grads(qc_ref[rows, pc], doc_ref[rows, pc], ljc_ref, dsc_ref, rows, pr * VL,
                                          k_ext[rows2, pc], v_ext[rows2, pc], mask)
                dq_ref[rows, pc] = dq
                if j > 0:
                    done = slice((j - 1) * BAND, j * BAND)
                    dk_ref[done, pc] = held_k[pr] + dk2[:BAND]
                    dv_ref[done, pc] = (held_v[pr] + dv2[:BAND]).astype(BF16)
                held_k[pr], held_v[pr] = dk2[BAND:], dv2[BAND:]

        last = slice((sub - 1) * BAND, sub * BAND)
        qi = lax.broadcasted_iota(jnp.int32, (BAND, BAND), 0)
        kj = lax.broadcasted_iota(jnp.int32, (BAND, BAND), 1)
        mask = (kj >= qi) & ((blk + 1) * sub < total)
        for pr in pairs:
            pc = slice(pr * VL, (pr + 1) * VL)
            _, dk2, dv2 = pair_grads(qn_ref[:, pc], don_ref[:, pc], ljn_ref, dsn_ref, slice(None), pr * VL,
                                     kc_ref[last, pc], vc_ref[last, pc], mask)
            dk_ref[last, pc] = held_k[pr] + dk2
            dv_ref[last, pc] = (held_v[pr] + dv2).astype(BF16)

    prev = qb // BAND
    cur = lambda r, b: (b, r)
    before = lambda r, b: (jnp.maximum(b * prev - 1, 0), r)
    after = lambda r, b: (jnp.minimum((b + 1) * prev, total - 1), r)
    big = pl.BlockSpec((qb, GW), cur)
    nxt = pl.BlockSpec((BAND, GW), after)
    prv = pl.BlockSpec((BAND, GW), before)
    return pl.pallas_call(
        body, out_shape=[jax.ShapeDtypeStruct((length, d * GW), F32), jax.ShapeDtypeStruct((length, d * GW), F32),
                         jax.ShapeDtypeStruct((length, d * GW), BF16)], grid=(d, nblk),
        in_specs=[big, nxt, big, prv, big, prv, big, nxt, big, nxt, big, nxt],
        out_specs=[big] * 3, name=name, compiler_params=_params(("parallel", "parallel")),
    )(q, q, k, k, v, v, do, do, lj, lj, dsum, dsum)


SUBLANES = 8
CONV_CHUNK = 32
SHIFT_ROWS = HALO - SUBLANES


def _shifted_copies(buf_ref, sh_ref, tm):
    for s in range(1, SUBLANES):
        sh_ref[s - 1] = buf_ref[s:s + tm + SHIFT_ROWS, :]


def _window(buf_ref, sh_ref, offset, r0, rows):
    tiles, shift = divmod(offset, SUBLANES)
    src = buf_ref if shift == 0 else sh_ref.at[shift - 1]
    return src[pl.ds(pl.multiple_of(r0 + tiles * SUBLANES, SUBLANES), rows), :]


def _conv_fwd(name, zu, conv_w, conv_b, ln_g, ln_b, tm=256):
    t = zu.shape[0]
    per = tm // HALO

    def body(a_ref, gl_ref, ah_ref, glh_ref, w_ref, b_ref, g_ref, beta_ref, hc_ref, s_ref, ext_ref, sh_ref):
        i = pl.program_id(0)
        halo = ah_ref[...] * _sigmoid(glh_ref[...])
        ext_ref[0:HALO, :] = jnp.where(i > 0, halo, 0.0)
        ext_ref[HALO:, :] = a_ref[...] * _sigmoid(gl_ref[...])
        _shifted_copies(ext_ref, sh_ref, tm)

        def chunk(r, carry):
            r0 = pl.multiple_of(r * CONV_CHUNK, CONV_CHUNK)
            part = jnp.broadcast_to(b_ref[...], (CONV_CHUNK, D))
            for kk in range(CONV_K):
                part = part + w_ref[kk:kk + 1, :] * _window(ext_ref, sh_ref, HALO - CONV_K + 1 + kk, r0, CONV_CHUNK)
            hc_ref[pl.ds(r0, CONV_CHUNK), :] = part
            return carry

        lax.fori_loop(0, tm // CONV_CHUNK, chunk, 0)
        acc = hc_ref[...]
        mu = jnp.mean(acc, axis=-1, keepdims=True)
        xc = acc - mu
        var = jnp.mean(xc * xc, axis=-1, keepdims=True)
        ln = xc * lax.rsqrt(var + EPS) * g_ref[...] + beta_ref[...]
        s_ref[...] = (ln * _sigmoid(ln)).astype(BF16)

    halo_map = lambda col: (lambda i: (jnp.maximum(i * per - 1, 0), col))
    return pl.pallas_call(
        body, out_shape=[jax.ShapeDtypeStruct((t, D), F32), jax.ShapeDtypeStruct((t, D), BF16)], grid=(t // tm,),
        in_specs=[_row_spec(tm, D, 0), _row_spec(tm, D, 1), pl.BlockSpec((HALO, D), halo_map(0)),
                  pl.BlockSpec((HALO, D), halo_map(1)), pl.BlockSpec((HALO, D), lambda i: (0, 0)),
                  _vec_spec(D), _vec_spec(D), _vec_spec(D)],
        out_specs=[_row_spec(tm, D), _row_spec(tm, D)],
        scratch_shapes=[pltpu.VMEM((tm + HALO, D), F32), pltpu.VMEM((SUBLANES - 1, tm + SHIFT_ROWS, D), F32)],
        name=name, compiler_params=_params(("parallel",)),
    )(zu, zu, zu, zu, conv_w, conv_b, ln_g, ln_b)


def _ln_swish_bwd(hv, dsv, g, beta):
    mu = jnp.mean(hv, axis=-1, keepdims=True)
    xc = hv - mu
    rstd = lax.rsqrt(jnp.mean(xc * xc, axis=-1, keepdims=True) + EPS)
    xh = xc * rstd
    ln = xh * g + beta
    sg = _sigmoid(ln)
    dln = dsv * (sg * (1.0 + ln * (1.0 - sg)))
    dxh = dln * g
    dh = rstd * (dxh - jnp.mean(dxh, axis=-1, keepdims=True) - xh * jnp.mean(dxh * xh, axis=-1, keepdims=True))
    return dh, dln, xh


def _conv_bwd(name, zu, hc, ds, conv_w, ln_g, ln_b, tm=256):
    t = zu.shape[0]
    per = tm // HALO
    steps = t // tm

    group = 4

    def body(a_ref, gl_ref, ah_ref, glh_ref, hc_ref, hcn_ref, ds_ref, dsn_ref, w_ref, g_ref, beta_ref,
             dz_ref, dw_ref, dg_ref, dbeta_ref, dbias_ref, ext_ref, sh_ref, dext_ref, dsh_ref, sg_ref, part_ref):
        i = pl.program_id(0)

        @pl.when(i == 0)
        def _():
            part_ref[...] = jnp.zeros_like(part_ref)
            dg_ref[...] = jnp.zeros_like(dg_ref)
            dbeta_ref[...] = jnp.zeros_like(dbeta_ref)
            dbias_ref[...] = jnp.zeros_like(dbias_ref)

        sg_ref[...] = _sigmoid(gl_ref[...])
        ext_ref[0:HALO, :] = jnp.where(i > 0, ah_ref[...] * _sigmoid(glh_ref[...]), 0.0)
        ext_ref[HALO:, :] = a_ref[...] * sg_ref[...]
        dh, dln, xh = _ln_swish_bwd(hc_ref[...], ds_ref[...], g_ref[...], beta_ref[...])
        dext_ref[0:tm, :] = dh
        dg_ref[...] += jnp.sum(dln * xh, axis=0, keepdims=True)
        dbeta_ref[...] += jnp.sum(dln, axis=0, keepdims=True)
        dbias_ref[...] += jnp.sum(dh, axis=0, keepdims=True)
        dh_next, _, _ = _ln_swish_bwd(hcn_ref[...], dsn_ref[...], g_ref[...], beta_ref[...])
        dext_ref[tm:, :] = jnp.where(i < steps - 1, dh_next, 0.0)
        _shifted_copies(ext_ref, sh_ref, tm)
        _shifted_copies(dext_ref, dsh_ref, tm)

        def chunk(r, carry):
            r0 = pl.multiple_of(r * CONV_CHUNK, CONV_CHUNK)
            rows = pl.ds(r0, CONV_CHUNK)
            part = jnp.zeros((CONV_CHUNK, D), F32)
            for kk in range(CONV_K):
                part = part + w_ref[kk:kk + 1, :] * _window(dext_ref, dsh_ref, CONV_K - 1 - kk, r0, CONV_CHUNK)
            sg = sg_ref[rows, :]
            dz_ref[rows, 0:D] = (part * sg).astype(BF16)
            dz_ref[rows, D:] = (part * a_ref[rows, :] * sg * (1.0 - sg)).astype(BF16)
            return carry

        lax.fori_loop(0, tm // CONV_CHUNK, chunk, 0)

        for k0 in range(0, CONV_K, group):
            taps = range(k0, min(k0 + group, CONV_K))

            def tile(r, parts, taps=taps):
                r0 = pl.multiple_of(r * CONV_CHUNK, CONV_CHUNK)
                dv = dext_ref[pl.ds(r0, CONV_CHUNK), :]
                out = []
                for p, kk in zip(parts, taps):
                    prod = dv * _window(ext_ref, sh_ref, HALO - CONV_K + 1 + kk, r0, CONV_CHUNK)
                    for s in range(0, CONV_CHUNK, SUBLANES):
                        p = p + prod[s:s + SUBLANES, :]
                    out.append(p)
                return tuple(out)

            parts = lax.fori_loop(0, tm // CONV_CHUNK, tile, tuple(jnp.zeros((SUBLANES, D), F32) for _ in taps))
            for p, kk in zip(parts, taps):
                part_ref[kk * SUBLANES:(kk + 1) * SUBLANES, :] += p

        @pl.when(i == steps - 1)
        def _():
            for kk in range(HALO):
                dw_ref[kk:kk + 1, :] = jnp.sum(part_ref[kk * SUBLANES:(kk + 1) * SUBLANES, :], axis=0, keepdims=True)

    halo_map = lambda col: (lambda i: (jnp.maximum(i * per - 1, 0), col))
    next_rows = pl.BlockSpec((HALO, D), lambda i: (jnp.minimum((i + 1) * per, t // HALO - 1), 0))
    shifted = pltpu.VMEM((SUBLANES - 1, tm + SHIFT_ROWS, D), F32)
    vec = jax.ShapeDtypeStruct((1, D), F32)
    return pl.pallas_call(
        body, out_shape=[jax.ShapeDtypeStruct((t, 2 * D), BF16), jax.ShapeDtypeStruct((HALO, D), F32), vec, vec, vec],
        grid=(steps,),
        in_specs=[_row_spec(tm, D, 0), _row_spec(tm, D, 1), pl.BlockSpec((HALO, D), halo_map(0)),
                  pl.BlockSpec((HALO, D), halo_map(1)), _row_spec(tm, D), next_rows, _row_spec(tm, D), next_rows,
                  pl.BlockSpec((HALO, D), lambda i: (0, 0)), _vec_spec(D), _vec_spec(D)],
        out_specs=[_row_spec(tm, 2 * D), pl.BlockSpec((HALO, D), lambda i: (0, 0))] + [_vec_spec(D)] * 3,
        scratch_shapes=[pltpu.VMEM((tm + HALO, D), F32), shifted, pltpu.VMEM((tm + HALO, D), F32), shifted,
                        pltpu.VMEM((tm, D), F32), pltpu.VMEM((HALO * SUBLANES, D), F32)],
        name=name, compiler_params=_params(("arbitrary",)),
    )(zu, zu, zu, zu, hc, hc, ds, ds, conv_w, ln_g, ln_b)


NT = (((1,), (1,)), ((), ()))


def _resident(w, at):
    block = (None,) * (1 + len(at)) + tuple(w.shape[-2:])
    return [pl.BlockSpec(block, functools.partial(lambda i, k: (k, *at, 0, 0), k=k), pipeline_mode=pl.Buffered(1))
            for k in range(SHARDS)]


def _ffn_fwd(tag, x, norm, w, at, gate, tm=512):
    t = x.shape[0]
    slab = min(tm, MM_SLAB)

    def body(*refs):
        x_ref, gain_ref, scale_ref, shift_ref, gate_ref = refs[:5]
        wg, wu, wd = refs[5:5 + SHARDS], refs[5 + SHARDS:5 + 2 * SHARDS], refs[5 + 2 * SHARDS:5 + 3 * SHARDS]
        h_ref, g_ref, u_ref, xn_ref, f_ref = refs[5 + 3 * SHARDS:]
        half_gate = 0.5 * gate_ref[...]
        for r0 in range(0, tm, slab):
            rows = slice(r0, r0 + slab)
            xs = x_ref[rows, :]
            r = lax.rsqrt(jnp.mean(xs * xs, axis=-1, keepdims=True) + EPS)
            hs = ((xs * r) * gain_ref[...] * (1.0 + scale_ref[...]) + shift_ref[...]).astype(BF16)
            h_ref[rows, :] = hs
            tot = None
            for k in range(SHARDS):
                gk = jnp.dot(hs, wg[k][...], preferred_element_type=F32)
                uk = jnp.dot(hs, wu[k][...], preferred_element_type=F32)
                g_ref[k, rows, :] = gk.astype(BF16)
                u_ref[k, rows, :] = uk.astype(BF16)
                ak = ((gk * _sigmoid(gk)) * uk).astype(BF16)
                part = jnp.dot(ak, wd[k][...], preferred_element_type=F32)
                tot = part if tot is None else tot + part
            xn_ref[rows, :] = xs + half_gate * tot
            f_ref[rows, :] = tot.astype(BF16)

    hidden = jax.ShapeDtypeStruct((SHARDS, t, FSH), BF16)
    hidden_spec = pl.BlockSpec((SHARDS, tm, FSH), lambda i: (0, i, 0))
    half = jax.ShapeDtypeStruct((t, D), BF16)
    h, gv, uv, x_new, f = pl.pallas_call(
        body, out_shape=[half, hidden, hidden, jax.ShapeDtypeStruct((t, D), F32), half], grid=(t // tm,),
        in_specs=[_row_spec(tm, D)] + [_vec_spec(D)] * 4 + _resident(w["ffn_wg"], at)
        + _resident(w["ffn_wu"], at) + _resident(w["ffn_wd"], at),
        out_specs=[_row_spec(tm, D), hidden_spec, hidden_spec, _row_spec(tm, D), _row_spec(tm, D)],
        name=f"ffn_fwd_{tag}", compiler_params=_params(("parallel",)),
    )(x, *norm, gate, *([w["ffn_wg"]] * SHARDS), *([w["ffn_wu"]] * SHARDS), *([w["ffn_wd"]] * SHARDS))
    return x_new, h, (gv, uv, f)


def _ffn_hidden_bwd(tag, dx, f, gate, gv, uv, x, gain, scale, w, at, tm=256):
    t = dx.shape[0]
    steps = t // tm
    slab = min(tm, MM_SLAB)

    def body(*refs):
        dx_ref, f_ref, gate_ref, g_ref, u_ref, x_ref, gain_ref, scale_ref = refs[:8]
        wg, wu, wd = refs[8:8 + SHARDS], refs[8 + SHARDS:8 + 2 * SHARDS], refs[8 + 2 * SHARDS:8 + 3 * SHARDS]
        df_ref, dg_ref, du_ref, dxin_ref, dgate_ref, dgain_ref, dscale_ref, dshift_ref = refs[8 + 3 * SHARDS:]
        i = pl.program_id(0)

        @pl.when(i == 0)
        def _():
            dgate_ref[...] = jnp.zeros_like(dgate_ref)
            dgain_ref[...] = jnp.zeros_like(dgain_ref)
            dshift_ref[...] = jnp.zeros_like(dshift_ref)

        half_gate = 0.5 * gate_ref[...]
        norm_w = gain_ref[...] * (1.0 + scale_ref[...])
        for r0 in range(0, tm, slab):
            rows = slice(r0, r0 + slab)
            dxs = dx_ref[rows, :]
            dfs = (half_gate * dxs).astype(BF16)
            df_ref[rows, :] = dfs
            dgate_ref[...] += jnp.sum((0.5 * f_ref[rows, :].astype(F32)) * dxs, axis=0, keepdims=True)
            tot = None
            for k in range(SHARDS):
                da = lax.dot_general(dfs, wd[k][...], NT, preferred_element_type=F32)
                gk, uk = g_ref[k, rows, :].astype(F32), u_ref[k, rows, :].astype(F32)
                sg = _sigmoid(gk)
                dgk = (da * uk * (sg * (1.0 + gk * (1.0 - sg)))).astype(BF16)
                duk = (da * (gk * sg)).astype(BF16)
                dg_ref[k, rows, :] = dgk
                du_ref[k, rows, :] = duk
                part = (lax.dot_general(dgk, wg[k][...], NT, preferred_element_type=F32)
                        + lax.dot_general(duk, wu[k][...], NT, preferred_element_type=F32))
                tot = part if tot is None else tot + part
            xs = x_ref[rows, :]
            r = lax.rsqrt(jnp.mean(xs * xs, axis=-1, keepdims=True) + EPS)
            xh = xs * r
            dxh = tot * norm_w
            dxin_ref[rows, :] = dxs + r * (dxh - xh * jnp.mean(dxh * xh, axis=-1, keepdims=True))
            dgain_ref[...] += jnp.sum(tot * xh, axis=0, keepdims=True)
            dshift_ref[...] += jnp.sum(tot, axis=0, keepdims=True)

        @pl.when(i == steps - 1)
        def _():
            acc = dgain_ref[...]
            dgain_ref[...] = acc * (1.0 + scale_ref[...])
            dscale_ref[...] = acc * gain_ref[...]

    hidden = jax.ShapeDtypeStruct((SHARDS, t, FSH), BF16)
    hidden_spec = pl.BlockSpec((SHARDS, tm, FSH), lambda i: (0, i, 0))
    vec = jax.ShapeDtypeStruct((1, D), F32)
    return pl.pallas_call(
        body, out_shape=[jax.ShapeDtypeStruct((t, D), BF16), hidden, hidden, jax.ShapeDtypeStruct((t, D), F32),
                         vec, vec, vec, vec],
        grid=(steps,),
        in_specs=[_row_spec(tm, D), _row_spec(tm, D), _vec_spec(D), hidden_spec, hidden_spec, _row_spec(tm, D),
                  _vec_spec(D), _vec_spec(D)]
        + _resident(w["ffn_wg"], at) + _resident(w["ffn_wu"], at) + _resident(w["ffn_wd"], at),
        out_specs=[_row_spec(tm, D), hidden_spec, hidden_spec, _row_spec(tm, D)] + [_vec_spec(D)] * 4,
        name=f"ffn_hidden_bwd_{tag}", compiler_params=_params(("arbitrary",)),
    )(dx, f, gate, gv, uv, x, gain, scale, *([w["ffn_wg"]] * SHARDS), *([w["ffn_wu"]] * SHARDS),
      *([w["ffn_wd"]] * SHARDS))


def _ffn_bwd(tag, dx, x, h, saved, w, at, g, scale, gate, into):
    gv, uv, f = saved
    df, dg, du, dx_in, dgate, dgn, dscale, dshift = _ffn_hidden_bwd(tag, dx, f, gate, gv, uv, x, g, scale, w, at)

    def act(blocks):
        gf, uf = blocks[0].astype(F32), blocks[1].astype(F32)
        return ((gf * _sigmoid(gf)) * uf).astype(BF16)

    dwd = _mm_tn(f"ffn_dwd_{tag}", [gv, uv], df, tk=FSH, tn=1024, tt=4096, a_fn=act, a_tiled=True,
                 into=(into["ffn_wd"], at))
    dwg = _mm_tn(f"ffn_dwg_{tag}", h, dg, tk=1024, tn=FSH, tt=4096, b_tiled=True, into=(into["ffn_wg"], at))
    dwu = _mm_tn(f"ffn_dwu_{tag}", h, du, tk=1024, tn=FSH, tt=4096, b_tiled=True, into=(into["ffn_wu"], at))
    return dx_in, dict(ffn_wg=dwg, ffn_wu=dwu, ffn_wd=dwd), dgn, (dshift, dscale, dgate)


def _mix_in_fwd(name, x, g, scale, shift, w_parts, cos_t, sin_t, tm=512):
    t = x.shape[0]
    ng = len(DILATIONS)
    n = 3 * ng
    halves = GW // VL
    strips = QKV // VL

    def body(x_ref, g_ref, sc_ref, sh_ref, cos_ref, sin_ref, wq_ref, wu_ref, wg_ref, h_ref, *rest):
        o_refs, zu_ref, zg_ref, strip_ref = rest[:n], rest[n], rest[n + 1], rest[n + 2]
        for r0 in range(0, tm, MM_SLAB):
            rows = slice(r0, r0 + MM_SLAB)
            xv = x_ref[rows, :]
            r = lax.rsqrt(jnp.mean(xv * xv, axis=-1, keepdims=True) + EPS)
            hv = ((xv * r) * g_ref[...] * (1.0 + sc_ref[...]) + sh_ref[...]).astype(BF16)
            h_ref[rows, :] = hv
            zu_ref[rows, :] = jnp.dot(hv, wu_ref[...], preferred_element_type=F32)
            zg_ref[rows, :] = jnp.dot(hv, wg_ref[...], preferred_element_type=F32).astype(BF16)
            zq = jnp.dot(hv, wq_ref[...], preferred_element_type=F32)
            for j in range(strips):
                strip_ref[j, rows, :] = zq[:, j * VL:(j + 1) * VL]
        for idx in range(n):
            d = DILATIONS[idx % ng]
            for r in range(d):
                rows = _every(d, r, tm)
                for hh in range(halves):
                    piece = strip_ref[halves * idx + hh, rows, :]
                    if idx < 2 * ng:
                        piece = _rotate(piece, cos_ref[rows, :], sin_ref[rows, :])
                    if idx < ng:
                        piece = piece * (HEAD ** -0.5)
                    o_refs[idx][:, r * GW + hh * VL:r * GW + (hh + 1) * VL] = piece.astype(BF16)

    dils = [DILATIONS[idx % ng] for idx in range(n)]
    return pl.pallas_call(
        body, out_shape=[jax.ShapeDtypeStruct((t, D), BF16)] + [_dilated_shape(t, d, BF16) for d in dils]
        + [jax.ShapeDtypeStruct((t, 2 * D), F32), jax.ShapeDtypeStruct((t, 2 * D), BF16)], grid=(t // tm,),
        in_specs=[_row_spec(tm, D), _vec_spec(D), _vec_spec(D), _vec_spec(D), _row_spec(tm, VL), _row_spec(tm, VL)]
        + [_whole(wp) for wp in w_parts],
        out_specs=[_row_spec(tm, D)] + [_dilated_spec(tm, d) for d in dils] + [_row_spec(tm, 2 * D)] * 2,
        scratch_shapes=[pltpu.VMEM((strips, tm, VL), F32)], name=name, compiler_params=_params(("parallel",)),
    )(x, g, scale, shift, cos_t, sin_t, *w_parts)


def _mix_fwd(tag, x, norm, w_in, attn_wo, conv_w, conv_b, ln_g, ln_b, conv_wo, w_out, gate, cos_t, sin_t):
    w_qkv, w_u, w_g = w_in[:, :QKV], w_in[:, QKV:QKV + 2 * D], w_in[:, QKV + 2 * D:]
    h, *qkv, zu, zg = _mix_in_fwd(f"mix_in_{tag}", x, *norm, [w_qkv, w_u, w_g], cos_t, sin_t)
    n = len(DILATIONS)
    outs, lses = [], []
    for grp in range(n):
        o, lse = _attn_fwd(f"attn_fwd_{tag}_{grp}", qkv[grp], qkv[n + grp], qkv[2 * n + grp], grp)
        outs.append(o)
        lses.append(lse)
    hc, s = _conv_fwd(f"conv_fwd_{tag}", zu, conv_w, conv_b, ln_g, ln_b)
    ob, of, lj, y, ya, yc, sa, sc, x_new, f = _mix_out_fwd(f"mix_out_{tag}", outs, lses, s, zg, x, gate, attn_wo,
                                                          conv_wo, w_out)
    return x_new, h, (zu, sa, sc, qkv, ob, of, lj, hc, s, y, ya, yc, f, (w_qkv, w_u, w_g))


def _whole(w):
    return pl.BlockSpec(w.shape, lambda i: (0, 0), pipeline_mode=pl.Buffered(1))


def _merge_groups(in_refs, rows_ref, ob_ref, of_ref, lj_ref, tm):
    n = len(in_refs) // 2
    for hh in range(GW // VL):
        for idx in range(2 * n):
            d = DILATIONS[idx % n]
            for r in range(d):
                cols = slice(r * GW + hh * VL, r * GW + (hh + 1) * VL)
                rows_ref[idx, _every(d, r, tm), :] = in_refs[idx][:, cols]
        ls = [rows_ref[n + g] for g in range(n)]
        m = ls[0]
        for v in ls[1:]:
            m = jnp.maximum(m, v)
        es = [jnp.exp(v - m) for v in ls]
        tot = es[0]
        for v in es[1:]:
            tot = tot + v
        acc = (es[0] / tot) * rows_ref[0]
        for g in range(1, n):
            acc = acc + (es[g] / tot) * rows_ref[g]
        half = slice(hh * VL, (hh + 1) * VL)
        ob_ref[:, half] = acc.astype(BF16)
        of_ref[:, half] = acc
        lj_ref[:, half] = m + jnp.log(tot)


def _mix_out_fwd(name, outs, lses, s, zg, x, gate, attn_wo, conv_wo, w_out, tm=512):
    t = x.shape[0]
    n = len(outs)

    def body(*refs):
        group_refs = refs[:2 * n]
        s_ref, za_ref, zc_ref, x_ref, gate_ref, wa_ref, wc_ref, wo_ref = refs[2 * n:2 * n + 8]
        ob_ref, of_ref, lj_ref, y_ref, ya_ref, yc_ref, sa_ref, sc_ref, xn_ref, f_ref, rows_ref = refs[2 * n + 8:]
        _merge_groups(group_refs, rows_ref, ob_ref, of_ref, lj_ref, tm)
        for r0 in range(0, tm, MM_SLAB):
            rows = slice(r0, r0 + MM_SLAB)
            ya = jnp.dot(ob_ref[rows, :], wa_ref[...], preferred_element_type=F32)
            yc = jnp.dot(s_ref[rows, :], wc_ref[...], preferred_element_type=F32)
            sa, sc = _sigmoid(za_ref[rows, :].astype(F32)), _sigmoid(zc_ref[rows, :].astype(F32))
            y = (sa * ya + sc * yc).astype(BF16)
            out = jnp.dot(y, wo_ref[...], preferred_element_type=F32)
            y_ref[rows, :], ya_ref[rows, :], yc_ref[rows, :] = y, ya.astype(BF16), yc.astype(BF16)
            sa_ref[rows, :], sc_ref[rows, :] = sa.astype(BF16), sc.astype(BF16)
            xn_ref[rows, :] = x_ref[rows, :] + gate_ref[...] * out
            f_ref[rows, :] = out.astype(BF16)

    half = jax.ShapeDtypeStruct((t, D), BF16)
    group = lambda dt: jax.ShapeDtypeStruct((t, GW), dt)
    return pl.pallas_call(
        body, out_shape=[group(BF16), group(F32), group(F32)] + [half] * 5 + [jax.ShapeDtypeStruct((t, D), F32), half],
        grid=(t // tm,),
        in_specs=[_dilated_spec(tm, DILATIONS[idx % n]) for idx in range(2 * n)]
        + [_row_spec(tm, D), _row_spec(tm, D, 0), _row_spec(tm, D, 1), _row_spec(tm, D), _vec_spec(D),
           _whole(attn_wo), _whole(conv_wo), _whole(w_out)],
        out_specs=[_row_spec(tm, GW)] * 3 + [_row_spec(tm, D)] * 7,
        scratch_shapes=[pltpu.VMEM((2 * n, tm, VL), F32)], name=name, compiler_params=_params(("parallel",)),
    )(*outs, *lses, s, zg, zg, x, gate, attn_wo, conv_wo, w_out)


def _mix_out_bwd(name, dx, f, gate, sa, sc, ya, yc, o, lj, attn_wo, conv_wo, w_out, tm=512):
    t = dx.shape[0]
    n = len(DILATIONS)
    halves = GW // VL

    def body(*refs):
        dx_ref, f_ref, gate_ref, sa_ref, sc_ref, ya_ref, yc_ref = refs[:7]
        o_refs, lj_refs = refs[7:7 + halves], refs[7 + halves:7 + 2 * halves]
        wa_ref, wc_ref, wo_ref = refs[7 + 2 * halves:10 + 2 * halves]
        df_ref, dya_ref, dyc_ref, dzg_ref, ds_ref, dgate_ref = refs[10 + 2 * halves:16 + 2 * halves]
        prep_refs = refs[16 + 2 * halves:16 + 2 * halves + 3 * n]
        do_ref, dsum_ref = refs[16 + 2 * halves + 3 * n:]

        @pl.when(pl.program_id(0) == 0)
        def _():
            dgate_ref[...] = jnp.zeros_like(dgate_ref)

        for r0 in range(0, tm, MM_SLAB):
            rows = slice(r0, r0 + MM_SLAB)
            dxs = dx_ref[rows, :]
            dfs = (gate_ref[...] * dxs).astype(BF16)
            df_ref[rows, :] = dfs
            dgate_ref[...] += jnp.sum(f_ref[rows, :].astype(F32) * dxs, axis=0, keepdims=True)
            dy = lax.dot_general(dfs, wo_ref[...], NT, preferred_element_type=F32)
            ga, gc = sa_ref[rows, :].astype(F32), sc_ref[rows, :].astype(F32)
            dya, dyc = (dy * ga).astype(BF16), (dy * gc).astype(BF16)
            dya_ref[rows, :], dyc_ref[rows, :] = dya, dyc
            dzg_ref[rows, 0:D] = (dy * ya_ref[rows, :].astype(F32) * (ga * (1.0 - ga))).astype(BF16)
            dzg_ref[rows, D:] = (dy * yc_ref[rows, :].astype(F32) * (gc * (1.0 - gc))).astype(BF16)
            do = lax.dot_general(dya, wa_ref[...], NT, preferred_element_type=F32)
            for hh in range(halves):
                do_ref[hh, rows, :] = do[:, hh * VL:(hh + 1) * VL]
            ds_ref[rows, :] = lax.dot_general(dyc, wc_ref[...], NT, preferred_element_type=F32)

        for hh in range(halves):
            prod = do_ref[hh] * o_refs[hh][...]
            parts = [jnp.broadcast_to(jnp.sum(prod[:, _head_cols(h)], axis=-1, keepdims=True), (tm, HEAD))
                     for h in range(VL // HEAD)]
            dsum_ref[...] = jnp.concatenate(parts, axis=1)
            for g, d in enumerate(DILATIONS):
                for r in range(d):
                    rows, cols = _every(d, r, tm), slice(r * GW + hh * VL, r * GW + (hh + 1) * VL)
                    prep_refs[g][:, cols] = dsum_ref[rows, :]
                    prep_refs[n + g][:, cols] = do_ref[hh, rows, :].astype(BF16)
                    prep_refs[2 * n + g][:, cols] = lj_refs[hh][rows, :]

    half = jax.ShapeDtypeStruct((t, D), BF16)
    prep_shapes = [_dilated_shape(t, d, dt) for dt in (F32, BF16, F32) for d in DILATIONS]
    half_specs = [_row_spec(tm, VL, hh) for hh in range(halves)]
    outs = pl.pallas_call(
        body, out_shape=[half, half, half, jax.ShapeDtypeStruct((t, 2 * D), BF16), jax.ShapeDtypeStruct((t, D), F32),
                         jax.ShapeDtypeStruct((1, D), F32)] + prep_shapes, grid=(t // tm,),
        in_specs=[_row_spec(tm, D), _row_spec(tm, D), _vec_spec(D)] + [_row_spec(tm, D)] * 4 + half_specs * 2
        + [_whole(attn_wo), _whole(conv_wo), _whole(w_out)],
        out_specs=[_row_spec(tm, D)] * 3 + [_row_spec(tm, 2 * D), _row_spec(tm, D), _vec_spec(D)]
        + [_dilated_spec(tm, d) for d in DILATIONS] * 3,
        scratch_shapes=[pltpu.VMEM((halves, tm, VL), F32), pltpu.VMEM((tm, VL), F32)],
        name=name, compiler_params=_params(("arbitrary",)),
    )(dx, f, gate, sa, sc, ya, yc, *([o] * halves), *([lj] * halves), attn_wo, conv_wo, w_out)
    return outs[:6], outs[6:]


def _mix_bwd(tag, dx, x, h, saved, attn_wo, conv_w, ln_g, ln_b, conv_wo, w_out, g, scale, gate, cos_t, sin_t):
    zu, sa, sc, qkv, ob, of, lj, hc, s, y, ya, yc, f, w_parts = saved
    n = len(DILATIONS)
    (df, dya, dyc, dzg, ds, dgate), prep = _mix_out_bwd(f"mix_out_bwd_{tag}", dx, f, gate, sa, sc, ya, yc, of, lj,
                                                         attn_wo, conv_wo, w_out)
    dw_out = _mm_tn(f"mix_dwout_{tag}", y, df, tk=1024, tn=1024, tt=4096)
    dw_attn = _mm_tn(f"mix_dwattn_{tag}", ob, dya, tk=GW, tn=1024, tt=4096)
    dw_conv_o = _mm_tn(f"mix_dwconvo_{tag}", s, dyc, tk=1024, tn=1024, tt=4096)

    dqs, dks, dvs = [], [], []
    for grp in range(n):
        dq, dk, dv = _attn_bwd(f"attn_bwd_{tag}_{grp}", qkv[grp], qkv[n + grp], qkv[2 * n + grp], prep[n + grp],
                               prep[2 * n + grp], prep[grp], grp)
        dqs.append(dq)
        dks.append(dk)
        dvs.append(dv)
    dzqkv = _rope_bwd(f"rope_bwd_{tag}", dqs + dks + dvs, cos_t, sin_t)

    dzu, dconv_w, dln_g, dln_b, dconv_b = _conv_bwd(f"conv_bwd_{tag}", zu, hc, ds, conv_w, ln_g, ln_b)

    dz_parts = [dzqkv, dzu, dzg]
    dw_in = jnp.concatenate(
        [_mm_tn(f"mix_dwin_{tag}_{i}", h, dzp, tk=1024, tn=dzp.shape[1] // 2, tt=2048)
         for i, dzp in enumerate(dz_parts)], axis=1)
    dx_in, dgn, dscale, dshift = _proj_norm_bwd(f"mix_dh_{tag}", dz_parts, list(w_parts), x, dx, g, scale)
    grads = dict(w_in=dw_in, attn_wo=dw_attn, conv_w=dconv_w[:CONV_K], conv_b=dconv_b, conv_ln_g=dln_g,
                 conv_ln_b=dln_b, conv_wo=dw_conv_o, w_out=dw_out)
    return dx_in, grads, dgn, (dshift, dscale, dgate)


def _local_step(x, mod, target, w, wf):
    t = x.shape[0]
    cos_t, sin_t = _rope_tables(t)
    row = lambda v: v.reshape(1, -1)
    conv_w_pad = jnp.concatenate([wf["conv_w"], jnp.zeros((DEPTH, HALO - CONV_K, D), F32)], axis=1)

    saved = []
    for l in range(DEPTH):
        mods = [mod[l:l + 1, i * D:(i + 1) * D] for i in range(N_MOD)]
        gains = [row(wf["norm_g"][l, i]) for i in range(3)]
        lay = dict(mods=mods, gains=gains)

        lay["x0"] = x
        x, lay["h0"], lay["ffn0"] = _ffn_fwd(f"a_{l}", x, (gains[0], mods[1], mods[0]), w, (l, 0), mods[2])
        lay["x1"] = x
        x, lay["h1"], lay["mix"] = _mix_fwd(f"{l}", x, (gains[1], mods[4], mods[3]), w["w_in"][l], w["attn_wo"][l],
                                            conv_w_pad[l],
                                 row(wf["conv_b"][l]), row(wf["conv_ln_g"][l]), row(wf["conv_ln_b"][l]),
                                 w["conv_wo"][l], w["w_out"][l], mods[5], cos_t, sin_t)
        lay["x2"] = x
        x, lay["h2"], lay["ffn1"] = _ffn_fwd(f"b_{l}", x, (gains[2], mods[7], mods[6]), w, (l, 1), mods[8])
        saved.append(lay)

    dx, dfinal_g, loss_cols = _loss_bwd("loss_head", x, target, row(wf["final_g"]))

    ffn_grads = {n: jnp.zeros(w[n].shape, F32) for n in ("ffn_wg", "ffn_wu", "ffn_wd")}
    per_layer = []
    for l in reversed(range(DEPTH)):
        lay = saved[l]
        mods, gains = lay["mods"], lay["gains"]
        dx, ffn_grads, dgn2, dmod2 = _ffn_bwd(f"b_{l}", dx, lay["x2"], lay["h2"], lay["ffn1"], w, (l, 1),
                                              gains[2], mods[7], mods[8], ffn_grads)
        dx, gm, dgn1, dmod1 = _mix_bwd(f"{l}", dx, lay["x1"], lay["h1"], lay["mix"], w["attn_wo"][l],
                                       conv_w_pad[l], row(wf["conv_ln_g"][l]), row(wf["conv_ln_b"][l]),
                                       w["conv_wo"][l], w["w_out"][l], gains[1], mods[4], mods[5], cos_t, sin_t)
        dx, ffn_grads, dgn0, dmod0 = _ffn_bwd(f"a_{l}", dx, lay["x0"], lay["h0"], lay["ffn0"], w, (l, 0),
                                              gains[0], mods[1], mods[2], ffn_grads)
        g = dict(gm)
        g["dmod"] = jnp.concatenate(list(dmod0) + list(dmod1) + list(dmod2), axis=1)
        g["norm_g"] = [dgn0[0], dgn1[0], dgn2[0]]
        for name in ("conv_b", "conv_ln_g", "conv_ln_b"):
            g[name] = g[name][0]
        per_layer.append(g)
    per_layer.reverse()
    grads = {name: [per_layer[l][name] for l in range(DEPTH)] for name in per_layer[0]}
    grads["dmod"] = jnp.concatenate(grads["dmod"], axis=0)
    grads.update(ffn_grads)
    grads["final_g"] = dfinal_g[0]
    return loss_cols, dx, grads


def _split_bits(w):
    bits = lax.bitcast_convert_type(w, jnp.uint32)
    hi = lax.bitcast_convert_type((bits >> 16).astype(jnp.uint16), BF16)
    lo = lax.bitcast_convert_type((bits & 0xFFFF).astype(jnp.uint16), BF16)
    return hi, lo


def _join_bits(hi, lo):
    h = lax.bitcast_convert_type(hi, jnp.uint16).astype(jnp.uint32)
    l = lax.bitcast_convert_type(lo, jnp.uint16).astype(jnp.uint32)
    return lax.bitcast_convert_type((h << 16) | l, F32)


def _pack(parts, rows):
    out = []
    for p in parts:
        flat = p.reshape(-1)
        pad = -flat.shape[0] % LANES
        out.append(jnp.concatenate([flat, jnp.zeros((pad,), flat.dtype)]) if pad else flat)
    flat = jnp.concatenate(out)
    return jnp.concatenate([flat, jnp.zeros((rows * LANES - flat.shape[0],), flat.dtype)]).reshape(rows, LANES)


def _unpack(buf, shapes):
    out, row = [], 0
    for shape in shapes:
        size = 1
        for s in shape:
            size *= s
        rows = -(-size // LANES)
        out.append(buf[row:row + rows].reshape(-1)[:size].reshape(shape))
        row += rows
    return out


def _place():
    x, y, c = lax.axis_index("x"), lax.axis_index("y"), lax.axis_index("c")
    chips = [(1 - x, y), (x, 1 - y), (1 - x, 1 - y)]
    return x, y, c, chips


def _chip_index():
    return (2 * lax.axis_index("x") + lax.axis_index("y")).astype(jnp.int32)


HBM_SPEC = pl.BlockSpec(memory_space=pltpu.HBM)


def _gather_rows(name, block):
    m, n = block.shape

    def body(x_ref, out_ref, send_sems, recv_sems, local_sem):
        x, y, c, chips = _place()
        me, sibling = (x, y, c), (x, y, 1 - c)

        def rows(px, py, pc):
            return out_ref.at[pl.ds((4 * px + 2 * py + pc) * m, m), :]

        def copy(k, owner, to, src=None):
            return pltpu.make_async_remote_copy(
                src_ref=rows(*owner) if src is None else src, dst_ref=rows(*owner), send_sem=send_sems.at[k],
                recv_sem=recv_sems.at[k], device_id=to, device_id_type=MESH)

        mine = pltpu.make_async_copy(x_ref, rows(*me), local_sem)
        mine.start()
        first = [copy(0, me, sibling, src=x_ref)] + [copy(1 + j, me, (*chip, c), src=x_ref)
                                                     for j, chip in enumerate(chips)]
        for cp in first:
            cp.start()
        passed = [copy(4 + j, (*chip, c), sibling) for j, chip in enumerate(chips)]
        for j, chip in enumerate(chips):
            copy(1 + j, (*chip, c), me).wait_recv()
            passed[j].start()
        copy(0, sibling, me).wait_recv()
        for j, chip in enumerate(chips):
            copy(4 + j, (*chip, 1 - c), me).wait_recv()
        for cp in first + passed:
            cp.wait_send()
        mine.wait()

    whole = pl.BlockSpec(memory_space=pltpu.VMEM)
    return pl.pallas_call(
        body, out_shape=jax.ShapeDtypeStruct((N_DEV * m, n), block.dtype), in_specs=[whole], out_specs=whole,
        scratch_shapes=[pltpu.SemaphoreType.DMA((7,)), pltpu.SemaphoreType.DMA((7,)), pltpu.SemaphoreType.DMA],
        name=name,
    )(block)


def _gather_weights(arrays):
    n = len(arrays)

    def body(*refs):
        outs, send_sems, recv_sems = refs[n:2 * n], refs[2 * n], refs[2 * n + 1]
        x, y, c, chips = _place()
        me = 2 * x + y
        sibling = (x, y, 1 - c)
        there = [2 * chip[0] + chip[1] for chip in chips]

        def copy(a, k, chip, layer, to):
            piece = outs[a].at[chip, layer]
            return pltpu.make_async_remote_copy(
                src_ref=piece, dst_ref=piece, send_sem=send_sems.at[6 * a + k], recv_sem=recv_sems.at[6 * a + k],
                device_id=to, device_id_type=MESH)

        first = [copy(a, j, me, c, (*chip, c)) for a in range(n) for j, chip in enumerate(chips)]
        for cp in first:
            cp.start()
        passed = []
        for a in range(n):
            for j in range(3):
                copy(a, j, there[j], c, sibling).wait_recv()
                passed.append(copy(a, 3 + j, there[j], c, sibling))
                passed[-1].start()
        for a in range(n):
            for j in range(3):
                copy(a, 3 + j, there[j], 1 - c, sibling).wait_recv()
        for cp in first + passed:
            cp.wait_send()

    return pl.pallas_call(
        body, out_shape=[jax.ShapeDtypeStruct(a.shape, a.dtype) for a in arrays],
        in_specs=[HBM_SPEC] * n, out_specs=[HBM_SPEC] * n,
        scratch_shapes=[pltpu.SemaphoreType.DMA((6 * n,)), pltpu.SemaphoreType.DMA((6 * n,))],
        input_output_aliases={i: i for i in range(n)}, name="gather_weights",
    )(*arrays)


def _row_block(rows, cols):
    for cand in (512, 256, 128, 64, 32, 16):
        if rows % cand == 0 and cand * cols * 4 <= 2560 * 1024:
            return cand
    return rows


def _swap_layers(grads):
    n = len(grads)

    def body(*refs):
        g_refs, out_refs, send_sems, recv_sems = refs[:n], refs[n:2 * n], refs[2 * n], refs[2 * n + 1]
        x, y, c, _ = _place()
        copies = [pltpu.make_async_remote_copy(
            src_ref=g_refs[a].at[:, 1 - c], dst_ref=out_refs[a], send_sem=send_sems.at[a], recv_sem=recv_sems.at[a],
            device_id=(x, y, 1 - c), device_id_type=MESH) for a in range(n)]
        for cp in copies:
            cp.start()
        for cp in copies:
            cp.wait()

    return pl.pallas_call(
        body, out_shape=[jax.ShapeDtypeStruct((g.shape[0],) + g.shape[2:], F32) for g in grads],
        in_specs=[HBM_SPEC] * n, out_specs=[HBM_SPEC] * n,
        scratch_shapes=[pltpu.SemaphoreType.DMA((n,)), pltpu.SemaphoreType.DMA((n,))], name="swap_layers",
    )(*grads)


def _add_layers(name, grad, other):
    shards, _, rows, cols = grad.shape
    tr = _row_block(rows, cols)

    def body(c_ref, g_ref, o_ref, out_ref):
        out_ref[...] = (g_ref[...] + o_ref[...]).astype(BF16)

    c = lax.axis_index("c").astype(jnp.int32).reshape(1)
    grid_spec = pltpu.PrefetchScalarGridSpec(
        num_scalar_prefetch=1, grid=(shards, rows // tr),
        in_specs=[pl.BlockSpec((None, None, tr, cols), lambda k, i, c_ref: (k, c_ref[0], i, 0)),
                  pl.BlockSpec((None, tr, cols), lambda k, i, c_ref: (k, i, 0))],
        out_specs=pl.BlockSpec((None, tr, cols), lambda k, i, c_ref: (k, i, 0)))
    return pl.pallas_call(
        body, out_shape=jax.ShapeDtypeStruct((shards, rows, cols), BF16), grid_spec=grid_spec,
        name=name, compiler_params=_params(("parallel", "parallel")),
    )(c, grad, other)


def _scatter_chips(parts):
    n = len(parts)

    def body(*refs):
        p_refs, out_refs, send_sems, recv_sems = refs[:n], refs[n:2 * n], refs[2 * n], refs[2 * n + 1]
        x, y, c, chips = _place()
        me = 2 * x + y
        there = [2 * chip[0] + chip[1] for chip in chips]

        def copy(a, j, slot):
            return pltpu.make_async_remote_copy(
                src_ref=p_refs[a].at[there[j]], dst_ref=out_refs[a].at[slot], send_sem=send_sems.at[3 * a + j],
                recv_sem=recv_sems.at[3 * a + j], device_id=(*chips[j], c), device_id_type=MESH)

        sends = [copy(a, j, me) for a in range(n) for j in range(3)]
        for cp in sends:
            cp.start()
        for a in range(n):
            for j in range(3):
                copy(a, j, there[j]).wait_recv()
        for cp in sends:
            cp.wait_send()

    return pl.pallas_call(
        body, out_shape=[jax.ShapeDtypeStruct(p.shape, p.dtype) for p in parts],
        in_specs=[HBM_SPEC] * n, out_specs=[HBM_SPEC] * n,
        scratch_shapes=[pltpu.SemaphoreType.DMA((3 * n,)), pltpu.SemaphoreType.DMA((3 * n,))],
        name="scatter_chips",
    )(*parts)


def _add_chips(name, part, others):
    shards, rows, cols = part.shape
    tr = _row_block(rows, cols)

    def body(pos_ref, own_ref, r0_ref, r1_ref, r2_ref, r3_ref, out_ref):
        me = pos_ref[0]
        own = own_ref[...].astype(F32)
        total = None
        for k, r_ref in enumerate((r0_ref, r1_ref, r2_ref, r3_ref)):
            term = jnp.where(me == k, own, r_ref[...].astype(F32))
            total = term if total is None else total + term
        out_ref[...] = total

    def other(k):
        return pl.BlockSpec((None, tr, cols),
                            lambda i, pos, k=k: (jnp.where(pos[0] == k, (k + 1) % shards, k), i, 0))

    pos = jnp.stack([_chip_index(), lax.axis_index("c").astype(jnp.int32)])
    grid_spec = pltpu.PrefetchScalarGridSpec(
        num_scalar_prefetch=1, grid=(rows // tr,),
        in_specs=[pl.BlockSpec((None, tr, cols), lambda i, pos: (pos[0], i, 0))] + [other(k) for k in range(shards)],
        out_specs=pl.BlockSpec((None, tr, cols), lambda i, pos: (pos[1], i, 0)))
    return pl.pallas_call(
        body, out_shape=jax.ShapeDtypeStruct((DEPTH, rows, cols), F32), grid_spec=grid_spec,
        name=name, compiler_params=_params(("parallel",)),
    )(pos, part, others, others, others, others)


def _join_layers(arrays):
    n = len(arrays)

    def body(*refs):
        outs, send_sems, recv_sems = refs[n:2 * n], refs[2 * n], refs[2 * n + 1]
        x, y, c, _ = _place()

        def copy(a, layer):
            piece = outs[a].at[layer]
            return pltpu.make_async_remote_copy(src_ref=piece, dst_ref=piece, send_sem=send_sems.at[a],
                                                recv_sem=recv_sems.at[a], device_id=(x, y, 1 - c),
                                                device_id_type=MESH)

        sends = [copy(a, c) for a in range(n)]
        for cp in sends:
            cp.start()
        for a in range(n):
            copy(a, 1 - c).wait_recv()
        for cp in sends:
            cp.wait_send()

    return pl.pallas_call(
        body, out_shape=[jax.ShapeDtypeStruct(a.shape, a.dtype) for a in arrays],
        in_specs=[HBM_SPEC] * n, out_specs=[HBM_SPEC] * n,
        scratch_shapes=[pltpu.SemaphoreType.DMA((n,)), pltpu.SemaphoreType.DMA((n,))],
        input_output_aliases={i: i for i in range(n)}, name="join_layers",
    )(*arrays)


def _reduce_scatter(grads):
    sums = [_add_layers(f"add_layers_{a}", g, o) for a, (g, o) in enumerate(zip(grads, _swap_layers(grads)))]
    others = _scatter_chips(sums)
    return _join_layers([_add_chips(f"add_chips_{a}", p, o) for a, (p, o) in enumerate(zip(sums, others))])


def _adamw(name, w, g, m, v):
    shape = w.shape
    cols = shape[-1]
    rows = w.size // cols
    tr = rows
    for cand in (512, 256, 128, 64, 32, 16, 8):
        if rows % cand == 0 and cand * cols * 4 <= 2 * 1024 * 1024:
            tr = cand
            break

    def body(w_ref, g_ref, m_ref, v_ref, go_ref, d_ref, nm_ref, nv_ref):
        gv = g_ref[...]
        go_ref[...] = gv
        nm = ADAM_B1 * m_ref[...] + (1.0 - ADAM_B1) * gv
        nv = ADAM_B2 * v_ref[...] + (1.0 - ADAM_B2) * (gv * gv)
        m_hat = nm / (1.0 - ADAM_B1 ** ADAM_STEP)
        v_hat = nv / (1.0 - ADAM_B2 ** ADAM_STEP)
        d_ref[...] = -ADAM_LR * (m_hat / (jnp.sqrt(v_hat) + ADAM_EPS) + ADAM_WD * w_ref[...])
        nm_ref[...] = nm
        nv_ref[...] = nv

    spec = pl.BlockSpec((tr, cols), lambda i: (i, 0))
    two = lambda a: a.reshape(rows, cols)
    outs = pl.pallas_call(
        body, out_shape=[jax.ShapeDtypeStruct((rows, cols), F32)] * 4, grid=(rows // tr,),
        in_specs=[spec] * 4, out_specs=[spec] * 4, name=name, compiler_params=_params(("parallel",)),
    )(two(w), two(g), two(m), two(v))
    return [o.reshape(shape) for o in outs]


BIG = ("ffn_wg", "ffn_wu", "ffn_wd", "w_in", "conv_wo", "w_out")
MISC_ROWS = 96


def _own_slot(shard):
    return lax.dynamic_update_slice(jnp.zeros((SHARDS,) + shard.shape, shard.dtype), shard[None],
                                    (_chip_index(),) + (0,) * shard.ndim)


def _as_matrices(a):
    return a.reshape(a.shape[0], a.shape[1], -1, a.shape[-1])


def kernel(x, c, ada_w, ada_b, norm_g, ffn_wg, ffn_wu, ffn_wd, w_in, attn_wo, conv_w, conv_b, conv_ln_g, conv_ln_b, conv_wo, w_out, final_g, loss_target, m_ada_w, m_ada_b, m_norm_g, m_ffn_wg, m_ffn_wu, m_ffn_wd, m_w_in, m_attn_wo, m_conv_w, m_conv_b, m_conv_ln_g, m_conv_ln_b, m_conv_wo, m_w_out, m_final_g, v_ada_w, v_ada_b, v_norm_g, v_ffn_wg, v_ffn_wu, v_ffn_wd, v_w_in, v_attn_wo, v_conv_w, v_conv_b, v_conv_ln_g, v_conv_ln_b, v_conv_wo, v_w_out, v_final_g):
    weights = dict(ada_w=ada_w, ada_b=ada_b, norm_g=norm_g, ffn_wg=ffn_wg, ffn_wu=ffn_wu, ffn_wd=ffn_wd, w_in=w_in,
                   attn_wo=attn_wo, conv_w=conv_w, conv_b=conv_b, conv_ln_g=conv_ln_g, conv_ln_b=conv_ln_b,
                   conv_wo=conv_wo, w_out=w_out, final_g=final_g)
    moments_m = dict(ada_w=m_ada_w, ada_b=m_ada_b, norm_g=m_norm_g, ffn_wg=m_ffn_wg, ffn_wu=m_ffn_wu,
                     ffn_wd=m_ffn_wd, w_in=m_w_in, attn_wo=m_attn_wo, conv_w=m_conv_w, conv_b=m_conv_b,
                     conv_ln_g=m_conv_ln_g, conv_ln_b=m_conv_ln_b, conv_wo=m_conv_wo, w_out=m_w_out,
                     final_g=m_final_g)
    moments_v = dict(ada_w=v_ada_w, ada_b=v_ada_b, norm_g=v_norm_g, ffn_wg=v_ffn_wg, ffn_wu=v_ffn_wu,
                     ffn_wd=v_ffn_wd, w_in=v_w_in, attn_wo=v_attn_wo, conv_w=v_conv_w, conv_b=v_conv_b,
                     conv_ln_g=v_conv_ln_g, conv_ln_b=v_conv_ln_b, conv_wo=v_conv_wo, w_out=v_w_out,
                     final_g=v_final_g)
    layers, shards = range(DEPTH), range(SHARDS)

    bits = {n: _split_bits(weights[n]) for n in EXACT}
    misc_w = jnp.stack([_pack([attn_wo[l].astype(BF16), bits["norm_g"][0][l], bits["norm_g"][1][l],
                               bits["conv_w"][0][l], bits["conv_w"][1][l]], MISC_ROWS) for l in layers])
    sent = [_own_slot(weights[n].astype(BF16)) for n in BIG] + [_own_slot(misc_w)]
    got = dict(zip(BIG + ("misc",), _gather_weights(sent)))
    w = {n: got[n] for n in ("ffn_wg", "ffn_wu", "ffn_wd")}
    w["w_in"] = got["w_in"].transpose(1, 2, 0, 3).reshape(DEPTH, D, -1)
    for n in ("conv_wo", "w_out"):
        w[n] = got[n].transpose(1, 0, 2, 3).reshape(DEPTH, D, D)
    misc_shapes = [(GW, GW), (3, GW), (3, GW), (CONV_K, GW), (CONV_K, GW)]
    pieces = [[_unpack(got["misc"][k, l], misc_shapes) for k in shards] for l in layers]
    whole = lambda i: jnp.stack([jnp.concatenate([pieces[l][k][i] for k in shards], axis=1) for l in layers])
    w["attn_wo"] = whole(0)
    vectors = dict(ada_b=ada_b, conv_b=conv_b, conv_ln_g=conv_ln_g, conv_ln_b=conv_ln_b, final_g=final_g,
                   norm_g=_join_bits(whole(1), whole(2)), conv_w=_join_bits(whole(3), whole(4)))

    me = 2 * _chip_index() + lax.axis_index("c").astype(jnp.int32)
    pad_rows = lambda a, rows: jnp.concatenate([a, jnp.zeros((rows - a.shape[0], a.shape[1]), a.dtype)])
    c_all = _gather_rows("gather_c", pad_rows(c, SUBLANES)).reshape(N_DEV, SUBLANES, D)[:, 0]
    mod_cols = _mod_fwd("mod_fwd", c_all, ada_w.astype(BF16))
    by_dev = _gather_rows("gather_mod", mod_cols).reshape(N_DEV, N_DEV, DEPTH, -1)
    mine = lax.dynamic_index_in_dim(by_dev[0::2], me, axis=1, keepdims=False)
    mod = mine.transpose(1, 0, 2).reshape(DEPTH, -1) + ada_b

    loss_cols, dx, grads = _local_step(x[0], mod, loss_target[0], w, vectors)
    loss = lax.psum(jnp.sum(loss_cols), ("x", "y", "c"))

    dmod_rows = DEPTH * N_MOD * D // LANES
    dmod_all = _gather_rows("gather_dmod", pad_rows(grads["dmod"].reshape(dmod_rows, LANES), 3 * SUBLANES))
    dmod_all = dmod_all.reshape(N_DEV, 3 * SUBLANES, LANES)[:, :dmod_rows].reshape(N_DEV, DEPTH, -1)
    grad_ada_b = _sum_devices("ada_b_grad", dmod_all.reshape(N_DEV, -1)).reshape(DEPTH, -1)
    cols = ada_w.shape[-1]
    dmod_cols = lax.dynamic_slice_in_dim(dmod_all, _chip_index() * cols, cols, axis=2).transpose(1, 0, 2)
    grad_ada_w = _mod_bwd("mod_bwd", c_all.T, dmod_cols)

    cols_of = lambda a, k: a[..., k * GW:(k + 1) * GW]
    misc_g = jnp.stack([jnp.stack([_pack(
        [cols_of(grads["attn_wo"][l], k), cols_of(jnp.stack(grads["norm_g"][l]), k), cols_of(grads["conv_w"][l], k),
         grads["conv_b"][l], grads["conv_ln_g"][l], grads["conv_ln_b"][l],
         grads["final_g"] if l == 0 else jnp.zeros_like(grads["final_g"])], MISC_ROWS)
        for l in layers]) for k in shards])
    by_chip = dict(
        ffn_wg=grads["ffn_wg"], ffn_wu=grads["ffn_wu"], ffn_wd=grads["ffn_wd"],
        w_in=jnp.stack(grads["w_in"]).reshape(DEPTH, D, SHARDS, -1).transpose(2, 0, 1, 3),
        conv_wo=jnp.stack(grads["conv_wo"]).reshape(DEPTH, SHARDS, -1, D).transpose(1, 0, 2, 3),
        w_out=jnp.stack(grads["w_out"]).reshape(DEPTH, SHARDS, -1, D).transpose(1, 0, 2, 3))
    reduced = _reduce_scatter([_as_matrices(by_chip[n]) for n in BIG] + [misc_g])
    summed = {n: r.reshape(weights[n].shape) for n, r in zip(BIG, reduced)}
    small_shapes = [(GW, GW), (3, GW), (CONV_K, GW), (D,), (D,), (D,), (D,)]
    small = [_unpack(reduced[-1][l], small_shapes) for l in layers]
    for i, n in enumerate(("attn_wo", "norm_g", "conv_w", "conv_b", "conv_ln_g", "conv_ln_b")):
        summed[n] = jnp.stack([small[l][i] for l in layers])
    summed["final_g"] = small[0][6]
    summed["ada_w"], summed["ada_b"] = grad_ada_w, grad_ada_b

    grad_out, deltas, new_m, new_v = {}, {}, {}, {}
    for n in WEIGHTS:
        grad_out[n], deltas[n], new_m[n], new_v[n] = _adamw(f"adamw_{n}", weights[n], summed[n], moments_m[n],
                                                           moments_v[n])

    return (loss, dx[None], *[grad_out[n] for n in WEIGHTS], *[deltas[n] for n in WEIGHTS],
            *[new_m[n] for n in WEIGHTS], *[new_v[n] for n in WEIGHTS])
```
